```python
import math
import jax, jax.numpy as jnp
from jax import lax
import numpy as np

D_MODEL = 2048
BATCH = 8
SEQ = 4096
DEPTH = 4

CHUNK = 64
MIX_W = D_MODEL // 2
N_BRANCH = 3
LRU_W = MIX_W
LRU_BLOCKS = 16
LRU_BW = LRU_W // LRU_BLOCKS
CONV_W = 4
LRU_C = 8.0
ATT_HEADS = 8
ATT_HD = MIX_W // ATT_HEADS
ATT_LEFT_CHUNKS = 8
ATT_BAND = (ATT_LEFT_CHUNKS + 1) * CHUNK
MAX_REL = 128
N_REL = 2 * MAX_REL + 1
SSM_W = MIX_W
SSM_GROUP = 16
SSM_G = SSM_W // SSM_GROUP
SSM_P = 64
FFN_HIDDEN = -(-8 * D_MODEL // (3 * 256)) * 256
IN_SPLITS = [LRU_W, LRU_W, MIX_W, MIX_W, MIX_W, SSM_W]
IN_W = sum(IN_SPLITS) + N_BRANCH * D_MODEL
NORM_EPS = 1e-6
MASK_VALUE = -1e30

kernel_name = "hybrid_gated_lru_chunkattn_s5_encoder"


def rmsnorm(x, g):
    xf = x.astype(jnp.float32)
    xf = xf * lax.rsqrt(jnp.mean(xf * xf, axis=-1, keepdims=True) + NORM_EPS)
    return xf.astype(x.dtype) * g


def causal_dwconv(x, w, b):
    s = x.shape[1]
    xp = jnp.pad(x, ((0, 0), (CONV_W - 1, 0), (0, 0)))
    y = b
    for k in range(CONV_W):
        y = y + xp[:, k:k + s] * w[k]
    return y


def _lin_op(e1, e2):
    a1, b1 = e1
    a2, b2 = e2
    return a1 * a2, a2 * b1 + b2


def _complex_lin_op(e1, e2):
    ar1, ai1, br1, bi1 = e1
    ar2, ai2, br2, bi2 = e2
    return (ar2 * ar1 - ai2 * ai1,
            ar2 * ai1 + ai2 * ar1,
            ar2 * br1 - ai2 * bi1 + br2,
            ar2 * bi1 + ai2 * br1 + bi2)


def rg_lru_branch(xin, gate_in, conv_w, conv_b, wa, ba, wx, bx, lam):
    b_, s_, _ = xin.shape
    xc = causal_dwconv(xin, conv_w, conv_b).astype(jnp.float32)
    xb = xc.reshape(b_, s_, LRU_BLOCKS, LRU_BW)
    r = jax.nn.sigmoid(jnp.einsum("bsnk,nkj->bsnj", xb, wa.astype(jnp.float32)).reshape(b_, s_, LRU_W) + ba)
    i = jax.nn.sigmoid(jnp.einsum("bsnk,nkj->bsnj", xb, wx.astype(jnp.float32)).reshape(b_, s_, LRU_W) + bx)
    log_a = -LRU_C * r * jax.nn.softplus(-lam.astype(jnp.float32))
    a = jnp.exp(log_a)
    inp = jnp.sqrt(-jnp.expm1(2.0 * log_a)) * (i * xc)
    h = lax.associative_scan(_lin_op, (a, inp), axis=1)[1]
    return (h * jax.nn.gelu(gate_in.astype(jnp.float32))).astype(xin.dtype)


def chunk_attention_branch(q, k, v, rel_bias):
    b_, s_, _ = q.shape
    n_c = s_ // CHUNK
    qc = q.reshape(b_, n_c, CHUNK, ATT_HEADS, ATT_HD)
    pad = ((0, 0), (ATT_LEFT_CHUNKS, 0), (0, 0), (0, 0), (0, 0))
    kp = jnp.pad(k.reshape(b_, n_c, CHUNK, ATT_HEADS, ATT_HD), pad)
    vp = jnp.pad(v.reshape(b_, n_c, CHUNK, ATT_HEADS, ATT_HD), pad)
    band_idx = jnp.arange(n_c)[:, None] + jnp.arange(ATT_LEFT_CHUNKS + 1)[None, :]
    kb = kp[:, band_idx].reshape(b_, n_c, ATT_BAND, ATT_HEADS, ATT_HD)
    vb = vp[:, band_idx].reshape(b_, n_c, ATT_BAND, ATT_HEADS, ATT_HD)
    scores = jnp.einsum("bcqhd,bckhd->bchqk", qc, kb).astype(jnp.float32) * (ATT_HD ** -0.5)
    q_pos = ATT_LEFT_CHUNKS * CHUNK + jnp.arange(CHUNK)
    k_pos = jnp.arange(ATT_BAND)
    rel = jnp.clip(q_pos[:, None] - k_pos[None, :], -MAX_REL, MAX_REL) + MAX_REL
    bias = rel_bias.astype(jnp.float32)[:, rel]
    key_abs = (jnp.arange(n_c)[:, None] - ATT_LEFT_CHUNKS) * CHUNK + k_pos[None, :]
    valid = key_abs >= 0
    scores = jnp.where(valid[None, :, None, None, :], scores + bias[None, None], MASK_VALUE)
    p = jax.nn.softmax(scores, axis=-1).astype(v.dtype)
    o = jnp.einsum("bchqk,bckhd->bcqhd", p, vb)
    return o.reshape(b_, s_, MIX_W)


def s5_branch(u, a_re, a_im, b_re, b_im, c_re, c_im, d, log_step):
    b_, s_, _ = u.shape
    uf = u.astype(jnp.float32)
    ug = uf.reshape(b_, s_, SSM_G, SSM_GROUP)
    a_re = a_re.astype(jnp.float32)
    a_im = a_im.astype(jnp.float32)
    step = jnp.exp(log_step.astype(jnp.float32))[:, None]
    mag = jnp.exp(a_re * step)
    ang = a_im * step
    lb_re = mag * jnp.cos(ang)
    lb_im = mag * jnp.sin(ang)
    den = a_re * a_re + a_im * a_im
    nr = lb_re - 1.0
    coef_re = (nr * a_re + lb_im * a_im) / den
    coef_im = (lb_im * a_re - nr * a_im) / den
    b_re = b_re.astype(jnp.float32)
    b_im = b_im.astype(jnp.float32)
    bb_re = coef_re[..., None] * b_re - coef_im[..., None] * b_im
    bb_im = coef_re[..., None] * b_im + coef_im[..., None] * b_re
    bu_re = jnp.einsum("bsgh,gph->bsgp", ug, bb_re)
    bu_im = jnp.einsum("bsgh,gph->bsgp", ug, bb_im)
    ar = jnp.broadcast_to(lb_re, bu_re.shape)
    ai = jnp.broadcast_to(lb_im, bu_re.shape)
    _, _, xs_re, xs_im = lax.associative_scan(_complex_lin_op, (ar, ai, bu_re, bu_im), axis=1)
    y = (jnp.einsum("bsgp,ghp->bsgh", xs_re, c_re.astype(jnp.float32))
         - jnp.einsum("bsgp,ghp->bsgh", xs_im, c_im.astype(jnp.float32)))
    y = y.reshape(b_, s_, SSM_W) + d.astype(jnp.float32) * uf
    return y.astype(u.dtype)


def _fwd_setup_inputs(seed: int = 0) -> dict:
    key = jax.random.key(seed)
    ks = jax.random.split(key, 32)
    f32 = jnp.float32

    def nrm(k, shape, scale):
        return jax.random.normal(k, shape, f32) * scale

    u_lam = jax.random.uniform(ks[8], (DEPTH, LRU_W), f32, 0.9, 0.999)
    p_lam = u_lam ** (1.0 / LRU_C)
    lru_lambda = jnp.log(p_lam) - jnp.log1p(-p_lam)
    n_idx = jnp.arange(SSM_P, dtype=f32)
    return {
        "x": nrm(ks[0], (BATCH, SEQ, D_MODEL), 1.0),
        "norm_mix_g": 1.0 + nrm(ks[1], (DEPTH, D_MODEL), 0.01),
        "w_in": nrm(ks[2], (DEPTH, D_MODEL, IN_W), D_MODEL ** -0.5),
        "gate_bias": nrm(ks[3], (DEPTH, N_BRANCH, D_MODEL), 0.01),
        "lru_conv_w": nrm(ks[4], (DEPTH, CONV_W, LRU_W), CONV_W ** -0.5),
        "lru_conv_b": nrm(ks[5], (DEPTH, LRU_W), 0.01),
        "lru_wa": nrm(ks[6], (DEPTH, LRU_BLOCKS, LRU_BW, LRU_BW), LRU_BW ** -0.5),
        "lru_ba": nrm(ks[7], (DEPTH, LRU_W), 0.01),
        "lru_wx": nrm(ks[9], (DEPTH, LRU_BLOCKS, LRU_BW, LRU_BW), LRU_BW ** -0.5),
        "lru_bx": nrm(ks[10], (DEPTH, LRU_W), 0.01),
        "lru_lambda": lru_lambda,
        "attn_rel_bias": nrm(ks[11], (DEPTH, ATT_HEADS, N_REL), 0.1),
        "ssm_a_re": -0.5 + nrm(ks[12], (DEPTH, SSM_G, SSM_P), 0.01),
        "ssm_a_im": math.pi * n_idx + nrm(ks[13], (DEPTH, SSM_G, SSM_P), 0.01),
        "ssm_b_re": nrm(ks[14], (DEPTH, SSM_G, SSM_P, SSM_GROUP), (2 * SSM_GROUP) ** -0.5),
        "ssm_b_im": nrm(ks[15], (DEPTH, SSM_G, SSM_P, SSM_GROUP), (2 * SSM_GROUP) ** -0.5),
        "ssm_c_re": nrm(ks[16], (DEPTH, SSM_G, SSM_GROUP, SSM_P), (2 * SSM_P) ** -0.5),
        "ssm_c_im": nrm(ks[17], (DEPTH, SSM_G, SSM_GROUP, SSM_P), (2 * SSM_P) ** -0.5),
        "ssm_d": nrm(ks[18], (DEPTH, SSM_W), 1.0),
        "ssm_log_step": jax.random.uniform(ks[19], (DEPTH, SSM_G), f32, math.log(1e-3), math.log(1e-1)),
        "ssm_w_glu": nrm(ks[20], (DEPTH, SSM_W, D_MODEL), SSM_W ** -0.5),
        "w_branch": nrm(ks[21], (DEPTH, N_BRANCH, MIX_W, D_MODEL), MIX_W ** -0.5),
        "w_out": nrm(ks[22], (DEPTH, D_MODEL, D_MODEL), D_MODEL ** -0.5),
        "norm_ffn_g": 1.0 + nrm(ks[23], (DEPTH, D_MODEL), 0.01),
        "w_ffn_gate": nrm(ks[24], (DEPTH, D_MODEL, FFN_HIDDEN), D_MODEL ** -0.5),
        "w_ffn_up": nrm(ks[25], (DEPTH, D_MODEL, FFN_HIDDEN), D_MODEL ** -0.5),
        "w_ffn_down": nrm(ks[26], (DEPTH, FFN_HIDDEN, D_MODEL), FFN_HIDDEN ** -0.5),
        "norm_final_g": 1.0 + nrm(ks[27], (D_MODEL,), 0.01),
    }


def _fwd_reference(x, norm_mix_g, w_in, gate_bias, lru_conv_w, lru_conv_b, lru_wa, lru_ba, lru_wx, lru_bx,
              lru_lambda, attn_rel_bias, ssm_a_re, ssm_a_im, ssm_b_re, ssm_b_im, ssm_c_re, ssm_c_im,
              ssm_d, ssm_log_step, ssm_w_glu, w_branch, w_out, norm_ffn_g, w_ffn_gate, w_ffn_up,
              w_ffn_down, norm_final_g):
    b_, s_, _ = x.shape
    split_pts = [int(p) for p in np.cumsum(IN_SPLITS)]
    for l in range(DEPTH):
        h = rmsnorm(x, norm_mix_g[l])
        proj = h @ w_in[l]
        lru_x, lru_gate, q, k, v, ssm_u, gates = jnp.split(proj, split_pts, axis=-1)
        y_a = rg_lru_branch(lru_x, lru_gate, lru_conv_w[l], lru_conv_b[l], lru_wa[l], lru_ba[l],
                            lru_wx[l], lru_bx[l], lru_lambda[l])
        y_b = chunk_attention_branch(q, k, v, attn_rel_bias[l])
        y_c = jax.nn.gelu(s5_branch(ssm_u, ssm_a_re[l], ssm_a_im[l], ssm_b_re[l], ssm_b_im[l],
                                    ssm_c_re[l], ssm_c_im[l], ssm_d[l], ssm_log_step[l]))
        br_a = y_a @ w_branch[l, 0]
        br_b = y_b @ w_branch[l, 1]
        br_c = (y_c @ w_branch[l, 2]) * jax.nn.sigmoid(y_c @ ssm_w_glu[l])
        g = jax.nn.sigmoid(gates.reshape(b_, s_, N_BRANCH, D_MODEL) + gate_bias[l])
        merged = g[:, :, 0] * br_a + g[:, :, 1] * br_b + g[:, :, 2] * br_c
        x = x + merged @ w_out[l]
        h = rmsnorm(x, norm_ffn_g[l])
        x = x + (jax.nn.silu(h @ w_ffn_gate[l]) * (h @ w_ffn_up[l])) @ w_ffn_down[l]
    return rmsnorm(x, norm_final_g)


import jax as _jax
import jax.numpy as _jnp

TWIN_FORMAT = 'train_step'
FWD_PARAMS = ['x', 'norm_mix_g', 'w_in', 'gate_bias', 'lru_conv_w', 'lru_conv_b', 'lru_wa', 'lru_ba', 'lru_wx', 'lru_bx', 'lru_lambda', 'attn_rel_bias', 'ssm_a_re', 'ssm_a_im', 'ssm_b_re', 'ssm_b_im', 'ssm_c_re', 'ssm_c_im', 'ssm_d', 'ssm_log_step', 'ssm_w_glu', 'w_branch', 'w_out', 'norm_ffn_g', 'w_ffn_gate', 'w_ffn_up', 'w_ffn_down', 'norm_final_g']
TWIN_WEIGHTS = ['norm_mix_g', 'w_in', 'gate_bias', 'lru_conv_w', 'lru_conv_b', 'lru_wa', 'lru_ba', 'lru_wx', 'lru_bx', 'lru_lambda', 'attn_rel_bias', 'ssm_a_re', 'ssm_a_im', 'ssm_b_re', 'ssm_b_im', 'ssm_c_re', 'ssm_c_im', 'ssm_d', 'ssm_log_step', 'ssm_w_glu', 'w_branch', 'w_out', 'norm_ffn_g', 'w_ffn_gate', 'w_ffn_up', 'w_ffn_down', 'norm_final_g']
TWIN_DIFF_INPUT = 'x'
TWIN_INPUTS = ['x', 'norm_mix_g', 'w_in', 'gate_bias', 'lru_conv_w', 'lru_conv_b', 'lru_wa', 'lru_ba', 'lru_wx', 'lru_bx', 'lru_lambda', 'attn_rel_bias', 'ssm_a_re', 'ssm_a_im', 'ssm_b_re', 'ssm_b_im', 'ssm_c_re', 'ssm_c_im', 'ssm_d', 'ssm_log_step', 'ssm_w_glu', 'w_branch', 'w_out', 'norm_ffn_g', 'w_ffn_gate', 'w_ffn_up', 'w_ffn_down', 'norm_final_g', 'loss_target', 'm_norm_mix_g', 'm_w_in', 'm_gate_bias', 'm_lru_conv_w', 'm_lru_conv_b', 'm_lru_wa', 'm_lru_ba', 'm_lru_wx', 'm_lru_bx', 'm_lru_lambda', 'm_attn_rel_bias', 'm_ssm_a_re', 'm_ssm_a_im', 'm_ssm_b_re', 'm_ssm_b_im', 'm_ssm_c_re', 'm_ssm_c_im', 'm_ssm_d', 'm_ssm_log_step', 'm_ssm_w_glu', 'm_w_branch', 'm_w_out', 'm_norm_ffn_g', 'm_w_ffn_gate', 'm_w_ffn_up', 'm_w_ffn_down', 'm_norm_final_g', 'v_norm_mix_g', 'v_w_in', 'v_gate_bias', 'v_lru_conv_w', 'v_lru_conv_b', 'v_lru_wa', 'v_lru_ba', 'v_lru_wx', 'v_lru_bx', 'v_lru_lambda', 'v_attn_rel_bias', 'v_ssm_a_re', 'v_ssm_a_im', 'v_ssm_b_re', 'v_ssm_b_im', 'v_ssm_c_re', 'v_ssm_c_im', 'v_ssm_d', 'v_ssm_log_step', 'v_ssm_w_glu', 'v_w_branch', 'v_w_out', 'v_norm_ffn_g', 'v_w_ffn_gate', 'v_w_ffn_up', 'v_w_ffn_down', 'v_norm_final_g']
TWIN_OUTPUTS = ['loss', 'grad_x', 'grad_norm_mix_g', 'grad_w_in', 'grad_gate_bias', 'grad_lru_conv_w', 'grad_lru_conv_b', 'grad_lru_wa', 'grad_lru_ba', 'grad_lru_wx', 'grad_lru_bx', 'grad_lru_lambda', 'grad_attn_rel_bias', 'grad_ssm_a_re', 'grad_ssm_a_im', 'grad_ssm_b_re', 'grad_ssm_b_im', 'grad_ssm_c_re', 'grad_ssm_c_im', 'grad_ssm_d', 'grad_ssm_log_step', 'grad_ssm_w_glu', 'grad_w_branch', 'grad_w_out', 'grad_norm_ffn_g', 'grad_w_ffn_gate', 'grad_w_ffn_up', 'grad_w_ffn_down', 'grad_norm_final_g', 'delta_norm_mix_g', 'delta_w_in', 'delta_gate_bias', 'delta_lru_conv_w', 'delta_lru_conv_b', 'delta_lru_wa', 'delta_lru_ba', 'delta_lru_wx', 'delta_lru_bx', 'delta_lru_lambda', 'delta_attn_rel_bias', 'delta_ssm_a_re', 'delta_ssm_a_im', 'delta_ssm_b_re', 'delta_ssm_b_im', 'delta_ssm_c_re', 'delta_ssm_c_im', 'delta_ssm_d', 'delta_ssm_log_step', 'delta_ssm_w_glu', 'delta_w_branch', 'delta_w_out', 'delta_norm_ffn_g', 'delta_w_ffn_gate', 'delta_w_ffn_up', 'delta_w_ffn_down', 'delta_norm_final_g', 'new_m_norm_mix_g', 'new_m_w_in', 'new_m_gate_bias', 'new_m_lru_conv_w', 'new_m_lru_conv_b', 'new_m_lru_wa', 'new_m_lru_ba', 'new_m_lru_wx', 'new_m_lru_bx', 'new_m_lru_lambda', 'new_m_attn_rel_bias', 'new_m_ssm_a_re', 'new_m_ssm_a_im', 'new_m_ssm_b_re', 'new_m_ssm_b_im', 'new_m_ssm_c_re', 'new_m_ssm_c_im', 'new_m_ssm_d', 'new_m_ssm_log_step', 'new_m_ssm_w_glu', 'new_m_w_branch', 'new_m_w_out', 'new_m_norm_ffn_g', 'new_m_w_ffn_gate', 'new_m_w_ffn_up', 'new_m_w_ffn_down', 'new_m_norm_final_g', 'new_v_norm_mix_g', 'new_v_w_in', 'new_v_gate_bias', 'new_v_lru_conv_w', 'new_v_lru_conv_b', 'new_v_lru_wa', 'new_v_lru_ba', 'new_v_lru_wx', 'new_v_lru_bx', 'new_v_lru_lambda', 'new_v_attn_rel_bias', 'new_v_ssm_a_re', 'new_v_ssm_a_im', 'new_v_ssm_b_re', 'new_v_ssm_b_im', 'new_v_ssm_c_re', 'new_v_ssm_c_im', 'new_v_ssm_d', 'new_v_ssm_log_step', 'new_v_ssm_w_glu', 'new_v_w_branch', 'new_v_w_out', 'new_v_norm_ffn_g', 'new_v_w_ffn_gate', 'new_v_w_ffn_up', 'new_v_w_ffn_down', 'new_v_norm_final_g']
TWIN_LEAF_KINDS = {'loss': 'loss', 'grad_x': 'grad_x', 'grad_norm_mix_g': 'grad_w', 'grad_w_in': 'grad_w', 'grad_gate_bias': 'grad_w', 'grad_lru_conv_w': 'grad_w', 'grad_lru_conv_b': 'grad_w', 'grad_lru_wa': 'grad_w', 'grad_lru_ba': 'grad_w', 'grad_lru_wx': 'grad_w', 'grad_lru_bx': 'grad_w', 'grad_lru_lambda': 'grad_w', 'grad_attn_rel_bias': 'grad_w', 'grad_ssm_a_re': 'grad_w', 'grad_ssm_a_im': 'grad_w', 'grad_ssm_b_re': 'grad_w', 'grad_ssm_b_im': 'grad_w', 'grad_ssm_c_re': 'grad_w', 'grad_ssm_c_im': 'grad_w', 'grad_ssm_d': 'grad_w', 'grad_ssm_log_step': 'grad_w', 'grad_ssm_w_glu': 'grad_w', 'grad_w_branch': 'grad_w', 'grad_w_out': 'grad_w', 'grad_norm_ffn_g': 'grad_w', 'grad_w_ffn_gate': 'grad_w', 'grad_w_ffn_up': 'grad_w', 'grad_w_ffn_down': 'grad_w', 'grad_norm_final_g': 'grad_w', 'delta_norm_mix_g': 'delta_w', 'delta_w_in': 'delta_w', 'delta_gate_bias': 'delta_w', 'delta_lru_conv_w': 'delta_w', 'delta_lru_conv_b': 'delta_w', 'delta_lru_wa': 'delta_w', 'delta_lru_ba': 'delta_w', 'delta_lru_wx': 'delta_w', 'delta_lru_bx': 'delta_w', 'delta_lru_lambda': 'delta_w', 'delta_attn_rel_bias': 'delta_w', 'delta_ssm_a_re': 'delta_w', 'delta_ssm_a_im': 'delta_w', 'delta_ssm_b_re': 'delta_w', 'delta_ssm_b_im': 'delta_w', 'delta_ssm_c_re': 'delta_w', 'delta_ssm_c_im': 'delta_w', 'delta_ssm_d': 'delta_w', 'delta_ssm_log_step': 'delta_w', 'delta_ssm_w_glu': 'delta_w', 'delta_w_branch': 'delta_w', 'delta_w_out': 'delta_w', 'delta_norm_ffn_g': 'delta_w', 'delta_w_ffn_gate': 'delta_w', 'delta_w_ffn_up': 'delta_w', 'delta_w_ffn_down': 'delta_w', 'delta_norm_final_g': 'delta_w', 'new_m_norm_mix_g': 'new_m', 'new_m_w_in': 'new_m', 'new_m_gate_bias': 'new_m', 'new_m_lru_conv_w': 'new_m', 'new_m_lru_conv_b': 'new_m', 'new_m_lru_wa': 'new_m', 'new_m_lru_ba': 'new_m', 'new_m_lru_wx': 'new_m', 'new_m_lru_bx': 'new_m', 'new_m_lru_lambda': 'new_m', 'new_m_attn_rel_bias': 'new_m', 'new_m_ssm_a_re': 'new_m', 'new_m_ssm_a_im': 'new_m', 'new_m_ssm_b_re': 'new_m', 'new_m_ssm_b_im': 'new_m', 'new_m_ssm_c_re': 'new_m', 'new_m_ssm_c_im': 'new_m', 'new_m_ssm_d': 'new_m', 'new_m_ssm_log_step': 'new_m', 'new_m_ssm_w_glu': 'new_m', 'new_m_w_branch': 'new_m', 'new_m_w_out': 'new_m', 'new_m_norm_ffn_g': 'new_m', 'new_m_w_ffn_gate': 'new_m', 'new_m_w_ffn_up': 'new_m', 'new_m_w_ffn_down': 'new_m', 'new_m_norm_final_g': 'new_m', 'new_v_norm_mix_g': 'new_v', 'new_v_w_in': 'new_v', 'new_v_gate_bias': 'new_v', 'new_v_lru_conv_w': 'new_v', 'new_v_lru_conv_b': 'new_v', 'new_v_lru_wa': 'new_v', 'new_v_lru_ba': 'new_v', 'new_v_lru_wx': 'new_v', 'new_v_lru_bx': 'new_v', 'new_v_lru_lambda': 'new_v', 'new_v_attn_rel_bias': 'new_v', 'new_v_ssm_a_re': 'new_v', 'new_v_ssm_a_im': 'new_v', 'new_v_ssm_b_re': 'new_v', 'new_v_ssm_b_im': 'new_v', 'new_v_ssm_c_re': 'new_v', 'new_v_ssm_c_im': 'new_v', 'new_v_ssm_d': 'new_v', 'new_v_ssm_log_step': 'new_v', 'new_v_ssm_w_glu': 'new_v', 'new_v_w_branch': 'new_v', 'new_v_w_out': 'new_v', 'new_v_norm_ffn_g': 'new_v', 'new_v_w_ffn_gate': 'new_v', 'new_v_w_ffn_up': 'new_v', 'new_v_w_ffn_down': 'new_v', 'new_v_norm_final_g': 'new_v'}


def _forward(args):
    return _fwd_reference(*[args[k] for k in FWD_PARAMS])


def _output_shape():
    def fwd():
        inp = _fwd_setup_inputs(0)
        return _fwd_reference(*[inp[k] for k in FWD_PARAMS])
    out = _jax.eval_shape(fwd)
    return out.shape, out.dtype

N_MICROBATCH = 1
ADAM_LR = 0.001
ADAM_B1 = 0.9
ADAM_B2 = 0.999
ADAM_EPS = 1e-08
ADAM_WD = 0.01
ADAM_STEP = 10
PER_EXAMPLE_BATCH_AXIS = {'x': 0, 'loss_target': 0}
SHARED_INPUTS = []
_WEIGHT_DTYPES = {'norm_mix_g': _jnp.float32, 'w_in': _jnp.float32, 'gate_bias': _jnp.float32, 'lru_conv_w': _jnp.float32, 'lru_conv_b': _jnp.float32, 'lru_wa': _jnp.float32, 'lru_ba': _jnp.float32, 'lru_wx': _jnp.float32, 'lru_bx': _jnp.float32, 'lru_lambda': _jnp.float32, 'attn_rel_bias': _jnp.float32, 'ssm_a_re': _jnp.float32, 'ssm_a_im': _jnp.float32, 'ssm_b_re': _jnp.float32, 'ssm_b_im': _jnp.float32, 'ssm_c_re': _jnp.float32, 'ssm_c_im': _jnp.float32, 'ssm_d': _jnp.float32, 'ssm_log_step': _jnp.float32, 'ssm_w_glu': _jnp.float32, 'w_branch': _jnp.float32, 'w_out': _jnp.float32, 'norm_ffn_g': _jnp.float32, 'w_ffn_gate': _jnp.float32, 'w_ffn_up': _jnp.float32, 'w_ffn_down': _jnp.float32, 'norm_final_g': _jnp.float32}
MOMENT_SCALE = {'norm_mix_g': 4.568226e-02, 'w_in': 1.886905e-02, 'gate_bias': 8.787104e-03, 'lru_conv_w': 4.561227e-02, 'lru_conv_b': 3.559302e-01, 'lru_wa': 1.335342e-02, 'lru_ba': 1.079097e-02, 'lru_wx': 2.415140e-02, 'lru_bx': 1.607137e-02, 'lru_lambda': 2.252552e-02, 'attn_rel_bias': 6.891679e-03, 'ssm_a_re': 1.310528e-03, 'ssm_a_im': 1.355793e-03, 'ssm_b_re': 8.507083e-04, 'ssm_b_im': 8.626513e-04, 'ssm_c_re': 1.692690e-03, 'ssm_c_im': 1.711475e-03, 'ssm_d': 3.298180e-02, 'ssm_log_step': 1.035889e+00, 'ssm_w_glu': 5.448543e-03, 'w_branch': 2.209101e-02, 'w_out': 3.810363e-02, 'norm_ffn_g': 6.425399e-02, 'w_ffn_gate': 2.756501e-02, 'w_ffn_up': 2.671594e-02, 'w_ffn_down': 4.428827e-02, 'norm_final_g': 1.600464e+01}


def _to_microbatches(a, axis):
    t = _jnp.moveaxis(a, axis, 0)
    t = t.reshape((N_MICROBATCH, t.shape[0] // N_MICROBATCH) + t.shape[1:])
    return _jnp.moveaxis(t, 1, axis + 1)


def setup_inputs(seed: int = 0) -> dict:
    inp = _fwd_setup_inputs(seed)
    key = _jax.random.fold_in(_jax.random.key(seed), 7919)
    shape, _ = _output_shape()
    out = dict(inp)
    out["loss_target"] = _jax.random.normal(_jax.random.fold_in(key, 0), shape, _jnp.float32)
    for i, name in enumerate(TWIN_WEIGHTS):
        w = inp[name].astype(_jnp.float32)
        if MOMENT_SCALE is None:
            s = _jnp.sqrt(_jnp.mean(_jnp.square(w)) + 1e-30)
        else:
            s = MOMENT_SCALE[name]
        km, kv = _jax.random.split(_jax.random.fold_in(key, i + 1))
        out[name] = w
        out["m_" + name] = s * _jax.random.normal(km, w.shape, _jnp.float32)
        out["v_" + name] = (s * s) * _jax.random.uniform(kv, w.shape, _jnp.float32, 0.5, 1.5)
    if N_MICROBATCH > 1:
        for name, axis in PER_EXAMPLE_BATCH_AXIS.items():
            out[name] = _to_microbatches(out[name], axis)
    return {'x': out['x'], 'norm_mix_g': out['norm_mix_g'], 'w_in': out['w_in'], 'gate_bias': out['gate_bias'], 'lru_conv_w': out['lru_conv_w'], 'lru_conv_b': out['lru_conv_b'], 'lru_wa': out['lru_wa'], 'lru_ba': out['lru_ba'], 'lru_wx': out['lru_wx'], 'lru_bx': out['lru_bx'], 'lru_lambda': out['lru_lambda'], 'attn_rel_bias': out['attn_rel_bias'], 'ssm_a_re': out['ssm_a_re'], 'ssm_a_im': out['ssm_a_im'], 'ssm_b_re': out['ssm_b_re'], 'ssm_b_im': out['ssm_b_im'], 'ssm_c_re': out['ssm_c_re'], 'ssm_c_im': out['ssm_c_im'], 'ssm_d': out['ssm_d'], 'ssm_log_step': out['ssm_log_step'], 'ssm_w_glu': out['ssm_w_glu'], 'w_branch': out['w_branch'], 'w_out': out['w_out'], 'norm_ffn_g': out['norm_ffn_g'], 'w_ffn_gate': out['w_ffn_gate'], 'w_ffn_up': out['w_ffn_up'], 'w_ffn_down': out['w_ffn_down'], 'norm_final_g': out['norm_final_g'], 'loss_target': out['loss_target'], 'm_norm_mix_g': out['m_norm_mix_g'], 'm_w_in': out['m_w_in'], 'm_gate_bias': out['m_gate_bias'], 'm_lru_conv_w': out['m_lru_conv_w'], 'm_lru_conv_b': out['m_lru_conv_b'], 'm_lru_wa': out['m_lru_wa'], 'm_lru_ba': out['m_lru_ba'], 'm_lru_wx': out['m_lru_wx'], 'm_lru_bx': out['m_lru_bx'], 'm_lru_lambda': out['m_lru_lambda'], 'm_attn_rel_bias': out['m_attn_rel_bias'], 'm_ssm_a_re': out['m_ssm_a_re'], 'm_ssm_a_im': out['m_ssm_a_im'], 'm_ssm_b_re': out['m_ssm_b_re'], 'm_ssm_b_im': out['m_ssm_b_im'], 'm_ssm_c_re': out['m_ssm_c_re'], 'm_ssm_c_im': out['m_ssm_c_im'], 'm_ssm_d': out['m_ssm_d'], 'm_ssm_log_step': out['m_ssm_log_step'], 'm_ssm_w_glu': out['m_ssm_w_glu'], 'm_w_branch': out['m_w_branch'], 'm_w_out': out['m_w_out'], 'm_norm_ffn_g': out['m_norm_ffn_g'], 'm_w_ffn_gate': out['m_w_ffn_gate'], 'm_w_ffn_up': out['m_w_ffn_up'], 'm_w_ffn_down': out['m_w_ffn_down'], 'm_norm_final_g': out['m_norm_final_g'], 'v_norm_mix_g': out['v_norm_mix_g'], 'v_w_in': out['v_w_in'], 'v_gate_bias': out['v_gate_bias'], 'v_lru_conv_w': out['v_lru_conv_w'], 'v_lru_conv_b': out['v_lru_conv_b'], 'v_lru_wa': out['v_lru_wa'], 'v_lru_ba': out['v_lru_ba'], 'v_lru_wx': out['v_lru_wx'], 'v_lru_bx': out['v_lru_bx'], 'v_lru_lambda': out['v_lru_lambda'], 'v_attn_rel_bias': out['v_attn_rel_bias'], 'v_ssm_a_re': out['v_ssm_a_re'], 'v_ssm_a_im': out['v_ssm_a_im'], 'v_ssm_b_re': out['v_ssm_b_re'], 'v_ssm_b_im': out['v_ssm_b_im'], 'v_ssm_c_re': out['v_ssm_c_re'], 'v_ssm_c_im': out['v_ssm_c_im'], 'v_ssm_d': out['v_ssm_d'], 'v_ssm_log_step': out['v_ssm_log_step'], 'v_ssm_w_glu': out['v_ssm_w_glu'], 'v_w_branch': out['v_w_branch'], 'v_w_out': out['v_w_out'], 'v_norm_ffn_g': out['v_norm_ffn_g'], 'v_w_ffn_gate': out['v_w_ffn_gate'], 'v_w_ffn_up': out['v_w_ffn_up'], 'v_w_ffn_down': out['v_w_ffn_down'], 'v_norm_final_g': out['v_norm_final_g']}


def _loss(weights, diff, rest, loss_target):
    with _jax.named_scope("forward"):
        args = {**rest, TWIN_DIFF_INPUT: diff, **{k: w.astype(_WEIGHT_DTYPES[k]) for k, w in weights.items()}}
        y = _forward(args)
    with _jax.named_scope("loss_head"):
        err = _jnp.square(y.astype(_jnp.float32) - loss_target)
        return 0.5 * _jnp.sum(_jnp.mean(err, axis=-1)) if err.ndim else 0.5 * err


def _adamw(w, g, m, v):
    m = ADAM_B1 * m + (1.0 - ADAM_B1) * g
    v = ADAM_B2 * v + (1.0 - ADAM_B2) * _jnp.square(g)
    m_hat = m / (1.0 - ADAM_B1 ** ADAM_STEP)
    v_hat = v / (1.0 - ADAM_B2 ** ADAM_STEP)
    delta = -ADAM_LR * (m_hat / (_jnp.sqrt(v_hat) + ADAM_EPS) + ADAM_WD * w)
    return delta, m, v


def reference(x, norm_mix_g, w_in, gate_bias, lru_conv_w, lru_conv_b, lru_wa, lru_ba, lru_wx, lru_bx, lru_lambda, attn_rel_bias, ssm_a_re, ssm_a_im, ssm_b_re, ssm_b_im, ssm_c_re, ssm_c_im, ssm_d, ssm_log_step, ssm_w_glu, w_branch, w_out, norm_ffn_g, w_ffn_gate, w_ffn_up, w_ffn_down, norm_final_g, loss_target, m_norm_mix_g, m_w_in, m_gate_bias, m_lru_conv_w, m_lru_conv_b, m_lru_wa, m_lru_ba, m_lru_wx, m_lru_bx, m_lru_lambda, m_attn_rel_bias, m_ssm_a_re, m_ssm_a_im, m_ssm_b_re, m_ssm_b_im, m_ssm_c_re, m_ssm_c_im, m_ssm_d, m_ssm_log_step, m_ssm_w_glu, m_w_branch, m_w_out, m_norm_ffn_g, m_w_ffn_gate, m_w_ffn_up, m_w_ffn_down, m_norm_final_g, v_norm_mix_g, v_w_in, v_gate_bias, v_lru_conv_w, v_lru_conv_b, v_lru_wa, v_lru_ba, v_lru_wx, v_lru_bx, v_lru_lambda, v_attn_rel_bias, v_ssm_a_re, v_ssm_a_im, v_ssm_b_re, v_ssm_b_im, v_ssm_c_re, v_ssm_c_im, v_ssm_d, v_ssm_log_step, v_ssm_w_glu, v_w_branch, v_w_out, v_norm_ffn_g, v_w_ffn_gate, v_w_ffn_up, v_w_ffn_down, v_norm_final_g):
    given = dict(x=x, norm_mix_g=norm_mix_g, w_in=w_in, gate_bias=gate_bias, lru_conv_w=lru_conv_w, lru_conv_b=lru_conv_b, lru_wa=lru_wa, lru_ba=lru_ba, lru_wx=lru_wx, lru_bx=lru_bx, lru_lambda=lru_lambda, attn_rel_bias=attn_rel_bias, ssm_a_re=ssm_a_re, ssm_a_im=ssm_a_im, ssm_b_re=ssm_b_re, ssm_b_im=ssm_b_im, ssm_c_re=ssm_c_re, ssm_c_im=ssm_c_im, ssm_d=ssm_d, ssm_log_step=ssm_log_step, ssm_w_glu=ssm_w_glu, w_branch=w_branch, w_out=w_out, norm_ffn_g=norm_ffn_g, w_ffn_gate=w_ffn_gate, w_ffn_up=w_ffn_up, w_ffn_down=w_ffn_down, norm_final_g=norm_final_g, loss_target=loss_target, m_norm_mix_g=m_norm_mix_g, m_w_in=m_w_in, m_gate_bias=m_gate_bias, m_lru_conv_w=m_lru_conv_w, m_lru_conv_b=m_lru_conv_b, m_lru_wa=m_lru_wa, m_lru_ba=m_lru_ba, m_lru_wx=m_lru_wx, m_lru_bx=m_lru_bx, m_lru_lambda=m_lru_lambda, m_attn_rel_bias=m_attn_rel_bias, m_ssm_a_re=m_ssm_a_re, m_ssm_a_im=m_ssm_a_im, m_ssm_b_re=m_ssm_b_re, m_ssm_b_im=m_ssm_b_im, m_ssm_c_re=m_ssm_c_re, m_ssm_c_im=m_ssm_c_im, m_ssm_d=m_ssm_d, m_ssm_log_step=m_ssm_log_step, m_ssm_w_glu=m_ssm_w_glu, m_w_branch=m_w_branch, m_w_out=m_w_out, m_norm_ffn_g=m_norm_ffn_g, m_w_ffn_gate=m_w_ffn_gate, m_w_ffn_up=m_w_ffn_up, m_w_ffn_down=m_w_ffn_down, m_norm_final_g=m_norm_final_g, v_norm_mix_g=v_norm_mix_g, v_w_in=v_w_in, v_gate_bias=v_gate_bias, v_lru_conv_w=v_lru_conv_w, v_lru_conv_b=v_lru_conv_b, v_lru_wa=v_lru_wa, v_lru_ba=v_lru_ba, v_lru_wx=v_lru_wx, v_lru_bx=v_lru_bx, v_lru_lambda=v_lru_lambda, v_attn_rel_bias=v_attn_rel_bias, v_ssm_a_re=v_ssm_a_re, v_ssm_a_im=v_ssm_a_im, v_ssm_b_re=v_ssm_b_re, v_ssm_b_im=v_ssm_b_im, v_ssm_c_re=v_ssm_c_re, v_ssm_c_im=v_ssm_c_im, v_ssm_d=v_ssm_d, v_ssm_log_step=v_ssm_log_step, v_ssm_w_glu=v_ssm_w_glu, v_w_branch=v_w_branch, v_w_out=v_w_out, v_norm_ffn_g=v_norm_ffn_g, v_w_ffn_gate=v_w_ffn_gate, v_w_ffn_up=v_w_ffn_up, v_w_ffn_down=v_w_ffn_down, v_norm_final_g=v_norm_final_g)
    weights = {n: given[n] for n in TWIN_WEIGHTS}
    shared = {n: given[n] for n in SHARED_INPUTS}
    per_example = {n: given[n] for n in ['x']}
    grad_fn = _jax.value_and_grad(_loss, argnums=(0, 1))

    def one_microbatch(ex, loss_target):
        ex = dict(ex)
        diff = ex.pop(TWIN_DIFF_INPUT)
        return grad_fn(weights, diff, {**shared, **ex}, loss_target)

    if N_MICROBATCH == 1:
        loss, (grad_w, grad_x) = one_microbatch(per_example, given["loss_target"])
    else:
        def body(carry, xs):
            loss_sum, grad_sum = carry
            l_k, (gw_k, gx_k) = one_microbatch(xs[0], xs[1])
            with _jax.named_scope("update"):
                return (loss_sum + l_k, _jax.tree.map(_jnp.add, grad_sum, gw_k)), gx_k

        init = (_jnp.zeros((), _jnp.float32), _jax.tree.map(_jnp.zeros_like, weights))
        (loss, grad_w), grad_x = _jax.lax.scan(body, init, (per_example, given["loss_target"]))
    with _jax.named_scope("update"):
        delta_w, new_m, new_v = {}, {}, {}
        for n in TWIN_WEIGHTS:
            delta_w[n], new_m[n], new_v[n] = _adamw(weights[n], grad_w[n], given["m_" + n], given["v_" + n])
    return (loss, grad_x, *[grad_w[n] for n in TWIN_WEIGHTS], *[delta_w[n] for n in TWIN_WEIGHTS],
            *[new_m[n] for n in TWIN_WEIGHTS], *[new_v[n] for n in TWIN_WEIGHTS])
```

```python
import functools
import math

import jax
import jax.numpy as jnp
from jax import lax
from jax.experimental import pallas as pl
from jax.experimental.pallas import tpu as pltpu

F32 = jnp.float32
BF16 = jnp.bfloat16

D_MODEL = 2048
MIX_W = 1024
N_BRANCH = 3
LRU_BLOCKS = 16
LRU_BW = 64
CONV_W = 4
LRU_C = 8.0
CHUNK = 64
ATT_HEADS = 8
ATT_HD = 128
ATT_LEFT = 8
ATT_BAND = (ATT_LEFT + 1) * CHUNK
MAX_REL = 128
N_REL = 2 * MAX_REL + 1
SSM_G = 64
SSM_H = 16
SSM_P = 64
FFN_H = 5632
IN_W = 6 * MIX_W + N_BRANCH * D_MODEL
NORM_EPS = 1e-6
MASK_VALUE = -1e30
ADAM_LR, ADAM_B1, ADAM_B2, ADAM_EPS, ADAM_WD, ADAM_STEP = 0.001, 0.9, 0.999, 1e-08, 0.01, 10

S5_L = 16
S5_LW = S5_L * SSM_H
N_CHIPS = 4
V7X_VMEM_LIMIT = 56 * 1024 * 1024
HI = lax.Precision.HIGHEST
MESH = pl.DeviceIdType.MESH


def _cparams(sem=None):
    return pltpu.CompilerParams(dimension_semantics=sem, vmem_limit_bytes=V7X_VMEM_LIMIT)


def _pick(n, prefs):
    for p in prefs:
        if n % p == 0:
            return p
    return n


def _mm(a, b, *, M, N, K, name, ta=False, tb=False, a_lead=None, b_lead=None, a_off=(0, 0), b_off=(0, 0),
        out_dtype=F32, res=None, tm=None, tn=None, tk=None):
    tm = tm or _pick(M, (1024, 512, 256, 128, 64, 32, 16, 8))
    tn = tn or _pick(N, (1024, 1408, 512, 256, 128))
    tk = tk or _pick(K, (1024, 1408, 512, 256, 128))
    nk = K // tk

    def spec(blk, lead, off, order):
        r0, c0 = off[0] // blk[0], off[1] // blk[1]
        assert off[0] % blk[0] == 0 and off[1] % blk[1] == 0
        if lead is None:
            return pl.BlockSpec(blk, lambda i, j, k: (r0 + order(i, j, k)[0], c0 + order(i, j, k)[1]))
        return pl.BlockSpec((None,) + blk, lambda i, j, k: (lead, r0 + order(i, j, k)[0], c0 + order(i, j, k)[1]))

    a_spec = spec((tk, tm), a_lead, a_off, lambda i, j, k: (k, i)) if ta else spec((tm, tk), a_lead, a_off, lambda i, j, k: (i, k))
    b_spec = spec((tn, tk), b_lead, b_off, lambda i, j, k: (j, k)) if tb else spec((tk, tn), b_lead, b_off, lambda i, j, k: (k, j))
    dims = (((0 if ta else 1,), (1 if tb else 0,)), ((), ()))
    in_specs = [a_spec, b_spec]
    args = [a, b]
    if res is not None:
        in_specs.append(pl.BlockSpec((tm, tn), lambda i, j, k: (i, j)))
        args.append(res)

    def body(*refs):
        if res is not None:
            a_ref, b_ref, r_ref, o_ref, acc = refs
        else:
            a_ref, b_ref, o_ref, acc = refs
            r_ref = None
        k = pl.program_id(2)
        part = lax.dot_general(a_ref[...], b_ref[...], dims, preferred_element_type=F32)

        @pl.when(k == 0)
        def _():
            acc[...] = part

        @pl.when(k > 0)
        def _():
            acc[...] += part

        @pl.when(k == nk - 1)
        def _():
            r = acc[...]
            if r_ref is not None:
                r = r + r_ref[...]
            o_ref[...] = r.astype(out_dtype)

    return pl.pallas_call(
        body, name=name, out_shape=jax.ShapeDtypeStruct((M, N), out_dtype),
        grid=(M // tm, N // tn, nk), in_specs=in_specs, out_specs=pl.BlockSpec((tm, tn), lambda i, j, k: (i, j)),
        scratch_shapes=[pltpu.VMEM((tm, tn), F32)],
        compiler_params=_cparams(("parallel", "parallel", "arbitrary")),
    )(*args)


def _rowwise(fn, rows, consts, out_rows, out_accs, *, R, name, tm=256, rs=16, cs=None):
    tm = min(tm, R)
    assert R % tm == 0 and tm % rs == 0
    n_r, n_c, n_o, n_a = len(rows), len(consts), len(out_rows), len(out_accs)
    nsteps = R // tm
    widths = [w for _, _, w in rows]
    if cs is not None:
        assert all(w == widths[0] for w in widths) and widths[0] % cs == 0
        col_chunks = [(c0, cs) for c0 in range(0, widths[0], cs)]
    else:
        col_chunks = [None]

    def body(*refs):
        r_refs = refs[:n_r]
        c_refs = refs[n_r:n_r + n_c]
        o_refs = refs[n_r + n_c:n_r + n_c + n_o]
        a_refs = refs[n_r + n_c + n_o:n_r + n_c + n_o + n_a]
        s_refs = refs[n_r + n_c + n_o + n_a:]
        i = pl.program_id(0)

        @pl.when(i == 0)
        def _():
            for s in s_refs:
                s[...] = jnp.zeros_like(s)

        def piece(g, carry):
            r0 = pl.multiple_of(g * rs, rs)
            for cc in col_chunks:
                csl = slice(None) if cc is None else slice(cc[0], cc[0] + cc[1])
                rp = [r[pl.ds(r0, rs), csl] for r in r_refs]
                cp = [c[:, csl] for c in c_refs]
                outs, accs = fn(rp, cp)
                for o_ref, o in zip(o_refs, outs):
                    o_ref[pl.ds(r0, rs), csl] = o.astype(o_ref.dtype)
                for s_ref, av in zip(s_refs, accs):
                    s_ref[:, csl] += av
            return carry

        lax.fori_loop(0, tm // rs, piece, 0)

        @pl.when(i == nsteps - 1)
        def _():
            for a_ref, s_ref in zip(a_refs, s_refs):
                a_ref[...] = jnp.sum(s_ref[...], axis=0, keepdims=True)

    in_specs = [pl.BlockSpec((tm, w), functools.partial(lambda i, cb: (i, cb), cb=off // w)) for _, off, w in rows]
    for _, off, w in rows:
        assert off % w == 0
    in_specs += [pl.BlockSpec(c.shape, lambda i: (0, 0)) for c in consts]
    out_specs = [pl.BlockSpec((tm, w), lambda i: (i, 0)) for w, _ in out_rows]
    out_specs += [pl.BlockSpec((1, w), lambda i: (0, 0)) for w in out_accs]
    out_shape = [jax.ShapeDtypeStruct((R, w), dt) for w, dt in out_rows]
    out_shape += [jax.ShapeDtypeStruct((1, w), F32) for w in out_accs]
    return pl.pallas_call(
        body, name=name, out_shape=tuple(out_shape), grid=(nsteps,), in_specs=in_specs, out_specs=tuple(out_specs),
        scratch_shapes=[pltpu.VMEM((rs, w), F32) for w in out_accs],
        compiler_params=_cparams(("arbitrary",)),
    )(*[r for r, _, _ in rows], *consts)


def _rms(x, g):
    r = lax.rsqrt(jnp.mean(x * x, axis=-1, keepdims=True) + NORM_EPS)
    return x * r * g


def _rms_fwd(x, g, name):
    R = x.shape[0]

    def fn(rp, cp):
        return [_rms(rp[0], cp[0])], []

    return _rowwise(fn, [(x, 0, D_MODEL)], [g], [(D_MODEL, BF16)], [], R=R, name=name)[0]


def _rms_bwd(x, g, dh, dres, name):
    R = x.shape[0]

    def fn(rp, cp):
        xv, dhv, drv = rp
        _, pull = jax.vjp(_rms, xv, jnp.broadcast_to(cp[0], xv.shape))
        dx, dgv = pull(dhv)
        return [drv + dx], [dgv]

    dx, dg = _rowwise(fn, [(x, 0, D_MODEL), (dh, 0, D_MODEL), (dres, 0, D_MODEL)], [g], [(D_MODEL, F32)], [D_MODEL],
                      R=R, name=name, rs=8)
    return dx, dg


def _final_loss(x, g, tgt, name):
    R = x.shape[0]

    def loss_rows(xv, gv, tv):
        e = _rms(xv, gv) - tv
        return 0.5 * jnp.mean(e * e, axis=-1, keepdims=True)

    def fn(rp, cp):
        xv, tv = rp
        lr, pull = jax.vjp(lambda a, b: loss_rows(a, b, tv), xv, jnp.broadcast_to(cp[0], xv.shape))
        dx, dgv = pull(jnp.ones_like(lr))
        return [dx], [dgv, jnp.broadcast_to(lr, (lr.shape[0], 128))]

    dx, dg, lsum = _rowwise(fn, [(x, 0, D_MODEL), (tgt, 0, D_MODEL)], [g], [(D_MODEL, F32)], [D_MODEL, 128],
                            R=R, name=name, rs=8)
    return lsum[0, 0], dx, dg


def _neg_expm1(z):
    u = jnp.exp(z)
    safe = jnp.where(u == 1.0, 0.5, u)
    return -jnp.where(u == 1.0, z, (safe - 1.0) * z / jnp.log(safe))


def _lru_ab(xc, pr, pi, lam):
    r = jax.nn.sigmoid(pr)
    i = jax.nn.sigmoid(pi)
    log_a = -LRU_C * r * jax.nn.softplus(-lam)
    a = jnp.exp(log_a)
    b = jnp.sqrt(_neg_expm1(2.0 * log_a)) * (i * xc)
    return a, b


def _gated(h, gate):
    return h * jax.nn.gelu(gate)


def _row_iota8(w):
    return lax.broadcasted_iota(jnp.int32, (8, w), 0)


def _shift_dn(x, halo, s):
    xs = pltpu.roll(x, s, 0)
    hs = pltpu.roll(halo, s, 0)
    first = jnp.where(_row_iota8(x.shape[1]) < s, hs, xs[0:8])
    return jnp.concatenate([first, xs[8:]], axis=0) if x.shape[0] > 8 else first


def _shift_up(x, nxt, s):
    n = x.shape[0]
    xs = pltpu.roll(x, n - s, 0)
    ns = pltpu.roll(nxt, 8 - s, 0)
    last = jnp.where(_row_iota8(x.shape[1]) >= 8 - s, ns, xs[n - 8:])
    return jnp.concatenate([xs[:n - 8], last], axis=0) if n > 8 else last


def _lru_tiles(T):
    tT = min(256, T)
    return tT, T // tT


def _lru_fwd(proj, cw, cb, wax, bax, lam, name):
    T = proj.shape[0]
    W = MIX_W
    tT, nT = _lru_tiles(T)

    def body(x_ref, xh_ref, gt_ref, cw_ref, cb_ref, wax_ref, bax_ref, lam_ref, y_ref, h_ref, a_s, b_s, hc_s):
        i = pl.program_id(0)

        @pl.when(i == 0)
        def _():
            hc_s[...] = jnp.zeros_like(hc_s)

        x = x_ref[...]
        halo = jnp.where(i > 0, xh_ref[...], 0.0)
        w = cw_ref[...]
        xc = (cb_ref[...] + w[3:4] * x + w[2:3] * _shift_dn(x, halo, 1) + w[1:2] * _shift_dn(x, halo, 2)
              + w[0:1] * _shift_dn(x, halo, 3))
        pre = jnp.dot(xc.astype(BF16), wax_ref[...], preferred_element_type=F32) + bax_ref[...]
        a, b = _lru_ab(xc, pre[:, :W], pre[:, W:], lam_ref[...])
        a_s[...] = a
        b_s[...] = b
        row = _row_iota8(W)

        def grp(gi, hprev):
            r0 = pl.multiple_of(gi * 8, 8)
            A = a_s[pl.ds(r0, 8), :]
            B = b_s[pl.ds(r0, 8), :]
            for s in (1, 2, 4):
                As = pltpu.roll(A, s, 0)
                Bs = pltpu.roll(B, s, 0)
                m = row >= s
                B = jnp.where(m, A * Bs + B, B)
                A = jnp.where(m, A * As, A)
            H = A * hprev + B
            h_ref[pl.ds(r0, 8), :] = H
            return H[7:8, :]

        hc_s[0:1, :] = lax.fori_loop(0, tT // 8, grp, hc_s[0:1, :])
        y_ref[...] = _gated(h_ref[...], gt_ref[...]).astype(BF16)

    hb = tT // 8
    return pl.pallas_call(
        body, name=name,
        out_shape=(jax.ShapeDtypeStruct((T, W), BF16), jax.ShapeDtypeStruct((T, W), F32)),
        grid=(nT,),
        in_specs=[pl.BlockSpec((tT, W), lambda i: (i, 0)),
                  pl.BlockSpec((8, W), lambda i: (jnp.maximum(i * hb - 1, 0), 0)),
                  pl.BlockSpec((tT, W), lambda i: (i, 1)),
                  pl.BlockSpec((CONV_W, W), lambda i: (0, 0)), pl.BlockSpec((1, W), lambda i: (0, 0)),
                  pl.BlockSpec((W, 2 * W), lambda i: (0, 0)), pl.BlockSpec((1, 2 * W), lambda i: (0, 0)),
                  pl.BlockSpec((1, W), lambda i: (0, 0))],
        out_specs=(pl.BlockSpec((tT, W), lambda i: (i, 0)), pl.BlockSpec((tT, W), lambda i: (i, 0))),
        scratch_shapes=[pltpu.VMEM((tT, W), F32), pltpu.VMEM((tT, W), F32), pltpu.VMEM((8, W), F32)],
        compiler_params=_cparams(("arbitrary",)),
    )(proj, proj, proj, cw, cb, wax, bax, lam)


def _lru_bwd(proj, h, dy, cw, cb, wax, bax, lam, name):
    T = proj.shape[0]
    W = MIX_W
    tT, nT = _lru_tiles(T)
    hb = tT // 8

    def body(x_ref, xh_ref, gt_ref, h_ref, hh_ref, dy_ref, cw_ref, cb_ref, wax_ref, bax_ref, lam_ref,
             dx_ref, dgt_ref, dcw_ref, dcb_ref, dwax_ref, dbax_ref, dlam_ref,
             al_s, be_s, d_s, ca_s, cd_s, cx_s):
        i = pl.program_id(0)
        ib = nT - 1 - i

        @pl.when(i == 0)
        def _():
            for r in (ca_s, cd_s, cx_s, dcw_ref, dcb_ref, dwax_ref, dbax_ref, dlam_ref):
                r[...] = jnp.zeros_like(r)

        x = x_ref[...]
        halo = jnp.where(ib > 0, xh_ref[...], 0.0)
        w = cw_ref[...]
        x1, x2, x3 = _shift_dn(x, halo, 1), _shift_dn(x, halo, 2), _shift_dn(x, halo, 3)
        xc = cb_ref[...] + w[3:4] * x + w[2:3] * x1 + w[1:2] * x2 + w[0:1] * x3
        xcb = xc.astype(BF16)
        pre = jnp.dot(xcb, wax_ref[...], preferred_element_type=F32) + bax_ref[...]
        (a, _), pull_ab = jax.vjp(_lru_ab, xc, pre[:, :W], pre[:, W:], lam_ref[...])
        hv = h_ref[...]
        hprev = _shift_dn(hv, jnp.where(ib > 0, hh_ref[...], 0.0), 1)
        _, pull_y = jax.vjp(_gated, hv, gt_ref[...])
        dh_out, dgt = pull_y(dy_ref[...])
        dgt_ref[...] = dgt.astype(BF16)
        al_s[...] = _shift_up(a, ca_s[...], 1)
        be_s[...] = dh_out
        row = _row_iota8(W)
        ng = tT // 8

        def grp(k, dnext):
            r0 = pl.multiple_of((ng - 1 - k) * 8, 8)
            A = al_s[pl.ds(r0, 8), :]
            B = be_s[pl.ds(r0, 8), :]
            for s in (1, 2, 4):
                As = pltpu.roll(A, 8 - s, 0)
                Bs = pltpu.roll(B, 8 - s, 0)
                m = row < 8 - s
                B = jnp.where(m, A * Bs + B, B)
                A = jnp.where(m, A * As, A)
            Dg = A * dnext + B
            d_s[pl.ds(r0, 8), :] = Dg
            return Dg[0:1, :]

        lax.fori_loop(0, ng, grp, cd_s[0:1, :])
        Dv = d_s[...]
        dxc1, dpr, dpi, dlam = pull_ab((Dv * hprev, Dv))
        dpre = jnp.concatenate([dpr, dpi], axis=1)
        dpb = dpre.astype(BF16)
        dxc = dxc1 + lax.dot_general(dpb, wax_ref[...], (((1,), (1,)), ((), ())), preferred_element_type=F32)
        dwax_ref[...] += lax.dot_general(xcb, dpb, (((0,), (0,)), ((), ())), preferred_element_type=F32)
        dbax_ref[...] += jnp.sum(dpre, axis=0, keepdims=True)
        dlam_ref[...] += dlam
        dcb_ref[...] += jnp.sum(dxc, axis=0, keepdims=True)
        dcw_ref[...] += jnp.concatenate([jnp.sum(dxc * x3, axis=0, keepdims=True), jnp.sum(dxc * x2, axis=0, keepdims=True),
                                         jnp.sum(dxc * x1, axis=0, keepdims=True), jnp.sum(dxc * x, axis=0, keepdims=True)], axis=0)
        nxt = cx_s[...]
        dx = (w[3:4] * dxc + w[2:3] * _shift_up(dxc, nxt, 1) + w[1:2] * _shift_up(dxc, nxt, 2)
              + w[0:1] * _shift_up(dxc, nxt, 3))
        dx_ref[...] = dx.astype(BF16)
        ca_s[...] = a[0:8]
        cd_s[...] = Dv[0:8]
        cx_s[...] = dxc[0:8]

    rev = lambda i: nT - 1 - i
    const = lambda shape: pl.BlockSpec(shape, lambda i: (0, 0))
    return pl.pallas_call(
        body, name=name,
        out_shape=(jax.ShapeDtypeStruct((T, W), BF16), jax.ShapeDtypeStruct((T, W), BF16),
                   jax.ShapeDtypeStruct((CONV_W, W), F32), jax.ShapeDtypeStruct((1, W), F32),
                   jax.ShapeDtypeStruct((W, 2 * W), F32), jax.ShapeDtypeStruct((1, 2 * W), F32),
                   jax.ShapeDtypeStruct((1, W), F32)),
        grid=(nT,),
        in_specs=[pl.BlockSpec((tT, W), lambda i: (rev(i), 0)),
                  pl.BlockSpec((8, W), lambda i: (jnp.maximum(rev(i) * hb - 1, 0), 0)),
                  pl.BlockSpec((tT, W), lambda i: (rev(i), 1)),
                  pl.BlockSpec((tT, W), lambda i: (rev(i), 0)),
                  pl.BlockSpec((8, W), lambda i: (jnp.maximum(rev(i) * hb - 1, 0), 0)),
                  pl.BlockSpec((tT, W), lambda i: (rev(i), 0)),
                  const((CONV_W, W)), const((1, W)), const((W, 2 * W)), const((1, 2 * W)), const((1, W))],
        out_specs=(pl.BlockSpec((tT, W), lambda i: (rev(i), 0)), pl.BlockSpec((tT, W), lambda i: (rev(i), 0)),
                   const((CONV_W, W)), const((1, W)), const((W, 2 * W)), const((1, 2 * W)), const((1, W))),
        scratch_shapes=[pltpu.VMEM((tT, W), F32), pltpu.VMEM((tT, W), F32), pltpu.VMEM((tT, W), F32),
                        pltpu.VMEM((8, W), F32), pltpu.VMEM((8, W), F32), pltpu.VMEM((8, W), F32)],
        compiler_params=_cparams(("arbitrary",)),
    )(proj, proj, proj, h, h, dy, cw, cb, wax, bax, lam)


def _lru_prep(wa, wx, ba, bx):
    eye = jnp.eye(LRU_BLOCKS, dtype=F32)

    def dense(wb):
        return (wb[:, :, None, :] * eye[:, None, :, None]).reshape(MIX_W, MIX_W)

    wax = jnp.concatenate([dense(wa), dense(wx)], axis=1)
    bax = jnp.concatenate([ba, bx])[None, :]
    return wax, bax


def _attn_bias(rel_bias):
    n_far = ATT_LEFT * CHUNK - MAX_REL + CHUNK
    far = jnp.broadcast_to(rel_bias[:, 2 * MAX_REL:], (ATT_HEADS, n_far))
    near = rel_bias[:, MAX_REL - (CHUNK - 1):2 * MAX_REL][:, ::-1]
    bvec = jnp.concatenate([far, near], axis=1)
    return jnp.stack([bvec[:, CHUNK - 1 - i:CHUNK - 1 - i + ATT_BAND] for i in range(CHUNK)], axis=1)


_NT = (((1,), (1,)), ((), ()))
_TN = (((0,), (0,)), ((), ()))
_ATT_PAD = ATT_LEFT * CHUNK
_Q_BLK, _K_BLK, _V_BLK = 2 * MIX_W // ATT_HD, 3 * MIX_W // ATT_HD, 4 * MIX_W // ATT_HD


def _attn_probs(q, kb, bias, c):
    s = lax.dot_general(q, kb, _NT, preferred_element_type=F32) * (ATT_HD ** -0.5) + bias
    kpos = lax.broadcasted_iota(jnp.int32, s.shape, 1)
    s = jnp.where(kpos >= (ATT_LEFT - c) * CHUNK, s, MASK_VALUE)
    e = jnp.exp(s - jnp.max(s, axis=-1, keepdims=True))
    return e / jnp.sum(e, axis=-1, keepdims=True)


def _attn_fwd(proj, bias, name):
    T = proj.shape[0]
    nC = T // CHUNK

    def body(q_ref, k_ref, v_ref, b_ref, o_ref, kp, vp):
        kp[0:_ATT_PAD, :] = jnp.zeros((_ATT_PAD, ATT_HD), BF16)
        vp[0:_ATT_PAD, :] = jnp.zeros((_ATT_PAD, ATT_HD), BF16)
        kp[_ATT_PAD:, :] = k_ref[...].astype(BF16)
        vp[_ATT_PAD:, :] = v_ref[...].astype(BF16)
        bias_v = b_ref[0]

        def chunk(c, carry):
            r0 = pl.multiple_of(c * CHUNK, CHUNK)
            q = q_ref[pl.ds(r0, CHUNK), :].astype(BF16)
            p = _attn_probs(q, kp[pl.ds(r0, ATT_BAND), :], bias_v, c)
            o = jnp.dot(p.astype(BF16), vp[pl.ds(r0, ATT_BAND), :], preferred_element_type=F32)
            o_ref[pl.ds(r0, CHUNK), :] = o.astype(BF16)
            return carry

        lax.fori_loop(0, nC, chunk, 0)

    return pl.pallas_call(
        body, name=name, out_shape=jax.ShapeDtypeStruct((T, MIX_W), BF16), grid=(ATT_HEADS,),
        in_specs=[pl.BlockSpec((T, ATT_HD), lambda h: (0, _Q_BLK + h)), pl.BlockSpec((T, ATT_HD), lambda h: (0, _K_BLK + h)),
                  pl.BlockSpec((T, ATT_HD), lambda h: (0, _V_BLK + h)), pl.BlockSpec((1, CHUNK, ATT_BAND), lambda h: (h, 0, 0))],
        out_specs=pl.BlockSpec((T, ATT_HD), lambda h: (0, h)),
        scratch_shapes=[pltpu.VMEM((T + _ATT_PAD, ATT_HD), BF16), pltpu.VMEM((T + _ATT_PAD, ATT_HD), BF16)],
        compiler_params=_cparams(("arbitrary",)),
    )(proj, proj, proj, bias)


def _attn_bwd(proj, bias, do, name):
    T = proj.shape[0]
    nC = T // CHUNK

    def body(q_ref, k_ref, v_ref, b_ref, do_ref, dq_ref, dk_ref, dv_ref, db_ref, kp, vp, dkp, dvp, dbs):
        kp[0:_ATT_PAD, :] = jnp.zeros((_ATT_PAD, ATT_HD), BF16)
        vp[0:_ATT_PAD, :] = jnp.zeros((_ATT_PAD, ATT_HD), BF16)
        kp[_ATT_PAD:, :] = k_ref[...].astype(BF16)
        vp[_ATT_PAD:, :] = v_ref[...].astype(BF16)
        dkp[...] = jnp.zeros_like(dkp)
        dvp[...] = jnp.zeros_like(dvp)
        dbs[...] = jnp.zeros_like(dbs)
        bias_v = b_ref[0]

        def chunk(c, carry):
            r0 = pl.multiple_of(c * CHUNK, CHUNK)
            q = q_ref[pl.ds(r0, CHUNK), :].astype(BF16)
            kb = kp[pl.ds(r0, ATT_BAND), :]
            vb = vp[pl.ds(r0, ATT_BAND), :]
            dob = do_ref[pl.ds(r0, CHUNK), :].astype(BF16)
            p = _attn_probs(q, kb, bias_v, c)
            dp = lax.dot_general(dob, vb, _NT, preferred_element_type=F32)
            ds = p * (dp - jnp.sum(p * dp, axis=-1, keepdims=True))
            dbs[...] += ds
            dsb = (ds * (ATT_HD ** -0.5)).astype(BF16)
            dq_ref[pl.ds(r0, CHUNK), :] = jnp.dot(dsb, kb, preferred_element_type=F32).astype(BF16)
            dkp[pl.ds(r0, ATT_BAND), :] += lax.dot_general(dsb, q, _TN, preferred_element_type=F32)
            dvp[pl.ds(r0, ATT_BAND), :] += lax.dot_general(p.astype(BF16), dob, _TN, preferred_element_type=F32)
            return carry

        lax.fori_loop(0, nC, chunk, 0)
        dk_ref[...] = dkp[_ATT_PAD:, :].astype(BF16)
        dv_ref[...] = dvp[_ATT_PAD:, :].astype(BF16)
        db_ref[0] = dbs[...]

    hspec = pl.BlockSpec((T, ATT_HD), lambda h: (0, h))
    osd = jax.ShapeDtypeStruct((T, MIX_W), BF16)
    return pl.pallas_call(
        body, name=name,
        out_shape=(osd, osd, osd, jax.ShapeDtypeStruct((ATT_HEADS, CHUNK, ATT_BAND), F32)), grid=(ATT_HEADS,),
        in_specs=[pl.BlockSpec((T, ATT_HD), lambda h: (0, _Q_BLK + h)), pl.BlockSpec((T, ATT_HD), lambda h: (0, _K_BLK + h)),
                  pl.BlockSpec((T, ATT_HD), lambda h: (0, _V_BLK + h)), pl.BlockSpec((1, CHUNK, ATT_BAND), lambda h: (h, 0, 0)),
                  hspec],
        out_specs=(hspec, hspec, hspec, pl.BlockSpec((1, CHUNK, ATT_BAND), lambda h: (h, 0, 0))),
        scratch_shapes=[pltpu.VMEM((T + _ATT_PAD, ATT_HD), BF16), pltpu.VMEM((T + _ATT_PAD, ATT_HD), BF16),
                        pltpu.VMEM((T + _ATT_PAD, ATT_HD), F32), pltpu.VMEM((T + _ATT_PAD, ATT_HD), F32),
                        pltpu.VMEM((CHUNK, ATT_BAND), F32)],
        compiler_params=_cparams(("arbitrary",)),
    )(proj, proj, proj, bias, do)


def _s5_prep(a_re, a_im, b_re, b_im, c_re, c_im, d, log_step):
    step = jnp.exp(log_step)[:, None]
    mag = jnp.exp(a_re * step)
    ang = a_im * step
    lb_re = mag * jnp.cos(ang)
    lb_im = mag * jnp.sin(ang)
    den = a_re * a_re + a_im * a_im
    nr = lb_re - 1.0
    coef_re = (nr * a_re + lb_im * a_im) / den
    coef_im = (lb_im * a_re - nr * a_im) / den
    bb_re = coef_re[..., None] * b_re - coef_im[..., None] * b_im
    bb_im = coef_re[..., None] * b_im + coef_im[..., None] * b_re
    prs, pis = [jnp.ones_like(lb_re)], [jnp.zeros_like(lb_re)]
    for _ in range(S5_L):
        prs.append(prs[-1] * lb_re - pis[-1] * lb_im)
        pis.append(prs[-2] * lb_im + pis[-1] * lb_re)
    PR, PI = jnp.stack(prs), jnp.stack(pis)
    cl_re = c_re[None] * PR[:, :, None, :] - c_im[None] * PI[:, :, None, :]
    cl_im = c_re[None] * PI[:, :, None, :] + c_im[None] * PR[:, :, None, :]
    kt = (jnp.einsum("tghp,gpk->tghk", cl_re[:S5_L], bb_re, precision=HI)
          - jnp.einsum("tghp,gpk->tghk", cl_im[:S5_L], bb_im, precision=HI))
    kpad = jnp.concatenate([jnp.zeros((S5_L - 1,) + kt.shape[1:], F32), kt], axis=0)
    blocks = jnp.stack([kpad[S5_L - 1 - lp:2 * S5_L - 1 - lp] for lp in range(S5_L)], axis=0)
    tgt = blocks.transpose(2, 0, 4, 1, 3).reshape(SSM_G, S5_LW, S5_LW)
    prr, pir = PR[:S5_L][::-1], PI[:S5_L][::-1]
    bret = (prr[:, :, None, :] * bb_re.transpose(0, 2, 1)[None] - pir[:, :, None, :] * bb_im.transpose(0, 2, 1)[None])
    bimt = (prr[:, :, None, :] * bb_im.transpose(0, 2, 1)[None] + pir[:, :, None, :] * bb_re.transpose(0, 2, 1)[None])
    bret = bret.transpose(1, 0, 2, 3).reshape(SSM_G, S5_LW, SSM_P)
    bimt = bimt.transpose(1, 0, 2, 3).reshape(SSM_G, S5_LW, SSM_P)
    cre = cl_re[1:].transpose(1, 3, 0, 2).reshape(SSM_G, SSM_P, S5_LW)
    cim = (-cl_im[1:]).transpose(1, 3, 0, 2).reshape(SSM_G, SSM_P, S5_LW)
    dflat = jnp.broadcast_to(d.reshape(SSM_G, 1, SSM_H), (SSM_G, S5_L, SSM_H)).reshape(SSM_G, 1, S5_LW)
    return tgt, bret, bimt, cre, cim, PR[S5_L], PI[S5_L], dflat


_S5_GB = 8


def _bdot(a, b, dims):
    return lax.dot_general(a, b, dims, preferred_element_type=F32, precision=HI)


_B_NN = (((2,), (1,)), ((0,), (0,)))
_B_NT = (((2,), (2,)), ((0,), (0,)))
_B_TN = (((1,), (1,)), ((0,), (0,)))


def _gspec(shape):
    return pl.BlockSpec((_S5_GB,) + shape, lambda g: (g, 0, 0))


def _s5_in(u, bret, bimt, name):
    C = u.shape[1]

    def body(u_ref, br_ref, bi_ref, sr_ref, si_ref):
        uv = u_ref[...]
        sr_ref[...] = _bdot(uv, br_ref[...], _B_NN)
        si_ref[...] = _bdot(uv, bi_ref[...], _B_NN)

    sd = jax.ShapeDtypeStruct((SSM_G, C, SSM_P), F32)
    return pl.pallas_call(
        body, name=name, out_shape=(sd, sd), grid=(SSM_G // _S5_GB,),
        in_specs=[_gspec((C, S5_LW)), _gspec((S5_LW, SSM_P)), _gspec((S5_LW, SSM_P))],
        out_specs=(_gspec((C, SSM_P)), _gspec((C, SSM_P))), compiler_params=_cparams(("parallel",)),
    )(u, bret, bimt)


def _s5_scan(sin_re, sin_im, lr, li, name):
    C = sin_re.shape[0]

    def body(ir_ref, ii_ref, lr_ref, li_ref, or_ref, oi_ref):
        lrv, liv = lr_ref[...], li_ref[...]

        def step(c, s):
            sr, si = s
            or_ref[c] = sr
            oi_ref[c] = si
            return lrv * sr - liv * si + ir_ref[c], lrv * si + liv * sr + ii_ref[c]

        z = jnp.zeros((SSM_G, SSM_P), F32)
        lax.fori_loop(0, C, step, (z, z))

    sd = jax.ShapeDtypeStruct((C, SSM_G, SSM_P), F32)
    return pl.pallas_call(body, name=name, out_shape=(sd, sd), compiler_params=_cparams())(sin_re, sin_im, lr, li)


def _s5_out(u, sp_re, sp_im, tgt, cre, cim, dflat, name):
    C = u.shape[1]

    def body(u_ref, sr_ref, si_ref, t_ref, cr_ref, ci_ref, d_ref, pre_ref, y_ref):
        uv = u_ref[...]
        pre = (_bdot(uv, t_ref[...], _B_NN) + _bdot(sr_ref[...], cr_ref[...], _B_NN)
               + _bdot(si_ref[...], ci_ref[...], _B_NN) + d_ref[...] * uv)
        pre_ref[...] = pre
        y_ref[...] = jax.nn.gelu(pre).astype(BF16)

    return pl.pallas_call(
        body, name=name,
        out_shape=(jax.ShapeDtypeStruct((SSM_G, C, S5_LW), F32), jax.ShapeDtypeStruct((SSM_G, C, S5_LW), BF16)),
        grid=(SSM_G // _S5_GB,),
        in_specs=[_gspec((C, S5_LW)), _gspec((C, SSM_P)), _gspec((C, SSM_P)), _gspec((S5_LW, S5_LW)),
                  _gspec((SSM_P, S5_LW)), _gspec((SSM_P, S5_LW)), _gspec((1, S5_LW))],
        out_specs=(_gspec((C, S5_LW)), _gspec((C, S5_LW))), compiler_params=_cparams(("parallel",)),
    )(u, sp_re, sp_im, tgt, cre, cim, dflat)


def _s5_bwd_out(dy, pre, u, sp_re, sp_im, cre, cim, name):
    C = u.shape[1]

    def body(dy_ref, pre_ref, u_ref, sr_ref, si_ref, cr_ref, ci_ref,
             dpre_ref, dsr_ref, dsi_ref, dt_ref, dcr_ref, dci_ref, dd_ref):
        _, pull = jax.vjp(jax.nn.gelu, pre_ref[...])
        dpre = pull(dy_ref[...])[0]
        uv = u_ref[...]
        dpre_ref[...] = dpre
        dsr_ref[...] = _bdot(dpre, cr_ref[...], _B_NT)
        dsi_ref[...] = _bdot(dpre, ci_ref[...], _B_NT)
        dt_ref[...] = _bdot(uv, dpre, _B_TN)
        dcr_ref[...] = _bdot(sr_ref[...], dpre, _B_TN)
        dci_ref[...] = _bdot(si_ref[...], dpre, _B_TN)
        dd_ref[...] = jnp.sum(dpre * uv, axis=1, keepdims=True)

    sd = jax.ShapeDtypeStruct
    return pl.pallas_call(
        body, name=name,
        out_shape=(sd((SSM_G, C, S5_LW), F32), sd((SSM_G, C, SSM_P), F32), sd((SSM_G, C, SSM_P), F32),
                   sd((SSM_G, S5_LW, S5_LW), F32), sd((SSM_G, SSM_P, S5_LW), F32), sd((SSM_G, SSM_P, S5_LW), F32),
                   sd((SSM_G, 1, S5_LW), F32)),
        grid=(SSM_G // _S5_GB,),
        in_specs=[_gspec((C, S5_LW)), _gspec((C, S5_LW)), _gspec((C, S5_LW)), _gspec((C, SSM_P)), _gspec((C, SSM_P)),
                  _gspec((SSM_P, S5_LW)), _gspec((SSM_P, S5_LW))],
        out_specs=(_gspec((C, S5_LW)), _gspec((C, SSM_P)), _gspec((C, SSM_P)), _gspec((S5_LW, S5_LW)),
                   _gspec((SSM_P, S5_LW)), _gspec((SSM_P, S5_LW)), _gspec((1, S5_LW))),
        compiler_params=_cparams(("parallel",)),
    )(dy, pre, u, sp_re, sp_im, cre, cim)


def _s5_rscan(dsp_re, dsp_im, sp_re, sp_im, lr, li, name):
    C = dsp_re.shape[0]

    def body(gr_ref, gi_ref, sr_ref, si_ref, lr_ref, li_ref, or_ref, oi_ref, dlr_ref, dli_ref):
        lrv, liv = lr_ref[...], li_ref[...]

        def step(k, carry):
            c = C - 1 - k
            dr, di, alr, ali = carry
            or_ref[c] = dr
            oi_ref[c] = di
            sr, si = sr_ref[c], si_ref[c]
            alr = alr + dr * sr + di * si
            ali = ali + di * sr - dr * si
            return gr_ref[c] + lrv * dr + liv * di, gi_ref[c] + lrv * di - liv * dr, alr, ali

        z = jnp.zeros((SSM_G, SSM_P), F32)
        _, _, alr, ali = lax.fori_loop(0, C, step, (z, z, z, z))
        dlr_ref[...] = alr
        dli_ref[...] = ali

    sd = jax.ShapeDtypeStruct((C, SSM_G, SSM_P), F32)
    sp = jax.ShapeDtypeStruct((SSM_G, SSM_P), F32)
    return pl.pallas_call(body, name=name, out_shape=(sd, sd, sp, sp), compiler_params=_cparams())(
        dsp_re, dsp_im, sp_re, sp_im, lr, li)


def _s5_bwd_in(dpre, dsin_re, dsin_im, u, tgt, bret, bimt, dflat, name):
    C = u.shape[1]

    def body(dp_ref, dr_ref, di_ref, u_ref, t_ref, br_ref, bi_ref, d_ref, du_ref, dbr_ref, dbi_ref):
        dp = dp_ref[...]
        dr, di, uv = dr_ref[...], di_ref[...], u_ref[...]
        du = (_bdot(dp, t_ref[...], _B_NT) + _bdot(dr, br_ref[...], _B_NT) + _bdot(di, bi_ref[...], _B_NT)
              + d_ref[...] * dp)
        du_ref[...] = du.astype(BF16)
        dbr_ref[...] = _bdot(uv, dr, _B_TN)
        dbi_ref[...] = _bdot(uv, di, _B_TN)

    sd = jax.ShapeDtypeStruct
    return pl.pallas_call(
        body, name=name,
        out_shape=(sd((SSM_G, C, S5_LW), BF16), sd((SSM_G, S5_LW, SSM_P), F32), sd((SSM_G, S5_LW, SSM_P), F32)),
        grid=(SSM_G // _S5_GB,),
        in_specs=[_gspec((C, S5_LW)), _gspec((C, SSM_P)), _gspec((C, SSM_P)), _gspec((C, S5_LW)),
                  _gspec((S5_LW, S5_LW)), _gspec((S5_LW, SSM_P)), _gspec((S5_LW, SSM_P)), _gspec((1, S5_LW))],
        out_specs=(_gspec((C, S5_LW)), _gspec((S5_LW, SSM_P)), _gspec((S5_LW, SSM_P))),
        compiler_params=_cparams(("parallel",)),
    )(dpre, dsin_re, dsin_im, u, tgt, bret, bimt, dflat)


def _to_chunks(v):
    T = v.shape[0]
    return v.reshape(T // S5_L, S5_L, SSM_G, SSM_H).transpose(2, 0, 1, 3).reshape(SSM_G, T // S5_L, S5_LW)


def _from_chunks(v):
    C = v.shape[1]
    return v.reshape(SSM_G, C, S5_L, SSM_H).transpose(1, 2, 0, 3).reshape(C * S5_L, MIX_W)


def _merge_fn(bra, brb, pc, pg, g0, g1, g2, b0, b1, b2):
    sg = jax.nn.sigmoid
    return sg(g0 + b0) * bra + sg(g1 + b1) * brb + sg(g2 + b2) * (pc * sg(pg))


_EW_CS = 256
_GATE_OFF = 6 * MIX_W


def _merge_rows(br4, proj):
    return [(br4, k * D_MODEL, D_MODEL) for k in range(4)] + [(proj, _GATE_OFF + k * D_MODEL, D_MODEL) for k in range(3)]


def _merge_fwd(br4, proj, gb3, name):
    def fn(rp, cp):
        b = cp[0]
        return [_merge_fn(*rp, b[0:1], b[1:2], b[2:3])], []

    return _rowwise(fn, _merge_rows(br4, proj), [gb3], [(D_MODEL, BF16)], [], R=proj.shape[0], name=name, cs=_EW_CS)[0]


def _merge_bwd(br4, proj, gb3, dm, name):
    def fn(rp, cp):
        b = cp[0]
        shp = rp[0].shape
        bs = [jnp.broadcast_to(b[k:k + 1], shp) for k in range(3)]
        _, pull = jax.vjp(_merge_fn, *rp[:7], *bs)
        g = pull(rp[7])
        return list(g[:7]), list(g[7:])

    rows = _merge_rows(br4, proj) + [(dm, 0, D_MODEL)]
    outs = _rowwise(fn, rows, [gb3], [(D_MODEL, BF16)] * 7, [D_MODEL] * 3, R=proj.shape[0], name=name, tm=128, cs=_EW_CS)
    return outs[:4], outs[4:7], jnp.concatenate(outs[7:], axis=0)


def _swiglu(g, u):
    return jax.nn.silu(g) * u


def _act_fwd(gu, name):
    def fn(rp, cp):
        return [_swiglu(*rp)], []

    return _rowwise(fn, [(gu, 0, FFN_H), (gu, FFN_H, FFN_H)], [], [(FFN_H, BF16)], [], R=gu.shape[0], name=name, cs=_EW_CS)[0]


def _act_bwd(gu, dact, name):
    def fn(rp, cp):
        _, pull = jax.vjp(_swiglu, rp[0], rp[1])
        return list(pull(rp[2])), []

    return _rowwise(fn, [(gu, 0, FFN_H), (gu, FFN_H, FFN_H), (dact, 0, FFN_H)], [], [(FFN_H, BF16)] * 2, [],
                    R=gu.shape[0], name=name, tm=128, cs=_EW_CS)


def _adamw_fn(w, g, m, v):
    m = ADAM_B1 * m + (1.0 - ADAM_B1) * g
    v = ADAM_B2 * v + (1.0 - ADAM_B2) * jnp.square(g)
    m_hat = m / (1.0 - ADAM_B1 ** ADAM_STEP)
    v_hat = v / (1.0 - ADAM_B2 ** ADAM_STEP)
    delta = -ADAM_LR * (m_hat / (jnp.sqrt(v_hat) + ADAM_EPS) + ADAM_WD * w)
    return delta, m, v


def _adamw(w, g, m, v, name):
    R, C = w.shape

    def fn(rp, cp):
        return list(_adamw_fn(*rp)), []

    cs = _pick(C, (512, 256, 128))
    tm = _pick(R, (256, 128, 64, 32, 16, 8))
    return _rowwise(fn, [(w, 0, C), (g, 0, C), (m, 0, C), (v, 0, C)], [], [(C, F32)] * 3, [], R=R, name=name,
                    tm=tm, rs=8, cs=cs)


def _my_place():
    return lax.axis_index("x"), lax.axis_index("y"), lax.axis_index("c")


def _other_chips(x, y):
    return [(1 - x, y), (x, 1 - y), (1 - x, 1 - y)]


_ANY = pl.BlockSpec(memory_space=pl.ANY)


def _rcopy(src, dst, ssem, rsem, to):
    return pltpu.make_async_remote_copy(src_ref=src, dst_ref=dst, send_sem=ssem, recv_sem=rsem, device_id=to,
                                        device_id_type=MESH)


def _gather_weights(shards, axes, name):
    n = len(shards)
    full_shapes = []
    for s, ax in zip(shards, axes):
        shp = list(s.shape)
        shp[ax] *= N_CHIPS
        full_shapes.append(jax.ShapeDtypeStruct(tuple(shp), s.dtype))

    def body(*refs):
        srcs, fulls = refs[:n], refs[n:2 * n]
        send_sems, recv_sems, local_sems = refs[2 * n:]
        x, y, c = _my_place()
        sib = (x, y, 1 - c)
        chips = _other_chips(x, y)

        def block(t, chip, half):
            lead = srcs[t].shape[0] // 2
            size = srcs[t].shape[axes[t]]
            j = 2 * chip[0] + chip[1]
            idx = [pl.ds(half * lead, lead), slice(None), slice(None)]
            idx[axes[t]] = pl.ds(j * size, size)
            return fulls[t].at[tuple(idx)]

        def whole(t, chip):
            size = srcs[t].shape[axes[t]]
            j = 2 * chip[0] + chip[1]
            idx = [slice(None), slice(None), slice(None)]
            idx[axes[t]] = pl.ds(j * size, size)
            return fulls[t].at[tuple(idx)]

        def src_half(t, half):
            lead = srcs[t].shape[0] // 2
            return srcs[t].at[pl.ds(half * lead, lead)]

        locals_ = [pltpu.make_async_copy(srcs[t], whole(t, (x, y)), local_sems.at[t]) for t in range(n)]
        for cp in locals_:
            cp.start()
        sends = []
        for t in range(n):
            for r, chip in enumerate(chips):
                k = 3 * t + r
                cp = _rcopy(src_half(t, c), block(t, (x, y), c), send_sems.at[k], recv_sems.at[k], (*chip, c))
                cp.start()
                sends.append(cp)
        passed = []
        for t in range(n):
            for r, chip in enumerate(chips):
                k = 3 * t + r
                landed = block(t, chip, c)
                _rcopy(landed, landed, send_sems.at[k], recv_sems.at[k], (*chip, c)).wait_recv()
                cp = _rcopy(landed, landed, send_sems.at[3 * n + k], recv_sems.at[3 * n + k], sib)
                cp.start()
                passed.append(cp)
        for t in range(n):
            for r, chip in enumerate(chips):
                k = 3 * n + 3 * t + r
                theirs = block(t, chip, 1 - c)
                _rcopy(theirs, theirs, send_sems.at[k], recv_sems.at[k], sib).wait_recv()
        for cp in sends + passed:
            cp.wait_send()
        for cp in locals_:
            cp.wait()

    return pl.pallas_call(
        body, name=name, out_shape=tuple(full_shapes), in_specs=[_ANY] * n, out_specs=tuple([_ANY] * n),
        scratch_shapes=[pltpu.SemaphoreType.DMA((6 * n,)), pltpu.SemaphoreType.DMA((6 * n,)), pltpu.SemaphoreType.DMA((n,))],
    )(*shards)


def _half_idx(shape, axis, half):
    size = shape[axis] // 2
    idx = [slice(None), slice(None)]
    idx[axis] = pl.ds(half * size, size)
    return tuple(idx)


def _pair_exchange(grads, half_axes, name):
    n = len(grads)
    out_shapes = []
    for g, ax in zip(grads, half_axes):
        shp = list(g.shape)
        shp[ax] //= 2
        out_shapes.append(jax.ShapeDtypeStruct(tuple(shp), g.dtype))

    def body(*refs):
        srcs, outs = refs[:n], refs[n:2 * n]
        send_sems, recv_sems = refs[2 * n:]
        x, y, c = _my_place()
        sib = (x, y, 1 - c)
        cps = []
        for t in range(n):
            cp = _rcopy(srcs[t].at[_half_idx(srcs[t].shape, half_axes[t], 1 - c)], outs[t], send_sems.at[t], recv_sems.at[t], sib)
            cp.start()
            cps.append(cp)
        for cp in cps:
            cp.wait()

    return pl.pallas_call(
        body, name=name, out_shape=tuple(out_shapes), in_specs=[_ANY] * n, out_specs=tuple([_ANY] * n),
        scratch_shapes=[pltpu.SemaphoreType.DMA((n,)), pltpu.SemaphoreType.DMA((n,))],
    )(*grads)


def _pair_sum(g, recv, half_axis, cidx, name):
    K, N = recv.shape
    tm = _pick(K, (256, 128, 64, 32, 16))
    tn = _pick(N, (1024, 1408, 512, 256, 128))
    nbr, nbc = K // tm, N // tn
    if half_axis == 0:
        gmap = lambda i, j, c: (c[0] * nbr + i, j)
    else:
        gmap = lambda i, j, c: (i, c[0] * nbc + j)

    def body(c_ref, g_ref, r_ref, of_ref, ob_ref):
        s = g_ref[...] + r_ref[...]
        of_ref[...] = s
        ob_ref[...] = s.astype(BF16)

    omap = lambda i, j, c: (i, j)
    return pl.pallas_call(
        body, name=name,
        out_shape=(jax.ShapeDtypeStruct((K, N), F32), jax.ShapeDtypeStruct((K, N), BF16)),
        grid_spec=pltpu.PrefetchScalarGridSpec(
            num_scalar_prefetch=1, grid=(nbr, nbc),
            in_specs=[pl.BlockSpec((tm, tn), gmap), pl.BlockSpec((tm, tn), omap)],
            out_specs=(pl.BlockSpec((tm, tn), omap), pl.BlockSpec((tm, tn), omap))),
        compiler_params=_cparams(("parallel", "parallel")),
    )(cidx, g, recv)


def _shard_idx(shape, axis, j):
    size = shape[axis] // N_CHIPS
    idx = [slice(None), slice(None)]
    idx[axis] = pl.ds(j * size, size)
    return tuple(idx)


def _chip_scatter(parts, shard_axes, name):
    n = len(parts)
    out_shapes = []
    for p, ax in zip(parts, shard_axes):
        shp = list(p.shape)
        shp[ax] //= N_CHIPS
        out_shapes.append(jax.ShapeDtypeStruct((3,) + tuple(shp), p.dtype))

    def body(*refs):
        srcs, outs = refs[:n], refs[n:2 * n]
        send_sems, recv_sems = refs[2 * n:]
        x, y, c = _my_place()
        cps = []
        for t in range(n):
            for r, chip in enumerate(_other_chips(x, y)):
                k = 3 * t + r
                j = 2 * chip[0] + chip[1]
                cp = _rcopy(srcs[t].at[_shard_idx(srcs[t].shape, shard_axes[t], j)], outs[t].at[r],
                            send_sems.at[k], recv_sems.at[k], (*chip, c))
                cp.start()
                cps.append(cp)
        for cp in cps:
            cp.wait()

    return pl.pallas_call(
        body, name=name, out_shape=tuple(out_shapes), in_specs=[_ANY] * n, out_specs=tuple([_ANY] * n),
        scratch_shapes=[pltpu.SemaphoreType.DMA((3 * n,)), pltpu.SemaphoreType.DMA((3 * n,))],
    )(*parts)


def _shard_sum(pf, recv, shard_axis, jidx, name):
    _, K, N = recv.shape
    tm = _pick(K, (256, 128, 64, 32, 16))
    tn = _pick(N, (1024, 1408, 512, 256, 128))
    nbr, nbc = K // tm, N // tn
    if shard_axis == 0:
        pmap = lambda i, j, s: (s[0] * nbr + i, j)
    else:
        pmap = lambda i, j, s: (i, s[0] * nbc + j)

    def body(j_ref, p_ref, r_ref, o_ref):
        o_ref[...] = ((p_ref[...] + r_ref[0].astype(F32)) + r_ref[1].astype(F32)) + r_ref[2].astype(F32)

    return pl.pallas_call(
        body, name=name, out_shape=jax.ShapeDtypeStruct((K, N), F32),
        grid_spec=pltpu.PrefetchScalarGridSpec(
            num_scalar_prefetch=1, grid=(nbr, nbc),
            in_specs=[pl.BlockSpec((tm, tn), pmap), pl.BlockSpec((3, tm, tn), lambda i, j, s: (0, i, j))],
            out_specs=pl.BlockSpec((tm, tn), lambda i, j, s: (i, j))),
        compiler_params=_cparams(("parallel", "parallel")),
    )(jidx, pf, recv)


def _pair_join(halves, half_axes, name):
    n = len(halves)
    n_l = len(halves[0])
    out_shapes = []
    for hs, ax in zip(halves, half_axes):
        shp = list(hs[0].shape)
        shp[ax] *= 2
        out_shapes.append(jax.ShapeDtypeStruct((n_l,) + tuple(shp), hs[0].dtype))
    flat = [h for hs in halves for h in hs]

    def body(*refs):
        srcs, outs = refs[:n * n_l], refs[n * n_l:n * n_l + n]
        send_sems, recv_sems, local_sems = refs[n * n_l + n:]
        x, y, c = _my_place()
        sib = (x, y, 1 - c)
        cps, lcs = [], []
        for t in range(n):
            for l in range(n_l):
                k = t * n_l + l
                src = srcs[k]
                shape2 = outs[t].shape[1:]
                mine = outs[t].at[(l,) + _half_idx(shape2, half_axes[t], c)]
                lc = pltpu.make_async_copy(src, mine, local_sems.at[k])
                lc.start()
                lcs.append(lc)
                cp = _rcopy(src, mine, send_sems.at[k], recv_sems.at[k], sib)
                cp.start()
                cps.append(cp)
        for t in range(n):
            for l in range(n_l):
                k = t * n_l + l
                shape2 = outs[t].shape[1:]
                theirs = outs[t].at[(l,) + _half_idx(shape2, half_axes[t], 1 - c)]
                _rcopy(srcs[k], theirs, send_sems.at[k], recv_sems.at[k], sib).wait_recv()
        for cp in cps:
            cp.wait_send()
        for lc in lcs:
            lc.wait()

    m = n * n_l
    return pl.pallas_call(
        body, name=name, out_shape=tuple(out_shapes), in_specs=[_ANY] * m, out_specs=tuple([_ANY] * n),
        scratch_shapes=[pltpu.SemaphoreType.DMA((m,)), pltpu.SemaphoreType.DMA((m,)), pltpu.SemaphoreType.DMA((m,))],
    )(*flat)


_N_DEV = 8


def _allreduce_small(flat, name):
    _, R, _ = flat.shape

    def body(in_ref, out_ref, stage, send1, recv1, send2, recv2):
        x, y, c = _my_place()
        me = 4 * x + 2 * y + c
        places = [(px, py, pc) for px in range(2) for py in range(2) for pc in range(2)]
        def peer(r):
            return (x ^ (r >> 2), y ^ ((r >> 1) & 1), c ^ (r & 1))

        def peer_id(r):
            p = peer(r)
            return 4 * p[0] + 2 * p[1] + p[2]

        stage[0] = in_ref[me]
        cps = []
        for r in range(1, _N_DEV):
            cp = _rcopy(in_ref.at[peer_id(r)], stage.at[r], send1.at[r], recv1.at[r], peer(r))
            cp.start()
            cps.append(cp)
        for cp in cps:
            cp.wait()
        tot = jnp.zeros((R, 128), F32)
        for d in range(_N_DEV):
            tot = tot + stage[me ^ d]
        out_ref[me] = tot
        cps = []
        for r in range(1, _N_DEV):
            cp = _rcopy(out_ref.at[me], out_ref.at[me], send2.at[r], recv2.at[r], peer(r))
            cp.start()
            cps.append(cp)
        for r in range(1, _N_DEV):
            _rcopy(out_ref.at[peer_id(r)], out_ref.at[peer_id(r)], send2.at[r], recv2.at[r], peer(r)).wait_recv()
        for cp in cps:
            cp.wait_send()

    vm = pl.BlockSpec(memory_space=pltpu.VMEM)
    return pl.pallas_call(
        body, name=name, out_shape=jax.ShapeDtypeStruct(flat.shape, F32), in_specs=[vm], out_specs=vm,
        scratch_shapes=[pltpu.VMEM(flat.shape, F32)] + [pltpu.SemaphoreType.DMA((_N_DEV,))] * 4,
        compiler_params=_cparams(),
    )(flat)


_BIG = ("w_in", "ssm_w_glu", "w_branch", "w_out", "w_ffn_gate", "w_ffn_up", "w_ffn_down")
_BIG_SHARD_AXIS = {"w_in": 1, "ssm_w_glu": 1, "w_branch": 1, "w_out": 0, "w_ffn_gate": 1, "w_ffn_up": 1, "w_ffn_down": 0}
_SMALL = ("norm_mix_g", "gate_bias", "lru_conv_w", "lru_conv_b", "lru_wa", "lru_ba", "lru_wx", "lru_bx", "lru_lambda",
          "attn_rel_bias", "ssm_a_re", "ssm_a_im", "ssm_b_re", "ssm_b_im", "ssm_c_re", "ssm_c_im", "ssm_d",
          "ssm_log_step", "norm_ffn_g", "norm_final_g")
_SMALL_SHARDED = {"gate_bias": 2, "lru_conv_w": 2}
_WEIGHTS = ("norm_mix_g", "w_in", "gate_bias", "lru_conv_w", "lru_conv_b", "lru_wa", "lru_ba", "lru_wx", "lru_bx",
            "lru_lambda", "attn_rel_bias", "ssm_a_re", "ssm_a_im", "ssm_b_re", "ssm_b_im", "ssm_c_re", "ssm_c_im",
            "ssm_d", "ssm_log_step", "ssm_w_glu", "w_branch", "w_out", "norm_ffn_g", "w_ffn_gate", "w_ffn_up",
            "w_ffn_down", "norm_final_g")


def _layer_fwd(l, x, W, sm):
    T = x.shape[0]
    nm = lambda s: f"{s}"
    h1 = _rms_fwd(x, sm["norm_mix_g"][l][None, :], nm("rms_fwd"))
    proj = _mm(h1, W["w_in"], M=T, N=IN_W, K=D_MODEL, b_lead=l, name=nm("mm_in"))
    wax, bax = sm["lru_prep"][l]
    cw, cb, lam = sm["lru_conv_w"][l], sm["lru_conv_b"][l][None, :], sm["lru_lambda"][l][None, :]
    y_a, hst = _lru_fwd(proj, cw, cb, wax, bax, lam, nm("lru_fwd"))
    bias = sm["attn_bias"][l]
    y_b = _attn_fwd(proj, bias, nm("attn_fwd"))
    tgt, bret, bimt, cre, cim, lr, li, dflat = sm["s5_prep"][l]
    u = _to_chunks(lax.slice_in_dim(proj, 5 * MIX_W, 6 * MIX_W, axis=1))
    sin_re, sin_im = _s5_in(u, bret, bimt, nm("s5_in"))
    sp_re, sp_im = _s5_scan(sin_re.transpose(1, 0, 2), sin_im.transpose(1, 0, 2), lr, li, nm("s5_scan"))
    sp_re, sp_im = sp_re.transpose(1, 0, 2), sp_im.transpose(1, 0, 2)
    pre, ycf = _s5_out(u, sp_re, sp_im, tgt, cre, cim, dflat, nm("s5_out"))
    y_c = _from_chunks(ycf)
    brs = []
    for k, yk in enumerate((y_a, y_b, y_c)):
        brs.append(_mm(yk, W["w_branch"], M=T, N=D_MODEL, K=MIX_W, b_lead=3 * l + k, name=nm("mm_branch")))
    brs.append(_mm(y_c, W["ssm_w_glu"], M=T, N=D_MODEL, K=MIX_W, b_lead=l, name=nm("mm_branch")))
    br4 = jnp.concatenate(brs, axis=1)
    gb3 = sm["gate_bias"][l]
    merged = _merge_fwd(br4, proj, gb3, nm("merge_fwd"))
    x1 = _mm(merged, W["w_out"], M=T, N=D_MODEL, K=D_MODEL, b_lead=l, res=x, name=nm("mm_out"))
    h2 = _rms_fwd(x1, sm["norm_ffn_g"][l][None, :], nm("rms_fwd"))
    gpre = _mm(h2, W["w_ffn_gate"], M=T, N=FFN_H, K=D_MODEL, b_lead=l, name=nm("mm_ffn_up"))
    upre = _mm(h2, W["w_ffn_up"], M=T, N=FFN_H, K=D_MODEL, b_lead=l, name=nm("mm_ffn_up"))
    gu = jnp.concatenate([gpre, upre], axis=1)
    act = _act_fwd(gu, nm("act_fwd"))
    x2 = _mm(act, W["w_ffn_down"], M=T, N=D_MODEL, K=FFN_H, b_lead=l, res=x1, name=nm("mm_down"))
    saved = dict(x=x, h1=h1, proj=proj, hst=hst, y_a=y_a, y_b=y_b, y_c=y_c, u=u, sp_re=sp_re, sp_im=sp_im, pre=pre,
                 br4=br4, merged=merged, x1=x1, h2=h2, gu=gu, act=act)
    return x2, saved


def _layer_bwd(l, dx2, sv, W, sm):
    T = dx2.shape[0]
    nm = lambda s: f"{s}"
    big, small = {}, {}
    dxb = dx2.astype(BF16)
    big["w_ffn_down"] = _mm(sv["act"], dxb, M=FFN_H, N=D_MODEL, K=T, ta=True, name=nm("mm_dw_down"))
    dact = _mm(dxb, W["w_ffn_down"], M=T, N=FFN_H, K=D_MODEL, tb=True, b_lead=l, name=nm("mm_dact"))
    dg, du = _act_bwd(sv["gu"], dact, nm("act_bwd"))
    big["w_ffn_gate"] = _mm(sv["h2"], dg, M=D_MODEL, N=FFN_H, K=T, ta=True, name=nm("mm_dw_up"))
    big["w_ffn_up"] = _mm(sv["h2"], du, M=D_MODEL, N=FFN_H, K=T, ta=True, name=nm("mm_dw_up"))
    dh2 = _mm(dg, W["w_ffn_gate"], M=T, N=D_MODEL, K=FFN_H, tb=True, b_lead=l, name=nm("mm_dh2"))
    dh2 = _mm(du, W["w_ffn_up"], M=T, N=D_MODEL, K=FFN_H, tb=True, b_lead=l, res=dh2, name=nm("mm_dh2r"))
    dx1, dgn = _rms_bwd(sv["x1"], sm["norm_ffn_g"][l][None, :], dh2, dx2, nm("rms_bwd"))
    small["norm_ffn_g"] = dgn[0]
    dx1b = dx1.astype(BF16)
    big["w_out"] = _mm(sv["merged"], dx1b, M=D_MODEL, N=D_MODEL, K=T, ta=True, name=nm("mm_dw_out"))
    dm = _mm(dx1b, W["w_out"], M=T, N=D_MODEL, K=D_MODEL, tb=True, b_lead=l, name=nm("mm_dmerged"))
    dbr, dgates, dgb = _merge_bwd(sv["br4"], sv["proj"], sm["gate_bias"][l], dm, nm("merge_bwd"))
    small["gate_bias"] = dgb
    ys = (sv["y_a"], sv["y_b"], sv["y_c"])
    big["w_branch"] = [_mm(ys[k], dbr[k], M=MIX_W, N=D_MODEL, K=T, ta=True, name=nm("mm_dw_branch")) for k in range(3)]
    big["ssm_w_glu"] = _mm(sv["y_c"], dbr[3], M=MIX_W, N=D_MODEL, K=T, ta=True, name=nm("mm_dw_branch"))
    dya = _mm(dbr[0], W["w_branch"], M=T, N=MIX_W, K=D_MODEL, tb=True, b_lead=3 * l, name=nm("mm_dy"))
    dyb = _mm(dbr[1], W["w_branch"], M=T, N=MIX_W, K=D_MODEL, tb=True, b_lead=3 * l + 1, name=nm("mm_dy"))
    dyc = _mm(dbr[2], W["w_branch"], M=T, N=MIX_W, K=D_MODEL, tb=True, b_lead=3 * l + 2, name=nm("mm_dy"))
    dyc = _mm(dbr[3], W["ssm_w_glu"], M=T, N=MIX_W, K=D_MODEL, tb=True, b_lead=l, res=dyc, name=nm("mm_dyr"))
    tgt, bret, bimt, cre, cim, lr, li, dflat = sm["s5_prep"][l]
    dpre, dsp_re, dsp_im, d_tgt, d_cre, d_cim, d_dflat = _s5_bwd_out(
        _to_chunks(dyc), sv["pre"], sv["u"], sv["sp_re"], sv["sp_im"], cre, cim, nm("s5_bwd_out"))
    tr = lambda a: a.transpose(1, 0, 2)
    dsin_re, dsin_im, d_lr, d_li = _s5_rscan(tr(dsp_re), tr(dsp_im), tr(sv["sp_re"]), tr(sv["sp_im"]), lr, li, nm("s5_rscan"))
    du_f, d_bret, d_bimt = _s5_bwd_in(dpre, dsin_re.transpose(1, 0, 2), dsin_im.transpose(1, 0, 2), sv["u"], tgt, bret, bimt,
                                      dflat, nm("s5_bwd_in"))
    d_u = _from_chunks(du_f)
    small["s5_tables"] = (d_tgt, d_bret, d_bimt, d_cre, d_cim, d_lr, d_li, d_dflat)
    dq, dk, dv, dbias = _attn_bwd(sv["proj"], sm["attn_bias"][l], dyb, nm("attn_bwd"))
    small["attn_bias"] = dbias
    wax, bax = sm["lru_prep"][l]
    cw, cb, lam = sm["lru_conv_w"][l], sm["lru_conv_b"][l][None, :], sm["lru_lambda"][l][None, :]
    d_lx, d_lg, d_cw, d_cb, d_wax, d_bax, d_lam = _lru_bwd(sv["proj"], sv["hst"], dya, cw, cb, wax, bax, lam, nm("lru_bwd"))
    small["lru_conv_w"], small["lru_conv_b"], small["lru_lambda"] = d_cw, d_cb[0], d_lam[0]
    small["lru_tables"] = (d_wax, d_bax)
    dproj = jnp.concatenate([d_lx, d_lg, dq, dk, dv, d_u] + list(dgates), axis=1)
    big["w_in"] = _mm(sv["h1"], dproj, M=D_MODEL, N=IN_W, K=T, ta=True, name=nm("mm_dw_in"))
    dh1 = _mm(dproj, W["w_in"], M=T, N=D_MODEL, K=IN_W, tb=True, b_lead=l, name=nm("mm_dh1"))
    dx, dgn = _rms_bwd(sv["x"], sm["norm_mix_g"][l][None, :], dh1, dx1, nm("rms_bwd"))
    small["norm_mix_g"] = dgn[0]
    return dx, big, small


def _reduce_big(l, big, cidx, jidx):
    names, grads, shard_axes = [], [], []
    for n in _BIG:
        gs = big[n] if isinstance(big[n], list) else [big[n]]
        for k, g in enumerate(gs):
            names.append((n, k))
            grads.append(g)
            shard_axes.append(_BIG_SHARD_AXIS[n])
    half_axes = [1 - a for a in shard_axes]
    recv = _pair_exchange(grads, half_axes, "pair_exchange")
    pfs, pbs = [], []
    for g, r, ha in zip(grads, recv, half_axes):
        pf, pb = _pair_sum(g, r, ha, cidx, "pair_sum")
        pfs.append(pf)
        pbs.append(pb)
    got = _chip_scatter(pbs, shard_axes, "chip_scatter")
    halves = [_shard_sum(pf, r, sa, jidx, "shard_sum") for pf, r, sa in zip(pfs, got, shard_axes)]
    return names, halves, half_axes


def _local_step(xs, tgt, W, sm, on_big):
    sm = dict(sm)
    depth = sm["norm_mix_g"].shape[0]
    lru_prep_vjps, attn_vjps, s5_vjps = [], [], []
    sm["lru_prep"], sm["attn_bias"], sm["s5_prep"] = [], [], []
    for l in range(depth):
        o, f = jax.vjp(_lru_prep, sm["lru_wa"][l], sm["lru_wx"][l], sm["lru_ba"][l], sm["lru_bx"][l])
        sm["lru_prep"].append((o[0].astype(BF16), o[1]))
        lru_prep_vjps.append(f)
        o, f = jax.vjp(_attn_bias, sm["attn_rel_bias"][l])
        sm["attn_bias"].append(o)
        attn_vjps.append(f)
        o, f = jax.vjp(_s5_prep, *[sm[n][l] for n in ("ssm_a_re", "ssm_a_im", "ssm_b_re", "ssm_b_im", "ssm_c_re", "ssm_c_im",
                                                       "ssm_d", "ssm_log_step")])
        sm["s5_prep"].append(o)
        s5_vjps.append(f)

    saved = []
    for l in range(depth):
        xs, sv = _layer_fwd(l, xs, W, sm)
        saved.append(sv)
    loss_part, dx, dgf = _final_loss(xs, sm["norm_final_g"][None, :], tgt, "final_loss")

    small_g = {n: [None] * depth for n in _SMALL if n != "norm_final_g"}
    for l in reversed(range(depth)):
        dx, big, small = _layer_bwd(l, dx, saved[l], W, sm)
        on_big(l, big)
        d_wa, d_wx, d_ba, d_bx = lru_prep_vjps[l](small["lru_tables"])
        (d_rel,) = attn_vjps[l](small["attn_bias"])
        d_s5 = s5_vjps[l](small["s5_tables"])
        vals = dict(norm_mix_g=small["norm_mix_g"], gate_bias=small["gate_bias"], lru_conv_w=small["lru_conv_w"],
                    lru_conv_b=small["lru_conv_b"], lru_wa=d_wa, lru_ba=d_ba, lru_wx=d_wx, lru_bx=d_bx,
                    lru_lambda=small["lru_lambda"], attn_rel_bias=d_rel, ssm_a_re=d_s5[0], ssm_a_im=d_s5[1],
                    ssm_b_re=d_s5[2], ssm_b_im=d_s5[3], ssm_c_re=d_s5[4], ssm_c_im=d_s5[5], ssm_d=d_s5[6],
                    ssm_log_step=d_s5[7], norm_ffn_g=small["norm_ffn_g"])
        for n, val in vals.items():
            small_g[n][l] = val
    small_tree = {n: jnp.stack(small_g[n]) for n in small_g}
    small_tree["norm_final_g"] = dgf[0]
    return loss_part, dx, small_tree


def _pack_small(tree, names):
    flat = jnp.concatenate([tree[n].reshape(-1) for n in names])
    per = -(-flat.shape[0] // (_N_DEV * 128 * 8)) * (128 * 8)
    flat = jnp.pad(flat, (0, _N_DEV * per - flat.shape[0]))
    return flat.reshape(_N_DEV, per // 128, 128)


def _unpack_small(flat, like, names):
    flat = flat.reshape(-1)
    out, off = {}, 0
    for n in names:
        size = math.prod(like[n].shape)
        out[n] = flat[off:off + size].reshape(like[n].shape)
        off += size
    return out


def kernel(x, norm_mix_g, w_in, gate_bias, lru_conv_w, lru_conv_b, lru_wa, lru_ba, lru_wx, lru_bx, lru_lambda, attn_rel_bias, ssm_a_re, ssm_a_im, ssm_b_re, ssm_b_im, ssm_c_re, ssm_c_im, ssm_d, ssm_log_step, ssm_w_glu, w_branch, w_out, norm_ffn_g, w_ffn_gate, w_ffn_up, w_ffn_down, norm_final_g, loss_target, m_norm_mix_g, m_w_in, m_gate_bias, m_lru_conv_w, m_lru_conv_b, m_lru_wa, m_lru_ba, m_lru_wx, m_lru_bx, m_lru_lambda, m_attn_rel_bias, m_ssm_a_re, m_ssm_a_im, m_ssm_b_re, m_ssm_b_im, m_ssm_c_re, m_ssm_c_im, m_ssm_d, m_ssm_log_step, m_ssm_w_glu, m_w_branch, m_w_out, m_norm_ffn_g, m_w_ffn_gate, m_w_ffn_up, m_w_ffn_down, m_norm_final_g, v_norm_mix_g, v_w_in, v_gate_bias, v_lru_conv_w, v_lru_conv_b, v_lru_wa, v_lru_ba, v_lru_wx, v_lru_bx, v_lru_lambda, v_attn_rel_bias, v_ssm_a_re, v_ssm_a_im, v_ssm_b_re, v_ssm_b_im, v_ssm_c_re, v_ssm_c_im, v_ssm_d, v_ssm_log_step, v_ssm_w_glu, v_w_branch, v_w_out, v_norm_ffn_g, v_w_ffn_gate, v_w_ffn_up, v_w_ffn_down, v_norm_final_g):
    args = dict(locals())
    w = {n: args[n] for n in _WEIGHTS}
    m = {n: args["m_" + n] for n in _WEIGHTS}
    v = {n: args["v_" + n] for n in _WEIGHTS}
    depth = w_in.shape[0]
    xc, yc, cc = _my_place()
    cidx = jnp.reshape(cc, (1,)).astype(jnp.int32)
    jidx = jnp.reshape(2 * xc + yc, (1,)).astype(jnp.int32)

    shards = [w[n].astype(BF16) for n in _BIG]
    shards[2] = shards[2].reshape(depth * N_BRANCH, MIX_W, -1)
    axes = [_BIG_SHARD_AXIS[n] + 1 for n in _BIG]
    shards += [w[n] for n in _SMALL_SHARDED]
    axes += [_SMALL_SHARDED[n] for n in _SMALL_SHARDED]
    gathered = _gather_weights(shards, axes, "gather_weights")
    W = dict(zip(_BIG, gathered))
    sm_full = dict(zip(_SMALL_SHARDED, gathered[len(_BIG):]))
    sm = {n: w[n] for n in _SMALL if n not in _SMALL_SHARDED}
    sm.update(sm_full)

    halves_by_tensor, half_axes_by_tensor, tensor_names = {}, {}, []

    def on_big(l, big):
        names, halves, half_axes = _reduce_big(l, big, cidx, jidx)
        tensor_names[:] = names
        for nk, hv, ha in zip(names, halves, half_axes):
            halves_by_tensor.setdefault(nk, [None] * depth)[l] = hv
            half_axes_by_tensor[nk] = ha

    loss_part, dx, small_tree = _local_step(x[0], loss_target[0], W, sm, on_big)
    loss = lax.psum(loss_part, ("x", "y", "c"))
    grad_x = dx[None]

    joined = _pair_join([halves_by_tensor[nk] for nk in tensor_names], [half_axes_by_tensor[nk] for nk in tensor_names],
                        "pair_join")
    jd = dict(zip(tensor_names, joined))
    grads = {}
    for n in _BIG:
        if n == "w_branch":
            grads[n] = jnp.stack([jd[(n, k)] for k in range(N_BRANCH)], axis=1)
        else:
            grads[n] = jd[(n, 0)]
    like = {n: (sm_full[n] if n in _SMALL_SHARDED else w[n]) for n in _SMALL}
    red = _unpack_small(_allreduce_small(_pack_small(small_tree, _SMALL), "allreduce_small"), like, _SMALL)
    for n in _SMALL:
        if n in _SMALL_SHARDED:
            size = w[n].shape[2]
            grads[n] = lax.dynamic_slice_in_dim(red[n], (2 * xc + yc) * size, size, axis=2)
        else:
            grads[n] = red[n]

    delta, new_m, new_v = {}, {}, {}
    for n in _BIG:
        shp = w[n].shape
        two = lambda a: a.reshape(-1, shp[-1])
        d_, m_, v_ = _adamw(two(w[n]), two(grads[n]), two(m[n]), two(v[n]), "adamw")
        delta[n], new_m[n], new_v[n] = d_.reshape(shp), m_.reshape(shp), v_.reshape(shp)
    pk = lambda tree: _pack_small(tree, _SMALL).reshape(-1, 128)
    d_, m_, v_ = _adamw(pk(w), pk(grads), pk(m), pk(v), "adamw_small")
    like_local = {n: w[n] for n in _SMALL}
    for tree, flat in ((delta, d_), (new_m, m_), (new_v, v_)):
        tree.update(_unpack_small(flat, like_local, _SMALL))
    return (loss, grad_x, *[grads[n] for n in _WEIGHTS], *[delta[n] for n in _WEIGHTS], *[new_m[n] for n in _WEIGHTS],
            *[new_v[n] for n in _WEIGHTS])
```

```python
import functools
import math

import jax
import jax.numpy as jnp
from jax import lax
from jax.experimental import pallas as pl
from jax.experimental.pallas import tpu as pltpu

F32 = jnp.float32
BF16 = jnp.bfloat16

D_MODEL = 2048
MIX_W = 1024
N_BRANCH = 3
LRU_BLOCKS = 16
LRU_BW = 64
CONV_W = 4
LRU_C = 8.0
CHUNK = 64
ATT_HEADS = 8
ATT_HD = 128
ATT_LEFT = 8
ATT_BAND = (ATT_LEFT + 1) * CHUNK
MAX_REL = 128
N_REL = 2 * MAX_REL + 1
SSM_G = 64
SSM_H = 16
SSM_P = 64
FFN_H = 5632
IN_W = 6 * MIX_W + N_BRANCH * D_MODEL
NORM_EPS = 1e-6
MASK_VALUE = -1e30
ADAM_LR, ADAM_B1, ADAM_B2, ADAM_EPS, ADAM_WD, ADAM_STEP = 0.001, 0.9, 0.999, 1e-08, 0.01, 10

S5_L = 16
S5_LW = S5_L * SSM_H
N_CHIPS = 4
V7X_VMEM_LIMIT = 56 * 1024 * 1024
HI = lax.Precision.HIGHEST
MESH = pl.DeviceIdType.MESH


def _cparams(sem=None):
    return pltpu.CompilerParams(dimension_semantics=sem, vmem_limit_bytes=V7X_VMEM_LIMIT)


def _pick(n, prefs):
    for p in prefs:
        if n % p == 0:
            return p
    return n


def _mm(a, b, *, M, N, K, name, ta=False, tb=False, a_lead=None, b_lead=None, a_off=(0, 0), b_off=(0, 0),
        out_dtype=F32, res=None, tm=None, tn=None, tk=None):
    tm = tm or _pick(M, (1024, 512, 256, 128, 64, 32, 16, 8))
    tn = tn or _pick(N, (1024, 1408, 512, 256, 128))
    tk = tk or _pick(K, (1024, 1408, 512, 256, 128))
    nk = K // tk

    def spec(blk, lead, off, order):
        r0, c0 = off[0] // blk[0], off[1] // blk[1]
        assert off[0] % blk[0] == 0 and off[1] % blk[1] == 0
        if lead is None:
            return pl.BlockSpec(blk, lambda i, j, k: (r0 + order(i, j, k)[0], c0 + order(i, j, k)[1]))
        return pl.BlockSpec((None,) + blk, lambda i, j, k: (lead, r0 + order(i, j, k)[0], c0 + order(i, j, k)[1]))

    a_spec = spec((tk, tm), a_lead, a_off, lambda i, j, k: (k, i)) if ta else spec((tm, tk), a_lead, a_off, lambda i, j, k: (i, k))
    b_spec = spec((tn, tk), b_lead, b_off, lambda i, j, k: (j, k)) if tb else spec((tk, tn), b_lead, b_off, lambda i, j, k: (k, j))
    dims = (((0 if ta else 1,), (1 if tb else 0,)), ((), ()))
    in_specs = [a_spec, b_spec]
    args = [a, b]
    if res is not None:
        in_specs.append(pl.BlockSpec((tm, tn), lambda i, j, k: (i, j)))
        args.append(res)

    def body(*refs):
        if res is not None:
            a_ref, b_ref, r_ref, o_ref, acc = refs
        else:
            a_ref, b_ref, o_ref, acc = refs
            r_ref = None
        k = pl.program_id(2)
        part = lax.dot_general(a_ref[...], b_ref[...], dims, preferred_element_type=F32)

        @pl.when(k == 0)
        def _():
            acc[...] = part

        @pl.when(k > 0)
        def _():
            acc[...] += part

        @pl.when(k == nk - 1)
        def _():
            r = acc[...]
            if r_ref is not None:
                r = r + r_ref[...]
            o_ref[...] = r.astype(out_dtype)

    return pl.pallas_call(
        body, name=name, out_shape=jax.ShapeDtypeStruct((M, N), out_dtype),
        grid=(M // tm, N // tn, nk), in_specs=in_specs, out_specs=pl.BlockSpec((tm, tn), lambda i, j, k: (i, j)),
        scratch_shapes=[pltpu.VMEM((tm, tn), F32)],
        compiler_params=_cparams(("parallel", "parallel", "arbitrary")),
    )(*args)


def _rowwise(fn, rows, consts, out_rows, out_accs, *, R, name, tm=256, rs=16, cs=None):
    tm = min(tm, R)
    assert R % tm == 0 and tm % rs == 0
    n_r, n_c, n_o, n_a = len(rows), len(consts), len(out_rows), len(out_accs)
    nsteps = R // tm
    widths = [w for _, _, w in rows]
    if cs is not None:
        assert all(w == widths[0] for w in widths) and widths[0] % cs == 0
        col_chunks = [(c0, cs) for c0 in range(0, widths[0], cs)]
    else:
        col_chunks = [None]

    def body(*refs):
        r_refs = refs[:n_r]
        c_refs = refs[n_r:n_r + n_c]
        o_refs = refs[n_r + n_c:n_r + n_c + n_o]
        a_refs = refs[n_r + n_c + n_o:n_r + n_c + n_o + n_a]
        s_refs = refs[n_r + n_c + n_o + n_a:]
        i = pl.program_id(0)

        @pl.when(i == 0)
        def _():
            for s in s_refs:
                s[...] = jnp.zeros_like(s)

        def piece(g, carry):
            r0 = pl.multiple_of(g * rs, rs)
            for cc in col_chunks:
                csl = slice(None) if cc is None else slice(cc[0], cc[0] + cc[1])
                rp = [r[pl.ds(r0, rs), csl] for r in r_refs]
                cp = [c[:, csl] for c in c_refs]
                outs, accs = fn(rp, cp)
                for o_ref, o in zip(o_refs, outs):
                    o_ref[pl.ds(r0, rs), csl] = o.astype(o_ref.dtype)
                for s_ref, av in zip(s_refs, accs):
                    s_ref[:, csl] += av
            return carry

        lax.fori_loop(0, tm // rs, piece, 0)

        @pl.when(i == nsteps - 1)
        def _():
            for a_ref, s_ref in zip(a_refs, s_refs):
                a_ref[...] = jnp.sum(s_ref[...], axis=0, keepdims=True)

    in_specs = [pl.BlockSpec((tm, w), functools.partial(lambda i, cb: (i, cb), cb=off // w)) for _, off, w in rows]
    for _, off, w in rows:
        assert off % w == 0
    in_specs += [pl.BlockSpec(c.shape, lambda i: (0, 0)) for c in consts]
    out_specs = [pl.BlockSpec((tm, w), lambda i: (i, 0)) for w, _ in out_rows]
    out_specs += [pl.BlockSpec((1, w), lambda i: (0, 0)) for w in out_accs]
    out_shape = [jax.ShapeDtypeStruct((R, w), dt) for w, dt in out_rows]
    out_shape += [jax.ShapeDtypeStruct((1, w), F32) for w in out_accs]
    return pl.pallas_call(
        body, name=name, out_shape=tuple(out_shape), grid=(nsteps,), in_specs=in_specs, out_specs=tuple(out_specs),
        scratch_shapes=[pltpu.VMEM((rs, w), F32) for w in out_accs],
        compiler_params=_cparams(("arbitrary",)),
    )(*[r for r, _, _ in rows], *consts)


def _rms(x, g):
    r = lax.rsqrt(jnp.mean(x * x, axis=-1, keepdims=True) + NORM_EPS)
    return x * r * g


def _rms_fwd(x, g, name):
    R = x.shape[0]

    def fn(rp, cp):
        return [_rms(rp[0], cp[0])], []

    return _rowwise(fn, [(x, 0, D_MODEL)], [g], [(D_MODEL, BF16)], [], R=R, name=name)[0]


def _rms_bwd(x, g, dh, dres, name):
    R = x.shape[0]

    def fn(rp, cp):
        xv, dhv, drv = rp
        _, pull = jax.vjp(_rms, xv, jnp.broadcast_to(cp[0], xv.shape))
        dx, dgv = pull(dhv)
        return [drv + dx], [dgv]

    dx, dg = _rowwise(fn, [(x, 0, D_MODEL), (dh, 0, D_MODEL), (dres, 0, D_MODEL)], [g], [(D_MODEL, F32)], [D_MODEL],
                      R=R, name=name, rs=8)
    return dx, dg


def _final_loss(x, g, tgt, name):
    R = x.shape[0]

    def loss_rows(xv, gv, tv):
        e = _rms(xv, gv) - tv
        return 0.5 * jnp.mean(e * e, axis=-1, keepdims=True)

    def fn(rp, cp):
        xv, tv = rp
        lr, pull = jax.vjp(lambda a, b: loss_rows(a, b, tv), xv, jnp.broadcast_to(cp[0], xv.shape))
        dx, dgv = pull(jnp.ones_like(lr))
        return [dx], [dgv, jnp.broadcast_to(lr, (lr.shape[0], 128))]

    dx, dg, lsum = _rowwise(fn, [(x, 0, D_MODEL), (tgt, 0, D_MODEL)], [g], [(D_MODEL, F32)], [D_MODEL, 128],
                            R=R, name=name, rs=8)
    return lsum[0, 0], dx, dg


def _neg_expm1(z):
    u = jnp.exp(z)
    safe = jnp.where(u == 1.0, 0.5, u)
    return -jnp.where(u == 1.0, z, (safe - 1.0) * z / jnp.log(safe))


def _lru_ab(xc, pr, pi, lam):
    r = jax.nn.sigmoid(pr)
    i = jax.nn.sigmoid(pi)
    log_a = -LRU_C * r * jax.nn.softplus(-lam)
    a = jnp.exp(log_a)
    b = jnp.sqrt(_neg_expm1(2.0 * log_a)) * (i * xc)
    return a, b


def _gated(h, gate):
    return h * jax.nn.gelu(gate)


def _row_iota8(w):
    return lax.broadcasted_iota(jnp.int32, (8, w), 0)


def _shift_dn(x, halo, s):
    xs = pltpu.roll(x, s, 0)
    hs = pltpu.roll(halo, s, 0)
    first = jnp.where(_row_iota8(x.shape[1]) < s, hs, xs[0:8])
    return jnp.concatenate([first, xs[8:]], axis=0) if x.shape[0] > 8 else first


def _shift_up(x, nxt, s):
    n = x.shape[0]
    xs = pltpu.roll(x, n - s, 0)
    ns = pltpu.roll(nxt, 8 - s, 0)
    last = jnp.where(_row_iota8(x.shape[1]) >= 8 - s, ns, xs[n - 8:])
    return jnp.concatenate([xs[:n - 8], last], axis=0) if n > 8 else last


def _lru_tiles(T):
    tT = min(256, T)
    return tT, T // tT


def _lru_fwd(proj, cw, cb, wax, bax, lam, name):
    T = proj.shape[0]
    W = MIX_W
    tT, nT = _lru_tiles(T)

    def body(x_ref, xh_ref, gt_ref, cw_ref, cb_ref, wax_ref, bax_ref, lam_ref, y_ref, h_ref, a_s, b_s, hc_s):
        i = pl.program_id(0)

        @pl.when(i == 0)
        def _():
            hc_s[...] = jnp.zeros_like(hc_s)

        x = x_ref[...]
        halo = jnp.where(i > 0, xh_ref[...], 0.0)
        w = cw_ref[...]
        xc = (cb_ref[...] + w[3:4] * x + w[2:3] * _shift_dn(x, halo, 1) + w[1:2] * _shift_dn(x, halo, 2)
              + w[0:1] * _shift_dn(x, halo, 3))
        pre = jnp.dot(xc.astype(BF16), wax_ref[...], preferred_element_type=F32) + bax_ref[...]
        a, b = _lru_ab(xc, pre[:, :W], pre[:, W:], lam_ref[...])
        a_s[...] = a
        b_s[...] = b
        row = _row_iota8(W)

        def grp(gi, hprev):
            r0 = pl.multiple_of(gi * 8, 8)
            A = a_s[pl.ds(r0, 8), :]
            B = b_s[pl.ds(r0, 8), :]
            for s in (1, 2, 4):
                As = pltpu.roll(A, s, 0)
                Bs = pltpu.roll(B, s, 0)
                m = row >= s
                B = jnp.where(m, A * Bs + B, B)
                A = jnp.where(m, A * As, A)
            H = A * hprev + B
            h_ref[pl.ds(r0, 8), :] = H
            return H[7:8, :]

        hc_s[0:1, :] = lax.fori_loop(0, tT // 8, grp, hc_s[0:1, :])
        y_ref[...] = _gated(h_ref[...], gt_ref[...]).astype(BF16)

    hb = tT // 8
    return pl.pallas_call(
        body, name=name,
        out_shape=(jax.ShapeDtypeStruct((T, W), BF16), jax.ShapeDtypeStruct((T, W), F32)),
        grid=(nT,),
        in_specs=[pl.BlockSpec((tT, W), lambda i: (i, 0)),
                  pl.BlockSpec((8, W), lambda i: (jnp.maximum(i * hb - 1, 0), 0)),
                  pl.BlockSpec((tT, W), lambda i: (i, 1)),
                  pl.BlockSpec((CONV_W, W), lambda i: (0, 0)), pl.BlockSpec((1, W), lambda i: (0, 0)),
                  pl.BlockSpec((W, 2 * W), lambda i: (0, 0)), pl.BlockSpec((1, 2 * W), lambda i: (0, 0)),
                  pl.BlockSpec((1, W), lambda i: (0, 0))],
        out_specs=(pl.BlockSpec((tT, W), lambda i: (i, 0)), pl.BlockSpec((tT, W), lambda i: (i, 0))),
        scratch_shapes=[pltpu.VMEM((tT, W), F32), pltpu.VMEM((tT, W), F32), pltpu.VMEM((8, W), F32)],
        compiler_params=_cparams(("arbitrary",)),
    )(proj, proj, proj, cw, cb, wax, bax, lam)


def _lru_bwd(proj, h, dy, cw, cb, wax, bax, lam, name):
    T = proj.shape[0]
    W = MIX_W
    tT, nT = _lru_tiles(T)
    hb = tT // 8

    def body(x_ref, xh_ref, gt_ref, h_ref, hh_ref, dy_ref, cw_ref, cb_ref, wax_ref, bax_ref, lam_ref,
             dx_ref, dgt_ref, dcw_ref, dcb_ref, dwax_ref, dbax_ref, dlam_ref,
             al_s, be_s, d_s, ca_s, cd_s, cx_s):
        i = pl.program_id(0)
        ib = nT - 1 - i

        @pl.when(i == 0)
        def _():
            for r in (ca_s, cd_s, cx_s, dcw_ref, dcb_ref, dwax_ref, dbax_ref, dlam_ref):
                r[...] = jnp.zeros_like(r)

        x = x_ref[...]
        halo = jnp.where(ib > 0, xh_ref[...], 0.0)
        w = cw_ref[...]
        x1, x2, x3 = _shift_dn(x, halo, 1), _shift_dn(x, halo, 2), _shift_dn(x, halo, 3)
        xc = cb_ref[...] + w[3:4] * x + w[2:3] * x1 + w[1:2] * x2 + w[0:1] * x3
        xcb = xc.astype(BF16)
        pre = jnp.dot(xcb, wax_ref[...], preferred_element_type=F32) + bax_ref[...]
        (a, _), pull_ab = jax.vjp(_lru_ab, xc, pre[:, :W], pre[:, W:], lam_ref[...])
        hv = h_ref[...]
        hprev = _shift_dn(hv, jnp.where(ib > 0, hh_ref[...], 0.0), 1)
        _, pull_y = jax.vjp(_gated, hv, gt_ref[...])
        dh_out, dgt = pull_y(dy_ref[...])
        dgt_ref[...] = dgt.astype(BF16)
        al_s[...] = _shift_up(a, ca_s[...], 1)
        be_s[...] = dh_out
        row = _row_iota8(W)
        ng = tT // 8

        def grp(k, dnext):
            r0 = pl.multiple_of((ng - 1 - k) * 8, 8)
            A = al_s[pl.ds(r0, 8), :]
            B = be_s[pl.ds(r0, 8), :]
            for s in (1, 2, 4):
                As = pltpu.roll(A, 8 - s, 0)
                Bs = pltpu.roll(B, 8 - s, 0)
                m = row < 8 - s
                B = jnp.where(m, A * Bs + B, B)
                A = jnp.where(m, A * As, A)
            Dg = A * dnext + B
            d_s[pl.ds(r0, 8), :] = Dg
            return Dg[0:1, :]

        lax.fori_loop(0, ng, grp, cd_s[0:1, :])
        Dv = d_s[...]
        dxc1, dpr, dpi, dlam = pull_ab((Dv * hprev, Dv))
        dpre = jnp.concatenate([dpr, dpi], axis=1)
        dpb = dpre.astype(BF16)
        dxc = dxc1 + lax.dot_general(dpb, wax_ref[...], (((1,), (1,)), ((), ())), preferred_element_type=F32)
        dwax_ref[...] += lax.dot_general(xcb, dpb, (((0,), (0,)), ((), ())), preferred_element_type=F32)
        dbax_ref[...] += jnp.sum(dpre, axis=0, keepdims=True)
        dlam_ref[...] += dlam
        dcb_ref[...] += jnp.sum(dxc, axis=0, keepdims=True)
        dcw_ref[...] += jnp.concatenate([jnp.sum(dxc * x3, axis=0, keepdims=True), jnp.sum(dxc * x2, axis=0, keepdims=True),
                                         jnp.sum(dxc * x1, axis=0, keepdims=True), jnp.sum(dxc * x, axis=0, keepdims=True)], axis=0)
        nxt = cx_s[...]
        dx = (w[3:4] * dxc + w[2:3] * _shift_up(dxc, nxt, 1) + w[1:2] * _shift_up(dxc, nxt, 2)
              + w[0:1] * _shift_up(dxc, nxt, 3))
        dx_ref[...] = dx.astype(BF16)
        ca_s[...] = a[0:8]
        cd_s[...] = Dv[0:8]
        cx_s[...] = dxc[0:8]

    rev = lambda i: nT - 1 - i
    const = lambda shape: pl.BlockSpec(shape, lambda i: (0, 0))
    return pl.pallas_call(
        body, name=name,
        out_shape=(jax.ShapeDtypeStruct((T, W), BF16), jax.ShapeDtypeStruct((T, W), BF16),
                   jax.ShapeDtypeStruct((CONV_W, W), F32), jax.ShapeDtypeStruct((1, W), F32),
                   jax.ShapeDtypeStruct((W, 2 * W), F32), jax.ShapeDtypeStruct((1, 2 * W), F32),
                   jax.ShapeDtypeStruct((1, W), F32)),
        grid=(nT,),
        in_specs=[pl.BlockSpec((tT, W), lambda i: (rev(i), 0)),
                  pl.BlockSpec((8, W), lambda i: (jnp.maximum(rev(i) * hb - 1, 0), 0)),
                  pl.BlockSpec((tT, W), lambda i: (rev(i), 1)),
                  pl.BlockSpec((tT, W), lambda i: (rev(i), 0)),
                  pl.BlockSpec((8, W), lambda i: (jnp.maximum(rev(i) * hb - 1, 0), 0)),
                  pl.BlockSpec((tT, W), lambda i: (rev(i), 0)),
                  const((CONV_W, W)), const((1, W)), const((W, 2 * W)), const((1, 2 * W)), const((1, W))],
        out_specs=(pl.BlockSpec((tT, W), lambda i: (rev(i), 0)), pl.BlockSpec((tT, W), lambda i: (rev(i), 0)),
                   const((CONV_W, W)), const((1, W)), const((W, 2 * W)), const((1, 2 * W)), const((1, W))),
        scratch_shapes=[pltpu.VMEM((tT, W), F32), pltpu.VMEM((tT, W), F32), pltpu.VMEM((tT, W), F32),
                        pltpu.VMEM((8, W), F32), pltpu.VMEM((8, W), F32), pltpu.VMEM((8, W), F32)],
        compiler_params=_cparams(("arbitrary",)),
    )(proj, proj, proj, h, h, dy, cw, cb, wax, bax, lam)


def _lru_prep(wa, wx, ba, bx):
    eye = jnp.eye(LRU_BLOCKS, dtype=F32)

    def dense(wb):
        return (wb[:, :, None, :] * eye[:, None, :, None]).reshape(MIX_W, MIX_W)

    wax = jnp.concatenate([dense(wa), dense(wx)], axis=1)
    bax = jnp.concatenate([ba, bx])[None, :]
    return wax, bax


_BVEC_W = 640


def _attn_bias(rel_bias):
    n_far = ATT_LEFT * CHUNK - MAX_REL + CHUNK
    far = jnp.broadcast_to(rel_bias[:, 2 * MAX_REL:], (ATT_HEADS, n_far))
    near = rel_bias[:, MAX_REL - (CHUNK - 1):2 * MAX_REL][:, ::-1]
    pad = jnp.zeros((ATT_HEADS, _BVEC_W - (CHUNK - 1 + ATT_BAND)), F32)
    return jnp.concatenate([far, near, pad], axis=1)[:, None, :]


def _bias_table(bvec_row):
    full = pltpu.roll(jnp.broadcast_to(bvec_row, (CHUNK, _BVEC_W)), _BVEC_W - (CHUNK - 1), 1, stride=1, stride_axis=0)
    return full[:, :ATT_BAND]


def _bias_table_t(ds):
    dpad = jnp.concatenate([ds, jnp.zeros((CHUNK, _BVEC_W - ATT_BAND), F32)], axis=1)
    r = lax.broadcasted_iota(jnp.int32, (CHUNK, CHUNK), 0)
    c = lax.broadcasted_iota(jnp.int32, (CHUNK, CHUNK), 1)
    rev = jnp.dot((r + c == CHUNK - 1).astype(F32), dpad, preferred_element_type=F32, precision=HI)
    return jnp.sum(pltpu.roll(rev, 0, 1, stride=1, stride_axis=0), axis=0, keepdims=True)


_NT = (((1,), (1,)), ((), ()))
_TN = (((0,), (0,)), ((), ()))
_ATT_PAD = ATT_LEFT * CHUNK
_Q_BLK, _K_BLK, _V_BLK = 2 * MIX_W // ATT_HD, 3 * MIX_W // ATT_HD, 4 * MIX_W // ATT_HD


def _attn_probs(q, kb, bias, c):
    s = lax.dot_general(q, kb, _NT, preferred_element_type=F32) * (ATT_HD ** -0.5) + bias
    kpos = lax.broadcasted_iota(jnp.int32, s.shape, 1)
    s = jnp.where(kpos >= (ATT_LEFT - c) * CHUNK, s, MASK_VALUE)
    e = jnp.exp(s - jnp.max(s, axis=-1, keepdims=True))
    return e / jnp.sum(e, axis=-1, keepdims=True)


def _attn_fwd(proj, bias, name):
    T = proj.shape[0]
    nC = T // CHUNK

    def body(q_ref, k_ref, v_ref, b_ref, o_ref, kp, vp):
        kp[0:_ATT_PAD, :] = jnp.zeros((_ATT_PAD, ATT_HD), BF16)
        vp[0:_ATT_PAD, :] = jnp.zeros((_ATT_PAD, ATT_HD), BF16)
        kp[_ATT_PAD:, :] = k_ref[...].astype(BF16)
        vp[_ATT_PAD:, :] = v_ref[...].astype(BF16)
        bias_v = _bias_table(b_ref[0])

        def chunk(c, carry):
            r0 = pl.multiple_of(c * CHUNK, CHUNK)
            q = q_ref[pl.ds(r0, CHUNK), :].astype(BF16)
            p = _attn_probs(q, kp[pl.ds(r0, ATT_BAND), :], bias_v, c)
            o = jnp.dot(p.astype(BF16), vp[pl.ds(r0, ATT_BAND), :], preferred_element_type=F32)
            o_ref[pl.ds(r0, CHUNK), :] = o.astype(BF16)
            return carry

        lax.fori_loop(0, nC, chunk, 0)

    return pl.pallas_call(
        body, name=name, out_shape=jax.ShapeDtypeStruct((T, MIX_W), BF16), grid=(ATT_HEADS,),
        in_specs=[pl.BlockSpec((T, ATT_HD), lambda h: (0, _Q_BLK + h)), pl.BlockSpec((T, ATT_HD), lambda h: (0, _K_BLK + h)),
                  pl.BlockSpec((T, ATT_HD), lambda h: (0, _V_BLK + h)), pl.BlockSpec((1, 1, _BVEC_W), lambda h: (h, 0, 0))],
        out_specs=pl.BlockSpec((T, ATT_HD), lambda h: (0, h)),
        scratch_shapes=[pltpu.VMEM((T + _ATT_PAD, ATT_HD), BF16), pltpu.VMEM((T + _ATT_PAD, ATT_HD), BF16)],
        compiler_params=_cparams(("arbitrary",)),
    )(proj, proj, proj, bias)


def _attn_bwd(proj, bias, do, name):
    T = proj.shape[0]
    nC = T // CHUNK

    def body(q_ref, k_ref, v_ref, b_ref, do_ref, dq_ref, dk_ref, dv_ref, db_ref, kp, vp, dkp, dvp, dbs):
        kp[0:_ATT_PAD, :] = jnp.zeros((_ATT_PAD, ATT_HD), BF16)
        vp[0:_ATT_PAD, :] = jnp.zeros((_ATT_PAD, ATT_HD), BF16)
        kp[_ATT_PAD:, :] = k_ref[...].astype(BF16)
        vp[_ATT_PAD:, :] = v_ref[...].astype(BF16)
        dkp[...] = jnp.zeros_like(dkp)
        dvp[...] = jnp.zeros_like(dvp)
        dbs[...] = jnp.zeros_like(dbs)
        bias_v = _bias_table(b_ref[0])

        def chunk(c, carry):
            r0 = pl.multiple_of(c * CHUNK, CHUNK)
            q = q_ref[pl.ds(r0, CHUNK), :].astype(BF16)
            kb = kp[pl.ds(r0, ATT_BAND), :]
            vb = vp[pl.ds(r0, ATT_BAND), :]
            dob = do_ref[pl.ds(r0, CHUNK), :].astype(BF16)
            p = _attn_probs(q, kb, bias_v, c)
            dp = lax.dot_general(dob, vb, _NT, preferred_element_type=F32)
            ds = p * (dp - jnp.sum(p * dp, axis=-1, keepdims=True))
            dbs[...] += ds
            dsb = (ds * (ATT_HD ** -0.5)).astype(BF16)
            dq_ref[pl.ds(r0, CHUNK), :] = jnp.dot(dsb, kb, preferred_element_type=F32).astype(BF16)
            dkp[pl.ds(r0, ATT_BAND), :] += lax.dot_general(dsb, q, _TN, preferred_element_type=F32)
            dvp[pl.ds(r0, ATT_BAND), :] += lax.dot_general(p.astype(BF16), dob, _TN, preferred_element_type=F32)
            return carry

        lax.fori_loop(0, nC, chunk, 0)
        dk_ref[...] = dkp[_ATT_PAD:, :].astype(BF16)
        dv_ref[...] = dvp[_ATT_PAD:, :].astype(BF16)
        db_ref[0] = _bias_table_t(dbs[...])

    hspec = pl.BlockSpec((T, ATT_HD), lambda h: (0, h))
    osd = jax.ShapeDtypeStruct((T, MIX_W), BF16)
    return pl.pallas_call(
        body, name=name,
        out_shape=(osd, osd, osd, jax.ShapeDtypeStruct((ATT_HEADS, 1, _BVEC_W), F32)), grid=(ATT_HEADS,),
        in_specs=[pl.BlockSpec((T, ATT_HD), lambda h: (0, _Q_BLK + h)), pl.BlockSpec((T, ATT_HD), lambda h: (0, _K_BLK + h)),
                  pl.BlockSpec((T, ATT_HD), lambda h: (0, _V_BLK + h)), pl.BlockSpec((1, 1, _BVEC_W), lambda h: (h, 0, 0)),
                  hspec],
        out_specs=(hspec, hspec, hspec, pl.BlockSpec((1, 1, _BVEC_W), lambda h: (h, 0, 0))),
        scratch_shapes=[pltpu.VMEM((T + _ATT_PAD, ATT_HD), BF16), pltpu.VMEM((T + _ATT_PAD, ATT_HD), BF16),
                        pltpu.VMEM((T + _ATT_PAD, ATT_HD), F32), pltpu.VMEM((T + _ATT_PAD, ATT_HD), F32),
                        pltpu.VMEM((CHUNK, ATT_BAND), F32)],
        compiler_params=_cparams(("arbitrary",)),
    )(proj, proj, proj, bias, do)


def _s5_prep(a_re, a_im, b_re, b_im, c_re, c_im, d, log_step):
    step = jnp.exp(log_step)[:, None]
    mag = jnp.exp(a_re * step)
    ang = a_im * step
    lb_re = mag * jnp.cos(ang)
    lb_im = mag * jnp.sin(ang)
    den = a_re * a_re + a_im * a_im
    nr = lb_re - 1.0
    coef_re = (nr * a_re + lb_im * a_im) / den
    coef_im = (lb_im * a_re - nr * a_im) / den
    bb_re = coef_re[..., None] * b_re - coef_im[..., None] * b_im
    bb_im = coef_re[..., None] * b_im + coef_im[..., None] * b_re
    prs, pis = [jnp.ones_like(lb_re)], [jnp.zeros_like(lb_re)]
    for _ in range(S5_L):
        prs.append(prs[-1] * lb_re - pis[-1] * lb_im)
        pis.append(prs[-2] * lb_im + pis[-1] * lb_re)
    PR, PI = jnp.stack(prs), jnp.stack(pis)
    cl_re = c_re[None] * PR[:, :, None, :] - c_im[None] * PI[:, :, None, :]
    cl_im = c_re[None] * PI[:, :, None, :] + c_im[None] * PR[:, :, None, :]
    kt = (jnp.einsum("tghp,gpk->tghk", cl_re[:S5_L], bb_re, precision=HI)
          - jnp.einsum("tghp,gpk->tghk", cl_im[:S5_L], bb_im, precision=HI))
    kpad = jnp.concatenate([jnp.zeros((S5_L - 1,) + kt.shape[1:], F32), kt], axis=0)
    blocks = jnp.stack([kpad[S5_L - 1 - lp:2 * S5_L - 1 - lp] for lp in range(S5_L)], axis=0)
    tgt = blocks.transpose(2, 0, 4, 1, 3).reshape(SSM_G, S5_LW, S5_LW)
    prr, pir = PR[:S5_L][::-1], PI[:S5_L][::-1]
    bret = (prr[:, :, None, :] * bb_re.transpose(0, 2, 1)[None] - pir[:, :, None, :] * bb_im.transpose(0, 2, 1)[None])
    bimt = (prr[:, :, None, :] * bb_im.transpose(0, 2, 1)[None] + pir[:, :, None, :] * bb_re.transpose(0, 2, 1)[None])
    bret = bret.transpose(1, 0, 2, 3).reshape(SSM_G, S5_LW, SSM_P)
    bimt = bimt.transpose(1, 0, 2, 3).reshape(SSM_G, S5_LW, SSM_P)
    cre = cl_re[1:].transpose(1, 3, 0, 2).reshape(SSM_G, SSM_P, S5_LW)
    cim = (-cl_im[1:]).transpose(1, 3, 0, 2).reshape(SSM_G, SSM_P, S5_LW)
    dflat = jnp.broadcast_to(d.reshape(SSM_G, 1, SSM_H), (SSM_G, S5_L, SSM_H)).reshape(SSM_G, 1, S5_LW)
    return tgt, bret, bimt, cre, cim, PR[S5_L], PI[S5_L], dflat


_S5_GB = 8


def _bdot(a, b, dims):
    return lax.dot_general(a, b, dims, preferred_element_type=F32, precision=HI)


_B_NN = (((2,), (1,)), ((0,), (0,)))
_B_NT = (((2,), (2,)), ((0,), (0,)))
_B_TN = (((1,), (1,)), ((0,), (0,)))


def _gspec(shape):
    return pl.BlockSpec((_S5_GB,) + shape, lambda g: (g, 0, 0))


def _s5_in(u, bret, bimt, name):
    C = u.shape[1]

    def body(u_ref, br_ref, bi_ref, sr_ref, si_ref):
        uv = u_ref[...]
        sr_ref[...] = _bdot(uv, br_ref[...], _B_NN)
        si_ref[...] = _bdot(uv, bi_ref[...], _B_NN)

    sd = jax.ShapeDtypeStruct((SSM_G, C, SSM_P), F32)
    return pl.pallas_call(
        body, name=name, out_shape=(sd, sd), grid=(SSM_G // _S5_GB,),
        in_specs=[_gspec((C, S5_LW)), _gspec((S5_LW, SSM_P)), _gspec((S5_LW, SSM_P))],
        out_specs=(_gspec((C, SSM_P)), _gspec((C, SSM_P))), compiler_params=_cparams(("parallel",)),
    )(u, bret, bimt)


def _s5_scan(sin_re, sin_im, lr, li, name):
    C = sin_re.shape[0]

    def body(ir_ref, ii_ref, lr_ref, li_ref, or_ref, oi_ref):
        lrv, liv = lr_ref[...], li_ref[...]

        def step(c, s):
            sr, si = s
            or_ref[c] = sr
            oi_ref[c] = si
            return lrv * sr - liv * si + ir_ref[c], lrv * si + liv * sr + ii_ref[c]

        z = jnp.zeros((SSM_G, SSM_P), F32)
        lax.fori_loop(0, C, step, (z, z))

    sd = jax.ShapeDtypeStruct((C, SSM_G, SSM_P), F32)
    return pl.pallas_call(body, name=name, out_shape=(sd, sd), compiler_params=_cparams())(sin_re, sin_im, lr, li)


def _s5_out(u, sp_re, sp_im, tgt, cre, cim, dflat, name):
    C = u.shape[1]

    def body(u_ref, sr_ref, si_ref, t_ref, cr_ref, ci_ref, d_ref, pre_ref, y_ref):
        uv = u_ref[...]
        pre = (_bdot(uv, t_ref[...], _B_NN) + _bdot(sr_ref[...], cr_ref[...], _B_NN)
               + _bdot(si_ref[...], ci_ref[...], _B_NN) + d_ref[...] * uv)
        pre_ref[...] = pre
        y_ref[...] = jax.nn.gelu(pre).astype(BF16)

    return pl.pallas_call(
        body, name=name,
        out_shape=(jax.ShapeDtypeStruct((SSM_G, C, S5_LW), F32), jax.ShapeDtypeStruct((SSM_G, C, S5_LW), BF16)),
        grid=(SSM_G // _S5_GB,),
        in_specs=[_gspec((C, S5_LW)), _gspec((C, SSM_P)), _gspec((C, SSM_P)), _gspec((S5_LW, S5_LW)),
                  _gspec((SSM_P, S5_LW)), _gspec((SSM_P, S5_LW)), _gspec((1, S5_LW))],
        out_specs=(_gspec((C, S5_LW)), _gspec((C, S5_LW))), compiler_params=_cparams(("parallel",)),
    )(u, sp_re, sp_im, tgt, cre, cim, dflat)


def _s5_bwd_out(dy, pre, u, sp_re, sp_im, cre, cim, name):
    C = u.shape[1]

    def body(dy_ref, pre_ref, u_ref, sr_ref, si_ref, cr_ref, ci_ref,
             dpre_ref, dsr_ref, dsi_ref, dt_ref, dcr_ref, dci_ref, dd_ref):
        _, pull = jax.vjp(jax.nn.gelu, pre_ref[...])
        dpre = pull(dy_ref[...])[0]
        uv = u_ref[...]
        dpre_ref[...] = dpre
        dsr_ref[...] = _bdot(dpre, cr_ref[...], _B_NT)
        dsi_ref[...] = _bdot(dpre, ci_ref[...], _B_NT)
        dt_ref[...] = _bdot(uv, dpre, _B_TN)
        dcr_ref[...] = _bdot(sr_ref[...], dpre, _B_TN)
        dci_ref[...] = _bdot(si_ref[...], dpre, _B_TN)
        dd_ref[...] = jnp.sum(dpre * uv, axis=1, keepdims=True)

    sd = jax.ShapeDtypeStruct
    return pl.pallas_call(
        body, name=name,
        out_shape=(sd((SSM_G, C, S5_LW), F32), sd((SSM_G, C, SSM_P), F32), sd((SSM_G, C, SSM_P), F32),
                   sd((SSM_G, S5_LW, S5_LW), F32), sd((SSM_G, SSM_P, S5_LW), F32), sd((SSM_G, SSM_P, S5_LW), F32),
                   sd((SSM_G, 1, S5_LW), F32)),
        grid=(SSM_G // _S5_GB,),
        in_specs=[_gspec((C, S5_LW)), _gspec((C, S5_LW)), _gspec((C, S5_LW)), _gspec((C, SSM_P)), _gspec((C, SSM_P)),
                  _gspec((SSM_P, S5_LW)), _gspec((SSM_P, S5_LW))],
        out_specs=(_gspec((C, S5_LW)), _gspec((C, SSM_P)), _gspec((C, SSM_P)), _gspec((S5_LW, S5_LW)),
                   _gspec((SSM_P, S5_LW)), _gspec((SSM_P, S5_LW)), _gspec((1, S5_LW))),
        compiler_params=_cparams(("parallel",)),
    )(dy, pre, u, sp_re, sp_im, cre, cim)


def _s5_rscan(dsp_re, dsp_im, sp_re, sp_im, lr, li, name):
    C = dsp_re.shape[0]

    def body(gr_ref, gi_ref, sr_ref, si_ref, lr_ref, li_ref, or_ref, oi_ref, dlr_ref, dli_ref):
        lrv, liv = lr_ref[...], li_ref[...]

        def step(k, carry):
            c = C - 1 - k
            dr, di, alr, ali = carry
            or_ref[c] = dr
            oi_ref[c] = di
            sr, si = sr_ref[c], si_ref[c]
            alr = alr + dr * sr + di * si
            ali = ali + di * sr - dr * si
            return gr_ref[c] + lrv * dr + liv * di, gi_ref[c] + lrv * di - liv * dr, alr, ali

        z = jnp.zeros((SSM_G, SSM_P), F32)
        _, _, alr, ali = lax.fori_loop(0, C, step, (z, z, z, z))
        dlr_ref[...] = alr
        dli_ref[...] = ali

    sd = jax.ShapeDtypeStruct((C, SSM_G, SSM_P), F32)
    sp = jax.ShapeDtypeStruct((SSM_G, SSM_P), F32)
    return pl.pallas_call(body, name=name, out_shape=(sd, sd, sp, sp), compiler_params=_cparams())(
        dsp_re, dsp_im, sp_re, sp_im, lr, li)


def _s5_bwd_in(dpre, dsin_re, dsin_im, u, tgt, bret, bimt, dflat, name):
    C = u.shape[1]

    def body(dp_ref, dr_ref, di_ref, u_ref, t_ref, br_ref, bi_ref, d_ref, du_ref, dbr_ref, dbi_ref):
        dp = dp_ref[...]
        dr, di, uv = dr_ref[...], di_ref[...], u_ref[...]
        du = (_bdot(dp, t_ref[...], _B_NT) + _bdot(dr, br_ref[...], _B_NT) + _bdot(di, bi_ref[...], _B_NT)
              + d_ref[...] * dp)
        du_ref[...] = du.astype(BF16)
        dbr_ref[...] = _bdot(uv, dr, _B_TN)
        dbi_ref[...] = _bdot(uv, di, _B_TN)

    sd = jax.ShapeDtypeStruct
    return pl.pallas_call(
        body, name=name,
        out_shape=(sd((SSM_G, C, S5_LW), BF16), sd((SSM_G, S5_LW, SSM_P), F32), sd((SSM_G, S5_LW, SSM_P), F32)),
        grid=(SSM_G // _S5_GB,),
        in_specs=[_gspec((C, S5_LW)), _gspec((C, SSM_P)), _gspec((C, SSM_P)), _gspec((C, S5_LW)),
                  _gspec((S5_LW, S5_LW)), _gspec((S5_LW, SSM_P)), _gspec((S5_LW, SSM_P)), _gspec((1, S5_LW))],
        out_specs=(_gspec((C, S5_LW)), _gspec((S5_LW, SSM_P)), _gspec((S5_LW, SSM_P))),
        compiler_params=_cparams(("parallel",)),
    )(dpre, dsin_re, dsin_im, u, tgt, bret, bimt, dflat)


def _to_chunks(v):
    T = v.shape[0]
    return v.reshape(T // S5_L, S5_L, SSM_G, SSM_H).transpose(2, 0, 1, 3).reshape(SSM_G, T // S5_L, S5_LW)


def _from_chunks(v):
    C = v.shape[1]
    return v.reshape(SSM_G, C, S5_L, SSM_H).transpose(1, 2, 0, 3).reshape(C * S5_L, MIX_W)


def _merge_fn(bra, brb, pc, pg, g0, g1, g2, b0, b1, b2):
    sg = jax.nn.sigmoid
    return sg(g0 + b0) * bra + sg(g1 + b1) * brb + sg(g2 + b2) * (pc * sg(pg))


_EW_CS = 256
_GATE_OFF = 6 * MIX_W


def _merge_rows(br4, proj):
    return [(b, 0, D_MODEL) for b in br4] + [(proj, _GATE_OFF + k * D_MODEL, D_MODEL) for k in range(3)]


def _merge_fwd(br4, proj, gb3, name):
    def fn(rp, cp):
        b = cp[0]
        return [_merge_fn(*rp, b[0:1], b[1:2], b[2:3])], []

    return _rowwise(fn, _merge_rows(br4, proj), [gb3], [(D_MODEL, BF16)], [], R=proj.shape[0], name=name, cs=_EW_CS)[0]


def _merge_bwd(br4, proj, gb3, dm, name):
    def fn(rp, cp):
        b = cp[0]
        shp = rp[0].shape
        bs = [jnp.broadcast_to(b[k:k + 1], shp) for k in range(3)]
        _, pull = jax.vjp(_merge_fn, *rp[:7], *bs)
        g = pull(rp[7])
        return list(g[:7]), list(g[7:])

    rows = _merge_rows(br4, proj) + [(dm, 0, D_MODEL)]
    outs = _rowwise(fn, rows, [gb3], [(D_MODEL, BF16)] * 7, [D_MODEL] * 3, R=proj.shape[0], name=name, tm=128, cs=_EW_CS)
    return outs[:4], outs[4:7], jnp.concatenate(outs[7:], axis=0)


def _swiglu(g, u):
    return jax.nn.silu(g) * u


def _act_fwd(gu, name):
    def fn(rp, cp):
        return [_swiglu(*rp)], []

    return _rowwise(fn, [(gu[0], 0, FFN_H), (gu[1], 0, FFN_H)], [], [(FFN_H, BF16)], [], R=gu[0].shape[0], name=name, cs=_EW_CS)[0]


def _act_bwd(gu, dact, name):
    def fn(rp, cp):
        _, pull = jax.vjp(_swiglu, rp[0], rp[1])
        return list(pull(rp[2])), []

    return _rowwise(fn, [(gu[0], 0, FFN_H), (gu[1], 0, FFN_H), (dact, 0, FFN_H)], [], [(FFN_H, BF16)] * 2, [],
                    R=dact.shape[0], name=name, tm=128, cs=_EW_CS)


def _adamw_fn(w, g, m, v):
    m = ADAM_B1 * m + (1.0 - ADAM_B1) * g
    v = ADAM_B2 * v + (1.0 - ADAM_B2) * jnp.square(g)
    m_hat = m / (1.0 - ADAM_B1 ** ADAM_STEP)
    v_hat = v / (1.0 - ADAM_B2 ** ADAM_STEP)
    delta = -ADAM_LR * (m_hat / (jnp.sqrt(v_hat) + ADAM_EPS) + ADAM_WD * w)
    return delta, m, v


def _adamw(w, g, m, v, name):
    R, C = w.shape

    def fn(rp, cp):
        return list(_adamw_fn(*rp)), []

    cs = _pick(C, (512, 256, 128))
    tm = _pick(R, (256, 128, 64, 32, 16, 8))
    return _rowwise(fn, [(w, 0, C), (g, 0, C), (m, 0, C), (v, 0, C)], [], [(C, F32)] * 3, [], R=R, name=name,
                    tm=tm, rs=8, cs=cs)


def _my_place():
    return lax.axis_index("x"), lax.axis_index("y"), lax.axis_index("c")


def _other_chips(x, y):
    return [(1 - x, y), (x, 1 - y), (1 - x, 1 - y)]


_ANY = pl.BlockSpec(memory_space=pl.ANY)


def _rcopy(src, dst, ssem, rsem, to):
    return pltpu.make_async_remote_copy(src_ref=src, dst_ref=dst, send_sem=ssem, recv_sem=rsem, device_id=to,
                                        device_id_type=MESH)


def _place_shard(local, axis, jidx, out_dtype, name):
    lead, r, c = local.shape
    shp = [lead, r, c]
    shp[axis] *= N_CHIPS
    tr = _pick(r, (512, 256, 128)) if r % 128 == 0 else r
    nr = r // tr
    omap = (lambda l, i, j: (l, i, j[0])) if axis == 2 else (lambda l, i, j: (l, j[0] * nr + i, 0))

    def body(j_ref, x_ref, o_ref):
        o_ref[...] = x_ref[...].astype(out_dtype)

    return pl.pallas_call(
        body, name=name, out_shape=jax.ShapeDtypeStruct(tuple(shp), out_dtype),
        grid_spec=pltpu.PrefetchScalarGridSpec(
            num_scalar_prefetch=1, grid=(lead, nr),
            in_specs=[pl.BlockSpec((None, tr, c), lambda l, i, j: (l, i, 0))],
            out_specs=pl.BlockSpec((None, tr, c), omap)),
        compiler_params=_cparams(("parallel", "parallel")),
    )(jidx, local)


def _gather_weights(fulls, axes, name):
    n = len(fulls)

    def body(*refs):
        outs = refs[n:2 * n]
        send_sems, recv_sems = refs[2 * n:]
        x, y, c = _my_place()
        sib = (x, y, 1 - c)
        chips = _other_chips(x, y)

        def block(t, chip, half):
            lead = outs[t].shape[0] // 2
            size = outs[t].shape[axes[t]] // N_CHIPS
            j = 2 * chip[0] + chip[1]
            idx = [pl.ds(half * lead, lead), slice(None), slice(None)]
            idx[axes[t]] = pl.ds(j * size, size)
            return outs[t].at[tuple(idx)]

        sends = []
        for t in range(n):
            mine = block(t, (x, y), c)
            for r, chip in enumerate(chips):
                k = 3 * t + r
                cp = _rcopy(mine, mine, send_sems.at[k], recv_sems.at[k], (*chip, c))
                cp.start()
                sends.append(cp)
        passed = []
        for t in range(n):
            for r, chip in enumerate(chips):
                k = 3 * t + r
                landed = block(t, chip, c)
                _rcopy(landed, landed, send_sems.at[k], recv_sems.at[k], (*chip, c)).wait_recv()
                cp = _rcopy(landed, landed, send_sems.at[3 * n + k], recv_sems.at[3 * n + k], sib)
                cp.start()
                passed.append(cp)
        for t in range(n):
            for r, chip in enumerate(chips):
                k = 3 * n + 3 * t + r
                theirs = block(t, chip, 1 - c)
                _rcopy(theirs, theirs, send_sems.at[k], recv_sems.at[k], sib).wait_recv()
        for cp in sends + passed:
            cp.wait_send()

    return pl.pallas_call(
        body, name=name, out_shape=tuple(jax.ShapeDtypeStruct(f.shape, f.dtype) for f in fulls),
        in_specs=[_ANY] * n, out_specs=tuple([_ANY] * n), input_output_aliases={t: t for t in range(n)},
        scratch_shapes=[pltpu.SemaphoreType.DMA((6 * n,)), pltpu.SemaphoreType.DMA((6 * n,))],
    )(*fulls)


def _half_idx(shape, axis, half):
    size = shape[axis] // 2
    idx = [slice(None), slice(None)]
    idx[axis] = pl.ds(half * size, size)
    return tuple(idx)


def _pair_exchange(grads, half_axes, name):
    n = len(grads)
    out_shapes = []
    for g, ax in zip(grads, half_axes):
        shp = list(g.shape)
        shp[ax] //= 2
        out_shapes.append(jax.ShapeDtypeStruct(tuple(shp), g.dtype))

    def body(*refs):
        srcs, outs = refs[:n], refs[n:2 * n]
        send_sems, recv_sems = refs[2 * n:]
        x, y, c = _my_place()
        sib = (x, y, 1 - c)
        cps = []
        for t in range(n):
            cp = _rcopy(srcs[t].at[_half_idx(srcs[t].shape, half_axes[t], 1 - c)], outs[t], send_sems.at[t], recv_sems.at[t], sib)
            cp.start()
            cps.append(cp)
        for cp in cps:
            cp.wait()

    return pl.pallas_call(
        body, name=name, out_shape=tuple(out_shapes), in_specs=[_ANY] * n, out_specs=tuple([_ANY] * n),
        scratch_shapes=[pltpu.SemaphoreType.DMA((n,)), pltpu.SemaphoreType.DMA((n,))],
    )(*grads)


def _pair_sum(g, recv, half_axis, cidx, name):
    K, N = recv.shape
    tm = _pick(K, (256, 128, 64, 32, 16))
    tn = _pick(N, (1024, 1408, 512, 256, 128))
    nbr, nbc = K // tm, N // tn
    if half_axis == 0:
        gmap = lambda i, j, c: (c[0] * nbr + i, j)
    else:
        gmap = lambda i, j, c: (i, c[0] * nbc + j)

    def body(c_ref, g_ref, r_ref, of_ref, ob_ref):
        s = g_ref[...] + r_ref[...]
        of_ref[...] = s
        ob_ref[...] = s.astype(BF16)

    omap = lambda i, j, c: (i, j)
    return pl.pallas_call(
        body, name=name,
        out_shape=(jax.ShapeDtypeStruct((K, N), F32), jax.ShapeDtypeStruct((K, N), BF16)),
        grid_spec=pltpu.PrefetchScalarGridSpec(
            num_scalar_prefetch=1, grid=(nbr, nbc),
            in_specs=[pl.BlockSpec((tm, tn), gmap), pl.BlockSpec((tm, tn), omap)],
            out_specs=(pl.BlockSpec((tm, tn), omap), pl.BlockSpec((tm, tn), omap))),
        compiler_params=_cparams(("parallel", "parallel")),
    )(cidx, g, recv)


def _shard_idx(shape, axis, j):
    size = shape[axis] // N_CHIPS
    idx = [slice(None), slice(None)]
    idx[axis] = pl.ds(j * size, size)
    return tuple(idx)


def _chip_scatter(parts, shard_axes, name):
    n = len(parts)
    out_shapes = []
    for p, ax in zip(parts, shard_axes):
        shp = list(p.shape)
        shp[ax] //= N_CHIPS
        out_shapes.append(jax.ShapeDtypeStruct((3,) + tuple(shp), p.dtype))

    def body(*refs):
        srcs, outs = refs[:n], refs[n:2 * n]
        send_sems, recv_sems = refs[2 * n:]
        x, y, c = _my_place()
        cps = []
        for t in range(n):
            for r, chip in enumerate(_other_chips(x, y)):
                k = 3 * t + r
                j = 2 * chip[0] + chip[1]
                cp = _rcopy(srcs[t].at[_shard_idx(srcs[t].shape, shard_axes[t], j)], outs[t].at[r],
                            send_sems.at[k], recv_sems.at[k], (*chip, c))
                cp.start()
                cps.append(cp)
        for cp in cps:
            cp.wait()

    return pl.pallas_call(
        body, name=name, out_shape=tuple(out_shapes), in_specs=[_ANY] * n, out_specs=tuple([_ANY] * n),
        scratch_shapes=[pltpu.SemaphoreType.DMA((3 * n,)), pltpu.SemaphoreType.DMA((3 * n,))],
    )(*parts)


def _shard_sum(pf, recv, acc, layer, shard_axis, jcidx, name):
    _, K, N = recv.shape
    tm = _pick(K, (256, 128, 64, 32, 16))
    tn = _pick(N, (1024, 1408, 512, 256, 128))
    nbr, nbc = K // tm, N // tn
    if shard_axis == 0:
        pmap = lambda i, j, s: (s[0] * nbr + i, j)
        omap = lambda i, j, s: (layer, i, s[1] * nbc + j)
    else:
        pmap = lambda i, j, s: (i, s[0] * nbc + j)
        omap = lambda i, j, s: (layer, s[1] * nbr + i, j)

    def body(j_ref, p_ref, r_ref, a_ref, o_ref):
        o_ref[...] = ((p_ref[...] + r_ref[0].astype(F32)) + r_ref[1].astype(F32)) + r_ref[2].astype(F32)

    return pl.pallas_call(
        body, name=name, out_shape=jax.ShapeDtypeStruct(acc.shape, F32),
        grid_spec=pltpu.PrefetchScalarGridSpec(
            num_scalar_prefetch=1, grid=(nbr, nbc),
            in_specs=[pl.BlockSpec((tm, tn), pmap), pl.BlockSpec((3, tm, tn), lambda i, j, s: (0, i, j)), _ANY],
            out_specs=pl.BlockSpec((None, tm, tn), omap)),
        input_output_aliases={3: 0},
        compiler_params=_cparams(("parallel", "parallel")),
    )(jcidx, pf, recv, acc)


def _pair_join(accs, half_axes, name):
    n = len(accs)

    def body(*refs):
        outs = refs[n:2 * n]
        send_sems, recv_sems = refs[2 * n:]
        x, y, c = _my_place()
        sib = (x, y, 1 - c)

        def half(t, hc):
            return outs[t].at[(slice(None),) + _half_idx(outs[t].shape[1:], half_axes[t], hc)]

        cps = []
        for t in range(n):
            cp = _rcopy(half(t, c), half(t, c), send_sems.at[t], recv_sems.at[t], sib)
            cp.start()
            cps.append(cp)
        for t in range(n):
            _rcopy(half(t, 1 - c), half(t, 1 - c), send_sems.at[t], recv_sems.at[t], sib).wait_recv()
        for cp in cps:
            cp.wait_send()

    return pl.pallas_call(
        body, name=name, out_shape=tuple(jax.ShapeDtypeStruct(a.shape, a.dtype) for a in accs),
        in_specs=[_ANY] * n, out_specs=tuple([_ANY] * n), input_output_aliases={t: t for t in range(n)},
        scratch_shapes=[pltpu.SemaphoreType.DMA((n,)), pltpu.SemaphoreType.DMA((n,))],
    )(*accs)


_N_DEV = 8


def _allreduce_small(flat, name):
    _, R, _ = flat.shape

    def body(in_ref, out_ref, stage, send1, recv1, send2, recv2):
        x, y, c = _my_place()
        me = 4 * x + 2 * y + c
        places = [(px, py, pc) for px in range(2) for py in range(2) for pc in range(2)]
        def peer(r):
            return (x ^ (r >> 2), y ^ ((r >> 1) & 1), c ^ (r & 1))

        def peer_id(r):
            p = peer(r)
            return 4 * p[0] + 2 * p[1] + p[2]

        stage[0] = in_ref[me]
        cps = []
        for r in range(1, _N_DEV):
            cp = _rcopy(in_ref.at[peer_id(r)], stage.at[r], send1.at[r], recv1.at[r], peer(r))
            cp.start()
            cps.append(cp)
        for cp in cps:
            cp.wait()
        tot = jnp.zeros((R, 128), F32)
        for d in range(_N_DEV):
            tot = tot + stage[me ^ d]
        out_ref[me] = tot
        cps = []
        for r in range(1, _N_DEV):
            cp = _rcopy(out_ref.at[me], out_ref.at[me], send2.at[r], recv2.at[r], peer(r))
            cp.start()
            cps.append(cp)
        for r in range(1, _N_DEV):
            _rcopy(out_ref.at[peer_id(r)], out_ref.at[peer_id(r)], send2.at[r], recv2.at[r], peer(r)).wait_recv()
        for cp in cps:
            cp.wait_send()

    vm = pl.BlockSpec(memory_space=pltpu.VMEM)
    return pl.pallas_call(
        body, name=name, out_shape=jax.ShapeDtypeStruct(flat.shape, F32), in_specs=[vm], out_specs=vm,
        scratch_shapes=[pltpu.VMEM(flat.shape, F32)] + [pltpu.SemaphoreType.DMA((_N_DEV,))] * 4,
        compiler_params=_cparams(),
    )(flat)


_BIG = ("w_in", "ssm_w_glu", "w_branch", "w_out", "w_ffn_gate", "w_ffn_up", "w_ffn_down")
_BIG_SHARD_AXIS = {"w_in": 1, "ssm_w_glu": 1, "w_branch": 1, "w_out": 0, "w_ffn_gate": 1, "w_ffn_up": 1, "w_ffn_down": 0}
_SMALL = ("norm_mix_g", "gate_bias", "lru_conv_w", "lru_conv_b", "lru_wa", "lru_ba", "lru_wx", "lru_bx", "lru_lambda",
          "attn_rel_bias", "ssm_a_re", "ssm_a_im", "ssm_b_re", "ssm_b_im", "ssm_c_re", "ssm_c_im", "ssm_d",
          "ssm_log_step", "norm_ffn_g", "norm_final_g")
_SMALL_SHARDED = {"gate_bias": 2, "lru_conv_w": 2}
_WEIGHTS = ("norm_mix_g", "w_in", "gate_bias", "lru_conv_w", "lru_conv_b", "lru_wa", "lru_ba", "lru_wx", "lru_bx",
            "lru_lambda", "attn_rel_bias", "ssm_a_re", "ssm_a_im", "ssm_b_re", "ssm_b_im", "ssm_c_re", "ssm_c_im",
            "ssm_d", "ssm_log_step", "ssm_w_glu", "w_branch", "w_out", "norm_ffn_g", "w_ffn_gate", "w_ffn_up",
            "w_ffn_down", "norm_final_g")


def _layer_fwd(l, x, W, sm):
    T = x.shape[0]
    nm = lambda s: f"{s}"
    h1 = _rms_fwd(x, sm["norm_mix_g"][l][None, :], nm("rms_fwd"))
    proj = _mm(h1, W["w_in"], M=T, N=IN_W, K=D_MODEL, b_lead=l, name=nm("mm_in"))
    wax, bax = sm["lru_prep"][l]
    cw, cb, lam = sm["lru_conv_w"][l], sm["lru_conv_b"][l][None, :], sm["lru_lambda"][l][None, :]
    y_a, hst = _lru_fwd(proj, cw, cb, wax, bax, lam, nm("lru_fwd"))
    bias = sm["attn_bias"][l]
    y_b = _attn_fwd(proj, bias, nm("attn_fwd"))
    tgt, bret, bimt, cre, cim, lr, li, dflat = sm["s5_prep"][l]
    u = _to_chunks(lax.slice_in_dim(proj, 5 * MIX_W, 6 * MIX_W, axis=1))
    sin_re, sin_im = _s5_in(u, bret, bimt, nm("s5_in"))
    sp_re, sp_im = _s5_scan(sin_re.transpose(1, 0, 2), sin_im.transpose(1, 0, 2), lr, li, nm("s5_scan"))
    sp_re, sp_im = sp_re.transpose(1, 0, 2), sp_im.transpose(1, 0, 2)
    pre, ycf = _s5_out(u, sp_re, sp_im, tgt, cre, cim, dflat, nm("s5_out"))
    y_c = _from_chunks(ycf)
    brs = []
    for k, yk in enumerate((y_a, y_b, y_c)):
        brs.append(_mm(yk, W["w_branch"], M=T, N=D_MODEL, K=MIX_W, b_lead=3 * l + k, name=nm("mm_branch")))
    brs.append(_mm(y_c, W["ssm_w_glu"], M=T, N=D_MODEL, K=MIX_W, b_lead=l, name=nm("mm_branch")))
    br4 = tuple(brs)
    gb3 = sm["gate_bias"][l]
    merged = _merge_fwd(br4, proj, gb3, nm("merge_fwd"))
    x1 = _mm(merged, W["w_out"], M=T, N=D_MODEL, K=D_MODEL, b_lead=l, res=x, name=nm("mm_out"))
    h2 = _rms_fwd(x1, sm["norm_ffn_g"][l][None, :], nm("rms_fwd"))
    gpre = _mm(h2, W["w_ffn_gate"], M=T, N=FFN_H, K=D_MODEL, b_lead=l, name=nm("mm_ffn_up"))
    upre = _mm(h2, W["w_ffn_up"], M=T, N=FFN_H, K=D_MODEL, b_lead=l, name=nm("mm_ffn_up"))
    gu = (gpre, upre)
    act = _act_fwd(gu, nm("act_fwd"))
    x2 = _mm(act, W["w_ffn_down"], M=T, N=D_MODEL, K=FFN_H, b_lead=l, res=x1, name=nm("mm_down"))
    saved = dict(x=x, h1=h1, proj=proj, hst=hst, y_a=y_a, y_b=y_b, y_c=y_c, u=u, sp_re=sp_re, sp_im=sp_im, pre=pre,
                 br4=br4, merged=merged, x1=x1, h2=h2, gu=gu, act=act)
    return x2, saved


def _layer_bwd(l, dx2, sv, W, sm):
    T = dx2.shape[0]
    nm = lambda s: f"{s}"
    big, small = {}, {}
    dxb = dx2.astype(BF16)
    big["w_ffn_down"] = _mm(sv["act"], dxb, M=FFN_H, N=D_MODEL, K=T, ta=True, name=nm("mm_dw_down"))
    dact = _mm(dxb, W["w_ffn_down"], M=T, N=FFN_H, K=D_MODEL, tb=True, b_lead=l, name=nm("mm_dact"))
    dg, du = _act_bwd(sv["gu"], dact, nm("act_bwd"))
    big["w_ffn_gate"] = _mm(sv["h2"], dg, M=D_MODEL, N=FFN_H, K=T, ta=True, name=nm("mm_dw_up"))
    big["w_ffn_up"] = _mm(sv["h2"], du, M=D_MODEL, N=FFN_H, K=T, ta=True, name=nm("mm_dw_up"))
    dh2 = _mm(dg, W["w_ffn_gate"], M=T, N=D_MODEL, K=FFN_H, tb=True, b_lead=l, name=nm("mm_dh2"))
    dh2 = _mm(du, W["w_ffn_up"], M=T, N=D_MODEL, K=FFN_H, tb=True, b_lead=l, res=dh2, name=nm("mm_dh2r"))
    dx1, dgn = _rms_bwd(sv["x1"], sm["norm_ffn_g"][l][None, :], dh2, dx2, nm("rms_bwd"))
    small["norm_ffn_g"] = dgn[0]
    dx1b = dx1.astype(BF16)
    big["w_out"] = _mm(sv["merged"], dx1b, M=D_MODEL, N=D_MODEL, K=T, ta=True, name=nm("mm_dw_out"))
    dm = _mm(dx1b, W["w_out"], M=T, N=D_MODEL, K=D_MODEL, tb=True, b_lead=l, name=nm("mm_dmerged"))
    dbr, dgates, dgb = _merge_bwd(sv["br4"], sv["proj"], sm["gate_bias"][l], dm, nm("merge_bwd"))
    small["gate_bias"] = dgb
    ys = (sv["y_a"], sv["y_b"], sv["y_c"])
    big["w_branch"] = [_mm(ys[k], dbr[k], M=MIX_W, N=D_MODEL, K=T, ta=True, name=nm("mm_dw_branch")) for k in range(3)]
    big["ssm_w_glu"] = _mm(sv["y_c"], dbr[3], M=MIX_W, N=D_MODEL, K=T, ta=True, name=nm("mm_dw_branch"))
    dya = _mm(dbr[0], W["w_branch"], M=T, N=MIX_W, K=D_MODEL, tb=True, b_lead=3 * l, name=nm("mm_dy"))
    dyb = _mm(dbr[1], W["w_branch"], M=T, N=MIX_W, K=D_MODEL, tb=True, b_lead=3 * l + 1, name=nm("mm_dy"))
    dyc = _mm(dbr[2], W["w_branch"], M=T, N=MIX_W, K=D_MODEL, tb=True, b_lead=3 * l + 2, name=nm("mm_dy"))
    dyc = _mm(dbr[3], W["ssm_w_glu"], M=T, N=MIX_W, K=D_MODEL, tb=True, b_lead=l, res=dyc, name=nm("mm_dyr"))
    tgt, bret, bimt, cre, cim, lr, li, dflat = sm["s5_prep"][l]
    dpre, dsp_re, dsp_im, d_tgt, d_cre, d_cim, d_dflat = _s5_bwd_out(
        _to_chunks(dyc), sv["pre"], sv["u"], sv["sp_re"], sv["sp_im"], cre, cim, nm("s5_bwd_out"))
    tr = lambda a: a.transpose(1, 0, 2)
    dsin_re, dsin_im, d_lr, d_li = _s5_rscan(tr(dsp_re), tr(dsp_im), tr(sv["sp_re"]), tr(sv["sp_im"]), lr, li, nm("s5_rscan"))
    du_f, d_bret, d_bimt = _s5_bwd_in(dpre, dsin_re.transpose(1, 0, 2), dsin_im.transpose(1, 0, 2), sv["u"], tgt, bret, bimt,
                                      dflat, nm("s5_bwd_in"))
    d_u = _from_chunks(du_f)
    small["s5_tables"] = (d_tgt, d_bret, d_bimt, d_cre, d_cim, d_lr, d_li, d_dflat)
    dq, dk, dv, dbias = _attn_bwd(sv["proj"], sm["attn_bias"][l], dyb, nm("attn_bwd"))
    small["attn_bias"] = dbias
    wax, bax = sm["lru_prep"][l]
    cw, cb, lam = sm["lru_conv_w"][l], sm["lru_conv_b"][l][None, :], sm["lru_lambda"][l][None, :]
    d_lx, d_lg, d_cw, d_cb, d_wax, d_bax, d_lam = _lru_bwd(sv["proj"], sv["hst"], dya, cw, cb, wax, bax, lam, nm("lru_bwd"))
    small["lru_conv_w"], small["lru_conv_b"], small["lru_lambda"] = d_cw, d_cb[0], d_lam[0]
    small["lru_tables"] = (d_wax, d_bax)
    dproj = jnp.concatenate([d_lx, d_lg, dq, dk, dv, d_u] + list(dgates), axis=1)
    big["w_in"] = _mm(sv["h1"], dproj, M=D_MODEL, N=IN_W, K=T, ta=True, name=nm("mm_dw_in"))
    dh1 = _mm(dproj, W["w_in"], M=T, N=D_MODEL, K=IN_W, tb=True, b_lead=l, name=nm("mm_dh1"))
    dx, dgn = _rms_bwd(sv["x"], sm["norm_mix_g"][l][None, :], dh1, dx1, nm("rms_bwd"))
    small["norm_mix_g"] = dgn[0]
    return dx, big, small


def _reduce_big(l, depth, big, accs, cidx, jcidx):
    names, grads, shard_axes = [], [], []
    for n in _BIG:
        gs = big[n] if isinstance(big[n], list) else [big[n]]
        for k, g in enumerate(gs):
            names.append((n, k))
            grads.append(g)
            shard_axes.append(_BIG_SHARD_AXIS[n])
    half_axes = [1 - a for a in shard_axes]
    recv = _pair_exchange(grads, half_axes, "pair_exchange")
    pfs, pbs = [], []
    for g, r, ha in zip(grads, recv, half_axes):
        pf, pb = _pair_sum(g, r, ha, cidx, "pair_sum")
        pfs.append(pf)
        pbs.append(pb)
    got = _chip_scatter(pbs, shard_axes, "chip_scatter")
    for nk, g, pf, r, sa in zip(names, grads, pfs, got, shard_axes):
        if nk not in accs:
            shp = list(g.shape)
            shp[sa] //= N_CHIPS
            accs[nk] = lax.empty((depth,) + tuple(shp), F32)
        accs[nk] = _shard_sum(pf, r, accs[nk], l, sa, jcidx, "shard_sum")
    return names, half_axes


def _local_step(xs, tgt, W, sm, on_big):
    sm = dict(sm)
    depth = sm["norm_mix_g"].shape[0]
    lru_prep_vjps, attn_vjps, s5_vjps = [], [], []
    sm["lru_prep"], sm["attn_bias"], sm["s5_prep"] = [], [], []
    for l in range(depth):
        o, f = jax.vjp(_lru_prep, sm["lru_wa"][l], sm["lru_wx"][l], sm["lru_ba"][l], sm["lru_bx"][l])
        sm["lru_prep"].append((o[0].astype(BF16), o[1]))
        lru_prep_vjps.append(f)
        o, f = jax.vjp(_attn_bias, sm["attn_rel_bias"][l])
        sm["attn_bias"].append(o)
        attn_vjps.append(f)
        o, f = jax.vjp(_s5_prep, *[sm[n][l] for n in ("ssm_a_re", "ssm_a_im", "ssm_b_re", "ssm_b_im", "ssm_c_re", "ssm_c_im",
                                                       "ssm_d", "ssm_log_step")])
        sm["s5_prep"].append(o)
        s5_vjps.append(f)

    saved = []
    for l in range(depth):
        xs, sv = _layer_fwd(l, xs, W, sm)
        saved.append(sv)
    loss_part, dx, dgf = _final_loss(xs, sm["norm_final_g"][None, :], tgt, "final_loss")

    small_g = {n: [None] * depth for n in _SMALL if n != "norm_final_g"}
    for l in reversed(range(depth)):
        dx, big, small = _layer_bwd(l, dx, saved[l], W, sm)
        on_big(l, big)
        d_wa, d_wx, d_ba, d_bx = lru_prep_vjps[l](small["lru_tables"])
        (d_rel,) = attn_vjps[l](small["attn_bias"])
        d_s5 = s5_vjps[l](small["s5_tables"])
        vals = dict(norm_mix_g=small["norm_mix_g"], gate_bias=small["gate_bias"], lru_conv_w=small["lru_conv_w"],
                    lru_conv_b=small["lru_conv_b"], lru_wa=d_wa, lru_ba=d_ba, lru_wx=d_wx, lru_bx=d_bx,
                    lru_lambda=small["lru_lambda"], attn_rel_bias=d_rel, ssm_a_re=d_s5[0], ssm_a_im=d_s5[1],
                    ssm_b_re=d_s5[2], ssm_b_im=d_s5[3], ssm_c_re=d_s5[4], ssm_c_im=d_s5[5], ssm_d=d_s5[6],
                    ssm_log_step=d_s5[7], norm_ffn_g=small["norm_ffn_g"])
        for n, val in vals.items():
            small_g[n][l] = val
    small_tree = {n: jnp.stack(small_g[n]) for n in small_g}
    small_tree["norm_final_g"] = dgf[0]
    return loss_part, dx, small_tree


def _pack_small(tree, names):
    flat = jnp.concatenate([tree[n].reshape(-1) for n in names])
    per = -(-flat.shape[0] // (_N_DEV * 128 * 8)) * (128 * 8)
    flat = jnp.pad(flat, (0, _N_DEV * per - flat.shape[0]))
    return flat.reshape(_N_DEV, per // 128, 128)


def _unpack_small(flat, like, names):
    flat = flat.reshape(-1)
    out, off = {}, 0
    for n in names:
        size = math.prod(like[n].shape)
        out[n] = flat[off:off + size].reshape(like[n].shape)
        off += size
    return out


def kernel(x, norm_mix_g, w_in, gate_bias, lru_conv_w, lru_conv_b, lru_wa, lru_ba, lru_wx, lru_bx, lru_lambda, attn_rel_bias, ssm_a_re, ssm_a_im, ssm_b_re, ssm_b_im, ssm_c_re, ssm_c_im, ssm_d, ssm_log_step, ssm_w_glu, w_branch, w_out, norm_ffn_g, w_ffn_gate, w_ffn_up, w_ffn_down, norm_final_g, loss_target, m_norm_mix_g, m_w_in, m_gate_bias, m_lru_conv_w, m_lru_conv_b, m_lru_wa, m_lru_ba, m_lru_wx, m_lru_bx, m_lru_lambda, m_attn_rel_bias, m_ssm_a_re, m_ssm_a_im, m_ssm_b_re, m_ssm_b_im, m_ssm_c_re, m_ssm_c_im, m_ssm_d, m_ssm_log_step, m_ssm_w_glu, m_w_branch, m_w_out, m_norm_ffn_g, m_w_ffn_gate, m_w_ffn_up, m_w_ffn_down, m_norm_final_g, v_norm_mix_g, v_w_in, v_gate_bias, v_lru_conv_w, v_lru_conv_b, v_lru_wa, v_lru_ba, v_lru_wx, v_lru_bx, v_lru_lambda, v_attn_rel_bias, v_ssm_a_re, v_ssm_a_im, v_ssm_b_re, v_ssm_b_im, v_ssm_c_re, v_ssm_c_im, v_ssm_d, v_ssm_log_step, v_ssm_w_glu, v_w_branch, v_w_out, v_norm_ffn_g, v_w_ffn_gate, v_w_ffn_up, v_w_ffn_down, v_norm_final_g):
    args = dict(locals())
    w = {n: args[n] for n in _WEIGHTS}
    m = {n: args["m_" + n] for n in _WEIGHTS}
    v = {n: args["v_" + n] for n in _WEIGHTS}
    depth = w_in.shape[0]
    xc, yc, cc = _my_place()
    jchip = 2 * xc + yc
    cidx = jnp.reshape(cc, (1,)).astype(jnp.int32)
    jidx = jnp.reshape(jchip, (1,)).astype(jnp.int32)
    jcidx = jnp.stack([jchip, cc]).astype(jnp.int32)

    blocks = [w[n] for n in _BIG]
    blocks[2] = blocks[2].reshape(depth * N_BRANCH, MIX_W, -1)
    axes = [_BIG_SHARD_AXIS[n] + 1 for n in _BIG] + [_SMALL_SHARDED[n] for n in _SMALL_SHARDED]
    placed = [_place_shard(b, ax, jidx, BF16, "place_shard") for b, ax in zip(blocks, axes)]
    placed += [_place_shard(w[n], _SMALL_SHARDED[n], jidx, F32, "place_shard") for n in _SMALL_SHARDED]
    gathered = _gather_weights(placed, axes, "gather_weights")
    W = dict(zip(_BIG, gathered))
    sm_full = dict(zip(_SMALL_SHARDED, gathered[len(_BIG):]))
    sm = {n: w[n] for n in _SMALL if n not in _SMALL_SHARDED}
    sm.update(sm_full)

    accs, half_axes_by_tensor, tensor_names = {}, {}, []

    def on_big(l, big):
        names, half_axes = _reduce_big(l, depth, big, accs, cidx, jcidx)
        tensor_names[:] = names
        half_axes_by_tensor.update(zip(names, half_axes))

    loss_part, dx, small_tree = _local_step(x[0], loss_target[0], W, sm, on_big)
    loss = lax.psum(loss_part, ("x", "y", "c"))
    grad_x = dx[None]

    joined = _pair_join([accs[nk] for nk in tensor_names], [half_axes_by_tensor[nk] for nk in tensor_names], "pair_join")
    jd = dict(zip(tensor_names, joined))
    grads = {}
    for n in _BIG:
        if n == "w_branch":
            grads[n] = jnp.stack([jd[(n, k)] for k in range(N_BRANCH)], axis=1)
        else:
            grads[n] = jd[(n, 0)]
    like = {n: (sm_full[n] if n in _SMALL_SHARDED else w[n]) for n in _SMALL}
    red = _unpack_small(_allreduce_small(_pack_small(small_tree, _SMALL), "allreduce_small"), like, _SMALL)
    for n in _SMALL:
        if n in _SMALL_SHARDED:
            size = w[n].shape[2]
            grads[n] = lax.dynamic_slice_in_dim(red[n], (2 * xc + yc) * size, size, axis=2)
        else:
            grads[n] = red[n]

    delta, new_m, new_v = {}, {}, {}
    for n in _BIG:
        shp = w[n].shape
        two = lambda a: a.reshape(-1, shp[-1])
        d_, m_, v_ = _adamw(two(w[n]), two(grads[n]), two(m[n]), two(v[n]), "adamw")
        delta[n], new_m[n], new_v[n] = d_.reshape(shp), m_.reshape(shp), v_.reshape(shp)
    pk = lambda tree: _pack_small(tree, _SMALL).reshape(-1, 128)
    d_, m_, v_ = _adamw(pk(w), pk(grads), pk(m), pk(v), "adamw_small")
    like_local = {n: w[n] for n in _SMALL}
    for tree, flat in ((delta, d_), (new_m, m_), (new_v, v_)):
        tree.update(_unpack_small(flat, like_local, _SMALL))
    return (loss, grad_x, *[grads[n] for n in _WEIGHTS], *[delta[n] for n in _WEIGHTS], *[new_m[n] for n in _WEIGHTS],
            *[new_v[n] for n in _WEIGHTS])
```

```python
import functools
import math

import jax
import jax.numpy as jnp
import numpy as np
from jax import lax
from jax.experimental import pallas as pl
from jax.experimental.pallas import tpu as pltpu

F32 = jnp.float32
BF16 = jnp.bfloat16

D_MODEL = 2048
MIX_W = 1024
N_BRANCH = 3
LRU_BLOCKS = 16
LRU_BW = 64
CONV_W = 4
LRU_C = 8.0
CHUNK = 64
ATT_HEADS = 8
ATT_HD = 128
ATT_LEFT = 8
ATT_BAND = (ATT_LEFT + 1) * CHUNK
MAX_REL = 128
N_REL = 2 * MAX_REL + 1
SSM_G = 64
SSM_H = 16
SSM_P = 64
FFN_H = 5632
IN_W = 6 * MIX_W + N_BRANCH * D_MODEL
NORM_EPS = 1e-6
MASK_VALUE = -1e30
ADAM_LR, ADAM_B1, ADAM_B2, ADAM_EPS, ADAM_WD, ADAM_STEP = 0.001, 0.9, 0.999, 1e-08, 0.01, 10

S5_L = 16
S5_LW = S5_L * SSM_H
N_CHIPS = 4
V7X_VMEM_LIMIT = 56 * 1024 * 1024
HI = lax.Precision.HIGHEST
MESH = pl.DeviceIdType.MESH


def _cparams(sem=None):
    return pltpu.CompilerParams(dimension_semantics=sem, vmem_limit_bytes=V7X_VMEM_LIMIT)


def _pick(n, prefs):
    for p in prefs:
        if n % p == 0:
            return p
    return n


_MM_VMEM_BUDGET = 46 * 1024 * 1024


def _mm_tiles(M, N, K, has_res, out_bytes):
    tn = _pick(N, (1024, 1408, 512, 256, 128))
    for tk in (K, 2048, 1408, 1024, 512, 256, 128):
        if K % tk:
            continue
        for tm in (1024, 512, 256, 128, 64, 32, 16, 8):
            if M % tm:
                continue
            need = 2 * 2 * (tm * tk + tk * tn) + 2 * tm * tn * out_bytes
            need += tm * tn * 4 if tk < K else 0
            need += 2 * tm * tn * 4 if has_res else 0
            if need <= _MM_VMEM_BUDGET:
                return tm, tn, tk
    raise ValueError((M, N, K))


def _call(body, *, name, grid, in_specs, out_specs, out_shape, args, scratch_shapes=(), sem=None, side=None):
    in_specs, out_specs, out_shape = list(in_specs), list(out_specs), list(out_shape)
    scratch_shapes = list(scratch_shapes)
    if side is None:
        outs = pl.pallas_call(body, name=name, out_shape=tuple(out_shape), grid=grid, in_specs=in_specs,
                              out_specs=tuple(out_specs), scratch_shapes=scratch_shapes, compiler_params=_cparams(sem))(*args)
        return tuple(outs), ()
    n_in, n_out, n_scr = len(in_specs), len(out_shape), len(scratch_shapes)
    s_in, s_out = len(side.inputs), len(side.out_shapes)

    def wrapped(*refs):
        mi, refs = refs[:n_in], refs[n_in:]
        si, refs = refs[:s_in], refs[s_in:]
        mo, refs = refs[:n_out], refs[n_out:]
        so, refs = refs[:s_out], refs[s_out:]
        scr, (send, recv) = refs[:n_scr], refs[n_scr:]
        first = functools.reduce(jnp.logical_and, [pl.program_id(d) == 0 for d in range(len(grid))])
        last = functools.reduce(jnp.logical_and, [pl.program_id(d) == g - 1 for d, g in enumerate(grid)])

        @pl.when(first)
        def _():
            side.start(si, so, send, recv)

        body(*mi, *mo, *scr)

        @pl.when(last)
        def _():
            side.finish(si, so, send, recv)

    outs = pl.pallas_call(
        wrapped, name=name, out_shape=tuple(out_shape + list(side.out_shapes)), grid=grid,
        in_specs=in_specs + [_ANY] * s_in, out_specs=tuple(out_specs + [_ANY] * s_out),
        scratch_shapes=scratch_shapes + [pltpu.SemaphoreType.DMA((side.n_sems,)), pltpu.SemaphoreType.DMA((side.n_sems,))],
        input_output_aliases={n_in + i: n_out + o for i, o in side.aliases.items()},
        compiler_params=_cparams(("arbitrary",) * len(grid)),
    )(*args, *side.inputs)
    return tuple(outs[:n_out]), tuple(outs[n_out:])


def _mm(a, b, *, M, N, K, name, ta=False, tb=False, a_lead=None, b_lead=None, a_off=(0, 0), b_off=(0, 0),
        out_dtype=F32, res=None, tm=None, tn=None, tk=None, side=None):
    if tm is None and tn is None and tk is None:
        tm, tn, tk = _mm_tiles(M, N, K, res is not None, jnp.dtype(out_dtype).itemsize)
    nk = K // tk

    def spec(blk, lead, off, order):
        r0, c0 = off[0] // blk[0], off[1] // blk[1]
        assert off[0] % blk[0] == 0 and off[1] % blk[1] == 0
        if lead is None:
            return pl.BlockSpec(blk, lambda i, j, k: (r0 + order(i, j, k)[0], c0 + order(i, j, k)[1]))
        return pl.BlockSpec((None,) + blk, lambda i, j, k: (lead, r0 + order(i, j, k)[0], c0 + order(i, j, k)[1]))

    a_spec = spec((tk, tm), a_lead, a_off, lambda i, j, k: (k, i)) if ta else spec((tm, tk), a_lead, a_off, lambda i, j, k: (i, k))
    b_spec = spec((tn, tk), b_lead, b_off, lambda i, j, k: (j, k)) if tb else spec((tk, tn), b_lead, b_off, lambda i, j, k: (k, j))
    dims = (((0 if ta else 1,), (1 if tb else 0,)), ((), ()))
    in_specs = [a_spec, b_spec]
    args = [a, b]
    if res is not None:
        in_specs.append(pl.BlockSpec((tm, tn), lambda i, j, k: (i, j)))
        args.append(res)

    def body(*refs):
        a_ref, b_ref = refs[:2]
        r_ref = refs[2] if res is not None else None
        o_ref = refs[3] if res is not None else refs[2]

        def dot():
            return lax.dot_general(a_ref[...], b_ref[...], dims, preferred_element_type=F32)

        def finish(r):
            if r_ref is not None:
                r = r + r_ref[...]
            o_ref[...] = r.astype(out_dtype)

        if nk == 1:
            finish(dot())
            return
        acc = refs[-1]
        k = pl.program_id(2)

        @pl.when(k == 0)
        def _():
            acc[...] = dot()

        @pl.when(jnp.logical_and(k > 0, k < nk - 1))
        def _():
            acc[...] += dot()

        @pl.when(k == nk - 1)
        def _():
            finish(acc[...] + dot())

    (out,), got = _call(
        body, name=name, out_shape=[jax.ShapeDtypeStruct((M, N), out_dtype)],
        grid=(M // tm, N // tn, nk), in_specs=in_specs, out_specs=[pl.BlockSpec((tm, tn), lambda i, j, k: (i, j))],
        scratch_shapes=[pltpu.VMEM((tm, tn), F32)] if nk > 1 else [],
        sem=("parallel", "parallel", "arbitrary"), args=args, side=side)
    return out if side is None else (out, got)


def _rowwise(fn, rows, consts, out_rows, out_accs, *, R, name, tm=256, rs=16, cs=None):
    tm = min(tm, R)
    assert R % tm == 0 and tm % rs == 0
    n_r, n_c, n_o, n_a = len(rows), len(consts), len(out_rows), len(out_accs)
    nsteps = R // tm
    widths = [w for _, _, w in rows]
    if cs is not None:
        assert all(w == widths[0] for w in widths) and widths[0] % cs == 0
        col_chunks = [(c0, cs) for c0 in range(0, widths[0], cs)]
    else:
        col_chunks = [None]

    def body(*refs):
        r_refs = refs[:n_r]
        c_refs = refs[n_r:n_r + n_c]
        o_refs = refs[n_r + n_c:n_r + n_c + n_o]
        a_refs = refs[n_r + n_c + n_o:n_r + n_c + n_o + n_a]
        s_refs = refs[n_r + n_c + n_o + n_a:]
        i = pl.program_id(0)

        @pl.when(i == 0)
        def _():
            for s in s_refs:
                s[...] = jnp.zeros_like(s)

        def piece(g, carry):
            r0 = pl.multiple_of(g * rs, rs)
            for cc in col_chunks:
                csl = slice(None) if cc is None else slice(cc[0], cc[0] + cc[1])
                rp = [r[pl.ds(r0, rs), csl] for r in r_refs]
                cp = [c[:, csl] for c in c_refs]
                outs, accs = fn(rp, cp)
                for o_ref, o in zip(o_refs, outs):
                    o_ref[pl.ds(r0, rs), csl] = o.astype(o_ref.dtype)
                for s_ref, av in zip(s_refs, accs):
                    s_ref[:, csl] += av
            return carry

        lax.fori_loop(0, tm // rs, piece, 0)

        @pl.when(i == nsteps - 1)
        def _():
            for a_ref, s_ref in zip(a_refs, s_refs):
                a_ref[...] = jnp.sum(s_ref[...], axis=0, keepdims=True)

    in_specs = [pl.BlockSpec((tm, w), functools.partial(lambda i, cb: (i, cb), cb=off // w)) for _, off, w in rows]
    for _, off, w in rows:
        assert off % w == 0
    in_specs += [pl.BlockSpec(c.shape, lambda i: (0, 0)) for c in consts]
    out_specs = [pl.BlockSpec((tm, w), lambda i: (i, 0)) for w, _ in out_rows]
    out_specs += [pl.BlockSpec((1, w), lambda i: (0, 0)) for w in out_accs]
    out_shape = [jax.ShapeDtypeStruct((R, w), dt) for w, dt in out_rows]
    out_shape += [jax.ShapeDtypeStruct((1, w), F32) for w in out_accs]
    return pl.pallas_call(
        body, name=name, out_shape=tuple(out_shape), grid=(nsteps,), in_specs=in_specs, out_specs=tuple(out_specs),
        scratch_shapes=[pltpu.VMEM((rs, w), F32) for w in out_accs],
        compiler_params=_cparams(("arbitrary",)),
    )(*[r for r, _, _ in rows], *consts)


def _rms(x, g):
    r = lax.rsqrt(jnp.mean(x * x, axis=-1, keepdims=True) + NORM_EPS)
    return x * r * g


def _rms_fwd(x, g, name):
    R = x.shape[0]

    def fn(rp, cp):
        return [_rms(rp[0], cp[0])], []

    return _rowwise(fn, [(x, 0, D_MODEL)], [g], [(D_MODEL, BF16)], [], R=R, name=name)[0]


def _rms_bwd(x, g, dh, dres, name):
    R = x.shape[0]

    def fn(rp, cp):
        xv, dhv, drv = rp
        _, pull = jax.vjp(_rms, xv, jnp.broadcast_to(cp[0], xv.shape))
        dx, dgv = pull(dhv)
        return [drv + dx], [dgv]

    dx, dg = _rowwise(fn, [(x, 0, D_MODEL), (dh, 0, D_MODEL), (dres, 0, D_MODEL)], [g], [(D_MODEL, F32)], [D_MODEL],
                      R=R, name=name, rs=8)
    return dx, dg


def _final_loss(x, g, tgt, name):
    R = x.shape[0]

    def loss_rows(xv, gv, tv):
        e = _rms(xv, gv) - tv
        return 0.5 * jnp.mean(e * e, axis=-1, keepdims=True)

    def fn(rp, cp):
        xv, tv = rp
        lr, pull = jax.vjp(lambda a, b: loss_rows(a, b, tv), xv, jnp.broadcast_to(cp[0], xv.shape))
        dx, dgv = pull(jnp.ones_like(lr))
        return [dx], [dgv, jnp.broadcast_to(lr, (lr.shape[0], 128))]

    dx, dg, lsum = _rowwise(fn, [(x, 0, D_MODEL), (tgt, 0, D_MODEL)], [g], [(D_MODEL, F32)], [D_MODEL, 128],
                            R=R, name=name, rs=8)
    return lsum[0, 0], dx, dg


def _neg_expm1(z):
    u = jnp.exp(z)
    safe = jnp.where(u == 1.0, 0.5, u)
    return -jnp.where(u == 1.0, z, (safe - 1.0) * z / jnp.log(safe))


def _lru_ab(xc, pr, pi, lam):
    r = jax.nn.sigmoid(pr)
    i = jax.nn.sigmoid(pi)
    log_a = -LRU_C * r * jax.nn.softplus(-lam)
    a = jnp.exp(log_a)
    b = jnp.sqrt(_neg_expm1(2.0 * log_a)) * (i * xc)
    return a, b


def _gated(h, gate):
    return h * jax.nn.gelu(gate)


def _row_iota8(w):
    return lax.broadcasted_iota(jnp.int32, (8, w), 0)


def _shift_dn(x, halo, s):
    xs = pltpu.roll(x, s, 0)
    hs = pltpu.roll(halo, s, 0)
    first = jnp.where(_row_iota8(x.shape[1]) < s, hs, xs[0:8])
    return jnp.concatenate([first, xs[8:]], axis=0) if x.shape[0] > 8 else first


def _shift_up(x, nxt, s):
    n = x.shape[0]
    xs = pltpu.roll(x, n - s, 0)
    ns = pltpu.roll(nxt, 8 - s, 0)
    last = jnp.where(_row_iota8(x.shape[1]) >= 8 - s, ns, xs[n - 8:])
    return jnp.concatenate([xs[:n - 8], last], axis=0) if n > 8 else last


def _lru_tiles(T):
    tT = min(256, T)
    return tT, T // tT


def _lru_fwd(proj, cw, cb, wax, bax, lam, name):
    T = proj.shape[0]
    W = MIX_W
    tT, nT = _lru_tiles(T)

    def body(x_ref, xh_ref, gt_ref, cw_ref, cb_ref, wax_ref, bax_ref, lam_ref, y_ref, h_ref, a_s, b_s, hc_s):
        i = pl.program_id(0)

        @pl.when(i == 0)
        def _():
            hc_s[...] = jnp.zeros_like(hc_s)

        x = x_ref[...]
        halo = jnp.where(i > 0, xh_ref[...], 0.0)
        w = cw_ref[...]
        xc = (cb_ref[...] + w[3:4] * x + w[2:3] * _shift_dn(x, halo, 1) + w[1:2] * _shift_dn(x, halo, 2)
              + w[0:1] * _shift_dn(x, halo, 3))
        pre = jnp.dot(xc.astype(BF16), wax_ref[...], preferred_element_type=F32) + bax_ref[...]
        a, b = _lru_ab(xc, pre[:, :W], pre[:, W:], lam_ref[...])
        a_s[...] = a
        b_s[...] = b
        row = _row_iota8(W)

        def grp(gi, hprev):
            r0 = pl.multiple_of(gi * 8, 8)
            A = a_s[pl.ds(r0, 8), :]
            B = b_s[pl.ds(r0, 8), :]
            for s in (1, 2, 4):
                As = pltpu.roll(A, s, 0)
                Bs = pltpu.roll(B, s, 0)
                m = row >= s
                B = jnp.where(m, A * Bs + B, B)
                A = jnp.where(m, A * As, A)
            H = A * hprev + B
            h_ref[pl.ds(r0, 8), :] = H
            return H[7:8, :]

        hc_s[0:1, :] = lax.fori_loop(0, tT // 8, grp, hc_s[0:1, :])
        y_ref[...] = _gated(h_ref[...], gt_ref[...]).astype(BF16)

    hb = tT // 8
    return pl.pallas_call(
        body, name=name,
        out_shape=(jax.ShapeDtypeStruct((T, W), BF16), jax.ShapeDtypeStruct((T, W), F32)),
        grid=(nT,),
        in_specs=[pl.BlockSpec((tT, W), lambda i: (i, 0)),
                  pl.BlockSpec((8, W), lambda i: (jnp.maximum(i * hb - 1, 0), 0)),
                  pl.BlockSpec((tT, W), lambda i: (i, 1)),
                  pl.BlockSpec((CONV_W, W), lambda i: (0, 0)), pl.BlockSpec((1, W), lambda i: (0, 0)),
                  pl.BlockSpec((W, 2 * W), lambda i: (0, 0)), pl.BlockSpec((1, 2 * W), lambda i: (0, 0)),
                  pl.BlockSpec((1, W), lambda i: (0, 0))],
        out_specs=(pl.BlockSpec((tT, W), lambda i: (i, 0)), pl.BlockSpec((tT, W), lambda i: (i, 0))),
        scratch_shapes=[pltpu.VMEM((tT, W), F32), pltpu.VMEM((tT, W), F32), pltpu.VMEM((8, W), F32)],
        compiler_params=_cparams(("arbitrary",)),
    )(proj, proj, proj, cw, cb, wax, bax, lam)


def _lru_bwd(proj, h, dy, cw, cb, wax, bax, lam, name):
    T = proj.shape[0]
    W = MIX_W
    tT, nT = _lru_tiles(T)
    hb = tT // 8

    def body(x_ref, xh_ref, gt_ref, h_ref, hh_ref, dy_ref, cw_ref, cb_ref, wax_ref, bax_ref, lam_ref,
             dx_ref, dgt_ref, dcw_ref, dcb_ref, dwax_ref, dbax_ref, dlam_ref,
             al_s, be_s, d_s, ca_s, cd_s, cx_s):
        i = pl.program_id(0)
        ib = nT - 1 - i

        @pl.when(i == 0)
        def _():
            for r in (ca_s, cd_s, cx_s, dcw_ref, dcb_ref, dwax_ref, dbax_ref, dlam_ref):
                r[...] = jnp.zeros_like(r)

        x = x_ref[...]
        halo = jnp.where(ib > 0, xh_ref[...], 0.0)
        w = cw_ref[...]
        x1, x2, x3 = _shift_dn(x, halo, 1), _shift_dn(x, halo, 2), _shift_dn(x, halo, 3)
        xc = cb_ref[...] + w[3:4] * x + w[2:3] * x1 + w[1:2] * x2 + w[0:1] * x3
        xcb = xc.astype(BF16)
        pre = jnp.dot(xcb, wax_ref[...], preferred_element_type=F32) + bax_ref[...]
        (a, _), pull_ab = jax.vjp(_lru_ab, xc, pre[:, :W], pre[:, W:], lam_ref[...])
        hv = h_ref[...]
        hprev = _shift_dn(hv, jnp.where(ib > 0, hh_ref[...], 0.0), 1)
        _, pull_y = jax.vjp(_gated, hv, gt_ref[...])
        dh_out, dgt = pull_y(dy_ref[...])
        dgt_ref[...] = dgt.astype(BF16)
        al_s[...] = _shift_up(a, ca_s[...], 1)
        be_s[...] = dh_out
        row = _row_iota8(W)
        ng = tT // 8

        def grp(k, dnext):
            r0 = pl.multiple_of((ng - 1 - k) * 8, 8)
            A = al_s[pl.ds(r0, 8), :]
            B = be_s[pl.ds(r0, 8), :]
            for s in (1, 2, 4):
                As = pltpu.roll(A, 8 - s, 0)
                Bs = pltpu.roll(B, 8 - s, 0)
                m = row < 8 - s
                B = jnp.where(m, A * Bs + B, B)
                A = jnp.where(m, A * As, A)
            Dg = A * dnext + B
            d_s[pl.ds(r0, 8), :] = Dg
            return Dg[0:1, :]

        lax.fori_loop(0, ng, grp, cd_s[0:1, :])
        Dv = d_s[...]
        dxc1, dpr, dpi, dlam = pull_ab((Dv * hprev, Dv))
        dpre = jnp.concatenate([dpr, dpi], axis=1)
        dpb = dpre.astype(BF16)
        dxc = dxc1 + lax.dot_general(dpb, wax_ref[...], (((1,), (1,)), ((), ())), preferred_element_type=F32)
        dwax_ref[...] += lax.dot_general(xcb, dpb, (((0,), (0,)), ((), ())), preferred_element_type=F32)
        dbax_ref[...] += jnp.sum(dpre, axis=0, keepdims=True)
        dlam_ref[...] += dlam
        dcb_ref[...] += jnp.sum(dxc, axis=0, keepdims=True)
        dcw_ref[...] += jnp.concatenate([jnp.sum(dxc * x3, axis=0, keepdims=True), jnp.sum(dxc * x2, axis=0, keepdims=True),
                                         jnp.sum(dxc * x1, axis=0, keepdims=True), jnp.sum(dxc * x, axis=0, keepdims=True)], axis=0)
        nxt = cx_s[...]
        dx = (w[3:4] * dxc + w[2:3] * _shift_up(dxc, nxt, 1) + w[1:2] * _shift_up(dxc, nxt, 2)
              + w[0:1] * _shift_up(dxc, nxt, 3))
        dx_ref[...] = dx.astype(BF16)
        ca_s[...] = a[0:8]
        cd_s[...] = Dv[0:8]
        cx_s[...] = dxc[0:8]

    rev = lambda i: nT - 1 - i
    const = lambda shape: pl.BlockSpec(shape, lambda i: (0, 0))
    return pl.pallas_call(
        body, name=name,
        out_shape=(jax.ShapeDtypeStruct((T, W), BF16), jax.ShapeDtypeStruct((T, W), BF16),
                   jax.ShapeDtypeStruct((CONV_W, W), F32), jax.ShapeDtypeStruct((1, W), F32),
                   jax.ShapeDtypeStruct((W, 2 * W), F32), jax.ShapeDtypeStruct((1, 2 * W), F32),
                   jax.ShapeDtypeStruct((1, W), F32)),
        grid=(nT,),
        in_specs=[pl.BlockSpec((tT, W), lambda i: (rev(i), 0)),
                  pl.BlockSpec((8, W), lambda i: (jnp.maximum(rev(i) * hb - 1, 0), 0)),
                  pl.BlockSpec((tT, W), lambda i: (rev(i), 1)),
                  pl.BlockSpec((tT, W), lambda i: (rev(i), 0)),
                  pl.BlockSpec((8, W), lambda i: (jnp.maximum(rev(i) * hb - 1, 0), 0)),
                  pl.BlockSpec((tT, W), lambda i: (rev(i), 0)),
                  const((CONV_W, W)), const((1, W)), const((W, 2 * W)), const((1, 2 * W)), const((1, W))],
        out_specs=(pl.BlockSpec((tT, W), lambda i: (rev(i), 0)), pl.BlockSpec((tT, W), lambda i: (rev(i), 0)),
                   const((CONV_W, W)), const((1, W)), const((W, 2 * W)), const((1, 2 * W)), const((1, W))),
        scratch_shapes=[pltpu.VMEM((tT, W), F32), pltpu.VMEM((tT, W), F32), pltpu.VMEM((tT, W), F32),
                        pltpu.VMEM((8, W), F32), pltpu.VMEM((8, W), F32), pltpu.VMEM((8, W), F32)],
        compiler_params=_cparams(("arbitrary",)),
    )(proj, proj, proj, h, h, dy, cw, cb, wax, bax, lam)


def _lru_prep(wa, wx, ba, bx):
    eye = jnp.eye(LRU_BLOCKS, dtype=F32)

    def dense(wb):
        return (wb[:, :, None, :] * eye[:, None, :, None]).reshape(MIX_W, MIX_W)

    wax = jnp.concatenate([dense(wa), dense(wx)], axis=1)
    bax = jnp.concatenate([ba, bx])[None, :]
    return wax, bax


_BVEC_W = 640


def _attn_bias(rel_bias):
    n_far = ATT_LEFT * CHUNK - MAX_REL + CHUNK
    far = jnp.broadcast_to(rel_bias[:, 2 * MAX_REL:], (ATT_HEADS, n_far))
    near = rel_bias[:, MAX_REL - (CHUNK - 1):2 * MAX_REL][:, ::-1]
    pad = jnp.zeros((ATT_HEADS, _BVEC_W - (CHUNK - 1 + ATT_BAND)), F32)
    return jnp.concatenate([far, near, pad], axis=1)[:, None, :]


def _bias_table(bvec_row):
    full = pltpu.roll(jnp.broadcast_to(bvec_row, (CHUNK, _BVEC_W)), _BVEC_W - (CHUNK - 1), 1, stride=1, stride_axis=0)
    return full[:, :ATT_BAND]


def _bias_table_t(ds):
    dpad = jnp.concatenate([ds, jnp.zeros((CHUNK, _BVEC_W - ATT_BAND), F32)], axis=1)
    r = lax.broadcasted_iota(jnp.int32, (CHUNK, CHUNK), 0)
    c = lax.broadcasted_iota(jnp.int32, (CHUNK, CHUNK), 1)
    rev = jnp.dot((r + c == CHUNK - 1).astype(F32), dpad, preferred_element_type=F32, precision=HI)
    return jnp.sum(pltpu.roll(rev, 0, 1, stride=1, stride_axis=0), axis=0, keepdims=True)


_NT = (((1,), (1,)), ((), ()))
_TN = (((0,), (0,)), ((), ()))
_ATT_PAD = ATT_LEFT * CHUNK
_Q_BLK, _K_BLK, _V_BLK = 2 * MIX_W // ATT_HD, 3 * MIX_W // ATT_HD, 4 * MIX_W // ATT_HD


def _attn_probs(q, kb, bias, c):
    s = lax.dot_general(q, kb, _NT, preferred_element_type=F32) * (ATT_HD ** -0.5) + bias
    kpos = lax.broadcasted_iota(jnp.int32, s.shape, 1)
    s = jnp.where(kpos >= (ATT_LEFT - c) * CHUNK, s, MASK_VALUE)
    e = jnp.exp(s - jnp.max(s, axis=-1, keepdims=True))
    return e / jnp.sum(e, axis=-1, keepdims=True)


def _attn_fwd(proj, bias, name, side=None):
    T = proj.shape[0]
    nC = T // CHUNK

    def body(q_ref, k_ref, v_ref, b_ref, o_ref, kp, vp):
        kp[0:_ATT_PAD, :] = jnp.zeros((_ATT_PAD, ATT_HD), BF16)
        vp[0:_ATT_PAD, :] = jnp.zeros((_ATT_PAD, ATT_HD), BF16)
        kp[_ATT_PAD:, :] = k_ref[...].astype(BF16)
        vp[_ATT_PAD:, :] = v_ref[...].astype(BF16)
        bias_v = _bias_table(b_ref[0])

        def chunk(c, carry):
            r0 = pl.multiple_of(c * CHUNK, CHUNK)
            q = q_ref[pl.ds(r0, CHUNK), :].astype(BF16)
            p = _attn_probs(q, kp[pl.ds(r0, ATT_BAND), :], bias_v, c)
            o = jnp.dot(p.astype(BF16), vp[pl.ds(r0, ATT_BAND), :], preferred_element_type=F32)
            o_ref[pl.ds(r0, CHUNK), :] = o.astype(BF16)
            return carry

        lax.fori_loop(0, nC, chunk, 0)

    (out,), got = _call(
        body, name=name, out_shape=[jax.ShapeDtypeStruct((T, MIX_W), BF16)], grid=(ATT_HEADS,),
        in_specs=[pl.BlockSpec((T, ATT_HD), lambda h: (0, _Q_BLK + h)), pl.BlockSpec((T, ATT_HD), lambda h: (0, _K_BLK + h)),
                  pl.BlockSpec((T, ATT_HD), lambda h: (0, _V_BLK + h)), pl.BlockSpec((1, 1, _BVEC_W), lambda h: (h, 0, 0))],
        out_specs=[pl.BlockSpec((T, ATT_HD), lambda h: (0, h))],
        scratch_shapes=[pltpu.VMEM((T + _ATT_PAD, ATT_HD), BF16), pltpu.VMEM((T + _ATT_PAD, ATT_HD), BF16)],
        sem=("arbitrary",), args=(proj, proj, proj, bias), side=side)
    return out if side is None else (out, got)


def _attn_bwd(proj, bias, do, name, side=None):
    T = proj.shape[0]
    nC = T // CHUNK

    def body(q_ref, k_ref, v_ref, b_ref, do_ref, dq_ref, dk_ref, dv_ref, db_ref, kp, vp, dkp, dvp, dbs):
        kp[0:_ATT_PAD, :] = jnp.zeros((_ATT_PAD, ATT_HD), BF16)
        vp[0:_ATT_PAD, :] = jnp.zeros((_ATT_PAD, ATT_HD), BF16)
        kp[_ATT_PAD:, :] = k_ref[...].astype(BF16)
        vp[_ATT_PAD:, :] = v_ref[...].astype(BF16)
        dkp[...] = jnp.zeros_like(dkp)
        dvp[...] = jnp.zeros_like(dvp)
        dbs[...] = jnp.zeros_like(dbs)
        bias_v = _bias_table(b_ref[0])

        def chunk(c, carry):
            r0 = pl.multiple_of(c * CHUNK, CHUNK)
            q = q_ref[pl.ds(r0, CHUNK), :].astype(BF16)
            kb = kp[pl.ds(r0, ATT_BAND), :]
            vb = vp[pl.ds(r0, ATT_BAND), :]
            dob = do_ref[pl.ds(r0, CHUNK), :].astype(BF16)
            p = _attn_probs(q, kb, bias_v, c)
            dp = lax.dot_general(dob, vb, _NT, preferred_element_type=F32)
            ds = p * (dp - jnp.sum(p * dp, axis=-1, keepdims=True))
            dbs[...] += ds
            dsb = (ds * (ATT_HD ** -0.5)).astype(BF16)
            dq_ref[pl.ds(r0, CHUNK), :] = jnp.dot(dsb, kb, preferred_element_type=F32).astype(BF16)
            dkp[pl.ds(r0, ATT_BAND), :] += lax.dot_general(dsb, q, _TN, preferred_element_type=F32)
            dvp[pl.ds(r0, ATT_BAND), :] += lax.dot_general(p.astype(BF16), dob, _TN, preferred_element_type=F32)
            return carry

        lax.fori_loop(0, nC, chunk, 0)
        dk_ref[...] = dkp[_ATT_PAD:, :].astype(BF16)
        dv_ref[...] = dvp[_ATT_PAD:, :].astype(BF16)
        db_ref[0] = _bias_table_t(dbs[...])

    hspec = pl.BlockSpec((T, ATT_HD), lambda h: (0, h))
    osd = jax.ShapeDtypeStruct((T, MIX_W), BF16)
    outs, got = _call(
        body, name=name,
        out_shape=[osd, osd, osd, jax.ShapeDtypeStruct((ATT_HEADS, 1, _BVEC_W), F32)], grid=(ATT_HEADS,),
        in_specs=[pl.BlockSpec((T, ATT_HD), lambda h: (0, _Q_BLK + h)), pl.BlockSpec((T, ATT_HD), lambda h: (0, _K_BLK + h)),
                  pl.BlockSpec((T, ATT_HD), lambda h: (0, _V_BLK + h)), pl.BlockSpec((1, 1, _BVEC_W), lambda h: (h, 0, 0)),
                  hspec],
        out_specs=[hspec, hspec, hspec, pl.BlockSpec((1, 1, _BVEC_W), lambda h: (h, 0, 0))],
        scratch_shapes=[pltpu.VMEM((T + _ATT_PAD, ATT_HD), BF16), pltpu.VMEM((T + _ATT_PAD, ATT_HD), BF16),
                        pltpu.VMEM((T + _ATT_PAD, ATT_HD), F32), pltpu.VMEM((T + _ATT_PAD, ATT_HD), F32),
                        pltpu.VMEM((CHUNK, ATT_BAND), F32)],
        sem=("arbitrary",), args=(proj, proj, proj, bias, do), side=side)
    return outs if side is None else (outs, got)


def _s5_prep(a_re, a_im, b_re, b_im, c_re, c_im, d, log_step):
    step = jnp.exp(log_step)[:, None]
    mag = jnp.exp(a_re * step)
    ang = a_im * step
    lb_re = mag * jnp.cos(ang)
    lb_im = mag * jnp.sin(ang)
    den = a_re * a_re + a_im * a_im
    nr = lb_re - 1.0
    coef_re = (nr * a_re + lb_im * a_im) / den
    coef_im = (lb_im * a_re - nr * a_im) / den
    bb_re = coef_re[..., None] * b_re - coef_im[..., None] * b_im
    bb_im = coef_re[..., None] * b_im + coef_im[..., None] * b_re
    prs, pis = [jnp.ones_like(lb_re)], [jnp.zeros_like(lb_re)]
    for _ in range(S5_L):
        prs.append(prs[-1] * lb_re - pis[-1] * lb_im)
        pis.append(prs[-2] * lb_im + pis[-1] * lb_re)
    PR, PI = jnp.stack(prs), jnp.stack(pis)
    cl_re = c_re[None] * PR[:, :, None, :] - c_im[None] * PI[:, :, None, :]
    cl_im = c_re[None] * PI[:, :, None, :] + c_im[None] * PR[:, :, None, :]
    kt = (jnp.einsum("tghp,gpk->tghk", cl_re[:S5_L], bb_re, precision=HI)
          - jnp.einsum("tghp,gpk->tghk", cl_im[:S5_L], bb_im, precision=HI))
    steps = np.arange(S5_L)
    lag = (steps[:, None, None] == steps[None, None, :] - steps[None, :, None]).astype(np.float32)
    tgt = jnp.einsum("tpl,tghk->gpklh", lag, kt, precision=HI).reshape(SSM_G, S5_LW, S5_LW)
    prr, pir = PR[:S5_L][::-1], PI[:S5_L][::-1]
    bret = (prr[:, :, None, :] * bb_re.transpose(0, 2, 1)[None] - pir[:, :, None, :] * bb_im.transpose(0, 2, 1)[None])
    bimt = (prr[:, :, None, :] * bb_im.transpose(0, 2, 1)[None] + pir[:, :, None, :] * bb_re.transpose(0, 2, 1)[None])
    bret = bret.transpose(1, 0, 2, 3).reshape(SSM_G, S5_LW, SSM_P)
    bimt = bimt.transpose(1, 0, 2, 3).reshape(SSM_G, S5_LW, SSM_P)
    cre = cl_re[1:].transpose(1, 3, 0, 2).reshape(SSM_G, SSM_P, S5_LW)
    cim = (-cl_im[1:]).transpose(1, 3, 0, 2).reshape(SSM_G, SSM_P, S5_LW)
    dflat = jnp.broadcast_to(d.reshape(SSM_G, 1, SSM_H), (SSM_G, S5_L, SSM_H)).reshape(SSM_G, 1, S5_LW)
    return tgt, bret, bimt, cre, cim, PR[S5_L], PI[S5_L], dflat


_S5_GB = 8


def _bdot(a, b, dims):
    return lax.dot_general(a, b, dims, preferred_element_type=F32, precision=HI)


_B_NN = (((2,), (1,)), ((0,), (0,)))
_B_NT = (((2,), (2,)), ((0,), (0,)))
_B_TN = (((1,), (1,)), ((0,), (0,)))


def _gspec(shape):
    return pl.BlockSpec((_S5_GB,) + shape, lambda g: (g, 0, 0))


def _s5_in(u, bret, bimt, name):
    C = u.shape[1]

    def body(u_ref, br_ref, bi_ref, sr_ref, si_ref):
        uv = u_ref[...]
        sr_ref[...] = _bdot(uv, br_ref[...], _B_NN)
        si_ref[...] = _bdot(uv, bi_ref[...], _B_NN)

    sd = jax.ShapeDtypeStruct((SSM_G, C, SSM_P), F32)
    return pl.pallas_call(
        body, name=name, out_shape=(sd, sd), grid=(SSM_G // _S5_GB,),
        in_specs=[_gspec((C, S5_LW)), _gspec((S5_LW, SSM_P)), _gspec((S5_LW, SSM_P))],
        out_specs=(_gspec((C, SSM_P)), _gspec((C, SSM_P))), compiler_params=_cparams(("parallel",)),
    )(u, bret, bimt)


def _s5_scan(sin_re, sin_im, lr, li, name):
    C = sin_re.shape[0]

    def body(ir_ref, ii_ref, lr_ref, li_ref, or_ref, oi_ref):
        lrv, liv = lr_ref[...], li_ref[...]

        def step(c, s):
            sr, si = s
            or_ref[c] = sr
            oi_ref[c] = si
            return lrv * sr - liv * si + ir_ref[c], lrv * si + liv * sr + ii_ref[c]

        z = jnp.zeros((SSM_G, SSM_P), F32)
        lax.fori_loop(0, C, step, (z, z))

    sd = jax.ShapeDtypeStruct((C, SSM_G, SSM_P), F32)
    return pl.pallas_call(body, name=name, out_shape=(sd, sd), compiler_params=_cparams())(sin_re, sin_im, lr, li)


def _s5_out(u, sp_re, sp_im, tgt, cre, cim, dflat, name):
    C = u.shape[1]

    def body(u_ref, sr_ref, si_ref, t_ref, cr_ref, ci_ref, d_ref, pre_ref, y_ref):
        uv = u_ref[...]
        pre = (_bdot(uv, t_ref[...], _B_NN) + _bdot(sr_ref[...], cr_ref[...], _B_NN)
               + _bdot(si_ref[...], ci_ref[...], _B_NN) + d_ref[...] * uv)
        pre_ref[...] = pre
        y_ref[...] = jax.nn.gelu(pre).astype(BF16)

    return pl.pallas_call(
        body, name=name,
        out_shape=(jax.ShapeDtypeStruct((SSM_G, C, S5_LW), F32), jax.ShapeDtypeStruct((SSM_G, C, S5_LW), BF16)),
        grid=(SSM_G // _S5_GB,),
        in_specs=[_gspec((C, S5_LW)), _gspec((C, SSM_P)), _gspec((C, SSM_P)), _gspec((S5_LW, S5_LW)),
                  _gspec((SSM_P, S5_LW)), _gspec((SSM_P, S5_LW)), _gspec((1, S5_LW))],
        out_specs=(_gspec((C, S5_LW)), _gspec((C, S5_LW))), compiler_params=_cparams(("parallel",)),
    )(u, sp_re, sp_im, tgt, cre, cim, dflat)


def _s5_bwd_out(dy, pre, u, sp_re, sp_im, cre, cim, name):
    C = u.shape[1]

    def body(dy_ref, pre_ref, u_ref, sr_ref, si_ref, cr_ref, ci_ref,
             dpre_ref, dsr_ref, dsi_ref, dt_ref, dcr_ref, dci_ref, dd_ref):
        _, pull = jax.vjp(jax.nn.gelu, pre_ref[...])
        dpre = pull(dy_ref[...])[0]
        uv = u_ref[...]
        dpre_ref[...] = dpre
        dsr_ref[...] = _bdot(dpre, cr_ref[...], _B_NT)
        dsi_ref[...] = _bdot(dpre, ci_ref[...], _B_NT)
        dt_ref[...] = _bdot(uv, dpre, _B_TN)
        dcr_ref[...] = _bdot(sr_ref[...], dpre, _B_TN)
        dci_ref[...] = _bdot(si_ref[...], dpre, _B_TN)
        dd_ref[...] = jnp.sum(dpre * uv, axis=1, keepdims=True)

    sd = jax.ShapeDtypeStruct
    return pl.pallas_call(
        body, name=name,
        out_shape=(sd((SSM_G, C, S5_LW), F32), sd((SSM_G, C, SSM_P), F32), sd((SSM_G, C, SSM_P), F32),
                   sd((SSM_G, S5_LW, S5_LW), F32), sd((SSM_G, SSM_P, S5_LW), F32), sd((SSM_G, SSM_P, S5_LW), F32),
                   sd((SSM_G, 1, S5_LW), F32)),
        grid=(SSM_G // _S5_GB,),
        in_specs=[_gspec((C, S5_LW)), _gspec((C, S5_LW)), _gspec((C, S5_LW)), _gspec((C, SSM_P)), _gspec((C, SSM_P)),
                  _gspec((SSM_P, S5_LW)), _gspec((SSM_P, S5_LW))],
        out_specs=(_gspec((C, S5_LW)), _gspec((C, SSM_P)), _gspec((C, SSM_P)), _gspec((S5_LW, S5_LW)),
                   _gspec((SSM_P, S5_LW)), _gspec((SSM_P, S5_LW)), _gspec((1, S5_LW))),
        compiler_params=_cparams(("parallel",)),
    )(dy, pre, u, sp_re, sp_im, cre, cim)


def _s5_rscan(dsp_re, dsp_im, sp_re, sp_im, lr, li, name):
    C = dsp_re.shape[0]

    def body(gr_ref, gi_ref, sr_ref, si_ref, lr_ref, li_ref, or_ref, oi_ref, dlr_ref, dli_ref):
        lrv, liv = lr_ref[...], li_ref[...]

        def step(k, carry):
            c = C - 1 - k
            dr, di, alr, ali = carry
            or_ref[c] = dr
            oi_ref[c] = di
            sr, si = sr_ref[c], si_ref[c]
            alr = alr + dr * sr + di * si
            ali = ali + di * sr - dr * si
            return gr_ref[c] + lrv * dr + liv * di, gi_ref[c] + lrv * di - liv * dr, alr, ali

        z = jnp.zeros((SSM_G, SSM_P), F32)
        _, _, alr, ali = lax.fori_loop(0, C, step, (z, z, z, z))
        dlr_ref[...] = alr
        dli_ref[...] = ali

    sd = jax.ShapeDtypeStruct((C, SSM_G, SSM_P), F32)
    sp = jax.ShapeDtypeStruct((SSM_G, SSM_P), F32)
    return pl.pallas_call(body, name=name, out_shape=(sd, sd, sp, sp), compiler_params=_cparams())(
        dsp_re, dsp_im, sp_re, sp_im, lr, li)


def _s5_bwd_in(dpre, dsin_re, dsin_im, u, tgt, bret, bimt, dflat, name):
    C = u.shape[1]

    def body(dp_ref, dr_ref, di_ref, u_ref, t_ref, br_ref, bi_ref, d_ref, du_ref, dbr_ref, dbi_ref):
        dp = dp_ref[...]
        dr, di, uv = dr_ref[...], di_ref[...], u_ref[...]
        du = (_bdot(dp, t_ref[...], _B_NT) + _bdot(dr, br_ref[...], _B_NT) + _bdot(di, bi_ref[...], _B_NT)
              + d_ref[...] * dp)
        du_ref[...] = du.astype(BF16)
        dbr_ref[...] = _bdot(uv, dr, _B_TN)
        dbi_ref[...] = _bdot(uv, di, _B_TN)

    sd = jax.ShapeDtypeStruct
    return pl.pallas_call(
        body, name=name,
        out_shape=(sd((SSM_G, C, S5_LW), BF16), sd((SSM_G, S5_LW, SSM_P), F32), sd((SSM_G, S5_LW, SSM_P), F32)),
        grid=(SSM_G // _S5_GB,),
        in_specs=[_gspec((C, S5_LW)), _gspec((C, SSM_P)), _gspec((C, SSM_P)), _gspec((C, S5_LW)),
                  _gspec((S5_LW, S5_LW)), _gspec((S5_LW, SSM_P)), _gspec((S5_LW, SSM_P)), _gspec((1, S5_LW))],
        out_specs=(_gspec((C, S5_LW)), _gspec((S5_LW, SSM_P)), _gspec((S5_LW, SSM_P))),
        compiler_params=_cparams(("parallel",)),
    )(dpre, dsin_re, dsin_im, u, tgt, bret, bimt, dflat)


def _to_chunks(v):
    T = v.shape[0]
    return v.reshape(T // S5_L, S5_L, SSM_G, SSM_H).transpose(2, 0, 1, 3).reshape(SSM_G, T // S5_L, S5_LW)


def _from_chunks(v):
    C = v.shape[1]
    return v.reshape(SSM_G, C, S5_L, SSM_H).transpose(1, 2, 0, 3).reshape(C * S5_L, MIX_W)


def _merge_fn(bra, brb, pc, pg, g0, g1, g2, b0, b1, b2):
    sg = jax.nn.sigmoid
    return sg(g0 + b0) * bra + sg(g1 + b1) * brb + sg(g2 + b2) * (pc * sg(pg))


_EW_CS = 256
_GATE_OFF = 6 * MIX_W


def _merge_rows(br4, proj):
    return [(b, 0, D_MODEL) for b in br4] + [(proj, _GATE_OFF + k * D_MODEL, D_MODEL) for k in range(3)]


def _merge_fwd(br4, proj, gb3, name):
    def fn(rp, cp):
        b = cp[0]
        return [_merge_fn(*rp, b[0:1], b[1:2], b[2:3])], []

    return _rowwise(fn, _merge_rows(br4, proj), [gb3], [(D_MODEL, BF16)], [], R=proj.shape[0], name=name, cs=_EW_CS)[0]


def _merge_bwd(br4, proj, gb3, dm, name):
    def fn(rp, cp):
        b = cp[0]
        shp = rp[0].shape
        bs = [jnp.broadcast_to(b[k:k + 1], shp) for k in range(3)]
        _, pull = jax.vjp(_merge_fn, *rp[:7], *bs)
        g = pull(rp[7])
        return list(g[:7]), list(g[7:])

    rows = _merge_rows(br4, proj) + [(dm, 0, D_MODEL)]
    outs = _rowwise(fn, rows, [gb3], [(D_MODEL, BF16)] * 7, [D_MODEL] * 3, R=proj.shape[0], name=name, tm=128, cs=_EW_CS)
    return outs[:4], outs[4:7], jnp.concatenate(outs[7:], axis=0)


def _swiglu(g, u):
    return jax.nn.silu(g) * u


def _act_fwd(gu, name):
    def fn(rp, cp):
        return [_swiglu(*rp)], []

    return _rowwise(fn, [(gu[0], 0, FFN_H), (gu[1], 0, FFN_H)], [], [(FFN_H, BF16)], [], R=gu[0].shape[0], name=name, cs=_EW_CS)[0]


def _act_bwd(gu, dact, name):
    def fn(rp, cp):
        _, pull = jax.vjp(_swiglu, rp[0], rp[1])
        return list(pull(rp[2])), []

    return _rowwise(fn, [(gu[0], 0, FFN_H), (gu[1], 0, FFN_H), (dact, 0, FFN_H)], [], [(FFN_H, BF16)] * 2, [],
                    R=dact.shape[0], name=name, tm=128, cs=_EW_CS)


def _adamw_fn(w, g, m, v):
    m = ADAM_B1 * m + (1.0 - ADAM_B1) * g
    v = ADAM_B2 * v + (1.0 - ADAM_B2) * jnp.square(g)
    m_hat = m / (1.0 - ADAM_B1 ** ADAM_STEP)
    v_hat = v / (1.0 - ADAM_B2 ** ADAM_STEP)
    delta = -ADAM_LR * (m_hat / (jnp.sqrt(v_hat) + ADAM_EPS) + ADAM_WD * w)
    return delta, m, v


def _adamw(w, g, m, v, name):
    R, C = w.shape

    def fn(rp, cp):
        return list(_adamw_fn(*rp)), []

    cs = _pick(C, (512, 256, 128))
    tm = _pick(R, (256, 128, 64, 32, 16, 8))
    return _rowwise(fn, [(w, 0, C), (g, 0, C), (m, 0, C), (v, 0, C)], [], [(C, F32)] * 3, [], R=R, name=name,
                    tm=tm, rs=8, cs=cs)


def _my_place():
    return lax.axis_index("x"), lax.axis_index("y"), lax.axis_index("c")


def _other_chips(x, y):
    return [(1 - x, y), (x, 1 - y), (1 - x, 1 - y)]


_ANY = pl.BlockSpec(memory_space=pl.ANY)


def _rcopy(src, dst, ssem, rsem, to):
    return pltpu.make_async_remote_copy(src_ref=src, dst_ref=dst, send_sem=ssem, recv_sem=rsem, device_id=to,
                                        device_id_type=MESH)


def _place_shard(local, axis, jidx, out_dtype, name):
    lead, r, c = local.shape
    shp = [lead, r, c]
    shp[axis] *= N_CHIPS
    tr = _pick(r, (512, 256, 128)) if r % 128 == 0 else r
    nr = r // tr
    omap = (lambda l, i, j: (l, i, j[0])) if axis == 2 else (lambda l, i, j: (l, j[0] * nr + i, 0))

    def body(j_ref, x_ref, o_ref):
        o_ref[...] = x_ref[...].astype(out_dtype)

    return pl.pallas_call(
        body, name=name, out_shape=jax.ShapeDtypeStruct(tuple(shp), out_dtype),
        grid_spec=pltpu.PrefetchScalarGridSpec(
            num_scalar_prefetch=1, grid=(lead, nr),
            in_specs=[pl.BlockSpec((None, tr, c), lambda l, i, j: (l, i, 0))],
            out_specs=pl.BlockSpec((None, tr, c), omap)),
        compiler_params=_cparams(("parallel", "parallel")),
    )(jidx, local)


class _GatherSide:
    def __init__(self, fulls, axes, regions):
        self.inputs = list(fulls)
        self.out_shapes = [jax.ShapeDtypeStruct(f.shape, f.dtype) for f in fulls]
        self.aliases = {t: t for t in range(len(fulls))}
        self.n_sems = 6 * len(fulls)
        self.axes, self.regions = list(axes), list(regions)

    def _block(self, outs, t, chip, half):
        start, size, split = self.regions[t]
        ax = self.axes[t]
        cut = outs[t].shape[ax] // N_CHIPS
        j = 2 * chip[0] + chip[1]
        idx = [pl.ds(start, size), slice(None), slice(None)]
        idx[ax] = pl.ds(j * cut, cut)
        if split == "lead":
            idx[0] = pl.ds(start + half * (size // 2), size // 2)
        else:
            other = 3 - ax
            h = outs[t].shape[other] // 2
            idx[other] = pl.ds(half * h, h)
        return outs[t].at[tuple(idx)]

    def _sends(self, outs, send, recv):
        x, y, c = _my_place()
        cps = []
        for t in range(len(outs)):
            mine = self._block(outs, t, (x, y), c)
            for r, chip in enumerate(_other_chips(x, y)):
                k = 3 * t + r
                cps.append(_rcopy(mine, mine, send.at[k], recv.at[k], (*chip, c)))
        return cps

    def start(self, ins, outs, send, recv):
        for cp in self._sends(outs, send, recv):
            cp.start()

    def finish(self, ins, outs, send, recv):
        x, y, c = _my_place()
        sib = (x, y, 1 - c)
        n = len(outs)
        chips = _other_chips(x, y)
        passed = []
        for t in range(n):
            for r, chip in enumerate(chips):
                k = 3 * t + r
                landed = self._block(outs, t, chip, c)
                _rcopy(landed, landed, send.at[k], recv.at[k], (*chip, c)).wait_recv()
                cp = _rcopy(landed, landed, send.at[3 * n + k], recv.at[3 * n + k], sib)
                cp.start()
                passed.append(cp)
        for t in range(n):
            for r, chip in enumerate(chips):
                k = 3 * n + 3 * t + r
                theirs = self._block(outs, t, chip, 1 - c)
                _rcopy(theirs, theirs, send.at[k], recv.at[k], sib).wait_recv()
        for cp in self._sends(outs, send, recv) + passed:
            cp.wait_send()


def _run_side(side, name):
    s_in = len(side.inputs)

    def body(*refs):
        ins, outs = refs[:s_in], refs[s_in:s_in + len(side.out_shapes)]
        send, recv = refs[s_in + len(side.out_shapes):]
        side.start(ins, outs, send, recv)
        side.finish(ins, outs, send, recv)

    return pl.pallas_call(
        body, name=name, out_shape=tuple(side.out_shapes), in_specs=[_ANY] * s_in,
        out_specs=tuple([_ANY] * len(side.out_shapes)), input_output_aliases=dict(side.aliases),
        scratch_shapes=[pltpu.SemaphoreType.DMA((side.n_sems,)), pltpu.SemaphoreType.DMA((side.n_sems,))],
    )(*side.inputs)


def _half_idx(shape, axis, half):
    size = shape[axis] // 2
    idx = [slice(None), slice(None)]
    idx[axis] = pl.ds(half * size, size)
    return tuple(idx)


def _pair_exchange(grads, half_axes, name):
    n = len(grads)
    out_shapes = []
    for g, ax in zip(grads, half_axes):
        shp = list(g.shape)
        shp[ax] //= 2
        out_shapes.append(jax.ShapeDtypeStruct(tuple(shp), g.dtype))

    def body(*refs):
        srcs, outs = refs[:n], refs[n:2 * n]
        send_sems, recv_sems = refs[2 * n:]
        x, y, c = _my_place()
        sib = (x, y, 1 - c)
        cps = []
        for t in range(n):
            cp = _rcopy(srcs[t].at[_half_idx(srcs[t].shape, half_axes[t], 1 - c)], outs[t], send_sems.at[t], recv_sems.at[t], sib)
            cp.start()
            cps.append(cp)
        for cp in cps:
            cp.wait()

    return pl.pallas_call(
        body, name=name, out_shape=tuple(out_shapes), in_specs=[_ANY] * n, out_specs=tuple([_ANY] * n),
        scratch_shapes=[pltpu.SemaphoreType.DMA((n,)), pltpu.SemaphoreType.DMA((n,))],
    )(*grads)


def _pair_sum(g, recv, half_axis, cidx, name):
    K, N = recv.shape
    tm = _pick(K, (256, 128, 64, 32, 16))
    tn = _pick(N, (1024, 1408, 512, 256, 128))
    nbr, nbc = K // tm, N // tn
    if half_axis == 0:
        gmap = lambda i, j, c: (c[0] * nbr + i, j)
    else:
        gmap = lambda i, j, c: (i, c[0] * nbc + j)

    def body(c_ref, g_ref, r_ref, of_ref, ob_ref):
        s = g_ref[...] + r_ref[...]
        of_ref[...] = s
        ob_ref[...] = s.astype(BF16)

    omap = lambda i, j, c: (i, j)
    return pl.pallas_call(
        body, name=name,
        out_shape=(jax.ShapeDtypeStruct((K, N), F32), jax.ShapeDtypeStruct((K, N), BF16)),
        grid_spec=pltpu.PrefetchScalarGridSpec(
            num_scalar_prefetch=1, grid=(nbr, nbc),
            in_specs=[pl.BlockSpec((tm, tn), gmap), pl.BlockSpec((tm, tn), omap)],
            out_specs=(pl.BlockSpec((tm, tn), omap), pl.BlockSpec((tm, tn), omap))),
        compiler_params=_cparams(("parallel", "parallel")),
    )(cidx, g, recv)


def _shard_idx(shape, axis, j):
    size = shape[axis] // N_CHIPS
    idx = [slice(None), slice(None)]
    idx[axis] = pl.ds(j * size, size)
    return tuple(idx)


class _ScatterSide:
    def __init__(self, parts, shard_axes):
        self.inputs = list(parts)
        self.shard_axes = list(shard_axes)
        self.out_shapes = []
        for p, ax in zip(parts, shard_axes):
            shp = list(p.shape)
            shp[ax] //= N_CHIPS
            self.out_shapes.append(jax.ShapeDtypeStruct((3,) + tuple(shp), p.dtype))
        self.aliases = {}
        self.n_sems = 3 * len(parts)

    def _copies(self, srcs, outs, send, recv):
        x, y, c = _my_place()
        cps = []
        for t in range(len(srcs)):
            for r, chip in enumerate(_other_chips(x, y)):
                k = 3 * t + r
                j = 2 * chip[0] + chip[1]
                cps.append(_rcopy(srcs[t].at[_shard_idx(srcs[t].shape, self.shard_axes[t], j)], outs[t].at[r],
                                  send.at[k], recv.at[k], (*chip, c)))
        return cps

    def start(self, srcs, outs, send, recv):
        for cp in self._copies(srcs, outs, send, recv):
            cp.start()

    def finish(self, srcs, outs, send, recv):
        for cp in self._copies(srcs, outs, send, recv):
            cp.wait()


def _shard_sum(pf, recv, acc, layer, shard_axis, jcidx, name):
    _, K, N = recv.shape
    tm = _pick(K, (256, 128, 64, 32, 16))
    tn = _pick(N, (1024, 1408, 512, 256, 128))
    nbr, nbc = K // tm, N // tn
    if shard_axis == 0:
        pmap = lambda i, j, s: (s[0] * nbr + i, j)
        omap = lambda i, j, s: (layer, i, s[1] * nbc + j)
    else:
        pmap = lambda i, j, s: (i, s[0] * nbc + j)
        omap = lambda i, j, s: (layer, s[1] * nbr + i, j)

    def body(j_ref, p_ref, r_ref, a_ref, o_ref):
        o_ref[...] = ((p_ref[...] + r_ref[0].astype(F32)) + r_ref[1].astype(F32)) + r_ref[2].astype(F32)

    return pl.pallas_call(
        body, name=name, out_shape=jax.ShapeDtypeStruct(acc.shape, F32),
        grid_spec=pltpu.PrefetchScalarGridSpec(
            num_scalar_prefetch=1, grid=(nbr, nbc),
            in_specs=[pl.BlockSpec((tm, tn), pmap), pl.BlockSpec((3, tm, tn), lambda i, j, s: (0, i, j)), _ANY],
            out_specs=pl.BlockSpec((None, tm, tn), omap)),
        input_output_aliases={3: 0},
        compiler_params=_cparams(("parallel", "parallel")),
    )(jcidx, pf, recv, acc)


def _pair_join(accs, half_axes, name):
    n = len(accs)

    def body(*refs):
        outs = refs[n:2 * n]
        send_sems, recv_sems = refs[2 * n:]
        x, y, c = _my_place()
        sib = (x, y, 1 - c)

        def half(t, hc):
            return outs[t].at[(slice(None),) + _half_idx(outs[t].shape[1:], half_axes[t], hc)]

        cps = []
        for t in range(n):
            cp = _rcopy(half(t, c), half(t, c), send_sems.at[t], recv_sems.at[t], sib)
            cp.start()
            cps.append(cp)
        for t in range(n):
            _rcopy(half(t, 1 - c), half(t, 1 - c), send_sems.at[t], recv_sems.at[t], sib).wait_recv()
        for cp in cps:
            cp.wait_send()

    return pl.pallas_call(
        body, name=name, out_shape=tuple(jax.ShapeDtypeStruct(a.shape, a.dtype) for a in accs),
        in_specs=[_ANY] * n, out_specs=tuple([_ANY] * n), input_output_aliases={t: t for t in range(n)},
        scratch_shapes=[pltpu.SemaphoreType.DMA((n,)), pltpu.SemaphoreType.DMA((n,))],
    )(*accs)


_N_DEV = 8


def _allreduce_small(flat, name):
    _, R, _ = flat.shape

    def body(in_ref, out_ref, stage, send1, recv1, send2, recv2):
        x, y, c = _my_place()
        me = 4 * x + 2 * y + c
        places = [(px, py, pc) for px in range(2) for py in range(2) for pc in range(2)]
        def peer(r):
            return (x ^ (r >> 2), y ^ ((r >> 1) & 1), c ^ (r & 1))

        def peer_id(r):
            p = peer(r)
            return 4 * p[0] + 2 * p[1] + p[2]

        stage[0] = in_ref[me]
        cps = []
        for r in range(1, _N_DEV):
            cp = _rcopy(in_ref.at[peer_id(r)], stage.at[r], send1.at[r], recv1.at[r], peer(r))
            cp.start()
            cps.append(cp)
        for cp in cps:
            cp.wait()
        tot = jnp.zeros((R, 128), F32)
        for d in range(_N_DEV):
            tot = tot + stage[me ^ d]
        out_ref[me] = tot
        cps = []
        for r in range(1, _N_DEV):
            cp = _rcopy(out_ref.at[me], out_ref.at[me], send2.at[r], recv2.at[r], peer(r))
            cp.start()
            cps.append(cp)
        for r in range(1, _N_DEV):
            _rcopy(out_ref.at[peer_id(r)], out_ref.at[peer_id(r)], send2.at[r], recv2.at[r], peer(r)).wait_recv()
        for cp in cps:
            cp.wait_send()

    vm = pl.BlockSpec(memory_space=pltpu.VMEM)
    return pl.pallas_call(
        body, name=name, out_shape=jax.ShapeDtypeStruct(flat.shape, F32), in_specs=[vm], out_specs=vm,
        scratch_shapes=[pltpu.VMEM(flat.shape, F32)] + [pltpu.SemaphoreType.DMA((_N_DEV,))] * 4,
        compiler_params=_cparams(),
    )(flat)


_BIG = ("w_in", "ssm_w_glu", "w_branch", "w_out", "w_ffn_gate", "w_ffn_up", "w_ffn_down")
_BIG_SHARD_AXIS = {"w_in": 1, "ssm_w_glu": 1, "w_branch": 1, "w_out": 0, "w_ffn_gate": 1, "w_ffn_up": 1, "w_ffn_down": 0}
_SMALL = ("norm_mix_g", "gate_bias", "lru_conv_w", "lru_conv_b", "lru_wa", "lru_ba", "lru_wx", "lru_bx", "lru_lambda",
          "attn_rel_bias", "ssm_a_re", "ssm_a_im", "ssm_b_re", "ssm_b_im", "ssm_c_re", "ssm_c_im", "ssm_d",
          "ssm_log_step", "norm_ffn_g", "norm_final_g")
_SMALL_SHARDED = {"gate_bias": 2, "lru_conv_w": 2}
_WEIGHTS = ("norm_mix_g", "w_in", "gate_bias", "lru_conv_w", "lru_conv_b", "lru_wa", "lru_ba", "lru_wx", "lru_bx",
            "lru_lambda", "attn_rel_bias", "ssm_a_re", "ssm_a_im", "ssm_b_re", "ssm_b_im", "ssm_c_re", "ssm_c_im",
            "ssm_d", "ssm_log_step", "ssm_w_glu", "w_branch", "w_out", "norm_ffn_g", "w_ffn_gate", "w_ffn_up",
            "w_ffn_down", "norm_final_g")


def _carried(comm, phase, l, key, W, fn, *args, **kw):
    side = comm.side(phase, l, key, W)
    if side is None:
        return fn(*args, **kw)
    out, got = fn(*args, side=side, **kw)
    comm.took(phase, l, key, got, W)
    return out


def _layer_fwd(l, x, W, sm, comm):
    T = x.shape[0]
    nm = lambda s: f"{s}"
    h1 = _rms_fwd(x, sm["norm_mix_g"][l][None, :], nm("rms_fwd"))
    proj = _carried(comm, "fwd", l, "mm_in", W, _mm, h1, W["w_in"], M=T, N=IN_W, K=D_MODEL, b_lead=l, name=nm("mm_in"))
    wax, bax = sm["lru_prep"][l]
    cw, cb, lam = sm["lru_conv_w"][l], sm["lru_conv_b"][l][None, :], sm["lru_lambda"][l][None, :]
    y_a, hst = _lru_fwd(proj, cw, cb, wax, bax, lam, nm("lru_fwd"))
    bias = sm["attn_bias"][l]
    y_b = _carried(comm, "fwd", l, "attn_fwd", W, _attn_fwd, proj, bias, nm("attn_fwd"))
    tgt, bret, bimt, cre, cim, lr, li, dflat = sm["s5_prep"][l]
    u = _to_chunks(lax.slice_in_dim(proj, 5 * MIX_W, 6 * MIX_W, axis=1))
    sin_re, sin_im = _s5_in(u, bret, bimt, nm("s5_in"))
    sp_re, sp_im = _s5_scan(sin_re.transpose(1, 0, 2), sin_im.transpose(1, 0, 2), lr, li, nm("s5_scan"))
    sp_re, sp_im = sp_re.transpose(1, 0, 2), sp_im.transpose(1, 0, 2)
    pre, ycf = _s5_out(u, sp_re, sp_im, tgt, cre, cim, dflat, nm("s5_out"))
    y_c = _from_chunks(ycf)
    brs = []
    for k, yk in enumerate((y_a, y_b, y_c)):
        brs.append(_mm(yk, W["w_branch"], M=T, N=D_MODEL, K=MIX_W, b_lead=3 * l + k, name=nm("mm_branch")))
    brs.append(_mm(y_c, W["ssm_w_glu"], M=T, N=D_MODEL, K=MIX_W, b_lead=l, name=nm("mm_branch")))
    br4 = tuple(brs)
    gb3 = sm["gate_bias"][l]
    merged = _merge_fwd(br4, proj, gb3, nm("merge_fwd"))
    x1 = _mm(merged, W["w_out"], M=T, N=D_MODEL, K=D_MODEL, b_lead=l, res=x, name=nm("mm_out"))
    h2 = _rms_fwd(x1, sm["norm_ffn_g"][l][None, :], nm("rms_fwd"))
    gpre = _carried(comm, "fwd", l, "mm_ffn_up", W, _mm, h2, W["w_ffn_gate"], M=T, N=FFN_H, K=D_MODEL, b_lead=l,
                    name=nm("mm_ffn_up"))
    upre = _mm(h2, W["w_ffn_up"], M=T, N=FFN_H, K=D_MODEL, b_lead=l, name=nm("mm_ffn_up"))
    gu = (gpre, upre)
    act = _act_fwd(gu, nm("act_fwd"))
    x2 = _mm(act, W["w_ffn_down"], M=T, N=D_MODEL, K=FFN_H, b_lead=l, res=x1, name=nm("mm_down"))
    saved = dict(x=x, h1=h1, proj=proj, hst=hst, y_a=y_a, y_b=y_b, y_c=y_c, u=u, sp_re=sp_re, sp_im=sp_im, pre=pre,
                 br4=br4, merged=merged, x1=x1, h2=h2, gu=gu, act=act)
    return x2, saved


def _layer_bwd(l, dx2, sv, W, sm, comm):
    T = dx2.shape[0]
    nm = lambda s: f"{s}"
    big, small = {}, {}
    dxb = dx2.astype(BF16)
    big["w_ffn_down"] = _mm(sv["act"], dxb, M=FFN_H, N=D_MODEL, K=T, ta=True, name=nm("mm_dw_down"))
    dact = _mm(dxb, W["w_ffn_down"], M=T, N=FFN_H, K=D_MODEL, tb=True, b_lead=l, name=nm("mm_dact"))
    dg, du = _act_bwd(sv["gu"], dact, nm("act_bwd"))
    big["w_ffn_gate"] = _mm(sv["h2"], dg, M=D_MODEL, N=FFN_H, K=T, ta=True, name=nm("mm_dw_up"))
    big["w_ffn_up"] = _mm(sv["h2"], du, M=D_MODEL, N=FFN_H, K=T, ta=True, name=nm("mm_dw_up"))
    dh2 = _mm(dg, W["w_ffn_gate"], M=T, N=D_MODEL, K=FFN_H, tb=True, b_lead=l, name=nm("mm_dh2"))
    dh2 = _mm(du, W["w_ffn_up"], M=T, N=D_MODEL, K=FFN_H, tb=True, b_lead=l, res=dh2, name=nm("mm_dh2r"))
    dx1, dgn = _rms_bwd(sv["x1"], sm["norm_ffn_g"][l][None, :], dh2, dx2, nm("rms_bwd"))
    small["norm_ffn_g"] = dgn[0]
    dx1b = dx1.astype(BF16)
    big["w_out"] = _mm(sv["merged"], dx1b, M=D_MODEL, N=D_MODEL, K=T, ta=True, name=nm("mm_dw_out"))
    dm = _mm(dx1b, W["w_out"], M=T, N=D_MODEL, K=D_MODEL, tb=True, b_lead=l, name=nm("mm_dmerged"))
    dbr, dgates, dgb = _merge_bwd(sv["br4"], sv["proj"], sm["gate_bias"][l], dm, nm("merge_bwd"))
    small["gate_bias"] = dgb
    ys = (sv["y_a"], sv["y_b"], sv["y_c"])
    big["w_branch"] = [_mm(ys[k], dbr[k], M=MIX_W, N=D_MODEL, K=T, ta=True, name=nm("mm_dw_branch")) for k in range(3)]
    big["ssm_w_glu"] = _mm(sv["y_c"], dbr[3], M=MIX_W, N=D_MODEL, K=T, ta=True, name=nm("mm_dw_branch"))
    dya = _mm(dbr[0], W["w_branch"], M=T, N=MIX_W, K=D_MODEL, tb=True, b_lead=3 * l, name=nm("mm_dy"))
    dyb = _mm(dbr[1], W["w_branch"], M=T, N=MIX_W, K=D_MODEL, tb=True, b_lead=3 * l + 1, name=nm("mm_dy"))
    dyc = _mm(dbr[2], W["w_branch"], M=T, N=MIX_W, K=D_MODEL, tb=True, b_lead=3 * l + 2, name=nm("mm_dy"))
    dyc = _mm(dbr[3], W["ssm_w_glu"], M=T, N=MIX_W, K=D_MODEL, tb=True, b_lead=l, res=dyc, name=nm("mm_dyr"))
    tgt, bret, bimt, cre, cim, lr, li, dflat = sm["s5_prep"][l]
    dpre, dsp_re, dsp_im, d_tgt, d_cre, d_cim, d_dflat = _s5_bwd_out(
        _to_chunks(dyc), sv["pre"], sv["u"], sv["sp_re"], sv["sp_im"], cre, cim, nm("s5_bwd_out"))
    tr = lambda a: a.transpose(1, 0, 2)
    dsin_re, dsin_im, d_lr, d_li = _s5_rscan(tr(dsp_re), tr(dsp_im), tr(sv["sp_re"]), tr(sv["sp_im"]), lr, li, nm("s5_rscan"))
    du_f, d_bret, d_bimt = _s5_bwd_in(dpre, dsin_re.transpose(1, 0, 2), dsin_im.transpose(1, 0, 2), sv["u"], tgt, bret, bimt,
                                      dflat, nm("s5_bwd_in"))
    d_u = _from_chunks(du_f)
    small["s5_tables"] = (d_tgt, d_bret, d_bimt, d_cre, d_cim, d_lr, d_li, d_dflat)
    dq, dk, dv, dbias = _carried(comm, "bwd", l, "attn_bwd", W, _attn_bwd, sv["proj"], sm["attn_bias"][l], dyb, nm("attn_bwd"))
    small["attn_bias"] = dbias
    wax, bax = sm["lru_prep"][l]
    cw, cb, lam = sm["lru_conv_w"][l], sm["lru_conv_b"][l][None, :], sm["lru_lambda"][l][None, :]
    d_lx, d_lg, d_cw, d_cb, d_wax, d_bax, d_lam = _lru_bwd(sv["proj"], sv["hst"], dya, cw, cb, wax, bax, lam, nm("lru_bwd"))
    small["lru_conv_w"], small["lru_conv_b"], small["lru_lambda"] = d_cw, d_cb[0], d_lam[0]
    small["lru_tables"] = (d_wax, d_bax)
    dproj = jnp.concatenate([d_lx, d_lg, dq, dk, dv, d_u] + list(dgates), axis=1)
    big["w_in"] = _carried(comm, "bwd", l, "mm_dw_in", W, _mm, sv["h1"], dproj, M=D_MODEL, N=IN_W, K=T, ta=True,
                           name=nm("mm_dw_in"))
    dh1 = _carried(comm, "bwd", l, "mm_dh1", W, _mm, dproj, W["w_in"], M=T, N=D_MODEL, K=IN_W, tb=True, b_lead=l,
                   name=nm("mm_dh1"))
    dx, dgn = _rms_bwd(sv["x"], sm["norm_mix_g"][l][None, :], dh1, dx1, nm("rms_bwd"))
    small["norm_mix_g"] = dgn[0]
    return dx, big, small


_TENSORS = tuple((n, k) for n in _BIG for k in range(N_BRANCH if n == "w_branch" else 1))
_FWD_CARRIERS = {"attn_fwd": ("w_in",), "mm_in": ("w_ffn_gate", "w_ffn_up"),
                 "mm_ffn_up": ("w_ffn_down", "w_out", "w_branch", "ssm_w_glu")}
_BWD_CARRIERS = {"attn_bwd": ("w_in",), "mm_dw_in": ("w_ffn_gate", "w_ffn_up"),
                 "mm_dh1": ("w_ffn_down", "w_out", "w_branch", "ssm_w_glu")}


class _StepComm:
    def __init__(self, depth, cidx, jcidx):
        self.depth, self.cidx, self.jcidx = depth, cidx, jcidx
        self.accs = {}
        self.pending = None

    def gather_side(self, W, names, l):
        fulls, axes, regions = [], [], []
        for n in names:
            per = N_BRANCH if n == "w_branch" else 1
            fulls.append(W[n])
            axes.append(_BIG_SHARD_AXIS[n] + 1)
            regions.append((per * l, per, "other"))
        return _GatherSide(fulls, axes, regions), names

    def side(self, phase, l, key, W):
        if phase == "fwd":
            if l + 1 >= self.depth or key not in _FWD_CARRIERS:
                return None
            return self.gather_side(W, _FWD_CARRIERS[key], l + 1)[0]
        if self.pending is None or key not in _BWD_CARRIERS:
            return None
        nks = [nk for nk in _TENSORS if nk[0] in _BWD_CARRIERS[key]]
        return _ScatterSide([self.pending[1][nk][1] for nk in nks], [_BIG_SHARD_AXIS[nk[0]] for nk in nks])

    def took(self, phase, l, key, got, W):
        if phase == "fwd":
            W.update(zip(_FWD_CARRIERS[key], got))
            return
        nks = [nk for nk in _TENSORS if nk[0] in _BWD_CARRIERS[key]]
        self.shard_sums(self.pending[0], nks, got)

    def shard_sums(self, l, nks, got):
        for nk, r in zip(nks, got):
            pf = self.pending[1][nk][0]
            sa = _BIG_SHARD_AXIS[nk[0]]
            if nk not in self.accs:
                shp = list(pf.shape)
                shp[sa] //= N_CHIPS
                shp[1 - sa] *= 2
                self.accs[nk] = lax.empty((self.depth,) + tuple(shp), F32)
            self.accs[nk] = _shard_sum(pf, r, self.accs[nk], l, sa, self.jcidx, "shard_sum")

    def on_big(self, l, big):
        grads = {}
        for n in _BIG:
            gs = big[n] if isinstance(big[n], list) else [big[n]]
            grads.update({(n, k): g for k, g in enumerate(gs)})
        half_axes = [1 - _BIG_SHARD_AXIS[nk[0]] for nk in _TENSORS]
        recv = _pair_exchange([grads[nk] for nk in _TENSORS], half_axes, "pair_exchange")
        self.pending = (l, {nk: _pair_sum(grads[nk], r, ha, self.cidx, "pair_sum")
                            for nk, r, ha in zip(_TENSORS, recv, half_axes)})
        if l == 0:
            side = _ScatterSide([self.pending[1][nk][1] for nk in _TENSORS], [_BIG_SHARD_AXIS[nk[0]] for nk in _TENSORS])
            self.shard_sums(0, _TENSORS, _run_side(side, "chip_scatter"))
            self.pending = None


class _NoComm:
    def __init__(self, on_big):
        self.on_big = on_big

    def side(self, phase, l, key, W):
        return None


def _local_step(xs, tgt, W, sm, comm):
    W = dict(W)
    sm = dict(sm)
    depth = sm["norm_mix_g"].shape[0]
    lru_prep_vjps, attn_vjps, s5_vjps = [], [], []
    sm["lru_prep"], sm["attn_bias"], sm["s5_prep"] = [], [], []
    for l in range(depth):
        o, f = jax.vjp(_lru_prep, sm["lru_wa"][l], sm["lru_wx"][l], sm["lru_ba"][l], sm["lru_bx"][l])
        sm["lru_prep"].append((o[0].astype(BF16), o[1]))
        lru_prep_vjps.append(f)
        o, f = jax.vjp(_attn_bias, sm["attn_rel_bias"][l])
        sm["attn_bias"].append(o)
        attn_vjps.append(f)
        o, f = jax.vjp(_s5_prep, *[sm[n][l] for n in ("ssm_a_re", "ssm_a_im", "ssm_b_re", "ssm_b_im", "ssm_c_re", "ssm_c_im",
                                                       "ssm_d", "ssm_log_step")])
        sm["s5_prep"].append(o)
        s5_vjps.append(f)

    saved = []
    for l in range(depth):
        xs, sv = _layer_fwd(l, xs, W, sm, comm)
        saved.append(sv)
    loss_part, dx, dgf = _final_loss(xs, sm["norm_final_g"][None, :], tgt, "final_loss")

    small_g = {n: [None] * depth for n in _SMALL if n != "norm_final_g"}
    for l in reversed(range(depth)):
        dx, big, small = _layer_bwd(l, dx, saved[l], W, sm, comm)
        comm.on_big(l, big)
        d_wa, d_wx, d_ba, d_bx = lru_prep_vjps[l](small["lru_tables"])
        (d_rel,) = attn_vjps[l](small["attn_bias"])
        d_s5 = s5_vjps[l](small["s5_tables"])
        vals = dict(norm_mix_g=small["norm_mix_g"], gate_bias=small["gate_bias"], lru_conv_w=small["lru_conv_w"],
                    lru_conv_b=small["lru_conv_b"], lru_wa=d_wa, lru_ba=d_ba, lru_wx=d_wx, lru_bx=d_bx,
                    lru_lambda=small["lru_lambda"], attn_rel_bias=d_rel, ssm_a_re=d_s5[0], ssm_a_im=d_s5[1],
                    ssm_b_re=d_s5[2], ssm_b_im=d_s5[3], ssm_c_re=d_s5[4], ssm_c_im=d_s5[5], ssm_d=d_s5[6],
                    ssm_log_step=d_s5[7], norm_ffn_g=small["norm_ffn_g"])
        for n, val in vals.items():
            small_g[n][l] = val
    small_tree = {n: jnp.stack(small_g[n]) for n in small_g}
    small_tree["norm_final_g"] = dgf[0]
    return loss_part, dx, small_tree


def _pack_small(tree, names):
    flat = jnp.concatenate([tree[n].reshape(-1) for n in names])
    per = -(-flat.shape[0] // (_N_DEV * 128 * 8)) * (128 * 8)
    flat = jnp.pad(flat, (0, _N_DEV * per - flat.shape[0]))
    return flat.reshape(_N_DEV, per // 128, 128)


def _unpack_small(flat, like, names):
    flat = flat.reshape(-1)
    out, off = {}, 0
    for n in names:
        size = math.prod(like[n].shape)
        out[n] = flat[off:off + size].reshape(like[n].shape)
        off += size
    return out


def kernel(x, norm_mix_g, w_in, gate_bias, lru_conv_w, lru_conv_b, lru_wa, lru_ba, lru_wx, lru_bx, lru_lambda, attn_rel_bias, ssm_a_re, ssm_a_im, ssm_b_re, ssm_b_im, ssm_c_re, ssm_c_im, ssm_d, ssm_log_step, ssm_w_glu, w_branch, w_out, norm_ffn_g, w_ffn_gate, w_ffn_up, w_ffn_down, norm_final_g, loss_target, m_norm_mix_g, m_w_in, m_gate_bias, m_lru_conv_w, m_lru_conv_b, m_lru_wa, m_lru_ba, m_lru_wx, m_lru_bx, m_lru_lambda, m_attn_rel_bias, m_ssm_a_re, m_ssm_a_im, m_ssm_b_re, m_ssm_b_im, m_ssm_c_re, m_ssm_c_im, m_ssm_d, m_ssm_log_step, m_ssm_w_glu, m_w_branch, m_w_out, m_norm_ffn_g, m_w_ffn_gate, m_w_ffn_up, m_w_ffn_down, m_norm_final_g, v_norm_mix_g, v_w_in, v_gate_bias, v_lru_conv_w, v_lru_conv_b, v_lru_wa, v_lru_ba, v_lru_wx, v_lru_bx, v_lru_lambda, v_attn_rel_bias, v_ssm_a_re, v_ssm_a_im, v_ssm_b_re, v_ssm_b_im, v_ssm_c_re, v_ssm_c_im, v_ssm_d, v_ssm_log_step, v_ssm_w_glu, v_w_branch, v_w_out, v_norm_ffn_g, v_w_ffn_gate, v_w_ffn_up, v_w_ffn_down, v_norm_final_g):
    args = dict(locals())
    w = {n: args[n] for n in _WEIGHTS}
    m = {n: args["m_" + n] for n in _WEIGHTS}
    v = {n: args["v_" + n] for n in _WEIGHTS}
    depth = w_in.shape[0]
    xc, yc, cc = _my_place()
    jchip = 2 * xc + yc
    cidx = jnp.reshape(cc, (1,)).astype(jnp.int32)
    jidx = jnp.reshape(jchip, (1,)).astype(jnp.int32)
    jcidx = jnp.stack([jchip, cc]).astype(jnp.int32)

    blocks = [w[n] for n in _BIG]
    blocks[2] = blocks[2].reshape(depth * N_BRANCH, MIX_W, -1)
    axes = [_BIG_SHARD_AXIS[n] + 1 for n in _BIG] + [_SMALL_SHARDED[n] for n in _SMALL_SHARDED]
    placed = [_place_shard(b, ax, jidx, BF16, "place_shard") for b, ax in zip(blocks, axes)]
    placed += [_place_shard(w[n], _SMALL_SHARDED[n], jidx, F32, "place_shard") for n in _SMALL_SHARDED]
    regions = [(0, N_BRANCH if n == "w_branch" else 1, "other") for n in _BIG] + [(0, depth, "lead")] * len(_SMALL_SHARDED)
    gathered = _run_side(_GatherSide(placed, axes, regions), "gather_weights")
    W = dict(zip(_BIG, gathered))
    sm_full = dict(zip(_SMALL_SHARDED, gathered[len(_BIG):]))
    sm = {n: w[n] for n in _SMALL if n not in _SMALL_SHARDED}
    sm.update(sm_full)

    comm = _StepComm(depth, cidx, jcidx)
    loss_part, dx, small_tree = _local_step(x[0], loss_target[0], W, sm, comm)
    loss = lax.psum(loss_part, ("x", "y", "c"))
    grad_x = dx[None]

    joined = _pair_join([comm.accs[nk] for nk in _TENSORS], [1 - _BIG_SHARD_AXIS[nk[0]] for nk in _TENSORS], "pair_join")
    jd = dict(zip(_TENSORS, joined))
    grads = {}
    for n in _BIG:
        if n == "w_branch":
            grads[n] = jnp.stack([jd[(n, k)] for k in range(N_BRANCH)], axis=1)
        else:
            grads[n] = jd[(n, 0)]
    like = {n: (sm_full[n] if n in _SMALL_SHARDED else w[n]) for n in _SMALL}
    red = _unpack_small(_allreduce_small(_pack_small(small_tree, _SMALL), "allreduce_small"), like, _SMALL)
    for n in _SMALL:
        if n in _SMALL_SHARDED:
            size = w[n].shape[2]
            grads[n] = lax.dynamic_slice_in_dim(red[n], (2 * xc + yc) * size, size, axis=2)
        else:
            grads[n] = red[n]

    delta, new_m, new_v = {}, {}, {}
    for n in _BIG:
        shp = w[n].shape
        two = lambda a: a.reshape(-1, shp[-1])
        d_, m_, v_ = _adamw(two(w[n]), two(grads[n]), two(m[n]), two(v[n]), "adamw")
        delta[n], new_m[n], new_v[n] = d_.reshape(shp), m_.reshape(shp), v_.reshape(shp)
    pk = lambda tree: _pack_small(tree, _SMALL).reshape(-1, 128)
    d_, m_, v_ = _adamw(pk(w), pk(grads), pk(m), pk(v), "adamw_small")
    like_local = {n: w[n] for n in _SMALL}
    for tree, flat in ((delta, d_), (new_m, m_), (new_v, v_)):
        tree.update(_unpack_small(flat, like_local, _SMALL))
    return (loss, grad_x, *[grads[n] for n in _WEIGHTS], *[delta[n] for n in _WEIGHTS], *[new_m[n] for n in _WEIGHTS],
            *[new_v[n] for n in _WEIGHTS])
```

```python
import functools
import math

import jax
import jax.numpy as jnp
import numpy as np
from jax import lax
from jax.experimental import pallas as pl
from jax.experimental.pallas import tpu as pltpu

F32 = jnp.float32
BF16 = jnp.bfloat16

D_MODEL = 2048
MIX_W = 1024
N_BRANCH = 3
LRU_BLOCKS = 16
LRU_BW = 64
CONV_W = 4
LRU_C = 8.0
CHUNK = 64
ATT_HEADS = 8
ATT_HD = 128
ATT_LEFT = 8
ATT_BAND = (ATT_LEFT + 1) * CHUNK
MAX_REL = 128
N_REL = 2 * MAX_REL + 1
SSM_G = 64
SSM_H = 16
SSM_P = 64
FFN_H = 5632
IN_W = 6 * MIX_W + N_BRANCH * D_MODEL
NORM_EPS = 1e-6
MASK_VALUE = -1e30
ADAM_LR, ADAM_B1, ADAM_B2, ADAM_EPS, ADAM_WD, ADAM_STEP = 0.001, 0.9, 0.999, 1e-08, 0.01, 10

S5_L = 16
S5_LW = S5_L * SSM_H
N_CHIPS = 4
V7X_VMEM_LIMIT = 56 * 1024 * 1024
HI = lax.Precision.HIGHEST
MESH = pl.DeviceIdType.MESH


def _cparams(sem=None):
    return pltpu.CompilerParams(dimension_semantics=sem, vmem_limit_bytes=V7X_VMEM_LIMIT)


def _pick(n, prefs):
    for p in prefs:
        if n % p == 0:
            return p
    return n


_MM_VMEM_BUDGET = 46 * 1024 * 1024


def _mm_tiles(M, N, K, has_res, out_bytes):
    tn = _pick(N, (1024, 1408, 512, 256, 128))
    for tk in (K, 2048, 1408, 1024, 512, 256, 128):
        if K % tk:
            continue
        for tm in (1024, 512, 256, 128, 64, 32, 16, 8):
            if M % tm:
                continue
            need = 2 * 2 * (tm * tk + tk * tn) + 2 * tm * tn * out_bytes
            need += tm * tn * 4 if tk < K else 0
            need += 2 * tm * tn * 4 if has_res else 0
            if need <= _MM_VMEM_BUDGET:
                return tm, tn, tk
    raise ValueError((M, N, K))


def _call(body, *, name, grid, in_specs, out_specs, out_shape, args, scratch_shapes=(), sem=None, side=None):
    in_specs, out_specs, out_shape = list(in_specs), list(out_specs), list(out_shape)
    scratch_shapes = list(scratch_shapes)
    if side is None:
        outs = pl.pallas_call(body, name=name, out_shape=tuple(out_shape), grid=grid, in_specs=in_specs,
                              out_specs=tuple(out_specs), scratch_shapes=scratch_shapes, compiler_params=_cparams(sem))(*args)
        return tuple(outs), ()
    n_in, n_out, n_scr = len(in_specs), len(out_shape), len(scratch_shapes)
    s_in, s_out = len(side.inputs), len(side.out_shapes)

    def wrapped(*refs):
        mi, refs = refs[:n_in], refs[n_in:]
        si, refs = refs[:s_in], refs[s_in:]
        mo, refs = refs[:n_out], refs[n_out:]
        so, refs = refs[:s_out], refs[s_out:]
        scr, (send, recv) = refs[:n_scr], refs[n_scr:]
        first = functools.reduce(jnp.logical_and, [pl.program_id(d) == 0 for d in range(len(grid))])
        last = functools.reduce(jnp.logical_and, [pl.program_id(d) == g - 1 for d, g in enumerate(grid)])

        @pl.when(first)
        def _():
            side.start(si, so, send, recv)

        body(*mi, *mo, *scr)

        @pl.when(last)
        def _():
            side.finish(si, so, send, recv)

    outs = pl.pallas_call(
        wrapped, name=name, out_shape=tuple(out_shape + list(side.out_shapes)), grid=grid,
        in_specs=in_specs + [_ANY] * s_in, out_specs=tuple(out_specs + [_ANY] * s_out),
        scratch_shapes=scratch_shapes + [pltpu.SemaphoreType.DMA((side.n_sems,)), pltpu.SemaphoreType.DMA((side.n_sems,))],
        input_output_aliases={n_in + i: n_out + o for i, o in side.aliases.items()},
        compiler_params=_cparams(("arbitrary",) * len(grid)),
    )(*args, *side.inputs)
    return tuple(outs[:n_out]), tuple(outs[n_out:])


def _mm(a, b, *, M, N, K, name, ta=False, tb=False, a_lead=None, b_lead=None, a_off=(0, 0), b_off=(0, 0),
        out_dtype=F32, res=None, tm=None, tn=None, tk=None, side=None):
    if tm is None and tn is None and tk is None:
        tm, tn, tk = _mm_tiles(M, N, K, res is not None, jnp.dtype(out_dtype).itemsize)
    nk = K // tk

    def spec(blk, lead, off, order):
        r0, c0 = off[0] // blk[0], off[1] // blk[1]
        assert off[0] % blk[0] == 0 and off[1] % blk[1] == 0
        if lead is None:
            return pl.BlockSpec(blk, lambda i, j, k: (r0 + order(i, j, k)[0], c0 + order(i, j, k)[1]))
        return pl.BlockSpec((None,) + blk, lambda i, j, k: (lead, r0 + order(i, j, k)[0], c0 + order(i, j, k)[1]))

    a_spec = spec((tk, tm), a_lead, a_off, lambda i, j, k: (k, i)) if ta else spec((tm, tk), a_lead, a_off, lambda i, j, k: (i, k))
    b_spec = spec((tn, tk), b_lead, b_off, lambda i, j, k: (j, k)) if tb else spec((tk, tn), b_lead, b_off, lambda i, j, k: (k, j))
    dims = (((0 if ta else 1,), (1 if tb else 0,)), ((), ()))
    in_specs = [a_spec, b_spec]
    args = [a, b]
    if res is not None:
        in_specs.append(pl.BlockSpec((tm, tn), lambda i, j, k: (i, j)))
        args.append(res)

    def body(*refs):
        a_ref, b_ref = refs[:2]
        r_ref = refs[2] if res is not None else None
        o_ref = refs[3] if res is not None else refs[2]

        def dot():
            return lax.dot_general(a_ref[...], b_ref[...], dims, preferred_element_type=F32)

        def finish(r):
            if r_ref is not None:
                r = r + r_ref[...]
            o_ref[...] = r.astype(out_dtype)

        if nk == 1:
            finish(dot())
            return
        acc = refs[-1]
        k = pl.program_id(2)

        @pl.when(k == 0)
        def _():
            acc[...] = dot()

        @pl.when(jnp.logical_and(k > 0, k < nk - 1))
        def _():
            acc[...] += dot()

        @pl.when(k == nk - 1)
        def _():
            finish(acc[...] + dot())

    (out,), got = _call(
        body, name=name, out_shape=[jax.ShapeDtypeStruct((M, N), out_dtype)],
        grid=(M // tm, N // tn, nk), in_specs=in_specs, out_specs=[pl.BlockSpec((tm, tn), lambda i, j, k: (i, j))],
        scratch_shapes=[pltpu.VMEM((tm, tn), F32)] if nk > 1 else [],
        sem=("parallel", "parallel", "arbitrary"), args=args, side=side)
    return out if side is None else (out, got)


def _rowwise(fn, rows, consts, out_rows, out_accs, *, R, name, tm=256, rs=16, cs=None):
    tm = min(tm, R)
    assert R % tm == 0 and tm % rs == 0
    n_r, n_c, n_o, n_a = len(rows), len(consts), len(out_rows), len(out_accs)
    nsteps = R // tm
    widths = [w for _, _, w in rows]
    if cs is not None:
        assert all(w == widths[0] for w in widths) and widths[0] % cs == 0
        col_chunks = [(c0, cs) for c0 in range(0, widths[0], cs)]
    else:
        col_chunks = [None]

    def body(*refs):
        r_refs = refs[:n_r]
        c_refs = refs[n_r:n_r + n_c]
        o_refs = refs[n_r + n_c:n_r + n_c + n_o]
        a_refs = refs[n_r + n_c + n_o:n_r + n_c + n_o + n_a]
        s_refs = refs[n_r + n_c + n_o + n_a:]
        i = pl.program_id(0)

        @pl.when(i == 0)
        def _():
            for s in s_refs:
                s[...] = jnp.zeros_like(s)

        def piece(g, carry):
            r0 = pl.multiple_of(g * rs, rs)
            for cc in col_chunks:
                csl = slice(None) if cc is None else slice(cc[0], cc[0] + cc[1])
                rp = [r[pl.ds(r0, rs), csl] for r in r_refs]
                cp = [c[:, csl] for c in c_refs]
                outs, accs = fn(rp, cp)
                for o_ref, o in zip(o_refs, outs):
                    o_ref[pl.ds(r0, rs), csl] = o.astype(o_ref.dtype)
                for s_ref, av in zip(s_refs, accs):
                    s_ref[:, csl] += av
            return carry

        lax.fori_loop(0, tm // rs, piece, 0)

        @pl.when(i == nsteps - 1)
        def _():
            for a_ref, s_ref in zip(a_refs, s_refs):
                a_ref[...] = jnp.sum(s_ref[...], axis=0, keepdims=True)

    in_specs = [pl.BlockSpec((tm, w), functools.partial(lambda i, cb: (i, cb), cb=off // w)) for _, off, w in rows]
    for _, off, w in rows:
        assert off % w == 0
    in_specs += [pl.BlockSpec(c.shape, lambda i: (0, 0)) for c in consts]
    out_specs = [pl.BlockSpec((tm, w), lambda i: (i, 0)) for w, _ in out_rows]
    out_specs += [pl.BlockSpec((1, w), lambda i: (0, 0)) for w in out_accs]
    out_shape = [jax.ShapeDtypeStruct((R, w), dt) for w, dt in out_rows]
    out_shape += [jax.ShapeDtypeStruct((1, w), F32) for w in out_accs]
    return pl.pallas_call(
        body, name=name, out_shape=tuple(out_shape), grid=(nsteps,), in_specs=in_specs, out_specs=tuple(out_specs),
        scratch_shapes=[pltpu.VMEM((rs, w), F32) for w in out_accs],
        compiler_params=_cparams(("arbitrary",)),
    )(*[r for r, _, _ in rows], *consts)


def _rms(x, g):
    r = lax.rsqrt(jnp.mean(x * x, axis=-1, keepdims=True) + NORM_EPS)
    return x * r * g


def _rms_fwd(x, g, name):
    R = x.shape[0]

    def fn(rp, cp):
        return [_rms(rp[0], cp[0])], []

    return _rowwise(fn, [(x, 0, D_MODEL)], [g], [(D_MODEL, BF16)], [], R=R, name=name)[0]


def _rms_bwd(x, g, dh, dres, name):
    R = x.shape[0]

    def fn(rp, cp):
        xv, dhv, drv = rp
        _, pull = jax.vjp(_rms, xv, jnp.broadcast_to(cp[0], xv.shape))
        dx, dgv = pull(dhv)
        return [drv + dx], [dgv]

    dx, dg = _rowwise(fn, [(x, 0, D_MODEL), (dh, 0, D_MODEL), (dres, 0, D_MODEL)], [g], [(D_MODEL, F32)], [D_MODEL],
                      R=R, name=name, rs=8)
    return dx, dg


def _final_loss(x, g, tgt, name):
    R = x.shape[0]

    def loss_rows(xv, gv, tv):
        e = _rms(xv, gv) - tv
        return 0.5 * jnp.mean(e * e, axis=-1, keepdims=True)

    def fn(rp, cp):
        xv, tv = rp
        lr, pull = jax.vjp(lambda a, b: loss_rows(a, b, tv), xv, jnp.broadcast_to(cp[0], xv.shape))
        dx, dgv = pull(jnp.ones_like(lr))
        return [dx], [dgv, jnp.broadcast_to(lr, (lr.shape[0], 128))]

    dx, dg, lsum = _rowwise(fn, [(x, 0, D_MODEL), (tgt, 0, D_MODEL)], [g], [(D_MODEL, F32)], [D_MODEL, 128],
                            R=R, name=name, rs=8)
    return lsum[0, 0], dx, dg


def _neg_expm1(z):
    u = jnp.exp(z)
    safe = jnp.where(u == 1.0, 0.5, u)
    return -jnp.where(u == 1.0, z, (safe - 1.0) * z / jnp.log(safe))


def _lru_ab(xc, pr, pi, lam):
    r = jax.nn.sigmoid(pr)
    i = jax.nn.sigmoid(pi)
    log_a = -LRU_C * r * jax.nn.softplus(-lam)
    a = jnp.exp(log_a)
    b = jnp.sqrt(_neg_expm1(2.0 * log_a)) * (i * xc)
    return a, b


def _gated(h, gate):
    return h * jax.nn.gelu(gate)


def _row_iota8(w):
    return lax.broadcasted_iota(jnp.int32, (8, w), 0)


def _shift_dn(x, halo, s):
    xs = pltpu.roll(x, s, 0)
    hs = pltpu.roll(halo, s, 0)
    first = jnp.where(_row_iota8(x.shape[1]) < s, hs, xs[0:8])
    return jnp.concatenate([first, xs[8:]], axis=0) if x.shape[0] > 8 else first


def _shift_up(x, nxt, s):
    n = x.shape[0]
    xs = pltpu.roll(x, n - s, 0)
    ns = pltpu.roll(nxt, 8 - s, 0)
    last = jnp.where(_row_iota8(x.shape[1]) >= 8 - s, ns, xs[n - 8:])
    return jnp.concatenate([xs[:n - 8], last], axis=0) if n > 8 else last


def _lru_tiles(T):
    tT = min(256, T)
    return tT, T // tT


def _lru_fwd(proj, cw, cb, wax, bax, lam, name):
    T = proj.shape[0]
    W = MIX_W
    tT, nT = _lru_tiles(T)

    def body(x_ref, xh_ref, gt_ref, cw_ref, cb_ref, wax_ref, bax_ref, lam_ref, y_ref, h_ref, a_s, b_s, hc_s):
        i = pl.program_id(0)

        @pl.when(i == 0)
        def _():
            hc_s[...] = jnp.zeros_like(hc_s)

        x = x_ref[...]
        halo = jnp.where(i > 0, xh_ref[...], 0.0)
        w = cw_ref[...]
        xc = (cb_ref[...] + w[3:4] * x + w[2:3] * _shift_dn(x, halo, 1) + w[1:2] * _shift_dn(x, halo, 2)
              + w[0:1] * _shift_dn(x, halo, 3))
        pre = jnp.dot(xc.astype(BF16), wax_ref[...], preferred_element_type=F32) + bax_ref[...]
        a, b = _lru_ab(xc, pre[:, :W], pre[:, W:], lam_ref[...])
        a_s[...] = a
        b_s[...] = b
        row = _row_iota8(W)

        def grp(gi, hprev):
            r0 = pl.multiple_of(gi * 8, 8)
            A = a_s[pl.ds(r0, 8), :]
            B = b_s[pl.ds(r0, 8), :]
            for s in (1, 2, 4):
                As = pltpu.roll(A, s, 0)
                Bs = pltpu.roll(B, s, 0)
                m = row >= s
                B = jnp.where(m, A * Bs + B, B)
                A = jnp.where(m, A * As, A)
            H = A * hprev + B
            h_ref[pl.ds(r0, 8), :] = H
            return H[7:8, :]

        hc_s[0:1, :] = lax.fori_loop(0, tT // 8, grp, hc_s[0:1, :])
        y_ref[...] = _gated(h_ref[...], gt_ref[...]).astype(BF16)

    hb = tT // 8
    return pl.pallas_call(
        body, name=name,
        out_shape=(jax.ShapeDtypeStruct((T, W), BF16), jax.ShapeDtypeStruct((T, W), F32)),
        grid=(nT,),
        in_specs=[pl.BlockSpec((tT, W), lambda i: (i, 0)),
                  pl.BlockSpec((8, W), lambda i: (jnp.maximum(i * hb - 1, 0), 0)),
                  pl.BlockSpec((tT, W), lambda i: (i, 1)),
                  pl.BlockSpec((CONV_W, W), lambda i: (0, 0)), pl.BlockSpec((1, W), lambda i: (0, 0)),
                  pl.BlockSpec((W, 2 * W), lambda i: (0, 0)), pl.BlockSpec((1, 2 * W), lambda i: (0, 0)),
                  pl.BlockSpec((1, W), lambda i: (0, 0))],
        out_specs=(pl.BlockSpec((tT, W), lambda i: (i, 0)), pl.BlockSpec((tT, W), lambda i: (i, 0))),
        scratch_shapes=[pltpu.VMEM((tT, W), F32), pltpu.VMEM((tT, W), F32), pltpu.VMEM((8, W), F32)],
        compiler_params=_cparams(("arbitrary",)),
    )(proj, proj, proj, cw, cb, wax, bax, lam)


def _lru_bwd(proj, h, dy, cw, cb, wax, bax, lam, name):
    T = proj.shape[0]
    W = MIX_W
    tT, nT = _lru_tiles(T)
    hb = tT // 8

    def body(x_ref, xh_ref, gt_ref, h_ref, hh_ref, dy_ref, cw_ref, cb_ref, wax_ref, bax_ref, lam_ref,
             dx_ref, dgt_ref, dcw_ref, dcb_ref, dwax_ref, dbax_ref, dlam_ref,
             al_s, be_s, d_s, ca_s, cd_s, cx_s):
        i = pl.program_id(0)
        ib = nT - 1 - i

        @pl.when(i == 0)
        def _():
            for r in (ca_s, cd_s, cx_s, dcw_ref, dcb_ref, dwax_ref, dbax_ref, dlam_ref):
                r[...] = jnp.zeros_like(r)

        x = x_ref[...]
        halo = jnp.where(ib > 0, xh_ref[...], 0.0)
        w = cw_ref[...]
        x1, x2, x3 = _shift_dn(x, halo, 1), _shift_dn(x, halo, 2), _shift_dn(x, halo, 3)
        xc = cb_ref[...] + w[3:4] * x + w[2:3] * x1 + w[1:2] * x2 + w[0:1] * x3
        xcb = xc.astype(BF16)
        pre = jnp.dot(xcb, wax_ref[...], preferred_element_type=F32) + bax_ref[...]
        (a, _), pull_ab = jax.vjp(_lru_ab, xc, pre[:, :W], pre[:, W:], lam_ref[...])
        hv = h_ref[...]
        hprev = _shift_dn(hv, jnp.where(ib > 0, hh_ref[...], 0.0), 1)
        _, pull_y = jax.vjp(_gated, hv, gt_ref[...])
        dh_out, dgt = pull_y(dy_ref[...])
        dgt_ref[...] = dgt.astype(BF16)
        al_s[...] = _shift_up(a, ca_s[...], 1)
        be_s[...] = dh_out
        row = _row_iota8(W)
        ng = tT // 8

        def grp(k, dnext):
            r0 = pl.multiple_of((ng - 1 - k) * 8, 8)
            A = al_s[pl.ds(r0, 8), :]
            B = be_s[pl.ds(r0, 8), :]
            for s in (1, 2, 4):
                As = pltpu.roll(A, 8 - s, 0)
                Bs = pltpu.roll(B, 8 - s, 0)
                m = row < 8 - s
                B = jnp.where(m, A * Bs + B, B)
                A = jnp.where(m, A * As, A)
            Dg = A * dnext + B
            d_s[pl.ds(r0, 8), :] = Dg
            return Dg[0:1, :]

        lax.fori_loop(0, ng, grp, cd_s[0:1, :])
        Dv = d_s[...]
        dxc1, dpr, dpi, dlam = pull_ab((Dv * hprev, Dv))
        dpre = jnp.concatenate([dpr, dpi], axis=1)
        dpb = dpre.astype(BF16)
        dxc = dxc1 + lax.dot_general(dpb, wax_ref[...], (((1,), (1,)), ((), ())), preferred_element_type=F32)
        dwax_ref[...] += lax.dot_general(xcb, dpb, (((0,), (0,)), ((), ())), preferred_element_type=F32)
        dbax_ref[...] += jnp.sum(dpre, axis=0, keepdims=True)
        dlam_ref[...] += dlam
        dcb_ref[...] += jnp.sum(dxc, axis=0, keepdims=True)
        dcw_ref[...] += jnp.concatenate([jnp.sum(dxc * x3, axis=0, keepdims=True), jnp.sum(dxc * x2, axis=0, keepdims=True),
                                         jnp.sum(dxc * x1, axis=0, keepdims=True), jnp.sum(dxc * x, axis=0, keepdims=True)], axis=0)
        nxt = cx_s[...]
        dx = (w[3:4] * dxc + w[2:3] * _shift_up(dxc, nxt, 1) + w[1:2] * _shift_up(dxc, nxt, 2)
              + w[0:1] * _shift_up(dxc, nxt, 3))
        dx_ref[...] = dx.astype(BF16)
        ca_s[...] = a[0:8]
        cd_s[...] = Dv[0:8]
        cx_s[...] = dxc[0:8]

    rev = lambda i: nT - 1 - i
    const = lambda shape: pl.BlockSpec(shape, lambda i: (0, 0))
    return pl.pallas_call(
        body, name=name,
        out_shape=(jax.ShapeDtypeStruct((T, W), BF16), jax.ShapeDtypeStruct((T, W), BF16),
                   jax.ShapeDtypeStruct((CONV_W, W), F32), jax.ShapeDtypeStruct((1, W), F32),
                   jax.ShapeDtypeStruct((W, 2 * W), F32), jax.ShapeDtypeStruct((1, 2 * W), F32),
                   jax.ShapeDtypeStruct((1, W), F32)),
        grid=(nT,),
        in_specs=[pl.BlockSpec((tT, W), lambda i: (rev(i), 0)),
                  pl.BlockSpec((8, W), lambda i: (jnp.maximum(rev(i) * hb - 1, 0), 0)),
                  pl.BlockSpec((tT, W), lambda i: (rev(i), 1)),
                  pl.BlockSpec((tT, W), lambda i: (rev(i), 0)),
                  pl.BlockSpec((8, W), lambda i: (jnp.maximum(rev(i) * hb - 1, 0), 0)),
                  pl.BlockSpec((tT, W), lambda i: (rev(i), 0)),
                  const((CONV_W, W)), const((1, W)), const((W, 2 * W)), const((1, 2 * W)), const((1, W))],
        out_specs=(pl.BlockSpec((tT, W), lambda i: (rev(i), 0)), pl.BlockSpec((tT, W), lambda i: (rev(i), 0)),
                   const((CONV_W, W)), const((1, W)), const((W, 2 * W)), const((1, 2 * W)), const((1, W))),
        scratch_shapes=[pltpu.VMEM((tT, W), F32), pltpu.VMEM((tT, W), F32), pltpu.VMEM((tT, W), F32),
                        pltpu.VMEM((8, W), F32), pltpu.VMEM((8, W), F32), pltpu.VMEM((8, W), F32)],
        compiler_params=_cparams(("arbitrary",)),
    )(proj, proj, proj, h, h, dy, cw, cb, wax, bax, lam)


def _lru_prep(wa, wx, ba, bx):
    eye = jnp.eye(LRU_BLOCKS, dtype=F32)

    def dense(wb):
        return (wb[:, :, None, :] * eye[:, None, :, None]).reshape(MIX_W, MIX_W)

    wax = jnp.concatenate([dense(wa), dense(wx)], axis=1)
    bax = jnp.concatenate([ba, bx])[None, :]
    return wax, bax


_ATT_QC = 4
_ATT_Q = _ATT_QC * CHUNK
_ATT_KW = (ATT_LEFT + _ATT_QC) * CHUNK
_BVEC_W = _ATT_KW
_N_OFFS = CHUNK - 1 + ATT_BAND


def _attn_bias(rel_bias):
    n_far = ATT_LEFT * CHUNK - MAX_REL + CHUNK
    far = jnp.broadcast_to(rel_bias[:, 2 * MAX_REL:], (ATT_HEADS, n_far))
    near = rel_bias[:, MAX_REL - (CHUNK - 1):2 * MAX_REL][:, ::-1]
    pad = jnp.zeros((ATT_HEADS, _BVEC_W - _N_OFFS), F32)
    return jnp.concatenate([far, near, pad], axis=1)[:, None, :]


def _bias_table(bvec_row):
    return pltpu.roll(jnp.broadcast_to(bvec_row, (_ATT_Q, _BVEC_W)), _BVEC_W - (CHUNK - 1), 1, stride=1, stride_axis=0)


def _bias_table_t(ds):
    r = lax.broadcasted_iota(jnp.int32, (_ATT_Q, _ATT_Q), 0)
    c = lax.broadcasted_iota(jnp.int32, (_ATT_Q, _ATT_Q), 1)
    rev = jnp.dot((r + c == _ATT_Q - 1).astype(F32), ds, preferred_element_type=F32, precision=HI)
    back = pltpu.roll(rev, _BVEC_W - (_ATT_Q - CHUNK), 1, stride=1, stride_axis=0)
    return jnp.sum(back, axis=0, keepdims=True)


_NT = (((1,), (1,)), ((), ()))
_TN = (((0,), (0,)), ((), ()))
_ATT_PAD = ATT_LEFT * CHUNK
_Q_BLK, _K_BLK, _V_BLK = 2 * MIX_W // ATT_HD, 3 * MIX_W // ATT_HD, 4 * MIX_W // ATT_HD


def _in_band():
    first = (lax.broadcasted_iota(jnp.int32, (_ATT_Q, _ATT_KW), 0) // CHUNK) * CHUNK
    k = lax.broadcasted_iota(jnp.int32, (_ATT_Q, _ATT_KW), 1)
    return jnp.logical_and(k >= first, k < first + ATT_BAND)


def _attn_probs(q, kb, bias, in_band, b):
    s = lax.dot_general(q, kb, _NT, preferred_element_type=F32) * (ATT_HD ** -0.5) + bias
    kpos = lax.broadcasted_iota(jnp.int32, s.shape, 1)
    s = jnp.where(jnp.logical_and(in_band, kpos >= _ATT_PAD - b * _ATT_Q), s, MASK_VALUE)
    e = jnp.exp(s - jnp.max(s, axis=-1, keepdims=True))
    return e / jnp.sum(e, axis=-1, keepdims=True)


def _attn_fwd(proj, bias, name, side=None):
    T = proj.shape[0]
    assert T % _ATT_Q == 0
    nB = T // _ATT_Q

    def body(q_ref, k_ref, v_ref, b_ref, o_ref, kp, vp):
        kp[0:_ATT_PAD, :] = jnp.zeros((_ATT_PAD, ATT_HD), BF16)
        vp[0:_ATT_PAD, :] = jnp.zeros((_ATT_PAD, ATT_HD), BF16)
        kp[_ATT_PAD:, :] = k_ref[...].astype(BF16)
        vp[_ATT_PAD:, :] = v_ref[...].astype(BF16)
        bias_v = _bias_table(b_ref[0])
        band = _in_band()

        def step(b, carry):
            r0 = pl.multiple_of(b * _ATT_Q, _ATT_Q)
            q = q_ref[pl.ds(r0, _ATT_Q), :].astype(BF16)
            p = _attn_probs(q, kp[pl.ds(r0, _ATT_KW), :], bias_v, band, b)
            o = jnp.dot(p.astype(BF16), vp[pl.ds(r0, _ATT_KW), :], preferred_element_type=F32)
            o_ref[pl.ds(r0, _ATT_Q), :] = o.astype(BF16)
            return carry

        lax.fori_loop(0, nB, step, 0)

    (out,), got = _call(
        body, name=name, out_shape=[jax.ShapeDtypeStruct((T, MIX_W), BF16)], grid=(ATT_HEADS,),
        in_specs=[pl.BlockSpec((T, ATT_HD), lambda h: (0, _Q_BLK + h)), pl.BlockSpec((T, ATT_HD), lambda h: (0, _K_BLK + h)),
                  pl.BlockSpec((T, ATT_HD), lambda h: (0, _V_BLK + h)), pl.BlockSpec((1, 1, _BVEC_W), lambda h: (h, 0, 0))],
        out_specs=[pl.BlockSpec((T, ATT_HD), lambda h: (0, h))],
        scratch_shapes=[pltpu.VMEM((T + _ATT_PAD, ATT_HD), BF16), pltpu.VMEM((T + _ATT_PAD, ATT_HD), BF16)],
        sem=("arbitrary",), args=(proj, proj, proj, bias), side=side)
    return out if side is None else (out, got)


def _attn_bwd(proj, bias, do, name, side=None):
    T = proj.shape[0]
    assert T % _ATT_Q == 0
    nB = T // _ATT_Q

    def body(q_ref, k_ref, v_ref, b_ref, do_ref, dq_ref, dk_ref, dv_ref, db_ref, kp, vp, dkp, dvp, dbs):
        kp[0:_ATT_PAD, :] = jnp.zeros((_ATT_PAD, ATT_HD), BF16)
        vp[0:_ATT_PAD, :] = jnp.zeros((_ATT_PAD, ATT_HD), BF16)
        kp[_ATT_PAD:, :] = k_ref[...].astype(BF16)
        vp[_ATT_PAD:, :] = v_ref[...].astype(BF16)
        dkp[...] = jnp.zeros_like(dkp)
        dvp[...] = jnp.zeros_like(dvp)
        dbs[...] = jnp.zeros_like(dbs)
        bias_v = _bias_table(b_ref[0])
        band = _in_band()

        def step(b, carry):
            r0 = pl.multiple_of(b * _ATT_Q, _ATT_Q)
            q = q_ref[pl.ds(r0, _ATT_Q), :].astype(BF16)
            kb = kp[pl.ds(r0, _ATT_KW), :]
            vb = vp[pl.ds(r0, _ATT_KW), :]
            dob = do_ref[pl.ds(r0, _ATT_Q), :].astype(BF16)
            p = _attn_probs(q, kb, bias_v, band, b)
            dp = lax.dot_general(dob, vb, _NT, preferred_element_type=F32)
            ds = p * (dp - jnp.sum(p * dp, axis=-1, keepdims=True))
            dbs[...] += ds
            dsb = (ds * (ATT_HD ** -0.5)).astype(BF16)
            dq_ref[pl.ds(r0, _ATT_Q), :] = jnp.dot(dsb, kb, preferred_element_type=F32).astype(BF16)
            dkp[pl.ds(r0, _ATT_KW), :] += lax.dot_general(dsb, q, _TN, preferred_element_type=F32)
            dvp[pl.ds(r0, _ATT_KW), :] += lax.dot_general(p.astype(BF16), dob, _TN, preferred_element_type=F32)
            return carry

        lax.fori_loop(0, nB, step, 0)
        dk_ref[...] = dkp[_ATT_PAD:, :].astype(BF16)
        dv_ref[...] = dvp[_ATT_PAD:, :].astype(BF16)
        db_ref[0] = _bias_table_t(dbs[...])

    hspec = pl.BlockSpec((T, ATT_HD), lambda h: (0, h))
    osd = jax.ShapeDtypeStruct((T, MIX_W), BF16)
    outs, got = _call(
        body, name=name,
        out_shape=[osd, osd, osd, jax.ShapeDtypeStruct((ATT_HEADS, 1, _BVEC_W), F32)], grid=(ATT_HEADS,),
        in_specs=[pl.BlockSpec((T, ATT_HD), lambda h: (0, _Q_BLK + h)), pl.BlockSpec((T, ATT_HD), lambda h: (0, _K_BLK + h)),
                  pl.BlockSpec((T, ATT_HD), lambda h: (0, _V_BLK + h)), pl.BlockSpec((1, 1, _BVEC_W), lambda h: (h, 0, 0)),
                  hspec],
        out_specs=[hspec, hspec, hspec, pl.BlockSpec((1, 1, _BVEC_W), lambda h: (h, 0, 0))],
        scratch_shapes=[pltpu.VMEM((T + _ATT_PAD, ATT_HD), BF16), pltpu.VMEM((T + _ATT_PAD, ATT_HD), BF16),
                        pltpu.VMEM((T + _ATT_PAD, ATT_HD), F32), pltpu.VMEM((T + _ATT_PAD, ATT_HD), F32),
                        pltpu.VMEM((_ATT_Q, _ATT_KW), F32)],
        sem=("arbitrary",), args=(proj, proj, proj, bias, do), side=side)
    return outs if side is None else (outs, got)


def _s5_prep(a_re, a_im, b_re, b_im, c_re, c_im, d, log_step):
    step = jnp.exp(log_step)[:, None]
    mag = jnp.exp(a_re * step)
    ang = a_im * step
    lb_re = mag * jnp.cos(ang)
    lb_im = mag * jnp.sin(ang)
    den = a_re * a_re + a_im * a_im
    nr = lb_re - 1.0
    coef_re = (nr * a_re + lb_im * a_im) / den
    coef_im = (lb_im * a_re - nr * a_im) / den
    bb_re = coef_re[..., None] * b_re - coef_im[..., None] * b_im
    bb_im = coef_re[..., None] * b_im + coef_im[..., None] * b_re
    prs, pis = [jnp.ones_like(lb_re)], [jnp.zeros_like(lb_re)]
    for _ in range(S5_L):
        prs.append(prs[-1] * lb_re - pis[-1] * lb_im)
        pis.append(prs[-2] * lb_im + pis[-1] * lb_re)
    PR, PI = jnp.stack(prs), jnp.stack(pis)
    cl_re = c_re[None] * PR[:, :, None, :] - c_im[None] * PI[:, :, None, :]
    cl_im = c_re[None] * PI[:, :, None, :] + c_im[None] * PR[:, :, None, :]
    kt = (jnp.einsum("tghp,gpk->tghk", cl_re[:S5_L], bb_re, precision=HI)
          - jnp.einsum("tghp,gpk->tghk", cl_im[:S5_L], bb_im, precision=HI))
    steps = np.arange(S5_L)
    lag = (steps[:, None, None] == steps[None, None, :] - steps[None, :, None]).astype(np.float32)
    tgt = jnp.einsum("tpl,tghk->gpklh", lag, kt, precision=HI).reshape(SSM_G, S5_LW, S5_LW)
    prr, pir = PR[:S5_L][::-1], PI[:S5_L][::-1]
    bret = (prr[:, :, None, :] * bb_re.transpose(0, 2, 1)[None] - pir[:, :, None, :] * bb_im.transpose(0, 2, 1)[None])
    bimt = (prr[:, :, None, :] * bb_im.transpose(0, 2, 1)[None] + pir[:, :, None, :] * bb_re.transpose(0, 2, 1)[None])
    bret = bret.transpose(1, 0, 2, 3).reshape(SSM_G, S5_LW, SSM_P)
    bimt = bimt.transpose(1, 0, 2, 3).reshape(SSM_G, S5_LW, SSM_P)
    cre = cl_re[1:].transpose(1, 3, 0, 2).reshape(SSM_G, SSM_P, S5_LW)
    cim = (-cl_im[1:]).transpose(1, 3, 0, 2).reshape(SSM_G, SSM_P, S5_LW)
    dflat = jnp.broadcast_to(d.reshape(SSM_G, 1, SSM_H), (SSM_G, S5_L, SSM_H)).reshape(SSM_G, 1, S5_LW)
    return tgt, bret, bimt, cre, cim, PR[S5_L], PI[S5_L], dflat


_S5_GB = 8


def _bdot(a, b, dims):
    return lax.dot_general(a, b, dims, preferred_element_type=F32, precision=HI)


_B_NN = (((2,), (1,)), ((0,), (0,)))
_B_NT = (((2,), (2,)), ((0,), (0,)))
_B_TN = (((1,), (1,)), ((0,), (0,)))


def _gspec(shape):
    return pl.BlockSpec((_S5_GB,) + shape, lambda g: (g, 0, 0))


def _s5_in(u, bret, bimt, name):
    C = u.shape[1]

    def body(u_ref, br_ref, bi_ref, sr_ref, si_ref):
        uv = u_ref[...]
        sr_ref[...] = _bdot(uv, br_ref[...], _B_NN)
        si_ref[...] = _bdot(uv, bi_ref[...], _B_NN)

    sd = jax.ShapeDtypeStruct((SSM_G, C, SSM_P), F32)
    return pl.pallas_call(
        body, name=name, out_shape=(sd, sd), grid=(SSM_G // _S5_GB,),
        in_specs=[_gspec((C, S5_LW)), _gspec((S5_LW, SSM_P)), _gspec((S5_LW, SSM_P))],
        out_specs=(_gspec((C, SSM_P)), _gspec((C, SSM_P))), compiler_params=_cparams(("parallel",)),
    )(u, bret, bimt)


def _s5_scan(sin_re, sin_im, lr, li, name):
    C = sin_re.shape[0]

    def body(ir_ref, ii_ref, lr_ref, li_ref, or_ref, oi_ref):
        lrv, liv = lr_ref[...], li_ref[...]

        def step(c, s):
            sr, si = s
            or_ref[c] = sr
            oi_ref[c] = si
            return lrv * sr - liv * si + ir_ref[c], lrv * si + liv * sr + ii_ref[c]

        z = jnp.zeros((SSM_G, SSM_P), F32)
        lax.fori_loop(0, C, step, (z, z))

    sd = jax.ShapeDtypeStruct((C, SSM_G, SSM_P), F32)
    return pl.pallas_call(body, name=name, out_shape=(sd, sd), compiler_params=_cparams())(sin_re, sin_im, lr, li)


def _s5_out(u, sp_re, sp_im, tgt, cre, cim, dflat, name):
    C = u.shape[1]

    def body(u_ref, sr_ref, si_ref, t_ref, cr_ref, ci_ref, d_ref, pre_ref, y_ref):
        uv = u_ref[...]
        pre = (_bdot(uv, t_ref[...], _B_NN) + _bdot(sr_ref[...], cr_ref[...], _B_NN)
               + _bdot(si_ref[...], ci_ref[...], _B_NN) + d_ref[...] * uv)
        pre_ref[...] = pre
        y_ref[...] = jax.nn.gelu(pre).astype(BF16)

    return pl.pallas_call(
        body, name=name,
        out_shape=(jax.ShapeDtypeStruct((SSM_G, C, S5_LW), F32), jax.ShapeDtypeStruct((SSM_G, C, S5_LW), BF16)),
        grid=(SSM_G // _S5_GB,),
        in_specs=[_gspec((C, S5_LW)), _gspec((C, SSM_P)), _gspec((C, SSM_P)), _gspec((S5_LW, S5_LW)),
                  _gspec((SSM_P, S5_LW)), _gspec((SSM_P, S5_LW)), _gspec((1, S5_LW))],
        out_specs=(_gspec((C, S5_LW)), _gspec((C, S5_LW))), compiler_params=_cparams(("parallel",)),
    )(u, sp_re, sp_im, tgt, cre, cim, dflat)


def _s5_bwd_out(dy, pre, u, sp_re, sp_im, cre, cim, name):
    C = u.shape[1]

    def body(dy_ref, pre_ref, u_ref, sr_ref, si_ref, cr_ref, ci_ref,
             dpre_ref, dsr_ref, dsi_ref, dt_ref, dcr_ref, dci_ref, dd_ref):
        _, pull = jax.vjp(jax.nn.gelu, pre_ref[...])
        dpre = pull(dy_ref[...])[0]
        uv = u_ref[...]
        dpre_ref[...] = dpre
        dsr_ref[...] = _bdot(dpre, cr_ref[...], _B_NT)
        dsi_ref[...] = _bdot(dpre, ci_ref[...], _B_NT)
        dt_ref[...] = _bdot(uv, dpre, _B_TN)
        dcr_ref[...] = _bdot(sr_ref[...], dpre, _B_TN)
        dci_ref[...] = _bdot(si_ref[...], dpre, _B_TN)
        dd_ref[...] = jnp.sum(dpre * uv, axis=1, keepdims=True)

    sd = jax.ShapeDtypeStruct
    return pl.pallas_call(
        body, name=name,
        out_shape=(sd((SSM_G, C, S5_LW), F32), sd((SSM_G, C, SSM_P), F32), sd((SSM_G, C, SSM_P), F32),
                   sd((SSM_G, S5_LW, S5_LW), F32), sd((SSM_G, SSM_P, S5_LW), F32), sd((SSM_G, SSM_P, S5_LW), F32),
                   sd((SSM_G, 1, S5_LW), F32)),
        grid=(SSM_G // _S5_GB,),
        in_specs=[_gspec((C, S5_LW)), _gspec((C, S5_LW)), _gspec((C, S5_LW)), _gspec((C, SSM_P)), _gspec((C, SSM_P)),
                  _gspec((SSM_P, S5_LW)), _gspec((SSM_P, S5_LW))],
        out_specs=(_gspec((C, S5_LW)), _gspec((C, SSM_P)), _gspec((C, SSM_P)), _gspec((S5_LW, S5_LW)),
                   _gspec((SSM_P, S5_LW)), _gspec((SSM_P, S5_LW)), _gspec((1, S5_LW))),
        compiler_params=_cparams(("parallel",)),
    )(dy, pre, u, sp_re, sp_im, cre, cim)


def _s5_rscan(dsp_re, dsp_im, sp_re, sp_im, lr, li, name):
    C = dsp_re.shape[0]

    def body(gr_ref, gi_ref, sr_ref, si_ref, lr_ref, li_ref, or_ref, oi_ref, dlr_ref, dli_ref):
        lrv, liv = lr_ref[...], li_ref[...]

        def step(k, carry):
            c = C - 1 - k
            dr, di, alr, ali = carry
            or_ref[c] = dr
            oi_ref[c] = di
            sr, si = sr_ref[c], si_ref[c]
            alr = alr + dr * sr + di * si
            ali = ali + di * sr - dr * si
            return gr_ref[c] + lrv * dr + liv * di, gi_ref[c] + lrv * di - liv * dr, alr, ali

        z = jnp.zeros((SSM_G, SSM_P), F32)
        _, _, alr, ali = lax.fori_loop(0, C, step, (z, z, z, z))
        dlr_ref[...] = alr
        dli_ref[...] = ali

    sd = jax.ShapeDtypeStruct((C, SSM_G, SSM_P), F32)
    sp = jax.ShapeDtypeStruct((SSM_G, SSM_P), F32)
    return pl.pallas_call(body, name=name, out_shape=(sd, sd, sp, sp), compiler_params=_cparams())(
        dsp_re, dsp_im, sp_re, sp_im, lr, li)


def _s5_bwd_in(dpre, dsin_re, dsin_im, u, tgt, bret, bimt, dflat, name):
    C = u.shape[1]

    def body(dp_ref, dr_ref, di_ref, u_ref, t_ref, br_ref, bi_ref, d_ref, du_ref, dbr_ref, dbi_ref):
        dp = dp_ref[...]
        dr, di, uv = dr_ref[...], di_ref[...], u_ref[...]
        du = (_bdot(dp, t_ref[...], _B_NT) + _bdot(dr, br_ref[...], _B_NT) + _bdot(di, bi_ref[...], _B_NT)
              + d_ref[...] * dp)
        du_ref[...] = du.astype(BF16)
        dbr_ref[...] = _bdot(uv, dr, _B_TN)
        dbi_ref[...] = _bdot(uv, di, _B_TN)

    sd = jax.ShapeDtypeStruct
    return pl.pallas_call(
        body, name=name,
        out_shape=(sd((SSM_G, C, S5_LW), BF16), sd((SSM_G, S5_LW, SSM_P), F32), sd((SSM_G, S5_LW, SSM_P), F32)),
        grid=(SSM_G // _S5_GB,),
        in_specs=[_gspec((C, S5_LW)), _gspec((C, SSM_P)), _gspec((C, SSM_P)), _gspec((C, S5_LW)),
                  _gspec((S5_LW, S5_LW)), _gspec((S5_LW, SSM_P)), _gspec((S5_LW, SSM_P)), _gspec((1, S5_LW))],
        out_specs=(_gspec((C, S5_LW)), _gspec((S5_LW, SSM_P)), _gspec((S5_LW, SSM_P))),
        compiler_params=_cparams(("parallel",)),
    )(dpre, dsin_re, dsin_im, u, tgt, bret, bimt, dflat)


def _to_chunks(v):
    T = v.shape[0]
    return v.reshape(T // S5_L, S5_L, SSM_G, SSM_H).transpose(2, 0, 1, 3).reshape(SSM_G, T // S5_L, S5_LW)


def _from_chunks(v):
    C = v.shape[1]
    return v.reshape(SSM_G, C, S5_L, SSM_H).transpose(1, 2, 0, 3).reshape(C * S5_L, MIX_W)


def _merge_fn(bra, brb, pc, pg, g0, g1, g2, b0, b1, b2):
    sg = jax.nn.sigmoid
    return sg(g0 + b0) * bra + sg(g1 + b1) * brb + sg(g2 + b2) * (pc * sg(pg))


_EW_CS = 256
_GATE_OFF = 6 * MIX_W


def _merge_rows(br4, proj):
    return [(b, 0, D_MODEL) for b in br4] + [(proj, _GATE_OFF + k * D_MODEL, D_MODEL) for k in range(3)]


def _merge_fwd(br4, proj, gb3, name):
    def fn(rp, cp):
        b = cp[0]
        return [_merge_fn(*rp, b[0:1], b[1:2], b[2:3])], []

    return _rowwise(fn, _merge_rows(br4, proj), [gb3], [(D_MODEL, BF16)], [], R=proj.shape[0], name=name, cs=_EW_CS)[0]


def _merge_bwd(br4, proj, gb3, dm, name):
    def fn(rp, cp):
        b = cp[0]
        shp = rp[0].shape
        bs = [jnp.broadcast_to(b[k:k + 1], shp) for k in range(3)]
        _, pull = jax.vjp(_merge_fn, *rp[:7], *bs)
        g = pull(rp[7])
        return list(g[:7]), list(g[7:])

    rows = _merge_rows(br4, proj) + [(dm, 0, D_MODEL)]
    outs = _rowwise(fn, rows, [gb3], [(D_MODEL, BF16)] * 7, [D_MODEL] * 3, R=proj.shape[0], name=name, tm=128, cs=_EW_CS)
    return outs[:4], outs[4:7], jnp.concatenate(outs[7:], axis=0)


def _swiglu(g, u):
    return jax.nn.silu(g) * u


def _act_fwd(gu, name):
    def fn(rp, cp):
        return [_swiglu(*rp)], []

    return _rowwise(fn, [(gu[0], 0, FFN_H), (gu[1], 0, FFN_H)], [], [(FFN_H, BF16)], [], R=gu[0].shape[0], name=name, cs=_EW_CS)[0]


def _act_bwd(gu, dact, name):
    def fn(rp, cp):
        _, pull = jax.vjp(_swiglu, rp[0], rp[1])
        return list(pull(rp[2])), []

    return _rowwise(fn, [(gu[0], 0, FFN_H), (gu[1], 0, FFN_H), (dact, 0, FFN_H)], [], [(FFN_H, BF16)] * 2, [],
                    R=dact.shape[0], name=name, tm=128, cs=_EW_CS)


def _adamw_fn(w, g, m, v):
    m = ADAM_B1 * m + (1.0 - ADAM_B1) * g
    v = ADAM_B2 * v + (1.0 - ADAM_B2) * jnp.square(g)
    m_hat = m / (1.0 - ADAM_B1 ** ADAM_STEP)
    v_hat = v / (1.0 - ADAM_B2 ** ADAM_STEP)
    delta = -ADAM_LR * (m_hat / (jnp.sqrt(v_hat) + ADAM_EPS) + ADAM_WD * w)
    return delta, m, v


def _adamw(w, g, m, v, name):
    R, C = w.shape

    def fn(rp, cp):
        return list(_adamw_fn(*rp)), []

    cs = _pick(C, (512, 256, 128))
    tm = _pick(R, (256, 128, 64, 32, 16, 8))
    return _rowwise(fn, [(w, 0, C), (g, 0, C), (m, 0, C), (v, 0, C)], [], [(C, F32)] * 3, [], R=R, name=name,
                    tm=tm, rs=8, cs=cs)


def _my_place():
    return lax.axis_index("x"), lax.axis_index("y"), lax.axis_index("c")


def _other_chips(x, y):
    return [(1 - x, y), (x, 1 - y), (1 - x, 1 - y)]


_ANY = pl.BlockSpec(memory_space=pl.ANY)


def _rcopy(src, dst, ssem, rsem, to):
    return pltpu.make_async_remote_copy(src_ref=src, dst_ref=dst, send_sem=ssem, recv_sem=rsem, device_id=to,
                                        device_id_type=MESH)


def _place_shard(local, axis, jidx, out_dtype, name):
    lead, r, c = local.shape
    shp = [lead, r, c]
    shp[axis] *= N_CHIPS
    tr = _pick(r, (512, 256, 128)) if r % 128 == 0 else r
    nr = r // tr
    omap = (lambda l, i, j: (l, i, j[0])) if axis == 2 else (lambda l, i, j: (l, j[0] * nr + i, 0))

    def body(j_ref, x_ref, o_ref):
        o_ref[...] = x_ref[...].astype(out_dtype)

    return pl.pallas_call(
        body, name=name, out_shape=jax.ShapeDtypeStruct(tuple(shp), out_dtype),
        grid_spec=pltpu.PrefetchScalarGridSpec(
            num_scalar_prefetch=1, grid=(lead, nr),
            in_specs=[pl.BlockSpec((None, tr, c), lambda l, i, j: (l, i, 0))],
            out_specs=pl.BlockSpec((None, tr, c), omap)),
        compiler_params=_cparams(("parallel", "parallel")),
    )(jidx, local)


class _GatherSide:
    def __init__(self, fulls, axes, regions):
        self.inputs = list(fulls)
        self.out_shapes = [jax.ShapeDtypeStruct(f.shape, f.dtype) for f in fulls]
        self.aliases = {t: t for t in range(len(fulls))}
        self.n_sems = 6 * len(fulls)
        self.axes, self.regions = list(axes), list(regions)

    def _block(self, outs, t, chip, half):
        start, size, split = self.regions[t][:3]
        piece, n_pieces = self.regions[t][3] if len(self.regions[t]) > 3 else (0, 1)
        ax = self.axes[t]
        cut = outs[t].shape[ax] // N_CHIPS
        j = 2 * chip[0] + chip[1]
        idx = [pl.ds(start, size), slice(None), slice(None)]
        idx[ax] = pl.ds(j * cut, cut)
        if split == "lead":
            idx[0] = pl.ds(start + half * (size // 2), size // 2)
        else:
            other = 3 - ax
            h = outs[t].shape[other] // (2 * n_pieces)
            idx[other] = pl.ds((half * n_pieces + piece) * h, h)
        return outs[t].at[tuple(idx)]

    def _sends(self, outs, send, recv):
        x, y, c = _my_place()
        cps = []
        for t in range(len(outs)):
            mine = self._block(outs, t, (x, y), c)
            for r, chip in enumerate(_other_chips(x, y)):
                k = 3 * t + r
                cps.append(_rcopy(mine, mine, send.at[k], recv.at[k], (*chip, c)))
        return cps

    def start(self, ins, outs, send, recv):
        for cp in self._sends(outs, send, recv):
            cp.start()

    def finish(self, ins, outs, send, recv):
        x, y, c = _my_place()
        sib = (x, y, 1 - c)
        n = len(outs)
        chips = _other_chips(x, y)
        passed = []
        for t in range(n):
            for r, chip in enumerate(chips):
                k = 3 * t + r
                landed = self._block(outs, t, chip, c)
                _rcopy(landed, landed, send.at[k], recv.at[k], (*chip, c)).wait_recv()
                cp = _rcopy(landed, landed, send.at[3 * n + k], recv.at[3 * n + k], sib)
                cp.start()
                passed.append(cp)
        for t in range(n):
            for r, chip in enumerate(chips):
                k = 3 * n + 3 * t + r
                theirs = self._block(outs, t, chip, 1 - c)
                _rcopy(theirs, theirs, send.at[k], recv.at[k], sib).wait_recv()
        for cp in self._sends(outs, send, recv) + passed:
            cp.wait_send()


def _run_side(side, name):
    s_in = len(side.inputs)

    def body(*refs):
        ins, outs = refs[:s_in], refs[s_in:s_in + len(side.out_shapes)]
        send, recv = refs[s_in + len(side.out_shapes):]
        side.start(ins, outs, send, recv)
        side.finish(ins, outs, send, recv)

    return pl.pallas_call(
        body, name=name, out_shape=tuple(side.out_shapes), in_specs=[_ANY] * s_in,
        out_specs=tuple([_ANY] * len(side.out_shapes)), input_output_aliases=dict(side.aliases),
        scratch_shapes=[pltpu.SemaphoreType.DMA((side.n_sems,)), pltpu.SemaphoreType.DMA((side.n_sems,))],
    )(*side.inputs)


def _half_idx(shape, axis, half):
    size = shape[axis] // 2
    idx = [slice(None), slice(None)]
    idx[axis] = pl.ds(half * size, size)
    return tuple(idx)


class _PairSide:
    def __init__(self, grads, half_axes):
        self.inputs = list(grads)
        self.half_axes = list(half_axes)
        self.out_shapes = []
        for g, ax in zip(grads, half_axes):
            shp = list(g.shape)
            shp[ax] //= 2
            self.out_shapes.append(jax.ShapeDtypeStruct(tuple(shp), g.dtype))
        self.aliases = {}
        self.n_sems = len(grads)

    def _copies(self, srcs, outs, send, recv):
        x, y, c = _my_place()
        return [_rcopy(srcs[t].at[_half_idx(srcs[t].shape, self.half_axes[t], 1 - c)], outs[t], send.at[t], recv.at[t],
                       (x, y, 1 - c)) for t in range(len(srcs))]

    def start(self, srcs, outs, send, recv):
        for cp in self._copies(srcs, outs, send, recv):
            cp.start()

    def finish(self, srcs, outs, send, recv):
        for cp in self._copies(srcs, outs, send, recv):
            cp.wait()


def _pair_sum(g, recv, half_axis, cidx, name):
    K, N = recv.shape
    tm = _pick(K, (256, 128, 64, 32, 16))
    tn = _pick(N, (1024, 1408, 512, 256, 128))
    nbr, nbc = K // tm, N // tn
    if half_axis == 0:
        gmap = lambda i, j, c: (c[0] * nbr + i, j)
    else:
        gmap = lambda i, j, c: (i, c[0] * nbc + j)

    def body(c_ref, g_ref, r_ref, of_ref, ob_ref):
        s = g_ref[...] + r_ref[...]
        of_ref[...] = s
        ob_ref[...] = s.astype(BF16)

    omap = lambda i, j, c: (i, j)
    return pl.pallas_call(
        body, name=name,
        out_shape=(jax.ShapeDtypeStruct((K, N), F32), jax.ShapeDtypeStruct((K, N), BF16)),
        grid_spec=pltpu.PrefetchScalarGridSpec(
            num_scalar_prefetch=1, grid=(nbr, nbc),
            in_specs=[pl.BlockSpec((tm, tn), gmap), pl.BlockSpec((tm, tn), omap)],
            out_specs=(pl.BlockSpec((tm, tn), omap), pl.BlockSpec((tm, tn), omap))),
        compiler_params=_cparams(("parallel", "parallel")),
    )(cidx, g, recv)


def _shard_idx(shape, axis, j):
    size = shape[axis] // N_CHIPS
    idx = [slice(None), slice(None)]
    idx[axis] = pl.ds(j * size, size)
    return tuple(idx)


class _ScatterSide:
    def __init__(self, parts, shard_axes):
        self.inputs = list(parts)
        self.shard_axes = list(shard_axes)
        self.out_shapes = []
        for p, ax in zip(parts, shard_axes):
            shp = list(p.shape)
            shp[ax] //= N_CHIPS
            self.out_shapes.append(jax.ShapeDtypeStruct((3,) + tuple(shp), p.dtype))
        self.aliases = {}
        self.n_sems = 3 * len(parts)

    def _copies(self, srcs, outs, send, recv):
        x, y, c = _my_place()
        cps = []
        for t in range(len(srcs)):
            for r, chip in enumerate(_other_chips(x, y)):
                k = 3 * t + r
                j = 2 * chip[0] + chip[1]
                cps.append(_rcopy(srcs[t].at[_shard_idx(srcs[t].shape, self.shard_axes[t], j)], outs[t].at[r],
                                  send.at[k], recv.at[k], (*chip, c)))
        return cps

    def start(self, srcs, outs, send, recv):
        for cp in self._copies(srcs, outs, send, recv):
            cp.start()

    def finish(self, srcs, outs, send, recv):
        for cp in self._copies(srcs, outs, send, recv):
            cp.wait()


def _shard_sum(pf, recv, acc, layer, shard_axis, jcidx, name):
    _, K, N = recv.shape
    tm = _pick(K, (256, 128, 64, 32, 16))
    tn = _pick(N, (1024, 1408, 512, 256, 128))
    nbr, nbc = K // tm, N // tn
    if shard_axis == 0:
        pmap = lambda i, j, s: (s[0] * nbr + i, j)
        omap = lambda i, j, s: (layer, i, s[1] * nbc + j)
    else:
        pmap = lambda i, j, s: (i, s[0] * nbc + j)
        omap = lambda i, j, s: (layer, s[1] * nbr + i, j)

    def body(j_ref, p_ref, r_ref, a_ref, o_ref):
        o_ref[...] = ((p_ref[...] + r_ref[0].astype(F32)) + r_ref[1].astype(F32)) + r_ref[2].astype(F32)

    return pl.pallas_call(
        body, name=name, out_shape=jax.ShapeDtypeStruct(acc.shape, F32),
        grid_spec=pltpu.PrefetchScalarGridSpec(
            num_scalar_prefetch=1, grid=(nbr, nbc),
            in_specs=[pl.BlockSpec((tm, tn), pmap), pl.BlockSpec((3, tm, tn), lambda i, j, s: (0, i, j)), _ANY],
            out_specs=pl.BlockSpec((None, tm, tn), omap)),
        input_output_aliases={3: 0},
        compiler_params=_cparams(("parallel", "parallel")),
    )(jcidx, pf, recv, acc)


def _pair_join(accs, half_axes, name):
    n = len(accs)

    def body(*refs):
        outs = refs[n:2 * n]
        send_sems, recv_sems = refs[2 * n:]
        x, y, c = _my_place()
        sib = (x, y, 1 - c)

        def half(t, hc):
            return outs[t].at[(slice(None),) + _half_idx(outs[t].shape[1:], half_axes[t], hc)]

        cps = []
        for t in range(n):
            cp = _rcopy(half(t, c), half(t, c), send_sems.at[t], recv_sems.at[t], sib)
            cp.start()
            cps.append(cp)
        for t in range(n):
            _rcopy(half(t, 1 - c), half(t, 1 - c), send_sems.at[t], recv_sems.at[t], sib).wait_recv()
        for cp in cps:
            cp.wait_send()

    return pl.pallas_call(
        body, name=name, out_shape=tuple(jax.ShapeDtypeStruct(a.shape, a.dtype) for a in accs),
        in_specs=[_ANY] * n, out_specs=tuple([_ANY] * n), input_output_aliases={t: t for t in range(n)},
        scratch_shapes=[pltpu.SemaphoreType.DMA((n,)), pltpu.SemaphoreType.DMA((n,))],
    )(*accs)


_N_DEV = 8


def _allreduce_small(flat, name):
    _, R, _ = flat.shape

    def body(in_ref, out_ref, stage, send1, recv1, send2, recv2):
        x, y, c = _my_place()
        me = 4 * x + 2 * y + c
        places = [(px, py, pc) for px in range(2) for py in range(2) for pc in range(2)]
        def peer(r):
            return (x ^ (r >> 2), y ^ ((r >> 1) & 1), c ^ (r & 1))

        def peer_id(r):
            p = peer(r)
            return 4 * p[0] + 2 * p[1] + p[2]

        stage[0] = in_ref[me]
        cps = []
        for r in range(1, _N_DEV):
            cp = _rcopy(in_ref.at[peer_id(r)], stage.at[r], send1.at[r], recv1.at[r], peer(r))
            cp.start()
            cps.append(cp)
        for cp in cps:
            cp.wait()
        tot = jnp.zeros((R, 128), F32)
        for d in range(_N_DEV):
            tot = tot + stage[me ^ d]
        out_ref[me] = tot
        cps = []
        for r in range(1, _N_DEV):
            cp = _rcopy(out_ref.at[me], out_ref.at[me], send2.at[r], recv2.at[r], peer(r))
            cp.start()
            cps.append(cp)
        for r in range(1, _N_DEV):
            _rcopy(out_ref.at[peer_id(r)], out_ref.at[peer_id(r)], send2.at[r], recv2.at[r], peer(r)).wait_recv()
        for cp in cps:
            cp.wait_send()

    vm = pl.BlockSpec(memory_space=pltpu.VMEM)
    return pl.pallas_call(
        body, name=name, out_shape=jax.ShapeDtypeStruct(flat.shape, F32), in_specs=[vm], out_specs=vm,
        scratch_shapes=[pltpu.VMEM(flat.shape, F32)] + [pltpu.SemaphoreType.DMA((_N_DEV,))] * 4,
        compiler_params=_cparams(),
    )(flat)


_BIG = ("w_in", "ssm_w_glu", "w_branch", "w_out", "w_ffn_gate", "w_ffn_up", "w_ffn_down")
_BIG_SHARD_AXIS = {"w_in": 1, "ssm_w_glu": 1, "w_branch": 1, "w_out": 0, "w_ffn_gate": 1, "w_ffn_up": 1, "w_ffn_down": 0}
_SMALL = ("norm_mix_g", "gate_bias", "lru_conv_w", "lru_conv_b", "lru_wa", "lru_ba", "lru_wx", "lru_bx", "lru_lambda",
          "attn_rel_bias", "ssm_a_re", "ssm_a_im", "ssm_b_re", "ssm_b_im", "ssm_c_re", "ssm_c_im", "ssm_d",
          "ssm_log_step", "norm_ffn_g", "norm_final_g")
_SMALL_SHARDED = {"gate_bias": 2, "lru_conv_w": 2}
_WEIGHTS = ("norm_mix_g", "w_in", "gate_bias", "lru_conv_w", "lru_conv_b", "lru_wa", "lru_ba", "lru_wx", "lru_bx",
            "lru_lambda", "attn_rel_bias", "ssm_a_re", "ssm_a_im", "ssm_b_re", "ssm_b_im", "ssm_c_re", "ssm_c_im",
            "ssm_d", "ssm_log_step", "ssm_w_glu", "w_branch", "w_out", "norm_ffn_g", "w_ffn_gate", "w_ffn_up",
            "w_ffn_down", "norm_final_g")


def _carried(comm, phase, l, key, W, fn, *args, **kw):
    side = comm.side(phase, l, key, W)
    if side is None:
        return fn(*args, **kw)
    out, got = fn(*args, side=side, **kw)
    comm.took(phase, l, key, got, W)
    return out


def _layer_fwd(l, x, W, sm, comm):
    T = x.shape[0]
    nm = lambda s: f"{s}"
    h1 = _rms_fwd(x, sm["norm_mix_g"][l][None, :], nm("rms_fwd"))
    proj = _carried(comm, "fwd", l, "mm_in", W, _mm, h1, W["w_in"], M=T, N=IN_W, K=D_MODEL, b_lead=l, name=nm("mm_in"))
    wax, bax = sm["lru_prep"][l]
    cw, cb, lam = sm["lru_conv_w"][l], sm["lru_conv_b"][l][None, :], sm["lru_lambda"][l][None, :]
    y_a, hst = _lru_fwd(proj, cw, cb, wax, bax, lam, nm("lru_fwd"))
    bias = sm["attn_bias"][l]
    y_b = _carried(comm, "fwd", l, "attn_fwd", W, _attn_fwd, proj, bias, nm("attn_fwd"))
    tgt, bret, bimt, cre, cim, lr, li, dflat = sm["s5_prep"][l]
    u = _to_chunks(lax.slice_in_dim(proj, 5 * MIX_W, 6 * MIX_W, axis=1))
    sin_re, sin_im = _s5_in(u, bret, bimt, nm("s5_in"))
    sp_re, sp_im = _s5_scan(sin_re.transpose(1, 0, 2), sin_im.transpose(1, 0, 2), lr, li, nm("s5_scan"))
    sp_re, sp_im = sp_re.transpose(1, 0, 2), sp_im.transpose(1, 0, 2)
    pre, ycf = _s5_out(u, sp_re, sp_im, tgt, cre, cim, dflat, nm("s5_out"))
    y_c = _from_chunks(ycf)
    brs = []
    for k, yk in enumerate((y_a, y_b, y_c)):
        brs.append(_mm(yk, W["w_branch"], M=T, N=D_MODEL, K=MIX_W, b_lead=3 * l + k, name=nm("mm_branch")))
    brs.append(_mm(y_c, W["ssm_w_glu"], M=T, N=D_MODEL, K=MIX_W, b_lead=l, name=nm("mm_branch")))
    br4 = tuple(brs)
    gb3 = sm["gate_bias"][l]
    merged = _merge_fwd(br4, proj, gb3, nm("merge_fwd"))
    x1 = _mm(merged, W["w_out"], M=T, N=D_MODEL, K=D_MODEL, b_lead=l, res=x, name=nm("mm_out"))
    h2 = _rms_fwd(x1, sm["norm_ffn_g"][l][None, :], nm("rms_fwd"))
    gpre = _carried(comm, "fwd", l, "mm_ffn_gate", W, _mm, h2, W["w_ffn_gate"], M=T, N=FFN_H, K=D_MODEL, b_lead=l,
                    name=nm("mm_ffn_up"))
    upre = _carried(comm, "fwd", l, "mm_ffn_up", W, _mm, h2, W["w_ffn_up"], M=T, N=FFN_H, K=D_MODEL, b_lead=l,
                    name=nm("mm_ffn_up"))
    gu = (gpre, upre)
    act = _act_fwd(gu, nm("act_fwd"))
    x2 = _carried(comm, "fwd", l, "mm_down", W, _mm, act, W["w_ffn_down"], M=T, N=D_MODEL, K=FFN_H, b_lead=l, res=x1,
                  name=nm("mm_down"))
    saved = dict(x=x, h1=h1, proj=proj, hst=hst, y_a=y_a, y_b=y_b, y_c=y_c, u=u, sp_re=sp_re, sp_im=sp_im, pre=pre,
                 br4=br4, merged=merged, x1=x1, h2=h2, gu=gu, act=act)
    return x2, saved


def _layer_bwd(l, dx2, sv, W, sm, comm):
    T = dx2.shape[0]
    nm = lambda s: f"{s}"
    big, small = {}, {}
    dxb = dx2.astype(BF16)
    big["w_ffn_down"] = _carried(comm, "bwd", l, "mm_dw_down", W, _mm, sv["act"], dxb, M=FFN_H, N=D_MODEL, K=T, ta=True,
                                 name=nm("mm_dw_down"))
    dact = _carried(comm, "bwd", l, "mm_dact", W, _mm, dxb, W["w_ffn_down"], M=T, N=FFN_H, K=D_MODEL, tb=True, b_lead=l,
                    name=nm("mm_dact"))
    dg, du = _act_bwd(sv["gu"], dact, nm("act_bwd"))
    big["w_ffn_gate"] = _carried(comm, "bwd", l, "mm_dw_gate", W, _mm, sv["h2"], dg, M=D_MODEL, N=FFN_H, K=T, ta=True,
                                 name=nm("mm_dw_up"))
    big["w_ffn_up"] = _mm(sv["h2"], du, M=D_MODEL, N=FFN_H, K=T, ta=True, name=nm("mm_dw_up"))
    dh2 = _mm(dg, W["w_ffn_gate"], M=T, N=D_MODEL, K=FFN_H, tb=True, b_lead=l, name=nm("mm_dh2"))
    dh2 = _mm(du, W["w_ffn_up"], M=T, N=D_MODEL, K=FFN_H, tb=True, b_lead=l, res=dh2, name=nm("mm_dh2r"))
    dx1, dgn = _rms_bwd(sv["x1"], sm["norm_ffn_g"][l][None, :], dh2, dx2, nm("rms_bwd"))
    small["norm_ffn_g"] = dgn[0]
    dx1b = dx1.astype(BF16)
    big["w_out"] = _mm(sv["merged"], dx1b, M=D_MODEL, N=D_MODEL, K=T, ta=True, name=nm("mm_dw_out"))
    dm = _mm(dx1b, W["w_out"], M=T, N=D_MODEL, K=D_MODEL, tb=True, b_lead=l, name=nm("mm_dmerged"))
    dbr, dgates, dgb = _merge_bwd(sv["br4"], sv["proj"], sm["gate_bias"][l], dm, nm("merge_bwd"))
    small["gate_bias"] = dgb
    ys = (sv["y_a"], sv["y_b"], sv["y_c"])
    big["w_branch"] = [_mm(ys[k], dbr[k], M=MIX_W, N=D_MODEL, K=T, ta=True, name=nm("mm_dw_branch")) for k in range(3)]
    big["ssm_w_glu"] = _mm(sv["y_c"], dbr[3], M=MIX_W, N=D_MODEL, K=T, ta=True, name=nm("mm_dw_branch"))
    dya = _mm(dbr[0], W["w_branch"], M=T, N=MIX_W, K=D_MODEL, tb=True, b_lead=3 * l, name=nm("mm_dy"))
    dyb = _mm(dbr[1], W["w_branch"], M=T, N=MIX_W, K=D_MODEL, tb=True, b_lead=3 * l + 1, name=nm("mm_dy"))
    dyc = _mm(dbr[2], W["w_branch"], M=T, N=MIX_W, K=D_MODEL, tb=True, b_lead=3 * l + 2, name=nm("mm_dy"))
    dyc = _mm(dbr[3], W["ssm_w_glu"], M=T, N=MIX_W, K=D_MODEL, tb=True, b_lead=l, res=dyc, name=nm("mm_dyr"))
    tgt, bret, bimt, cre, cim, lr, li, dflat = sm["s5_prep"][l]
    dpre, dsp_re, dsp_im, d_tgt, d_cre, d_cim, d_dflat = _s5_bwd_out(
        _to_chunks(dyc), sv["pre"], sv["u"], sv["sp_re"], sv["sp_im"], cre, cim, nm("s5_bwd_out"))
    tr = lambda a: a.transpose(1, 0, 2)
    dsin_re, dsin_im, d_lr, d_li = _s5_rscan(tr(dsp_re), tr(dsp_im), tr(sv["sp_re"]), tr(sv["sp_im"]), lr, li, nm("s5_rscan"))
    du_f, d_bret, d_bimt = _s5_bwd_in(dpre, dsin_re.transpose(1, 0, 2), dsin_im.transpose(1, 0, 2), sv["u"], tgt, bret, bimt,
                                      dflat, nm("s5_bwd_in"))
    d_u = _from_chunks(du_f)
    small["s5_tables"] = (d_tgt, d_bret, d_bimt, d_cre, d_cim, d_lr, d_li, d_dflat)
    dq, dk, dv, dbias = _carried(comm, "bwd", l, "attn_bwd", W, _attn_bwd, sv["proj"], sm["attn_bias"][l], dyb, nm("attn_bwd"))
    small["attn_bias"] = dbias
    wax, bax = sm["lru_prep"][l]
    cw, cb, lam = sm["lru_conv_w"][l], sm["lru_conv_b"][l][None, :], sm["lru_lambda"][l][None, :]
    d_lx, d_lg, d_cw, d_cb, d_wax, d_bax, d_lam = _lru_bwd(sv["proj"], sv["hst"], dya, cw, cb, wax, bax, lam, nm("lru_bwd"))
    small["lru_conv_w"], small["lru_conv_b"], small["lru_lambda"] = d_cw, d_cb[0], d_lam[0]
    small["lru_tables"] = (d_wax, d_bax)
    dproj = jnp.concatenate([d_lx, d_lg, dq, dk, dv, d_u] + list(dgates), axis=1)
    big["w_in"] = _carried(comm, "bwd", l, "mm_dw_in", W, _mm, sv["h1"], dproj, M=D_MODEL, N=IN_W, K=T, ta=True,
                           name=nm("mm_dw_in"))
    dh1 = _carried(comm, "bwd", l, "mm_dh1", W, _mm, dproj, W["w_in"], M=T, N=D_MODEL, K=IN_W, tb=True, b_lead=l,
                   name=nm("mm_dh1"))
    dx, dgn = _rms_bwd(sv["x"], sm["norm_mix_g"][l][None, :], dh1, dx1, nm("rms_bwd"))
    small["norm_mix_g"] = dgn[0]
    return dx, big, small


_TENSORS = tuple((n, k) for n in _BIG for k in range(N_BRANCH if n == "w_branch" else 1))
_FWD_CARRIERS = {"mm_in": ("w_ffn_gate", "w_ffn_up"), "attn_fwd": ("w_ffn_down",), "mm_ffn_gate": ("w_in",),
                 "mm_ffn_up": ("w_in",), "mm_down": ("w_out", "w_branch", "ssm_w_glu")}
_FWD_PIECES = {"mm_ffn_gate": (0, 2), "mm_ffn_up": (1, 2)}
_BWD_PAIR_CARRIERS = {"mm_dw_down": ("w_in", "w_out", "ssm_w_glu"), "mm_dact": ("w_ffn_gate", "w_ffn_up", "w_ffn_down", "w_branch")}
_BWD_CARRIERS = {"mm_dw_gate": ("w_out", "w_branch", "ssm_w_glu"), "attn_bwd": ("w_ffn_down",), "mm_dw_in": ("w_in",),
                 "mm_dh1": ("w_ffn_gate", "w_ffn_up")}


class _StepComm:
    def __init__(self, depth, cidx, jcidx):
        self.depth, self.cidx, self.jcidx = depth, cidx, jcidx
        self.accs = {}
        self.raw = None
        self.paired = {}
        self.pending = None

    def gather_side(self, W, names, l, piece):
        fulls, axes, regions = [], [], []
        for n in names:
            per = N_BRANCH if n == "w_branch" else 1
            fulls.append(W[n])
            axes.append(_BIG_SHARD_AXIS[n] + 1)
            regions.append((per * l, per, "other", piece))
        return _GatherSide(fulls, axes, regions)

    def side(self, phase, l, key, W):
        if phase == "fwd":
            if l + 1 >= self.depth or key not in _FWD_CARRIERS:
                return None
            return self.gather_side(W, _FWD_CARRIERS[key], l + 1, _FWD_PIECES.get(key, (0, 1)))
        if key in _BWD_PAIR_CARRIERS and self.raw is not None:
            nks = [nk for nk in _TENSORS if nk[0] in _BWD_PAIR_CARRIERS[key]]
            return _PairSide([self.raw[1][nk] for nk in nks], [1 - _BIG_SHARD_AXIS[nk[0]] for nk in nks])
        if key in _BWD_CARRIERS and self.pending is not None:
            nks = [nk for nk in _TENSORS if nk[0] in _BWD_CARRIERS[key]]
            return _ScatterSide([self.pending[1][nk][1] for nk in nks], [_BIG_SHARD_AXIS[nk[0]] for nk in nks])
        return None

    def took(self, phase, l, key, got, W):
        if phase == "fwd":
            W.update(zip(_FWD_CARRIERS[key], got))
        elif key in _BWD_PAIR_CARRIERS:
            nks = [nk for nk in _TENSORS if nk[0] in _BWD_PAIR_CARRIERS[key]]
            self.paired.update(zip(nks, got))
            if len(self.paired) == len(_TENSORS):
                self.pair_sums()
        else:
            nks = [nk for nk in _TENSORS if nk[0] in _BWD_CARRIERS[key]]
            self.shard_sums(self.pending[0], nks, got)

    def pair_sums(self):
        l, grads = self.raw
        self.pending = (l, {nk: _pair_sum(grads[nk], self.paired[nk], 1 - _BIG_SHARD_AXIS[nk[0]], self.cidx, "pair_sum")
                            for nk in _TENSORS})
        self.raw, self.paired = None, {}

    def shard_sums(self, l, nks, got):
        for nk, r in zip(nks, got):
            pf = self.pending[1][nk][0]
            sa = _BIG_SHARD_AXIS[nk[0]]
            if nk not in self.accs:
                shp = list(pf.shape)
                shp[sa] //= N_CHIPS
                shp[1 - sa] *= 2
                self.accs[nk] = lax.empty((self.depth,) + tuple(shp), F32)
            self.accs[nk] = _shard_sum(pf, r, self.accs[nk], l, sa, self.jcidx, "shard_sum")

    def on_big(self, l, big):
        grads = {}
        for n in _BIG:
            gs = big[n] if isinstance(big[n], list) else [big[n]]
            grads.update({(n, k): g for k, g in enumerate(gs)})
        self.raw = (l, grads)
        if l == 0:
            half_axes = [1 - _BIG_SHARD_AXIS[nk[0]] for nk in _TENSORS]
            self.paired = dict(zip(_TENSORS, _run_side(_PairSide([grads[nk] for nk in _TENSORS], half_axes), "pair_exchange")))
            self.pair_sums()
            side = _ScatterSide([self.pending[1][nk][1] for nk in _TENSORS], [_BIG_SHARD_AXIS[nk[0]] for nk in _TENSORS])
            self.shard_sums(0, _TENSORS, _run_side(side, "chip_scatter"))
            self.pending = None


class _NoComm:
    def __init__(self, on_big):
        self.on_big = on_big

    def side(self, phase, l, key, W):
        return None


def _local_step(xs, tgt, W, sm, comm):
    W = dict(W)
    sm = dict(sm)
    depth = sm["norm_mix_g"].shape[0]
    lru_prep_vjps, attn_vjps, s5_vjps = [], [], []
    sm["lru_prep"], sm["attn_bias"], sm["s5_prep"] = [], [], []
    for l in range(depth):
        o, f = jax.vjp(_lru_prep, sm["lru_wa"][l], sm["lru_wx"][l], sm["lru_ba"][l], sm["lru_bx"][l])
        sm["lru_prep"].append((o[0].astype(BF16), o[1]))
        lru_prep_vjps.append(f)
        o, f = jax.vjp(_attn_bias, sm["attn_rel_bias"][l])
        sm["attn_bias"].append(o)
        attn_vjps.append(f)
        o, f = jax.vjp(_s5_prep, *[sm[n][l] for n in ("ssm_a_re", "ssm_a_im", "ssm_b_re", "ssm_b_im", "ssm_c_re", "ssm_c_im",
                                                       "ssm_d", "ssm_log_step")])
        sm["s5_prep"].append(o)
        s5_vjps.append(f)

    saved = []
    for l in range(depth):
        xs, sv = _layer_fwd(l, xs, W, sm, comm)
        saved.append(sv)
    loss_part, dx, dgf = _final_loss(xs, sm["norm_final_g"][None, :], tgt, "final_loss")

    small_g = {n: [None] * depth for n in _SMALL if n != "norm_final_g"}
    for l in reversed(range(depth)):
        dx, big, small = _layer_bwd(l, dx, saved[l], W, sm, comm)
        comm.on_big(l, big)
        d_wa, d_wx, d_ba, d_bx = lru_prep_vjps[l](small["lru_tables"])
        (d_rel,) = attn_vjps[l](small["attn_bias"])
        d_s5 = s5_vjps[l](small["s5_tables"])
        vals = dict(norm_mix_g=small["norm_mix_g"], gate_bias=small["gate_bias"], lru_conv_w=small["lru_conv_w"],
                    lru_conv_b=small["lru_conv_b"], lru_wa=d_wa, lru_ba=d_ba, lru_wx=d_wx, lru_bx=d_bx,
                    lru_lambda=small["lru_lambda"], attn_rel_bias=d_rel, ssm_a_re=d_s5[0], ssm_a_im=d_s5[1],
                    ssm_b_re=d_s5[2], ssm_b_im=d_s5[3], ssm_c_re=d_s5[4], ssm_c_im=d_s5[5], ssm_d=d_s5[6],
                    ssm_log_step=d_s5[7], norm_ffn_g=small["norm_ffn_g"])
        for n, val in vals.items():
            small_g[n][l] = val
    small_tree = {n: jnp.stack(small_g[n]) for n in small_g}
    small_tree["norm_final_g"] = dgf[0]
    return loss_part, dx, small_tree


def _pack_small(tree, names):
    flat = jnp.concatenate([tree[n].reshape(-1) for n in names])
    per = -(-flat.shape[0] // (_N_DEV * 128 * 8)) * (128 * 8)
    flat = jnp.pad(flat, (0, _N_DEV * per - flat.shape[0]))
    return flat.reshape(_N_DEV, per // 128, 128)


def _unpack_small(flat, like, names):
    flat = flat.reshape(-1)
    out, off = {}, 0
    for n in names:
        size = math.prod(like[n].shape)
        out[n] = flat[off:off + size].reshape(like[n].shape)
        off += size
    return out


def kernel(x, norm_mix_g, w_in, gate_bias, lru_conv_w, lru_conv_b, lru_wa, lru_ba, lru_wx, lru_bx, lru_lambda, attn_rel_bias, ssm_a_re, ssm_a_im, ssm_b_re, ssm_b_im, ssm_c_re, ssm_c_im, ssm_d, ssm_log_step, ssm_w_glu, w_branch, w_out, norm_ffn_g, w_ffn_gate, w_ffn_up, w_ffn_down, norm_final_g, loss_target, m_norm_mix_g, m_w_in, m_gate_bias, m_lru_conv_w, m_lru_conv_b, m_lru_wa, m_lru_ba, m_lru_wx, m_lru_bx, m_lru_lambda, m_attn_rel_bias, m_ssm_a_re, m_ssm_a_im, m_ssm_b_re, m_ssm_b_im, m_ssm_c_re, m_ssm_c_im, m_ssm_d, m_ssm_log_step, m_ssm_w_glu, m_w_branch, m_w_out, m_norm_ffn_g, m_w_ffn_gate, m_w_ffn_up, m_w_ffn_down, m_norm_final_g, v_norm_mix_g, v_w_in, v_gate_bias, v_lru_conv_w, v_lru_conv_b, v_lru_wa, v_lru_ba, v_lru_wx, v_lru_bx, v_lru_lambda, v_attn_rel_bias, v_ssm_a_re, v_ssm_a_im, v_ssm_b_re, v_ssm_b_im, v_ssm_c_re, v_ssm_c_im, v_ssm_d, v_ssm_log_step, v_ssm_w_glu, v_w_branch, v_w_out, v_norm_ffn_g, v_w_ffn_gate, v_w_ffn_up, v_w_ffn_down, v_norm_final_g):
    args = dict(locals())
    w = {n: args[n] for n in _WEIGHTS}
    m = {n: args["m_" + n] for n in _WEIGHTS}
    v = {n: args["v_" + n] for n in _WEIGHTS}
    depth = w_in.shape[0]
    xc, yc, cc = _my_place()
    jchip = 2 * xc + yc
    cidx = jnp.reshape(cc, (1,)).astype(jnp.int32)
    jidx = jnp.reshape(jchip, (1,)).astype(jnp.int32)
    jcidx = jnp.stack([jchip, cc]).astype(jnp.int32)

    blocks = [w[n] for n in _BIG]
    blocks[2] = blocks[2].reshape(depth * N_BRANCH, MIX_W, -1)
    axes = [_BIG_SHARD_AXIS[n] + 1 for n in _BIG] + [_SMALL_SHARDED[n] for n in _SMALL_SHARDED]
    placed = [_place_shard(b, ax, jidx, BF16, "place_shard") for b, ax in zip(blocks, axes)]
    placed += [_place_shard(w[n], _SMALL_SHARDED[n], jidx, F32, "place_shard") for n in _SMALL_SHARDED]
    regions = [(0, N_BRANCH if n == "w_branch" else 1, "other") for n in _BIG] + [(0, depth, "lead")] * len(_SMALL_SHARDED)
    gathered = _run_side(_GatherSide(placed, axes, regions), "gather_weights")
    W = dict(zip(_BIG, gathered))
    sm_full = dict(zip(_SMALL_SHARDED, gathered[len(_BIG):]))
    sm = {n: w[n] for n in _SMALL if n not in _SMALL_SHARDED}
    sm.update(sm_full)

    comm = _StepComm(depth, cidx, jcidx)
    loss_part, dx, small_tree = _local_step(x[0], loss_target[0], W, sm, comm)
    loss = lax.psum(loss_part, ("x", "y", "c"))
    grad_x = dx[None]

    joined = _pair_join([comm.accs[nk] for nk in _TENSORS], [1 - _BIG_SHARD_AXIS[nk[0]] for nk in _TENSORS], "pair_join")
    jd = dict(zip(_TENSORS, joined))
    grads = {}
    for n in _BIG:
        if n == "w_branch":
            grads[n] = jnp.stack([jd[(n, k)] for k in range(N_BRANCH)], axis=1)
        else:
            grads[n] = jd[(n, 0)]
    like = {n: (sm_full[n] if n in _SMALL_SHARDED else w[n]) for n in _SMALL}
    red = _unpack_small(_allreduce_small(_pack_small(small_tree, _SMALL), "allreduce_small"), like, _SMALL)
    for n in _SMALL:
        if n in _SMALL_SHARDED:
            size = w[n].shape[2]
            grads[n] = lax.dynamic_slice_in_dim(red[n], (2 * xc + yc) * size, size, axis=2)
        else:
            grads[n] = red[n]

    delta, new_m, new_v = {}, {}, {}
    for n in _BIG:
        shp = w[n].shape
        two = lambda a: a.reshape(-1, shp[-1])
        d_, m_, v_ = _adamw(two(w[n]), two(grads[n]), two(m[n]), two(v[n]), "adamw")
        delta[n], new_m[n], new_v[n] = d_.reshape(shp), m_.reshape(shp), v_.reshape(shp)
    pk = lambda tree: _pack_small(tree, _SMALL).reshape(-1, 128)
    d_, m_, v_ = _adamw(pk(w), pk(grads), pk(m), pk(v), "adamw_small")
    like_local = {n: w[n] for n in _SMALL}
    for tree, flat in ((delta, d_), (new_m, m_), (new_v, v_)):
        tree.update(_unpack_small(flat, like_local, _SMALL))
    return (loss, grad_x, *[grads[n] for n in _WEIGHTS], *[delta[n] for n in _WEIGHTS], *[new_m[n] for n in _WEIGHTS],
            *[new_v[n] for n in _WEIGHTS])
```

```python
import functools
import math

import jax
import jax.numpy as jnp
import numpy as np
from jax import lax
from jax.experimental import pallas as pl
from jax.experimental.pallas import tpu as pltpu

F32 = jnp.float32
BF16 = jnp.bfloat16

D_MODEL = 2048
MIX_W = 1024
N_BRANCH = 3
LRU_BLOCKS = 16
LRU_BW = 64
CONV_W = 4
LRU_C = 8.0
CHUNK = 64
ATT_HEADS = 8
ATT_HD = 128
ATT_LEFT = 8
ATT_BAND = (ATT_LEFT + 1) * CHUNK
MAX_REL = 128
N_REL = 2 * MAX_REL + 1
SSM_G = 64
SSM_H = 16
SSM_P = 64
FFN_H = 5632
IN_W = 6 * MIX_W + N_BRANCH * D_MODEL
NORM_EPS = 1e-6
MASK_VALUE = -1e30
ADAM_LR, ADAM_B1, ADAM_B2, ADAM_EPS, ADAM_WD, ADAM_STEP = 0.001, 0.9, 0.999, 1e-08, 0.01, 10

S5_L = 16
S5_LW = S5_L * SSM_H
N_CHIPS = 4
V7X_VMEM_LIMIT = 56 * 1024 * 1024
HI = lax.Precision.HIGHEST
MESH = pl.DeviceIdType.MESH


def _cparams(sem=None):
    return pltpu.CompilerParams(dimension_semantics=sem, vmem_limit_bytes=V7X_VMEM_LIMIT)


def _pick(n, prefs):
    for p in prefs:
        if n % p == 0:
            return p
    return n


_MM_VMEM_BUDGET = 46 * 1024 * 1024


def _mm_tiles(M, N, K, has_res, out_bytes):
    tn = _pick(N, (1024, 1408, 512, 256, 128))
    for tk in (K, 2048, 1408, 1024, 512, 256, 128):
        if K % tk:
            continue
        for tm in (1024, 512, 256, 128, 64, 32, 16, 8):
            if M % tm:
                continue
            need = 2 * 2 * (tm * tk + tk * tn) + 2 * tm * tn * out_bytes
            need += tm * tn * 4 if tk < K else 0
            need += 2 * tm * tn * 4 if has_res else 0
            if need <= _MM_VMEM_BUDGET:
                return tm, tn, tk
    raise ValueError((M, N, K))


def _call(body, *, name, grid, in_specs, out_specs, out_shape, args, scratch_shapes=(), sem=None, side=None):
    in_specs, out_specs, out_shape = list(in_specs), list(out_specs), list(out_shape)
    scratch_shapes = list(scratch_shapes)
    if side is None:
        outs = pl.pallas_call(body, name=name, out_shape=tuple(out_shape), grid=grid, in_specs=in_specs,
                              out_specs=tuple(out_specs), scratch_shapes=scratch_shapes, compiler_params=_cparams(sem))(*args)
        return tuple(outs), ()
    n_in, n_out, n_scr = len(in_specs), len(out_shape), len(scratch_shapes)
    s_in, s_out = len(side.inputs), len(side.out_shapes)

    def wrapped(*refs):
        mi, refs = refs[:n_in], refs[n_in:]
        si, refs = refs[:s_in], refs[s_in:]
        mo, refs = refs[:n_out], refs[n_out:]
        so, refs = refs[:s_out], refs[s_out:]
        scr, (send, recv) = refs[:n_scr], refs[n_scr:]
        first = functools.reduce(jnp.logical_and, [pl.program_id(d) == 0 for d in range(len(grid))])
        last = functools.reduce(jnp.logical_and, [pl.program_id(d) == g - 1 for d, g in enumerate(grid)])

        @pl.when(first)
        def _():
            side.start(si, so, send, recv)

        body(*mi, *mo, *scr)

        @pl.when(last)
        def _():
            side.finish(si, so, send, recv)

    outs = pl.pallas_call(
        wrapped, name=name, out_shape=tuple(out_shape + list(side.out_shapes)), grid=grid,
        in_specs=in_specs + [_ANY] * s_in, out_specs=tuple(out_specs + [_ANY] * s_out),
        scratch_shapes=scratch_shapes + [pltpu.SemaphoreType.DMA((side.n_sems,)), pltpu.SemaphoreType.DMA((side.n_sems,))],
        input_output_aliases={n_in + i: n_out + o for i, o in side.aliases.items()},
        compiler_params=_cparams(("arbitrary",) * len(grid)),
    )(*args, *side.inputs)
    return tuple(outs[:n_out]), tuple(outs[n_out:])


def _mm(a, b, *, M, N, K, name, ta=False, tb=False, a_lead=None, b_lead=None, a_off=(0, 0), b_off=(0, 0),
        out_dtype=F32, res=None, tm=None, tn=None, tk=None, side=None):
    if tm is None and tn is None and tk is None:
        tm, tn, tk = _mm_tiles(M, N, K, res is not None, jnp.dtype(out_dtype).itemsize)
    nk = K // tk

    def spec(blk, lead, off, order):
        r0, c0 = off[0] // blk[0], off[1] // blk[1]
        assert off[0] % blk[0] == 0 and off[1] % blk[1] == 0
        if lead is None:
            return pl.BlockSpec(blk, lambda i, j, k: (r0 + order(i, j, k)[0], c0 + order(i, j, k)[1]))
        return pl.BlockSpec((None,) + blk, lambda i, j, k: (lead, r0 + order(i, j, k)[0], c0 + order(i, j, k)[1]))

    a_spec = spec((tk, tm), a_lead, a_off, lambda i, j, k: (k, i)) if ta else spec((tm, tk), a_lead, a_off, lambda i, j, k: (i, k))
    b_spec = spec((tn, tk), b_lead, b_off, lambda i, j, k: (j, k)) if tb else spec((tk, tn), b_lead, b_off, lambda i, j, k: (k, j))
    dims = (((0 if ta else 1,), (1 if tb else 0,)), ((), ()))
    in_specs = [a_spec, b_spec]
    args = [a, b]
    if res is not None:
        in_specs.append(pl.BlockSpec((tm, tn), lambda i, j, k: (i, j)))
        args.append(res)

    def body(*refs):
        a_ref, b_ref = refs[:2]
        r_ref = refs[2] if res is not None else None
        o_ref = refs[3] if res is not None else refs[2]

        def dot():
            return lax.dot_general(a_ref[...], b_ref[...], dims, preferred_element_type=F32)

        def finish(r):
            if r_ref is not None:
                r = r + r_ref[...]
            o_ref[...] = r.astype(out_dtype)

        if nk == 1:
            finish(dot())
            return
        acc = refs[-1]
        k = pl.program_id(2)

        @pl.when(k == 0)
        def _():
            acc[...] = dot()

        @pl.when(jnp.logical_and(k > 0, k < nk - 1))
        def _():
            acc[...] += dot()

        @pl.when(k == nk - 1)
        def _():
            finish(acc[...] + dot())

    (out,), got = _call(
        body, name=name, out_shape=[jax.ShapeDtypeStruct((M, N), out_dtype)],
        grid=(M // tm, N // tn, nk), in_specs=in_specs, out_specs=[pl.BlockSpec((tm, tn), lambda i, j, k: (i, j))],
        scratch_shapes=[pltpu.VMEM((tm, tn), F32)] if nk > 1 else [],
        sem=("parallel", "parallel", "arbitrary"), args=args, side=side)
    return out if side is None else (out, got)


def _rowwise(fn, rows, consts, out_rows, out_accs, *, R, name, tm=256, rs=16, cs=None):
    tm = min(tm, R)
    assert R % tm == 0 and tm % rs == 0
    n_r, n_c, n_o, n_a = len(rows), len(consts), len(out_rows), len(out_accs)
    nsteps = R // tm
    widths = [w for _, _, w in rows]
    if cs is not None:
        assert all(w == widths[0] for w in widths) and widths[0] % cs == 0
        col_chunks = [(c0, cs) for c0 in range(0, widths[0], cs)]
    else:
        col_chunks = [None]

    def body(*refs):
        r_refs = refs[:n_r]
        c_refs = refs[n_r:n_r + n_c]
        o_refs = refs[n_r + n_c:n_r + n_c + n_o]
        a_refs = refs[n_r + n_c + n_o:n_r + n_c + n_o + n_a]
        s_refs = refs[n_r + n_c + n_o + n_a:]
        i = pl.program_id(0)

        @pl.when(i == 0)
        def _():
            for s in s_refs:
                s[...] = jnp.zeros_like(s)

        def piece(g, carry):
            r0 = pl.multiple_of(g * rs, rs)
            for cc in col_chunks:
                csl = slice(None) if cc is None else slice(cc[0], cc[0] + cc[1])
                rp = [r[pl.ds(r0, rs), csl].astype(F32) for r in r_refs]
                cp = [c[:, csl] for c in c_refs]
                outs, accs = fn(rp, cp)
                for o_ref, o in zip(o_refs, outs):
                    o_ref[pl.ds(r0, rs), csl] = o.astype(o_ref.dtype)
                for s_ref, av in zip(s_refs, accs):
                    s_ref[:, csl] += av
            return carry

        lax.fori_loop(0, tm // rs, piece, 0)

        @pl.when(i == nsteps - 1)
        def _():
            for a_ref, s_ref in zip(a_refs, s_refs):
                a_ref[...] = jnp.sum(s_ref[...], axis=0, keepdims=True)

    in_specs = [pl.BlockSpec((tm, w), functools.partial(lambda i, cb: (i, cb), cb=off // w)) for _, off, w in rows]
    for _, off, w in rows:
        assert off % w == 0
    in_specs += [pl.BlockSpec(c.shape, lambda i: (0, 0)) for c in consts]
    out_specs = [pl.BlockSpec((tm, w), lambda i: (i, 0)) for w, _ in out_rows]
    out_specs += [pl.BlockSpec((1, w), lambda i: (0, 0)) for w in out_accs]
    out_shape = [jax.ShapeDtypeStruct((R, w), dt) for w, dt in out_rows]
    out_shape += [jax.ShapeDtypeStruct((1, w), F32) for w in out_accs]
    return pl.pallas_call(
        body, name=name, out_shape=tuple(out_shape), grid=(nsteps,), in_specs=in_specs, out_specs=tuple(out_specs),
        scratch_shapes=[pltpu.VMEM((rs, w), F32) for w in out_accs],
        compiler_params=_cparams(("arbitrary",)),
    )(*[r for r, _, _ in rows], *consts)


def _rms(x, g):
    r = lax.rsqrt(jnp.mean(x * x, axis=-1, keepdims=True) + NORM_EPS)
    return x * r * g


def _rms_fwd(x, g, name):
    R = x.shape[0]

    def fn(rp, cp):
        return [_rms(rp[0], cp[0])], []

    return _rowwise(fn, [(x, 0, D_MODEL)], [g], [(D_MODEL, BF16)], [], R=R, name=name)[0]


def _rms_bwd(x, g, dh, dres, name):
    R = x.shape[0]

    def fn(rp, cp):
        xv, dhv, drv = rp
        _, pull = jax.vjp(_rms, xv, jnp.broadcast_to(cp[0], xv.shape))
        dx, dgv = pull(dhv)
        return [drv + dx], [dgv]

    dx, dg = _rowwise(fn, [(x, 0, D_MODEL), (dh, 0, D_MODEL), (dres, 0, D_MODEL)], [g], [(D_MODEL, F32)], [D_MODEL],
                      R=R, name=name, rs=8)
    return dx, dg


def _final_loss(x, g, tgt, name):
    R = x.shape[0]

    def loss_rows(xv, gv, tv):
        e = _rms(xv, gv) - tv
        return 0.5 * jnp.mean(e * e, axis=-1, keepdims=True)

    def fn(rp, cp):
        xv, tv = rp
        lr, pull = jax.vjp(lambda a, b: loss_rows(a, b, tv), xv, jnp.broadcast_to(cp[0], xv.shape))
        dx, dgv = pull(jnp.ones_like(lr))
        return [dx], [dgv, jnp.broadcast_to(lr, (lr.shape[0], 128))]

    dx, dg, lsum = _rowwise(fn, [(x, 0, D_MODEL), (tgt, 0, D_MODEL)], [g], [(D_MODEL, F32)], [D_MODEL, 128],
                            R=R, name=name, rs=8)
    return lsum[0, 0], dx, dg


def _neg_expm1(z):
    u = jnp.exp(z)
    safe = jnp.where(u == 1.0, 0.5, u)
    return -jnp.where(u == 1.0, z, (safe - 1.0) * z / jnp.log(safe))


def _lru_ab(xc, pr, pi, lam):
    r = jax.nn.sigmoid(pr)
    i = jax.nn.sigmoid(pi)
    log_a = -LRU_C * r * jax.nn.softplus(-lam)
    a = jnp.exp(log_a)
    b = jnp.sqrt(_neg_expm1(2.0 * log_a)) * (i * xc)
    return a, b


def _gated(h, gate):
    return h * jax.nn.gelu(gate)


def _row_iota8(w):
    return lax.broadcasted_iota(jnp.int32, (8, w), 0)


def _shift_dn(x, halo, s):
    xs = pltpu.roll(x, s, 0)
    hs = pltpu.roll(halo, s, 0)
    first = jnp.where(_row_iota8(x.shape[1]) < s, hs, xs[0:8])
    return jnp.concatenate([first, xs[8:]], axis=0) if x.shape[0] > 8 else first


def _shift_up(x, nxt, s):
    n = x.shape[0]
    xs = pltpu.roll(x, n - s, 0)
    ns = pltpu.roll(nxt, 8 - s, 0)
    last = jnp.where(_row_iota8(x.shape[1]) >= 8 - s, ns, xs[n - 8:])
    return jnp.concatenate([xs[:n - 8], last], axis=0) if n > 8 else last


def _lru_tiles(T):
    tT = min(256, T)
    return tT, T // tT


def _lru_fwd(proj, cw, cb, wax, bax, lam, name):
    T = proj.shape[0]
    W = MIX_W
    tT, nT = _lru_tiles(T)

    def body(x_ref, xh_ref, gt_ref, cw_ref, cb_ref, wax_ref, bax_ref, lam_ref, y_ref, h_ref, a_s, b_s, hc_s):
        i = pl.program_id(0)

        @pl.when(i == 0)
        def _():
            hc_s[...] = jnp.zeros_like(hc_s)

        x = x_ref[...]
        halo = jnp.where(i > 0, xh_ref[...], 0.0)
        w = cw_ref[...]
        xc = (cb_ref[...] + w[3:4] * x + w[2:3] * _shift_dn(x, halo, 1) + w[1:2] * _shift_dn(x, halo, 2)
              + w[0:1] * _shift_dn(x, halo, 3))
        pre = jnp.dot(xc.astype(BF16), wax_ref[...], preferred_element_type=F32) + bax_ref[...]
        a, b = _lru_ab(xc, pre[:, :W], pre[:, W:], lam_ref[...])
        a_s[...] = a
        b_s[...] = b
        row = _row_iota8(W)

        def grp(gi, hprev):
            r0 = pl.multiple_of(gi * 8, 8)
            A = a_s[pl.ds(r0, 8), :]
            B = b_s[pl.ds(r0, 8), :]
            for s in (1, 2, 4):
                As = pltpu.roll(A, s, 0)
                Bs = pltpu.roll(B, s, 0)
                m = row >= s
                B = jnp.where(m, A * Bs + B, B)
                A = jnp.where(m, A * As, A)
            H = A * hprev + B
            h_ref[pl.ds(r0, 8), :] = H
            return H[7:8, :]

        hc_s[0:1, :] = lax.fori_loop(0, tT // 8, grp, hc_s[0:1, :])
        y_ref[...] = _gated(h_ref[...], gt_ref[...]).astype(BF16)

    hb = tT // 8
    return pl.pallas_call(
        body, name=name,
        out_shape=(jax.ShapeDtypeStruct((T, W), BF16), jax.ShapeDtypeStruct((T, W), F32)),
        grid=(nT,),
        in_specs=[pl.BlockSpec((tT, W), lambda i: (i, 0)),
                  pl.BlockSpec((8, W), lambda i: (jnp.maximum(i * hb - 1, 0), 0)),
                  pl.BlockSpec((tT, W), lambda i: (i, 1)),
                  pl.BlockSpec((CONV_W, W), lambda i: (0, 0)), pl.BlockSpec((1, W), lambda i: (0, 0)),
                  pl.BlockSpec((W, 2 * W), lambda i: (0, 0)), pl.BlockSpec((1, 2 * W), lambda i: (0, 0)),
                  pl.BlockSpec((1, W), lambda i: (0, 0))],
        out_specs=(pl.BlockSpec((tT, W), lambda i: (i, 0)), pl.BlockSpec((tT, W), lambda i: (i, 0))),
        scratch_shapes=[pltpu.VMEM((tT, W), F32), pltpu.VMEM((tT, W), F32), pltpu.VMEM((8, W), F32)],
        compiler_params=_cparams(("arbitrary",)),
    )(proj, proj, proj, cw, cb, wax, bax, lam)


def _lru_bwd(proj, h, dy, cw, cb, wax, bax, lam, name):
    T = proj.shape[0]
    W = MIX_W
    tT, nT = _lru_tiles(T)
    hb = tT // 8

    def body(x_ref, xh_ref, gt_ref, h_ref, hh_ref, dy_ref, cw_ref, cb_ref, wax_ref, bax_ref, lam_ref,
             dx_ref, dgt_ref, dcw_ref, dcb_ref, dwax_ref, dbax_ref, dlam_ref,
             al_s, be_s, d_s, ca_s, cd_s, cx_s):
        i = pl.program_id(0)
        ib = nT - 1 - i

        @pl.when(i == 0)
        def _():
            for r in (ca_s, cd_s, cx_s, dcw_ref, dcb_ref, dwax_ref, dbax_ref, dlam_ref):
                r[...] = jnp.zeros_like(r)

        x = x_ref[...]
        halo = jnp.where(ib > 0, xh_ref[...], 0.0)
        w = cw_ref[...]
        x1, x2, x3 = _shift_dn(x, halo, 1), _shift_dn(x, halo, 2), _shift_dn(x, halo, 3)
        xc = cb_ref[...] + w[3:4] * x + w[2:3] * x1 + w[1:2] * x2 + w[0:1] * x3
        xcb = xc.astype(BF16)
        pre = jnp.dot(xcb, wax_ref[...], preferred_element_type=F32) + bax_ref[...]
        (a, _), pull_ab = jax.vjp(_lru_ab, xc, pre[:, :W], pre[:, W:], lam_ref[...])
        hv = h_ref[...]
        hprev = _shift_dn(hv, jnp.where(ib > 0, hh_ref[...], 0.0), 1)
        _, pull_y = jax.vjp(_gated, hv, gt_ref[...])
        dh_out, dgt = pull_y(dy_ref[...])
        dgt_ref[...] = dgt.astype(BF16)
        al_s[...] = _shift_up(a, ca_s[...], 1)
        be_s[...] = dh_out
        row = _row_iota8(W)
        ng = tT // 8

        def grp(k, dnext):
            r0 = pl.multiple_of((ng - 1 - k) * 8, 8)
            A = al_s[pl.ds(r0, 8), :]
            B = be_s[pl.ds(r0, 8), :]
            for s in (1, 2, 4):
                As = pltpu.roll(A, 8 - s, 0)
                Bs = pltpu.roll(B, 8 - s, 0)
                m = row < 8 - s
                B = jnp.where(m, A * Bs + B, B)
                A = jnp.where(m, A * As, A)
            Dg = A * dnext + B
            d_s[pl.ds(r0, 8), :] = Dg
            return Dg[0:1, :]

        lax.fori_loop(0, ng, grp, cd_s[0:1, :])
        Dv = d_s[...]
        dxc1, dpr, dpi, dlam = pull_ab((Dv * hprev, Dv))
        dpre = jnp.concatenate([dpr, dpi], axis=1)
        dpb = dpre.astype(BF16)
        dxc = dxc1 + lax.dot_general(dpb, wax_ref[...], (((1,), (1,)), ((), ())), preferred_element_type=F32)
        dwax_ref[...] += lax.dot_general(xcb, dpb, (((0,), (0,)), ((), ())), preferred_element_type=F32)
        dbax_ref[...] += jnp.sum(dpre, axis=0, keepdims=True)
        dlam_ref[...] += dlam
        dcb_ref[...] += jnp.sum(dxc, axis=0, keepdims=True)
        dcw_ref[...] += jnp.concatenate([jnp.sum(dxc * x3, axis=0, keepdims=True), jnp.sum(dxc * x2, axis=0, keepdims=True),
                                         jnp.sum(dxc * x1, axis=0, keepdims=True), jnp.sum(dxc * x, axis=0, keepdims=True)], axis=0)
        nxt = cx_s[...]
        dx = (w[3:4] * dxc + w[2:3] * _shift_up(dxc, nxt, 1) + w[1:2] * _shift_up(dxc, nxt, 2)
              + w[0:1] * _shift_up(dxc, nxt, 3))
        dx_ref[...] = dx.astype(BF16)
        ca_s[...] = a[0:8]
        cd_s[...] = Dv[0:8]
        cx_s[...] = dxc[0:8]

    rev = lambda i: nT - 1 - i
    const = lambda shape: pl.BlockSpec(shape, lambda i: (0, 0))
    return pl.pallas_call(
        body, name=name,
        out_shape=(jax.ShapeDtypeStruct((T, W), BF16), jax.ShapeDtypeStruct((T, W), BF16),
                   jax.ShapeDtypeStruct((CONV_W, W), F32), jax.ShapeDtypeStruct((1, W), F32),
                   jax.ShapeDtypeStruct((W, 2 * W), F32), jax.ShapeDtypeStruct((1, 2 * W), F32),
                   jax.ShapeDtypeStruct((1, W), F32)),
        grid=(nT,),
        in_specs=[pl.BlockSpec((tT, W), lambda i: (rev(i), 0)),
                  pl.BlockSpec((8, W), lambda i: (jnp.maximum(rev(i) * hb - 1, 0), 0)),
                  pl.BlockSpec((tT, W), lambda i: (rev(i), 1)),
                  pl.BlockSpec((tT, W), lambda i: (rev(i), 0)),
                  pl.BlockSpec((8, W), lambda i: (jnp.maximum(rev(i) * hb - 1, 0), 0)),
                  pl.BlockSpec((tT, W), lambda i: (rev(i), 0)),
                  const((CONV_W, W)), const((1, W)), const((W, 2 * W)), const((1, 2 * W)), const((1, W))],
        out_specs=(pl.BlockSpec((tT, W), lambda i: (rev(i), 0)), pl.BlockSpec((tT, W), lambda i: (rev(i), 0)),
                   const((CONV_W, W)), const((1, W)), const((W, 2 * W)), const((1, 2 * W)), const((1, W))),
        scratch_shapes=[pltpu.VMEM((tT, W), F32), pltpu.VMEM((tT, W), F32), pltpu.VMEM((tT, W), F32),
                        pltpu.VMEM((8, W), F32), pltpu.VMEM((8, W), F32), pltpu.VMEM((8, W), F32)],
        compiler_params=_cparams(("arbitrary",)),
    )(proj, proj, proj, h, h, dy, cw, cb, wax, bax, lam)


def _lru_prep(wa, wx, ba, bx):
    eye = jnp.eye(LRU_BLOCKS, dtype=F32)

    def dense(wb):
        return (wb[:, :, None, :] * eye[:, None, :, None]).reshape(MIX_W, MIX_W)

    wax = jnp.concatenate([dense(wa), dense(wx)], axis=1)
    bax = jnp.concatenate([ba, bx])[None, :]
    return wax, bax


_ATT_QC = 4
_ATT_Q = _ATT_QC * CHUNK
_ATT_KW = (ATT_LEFT + _ATT_QC) * CHUNK
_BVEC_W = _ATT_KW
_N_OFFS = CHUNK - 1 + ATT_BAND


def _attn_bias(rel_bias):
    n_far = ATT_LEFT * CHUNK - MAX_REL + CHUNK
    far = jnp.broadcast_to(rel_bias[:, 2 * MAX_REL:], (ATT_HEADS, n_far))
    near = rel_bias[:, MAX_REL - (CHUNK - 1):2 * MAX_REL][:, ::-1]
    pad = jnp.zeros((ATT_HEADS, _BVEC_W - _N_OFFS), F32)
    return jnp.concatenate([far, near, pad], axis=1)[:, None, :]


def _bias_table(bvec_row):
    return pltpu.roll(jnp.broadcast_to(bvec_row, (_ATT_Q, _BVEC_W)), _BVEC_W - (CHUNK - 1), 1, stride=1, stride_axis=0)


def _bias_table_t(ds):
    r = lax.broadcasted_iota(jnp.int32, (_ATT_Q, _ATT_Q), 0)
    c = lax.broadcasted_iota(jnp.int32, (_ATT_Q, _ATT_Q), 1)
    rev = jnp.dot((r + c == _ATT_Q - 1).astype(F32), ds, preferred_element_type=F32, precision=HI)
    back = pltpu.roll(rev, _BVEC_W - (_ATT_Q - CHUNK), 1, stride=1, stride_axis=0)
    return jnp.sum(back, axis=0, keepdims=True)


_NT = (((1,), (1,)), ((), ()))
_TN = (((0,), (0,)), ((), ()))
_ATT_PAD = ATT_LEFT * CHUNK
_Q_BLK, _K_BLK, _V_BLK = 2 * MIX_W // ATT_HD, 3 * MIX_W // ATT_HD, 4 * MIX_W // ATT_HD


def _in_band():
    first = (lax.broadcasted_iota(jnp.int32, (_ATT_Q, _ATT_KW), 0) // CHUNK) * CHUNK
    k = lax.broadcasted_iota(jnp.int32, (_ATT_Q, _ATT_KW), 1)
    return jnp.logical_and(k >= first, k < first + ATT_BAND)


def _attn_probs(q, kb, bias, in_band, b):
    s = lax.dot_general(q, kb, _NT, preferred_element_type=F32) * (ATT_HD ** -0.5) + bias
    kpos = lax.broadcasted_iota(jnp.int32, s.shape, 1)
    s = jnp.where(jnp.logical_and(in_band, kpos >= _ATT_PAD - b * _ATT_Q), s, MASK_VALUE)
    e = jnp.exp(s - jnp.max(s, axis=-1, keepdims=True))
    return e / jnp.sum(e, axis=-1, keepdims=True)


def _attn_fwd(proj, bias, name, side=None):
    T = proj.shape[0]
    assert T % _ATT_Q == 0
    nB = T // _ATT_Q

    def body(q_ref, k_ref, v_ref, b_ref, o_ref, kp, vp):
        kp[0:_ATT_PAD, :] = jnp.zeros((_ATT_PAD, ATT_HD), BF16)
        vp[0:_ATT_PAD, :] = jnp.zeros((_ATT_PAD, ATT_HD), BF16)
        kp[_ATT_PAD:, :] = k_ref[...].astype(BF16)
        vp[_ATT_PAD:, :] = v_ref[...].astype(BF16)
        bias_v = _bias_table(b_ref[0])
        band = _in_band()

        def step(b, carry):
            r0 = pl.multiple_of(b * _ATT_Q, _ATT_Q)
            q = q_ref[pl.ds(r0, _ATT_Q), :].astype(BF16)
            p = _attn_probs(q, kp[pl.ds(r0, _ATT_KW), :], bias_v, band, b)
            o = jnp.dot(p.astype(BF16), vp[pl.ds(r0, _ATT_KW), :], preferred_element_type=F32)
            o_ref[pl.ds(r0, _ATT_Q), :] = o.astype(BF16)
            return carry

        lax.fori_loop(0, nB, step, 0)

    (out,), got = _call(
        body, name=name, out_shape=[jax.ShapeDtypeStruct((T, MIX_W), BF16)], grid=(ATT_HEADS,),
        in_specs=[pl.BlockSpec((T, ATT_HD), lambda h: (0, _Q_BLK + h)), pl.BlockSpec((T, ATT_HD), lambda h: (0, _K_BLK + h)),
                  pl.BlockSpec((T, ATT_HD), lambda h: (0, _V_BLK + h)), pl.BlockSpec((1, 1, _BVEC_W), lambda h: (h, 0, 0))],
        out_specs=[pl.BlockSpec((T, ATT_HD), lambda h: (0, h))],
        scratch_shapes=[pltpu.VMEM((T + _ATT_PAD, ATT_HD), BF16), pltpu.VMEM((T + _ATT_PAD, ATT_HD), BF16)],
        sem=("arbitrary",), args=(proj, proj, proj, bias), side=side)
    return out if side is None else (out, got)


def _attn_bwd(proj, bias, do, name, side=None):
    T = proj.shape[0]
    assert T % _ATT_Q == 0
    nB = T // _ATT_Q

    def body(q_ref, k_ref, v_ref, b_ref, do_ref, dq_ref, dk_ref, dv_ref, db_ref, kp, vp, dkp, dvp, dbs):
        kp[0:_ATT_PAD, :] = jnp.zeros((_ATT_PAD, ATT_HD), BF16)
        vp[0:_ATT_PAD, :] = jnp.zeros((_ATT_PAD, ATT_HD), BF16)
        kp[_ATT_PAD:, :] = k_ref[...].astype(BF16)
        vp[_ATT_PAD:, :] = v_ref[...].astype(BF16)
        dkp[...] = jnp.zeros_like(dkp)
        dvp[...] = jnp.zeros_like(dvp)
        dbs[...] = jnp.zeros_like(dbs)
        bias_v = _bias_table(b_ref[0])
        band = _in_band()

        def step(b, carry):
            r0 = pl.multiple_of(b * _ATT_Q, _ATT_Q)
            q = q_ref[pl.ds(r0, _ATT_Q), :].astype(BF16)
            kb = kp[pl.ds(r0, _ATT_KW), :]
            vb = vp[pl.ds(r0, _ATT_KW), :]
            dob = do_ref[pl.ds(r0, _ATT_Q), :].astype(BF16)
            p = _attn_probs(q, kb, bias_v, band, b)
            dp = lax.dot_general(dob, vb, _NT, preferred_element_type=F32)
            ds = p * (dp - jnp.sum(p * dp, axis=-1, keepdims=True))
            dbs[...] += ds
            dsb = (ds * (ATT_HD ** -0.5)).astype(BF16)
            dq_ref[pl.ds(r0, _ATT_Q), :] = jnp.dot(dsb, kb, preferred_element_type=F32).astype(BF16)
            dkp[pl.ds(r0, _ATT_KW), :] += lax.dot_general(dsb, q, _TN, preferred_element_type=F32)
            dvp[pl.ds(r0, _ATT_KW), :] += lax.dot_general(p.astype(BF16), dob, _TN, preferred_element_type=F32)
            return carry

        lax.fori_loop(0, nB, step, 0)
        dk_ref[...] = dkp[_ATT_PAD:, :].astype(BF16)
        dv_ref[...] = dvp[_ATT_PAD:, :].astype(BF16)
        db_ref[0] = _bias_table_t(dbs[...])

    hspec = pl.BlockSpec((T, ATT_HD), lambda h: (0, h))
    osd = jax.ShapeDtypeStruct((T, MIX_W), BF16)
    outs, got = _call(
        body, name=name,
        out_shape=[osd, osd, osd, jax.ShapeDtypeStruct((ATT_HEADS, 1, _BVEC_W), F32)], grid=(ATT_HEADS,),
        in_specs=[pl.BlockSpec((T, ATT_HD), lambda h: (0, _Q_BLK + h)), pl.BlockSpec((T, ATT_HD), lambda h: (0, _K_BLK + h)),
                  pl.BlockSpec((T, ATT_HD), lambda h: (0, _V_BLK + h)), pl.BlockSpec((1, 1, _BVEC_W), lambda h: (h, 0, 0)),
                  hspec],
        out_specs=[hspec, hspec, hspec, pl.BlockSpec((1, 1, _BVEC_W), lambda h: (h, 0, 0))],
        scratch_shapes=[pltpu.VMEM((T + _ATT_PAD, ATT_HD), BF16), pltpu.VMEM((T + _ATT_PAD, ATT_HD), BF16),
                        pltpu.VMEM((T + _ATT_PAD, ATT_HD), F32), pltpu.VMEM((T + _ATT_PAD, ATT_HD), F32),
                        pltpu.VMEM((_ATT_Q, _ATT_KW), F32)],
        sem=("arbitrary",), args=(proj, proj, proj, bias, do), side=side)
    return outs if side is None else (outs, got)


def _s5_prep(a_re, a_im, b_re, b_im, c_re, c_im, d, log_step):
    step = jnp.exp(log_step)[:, None]
    mag = jnp.exp(a_re * step)
    ang = a_im * step
    lb_re = mag * jnp.cos(ang)
    lb_im = mag * jnp.sin(ang)
    den = a_re * a_re + a_im * a_im
    nr = lb_re - 1.0
    coef_re = (nr * a_re + lb_im * a_im) / den
    coef_im = (lb_im * a_re - nr * a_im) / den
    bb_re = coef_re[..., None] * b_re - coef_im[..., None] * b_im
    bb_im = coef_re[..., None] * b_im + coef_im[..., None] * b_re
    prs, pis = [jnp.ones_like(lb_re)], [jnp.zeros_like(lb_re)]
    for _ in range(S5_L):
        prs.append(prs[-1] * lb_re - pis[-1] * lb_im)
        pis.append(prs[-2] * lb_im + pis[-1] * lb_re)
    PR, PI = jnp.stack(prs), jnp.stack(pis)
    cl_re = c_re[None] * PR[:, :, None, :] - c_im[None] * PI[:, :, None, :]
    cl_im = c_re[None] * PI[:, :, None, :] + c_im[None] * PR[:, :, None, :]
    kt = (jnp.einsum("tghp,gpk->tghk", cl_re[:S5_L], bb_re, precision=HI)
          - jnp.einsum("tghp,gpk->tghk", cl_im[:S5_L], bb_im, precision=HI))
    steps = np.arange(S5_L)
    lag = (steps[:, None, None] == steps[None, None, :] - steps[None, :, None]).astype(np.float32)
    tgt = jnp.einsum("tpl,tghk->gpklh", lag, kt, precision=HI).reshape(SSM_G, S5_LW, S5_LW)
    prr, pir = PR[:S5_L][::-1], PI[:S5_L][::-1]
    bret = (prr[:, :, None, :] * bb_re.transpose(0, 2, 1)[None] - pir[:, :, None, :] * bb_im.transpose(0, 2, 1)[None])
    bimt = (prr[:, :, None, :] * bb_im.transpose(0, 2, 1)[None] + pir[:, :, None, :] * bb_re.transpose(0, 2, 1)[None])
    bret = bret.transpose(1, 0, 2, 3).reshape(SSM_G, S5_LW, SSM_P)
    bimt = bimt.transpose(1, 0, 2, 3).reshape(SSM_G, S5_LW, SSM_P)
    cre = cl_re[1:].transpose(1, 3, 0, 2).reshape(SSM_G, SSM_P, S5_LW)
    cim = (-cl_im[1:]).transpose(1, 3, 0, 2).reshape(SSM_G, SSM_P, S5_LW)
    dflat = jnp.broadcast_to(d.reshape(SSM_G, 1, SSM_H), (SSM_G, S5_L, SSM_H)).reshape(SSM_G, 1, S5_LW)
    return tgt, bret, bimt, cre, cim, PR[S5_L], PI[S5_L], dflat


_S5_GB = 8


def _bdot(a, b, dims):
    return lax.dot_general(a, b, dims, preferred_element_type=F32, precision=HI)


_B_NN = (((2,), (1,)), ((0,), (0,)))
_B_NT = (((2,), (2,)), ((0,), (0,)))
_B_TN = (((1,), (1,)), ((0,), (0,)))


def _gspec(shape):
    return pl.BlockSpec((_S5_GB,) + shape, lambda g: (g, 0, 0))


def _s5_in(u, bret, bimt, name):
    C = u.shape[1]

    def body(u_ref, br_ref, bi_ref, sr_ref, si_ref):
        uv = u_ref[...]
        sr_ref[...] = _bdot(uv, br_ref[...], _B_NN)
        si_ref[...] = _bdot(uv, bi_ref[...], _B_NN)

    sd = jax.ShapeDtypeStruct((SSM_G, C, SSM_P), F32)
    return pl.pallas_call(
        body, name=name, out_shape=(sd, sd), grid=(SSM_G // _S5_GB,),
        in_specs=[_gspec((C, S5_LW)), _gspec((S5_LW, SSM_P)), _gspec((S5_LW, SSM_P))],
        out_specs=(_gspec((C, SSM_P)), _gspec((C, SSM_P))), compiler_params=_cparams(("parallel",)),
    )(u, bret, bimt)


def _s5_scan(sin_re, sin_im, lr, li, name):
    C = sin_re.shape[0]

    def body(ir_ref, ii_ref, lr_ref, li_ref, or_ref, oi_ref):
        lrv, liv = lr_ref[...], li_ref[...]

        def step(c, s):
            sr, si = s
            or_ref[c] = sr
            oi_ref[c] = si
            return lrv * sr - liv * si + ir_ref[c], lrv * si + liv * sr + ii_ref[c]

        z = jnp.zeros((SSM_G, SSM_P), F32)
        lax.fori_loop(0, C, step, (z, z))

    sd = jax.ShapeDtypeStruct((C, SSM_G, SSM_P), F32)
    return pl.pallas_call(body, name=name, out_shape=(sd, sd), compiler_params=_cparams())(sin_re, sin_im, lr, li)


def _s5_out(u, sp_re, sp_im, tgt, cre, cim, dflat, name):
    C = u.shape[1]

    def body(u_ref, sr_ref, si_ref, t_ref, cr_ref, ci_ref, d_ref, pre_ref, y_ref):
        uv = u_ref[...]
        pre = (_bdot(uv, t_ref[...], _B_NN) + _bdot(sr_ref[...], cr_ref[...], _B_NN)
               + _bdot(si_ref[...], ci_ref[...], _B_NN) + d_ref[...] * uv)
        pre_ref[...] = pre
        y_ref[...] = jax.nn.gelu(pre).astype(BF16)

    return pl.pallas_call(
        body, name=name,
        out_shape=(jax.ShapeDtypeStruct((SSM_G, C, S5_LW), F32), jax.ShapeDtypeStruct((SSM_G, C, S5_LW), BF16)),
        grid=(SSM_G // _S5_GB,),
        in_specs=[_gspec((C, S5_LW)), _gspec((C, SSM_P)), _gspec((C, SSM_P)), _gspec((S5_LW, S5_LW)),
                  _gspec((SSM_P, S5_LW)), _gspec((SSM_P, S5_LW)), _gspec((1, S5_LW))],
        out_specs=(_gspec((C, S5_LW)), _gspec((C, S5_LW))), compiler_params=_cparams(("parallel",)),
    )(u, sp_re, sp_im, tgt, cre, cim, dflat)


def _s5_bwd_out(dy, pre, u, sp_re, sp_im, cre, cim, name):
    C = u.shape[1]

    def body(dy_ref, pre_ref, u_ref, sr_ref, si_ref, cr_ref, ci_ref,
             dpre_ref, dsr_ref, dsi_ref, dt_ref, dcr_ref, dci_ref, dd_ref):
        _, pull = jax.vjp(jax.nn.gelu, pre_ref[...])
        dpre = pull(dy_ref[...])[0]
        uv = u_ref[...]
        dpre_ref[...] = dpre
        dsr_ref[...] = _bdot(dpre, cr_ref[...], _B_NT)
        dsi_ref[...] = _bdot(dpre, ci_ref[...], _B_NT)
        dt_ref[...] = _bdot(uv, dpre, _B_TN)
        dcr_ref[...] = _bdot(sr_ref[...], dpre, _B_TN)
        dci_ref[...] = _bdot(si_ref[...], dpre, _B_TN)
        dd_ref[...] = jnp.sum(dpre * uv, axis=1, keepdims=True)

    sd = jax.ShapeDtypeStruct
    return pl.pallas_call(
        body, name=name,
        out_shape=(sd((SSM_G, C, S5_LW), F32), sd((SSM_G, C, SSM_P), F32), sd((SSM_G, C, SSM_P), F32),
                   sd((SSM_G, S5_LW, S5_LW), F32), sd((SSM_G, SSM_P, S5_LW), F32), sd((SSM_G, SSM_P, S5_LW), F32),
                   sd((SSM_G, 1, S5_LW), F32)),
        grid=(SSM_G // _S5_GB,),
        in_specs=[_gspec((C, S5_LW)), _gspec((C, S5_LW)), _gspec((C, S5_LW)), _gspec((C, SSM_P)), _gspec((C, SSM_P)),
                  _gspec((SSM_P, S5_LW)), _gspec((SSM_P, S5_LW))],
        out_specs=(_gspec((C, S5_LW)), _gspec((C, SSM_P)), _gspec((C, SSM_P)), _gspec((S5_LW, S5_LW)),
                   _gspec((SSM_P, S5_LW)), _gspec((SSM_P, S5_LW)), _gspec((1, S5_LW))),
        compiler_params=_cparams(("parallel",)),
    )(dy, pre, u, sp_re, sp_im, cre, cim)


def _s5_rscan(dsp_re, dsp_im, sp_re, sp_im, lr, li, name):
    C = dsp_re.shape[0]

    def body(gr_ref, gi_ref, sr_ref, si_ref, lr_ref, li_ref, or_ref, oi_ref, dlr_ref, dli_ref):
        lrv, liv = lr_ref[...], li_ref[...]

        def step(k, carry):
            c = C - 1 - k
            dr, di, alr, ali = carry
            or_ref[c] = dr
            oi_ref[c] = di
            sr, si = sr_ref[c], si_ref[c]
            alr = alr + dr * sr + di * si
            ali = ali + di * sr - dr * si
            return gr_ref[c] + lrv * dr + liv * di, gi_ref[c] + lrv * di - liv * dr, alr, ali

        z = jnp.zeros((SSM_G, SSM_P), F32)
        _, _, alr, ali = lax.fori_loop(0, C, step, (z, z, z, z))
        dlr_ref[...] = alr
        dli_ref[...] = ali

    sd = jax.ShapeDtypeStruct((C, SSM_G, SSM_P), F32)
    sp = jax.ShapeDtypeStruct((SSM_G, SSM_P), F32)
    return pl.pallas_call(body, name=name, out_shape=(sd, sd, sp, sp), compiler_params=_cparams())(
        dsp_re, dsp_im, sp_re, sp_im, lr, li)


def _s5_bwd_in(dpre, dsin_re, dsin_im, u, tgt, bret, bimt, dflat, name):
    C = u.shape[1]

    def body(dp_ref, dr_ref, di_ref, u_ref, t_ref, br_ref, bi_ref, d_ref, du_ref, dbr_ref, dbi_ref):
        dp = dp_ref[...]
        dr, di, uv = dr_ref[...], di_ref[...], u_ref[...]
        du = (_bdot(dp, t_ref[...], _B_NT) + _bdot(dr, br_ref[...], _B_NT) + _bdot(di, bi_ref[...], _B_NT)
              + d_ref[...] * dp)
        du_ref[...] = du.astype(BF16)
        dbr_ref[...] = _bdot(uv, dr, _B_TN)
        dbi_ref[...] = _bdot(uv, di, _B_TN)

    sd = jax.ShapeDtypeStruct
    return pl.pallas_call(
        body, name=name,
        out_shape=(sd((SSM_G, C, S5_LW), BF16), sd((SSM_G, S5_LW, SSM_P), F32), sd((SSM_G, S5_LW, SSM_P), F32)),
        grid=(SSM_G // _S5_GB,),
        in_specs=[_gspec((C, S5_LW)), _gspec((C, SSM_P)), _gspec((C, SSM_P)), _gspec((C, S5_LW)),
                  _gspec((S5_LW, S5_LW)), _gspec((S5_LW, SSM_P)), _gspec((S5_LW, SSM_P)), _gspec((1, S5_LW))],
        out_specs=(_gspec((C, S5_LW)), _gspec((S5_LW, SSM_P)), _gspec((S5_LW, SSM_P))),
        compiler_params=_cparams(("parallel",)),
    )(dpre, dsin_re, dsin_im, u, tgt, bret, bimt, dflat)


def _to_chunks(v):
    T = v.shape[0]
    return v.reshape(T // S5_L, S5_L, SSM_G, SSM_H).transpose(2, 0, 1, 3).reshape(SSM_G, T // S5_L, S5_LW)


def _from_chunks(v):
    C = v.shape[1]
    return v.reshape(SSM_G, C, S5_L, SSM_H).transpose(1, 2, 0, 3).reshape(C * S5_L, MIX_W)


def _merge_fn(bra, brb, pc, pg, g0, g1, g2, b0, b1, b2):
    sg = jax.nn.sigmoid
    return sg(g0 + b0) * bra + sg(g1 + b1) * brb + sg(g2 + b2) * (pc * sg(pg))


_EW_CS = 256
_GATE_OFF = 6 * MIX_W


def _merge_rows(br4, proj):
    return [(b, 0, D_MODEL) for b in br4] + [(proj, _GATE_OFF + k * D_MODEL, D_MODEL) for k in range(3)]


def _merge_fwd(br4, proj, gb3, name):
    def fn(rp, cp):
        b = cp[0]
        return [_merge_fn(*rp, b[0:1], b[1:2], b[2:3])], []

    return _rowwise(fn, _merge_rows(br4, proj), [gb3], [(D_MODEL, BF16)], [], R=proj.shape[0], name=name, cs=_EW_CS)[0]


def _merge_bwd(br4, proj, gb3, dm, name):
    def fn(rp, cp):
        b = cp[0]
        shp = rp[0].shape
        bs = [jnp.broadcast_to(b[k:k + 1], shp) for k in range(3)]
        _, pull = jax.vjp(_merge_fn, *rp[:7], *bs)
        g = pull(rp[7])
        return list(g[:7]), list(g[7:])

    rows = _merge_rows(br4, proj) + [(dm, 0, D_MODEL)]
    outs = _rowwise(fn, rows, [gb3], [(D_MODEL, BF16)] * 7, [D_MODEL] * 3, R=proj.shape[0], name=name, tm=128, cs=_EW_CS)
    return outs[:4], outs[4:7], jnp.concatenate(outs[7:], axis=0)


def _swiglu(g, u):
    return jax.nn.silu(g) * u


def _act_fwd(gu, name):
    def fn(rp, cp):
        return [_swiglu(*rp)], []

    return _rowwise(fn, [(gu[0], 0, FFN_H), (gu[1], 0, FFN_H)], [], [(FFN_H, BF16)], [], R=gu[0].shape[0], name=name, cs=_EW_CS)[0]


def _act_bwd(gu, dact, name):
    def fn(rp, cp):
        _, pull = jax.vjp(_swiglu, rp[0], rp[1])
        return list(pull(rp[2])), []

    return _rowwise(fn, [(gu[0], 0, FFN_H), (gu[1], 0, FFN_H), (dact, 0, FFN_H)], [], [(FFN_H, BF16)] * 2, [],
                    R=dact.shape[0], name=name, tm=128, cs=_EW_CS)


def _adamw_fn(w, g, m, v):
    m = ADAM_B1 * m + (1.0 - ADAM_B1) * g
    v = ADAM_B2 * v + (1.0 - ADAM_B2) * jnp.square(g)
    m_hat = m / (1.0 - ADAM_B1 ** ADAM_STEP)
    v_hat = v / (1.0 - ADAM_B2 ** ADAM_STEP)
    delta = -ADAM_LR * (m_hat / (jnp.sqrt(v_hat) + ADAM_EPS) + ADAM_WD * w)
    return delta, m, v


def _adamw(w, g, m, v, name):
    R, C = w.shape

    def fn(rp, cp):
        return list(_adamw_fn(*rp)), []

    cs = _pick(C, (512, 256, 128))
    tm = _pick(R, (256, 128, 64, 32, 16, 8))
    return _rowwise(fn, [(w, 0, C), (g, 0, C), (m, 0, C), (v, 0, C)], [], [(C, F32)] * 3, [], R=R, name=name,
                    tm=tm, rs=8, cs=cs)


def _my_place():
    return lax.axis_index("x"), lax.axis_index("y"), lax.axis_index("c")


def _other_chips(x, y):
    return [(1 - x, y), (x, 1 - y), (1 - x, 1 - y)]


_ANY = pl.BlockSpec(memory_space=pl.ANY)


def _rcopy(src, dst, ssem, rsem, to):
    return pltpu.make_async_remote_copy(src_ref=src, dst_ref=dst, send_sem=ssem, recv_sem=rsem, device_id=to,
                                        device_id_type=MESH)


def _place_shard(local, axis, jidx, out_dtype, name):
    lead, r, c = local.shape
    shp = [lead, r, c]
    shp[axis] *= N_CHIPS
    tr = _pick(r, (512, 256, 128)) if r % 128 == 0 else r
    nr = r // tr
    omap = (lambda l, i, j: (l, i, j[0])) if axis == 2 else (lambda l, i, j: (l, j[0] * nr + i, 0))

    def body(j_ref, x_ref, o_ref):
        o_ref[...] = x_ref[...].astype(out_dtype)

    return pl.pallas_call(
        body, name=name, out_shape=jax.ShapeDtypeStruct(tuple(shp), out_dtype),
        grid_spec=pltpu.PrefetchScalarGridSpec(
            num_scalar_prefetch=1, grid=(lead, nr),
            in_specs=[pl.BlockSpec((None, tr, c), lambda l, i, j: (l, i, 0))],
            out_specs=pl.BlockSpec((None, tr, c), omap)),
        compiler_params=_cparams(("parallel", "parallel")),
    )(jidx, local)


class _GatherSide:
    def __init__(self, fulls, axes, regions):
        self.inputs = list(fulls)
        self.out_shapes = [jax.ShapeDtypeStruct(f.shape, f.dtype) for f in fulls]
        self.aliases = {t: t for t in range(len(fulls))}
        self.n_sems = 6 * len(fulls)
        self.axes, self.regions = list(axes), list(regions)

    def _block(self, outs, t, chip, half):
        start, size, split = self.regions[t][:3]
        piece, n_pieces = self.regions[t][3] if len(self.regions[t]) > 3 else (0, 1)
        ax = self.axes[t]
        cut = outs[t].shape[ax] // N_CHIPS
        j = 2 * chip[0] + chip[1]
        idx = [pl.ds(start, size), slice(None), slice(None)]
        idx[ax] = pl.ds(j * cut, cut)
        if split == "lead":
            idx[0] = pl.ds(start + half * (size // 2), size // 2)
        else:
            other = 3 - ax
            h = outs[t].shape[other] // (2 * n_pieces)
            idx[other] = pl.ds((half * n_pieces + piece) * h, h)
        return outs[t].at[tuple(idx)]

    def _sends(self, outs, send, recv):
        x, y, c = _my_place()
        cps = []
        for t in range(len(outs)):
            mine = self._block(outs, t, (x, y), c)
            for r, chip in enumerate(_other_chips(x, y)):
                k = 3 * t + r
                cps.append(_rcopy(mine, mine, send.at[k], recv.at[k], (*chip, c)))
        return cps

    def start(self, ins, outs, send, recv):
        for cp in self._sends(outs, send, recv):
            cp.start()

    def finish(self, ins, outs, send, recv):
        x, y, c = _my_place()
        sib = (x, y, 1 - c)
        n = len(outs)
        chips = _other_chips(x, y)
        passed = []
        for t in range(n):
            for r, chip in enumerate(chips):
                k = 3 * t + r
                landed = self._block(outs, t, chip, c)
                _rcopy(landed, landed, send.at[k], recv.at[k], (*chip, c)).wait_recv()
                cp = _rcopy(landed, landed, send.at[3 * n + k], recv.at[3 * n + k], sib)
                cp.start()
                passed.append(cp)
        for t in range(n):
            for r, chip in enumerate(chips):
                k = 3 * n + 3 * t + r
                theirs = self._block(outs, t, chip, 1 - c)
                _rcopy(theirs, theirs, send.at[k], recv.at[k], sib).wait_recv()
        for cp in self._sends(outs, send, recv) + passed:
            cp.wait_send()


def _run_side(side, name):
    s_in = len(side.inputs)

    def body(*refs):
        ins, outs = refs[:s_in], refs[s_in:s_in + len(side.out_shapes)]
        send, recv = refs[s_in + len(side.out_shapes):]
        side.start(ins, outs, send, recv)
        side.finish(ins, outs, send, recv)

    return pl.pallas_call(
        body, name=name, out_shape=tuple(side.out_shapes), in_specs=[_ANY] * s_in,
        out_specs=tuple([_ANY] * len(side.out_shapes)), input_output_aliases=dict(side.aliases),
        scratch_shapes=[pltpu.SemaphoreType.DMA((side.n_sems,)), pltpu.SemaphoreType.DMA((side.n_sems,))],
    )(*side.inputs)


def _half_idx(shape, axis, half):
    size = shape[axis] // 2
    idx = [slice(None), slice(None)]
    idx[axis] = pl.ds(half * size, size)
    return tuple(idx)


class _PairSide:
    def __init__(self, grads, half_axes):
        self.inputs = list(grads)
        self.half_axes = list(half_axes)
        self.out_shapes = []
        for g, ax in zip(grads, half_axes):
            shp = list(g.shape)
            shp[ax] //= 2
            self.out_shapes.append(jax.ShapeDtypeStruct(tuple(shp), g.dtype))
        self.aliases = {}
        self.n_sems = len(grads)

    def _copies(self, srcs, outs, send, recv):
        x, y, c = _my_place()
        return [_rcopy(srcs[t].at[_half_idx(srcs[t].shape, self.half_axes[t], 1 - c)], outs[t], send.at[t], recv.at[t],
                       (x, y, 1 - c)) for t in range(len(srcs))]

    def start(self, srcs, outs, send, recv):
        for cp in self._copies(srcs, outs, send, recv):
            cp.start()

    def finish(self, srcs, outs, send, recv):
        for cp in self._copies(srcs, outs, send, recv):
            cp.wait()


def _pair_sum(g, recv, half_axis, cidx, name):
    K, N = recv.shape
    tm = _pick(K, (256, 128, 64, 32, 16))
    tn = _pick(N, (1024, 1408, 512, 256, 128))
    nbr, nbc = K // tm, N // tn
    if half_axis == 0:
        gmap = lambda i, j, c: (c[0] * nbr + i, j)
    else:
        gmap = lambda i, j, c: (i, c[0] * nbc + j)

    def body(c_ref, g_ref, r_ref, of_ref, ob_ref):
        s = g_ref[...] + r_ref[...]
        of_ref[...] = s
        ob_ref[...] = s.astype(BF16)

    omap = lambda i, j, c: (i, j)
    return pl.pallas_call(
        body, name=name,
        out_shape=(jax.ShapeDtypeStruct((K, N), F32), jax.ShapeDtypeStruct((K, N), BF16)),
        grid_spec=pltpu.PrefetchScalarGridSpec(
            num_scalar_prefetch=1, grid=(nbr, nbc),
            in_specs=[pl.BlockSpec((tm, tn), gmap), pl.BlockSpec((tm, tn), omap)],
            out_specs=(pl.BlockSpec((tm, tn), omap), pl.BlockSpec((tm, tn), omap))),
        compiler_params=_cparams(("parallel", "parallel")),
    )(cidx, g, recv)


def _shard_idx(shape, axis, j):
    size = shape[axis] // N_CHIPS
    idx = [slice(None), slice(None)]
    idx[axis] = pl.ds(j * size, size)
    return tuple(idx)


class _ScatterSide:
    def __init__(self, parts, shard_axes):
        self.inputs = list(parts)
        self.shard_axes = list(shard_axes)
        self.out_shapes = []
        for p, ax in zip(parts, shard_axes):
            shp = list(p.shape)
            shp[ax] //= N_CHIPS
            self.out_shapes.append(jax.ShapeDtypeStruct((3,) + tuple(shp), p.dtype))
        self.aliases = {}
        self.n_sems = 3 * len(parts)

    def _copies(self, srcs, outs, send, recv):
        x, y, c = _my_place()
        cps = []
        for t in range(len(srcs)):
            for r, chip in enumerate(_other_chips(x, y)):
                k = 3 * t + r
                j = 2 * chip[0] + chip[1]
                cps.append(_rcopy(srcs[t].at[_shard_idx(srcs[t].shape, self.shard_axes[t], j)], outs[t].at[r],
                                  send.at[k], recv.at[k], (*chip, c)))
        return cps

    def start(self, srcs, outs, send, recv):
        for cp in self._copies(srcs, outs, send, recv):
            cp.start()

    def finish(self, srcs, outs, send, recv):
        for cp in self._copies(srcs, outs, send, recv):
            cp.wait()


def _shard_sum(pf, recv, acc, layer, shard_axis, jcidx, name):
    _, K, N = recv.shape
    tm = _pick(K, (256, 128, 64, 32, 16))
    tn = _pick(N, (1024, 1408, 512, 256, 128))
    nbr, nbc = K // tm, N // tn
    if shard_axis == 0:
        pmap = lambda i, j, s: (s[0] * nbr + i, j)
        omap = lambda i, j, s: (layer, i, s[1] * nbc + j)
    else:
        pmap = lambda i, j, s: (i, s[0] * nbc + j)
        omap = lambda i, j, s: (layer, s[1] * nbr + i, j)

    def body(j_ref, p_ref, r_ref, a_ref, o_ref):
        o_ref[...] = ((p_ref[...] + r_ref[0].astype(F32)) + r_ref[1].astype(F32)) + r_ref[2].astype(F32)

    return pl.pallas_call(
        body, name=name, out_shape=jax.ShapeDtypeStruct(acc.shape, F32),
        grid_spec=pltpu.PrefetchScalarGridSpec(
            num_scalar_prefetch=1, grid=(nbr, nbc),
            in_specs=[pl.BlockSpec((tm, tn), pmap), pl.BlockSpec((3, tm, tn), lambda i, j, s: (0, i, j)), _ANY],
            out_specs=pl.BlockSpec((None, tm, tn), omap)),
        input_output_aliases={3: 0},
        compiler_params=_cparams(("parallel", "parallel")),
    )(jcidx, pf, recv, acc)


def _pair_join(accs, half_axes, name):
    n = len(accs)

    def body(*refs):
        outs = refs[n:2 * n]
        send_sems, recv_sems = refs[2 * n:]
        x, y, c = _my_place()
        sib = (x, y, 1 - c)

        def half(t, hc):
            return outs[t].at[(slice(None),) + _half_idx(outs[t].shape[1:], half_axes[t], hc)]

        cps = []
        for t in range(n):
            cp = _rcopy(half(t, c), half(t, c), send_sems.at[t], recv_sems.at[t], sib)
            cp.start()
            cps.append(cp)
        for t in range(n):
            _rcopy(half(t, 1 - c), half(t, 1 - c), send_sems.at[t], recv_sems.at[t], sib).wait_recv()
        for cp in cps:
            cp.wait_send()

    return pl.pallas_call(
        body, name=name, out_shape=tuple(jax.ShapeDtypeStruct(a.shape, a.dtype) for a in accs),
        in_specs=[_ANY] * n, out_specs=tuple([_ANY] * n), input_output_aliases={t: t for t in range(n)},
        scratch_shapes=[pltpu.SemaphoreType.DMA((n,)), pltpu.SemaphoreType.DMA((n,))],
    )(*accs)


_N_DEV = 8


def _allreduce_small(flat, name):
    _, R, _ = flat.shape

    def body(in_ref, out_ref, stage, send1, recv1, send2, recv2):
        x, y, c = _my_place()
        me = 4 * x + 2 * y + c
        places = [(px, py, pc) for px in range(2) for py in range(2) for pc in range(2)]
        def peer(r):
            return (x ^ (r >> 2), y ^ ((r >> 1) & 1), c ^ (r & 1))

        def peer_id(r):
            p = peer(r)
            return 4 * p[0] + 2 * p[1] + p[2]

        stage[0] = in_ref[me]
        cps = []
        for r in range(1, _N_DEV):
            cp = _rcopy(in_ref.at[peer_id(r)], stage.at[r], send1.at[r], recv1.at[r], peer(r))
            cp.start()
            cps.append(cp)
        for cp in cps:
            cp.wait()
        tot = jnp.zeros((R, 128), F32)
        for d in range(_N_DEV):
            tot = tot + stage[me ^ d]
        out_ref[me] = tot
        cps = []
        for r in range(1, _N_DEV):
            cp = _rcopy(out_ref.at[me], out_ref.at[me], send2.at[r], recv2.at[r], peer(r))
            cp.start()
            cps.append(cp)
        for r in range(1, _N_DEV):
            _rcopy(out_ref.at[peer_id(r)], out_ref.at[peer_id(r)], send2.at[r], recv2.at[r], peer(r)).wait_recv()
        for cp in cps:
            cp.wait_send()

    vm = pl.BlockSpec(memory_space=pltpu.VMEM)
    return pl.pallas_call(
        body, name=name, out_shape=jax.ShapeDtypeStruct(flat.shape, F32), in_specs=[vm], out_specs=vm,
        scratch_shapes=[pltpu.VMEM(flat.shape, F32)] + [pltpu.SemaphoreType.DMA((_N_DEV,))] * 4,
        compiler_params=_cparams(),
    )(flat)


_BIG = ("w_in", "ssm_w_glu", "w_branch", "w_out", "w_ffn_gate", "w_ffn_up", "w_ffn_down")
_BIG_SHARD_AXIS = {"w_in": 1, "ssm_w_glu": 1, "w_branch": 1, "w_out": 0, "w_ffn_gate": 1, "w_ffn_up": 1, "w_ffn_down": 0}
_SMALL = ("norm_mix_g", "gate_bias", "lru_conv_w", "lru_conv_b", "lru_wa", "lru_ba", "lru_wx", "lru_bx", "lru_lambda",
          "attn_rel_bias", "ssm_a_re", "ssm_a_im", "ssm_b_re", "ssm_b_im", "ssm_c_re", "ssm_c_im", "ssm_d",
          "ssm_log_step", "norm_ffn_g", "norm_final_g")
_SMALL_SHARDED = {"gate_bias": 2, "lru_conv_w": 2}
_WEIGHTS = ("norm_mix_g", "w_in", "gate_bias", "lru_conv_w", "lru_conv_b", "lru_wa", "lru_ba", "lru_wx", "lru_bx",
            "lru_lambda", "attn_rel_bias", "ssm_a_re", "ssm_a_im", "ssm_b_re", "ssm_b_im", "ssm_c_re", "ssm_c_im",
            "ssm_d", "ssm_log_step", "ssm_w_glu", "w_branch", "w_out", "norm_ffn_g", "w_ffn_gate", "w_ffn_up",
            "w_ffn_down", "norm_final_g")


def _carried(comm, phase, l, key, W, fn, *args, **kw):
    side = comm.side(phase, l, key, W)
    if side is None:
        return fn(*args, **kw)
    out, got = fn(*args, side=side, **kw)
    comm.took(phase, l, key, got, W)
    return out


def _layer_fwd(l, x, W, sm, comm):
    T = x.shape[0]
    nm = lambda s: f"{s}"
    h1 = _rms_fwd(x, sm["norm_mix_g"][l][None, :], nm("rms_fwd"))
    proj = _carried(comm, "fwd", l, "mm_in", W, _mm, h1, W["w_in"], M=T, N=IN_W, K=D_MODEL, b_lead=l, name=nm("mm_in"))
    wax, bax = sm["lru_prep"][l]
    cw, cb, lam = sm["lru_conv_w"][l], sm["lru_conv_b"][l][None, :], sm["lru_lambda"][l][None, :]
    y_a, hst = _lru_fwd(proj, cw, cb, wax, bax, lam, nm("lru_fwd"))
    bias = sm["attn_bias"][l]
    y_b = _carried(comm, "fwd", l, "attn_fwd", W, _attn_fwd, proj, bias, nm("attn_fwd"))
    tgt, bret, bimt, cre, cim, lr, li, dflat = sm["s5_prep"][l]
    u = _to_chunks(lax.slice_in_dim(proj, 5 * MIX_W, 6 * MIX_W, axis=1))
    sin_re, sin_im = _s5_in(u, bret, bimt, nm("s5_in"))
    sp_re, sp_im = _s5_scan(sin_re.transpose(1, 0, 2), sin_im.transpose(1, 0, 2), lr, li, nm("s5_scan"))
    sp_re, sp_im = sp_re.transpose(1, 0, 2), sp_im.transpose(1, 0, 2)
    pre, ycf = _s5_out(u, sp_re, sp_im, tgt, cre, cim, dflat, nm("s5_out"))
    y_c = _from_chunks(ycf)
    brs = []
    for k, yk in enumerate((y_a, y_b, y_c)):
        brs.append(_mm(yk, W["w_branch"], M=T, N=D_MODEL, K=MIX_W, b_lead=3 * l + k, out_dtype=BF16, name=nm("mm_branch")))
    brs.append(_mm(y_c, W["ssm_w_glu"], M=T, N=D_MODEL, K=MIX_W, b_lead=l, out_dtype=BF16, name=nm("mm_branch")))
    br4 = tuple(brs)
    gb3 = sm["gate_bias"][l]
    merged = _merge_fwd(br4, proj, gb3, nm("merge_fwd"))
    x1 = _mm(merged, W["w_out"], M=T, N=D_MODEL, K=D_MODEL, b_lead=l, res=x, name=nm("mm_out"))
    h2 = _rms_fwd(x1, sm["norm_ffn_g"][l][None, :], nm("rms_fwd"))
    gpre = _carried(comm, "fwd", l, "mm_ffn_gate", W, _mm, h2, W["w_ffn_gate"], M=T, N=FFN_H, K=D_MODEL, b_lead=l,
                    out_dtype=BF16, name=nm("mm_ffn_up"))
    upre = _carried(comm, "fwd", l, "mm_ffn_up", W, _mm, h2, W["w_ffn_up"], M=T, N=FFN_H, K=D_MODEL, b_lead=l,
                    out_dtype=BF16, name=nm("mm_ffn_up"))
    gu = (gpre, upre)
    act = _act_fwd(gu, nm("act_fwd"))
    x2 = _carried(comm, "fwd", l, "mm_down", W, _mm, act, W["w_ffn_down"], M=T, N=D_MODEL, K=FFN_H, b_lead=l, res=x1,
                  name=nm("mm_down"))
    saved = dict(x=x, h1=h1, proj=proj, hst=hst, y_a=y_a, y_b=y_b, y_c=y_c, u=u, sp_re=sp_re, sp_im=sp_im, pre=pre,
                 br4=br4, merged=merged, x1=x1, h2=h2, gu=gu, act=act)
    return x2, saved


def _layer_bwd(l, dx2, sv, W, sm, comm):
    T = dx2.shape[0]
    nm = lambda s: f"{s}"
    big, small = {}, {}
    dxb = dx2.astype(BF16)
    big["w_ffn_down"] = _carried(comm, "bwd", l, "mm_dw_down", W, _mm, sv["act"], dxb, M=FFN_H, N=D_MODEL, K=T, ta=True,
                                 name=nm("mm_dw_down"))
    dact = _carried(comm, "bwd", l, "mm_dact", W, _mm, dxb, W["w_ffn_down"], M=T, N=FFN_H, K=D_MODEL, tb=True, b_lead=l,
                    out_dtype=BF16, name=nm("mm_dact"))
    dg, du = _act_bwd(sv["gu"], dact, nm("act_bwd"))
    big["w_ffn_gate"] = _carried(comm, "bwd", l, "mm_dw_gate", W, _mm, sv["h2"], dg, M=D_MODEL, N=FFN_H, K=T, ta=True,
                                 name=nm("mm_dw_up"))
    big["w_ffn_up"] = _mm(sv["h2"], du, M=D_MODEL, N=FFN_H, K=T, ta=True, name=nm("mm_dw_up"))
    dh2 = _mm(dg, W["w_ffn_gate"], M=T, N=D_MODEL, K=FFN_H, tb=True, b_lead=l, name=nm("mm_dh2"))
    dh2 = _mm(du, W["w_ffn_up"], M=T, N=D_MODEL, K=FFN_H, tb=True, b_lead=l, res=dh2, name=nm("mm_dh2r"))
    dx1, dgn = _rms_bwd(sv["x1"], sm["norm_ffn_g"][l][None, :], dh2, dx2, nm("rms_bwd"))
    small["norm_ffn_g"] = dgn[0]
    dx1b = dx1.astype(BF16)
    big["w_out"] = _mm(sv["merged"], dx1b, M=D_MODEL, N=D_MODEL, K=T, ta=True, name=nm("mm_dw_out"))
    dm = _mm(dx1b, W["w_out"], M=T, N=D_MODEL, K=D_MODEL, tb=True, b_lead=l, out_dtype=BF16, name=nm("mm_dmerged"))
    dbr, dgates, dgb = _merge_bwd(sv["br4"], sv["proj"], sm["gate_bias"][l], dm, nm("merge_bwd"))
    small["gate_bias"] = dgb
    ys = (sv["y_a"], sv["y_b"], sv["y_c"])
    big["w_branch"] = [_mm(ys[k], dbr[k], M=MIX_W, N=D_MODEL, K=T, ta=True, name=nm("mm_dw_branch")) for k in range(3)]
    big["ssm_w_glu"] = _mm(sv["y_c"], dbr[3], M=MIX_W, N=D_MODEL, K=T, ta=True, name=nm("mm_dw_branch"))
    dya = _mm(dbr[0], W["w_branch"], M=T, N=MIX_W, K=D_MODEL, tb=True, b_lead=3 * l, name=nm("mm_dy"))
    dyb = _mm(dbr[1], W["w_branch"], M=T, N=MIX_W, K=D_MODEL, tb=True, b_lead=3 * l + 1, out_dtype=BF16, name=nm("mm_dy"))
    dyc = _mm(dbr[2], W["w_branch"], M=T, N=MIX_W, K=D_MODEL, tb=True, b_lead=3 * l + 2, name=nm("mm_dy"))
    dyc = _mm(dbr[3], W["ssm_w_glu"], M=T, N=MIX_W, K=D_MODEL, tb=True, b_lead=l, res=dyc, name=nm("mm_dyr"))
    tgt, bret, bimt, cre, cim, lr, li, dflat = sm["s5_prep"][l]
    dpre, dsp_re, dsp_im, d_tgt, d_cre, d_cim, d_dflat = _s5_bwd_out(
        _to_chunks(dyc), sv["pre"], sv["u"], sv["sp_re"], sv["sp_im"], cre, cim, nm("s5_bwd_out"))
    tr = lambda a: a.transpose(1, 0, 2)
    dsin_re, dsin_im, d_lr, d_li = _s5_rscan(tr(dsp_re), tr(dsp_im), tr(sv["sp_re"]), tr(sv["sp_im"]), lr, li, nm("s5_rscan"))
    du_f, d_bret, d_bimt = _s5_bwd_in(dpre, dsin_re.transpose(1, 0, 2), dsin_im.transpose(1, 0, 2), sv["u"], tgt, bret, bimt,
                                      dflat, nm("s5_bwd_in"))
    d_u = _from_chunks(du_f)
    small["s5_tables"] = (d_tgt, d_bret, d_bimt, d_cre, d_cim, d_lr, d_li, d_dflat)
    dq, dk, dv, dbias = _carried(comm, "bwd", l, "attn_bwd", W, _attn_bwd, sv["proj"], sm["attn_bias"][l], dyb, nm("attn_bwd"))
    small["attn_bias"] = dbias
    wax, bax = sm["lru_prep"][l]
    cw, cb, lam = sm["lru_conv_w"][l], sm["lru_conv_b"][l][None, :], sm["lru_lambda"][l][None, :]
    d_lx, d_lg, d_cw, d_cb, d_wax, d_bax, d_lam = _lru_bwd(sv["proj"], sv["hst"], dya, cw, cb, wax, bax, lam, nm("lru_bwd"))
    small["lru_conv_w"], small["lru_conv_b"], small["lru_lambda"] = d_cw, d_cb[0], d_lam[0]
    small["lru_tables"] = (d_wax, d_bax)
    dproj = jnp.concatenate([d_lx, d_lg, dq, dk, dv, d_u] + list(dgates), axis=1)
    big["w_in"] = _carried(comm, "bwd", l, "mm_dw_in", W, _mm, sv["h1"], dproj, M=D_MODEL, N=IN_W, K=T, ta=True,
                           name=nm("mm_dw_in"))
    dh1 = _carried(comm, "bwd", l, "mm_dh1", W, _mm, dproj, W["w_in"], M=T, N=D_MODEL, K=IN_W, tb=True, b_lead=l,
                   name=nm("mm_dh1"))
    dx, dgn = _rms_bwd(sv["x"], sm["norm_mix_g"][l][None, :], dh1, dx1, nm("rms_bwd"))
    small["norm_mix_g"] = dgn[0]
    return dx, big, small


_TENSORS = tuple((n, k) for n in _BIG for k in range(N_BRANCH if n == "w_branch" else 1))
_FWD_CARRIERS = {"mm_in": (("w_out", "w_branch", "ssm_w_glu", "w_ffn_gate"), 0), "attn_fwd": (("w_ffn_up",), 0),
                 "mm_ffn_gate": (("w_ffn_down",), 0), "mm_ffn_up": (("w_in",), 1), "mm_down": (("w_in",), 1)}
_FWD_PIECES = {"mm_ffn_up": (0, 2), "mm_down": (1, 2)}
_BWD_PAIR_CARRIERS = {"mm_dw_down": ("w_in", "w_out", "ssm_w_glu"), "mm_dact": ("w_ffn_gate", "w_ffn_up", "w_ffn_down", "w_branch")}
_BWD_CARRIERS = {"mm_dw_gate": ("w_out", "w_branch", "ssm_w_glu"), "attn_bwd": ("w_ffn_down",), "mm_dw_in": ("w_in",),
                 "mm_dh1": ("w_ffn_gate", "w_ffn_up")}


class _StepComm:
    def __init__(self, depth, cidx, jcidx):
        self.depth, self.cidx, self.jcidx = depth, cidx, jcidx
        self.accs = {}
        self.raw = None
        self.paired = {}
        self.pending = None

    def gather_side(self, W, names, l, piece):
        fulls, axes, regions = [], [], []
        for n in names:
            per = N_BRANCH if n == "w_branch" else 1
            fulls.append(W[n])
            axes.append(_BIG_SHARD_AXIS[n] + 1)
            regions.append((per * l, per, "other", piece))
        return _GatherSide(fulls, axes, regions)

    def side(self, phase, l, key, W):
        if phase == "fwd":
            if key not in _FWD_CARRIERS or l + _FWD_CARRIERS[key][1] >= self.depth:
                return None
            names, ahead = _FWD_CARRIERS[key]
            return self.gather_side(W, names, l + ahead, _FWD_PIECES.get(key, (0, 1)))
        if key in _BWD_PAIR_CARRIERS and self.raw is not None:
            nks = [nk for nk in _TENSORS if nk[0] in _BWD_PAIR_CARRIERS[key]]
            return _PairSide([self.raw[1][nk] for nk in nks], [1 - _BIG_SHARD_AXIS[nk[0]] for nk in nks])
        if key in _BWD_CARRIERS and self.pending is not None:
            nks = [nk for nk in _TENSORS if nk[0] in _BWD_CARRIERS[key]]
            return _ScatterSide([self.pending[1][nk][1] for nk in nks], [_BIG_SHARD_AXIS[nk[0]] for nk in nks])
        return None

    def took(self, phase, l, key, got, W):
        if phase == "fwd":
            W.update(zip(_FWD_CARRIERS[key][0], got))
        elif key in _BWD_PAIR_CARRIERS:
            nks = [nk for nk in _TENSORS if nk[0] in _BWD_PAIR_CARRIERS[key]]
            self.paired.update(zip(nks, got))
            if len(self.paired) == len(_TENSORS):
                self.pair_sums()
        else:
            nks = [nk for nk in _TENSORS if nk[0] in _BWD_CARRIERS[key]]
            self.shard_sums(self.pending[0], nks, got)

    def pair_sums(self):
        l, grads = self.raw
        self.pending = (l, {nk: _pair_sum(grads[nk], self.paired[nk], 1 - _BIG_SHARD_AXIS[nk[0]], self.cidx, "pair_sum")
                            for nk in _TENSORS})
        self.raw, self.paired = None, {}

    def shard_sums(self, l, nks, got):
        for nk, r in zip(nks, got):
            pf = self.pending[1][nk][0]
            sa = _BIG_SHARD_AXIS[nk[0]]
            if nk not in self.accs:
                shp = list(pf.shape)
                shp[sa] //= N_CHIPS
                shp[1 - sa] *= 2
                self.accs[nk] = lax.empty((self.depth,) + tuple(shp), F32)
            self.accs[nk] = _shard_sum(pf, r, self.accs[nk], l, sa, self.jcidx, "shard_sum")

    def on_big(self, l, big):
        grads = {}
        for n in _BIG:
            gs = big[n] if isinstance(big[n], list) else [big[n]]
            grads.update({(n, k): g for k, g in enumerate(gs)})
        self.raw = (l, grads)
        if l == 0:
            half_axes = [1 - _BIG_SHARD_AXIS[nk[0]] for nk in _TENSORS]
            self.paired = dict(zip(_TENSORS, _run_side(_PairSide([grads[nk] for nk in _TENSORS], half_axes), "pair_exchange")))
            self.pair_sums()
            side = _ScatterSide([self.pending[1][nk][1] for nk in _TENSORS], [_BIG_SHARD_AXIS[nk[0]] for nk in _TENSORS])
            self.shard_sums(0, _TENSORS, _run_side(side, "chip_scatter"))
            self.pending = None


class _NoComm:
    def __init__(self, on_big):
        self.on_big = on_big

    def side(self, phase, l, key, W):
        return None


def _local_step(xs, tgt, W, sm, comm):
    W = dict(W)
    sm = dict(sm)
    depth = sm["norm_mix_g"].shape[0]
    lru_o, lru_vjp = jax.vjp(jax.vmap(_lru_prep), sm["lru_wa"], sm["lru_wx"], sm["lru_ba"], sm["lru_bx"])
    attn_o, attn_vjp = jax.vjp(jax.vmap(_attn_bias), sm["attn_rel_bias"])
    s5_names = ("ssm_a_re", "ssm_a_im", "ssm_b_re", "ssm_b_im", "ssm_c_re", "ssm_c_im", "ssm_d", "ssm_log_step")
    s5_o, s5_vjp = jax.vjp(jax.vmap(_s5_prep), *[sm[n] for n in s5_names])
    wax_all = lru_o[0].astype(BF16)
    sm["lru_prep"] = [(wax_all[l], lru_o[1][l]) for l in range(depth)]
    sm["attn_bias"] = [attn_o[l] for l in range(depth)]
    sm["s5_prep"] = [tuple(t[l] for t in s5_o) for l in range(depth)]

    saved = []
    for l in range(depth):
        xs, sv = _layer_fwd(l, xs, W, sm, comm)
        saved.append(sv)
    loss_part, dx, dgf = _final_loss(xs, sm["norm_final_g"][None, :], tgt, "final_loss")

    direct = ("norm_mix_g", "gate_bias", "lru_conv_w", "lru_conv_b", "lru_lambda", "norm_ffn_g")
    per_layer = [None] * depth
    for l in reversed(range(depth)):
        dx, big, per_layer[l] = _layer_bwd(l, dx, saved[l], W, sm, comm)
        comm.on_big(l, big)
    stacked = lambda key, i: jnp.stack([per_layer[l][key][i] for l in range(depth)])
    small_tree = {n: jnp.stack([per_layer[l][n] for l in range(depth)]) for n in direct}
    d_wa, d_wx, d_ba, d_bx = lru_vjp(tuple(stacked("lru_tables", i) for i in range(2)))
    (d_rel,) = attn_vjp(jnp.stack([per_layer[l]["attn_bias"] for l in range(depth)]))
    d_s5 = s5_vjp(tuple(stacked("s5_tables", i) for i in range(len(s5_o))))
    small_tree.update(lru_wa=d_wa, lru_wx=d_wx, lru_ba=d_ba, lru_bx=d_bx, attn_rel_bias=d_rel, norm_final_g=dgf[0])
    small_tree.update(zip(s5_names, d_s5))
    return loss_part, dx, small_tree


def _pack_small(tree, names):
    flat = jnp.concatenate([tree[n].reshape(-1) for n in names])
    per = -(-flat.shape[0] // (_N_DEV * 128 * 8)) * (128 * 8)
    flat = jnp.pad(flat, (0, _N_DEV * per - flat.shape[0]))
    return flat.reshape(_N_DEV, per // 128, 128)


def _unpack_small(flat, like, names):
    flat = flat.reshape(-1)
    out, off = {}, 0
    for n in names:
        size = math.prod(like[n].shape)
        out[n] = flat[off:off + size].reshape(like[n].shape)
        off += size
    return out


def kernel(x, norm_mix_g, w_in, gate_bias, lru_conv_w, lru_conv_b, lru_wa, lru_ba, lru_wx, lru_bx, lru_lambda, attn_rel_bias, ssm_a_re, ssm_a_im, ssm_b_re, ssm_b_im, ssm_c_re, ssm_c_im, ssm_d, ssm_log_step, ssm_w_glu, w_branch, w_out, norm_ffn_g, w_ffn_gate, w_ffn_up, w_ffn_down, norm_final_g, loss_target, m_norm_mix_g, m_w_in, m_gate_bias, m_lru_conv_w, m_lru_conv_b, m_lru_wa, m_lru_ba, m_lru_wx, m_lru_bx, m_lru_lambda, m_attn_rel_bias, m_ssm_a_re, m_ssm_a_im, m_ssm_b_re, m_ssm_b_im, m_ssm_c_re, m_ssm_c_im, m_ssm_d, m_ssm_log_step, m_ssm_w_glu, m_w_branch, m_w_out, m_norm_ffn_g, m_w_ffn_gate, m_w_ffn_up, m_w_ffn_down, m_norm_final_g, v_norm_mix_g, v_w_in, v_gate_bias, v_lru_conv_w, v_lru_conv_b, v_lru_wa, v_lru_ba, v_lru_wx, v_lru_bx, v_lru_lambda, v_attn_rel_bias, v_ssm_a_re, v_ssm_a_im, v_ssm_b_re, v_ssm_b_im, v_ssm_c_re, v_ssm_c_im, v_ssm_d, v_ssm_log_step, v_ssm_w_glu, v_w_branch, v_w_out, v_norm_ffn_g, v_w_ffn_gate, v_w_ffn_up, v_w_ffn_down, v_norm_final_g):
    args = dict(locals())
    w = {n: args[n] for n in _WEIGHTS}
    m = {n: args["m_" + n] for n in _WEIGHTS}
    v = {n: args["v_" + n] for n in _WEIGHTS}
    depth = w_in.shape[0]
    xc, yc, cc = _my_place()
    jchip = 2 * xc + yc
    cidx = jnp.reshape(cc, (1,)).astype(jnp.int32)
    jidx = jnp.reshape(jchip, (1,)).astype(jnp.int32)
    jcidx = jnp.stack([jchip, cc]).astype(jnp.int32)

    blocks = [w[n] for n in _BIG]
    blocks[2] = blocks[2].reshape(depth * N_BRANCH, MIX_W, -1)
    axes = [_BIG_SHARD_AXIS[n] + 1 for n in _BIG] + [_SMALL_SHARDED[n] for n in _SMALL_SHARDED]
    placed = [_place_shard(b, ax, jidx, BF16, "place_shard") for b, ax in zip(blocks, axes)]
    placed += [_place_shard(w[n], _SMALL_SHARDED[n], jidx, F32, "place_shard") for n in _SMALL_SHARDED]
    W = dict(zip(_BIG, placed))
    first = [W["w_in"]] + placed[len(_BIG):]
    regions = [(0, 1, "other")] + [(0, depth, "lead")] * len(_SMALL_SHARDED)
    gathered = _run_side(_GatherSide(first, [axes[0]] + axes[len(_BIG):], regions), "gather_weights")
    W["w_in"] = gathered[0]
    sm_full = dict(zip(_SMALL_SHARDED, gathered[1:]))
    sm = {n: w[n] for n in _SMALL if n not in _SMALL_SHARDED}
    sm.update(sm_full)

    comm = _StepComm(depth, cidx, jcidx)
    loss_part, dx, small_tree = _local_step(x[0], loss_target[0], W, sm, comm)
    loss = lax.psum(loss_part, ("x", "y", "c"))
    grad_x = dx[None]

    joined = _pair_join([comm.accs[nk] for nk in _TENSORS], [1 - _BIG_SHARD_AXIS[nk[0]] for nk in _TENSORS], "pair_join")
    jd = dict(zip(_TENSORS, joined))
    grads = {}
    for n in _BIG:
        if n == "w_branch":
            grads[n] = jnp.stack([jd[(n, k)] for k in range(N_BRANCH)], axis=1)
        else:
            grads[n] = jd[(n, 0)]
    like = {n: (sm_full[n] if n in _SMALL_SHARDED else w[n]) for n in _SMALL}
    red = _unpack_small(_allreduce_small(_pack_small(small_tree, _SMALL), "allreduce_small"), like, _SMALL)
    for n in _SMALL:
        if n in _SMALL_SHARDED:
            size = w[n].shape[2]
            grads[n] = lax.dynamic_slice_in_dim(red[n], (2 * xc + yc) * size, size, axis=2)
        else:
            grads[n] = red[n]

    delta, new_m, new_v = {}, {}, {}
    for n in _BIG:
        shp = w[n].shape
        two = lambda a: a.reshape(-1, shp[-1])
        d_, m_, v_ = _adamw(two(w[n]), two(grads[n]), two(m[n]), two(v[n]), "adamw")
        delta[n], new_m[n], new_v[n] = d_.reshape(shp), m_.reshape(shp), v_.reshape(shp)
    pk = lambda tree: _pack_small(tree, _SMALL).reshape(-1, 128)
    d_, m_, v_ = _adamw(pk(w), pk(grads), pk(m), pk(v), "adamw_small")
    like_local = {n: w[n] for n in _SMALL}
    for tree, flat in ((delta, d_), (new_m, m_), (new_v, v_)):
        tree.update(_unpack_small(flat, like_local, _SMALL))
    return (loss, grad_x, *[grads[n] for n in _WEIGHTS], *[delta[n] for n in _WEIGHTS], *[new_m[n] for n in _WEIGHTS],
            *[new_v[n] for n in _WEIGHTS])
```

```python
import functools
import math

import jax
import jax.numpy as jnp
from jax import lax
from jax.experimental import pallas as pl
from jax.experimental.pallas import tpu as pltpu

F32 = jnp.float32
BF16 = jnp.bfloat16

D_MODEL = 2048
MIX_W = 1024
N_BRANCH = 3
LRU_BLOCKS = 16
LRU_BW = 64
CONV_W = 4
LRU_C = 8.0
CHUNK = 64
ATT_HEADS = 8
ATT_HD = 128
ATT_LEFT = 8
ATT_BAND = (ATT_LEFT + 1) * CHUNK
MAX_REL = 128
N_REL = 2 * MAX_REL + 1
SSM_G = 64
SSM_H = 16
SSM_P = 64
FFN_H = 5632
IN_W = 6 * MIX_W + N_BRANCH * D_MODEL
NORM_EPS = 1e-6
MASK_VALUE = -1e30
ADAM_LR, ADAM_B1, ADAM_B2, ADAM_EPS, ADAM_WD, ADAM_STEP = 0.001, 0.9, 0.999, 1e-08, 0.01, 10

S5_L = 16
S5_LW = S5_L * SSM_H
N_CHIPS = 4
V7X_VMEM_LIMIT = 56 * 1024 * 1024
HI = lax.Precision.HIGHEST
MESH = pl.DeviceIdType.MESH


def _cparams(sem=None):
    return pltpu.CompilerParams(dimension_semantics=sem, vmem_limit_bytes=V7X_VMEM_LIMIT)


def _pick(n, prefs):
    for p in prefs:
        if n % p == 0:
            return p
    return n


_MM_VMEM_BUDGET = 46 * 1024 * 1024


def _mm_tiles(M, N, K, has_res, out_bytes):
    tn = _pick(N, (1024, 1408, 512, 256, 128))
    for tk in (K, 2048, 1408, 1024, 512, 256, 128):
        if K % tk:
            continue
        for tm in (1024, 512, 256, 128, 64, 32, 16, 8):
            if M % tm:
                continue
            need = 2 * 2 * (tm * tk + tk * tn) + 2 * tm * tn * out_bytes
            need += tm * tn * 4 if tk < K else 0
            need += 2 * tm * tn * 4 if has_res else 0
            if need <= _MM_VMEM_BUDGET:
                return tm, tn, tk
    raise ValueError((M, N, K))


def _call(body, *, name, grid, in_specs, out_specs, out_shape, args, scratch_shapes=(), sem=None, side=None):
    in_specs, out_specs, out_shape = list(in_specs), list(out_specs), list(out_shape)
    scratch_shapes = list(scratch_shapes)
    if side is None:
        outs = pl.pallas_call(body, name=name, out_shape=tuple(out_shape), grid=grid, in_specs=in_specs,
                              out_specs=tuple(out_specs), scratch_shapes=scratch_shapes, compiler_params=_cparams(sem))(*args)
        return tuple(outs), ()
    n_in, n_out, n_scr = len(in_specs), len(out_shape), len(scratch_shapes)
    s_in, s_out = len(side.inputs), len(side.out_shapes)

    def wrapped(*refs):
        mi, refs = refs[:n_in], refs[n_in:]
        si, refs = refs[:s_in], refs[s_in:]
        mo, refs = refs[:n_out], refs[n_out:]
        so, refs = refs[:s_out], refs[s_out:]
        scr, (send, recv) = refs[:n_scr], refs[n_scr:]
        first = functools.reduce(jnp.logical_and, [pl.program_id(d) == 0 for d in range(len(grid))])
        last = functools.reduce(jnp.logical_and, [pl.program_id(d) == g - 1 for d, g in enumerate(grid)])

        @pl.when(first)
        def _():
            side.start(si, so, send, recv)

        body(*mi, *mo, *scr)

        @pl.when(last)
        def _():
            side.finish(si, so, send, recv)

    outs = pl.pallas_call(
        wrapped, name=name, out_shape=tuple(out_shape + list(side.out_shapes)), grid=grid,
        in_specs=in_specs + [_ANY] * s_in, out_specs=tuple(out_specs + [_ANY] * s_out),
        scratch_shapes=scratch_shapes + [pltpu.SemaphoreType.DMA((side.n_sems,)), pltpu.SemaphoreType.DMA((side.n_sems,))],
        input_output_aliases={n_in + i: n_out + o for i, o in side.aliases.items()},
        compiler_params=_cparams(("arbitrary",) * len(grid)),
    )(*args, *side.inputs)
    return tuple(outs[:n_out]), tuple(outs[n_out:])


def _mm(a, b, *, M, N, K, name, ta=False, tb=False, a_lead=None, b_lead=None, a_off=(0, 0), b_off=(0, 0),
        out_dtype=F32, res=None, tm=None, tn=None, tk=None, side=None):
    if tm is None and tn is None and tk is None:
        tm, tn, tk = _mm_tiles(M, N, K, res is not None, jnp.dtype(out_dtype).itemsize)
    nk = K // tk

    def spec(blk, lead, off, order):
        r0, c0 = off[0] // blk[0], off[1] // blk[1]
        assert off[0] % blk[0] == 0 and off[1] % blk[1] == 0
        if lead is None:
            return pl.BlockSpec(blk, lambda i, j, k: (r0 + order(i, j, k)[0], c0 + order(i, j, k)[1]))
        return pl.BlockSpec((None,) + blk, lambda i, j, k: (lead, r0 + order(i, j, k)[0], c0 + order(i, j, k)[1]))

    a_spec = spec((tk, tm), a_lead, a_off, lambda i, j, k: (k, i)) if ta else spec((tm, tk), a_lead, a_off, lambda i, j, k: (i, k))
    b_spec = spec((tn, tk), b_lead, b_off, lambda i, j, k: (j, k)) if tb else spec((tk, tn), b_lead, b_off, lambda i, j, k: (k, j))
    dims = (((0 if ta else 1,), (1 if tb else 0,)), ((), ()))
    in_specs = [a_spec, b_spec]
    args = [a, b]
    if res is not None:
        in_specs.append(pl.BlockSpec((tm, tn), lambda i, j, k: (i, j)))
        args.append(res)

    def body(*refs):
        a_ref, b_ref = refs[:2]
        r_ref = refs[2] if res is not None else None
        o_ref = refs[3] if res is not None else refs[2]

        def dot():
            return lax.dot_general(a_ref[...], b_ref[...], dims, preferred_element_type=F32)

        def finish(r):
            if r_ref is not None:
                r = r + r_ref[...]
            o_ref[...] = r.astype(out_dtype)

        if nk == 1:
            finish(dot())
            return
        acc = refs[-1]
        k = pl.program_id(2)

        @pl.when(k == 0)
        def _():
            acc[...] = dot()

        @pl.when(jnp.logical_and(k > 0, k < nk - 1))
        def _():
            acc[...] += dot()

        @pl.when(k == nk - 1)
        def _():
            finish(acc[...] + dot())

    (out,), got = _call(
        body, name=name, out_shape=[jax.ShapeDtypeStruct((M, N), out_dtype)],
        grid=(M // tm, N // tn, nk), in_specs=in_specs, out_specs=[pl.BlockSpec((tm, tn), lambda i, j, k: (i, j))],
        scratch_shapes=[pltpu.VMEM((tm, tn), F32)] if nk > 1 else [],
        sem=("parallel", "parallel", "arbitrary"), args=args, side=side)
    return out if side is None else (out, got)


def _rowwise(fn, rows, consts, out_rows, out_accs, *, R, name, tm=256, rs=16, cs=None):
    tm = min(tm, R)
    assert R % tm == 0 and tm % rs == 0
    n_r, n_c, n_o, n_a = len(rows), len(consts), len(out_rows), len(out_accs)
    nsteps = R // tm
    widths = [w for _, _, w in rows]
    if cs is not None:
        assert all(w == widths[0] for w in widths) and widths[0] % cs == 0
        col_chunks = [(c0, cs) for c0 in range(0, widths[0], cs)]
    else:
        col_chunks = [None]

    def body(*refs):
        r_refs = refs[:n_r]
        c_refs = refs[n_r:n_r + n_c]
        o_refs = refs[n_r + n_c:n_r + n_c + n_o]
        a_refs = refs[n_r + n_c + n_o:n_r + n_c + n_o + n_a]
        s_refs = refs[n_r + n_c + n_o + n_a:]
        i = pl.program_id(0)

        @pl.when(i == 0)
        def _():
            for s in s_refs:
                s[...] = jnp.zeros_like(s)

        def piece(g, carry):
            r0 = pl.multiple_of(g * rs, rs)
            for cc in col_chunks:
                csl = slice(None) if cc is None else slice(cc[0], cc[0] + cc[1])
                rp = [r[pl.ds(r0, rs), csl].astype(F32) for r in r_refs]
                cp = [c[:, csl] for c in c_refs]
                outs, accs = fn(rp, cp)
                for o_ref, o in zip(o_refs, outs):
                    o_ref[pl.ds(r0, rs), csl] = o.astype(o_ref.dtype)
                for s_ref, av in zip(s_refs, accs):
                    s_ref[:, csl] += av
            return carry

        lax.fori_loop(0, tm // rs, piece, 0)

        @pl.when(i == nsteps - 1)
        def _():
            for a_ref, s_ref in zip(a_refs, s_refs):
                a_ref[...] = jnp.sum(s_ref[...], axis=0, keepdims=True)

    in_specs = [pl.BlockSpec((tm, w), functools.partial(lambda i, cb: (i, cb), cb=off // w)) for _, off, w in rows]
    for _, off, w in rows:
        assert off % w == 0
    in_specs += [pl.BlockSpec(c.shape, lambda i: (0, 0)) for c in consts]
    out_specs = [pl.BlockSpec((tm, w), lambda i: (i, 0)) for w, _ in out_rows]
    out_specs += [pl.BlockSpec((1, w), lambda i: (0, 0)) for w in out_accs]
    out_shape = [jax.ShapeDtypeStruct((R, w), dt) for w, dt in out_rows]
    out_shape += [jax.ShapeDtypeStruct((1, w), F32) for w in out_accs]
    return pl.pallas_call(
        body, name=name, out_shape=tuple(out_shape), grid=(nsteps,), in_specs=in_specs, out_specs=tuple(out_specs),
        scratch_shapes=[pltpu.VMEM((rs, w), F32) for w in out_accs],
        compiler_params=_cparams(("arbitrary",)),
    )(*[r for r, _, _ in rows], *consts)


def _rms(x, g):
    r = lax.rsqrt(jnp.mean(x * x, axis=-1, keepdims=True) + NORM_EPS)
    return x * r * g


def _rms_fwd(x, g, name):
    R = x.shape[0]

    def fn(rp, cp):
        return [_rms(rp[0], cp[0])], []

    return _rowwise(fn, [(x, 0, D_MODEL)], [g], [(D_MODEL, BF16)], [], R=R, name=name)[0]


def _rms_bwd(x, g, dh, dres, name):
    R = x.shape[0]

    def fn(rp, cp):
        xv, dhv, drv = rp
        _, pull = jax.vjp(_rms, xv, jnp.broadcast_to(cp[0], xv.shape))
        dx, dgv = pull(dhv)
        return [drv + dx], [dgv]

    dx, dg = _rowwise(fn, [(x, 0, D_MODEL), (dh, 0, D_MODEL), (dres, 0, D_MODEL)], [g], [(D_MODEL, F32)], [D_MODEL],
                      R=R, name=name, rs=8)
    return dx, dg


def _final_loss(x, g, tgt, name):
    R = x.shape[0]

    def loss_rows(xv, gv, tv):
        e = _rms(xv, gv) - tv
        return 0.5 * jnp.mean(e * e, axis=-1, keepdims=True)

    def fn(rp, cp):
        xv, tv = rp
        lr, pull = jax.vjp(lambda a, b: loss_rows(a, b, tv), xv, jnp.broadcast_to(cp[0], xv.shape))
        dx, dgv = pull(jnp.ones_like(lr))
        return [dx], [dgv, jnp.broadcast_to(lr, (lr.shape[0], 128))]

    dx, dg, lsum = _rowwise(fn, [(x, 0, D_MODEL), (tgt, 0, D_MODEL)], [g], [(D_MODEL, F32)], [D_MODEL, 128],
                            R=R, name=name, rs=8)
    return lsum[0, 0], dx, dg


def _neg_expm1(z):
    u = jnp.exp(z)
    safe = jnp.where(u == 1.0, 0.5, u)
    return -jnp.where(u == 1.0, z, (safe - 1.0) * z / jnp.log(safe))


def _lru_ab(xc, pr, pi, lam):
    r = jax.nn.sigmoid(pr)
    i = jax.nn.sigmoid(pi)
    log_a = -LRU_C * r * jax.nn.softplus(-lam)
    a = jnp.exp(log_a)
    b = jnp.sqrt(_neg_expm1(2.0 * log_a)) * (i * xc)
    return a, b


def _gated(h, gate):
    return h * jax.nn.gelu(gate)


def _row_iota8(w):
    return lax.broadcasted_iota(jnp.int32, (8, w), 0)


def _shift_dn(x, halo, s):
    xs = pltpu.roll(x, s, 0)
    hs = pltpu.roll(halo, s, 0)
    first = jnp.where(_row_iota8(x.shape[1]) < s, hs, xs[0:8])
    return jnp.concatenate([first, xs[8:]], axis=0) if x.shape[0] > 8 else first


def _shift_up(x, nxt, s):
    n = x.shape[0]
    xs = pltpu.roll(x, n - s, 0)
    ns = pltpu.roll(nxt, 8 - s, 0)
    last = jnp.where(_row_iota8(x.shape[1]) >= 8 - s, ns, xs[n - 8:])
    return jnp.concatenate([xs[:n - 8], last], axis=0) if n > 8 else last


def _lru_tiles(T):
    tT = min(256, T)
    return tT, T // tT


def _lru_fwd(proj, cw, cb, wax, bax, lam, name):
    T = proj.shape[0]
    W = MIX_W
    tT, nT = _lru_tiles(T)

    def body(x_ref, xh_ref, gt_ref, cw_ref, cb_ref, wax_ref, bax_ref, lam_ref, y_ref, h_ref, a_s, b_s, hc_s):
        i = pl.program_id(0)

        @pl.when(i == 0)
        def _():
            hc_s[...] = jnp.zeros_like(hc_s)

        x = x_ref[...]
        halo = jnp.where(i > 0, xh_ref[...], 0.0)
        w = cw_ref[...]
        xc = (cb_ref[...] + w[3:4] * x + w[2:3] * _shift_dn(x, halo, 1) + w[1:2] * _shift_dn(x, halo, 2)
              + w[0:1] * _shift_dn(x, halo, 3))
        pre = jnp.dot(xc.astype(BF16), wax_ref[...], preferred_element_type=F32) + bax_ref[...]
        a, b = _lru_ab(xc, pre[:, :W], pre[:, W:], lam_ref[...])
        a_s[...] = a
        b_s[...] = b
        row = _row_iota8(W)

        def grp(gi, hprev):
            r0 = pl.multiple_of(gi * 8, 8)
            A = a_s[pl.ds(r0, 8), :]
            B = b_s[pl.ds(r0, 8), :]
            for s in (1, 2, 4):
                As = pltpu.roll(A, s, 0)
                Bs = pltpu.roll(B, s, 0)
                m = row >= s
                B = jnp.where(m, A * Bs + B, B)
                A = jnp.where(m, A * As, A)
            H = A * hprev + B
            h_ref[pl.ds(r0, 8), :] = H
            return H[7:8, :]

        hc_s[0:1, :] = lax.fori_loop(0, tT // 8, grp, hc_s[0:1, :])
        y_ref[...] = _gated(h_ref[...], gt_ref[...]).astype(BF16)

    hb = tT // 8
    return pl.pallas_call(
        body, name=name,
        out_shape=(jax.ShapeDtypeStruct((T, W), BF16), jax.ShapeDtypeStruct((T, W), F32)),
        grid=(nT,),
        in_specs=[pl.BlockSpec((tT, W), lambda i: (i, 0)),
                  pl.BlockSpec((8, W), lambda i: (jnp.maximum(i * hb - 1, 0), 0)),
                  pl.BlockSpec((tT, W), lambda i: (i, 1)),
                  pl.BlockSpec((CONV_W, W), lambda i: (0, 0)), pl.BlockSpec((1, W), lambda i: (0, 0)),
                  pl.BlockSpec((W, 2 * W), lambda i: (0, 0)), pl.BlockSpec((1, 2 * W), lambda i: (0, 0)),
                  pl.BlockSpec((1, W), lambda i: (0, 0))],
        out_specs=(pl.BlockSpec((tT, W), lambda i: (i, 0)), pl.BlockSpec((tT, W), lambda i: (i, 0))),
        scratch_shapes=[pltpu.VMEM((tT, W), F32), pltpu.VMEM((tT, W), F32), pltpu.VMEM((8, W), F32)],
        compiler_params=_cparams(("arbitrary",)),
    )(proj, proj, proj, cw, cb, wax, bax, lam)


def _lru_bwd(proj, h, dy, cw, cb, wax, bax, lam, name):
    T = proj.shape[0]
    W = MIX_W
    tT, nT = _lru_tiles(T)
    hb = tT // 8

    def body(x_ref, xh_ref, gt_ref, h_ref, hh_ref, dy_ref, cw_ref, cb_ref, wax_ref, bax_ref, lam_ref,
             dx_ref, dgt_ref, dcw_ref, dcb_ref, dwax_ref, dbax_ref, dlam_ref,
             al_s, be_s, d_s, ca_s, cd_s, cx_s):
        i = pl.program_id(0)
        ib = nT - 1 - i

        @pl.when(i == 0)
        def _():
            for r in (ca_s, cd_s, cx_s, dcw_ref, dcb_ref, dwax_ref, dbax_ref, dlam_ref):
                r[...] = jnp.zeros_like(r)

        x = x_ref[...]
        halo = jnp.where(ib > 0, xh_ref[...], 0.0)
        w = cw_ref[...]
        x1, x2, x3 = _shift_dn(x, halo, 1), _shift_dn(x, halo, 2), _shift_dn(x, halo, 3)
        xc = cb_ref[...] + w[3:4] * x + w[2:3] * x1 + w[1:2] * x2 + w[0:1] * x3
        xcb = xc.astype(BF16)
        pre = jnp.dot(xcb, wax_ref[...], preferred_element_type=F32) + bax_ref[...]
        (a, _), pull_ab = jax.vjp(_lru_ab, xc, pre[:, :W], pre[:, W:], lam_ref[...])
        hv = h_ref[...]
        hprev = _shift_dn(hv, jnp.where(ib > 0, hh_ref[...], 0.0), 1)
        _, pull_y = jax.vjp(_gated, hv, gt_ref[...])
        dh_out, dgt = pull_y(dy_ref[...])
        dgt_ref[...] = dgt.astype(BF16)
        al_s[...] = _shift_up(a, ca_s[...], 1)
        be_s[...] = dh_out
        row = _row_iota8(W)
        ng = tT // 8

        def grp(k, dnext):
            r0 = pl.multiple_of((ng - 1 - k) * 8, 8)
            A = al_s[pl.ds(r0, 8), :]
            B = be_s[pl.ds(r0, 8), :]
            for s in (1, 2, 4):
                As = pltpu.roll(A, 8 - s, 0)
                Bs = pltpu.roll(B, 8 - s, 0)
                m = row < 8 - s
                B = jnp.where(m, A * Bs + B, B)
                A = jnp.where(m, A * As, A)
            Dg = A * dnext + B
            d_s[pl.ds(r0, 8), :] = Dg
            return Dg[0:1, :]

        lax.fori_loop(0, ng, grp, cd_s[0:1, :])
        Dv = d_s[...]
        dxc1, dpr, dpi, dlam = pull_ab((Dv * hprev, Dv))
        dpre = jnp.concatenate([dpr, dpi], axis=1)
        dpb = dpre.astype(BF16)
        dxc = dxc1 + lax.dot_general(dpb, wax_ref[...], (((1,), (1,)), ((), ())), preferred_element_type=F32)
        dwax_ref[...] += lax.dot_general(xcb, dpb, (((0,), (0,)), ((), ())), preferred_element_type=F32)
        dbax_ref[...] += jnp.sum(dpre, axis=0, keepdims=True)
        dlam_ref[...] += dlam
        dcb_ref[...] += jnp.sum(dxc, axis=0, keepdims=True)
        dcw_ref[...] += jnp.concatenate([jnp.sum(dxc * x3, axis=0, keepdims=True), jnp.sum(dxc * x2, axis=0, keepdims=True),
                                         jnp.sum(dxc * x1, axis=0, keepdims=True), jnp.sum(dxc * x, axis=0, keepdims=True)], axis=0)
        nxt = cx_s[...]
        dx = (w[3:4] * dxc + w[2:3] * _shift_up(dxc, nxt, 1) + w[1:2] * _shift_up(dxc, nxt, 2)
              + w[0:1] * _shift_up(dxc, nxt, 3))
        dx_ref[...] = dx.astype(BF16)
        ca_s[...] = a[0:8]
        cd_s[...] = Dv[0:8]
        cx_s[...] = dxc[0:8]

    rev = lambda i: nT - 1 - i
    const = lambda shape: pl.BlockSpec(shape, lambda i: (0, 0))
    return pl.pallas_call(
        body, name=name,
        out_shape=(jax.ShapeDtypeStruct((T, W), BF16), jax.ShapeDtypeStruct((T, W), BF16),
                   jax.ShapeDtypeStruct((CONV_W, W), F32), jax.ShapeDtypeStruct((1, W), F32),
                   jax.ShapeDtypeStruct((W, 2 * W), F32), jax.ShapeDtypeStruct((1, 2 * W), F32),
                   jax.ShapeDtypeStruct((1, W), F32)),
        grid=(nT,),
        in_specs=[pl.BlockSpec((tT, W), lambda i: (rev(i), 0)),
                  pl.BlockSpec((8, W), lambda i: (jnp.maximum(rev(i) * hb - 1, 0), 0)),
                  pl.BlockSpec((tT, W), lambda i: (rev(i), 1)),
                  pl.BlockSpec((tT, W), lambda i: (rev(i), 0)),
                  pl.BlockSpec((8, W), lambda i: (jnp.maximum(rev(i) * hb - 1, 0), 0)),
                  pl.BlockSpec((tT, W), lambda i: (rev(i), 0)),
                  const((CONV_W, W)), const((1, W)), const((W, 2 * W)), const((1, 2 * W)), const((1, W))],
        out_specs=(pl.BlockSpec((tT, W), lambda i: (rev(i), 0)), pl.BlockSpec((tT, W), lambda i: (rev(i), 0)),
                   const((CONV_W, W)), const((1, W)), const((W, 2 * W)), const((1, 2 * W)), const((1, W))),
        scratch_shapes=[pltpu.VMEM((tT, W), F32), pltpu.VMEM((tT, W), F32), pltpu.VMEM((tT, W), F32),
                        pltpu.VMEM((8, W), F32), pltpu.VMEM((8, W), F32), pltpu.VMEM((8, W), F32)],
        compiler_params=_cparams(("arbitrary",)),
    )(proj, proj, proj, h, h, dy, cw, cb, wax, bax, lam)


def _lru_prep(wa, wx, ba, bx):
    eye = jnp.eye(LRU_BLOCKS, dtype=F32)

    def dense(wb):
        return (wb[:, :, None, :] * eye[:, None, :, None]).reshape(MIX_W, MIX_W)

    wax = jnp.concatenate([dense(wa), dense(wx)], axis=1)
    bax = jnp.concatenate([ba, bx])[None, :]
    return wax, bax


_ATT_QC = 4
_ATT_Q = _ATT_QC * CHUNK
_ATT_KW = (ATT_LEFT + _ATT_QC) * CHUNK
_BVEC_W = _ATT_KW
_N_OFFS = CHUNK - 1 + ATT_BAND


def _attn_bias(rel_bias):
    n_far = ATT_LEFT * CHUNK - MAX_REL + CHUNK
    far = jnp.broadcast_to(rel_bias[:, 2 * MAX_REL:], (ATT_HEADS, n_far))
    near = rel_bias[:, MAX_REL - (CHUNK - 1):2 * MAX_REL][:, ::-1]
    pad = jnp.zeros((ATT_HEADS, _BVEC_W - _N_OFFS), F32)
    return jnp.concatenate([far, near, pad], axis=1)[:, None, :]


def _bias_table(bvec_row):
    return pltpu.roll(jnp.broadcast_to(bvec_row, (_ATT_Q, _BVEC_W)), _BVEC_W - (CHUNK - 1), 1, stride=1, stride_axis=0)


def _bias_table_t(ds):
    r = lax.broadcasted_iota(jnp.int32, (_ATT_Q, _ATT_Q), 0)
    c = lax.broadcasted_iota(jnp.int32, (_ATT_Q, _ATT_Q), 1)
    rev = jnp.dot((r + c == _ATT_Q - 1).astype(F32), ds, preferred_element_type=F32, precision=HI)
    back = pltpu.roll(rev, _BVEC_W - (_ATT_Q - CHUNK), 1, stride=1, stride_axis=0)
    return jnp.sum(back, axis=0, keepdims=True)


_NT = (((1,), (1,)), ((), ()))
_TN = (((0,), (0,)), ((), ()))
_ATT_PAD = ATT_LEFT * CHUNK
_Q_BLK, _K_BLK, _V_BLK = 2 * MIX_W // ATT_HD, 3 * MIX_W // ATT_HD, 4 * MIX_W // ATT_HD


def _in_band():
    first = (lax.broadcasted_iota(jnp.int32, (_ATT_Q, _ATT_KW), 0) // CHUNK) * CHUNK
    k = lax.broadcasted_iota(jnp.int32, (_ATT_Q, _ATT_KW), 1)
    return jnp.logical_and(k >= first, k < first + ATT_BAND)


def _attn_probs(q, kb, bias, in_band, b):
    s = lax.dot_general(q, kb, _NT, preferred_element_type=F32) * (ATT_HD ** -0.5) + bias
    kpos = lax.broadcasted_iota(jnp.int32, s.shape, 1)
    s = jnp.where(jnp.logical_and(in_band, kpos >= _ATT_PAD - b * _ATT_Q), s, MASK_VALUE)
    e = jnp.exp(s - jnp.max(s, axis=-1, keepdims=True))
    return e / jnp.sum(e, axis=-1, keepdims=True)


def _attn_fwd(proj, bias, name, side=None):
    T = proj.shape[0]
    assert T % _ATT_Q == 0
    nB = T // _ATT_Q

    def body(q_ref, k_ref, v_ref, b_ref, o_ref, kp, vp):
        kp[0:_ATT_PAD, :] = jnp.zeros((_ATT_PAD, ATT_HD), BF16)
        vp[0:_ATT_PAD, :] = jnp.zeros((_ATT_PAD, ATT_HD), BF16)
        kp[_ATT_PAD:, :] = k_ref[...].astype(BF16)
        vp[_ATT_PAD:, :] = v_ref[...].astype(BF16)
        bias_v = _bias_table(b_ref[0])
        band = _in_band()

        def step(b, carry):
            r0 = pl.multiple_of(b * _ATT_Q, _ATT_Q)
            q = q_ref[pl.ds(r0, _ATT_Q), :].astype(BF16)
            p = _attn_probs(q, kp[pl.ds(r0, _ATT_KW), :], bias_v, band, b)
            o = jnp.dot(p.astype(BF16), vp[pl.ds(r0, _ATT_KW), :], preferred_element_type=F32)
            o_ref[pl.ds(r0, _ATT_Q), :] = o.astype(BF16)
            return carry

        lax.fori_loop(0, nB, step, 0)

    (out,), got = _call(
        body, name=name, out_shape=[jax.ShapeDtypeStruct((T, MIX_W), BF16)], grid=(ATT_HEADS,),
        in_specs=[pl.BlockSpec((T, ATT_HD), lambda h: (0, _Q_BLK + h)), pl.BlockSpec((T, ATT_HD), lambda h: (0, _K_BLK + h)),
                  pl.BlockSpec((T, ATT_HD), lambda h: (0, _V_BLK + h)), pl.BlockSpec((1, 1, _BVEC_W), lambda h: (h, 0, 0))],
        out_specs=[pl.BlockSpec((T, ATT_HD), lambda h: (0, h))],
        scratch_shapes=[pltpu.VMEM((T + _ATT_PAD, ATT_HD), BF16), pltpu.VMEM((T + _ATT_PAD, ATT_HD), BF16)],
        sem=("arbitrary",), args=(proj, proj, proj, bias), side=side)
    return out if side is None else (out, got)


def _attn_bwd(proj, bias, do, name, side=None):
    T = proj.shape[0]
    assert T % _ATT_Q == 0
    nB = T // _ATT_Q

    def body(q_ref, k_ref, v_ref, b_ref, do_ref, dq_ref, dk_ref, dv_ref, db_ref, kp, vp, dkp, dvp, dbs):
        kp[0:_ATT_PAD, :] = jnp.zeros((_ATT_PAD, ATT_HD), BF16)
        vp[0:_ATT_PAD, :] = jnp.zeros((_ATT_PAD, ATT_HD), BF16)
        kp[_ATT_PAD:, :] = k_ref[...].astype(BF16)
        vp[_ATT_PAD:, :] = v_ref[...].astype(BF16)
        dkp[...] = jnp.zeros_like(dkp)
        dvp[...] = jnp.zeros_like(dvp)
        dbs[...] = jnp.zeros_like(dbs)
        bias_v = _bias_table(b_ref[0])
        band = _in_band()

        def step(b, carry):
            r0 = pl.multiple_of(b * _ATT_Q, _ATT_Q)
            q = q_ref[pl.ds(r0, _ATT_Q), :].astype(BF16)
            kb = kp[pl.ds(r0, _ATT_KW), :]
            vb = vp[pl.ds(r0, _ATT_KW), :]
            dob = do_ref[pl.ds(r0, _ATT_Q), :].astype(BF16)
            p = _attn_probs(q, kb, bias_v, band, b)
            dp = lax.dot_general(dob, vb, _NT, preferred_element_type=F32)
            ds = p * (dp - jnp.sum(p * dp, axis=-1, keepdims=True))
            dbs[...] += ds
            dsb = (ds * (ATT_HD ** -0.5)).astype(BF16)
            dq_ref[pl.ds(r0, _ATT_Q), :] = jnp.dot(dsb, kb, preferred_element_type=F32).astype(BF16)
            dkp[pl.ds(r0, _ATT_KW), :] += lax.dot_general(dsb, q, _TN, preferred_element_type=F32)
            dvp[pl.ds(r0, _ATT_KW), :] += lax.dot_general(p.astype(BF16), dob, _TN, preferred_element_type=F32)
            return carry

        lax.fori_loop(0, nB, step, 0)
        dk_ref[...] = dkp[_ATT_PAD:, :].astype(BF16)
        dv_ref[...] = dvp[_ATT_PAD:, :].astype(BF16)
        db_ref[0] = _bias_table_t(dbs[...])

    hspec = pl.BlockSpec((T, ATT_HD), lambda h: (0, h))
    osd = jax.ShapeDtypeStruct((T, MIX_W), BF16)
    outs, got = _call(
        body, name=name,
        out_shape=[osd, osd, osd, jax.ShapeDtypeStruct((ATT_HEADS, 1, _BVEC_W), F32)], grid=(ATT_HEADS,),
        in_specs=[pl.BlockSpec((T, ATT_HD), lambda h: (0, _Q_BLK + h)), pl.BlockSpec((T, ATT_HD), lambda h: (0, _K_BLK + h)),
                  pl.BlockSpec((T, ATT_HD), lambda h: (0, _V_BLK + h)), pl.BlockSpec((1, 1, _BVEC_W), lambda h: (h, 0, 0)),
                  hspec],
        out_specs=[hspec, hspec, hspec, pl.BlockSpec((1, 1, _BVEC_W), lambda h: (h, 0, 0))],
        scratch_shapes=[pltpu.VMEM((T + _ATT_PAD, ATT_HD), BF16), pltpu.VMEM((T + _ATT_PAD, ATT_HD), BF16),
                        pltpu.VMEM((T + _ATT_PAD, ATT_HD), F32), pltpu.VMEM((T + _ATT_PAD, ATT_HD), F32),
                        pltpu.VMEM((_ATT_Q, _ATT_KW), F32)],
        sem=("arbitrary",), args=(proj, proj, proj, bias, do), side=side)
    return outs if side is None else (outs, got)


def _s5_prep(a_re, a_im, b_re, b_im, c_re, c_im, d, log_step):
    step = jnp.exp(log_step)[:, None]
    mag = jnp.exp(a_re * step)
    ang = a_im * step
    lb_re = mag * jnp.cos(ang)
    lb_im = mag * jnp.sin(ang)
    den = a_re * a_re + a_im * a_im
    nr = lb_re - 1.0
    coef_re = (nr * a_re + lb_im * a_im) / den
    coef_im = (lb_im * a_re - nr * a_im) / den
    bb_re = coef_re[..., None] * b_re - coef_im[..., None] * b_im
    bb_im = coef_re[..., None] * b_im + coef_im[..., None] * b_re
    prs, pis = [jnp.ones_like(lb_re)], [jnp.zeros_like(lb_re)]
    for _ in range(S5_L):
        prs.append(prs[-1] * lb_re - pis[-1] * lb_im)
        pis.append(prs[-2] * lb_im + pis[-1] * lb_re)
    PR, PI = jnp.stack(prs), jnp.stack(pis)
    cl_re = c_re[None] * PR[:, :, None, :] - c_im[None] * PI[:, :, None, :]
    cl_im = c_re[None] * PI[:, :, None, :] + c_im[None] * PR[:, :, None, :]
    cla_re = cl_re[:S5_L].transpose(1, 0, 2, 3).reshape(SSM_G, S5_LW, SSM_P)
    cla_im = cl_im[:S5_L].transpose(1, 0, 2, 3).reshape(SSM_G, S5_LW, SSM_P)
    bbt_re, bbt_im = bb_re.transpose(0, 2, 1), bb_im.transpose(0, 2, 1)
    prr, pir = PR[:S5_L][::-1], PI[:S5_L][::-1]
    bret = (prr[:, :, None, :] * bb_re.transpose(0, 2, 1)[None] - pir[:, :, None, :] * bb_im.transpose(0, 2, 1)[None])
    bimt = (prr[:, :, None, :] * bb_im.transpose(0, 2, 1)[None] + pir[:, :, None, :] * bb_re.transpose(0, 2, 1)[None])
    bret = bret.transpose(1, 0, 2, 3).reshape(SSM_G, S5_LW, SSM_P)
    bimt = bimt.transpose(1, 0, 2, 3).reshape(SSM_G, S5_LW, SSM_P)
    cre = cl_re[1:].transpose(1, 3, 0, 2).reshape(SSM_G, SSM_P, S5_LW)
    cim = (-cl_im[1:]).transpose(1, 3, 0, 2).reshape(SSM_G, SSM_P, S5_LW)
    dflat = jnp.broadcast_to(d.reshape(SSM_G, 1, SSM_H), (SSM_G, S5_L, SSM_H)).reshape(SSM_G, 1, S5_LW)
    return cla_re, cla_im, bbt_re, bbt_im, bret, bimt, cre, cim, PR[S5_L], PI[S5_L], dflat


_S5_GB = 8


def _bdot(a, b, dims):
    return lax.dot_general(a, b, dims, preferred_element_type=F32, precision=HI)


_B_NN = (((2,), (1,)), ((0,), (0,)))
_B_NT = (((2,), (2,)), ((0,), (0,)))
_B_TN = (((1,), (1,)), ((0,), (0,)))


def _gspec(shape):
    return pl.BlockSpec((_S5_GB,) + shape, lambda g: (g, 0, 0))


def _s5_tgt(cla_re, cla_im, bbt_re, bbt_im, name):
    n_g = cla_re.shape[0]

    def body(cr_ref, ci_ref, br_ref, bi_ref, t_ref):
        kt = _bdot(br_ref[...], cr_ref[...], _B_NT) - _bdot(bi_ref[...], ci_ref[...], _B_NT)
        lane = lax.broadcasted_iota(jnp.int32, (SSM_H, S5_LW), 1)
        for g in range(_S5_GB):
            for lp in range(S5_L):
                rows = kt[g] if lp == 0 else jnp.where(lane >= lp * SSM_H, pltpu.roll(kt[g], lp * SSM_H, 1), 0.0)
                t_ref[g, lp * SSM_H:(lp + 1) * SSM_H, :] = rows

    return pl.pallas_call(
        body, name=name, out_shape=jax.ShapeDtypeStruct((n_g, S5_LW, S5_LW), F32), grid=(n_g // _S5_GB,),
        in_specs=[_gspec((S5_LW, SSM_P)), _gspec((S5_LW, SSM_P)), _gspec((SSM_H, SSM_P)), _gspec((SSM_H, SSM_P))],
        out_specs=_gspec((S5_LW, S5_LW)), compiler_params=_cparams(("parallel",)),
    )(cla_re, cla_im, bbt_re, bbt_im)


def _s5_tgt_t(d_tgt, cla_re, cla_im, bbt_re, bbt_im, name):
    n_g = cla_re.shape[0]

    def body(dt_ref, cr_ref, ci_ref, br_ref, bi_ref, dcr_ref, dci_ref, dbr_ref, dbi_ref):
        lane = lax.broadcasted_iota(jnp.int32, (SSM_H, S5_LW), 1)
        dks = []
        for g in range(_S5_GB):
            dk = dt_ref[g, 0:SSM_H, :]
            for lp in range(1, S5_L):
                rows = dt_ref[g, lp * SSM_H:(lp + 1) * SSM_H, :]
                dk = dk + jnp.where(lane < S5_LW - lp * SSM_H, pltpu.roll(rows, S5_LW - lp * SSM_H, 1), 0.0)
            dks.append(dk)
        dkt = jnp.stack(dks)
        dbr_ref[...] = _bdot(dkt, cr_ref[...], _B_NN)
        dbi_ref[...] = -_bdot(dkt, ci_ref[...], _B_NN)
        dcr_ref[...] = _bdot(dkt, br_ref[...], _B_TN)
        dci_ref[...] = -_bdot(dkt, bi_ref[...], _B_TN)

    sd = jax.ShapeDtypeStruct
    return pl.pallas_call(
        body, name=name,
        out_shape=(sd((n_g, S5_LW, SSM_P), F32), sd((n_g, S5_LW, SSM_P), F32), sd((n_g, SSM_H, SSM_P), F32),
                   sd((n_g, SSM_H, SSM_P), F32)),
        grid=(n_g // _S5_GB,),
        in_specs=[_gspec((S5_LW, S5_LW)), _gspec((S5_LW, SSM_P)), _gspec((S5_LW, SSM_P)), _gspec((SSM_H, SSM_P)),
                  _gspec((SSM_H, SSM_P))],
        out_specs=(_gspec((S5_LW, SSM_P)), _gspec((S5_LW, SSM_P)), _gspec((SSM_H, SSM_P)), _gspec((SSM_H, SSM_P))),
        compiler_params=_cparams(("parallel",)),
    )(d_tgt, cla_re, cla_im, bbt_re, bbt_im)


def _s5_in(u, bret, bimt, name):
    C = u.shape[1]

    def body(u_ref, br_ref, bi_ref, sr_ref, si_ref):
        uv = u_ref[...]
        sr_ref[...] = _bdot(uv, br_ref[...], _B_NN)
        si_ref[...] = _bdot(uv, bi_ref[...], _B_NN)

    sd = jax.ShapeDtypeStruct((SSM_G, C, SSM_P), F32)
    return pl.pallas_call(
        body, name=name, out_shape=(sd, sd), grid=(SSM_G // _S5_GB,),
        in_specs=[_gspec((C, S5_LW)), _gspec((S5_LW, SSM_P)), _gspec((S5_LW, SSM_P))],
        out_specs=(_gspec((C, SSM_P)), _gspec((C, SSM_P))), compiler_params=_cparams(("parallel",)),
    )(u, bret, bimt)


def _s5_scan(sin_re, sin_im, lr, li, name):
    C = sin_re.shape[0]

    def body(ir_ref, ii_ref, lr_ref, li_ref, or_ref, oi_ref):
        lrv, liv = lr_ref[...], li_ref[...]

        def step(c, s):
            sr, si = s
            or_ref[c] = sr
            oi_ref[c] = si
            return lrv * sr - liv * si + ir_ref[c], lrv * si + liv * sr + ii_ref[c]

        z = jnp.zeros((SSM_G, SSM_P), F32)
        lax.fori_loop(0, C, step, (z, z))

    sd = jax.ShapeDtypeStruct((C, SSM_G, SSM_P), F32)
    return pl.pallas_call(body, name=name, out_shape=(sd, sd), compiler_params=_cparams())(sin_re, sin_im, lr, li)


def _s5_out(u, sp_re, sp_im, tgt, cre, cim, dflat, name):
    C = u.shape[1]

    def body(u_ref, sr_ref, si_ref, t_ref, cr_ref, ci_ref, d_ref, pre_ref, y_ref):
        uv = u_ref[...]
        pre = (_bdot(uv, t_ref[...], _B_NN) + _bdot(sr_ref[...], cr_ref[...], _B_NN)
               + _bdot(si_ref[...], ci_ref[...], _B_NN) + d_ref[...] * uv)
        pre_ref[...] = pre
        y_ref[...] = jax.nn.gelu(pre)

    return pl.pallas_call(
        body, name=name,
        out_shape=(jax.ShapeDtypeStruct((SSM_G, C, S5_LW), F32), jax.ShapeDtypeStruct((SSM_G, C, S5_LW), F32)),
        grid=(SSM_G // _S5_GB,),
        in_specs=[_gspec((C, S5_LW)), _gspec((C, SSM_P)), _gspec((C, SSM_P)), _gspec((S5_LW, S5_LW)),
                  _gspec((SSM_P, S5_LW)), _gspec((SSM_P, S5_LW)), _gspec((1, S5_LW))],
        out_specs=(_gspec((C, S5_LW)), _gspec((C, S5_LW))), compiler_params=_cparams(("parallel",)),
    )(u, sp_re, sp_im, tgt, cre, cim, dflat)


def _s5_bwd_out(dy, pre, u, sp_re, sp_im, cre, cim, name):
    C = u.shape[1]

    def body(dy_ref, pre_ref, u_ref, sr_ref, si_ref, cr_ref, ci_ref,
             dpre_ref, dsr_ref, dsi_ref, dt_ref, dcr_ref, dci_ref, dd_ref):
        _, pull = jax.vjp(jax.nn.gelu, pre_ref[...])
        dpre = pull(dy_ref[...])[0]
        uv = u_ref[...]
        dpre_ref[...] = dpre
        dsr_ref[...] = _bdot(dpre, cr_ref[...], _B_NT)
        dsi_ref[...] = _bdot(dpre, ci_ref[...], _B_NT)
        dt_ref[...] = _bdot(uv, dpre, _B_TN)
        dcr_ref[...] = _bdot(sr_ref[...], dpre, _B_TN)
        dci_ref[...] = _bdot(si_ref[...], dpre, _B_TN)
        dd_ref[...] = jnp.sum(dpre * uv, axis=1, keepdims=True)

    sd = jax.ShapeDtypeStruct
    return pl.pallas_call(
        body, name=name,
        out_shape=(sd((SSM_G, C, S5_LW), F32), sd((SSM_G, C, SSM_P), F32), sd((SSM_G, C, SSM_P), F32),
                   sd((SSM_G, S5_LW, S5_LW), F32), sd((SSM_G, SSM_P, S5_LW), F32), sd((SSM_G, SSM_P, S5_LW), F32),
                   sd((SSM_G, 1, S5_LW), F32)),
        grid=(SSM_G // _S5_GB,),
        in_specs=[_gspec((C, S5_LW)), _gspec((C, S5_LW)), _gspec((C, S5_LW)), _gspec((C, SSM_P)), _gspec((C, SSM_P)),
                  _gspec((SSM_P, S5_LW)), _gspec((SSM_P, S5_LW))],
        out_specs=(_gspec((C, S5_LW)), _gspec((C, SSM_P)), _gspec((C, SSM_P)), _gspec((S5_LW, S5_LW)),
                   _gspec((SSM_P, S5_LW)), _gspec((SSM_P, S5_LW)), _gspec((1, S5_LW))),
        compiler_params=_cparams(("parallel",)),
    )(dy, pre, u, sp_re, sp_im, cre, cim)


def _s5_rscan(dsp_re, dsp_im, sp_re, sp_im, lr, li, name):
    C = dsp_re.shape[0]

    def body(gr_ref, gi_ref, sr_ref, si_ref, lr_ref, li_ref, or_ref, oi_ref, dlr_ref, dli_ref):
        lrv, liv = lr_ref[...], li_ref[...]

        def step(k, carry):
            c = C - 1 - k
            dr, di, alr, ali = carry
            or_ref[c] = dr
            oi_ref[c] = di
            sr, si = sr_ref[c], si_ref[c]
            alr = alr + dr * sr + di * si
            ali = ali + di * sr - dr * si
            return gr_ref[c] + lrv * dr + liv * di, gi_ref[c] + lrv * di - liv * dr, alr, ali

        z = jnp.zeros((SSM_G, SSM_P), F32)
        _, _, alr, ali = lax.fori_loop(0, C, step, (z, z, z, z))
        dlr_ref[...] = alr
        dli_ref[...] = ali

    sd = jax.ShapeDtypeStruct((C, SSM_G, SSM_P), F32)
    sp = jax.ShapeDtypeStruct((SSM_G, SSM_P), F32)
    return pl.pallas_call(body, name=name, out_shape=(sd, sd, sp, sp), compiler_params=_cparams())(
        dsp_re, dsp_im, sp_re, sp_im, lr, li)


def _s5_bwd_in(dpre, dsin_re, dsin_im, u, tgt, bret, bimt, dflat, name):
    C = u.shape[1]

    def body(dp_ref, dr_ref, di_ref, u_ref, t_ref, br_ref, bi_ref, d_ref, du_ref, dbr_ref, dbi_ref):
        dp = dp_ref[...]
        dr, di, uv = dr_ref[...], di_ref[...], u_ref[...]
        du = (_bdot(dp, t_ref[...], _B_NT) + _bdot(dr, br_ref[...], _B_NT) + _bdot(di, bi_ref[...], _B_NT)
              + d_ref[...] * dp)
        du_ref[...] = du
        dbr_ref[...] = _bdot(uv, dr, _B_TN)
        dbi_ref[...] = _bdot(uv, di, _B_TN)

    sd = jax.ShapeDtypeStruct
    return pl.pallas_call(
        body, name=name,
        out_shape=(sd((SSM_G, C, S5_LW), F32), sd((SSM_G, S5_LW, SSM_P), F32), sd((SSM_G, S5_LW, SSM_P), F32)),
        grid=(SSM_G // _S5_GB,),
        in_specs=[_gspec((C, S5_LW)), _gspec((C, SSM_P)), _gspec((C, SSM_P)), _gspec((C, S5_LW)),
                  _gspec((S5_LW, S5_LW)), _gspec((S5_LW, SSM_P)), _gspec((S5_LW, SSM_P)), _gspec((1, S5_LW))],
        out_specs=(_gspec((C, S5_LW)), _gspec((S5_LW, SSM_P)), _gspec((S5_LW, SSM_P))),
        compiler_params=_cparams(("parallel",)),
    )(dpre, dsin_re, dsin_im, u, tgt, bret, bimt, dflat)


_LANES = 128
_GROUPS_PER_TILE = _LANES // SSM_H


def _to_chunks(v, col0, name):
    T = v.shape[0]
    C = T // S5_L

    def body(x_ref, o_ref):
        for l in range(S5_L):
            xl = x_ref[pl.ds(l, C, stride=S5_L), :]
            for k in range(_GROUPS_PER_TILE):
                o_ref[k, :, l * SSM_H:(l + 1) * SSM_H] = xl[:, k * SSM_H:(k + 1) * SSM_H]

    return pl.pallas_call(
        body, name=name, out_shape=jax.ShapeDtypeStruct((SSM_G, C, S5_LW), F32), grid=(SSM_G // _GROUPS_PER_TILE,),
        in_specs=[pl.BlockSpec((T, _LANES), lambda g: (0, col0 // _LANES + g))],
        out_specs=pl.BlockSpec((_GROUPS_PER_TILE, C, S5_LW), lambda g: (g, 0, 0)),
        compiler_params=_cparams(("parallel",)),
    )(v)


def _from_chunks(v, out_dtype, name):
    C = v.shape[1]
    T = C * S5_L

    def body(f_ref, o_ref, rows):
        for l in range(S5_L):
            rows[pl.ds(l, C, stride=S5_L), :] = jnp.concatenate(
                [f_ref[k, :, l * SSM_H:(l + 1) * SSM_H] for k in range(_GROUPS_PER_TILE)], axis=1)
        o_ref[...] = rows[...].astype(out_dtype)

    return pl.pallas_call(
        body, name=name, out_shape=jax.ShapeDtypeStruct((T, MIX_W), out_dtype), grid=(SSM_G // _GROUPS_PER_TILE,),
        in_specs=[pl.BlockSpec((_GROUPS_PER_TILE, C, S5_LW), lambda g: (g, 0, 0))],
        out_specs=pl.BlockSpec((T, _LANES), lambda g: (0, g)),
        scratch_shapes=[pltpu.VMEM((T, _LANES), F32)],
        compiler_params=_cparams(("parallel",)),
    )(v)


def _merge_fn(bra, brb, pc, pg, g0, g1, g2, b0, b1, b2):
    sg = jax.nn.sigmoid
    return sg(g0 + b0) * bra + sg(g1 + b1) * brb + sg(g2 + b2) * (pc * sg(pg))


_EW_CS = 256
_GATE_OFF = 6 * MIX_W


def _merge_rows(br4, proj):
    return [(b, 0, D_MODEL) for b in br4] + [(proj, _GATE_OFF + k * D_MODEL, D_MODEL) for k in range(3)]


def _merge_fwd(br4, proj, gb3, name):
    def fn(rp, cp):
        b = cp[0]
        return [_merge_fn(*rp, b[0:1], b[1:2], b[2:3])], []

    return _rowwise(fn, _merge_rows(br4, proj), [gb3], [(D_MODEL, BF16)], [], R=proj.shape[0], name=name, cs=_EW_CS)[0]


def _merge_bwd(br4, proj, gb3, dm, name):
    def fn(rp, cp):
        b = cp[0]
        shp = rp[0].shape
        bs = [jnp.broadcast_to(b[k:k + 1], shp) for k in range(3)]
        _, pull = jax.vjp(_merge_fn, *rp[:7], *bs)
        g = pull(rp[7])
        return list(g[:7]), list(g[7:])

    rows = _merge_rows(br4, proj) + [(dm, 0, D_MODEL)]
    outs = _rowwise(fn, rows, [gb3], [(D_MODEL, BF16)] * 7, [D_MODEL] * 3, R=proj.shape[0], name=name, tm=128, cs=_EW_CS)
    return outs[:4], outs[4:7], jnp.concatenate(outs[7:], axis=0)


def _swiglu(g, u):
    return jax.nn.silu(g) * u


def _act_fwd(gu, name):
    def fn(rp, cp):
        return [_swiglu(*rp)], []

    return _rowwise(fn, [(gu[0], 0, FFN_H), (gu[1], 0, FFN_H)], [], [(FFN_H, BF16)], [], R=gu[0].shape[0], name=name, cs=_EW_CS)[0]


def _act_bwd(gu, dact, name):
    def fn(rp, cp):
        _, pull = jax.vjp(_swiglu, rp[0], rp[1])
        return list(pull(rp[2])), []

    return _rowwise(fn, [(gu[0], 0, FFN_H), (gu[1], 0, FFN_H), (dact, 0, FFN_H)], [], [(FFN_H, BF16)] * 2, [],
                    R=dact.shape[0], name=name, tm=128, cs=_EW_CS)


def _adamw_fn(w, g, m, v):
    m = ADAM_B1 * m + (1.0 - ADAM_B1) * g
    v = ADAM_B2 * v + (1.0 - ADAM_B2) * jnp.square(g)
    m_hat = m / (1.0 - ADAM_B1 ** ADAM_STEP)
    v_hat = v / (1.0 - ADAM_B2 ** ADAM_STEP)
    delta = -ADAM_LR * (m_hat / (jnp.sqrt(v_hat) + ADAM_EPS) + ADAM_WD * w)
    return delta, m, v


def _adamw(w, g, m, v, name):
    R, C = w.shape

    def fn(rp, cp):
        return list(_adamw_fn(*rp)), []

    cs = _pick(C, (512, 256, 128))
    tm = _pick(R, (256, 128, 64, 32, 16, 8))
    return _rowwise(fn, [(w, 0, C), (g, 0, C), (m, 0, C), (v, 0, C)], [], [(C, F32)] * 3, [], R=R, name=name,
                    tm=tm, rs=8, cs=cs)


def _my_place():
    return lax.axis_index("x"), lax.axis_index("y"), lax.axis_index("c")


def _other_chips(x, y):
    return [(1 - x, y), (x, 1 - y), (1 - x, 1 - y)]


_ANY = pl.BlockSpec(memory_space=pl.ANY)


def _rcopy(src, dst, ssem, rsem, to):
    return pltpu.make_async_remote_copy(src_ref=src, dst_ref=dst, send_sem=ssem, recv_sem=rsem, device_id=to,
                                        device_id_type=MESH)


def _place_shard(local, axis, jidx, out_dtype, name):
    lead, r, c = local.shape
    shp = [lead, r, c]
    shp[axis] *= N_CHIPS
    tr = _pick(r, (512, 256, 128)) if r % 128 == 0 else r
    nr = r // tr
    omap = (lambda l, i, j: (l, i, j[0])) if axis == 2 else (lambda l, i, j: (l, j[0] * nr + i, 0))

    def body(j_ref, x_ref, o_ref):
        o_ref[...] = x_ref[...].astype(out_dtype)

    return pl.pallas_call(
        body, name=name, out_shape=jax.ShapeDtypeStruct(tuple(shp), out_dtype),
        grid_spec=pltpu.PrefetchScalarGridSpec(
            num_scalar_prefetch=1, grid=(lead, nr),
            in_specs=[pl.BlockSpec((None, tr, c), lambda l, i, j: (l, i, 0))],
            out_specs=pl.BlockSpec((None, tr, c), omap)),
        compiler_params=_cparams(("parallel", "parallel")),
    )(jidx, local)


class _GatherSide:
    def __init__(self, fulls, axes, regions):
        self.inputs = list(fulls)
        self.out_shapes = [jax.ShapeDtypeStruct(f.shape, f.dtype) for f in fulls]
        self.aliases = {t: t for t in range(len(fulls))}
        self.n_sems = 6 * len(fulls)
        self.axes, self.regions = list(axes), list(regions)

    def _block(self, outs, t, chip, half):
        start, size, split = self.regions[t][:3]
        piece, n_pieces = self.regions[t][3] if len(self.regions[t]) > 3 else (0, 1)
        ax = self.axes[t]
        cut = outs[t].shape[ax] // N_CHIPS
        j = 2 * chip[0] + chip[1]
        idx = [pl.ds(start, size), slice(None), slice(None)]
        idx[ax] = pl.ds(j * cut, cut)
        if split == "lead":
            idx[0] = pl.ds(start + half * (size // 2), size // 2)
        else:
            other = 3 - ax
            h = outs[t].shape[other] // (2 * n_pieces)
            idx[other] = pl.ds((half * n_pieces + piece) * h, h)
        return outs[t].at[tuple(idx)]

    def _sends(self, outs, send, recv):
        x, y, c = _my_place()
        cps = []
        for t in range(len(outs)):
            mine = self._block(outs, t, (x, y), c)
            for r, chip in enumerate(_other_chips(x, y)):
                k = 3 * t + r
                cps.append(_rcopy(mine, mine, send.at[k], recv.at[k], (*chip, c)))
        return cps

    def start(self, ins, outs, send, recv):
        for cp in self._sends(outs, send, recv):
            cp.start()

    def finish(self, ins, outs, send, recv):
        x, y, c = _my_place()
        sib = (x, y, 1 - c)
        n = len(outs)
        chips = _other_chips(x, y)
        passed = []
        for t in range(n):
            for r, chip in enumerate(chips):
                k = 3 * t + r
                landed = self._block(outs, t, chip, c)
                _rcopy(landed, landed, send.at[k], recv.at[k], (*chip, c)).wait_recv()
                cp = _rcopy(landed, landed, send.at[3 * n + k], recv.at[3 * n + k], sib)
                cp.start()
                passed.append(cp)
        for t in range(n):
            for r, chip in enumerate(chips):
                k = 3 * n + 3 * t + r
                theirs = self._block(outs, t, chip, 1 - c)
                _rcopy(theirs, theirs, send.at[k], recv.at[k], sib).wait_recv()
        for cp in self._sends(outs, send, recv) + passed:
            cp.wait_send()


def _run_side(side, name):
    s_in = len(side.inputs)

    def body(*refs):
        ins, outs = refs[:s_in], refs[s_in:s_in + len(side.out_shapes)]
        send, recv = refs[s_in + len(side.out_shapes):]
        side.start(ins, outs, send, recv)
        side.finish(ins, outs, send, recv)

    return pl.pallas_call(
        body, name=name, out_shape=tuple(side.out_shapes), in_specs=[_ANY] * s_in,
        out_specs=tuple([_ANY] * len(side.out_shapes)), input_output_aliases=dict(side.aliases),
        scratch_shapes=[pltpu.SemaphoreType.DMA((side.n_sems,)), pltpu.SemaphoreType.DMA((side.n_sems,))],
    )(*side.inputs)


def _half_idx(shape, axis, half):
    size = shape[axis] // 2
    idx = [slice(None), slice(None)]
    idx[axis] = pl.ds(half * size, size)
    return tuple(idx)


class _PairSide:
    def __init__(self, grads, half_axes):
        self.inputs = list(grads)
        self.half_axes = list(half_axes)
        self.out_shapes = []
        for g, ax in zip(grads, half_axes):
            shp = list(g.shape)
            shp[ax] //= 2
            self.out_shapes.append(jax.ShapeDtypeStruct(tuple(shp), g.dtype))
        self.aliases = {}
        self.n_sems = len(grads)

    def _copies(self, srcs, outs, send, recv):
        x, y, c = _my_place()
        return [_rcopy(srcs[t].at[_half_idx(srcs[t].shape, self.half_axes[t], 1 - c)], outs[t], send.at[t], recv.at[t],
                       (x, y, 1 - c)) for t in range(len(srcs))]

    def start(self, srcs, outs, send, recv):
        for cp in self._copies(srcs, outs, send, recv):
            cp.start()

    def finish(self, srcs, outs, send, recv):
        for cp in self._copies(srcs, outs, send, recv):
            cp.wait()


def _pair_sum(g, recv, half_axis, cidx, name):
    K, N = recv.shape
    tm = _pick(K, (256, 128, 64, 32, 16))
    tn = _pick(N, (1024, 1408, 512, 256, 128))
    nbr, nbc = K // tm, N // tn
    if half_axis == 0:
        gmap = lambda i, j, c: (c[0] * nbr + i, j)
    else:
        gmap = lambda i, j, c: (i, c[0] * nbc + j)

    def body(c_ref, g_ref, r_ref, of_ref, ob_ref):
        s = g_ref[...] + r_ref[...]
        of_ref[...] = s
        ob_ref[...] = s.astype(BF16)

    omap = lambda i, j, c: (i, j)
    return pl.pallas_call(
        body, name=name,
        out_shape=(jax.ShapeDtypeStruct((K, N), F32), jax.ShapeDtypeStruct((K, N), BF16)),
        grid_spec=pltpu.PrefetchScalarGridSpec(
            num_scalar_prefetch=1, grid=(nbr, nbc),
            in_specs=[pl.BlockSpec((tm, tn), gmap), pl.BlockSpec((tm, tn), omap)],
            out_specs=(pl.BlockSpec((tm, tn), omap), pl.BlockSpec((tm, tn), omap))),
        compiler_params=_cparams(("parallel", "parallel")),
    )(cidx, g, recv)


def _shard_idx(shape, axis, j):
    size = shape[axis] // N_CHIPS
    idx = [slice(None), slice(None)]
    idx[axis] = pl.ds(j * size, size)
    return tuple(idx)


class _ScatterSide:
    def __init__(self, parts, shard_axes):
        self.inputs = list(parts)
        self.shard_axes = list(shard_axes)
        self.out_shapes = []
        for p, ax in zip(parts, shard_axes):
            shp = list(p.shape)
            shp[ax] //= N_CHIPS
            self.out_shapes.append(jax.ShapeDtypeStruct((3,) + tuple(shp), p.dtype))
        self.aliases = {}
        self.n_sems = 3 * len(parts)

    def _copies(self, srcs, outs, send, recv):
        x, y, c = _my_place()
        cps = []
        for t in range(len(srcs)):
            for r, chip in enumerate(_other_chips(x, y)):
                k = 3 * t + r
                j = 2 * chip[0] + chip[1]
                cps.append(_rcopy(srcs[t].at[_shard_idx(srcs[t].shape, self.shard_axes[t], j)], outs[t].at[r],
                                  send.at[k], recv.at[k], (*chip, c)))
        return cps

    def start(self, srcs, outs, send, recv):
        for cp in self._copies(srcs, outs, send, recv):
            cp.start()

    def finish(self, srcs, outs, send, recv):
        for cp in self._copies(srcs, outs, send, recv):
            cp.wait()


def _shard_sum(pf, recv, acc, layer, shard_axis, jcidx, name):
    _, K, N = recv.shape
    tm = _pick(K, (256, 128, 64, 32, 16))
    tn = _pick(N, (1024, 1408, 512, 256, 128))
    nbr, nbc = K // tm, N // tn
    if shard_axis == 0:
        pmap = lambda i, j, s: (s[0] * nbr + i, j)
        omap = lambda i, j, s: (layer, i, s[1] * nbc + j)
    else:
        pmap = lambda i, j, s: (i, s[0] * nbc + j)
        omap = lambda i, j, s: (layer, s[1] * nbr + i, j)

    def body(j_ref, p_ref, r_ref, a_ref, o_ref):
        o_ref[...] = ((p_ref[...] + r_ref[0].astype(F32)) + r_ref[1].astype(F32)) + r_ref[2].astype(F32)

    return pl.pallas_call(
        body, name=name, out_shape=jax.ShapeDtypeStruct(acc.shape, F32),
        grid_spec=pltpu.PrefetchScalarGridSpec(
            num_scalar_prefetch=1, grid=(nbr, nbc),
            in_specs=[pl.BlockSpec((tm, tn), pmap), pl.BlockSpec((3, tm, tn), lambda i, j, s: (0, i, j)), _ANY],
            out_specs=pl.BlockSpec((None, tm, tn), omap)),
        input_output_aliases={3: 0},
        compiler_params=_cparams(("parallel", "parallel")),
    )(jcidx, pf, recv, acc)


def _pair_join(accs, half_axes, name):
    n = len(accs)

    def body(*refs):
        outs = refs[n:2 * n]
        send_sems, recv_sems = refs[2 * n:]
        x, y, c = _my_place()
        sib = (x, y, 1 - c)

        def half(t, hc):
            return outs[t].at[(slice(None),) + _half_idx(outs[t].shape[1:], half_axes[t], hc)]

        cps = []
        for t in range(n):
            cp = _rcopy(half(t, c), half(t, c), send_sems.at[t], recv_sems.at[t], sib)
            cp.start()
            cps.append(cp)
        for t in range(n):
            _rcopy(half(t, 1 - c), half(t, 1 - c), send_sems.at[t], recv_sems.at[t], sib).wait_recv()
        for cp in cps:
            cp.wait_send()

    return pl.pallas_call(
        body, name=name, out_shape=tuple(jax.ShapeDtypeStruct(a.shape, a.dtype) for a in accs),
        in_specs=[_ANY] * n, out_specs=tuple([_ANY] * n), input_output_aliases={t: t for t in range(n)},
        scratch_shapes=[pltpu.SemaphoreType.DMA((n,)), pltpu.SemaphoreType.DMA((n,))],
    )(*accs)


_N_DEV = 8


def _allreduce_small(flat, name):
    _, R, _ = flat.shape

    def body(in_ref, out_ref, stage, send1, recv1, send2, recv2):
        x, y, c = _my_place()
        me = 4 * x + 2 * y + c
        places = [(px, py, pc) for px in range(2) for py in range(2) for pc in range(2)]
        def peer(r):
            return (x ^ (r >> 2), y ^ ((r >> 1) & 1), c ^ (r & 1))

        def peer_id(r):
            p = peer(r)
            return 4 * p[0] + 2 * p[1] + p[2]

        stage[0] = in_ref[me]
        cps = []
        for r in range(1, _N_DEV):
            cp = _rcopy(in_ref.at[peer_id(r)], stage.at[r], send1.at[r], recv1.at[r], peer(r))
            cp.start()
            cps.append(cp)
        for cp in cps:
            cp.wait()
        tot = jnp.zeros((R, 128), F32)
        for d in range(_N_DEV):
            tot = tot + stage[me ^ d]
        out_ref[me] = tot
        cps = []
        for r in range(1, _N_DEV):
            cp = _rcopy(out_ref.at[me], out_ref.at[me], send2.at[r], recv2.at[r], peer(r))
            cp.start()
            cps.append(cp)
        for r in range(1, _N_DEV):
            _rcopy(out_ref.at[peer_id(r)], out_ref.at[peer_id(r)], send2.at[r], recv2.at[r], peer(r)).wait_recv()
        for cp in cps:
            cp.wait_send()

    vm = pl.BlockSpec(memory_space=pltpu.VMEM)
    return pl.pallas_call(
        body, name=name, out_shape=jax.ShapeDtypeStruct(flat.shape, F32), in_specs=[vm], out_specs=vm,
        scratch_shapes=[pltpu.VMEM(flat.shape, F32)] + [pltpu.SemaphoreType.DMA((_N_DEV,))] * 4,
        compiler_params=_cparams(),
    )(flat)


_BIG = ("w_in", "ssm_w_glu", "w_branch", "w_out", "w_ffn_gate", "w_ffn_up", "w_ffn_down")
_BIG_SHARD_AXIS = {"w_in": 1, "ssm_w_glu": 1, "w_branch": 1, "w_out": 0, "w_ffn_gate": 1, "w_ffn_up": 1, "w_ffn_down": 0}
_SMALL = ("norm_mix_g", "gate_bias", "lru_conv_w", "lru_conv_b", "lru_wa", "lru_ba", "lru_wx", "lru_bx", "lru_lambda",
          "attn_rel_bias", "ssm_a_re", "ssm_a_im", "ssm_b_re", "ssm_b_im", "ssm_c_re", "ssm_c_im", "ssm_d",
          "ssm_log_step", "norm_ffn_g", "norm_final_g")
_SMALL_SHARDED = {"gate_bias": 2, "lru_conv_w": 2}
_WEIGHTS = ("norm_mix_g", "w_in", "gate_bias", "lru_conv_w", "lru_conv_b", "lru_wa", "lru_ba", "lru_wx", "lru_bx",
            "lru_lambda", "attn_rel_bias", "ssm_a_re", "ssm_a_im", "ssm_b_re", "ssm_b_im", "ssm_c_re", "ssm_c_im",
            "ssm_d", "ssm_log_step", "ssm_w_glu", "w_branch", "w_out", "norm_ffn_g", "w_ffn_gate", "w_ffn_up",
            "w_ffn_down", "norm_final_g")


def _carried(comm, phase, l, key, W, fn, *args, **kw):
    side = comm.side(phase, l, key, W)
    if side is None:
        return fn(*args, **kw)
    out, got = fn(*args, side=side, **kw)
    comm.took(phase, l, key, got, W)
    return out


def _layer_fwd(l, x, W, sm, comm):
    T = x.shape[0]
    nm = lambda s: f"{s}"
    h1 = _rms_fwd(x, sm["norm_mix_g"][l][None, :], nm("rms_fwd"))
    proj = _carried(comm, "fwd", l, "mm_in", W, _mm, h1, W["w_in"], M=T, N=IN_W, K=D_MODEL, b_lead=l, name=nm("mm_in"))
    wax, bax = sm["lru_prep"][l]
    cw, cb, lam = sm["lru_conv_w"][l], sm["lru_conv_b"][l][None, :], sm["lru_lambda"][l][None, :]
    y_a, hst = _lru_fwd(proj, cw, cb, wax, bax, lam, nm("lru_fwd"))
    bias = sm["attn_bias"][l]
    y_b = _carried(comm, "fwd", l, "attn_fwd", W, _attn_fwd, proj, bias, nm("attn_fwd"))
    tgt, bret, bimt, cre, cim, lr, li, dflat = sm["s5_prep"][l]
    u = _to_chunks(proj, 5 * MIX_W, nm("to_chunks"))
    sin_re, sin_im = _s5_in(u, bret, bimt, nm("s5_in"))
    sp_re, sp_im = _s5_scan(sin_re.transpose(1, 0, 2), sin_im.transpose(1, 0, 2), lr, li, nm("s5_scan"))
    sp_re, sp_im = sp_re.transpose(1, 0, 2), sp_im.transpose(1, 0, 2)
    pre, ycf = _s5_out(u, sp_re, sp_im, tgt, cre, cim, dflat, nm("s5_out"))
    y_c = _from_chunks(ycf, BF16, nm("from_chunks"))
    brs = []
    for k, yk in enumerate((y_a, y_b, y_c)):
        brs.append(_mm(yk, W["w_branch"], M=T, N=D_MODEL, K=MIX_W, b_lead=3 * l + k, out_dtype=BF16, name=nm("mm_branch")))
    brs.append(_mm(y_c, W["ssm_w_glu"], M=T, N=D_MODEL, K=MIX_W, b_lead=l, out_dtype=BF16, name=nm("mm_branch")))
    br4 = tuple(brs)
    gb3 = sm["gate_bias"][l]
    merged = _merge_fwd(br4, proj, gb3, nm("merge_fwd"))
    x1 = _mm(merged, W["w_out"], M=T, N=D_MODEL, K=D_MODEL, b_lead=l, res=x, name=nm("mm_out"))
    h2 = _rms_fwd(x1, sm["norm_ffn_g"][l][None, :], nm("rms_fwd"))
    gpre = _carried(comm, "fwd", l, "mm_ffn_gate", W, _mm, h2, W["w_ffn_gate"], M=T, N=FFN_H, K=D_MODEL, b_lead=l,
                    out_dtype=BF16, name=nm("mm_ffn_up"))
    upre = _carried(comm, "fwd", l, "mm_ffn_up", W, _mm, h2, W["w_ffn_up"], M=T, N=FFN_H, K=D_MODEL, b_lead=l,
                    out_dtype=BF16, name=nm("mm_ffn_up"))
    gu = (gpre, upre)
    act = _act_fwd(gu, nm("act_fwd"))
    x2 = _carried(comm, "fwd", l, "mm_down", W, _mm, act, W["w_ffn_down"], M=T, N=D_MODEL, K=FFN_H, b_lead=l, res=x1,
                  name=nm("mm_down"))
    saved = dict(x=x, h1=h1, proj=proj, hst=hst, y_a=y_a, y_b=y_b, y_c=y_c, u=u, sp_re=sp_re, sp_im=sp_im, pre=pre,
                 br4=br4, merged=merged, x1=x1, h2=h2, gu=gu, act=act)
    return x2, saved


def _layer_bwd(l, dx2, sv, W, sm, comm):
    T = dx2.shape[0]
    nm = lambda s: f"{s}"
    big, small = {}, {}
    dxb = dx2.astype(BF16)
    big["w_ffn_down"] = _carried(comm, "bwd", l, "mm_dw_down", W, _mm, sv["act"], dxb, M=FFN_H, N=D_MODEL, K=T, ta=True,
                                 name=nm("mm_dw_down"))
    dact = _carried(comm, "bwd", l, "mm_dact", W, _mm, dxb, W["w_ffn_down"], M=T, N=FFN_H, K=D_MODEL, tb=True, b_lead=l,
                    out_dtype=BF16, name=nm("mm_dact"))
    dg, du = _act_bwd(sv["gu"], dact, nm("act_bwd"))
    big["w_ffn_gate"] = _carried(comm, "bwd", l, "mm_dw_gate", W, _mm, sv["h2"], dg, M=D_MODEL, N=FFN_H, K=T, ta=True,
                                 name=nm("mm_dw_up"))
    big["w_ffn_up"] = _mm(sv["h2"], du, M=D_MODEL, N=FFN_H, K=T, ta=True, name=nm("mm_dw_up"))
    dh2 = _mm(dg, W["w_ffn_gate"], M=T, N=D_MODEL, K=FFN_H, tb=True, b_lead=l, name=nm("mm_dh2"))
    dh2 = _mm(du, W["w_ffn_up"], M=T, N=D_MODEL, K=FFN_H, tb=True, b_lead=l, res=dh2, name=nm("mm_dh2r"))
    dx1, dgn = _rms_bwd(sv["x1"], sm["norm_ffn_g"][l][None, :], dh2, dx2, nm("rms_bwd"))
    small["norm_ffn_g"] = dgn[0]
    dx1b = dx1.astype(BF16)
    big["w_out"] = _mm(sv["merged"], dx1b, M=D_MODEL, N=D_MODEL, K=T, ta=True, name=nm("mm_dw_out"))
    dm = _mm(dx1b, W["w_out"], M=T, N=D_MODEL, K=D_MODEL, tb=True, b_lead=l, out_dtype=BF16, name=nm("mm_dmerged"))
    dbr, dgates, dgb = _merge_bwd(sv["br4"], sv["proj"], sm["gate_bias"][l], dm, nm("merge_bwd"))
    small["gate_bias"] = dgb
    ys = (sv["y_a"], sv["y_b"], sv["y_c"])
    big["w_branch"] = [_mm(ys[k], dbr[k], M=MIX_W, N=D_MODEL, K=T, ta=True, name=nm("mm_dw_branch")) for k in range(3)]
    big["ssm_w_glu"] = _mm(sv["y_c"], dbr[3], M=MIX_W, N=D_MODEL, K=T, ta=True, name=nm("mm_dw_branch"))
    dya = _mm(dbr[0], W["w_branch"], M=T, N=MIX_W, K=D_MODEL, tb=True, b_lead=3 * l, name=nm("mm_dy"))
    dyb = _mm(dbr[1], W["w_branch"], M=T, N=MIX_W, K=D_MODEL, tb=True, b_lead=3 * l + 1, out_dtype=BF16, name=nm("mm_dy"))
    dyc = _mm(dbr[2], W["w_branch"], M=T, N=MIX_W, K=D_MODEL, tb=True, b_lead=3 * l + 2, name=nm("mm_dy"))
    dyc = _mm(dbr[3], W["ssm_w_glu"], M=T, N=MIX_W, K=D_MODEL, tb=True, b_lead=l, res=dyc, name=nm("mm_dyr"))
    tgt, bret, bimt, cre, cim, lr, li, dflat = sm["s5_prep"][l]
    dpre, dsp_re, dsp_im, d_tgt, d_cre, d_cim, d_dflat = _s5_bwd_out(
        _to_chunks(dyc, 0, nm("to_chunks")), sv["pre"], sv["u"], sv["sp_re"], sv["sp_im"], cre, cim, nm("s5_bwd_out"))
    tr = lambda a: a.transpose(1, 0, 2)
    dsin_re, dsin_im, d_lr, d_li = _s5_rscan(tr(dsp_re), tr(dsp_im), tr(sv["sp_re"]), tr(sv["sp_im"]), lr, li, nm("s5_rscan"))
    du_f, d_bret, d_bimt = _s5_bwd_in(dpre, dsin_re.transpose(1, 0, 2), dsin_im.transpose(1, 0, 2), sv["u"], tgt, bret, bimt,
                                      dflat, nm("s5_bwd_in"))
    d_u = _from_chunks(du_f, BF16, nm("from_chunks"))
    small["s5_tables"] = (d_tgt, d_bret, d_bimt, d_cre, d_cim, d_lr, d_li, d_dflat)
    dq, dk, dv, dbias = _carried(comm, "bwd", l, "attn_bwd", W, _attn_bwd, sv["proj"], sm["attn_bias"][l], dyb, nm("attn_bwd"))
    small["attn_bias"] = dbias
    wax, bax = sm["lru_prep"][l]
    cw, cb, lam = sm["lru_conv_w"][l], sm["lru_conv_b"][l][None, :], sm["lru_lambda"][l][None, :]
    d_lx, d_lg, d_cw, d_cb, d_wax, d_bax, d_lam = _lru_bwd(sv["proj"], sv["hst"], dya, cw, cb, wax, bax, lam, nm("lru_bwd"))
    small["lru_conv_w"], small["lru_conv_b"], small["lru_lambda"] = d_cw, d_cb[0], d_lam[0]
    small["lru_tables"] = (d_wax, d_bax)
    dproj = jnp.concatenate([d_lx, d_lg, dq, dk, dv, d_u] + list(dgates), axis=1)
    big["w_in"] = _carried(comm, "bwd", l, "mm_dw_in", W, _mm, sv["h1"], dproj, M=D_MODEL, N=IN_W, K=T, ta=True,
                           name=nm("mm_dw_in"))
    dh1 = _carried(comm, "bwd", l, "mm_dh1", W, _mm, dproj, W["w_in"], M=T, N=D_MODEL, K=IN_W, tb=True, b_lead=l,
                   name=nm("mm_dh1"))
    dx, dgn = _rms_bwd(sv["x"], sm["norm_mix_g"][l][None, :], dh1, dx1, nm("rms_bwd"))
    small["norm_mix_g"] = dgn[0]
    return dx, big, small


_TENSORS = tuple((n, k) for n in _BIG for k in range(N_BRANCH if n == "w_branch" else 1))
_FWD_CARRIERS = {"mm_in": (("w_out", "w_branch", "ssm_w_glu", "w_ffn_gate"), 0), "attn_fwd": (("w_ffn_up",), 0),
                 "mm_ffn_gate": (("w_ffn_down",), 0), "mm_ffn_up": (("w_in",), 1), "mm_down": (("w_in",), 1)}
_FWD_PIECES = {"mm_ffn_up": (0, 2), "mm_down": (1, 2)}
_BWD_PAIR_CARRIERS = {"mm_dw_down": ("w_in", "w_out", "ssm_w_glu"), "mm_dact": ("w_ffn_gate", "w_ffn_up", "w_ffn_down", "w_branch")}
_BWD_CARRIERS = {"mm_dw_gate": ("w_out", "w_branch", "ssm_w_glu"), "attn_bwd": ("w_ffn_down",), "mm_dw_in": ("w_in",),
                 "mm_dh1": ("w_ffn_gate", "w_ffn_up")}


class _StepComm:
    def __init__(self, depth, cidx, jcidx):
        self.depth, self.cidx, self.jcidx = depth, cidx, jcidx
        self.accs = {}
        self.raw = None
        self.paired = {}
        self.pending = None

    def gather_side(self, W, names, l, piece):
        fulls, axes, regions = [], [], []
        for n in names:
            per = N_BRANCH if n == "w_branch" else 1
            fulls.append(W[n])
            axes.append(_BIG_SHARD_AXIS[n] + 1)
            regions.append((per * l, per, "other", piece))
        return _GatherSide(fulls, axes, regions)

    def side(self, phase, l, key, W):
        if phase == "fwd":
            if key not in _FWD_CARRIERS or l + _FWD_CARRIERS[key][1] >= self.depth:
                return None
            names, ahead = _FWD_CARRIERS[key]
            return self.gather_side(W, names, l + ahead, _FWD_PIECES.get(key, (0, 1)))
        if key in _BWD_PAIR_CARRIERS and self.raw is not None:
            nks = [nk for nk in _TENSORS if nk[0] in _BWD_PAIR_CARRIERS[key]]
            return _PairSide([self.raw[1][nk] for nk in nks], [1 - _BIG_SHARD_AXIS[nk[0]] for nk in nks])
        if key in _BWD_CARRIERS and self.pending is not None:
            nks = [nk for nk in _TENSORS if nk[0] in _BWD_CARRIERS[key]]
            return _ScatterSide([self.pending[1][nk][1] for nk in nks], [_BIG_SHARD_AXIS[nk[0]] for nk in nks])
        return None

    def took(self, phase, l, key, got, W):
        if phase == "fwd":
            W.update(zip(_FWD_CARRIERS[key][0], got))
        elif key in _BWD_PAIR_CARRIERS:
            nks = [nk for nk in _TENSORS if nk[0] in _BWD_PAIR_CARRIERS[key]]
            self.paired.update(zip(nks, got))
            if len(self.paired) == len(_TENSORS):
                self.pair_sums()
        else:
            nks = [nk for nk in _TENSORS if nk[0] in _BWD_CARRIERS[key]]
            self.shard_sums(self.pending[0], nks, got)

    def pair_sums(self):
        l, grads = self.raw
        self.pending = (l, {nk: _pair_sum(grads[nk], self.paired[nk], 1 - _BIG_SHARD_AXIS[nk[0]], self.cidx, "pair_sum")
                            for nk in _TENSORS})
        self.raw, self.paired = None, {}

    def shard_sums(self, l, nks, got):
        for nk, r in zip(nks, got):
            pf = self.pending[1][nk][0]
            sa = _BIG_SHARD_AXIS[nk[0]]
            if nk not in self.accs:
                shp = list(pf.shape)
                shp[sa] //= N_CHIPS
                shp[1 - sa] *= 2
                self.accs[nk] = lax.empty((self.depth,) + tuple(shp), F32)
            self.accs[nk] = _shard_sum(pf, r, self.accs[nk], l, sa, self.jcidx, "shard_sum")

    def on_big(self, l, big):
        grads = {}
        for n in _BIG:
            gs = big[n] if isinstance(big[n], list) else [big[n]]
            grads.update({(n, k): g for k, g in enumerate(gs)})
        self.raw = (l, grads)
        if l == 0:
            half_axes = [1 - _BIG_SHARD_AXIS[nk[0]] for nk in _TENSORS]
            self.paired = dict(zip(_TENSORS, _run_side(_PairSide([grads[nk] for nk in _TENSORS], half_axes), "pair_exchange")))
            self.pair_sums()
            side = _ScatterSide([self.pending[1][nk][1] for nk in _TENSORS], [_BIG_SHARD_AXIS[nk[0]] for nk in _TENSORS])
            self.shard_sums(0, _TENSORS, _run_side(side, "chip_scatter"))
            self.pending = None


class _NoComm:
    def __init__(self, on_big):
        self.on_big = on_big

    def side(self, phase, l, key, W):
        return None


def _local_step(xs, tgt, W, sm, comm):
    W = dict(W)
    sm = dict(sm)
    depth = sm["norm_mix_g"].shape[0]
    lru_o, lru_vjp = jax.vjp(jax.vmap(_lru_prep), sm["lru_wa"], sm["lru_wx"], sm["lru_ba"], sm["lru_bx"])
    attn_o, attn_vjp = jax.vjp(jax.vmap(_attn_bias), sm["attn_rel_bias"])
    s5_names = ("ssm_a_re", "ssm_a_im", "ssm_b_re", "ssm_b_im", "ssm_c_re", "ssm_c_im", "ssm_d", "ssm_log_step")
    s5_o, s5_vjp = jax.vjp(jax.vmap(_s5_prep), *[sm[n] for n in s5_names])
    wax_all = lru_o[0].astype(BF16)
    sm["lru_prep"] = [(wax_all[l], lru_o[1][l]) for l in range(depth)]
    sm["attn_bias"] = [attn_o[l] for l in range(depth)]
    kt_in = [t.reshape((depth * SSM_G,) + t.shape[2:]) for t in s5_o[:4]]
    tgt_all = _s5_tgt(*kt_in, "s5_tgt").reshape(depth, SSM_G, S5_LW, S5_LW)
    sm["s5_prep"] = [(tgt_all[l],) + tuple(t[l] for t in s5_o[4:]) for l in range(depth)]

    saved = []
    for l in range(depth):
        xs, sv = _layer_fwd(l, xs, W, sm, comm)
        saved.append(sv)
    loss_part, dx, dgf = _final_loss(xs, sm["norm_final_g"][None, :], tgt, "final_loss")

    direct = ("norm_mix_g", "gate_bias", "lru_conv_w", "lru_conv_b", "lru_lambda", "norm_ffn_g")
    per_layer = [None] * depth
    for l in reversed(range(depth)):
        dx, big, per_layer[l] = _layer_bwd(l, dx, saved[l], W, sm, comm)
        comm.on_big(l, big)
    stacked = lambda key, i: jnp.stack([per_layer[l][key][i] for l in range(depth)])
    small_tree = {n: jnp.stack([per_layer[l][n] for l in range(depth)]) for n in direct}
    d_wa, d_wx, d_ba, d_bx = lru_vjp(tuple(stacked("lru_tables", i) for i in range(2)))
    (d_rel,) = attn_vjp(jnp.stack([per_layer[l]["attn_bias"] for l in range(depth)]))
    d_kt_in = _s5_tgt_t(stacked("s5_tables", 0).reshape(depth * SSM_G, S5_LW, S5_LW), *kt_in, "s5_tgt_t")
    d_kt_in = [t.reshape((depth, SSM_G) + t.shape[1:]) for t in d_kt_in]
    d_s5 = s5_vjp(tuple(d_kt_in) + tuple(stacked("s5_tables", i) for i in range(1, 8)))
    small_tree.update(lru_wa=d_wa, lru_wx=d_wx, lru_ba=d_ba, lru_bx=d_bx, attn_rel_bias=d_rel, norm_final_g=dgf[0])
    small_tree.update(zip(s5_names, d_s5))
    return loss_part, dx, small_tree


def _pack_small(tree, names):
    flat = jnp.concatenate([tree[n].reshape(-1) for n in names])
    per = -(-flat.shape[0] // (_N_DEV * 128 * 8)) * (128 * 8)
    flat = jnp.pad(flat, (0, _N_DEV * per - flat.shape[0]))
    return flat.reshape(_N_DEV, per // 128, 128)


def _unpack_small(flat, like, names):
    flat = flat.reshape(-1)
    out, off = {}, 0
    for n in names:
        size = math.prod(like[n].shape)
        out[n] = flat[off:off + size].reshape(like[n].shape)
        off += size
    return out


def kernel(x, norm_mix_g, w_in, gate_bias, lru_conv_w, lru_conv_b, lru_wa, lru_ba, lru_wx, lru_bx, lru_lambda, attn_rel_bias, ssm_a_re, ssm_a_im, ssm_b_re, ssm_b_im, ssm_c_re, ssm_c_im, ssm_d, ssm_log_step, ssm_w_glu, w_branch, w_out, norm_ffn_g, w_ffn_gate, w_ffn_up, w_ffn_down, norm_final_g, loss_target, m_norm_mix_g, m_w_in, m_gate_bias, m_lru_conv_w, m_lru_conv_b, m_lru_wa, m_lru_ba, m_lru_wx, m_lru_bx, m_lru_lambda, m_attn_rel_bias, m_ssm_a_re, m_ssm_a_im, m_ssm_b_re, m_ssm_b_im, m_ssm_c_re, m_ssm_c_im, m_ssm_d, m_ssm_log_step, m_ssm_w_glu, m_w_branch, m_w_out, m_norm_ffn_g, m_w_ffn_gate, m_w_ffn_up, m_w_ffn_down, m_norm_final_g, v_norm_mix_g, v_w_in, v_gate_bias, v_lru_conv_w, v_lru_conv_b, v_lru_wa, v_lru_ba, v_lru_wx, v_lru_bx, v_lru_lambda, v_attn_rel_bias, v_ssm_a_re, v_ssm_a_im, v_ssm_b_re, v_ssm_b_im, v_ssm_c_re, v_ssm_c_im, v_ssm_d, v_ssm_log_step, v_ssm_w_glu, v_w_branch, v_w_out, v_norm_ffn_g, v_w_ffn_gate, v_w_ffn_up, v_w_ffn_down, v_norm_final_g):
    args = dict(locals())
    w = {n: args[n] for n in _WEIGHTS}
    m = {n: args["m_" + n] for n in _WEIGHTS}
    v = {n: args["v_" + n] for n in _WEIGHTS}
    depth = w_in.shape[0]
    xc, yc, cc = _my_place()
    jchip = 2 * xc + yc
    cidx = jnp.reshape(cc, (1,)).astype(jnp.int32)
    jidx = jnp.reshape(jchip, (1,)).astype(jnp.int32)
    jcidx = jnp.stack([jchip, cc]).astype(jnp.int32)

    blocks = [w[n] for n in _BIG]
    blocks[2] = blocks[2].reshape(depth * N_BRANCH, MIX_W, -1)
    axes = [_BIG_SHARD_AXIS[n] + 1 for n in _BIG] + [_SMALL_SHARDED[n] for n in _SMALL_SHARDED]
    placed = [_place_shard(b, ax, jidx, BF16, "place_shard") for b, ax in zip(blocks, axes)]
    placed += [_place_shard(w[n], _SMALL_SHARDED[n], jidx, F32, "place_shard") for n in _SMALL_SHARDED]
    W = dict(zip(_BIG, placed))
    first = [W["w_in"]] + placed[len(_BIG):]
    regions = [(0, 1, "other")] + [(0, depth, "lead")] * len(_SMALL_SHARDED)
    gathered = _run_side(_GatherSide(first, [axes[0]] + axes[len(_BIG):], regions), "gather_weights")
    W["w_in"] = gathered[0]
    sm_full = dict(zip(_SMALL_SHARDED, gathered[1:]))
    sm = {n: w[n] for n in _SMALL if n not in _SMALL_SHARDED}
    sm.update(sm_full)

    comm = _StepComm(depth, cidx, jcidx)
    loss_part, dx, small_tree = _local_step(x[0], loss_target[0], W, sm, comm)
    loss = lax.psum(loss_part, ("x", "y", "c"))
    grad_x = dx[None]

    joined = _pair_join([comm.accs[nk] for nk in _TENSORS], [1 - _BIG_SHARD_AXIS[nk[0]] for nk in _TENSORS], "pair_join")
    jd = dict(zip(_TENSORS, joined))
    grads = {}
    for n in _BIG:
        if n == "w_branch":
            grads[n] = jnp.stack([jd[(n, k)] for k in range(N_BRANCH)], axis=1)
        else:
            grads[n] = jd[(n, 0)]
    like = {n: (sm_full[n] if n in _SMALL_SHARDED else w[n]) for n in _SMALL}
    red = _unpack_small(_allreduce_small(_pack_small(small_tree, _SMALL), "allreduce_small"), like, _SMALL)
    for n in _SMALL:
        if n in _SMALL_SHARDED:
            size = w[n].shape[2]
            grads[n] = lax.dynamic_slice_in_dim(red[n], (2 * xc + yc) * size, size, axis=2)
        else:
            grads[n] = red[n]

    delta, new_m, new_v = {}, {}, {}
    for n in _BIG:
        shp = w[n].shape
        two = lambda a: a.reshape(-1, shp[-1])
        d_, m_, v_ = _adamw(two(w[n]), two(grads[n]), two(m[n]), two(v[n]), "adamw")
        delta[n], new_m[n], new_v[n] = d_.reshape(shp), m_.reshape(shp), v_.reshape(shp)
    pk = lambda tree: _pack_small(tree, _SMALL).reshape(-1, 128)
    d_, m_, v_ = _adamw(pk(w), pk(grads), pk(m), pk(v), "adamw_small")
    like_local = {n: w[n] for n in _SMALL}
    for tree, flat in ((delta, d_), (new_m, m_), (new_v, v_)):
        tree.update(_unpack_small(flat, like_local, _SMALL))
    return (loss, grad_x, *[grads[n] for n in _WEIGHTS], *[delta[n] for n in _WEIGHTS], *[new_m[n] for n in _WEIGHTS],
            *[new_v[n] for n in _WEIGHTS])
```

```python
import functools
import math

import jax
import jax.numpy as jnp
from jax import lax
from jax.experimental import pallas as pl
from jax.experimental.pallas import tpu as pltpu

F32 = jnp.float32
BF16 = jnp.bfloat16

D_MODEL = 2048
MIX_W = 1024
N_BRANCH = 3
LRU_BLOCKS = 16
LRU_BW = 64
CONV_W = 4
LRU_C = 8.0
CHUNK = 64
ATT_HEADS = 8
ATT_HD = 128
ATT_LEFT = 8
ATT_BAND = (ATT_LEFT + 1) * CHUNK
MAX_REL = 128
N_REL = 2 * MAX_REL + 1
SSM_G = 64
SSM_H = 16
SSM_P = 64
FFN_H = 5632
IN_W = 6 * MIX_W + N_BRANCH * D_MODEL
NORM_EPS = 1e-6
MASK_VALUE = -1e30
ADAM_LR, ADAM_B1, ADAM_B2, ADAM_EPS, ADAM_WD, ADAM_STEP = 0.001, 0.9, 0.999, 1e-08, 0.01, 10

S5_L = 16
S5_LW = S5_L * SSM_H
N_CHIPS = 4
V7X_VMEM_LIMIT = 56 * 1024 * 1024
HI = lax.Precision.HIGHEST
MESH = pl.DeviceIdType.MESH


def _cparams(sem=None):
    return pltpu.CompilerParams(dimension_semantics=sem, vmem_limit_bytes=V7X_VMEM_LIMIT)


def _pick(n, prefs):
    for p in prefs:
        if n % p == 0:
            return p
    return n


_MM_VMEM_BUDGET = 46 * 1024 * 1024


def _mm_tiles(M, N, K, has_res, out_bytes):
    tn = _pick(N, (1024, 1408, 512, 256, 128))
    for tk in (K, 2048, 1408, 1024, 512, 256, 128):
        if K % tk:
            continue
        for tm in (1024, 512, 256, 128, 64, 32, 16, 8):
            if M % tm:
                continue
            need = 2 * 2 * (tm * tk + tk * tn) + 2 * tm * tn * out_bytes
            need += tm * tn * 4 if tk < K else 0
            need += 2 * tm * tn * 4 if has_res else 0
            if need <= _MM_VMEM_BUDGET:
                return tm, tn, tk
    raise ValueError((M, N, K))


def _call(body, *, name, grid, in_specs, out_specs, out_shape, args, scratch_shapes=(), sem=None, side=None):
    in_specs, out_specs, out_shape = list(in_specs), list(out_specs), list(out_shape)
    scratch_shapes = list(scratch_shapes)
    if side is None:
        outs = pl.pallas_call(body, name=name, out_shape=tuple(out_shape), grid=grid, in_specs=in_specs,
                              out_specs=tuple(out_specs), scratch_shapes=scratch_shapes, compiler_params=_cparams(sem))(*args)
        return tuple(outs), ()
    n_in, n_out, n_scr = len(in_specs), len(out_shape), len(scratch_shapes)
    s_in, s_out = len(side.inputs), len(side.out_shapes)

    def wrapped(*refs):
        mi, refs = refs[:n_in], refs[n_in:]
        si, refs = refs[:s_in], refs[s_in:]
        mo, refs = refs[:n_out], refs[n_out:]
        so, refs = refs[:s_out], refs[s_out:]
        scr, (send, recv) = refs[:n_scr], refs[n_scr:]
        first = functools.reduce(jnp.logical_and, [pl.program_id(d) == 0 for d in range(len(grid))])
        last = functools.reduce(jnp.logical_and, [pl.program_id(d) == g - 1 for d, g in enumerate(grid)])

        @pl.when(first)
        def _():
            side.start(si, so, send, recv)

        body(*mi, *mo, *scr)

        @pl.when(last)
        def _():
            side.finish(si, so, send, recv)

    outs = pl.pallas_call(
        wrapped, name=name, out_shape=tuple(out_shape + list(side.out_shapes)), grid=grid,
        in_specs=in_specs + [_ANY] * s_in, out_specs=tuple(out_specs + [_ANY] * s_out),
        scratch_shapes=scratch_shapes + [pltpu.SemaphoreType.DMA((side.n_sems,)), pltpu.SemaphoreType.DMA((side.n_sems,))],
        input_output_aliases={n_in + i: n_out + o for i, o in side.aliases.items()},
        compiler_params=_cparams(("arbitrary",) * len(grid)),
    )(*args, *side.inputs)
    return tuple(outs[:n_out]), tuple(outs[n_out:])


def _mm(a, b, *, M, N, K, name, ta=False, tb=False, a_lead=None, b_lead=None, a_off=(0, 0), b_off=(0, 0),
        out_dtype=F32, res=None, tm=None, tn=None, tk=None, side=None):
    if tm is None and tn is None and tk is None:
        tm, tn, tk = _mm_tiles(M, N, K, res is not None, jnp.dtype(out_dtype).itemsize)
    nk = K // tk

    def spec(blk, lead, off, order):
        r0, c0 = off[0] // blk[0], off[1] // blk[1]
        assert off[0] % blk[0] == 0 and off[1] % blk[1] == 0
        if lead is None:
            return pl.BlockSpec(blk, lambda i, j, k: (r0 + order(i, j, k)[0], c0 + order(i, j, k)[1]))
        return pl.BlockSpec((None,) + blk, lambda i, j, k: (lead, r0 + order(i, j, k)[0], c0 + order(i, j, k)[1]))

    a_spec = spec((tk, tm), a_lead, a_off, lambda i, j, k: (k, i)) if ta else spec((tm, tk), a_lead, a_off, lambda i, j, k: (i, k))
    b_spec = spec((tn, tk), b_lead, b_off, lambda i, j, k: (j, k)) if tb else spec((tk, tn), b_lead, b_off, lambda i, j, k: (k, j))
    dims = (((0 if ta else 1,), (1 if tb else 0,)), ((), ()))
    in_specs = [a_spec, b_spec]
    args = [a, b]
    if res is not None:
        in_specs.append(pl.BlockSpec((tm, tn), lambda i, j, k: (i, j)))
        args.append(res)

    def body(*refs):
        a_ref, b_ref = refs[:2]
        r_ref = refs[2] if res is not None else None
        o_ref = refs[3] if res is not None else refs[2]

        def dot():
            return lax.dot_general(a_ref[...], b_ref[...], dims, preferred_element_type=F32)

        def finish(r):
            if r_ref is not None:
                r = r + r_ref[...]
            o_ref[...] = r.astype(out_dtype)

        if nk == 1:
            finish(dot())
            return
        acc = refs[-1]
        k = pl.program_id(2)

        @pl.when(k == 0)
        def _():
            acc[...] = dot()

        @pl.when(jnp.logical_and(k > 0, k < nk - 1))
        def _():
            acc[...] += dot()

        @pl.when(k == nk - 1)
        def _():
            finish(acc[...] + dot())

    (out,), got = _call(
        body, name=name, out_shape=[jax.ShapeDtypeStruct((M, N), out_dtype)],
        grid=(M // tm, N // tn, nk), in_specs=in_specs, out_specs=[pl.BlockSpec((tm, tn), lambda i, j, k: (i, j))],
        scratch_shapes=[pltpu.VMEM((tm, tn), F32)] if nk > 1 else [],
        sem=("parallel", "parallel", "arbitrary"), args=args, side=side)
    return out if side is None else (out, got)


def _rowwise(fn, rows, consts, out_rows, out_accs, *, R, name, tm=256, rs=16, cs=None):
    tm = min(tm, R)
    assert R % tm == 0 and tm % rs == 0
    n_r, n_c, n_o, n_a = len(rows), len(consts), len(out_rows), len(out_accs)
    nsteps = R // tm
    widths = [w for _, _, w in rows]
    if cs is not None:
        assert all(w == widths[0] for w in widths) and widths[0] % cs == 0
        col_chunks = [(c0, cs) for c0 in range(0, widths[0], cs)]
    else:
        col_chunks = [None]

    def body(*refs):
        r_refs = refs[:n_r]
        c_refs = refs[n_r:n_r + n_c]
        o_refs = refs[n_r + n_c:n_r + n_c + n_o]
        a_refs = refs[n_r + n_c + n_o:n_r + n_c + n_o + n_a]
        s_refs = refs[n_r + n_c + n_o + n_a:]
        i = pl.program_id(0)

        @pl.when(i == 0)
        def _():
            for s in s_refs:
                s[...] = jnp.zeros_like(s)

        def piece(g, carry):
            r0 = pl.multiple_of(g * rs, rs)
            for cc in col_chunks:
                csl = slice(None) if cc is None else slice(cc[0], cc[0] + cc[1])
                rp = [r[pl.ds(r0, rs), csl].astype(F32) for r in r_refs]
                cp = [c[:, csl] for c in c_refs]
                outs, accs = fn(rp, cp)
                for o_ref, o in zip(o_refs, outs):
                    o_ref[pl.ds(r0, rs), csl] = o.astype(o_ref.dtype)
                for s_ref, av in zip(s_refs, accs):
                    s_ref[:, csl] += av
            return carry

        lax.fori_loop(0, tm // rs, piece, 0)

        @pl.when(i == nsteps - 1)
        def _():
            for a_ref, s_ref in zip(a_refs, s_refs):
                a_ref[...] = jnp.sum(s_ref[...], axis=0, keepdims=True)

    in_specs = [pl.BlockSpec((tm, w), functools.partial(lambda i, cb: (i, cb), cb=off // w)) for _, off, w in rows]
    for _, off, w in rows:
        assert off % w == 0
    in_specs += [pl.BlockSpec(c.shape, lambda i: (0, 0)) for c in consts]
    out_specs = [pl.BlockSpec((tm, w), lambda i: (i, 0)) for w, _ in out_rows]
    out_specs += [pl.BlockSpec((1, w), lambda i: (0, 0)) for w in out_accs]
    out_shape = [jax.ShapeDtypeStruct((R, w), dt) for w, dt in out_rows]
    out_shape += [jax.ShapeDtypeStruct((1, w), F32) for w in out_accs]
    return pl.pallas_call(
        body, name=name, out_shape=tuple(out_shape), grid=(nsteps,), in_specs=in_specs, out_specs=tuple(out_specs),
        scratch_shapes=[pltpu.VMEM((rs, w), F32) for w in out_accs],
        compiler_params=_cparams(("arbitrary",)),
    )(*[r for r, _, _ in rows], *consts)


def _rms(x, g):
    r = lax.rsqrt(jnp.mean(x * x, axis=-1, keepdims=True) + NORM_EPS)
    return x * r * g


def _rms_fwd(x, g, name):
    R = x.shape[0]

    def fn(rp, cp):
        return [_rms(rp[0], cp[0])], []

    return _rowwise(fn, [(x, 0, D_MODEL)], [g], [(D_MODEL, BF16)], [], R=R, name=name)[0]


def _rms_bwd(x, g, dh, dres, name):
    R = x.shape[0]

    def fn(rp, cp):
        xv, dhv, drv = rp
        _, pull = jax.vjp(_rms, xv, jnp.broadcast_to(cp[0], xv.shape))
        dx, dgv = pull(dhv)
        return [drv + dx], [dgv]

    dx, dg = _rowwise(fn, [(x, 0, D_MODEL), (dh, 0, D_MODEL), (dres, 0, D_MODEL)], [g], [(D_MODEL, F32)], [D_MODEL],
                      R=R, name=name, rs=8)
    return dx, dg


def _final_loss(x, g, tgt, name):
    R = x.shape[0]

    def loss_rows(xv, gv, tv):
        e = _rms(xv, gv) - tv
        return 0.5 * jnp.mean(e * e, axis=-1, keepdims=True)

    def fn(rp, cp):
        xv, tv = rp
        lr, pull = jax.vjp(lambda a, b: loss_rows(a, b, tv), xv, jnp.broadcast_to(cp[0], xv.shape))
        dx, dgv = pull(jnp.ones_like(lr))
        return [dx], [dgv, jnp.broadcast_to(lr, (lr.shape[0], 128))]

    dx, dg, lsum = _rowwise(fn, [(x, 0, D_MODEL), (tgt, 0, D_MODEL)], [g], [(D_MODEL, F32)], [D_MODEL, 128],
                            R=R, name=name, rs=8)
    return lsum[0, 0], dx, dg


def _neg_expm1(z):
    u = jnp.exp(z)
    safe = jnp.where(u == 1.0, 0.5, u)
    return -jnp.where(u == 1.0, z, (safe - 1.0) * z / jnp.log(safe))


def _lru_ab(xc, pr, pi, lam):
    r = jax.nn.sigmoid(pr)
    i = jax.nn.sigmoid(pi)
    log_a = -LRU_C * r * jax.nn.softplus(-lam)
    a = jnp.exp(log_a)
    b = jnp.sqrt(_neg_expm1(2.0 * log_a)) * (i * xc)
    return a, b


def _gated(h, gate):
    return h * jax.nn.gelu(gate)


def _row_iota8(w):
    return lax.broadcasted_iota(jnp.int32, (8, w), 0)


def _shift_dn(x, halo, s):
    xs = pltpu.roll(x, s, 0)
    hs = pltpu.roll(halo, s, 0)
    first = jnp.where(_row_iota8(x.shape[1]) < s, hs, xs[0:8])
    return jnp.concatenate([first, xs[8:]], axis=0) if x.shape[0] > 8 else first


def _shift_up(x, nxt, s):
    n = x.shape[0]
    xs = pltpu.roll(x, n - s, 0)
    ns = pltpu.roll(nxt, 8 - s, 0)
    last = jnp.where(_row_iota8(x.shape[1]) >= 8 - s, ns, xs[n - 8:])
    return jnp.concatenate([xs[:n - 8], last], axis=0) if n > 8 else last


def _lru_tiles(T):
    tT = min(256, T)
    return tT, T // tT


def _lru_fwd(proj, cw, cb, wax, bax, lam, name):
    T = proj.shape[0]
    W = MIX_W
    tT, nT = _lru_tiles(T)

    def body(x_ref, xh_ref, gt_ref, cw_ref, cb_ref, wax_ref, bax_ref, lam_ref, y_ref, h_ref, a_s, b_s, hc_s):
        i = pl.program_id(0)

        @pl.when(i == 0)
        def _():
            hc_s[...] = jnp.zeros_like(hc_s)

        x = x_ref[...]
        halo = jnp.where(i > 0, xh_ref[...], 0.0)
        w = cw_ref[...]
        xc = (cb_ref[...] + w[3:4] * x + w[2:3] * _shift_dn(x, halo, 1) + w[1:2] * _shift_dn(x, halo, 2)
              + w[0:1] * _shift_dn(x, halo, 3))
        pre = jnp.dot(xc.astype(BF16), wax_ref[...], preferred_element_type=F32) + bax_ref[...]
        a, b = _lru_ab(xc, pre[:, :W], pre[:, W:], lam_ref[...])
        a_s[...] = a
        b_s[...] = b
        row = _row_iota8(W)

        def grp(gi, hprev):
            r0 = pl.multiple_of(gi * 8, 8)
            A = a_s[pl.ds(r0, 8), :]
            B = b_s[pl.ds(r0, 8), :]
            for s in (1, 2, 4):
                As = pltpu.roll(A, s, 0)
                Bs = pltpu.roll(B, s, 0)
                m = row >= s
                B = jnp.where(m, A * Bs + B, B)
                A = jnp.where(m, A * As, A)
            H = A * hprev + B
            h_ref[pl.ds(r0, 8), :] = H
            return H[7:8, :]

        hc_s[0:1, :] = lax.fori_loop(0, tT // 8, grp, hc_s[0:1, :])
        y_ref[...] = _gated(h_ref[...], gt_ref[...]).astype(BF16)

    hb = tT // 8
    return pl.pallas_call(
        body, name=name,
        out_shape=(jax.ShapeDtypeStruct((T, W), BF16), jax.ShapeDtypeStruct((T, W), F32)),
        grid=(nT,),
        in_specs=[pl.BlockSpec((tT, W), lambda i: (i, 0)),
                  pl.BlockSpec((8, W), lambda i: (jnp.maximum(i * hb - 1, 0), 0)),
                  pl.BlockSpec((tT, W), lambda i: (i, 1)),
                  pl.BlockSpec((CONV_W, W), lambda i: (0, 0)), pl.BlockSpec((1, W), lambda i: (0, 0)),
                  pl.BlockSpec((W, 2 * W), lambda i: (0, 0)), pl.BlockSpec((1, 2 * W), lambda i: (0, 0)),
                  pl.BlockSpec((1, W), lambda i: (0, 0))],
        out_specs=(pl.BlockSpec((tT, W), lambda i: (i, 0)), pl.BlockSpec((tT, W), lambda i: (i, 0))),
        scratch_shapes=[pltpu.VMEM((tT, W), F32), pltpu.VMEM((tT, W), F32), pltpu.VMEM((8, W), F32)],
        compiler_params=_cparams(("arbitrary",)),
    )(proj, proj, proj, cw, cb, wax, bax, lam)


def _lru_bwd(proj, h, dy, cw, cb, wax, bax, lam, name):
    T = proj.shape[0]
    W = MIX_W
    tT, nT = _lru_tiles(T)
    hb = tT // 8

    def body(x_ref, xh_ref, gt_ref, h_ref, hh_ref, dy_ref, cw_ref, cb_ref, wax_ref, bax_ref, lam_ref,
             dx_ref, dgt_ref, dcw_ref, dcb_ref, dwax_ref, dbax_ref, dlam_ref,
             al_s, be_s, d_s, ca_s, cd_s, cx_s):
        i = pl.program_id(0)
        ib = nT - 1 - i

        @pl.when(i == 0)
        def _():
            for r in (ca_s, cd_s, cx_s, dcw_ref, dcb_ref, dwax_ref, dbax_ref, dlam_ref):
                r[...] = jnp.zeros_like(r)

        x = x_ref[...]
        halo = jnp.where(ib > 0, xh_ref[...], 0.0)
        w = cw_ref[...]
        x1, x2, x3 = _shift_dn(x, halo, 1), _shift_dn(x, halo, 2), _shift_dn(x, halo, 3)
        xc = cb_ref[...] + w[3:4] * x + w[2:3] * x1 + w[1:2] * x2 + w[0:1] * x3
        xcb = xc.astype(BF16)
        pre = jnp.dot(xcb, wax_ref[...], preferred_element_type=F32) + bax_ref[...]
        (a, _), pull_ab = jax.vjp(_lru_ab, xc, pre[:, :W], pre[:, W:], lam_ref[...])
        hv = h_ref[...]
        hprev = _shift_dn(hv, jnp.where(ib > 0, hh_ref[...], 0.0), 1)
        _, pull_y = jax.vjp(_gated, hv, gt_ref[...])
        dh_out, dgt = pull_y(dy_ref[...])
        dgt_ref[...] = dgt.astype(BF16)
        al_s[...] = _shift_up(a, ca_s[...], 1)
        be_s[...] = dh_out
        row = _row_iota8(W)
        ng = tT // 8

        def grp(k, dnext):
            r0 = pl.multiple_of((ng - 1 - k) * 8, 8)
            A = al_s[pl.ds(r0, 8), :]
            B = be_s[pl.ds(r0, 8), :]
            for s in (1, 2, 4):
                As = pltpu.roll(A, 8 - s, 0)
                Bs = pltpu.roll(B, 8 - s, 0)
                m = row < 8 - s
                B = jnp.where(m, A * Bs + B, B)
                A = jnp.where(m, A * As, A)
            Dg = A * dnext + B
            d_s[pl.ds(r0, 8), :] = Dg
            return Dg[0:1, :]

        lax.fori_loop(0, ng, grp, cd_s[0:1, :])
        Dv = d_s[...]
        dxc1, dpr, dpi, dlam = pull_ab((Dv * hprev, Dv))
        dpre = jnp.concatenate([dpr, dpi], axis=1)
        dpb = dpre.astype(BF16)
        dxc = dxc1 + lax.dot_general(dpb, wax_ref[...], (((1,), (1,)), ((), ())), preferred_element_type=F32)
        dwax_ref[...] += lax.dot_general(xcb, dpb, (((0,), (0,)), ((), ())), preferred_element_type=F32)
        dbax_ref[...] += jnp.sum(dpre, axis=0, keepdims=True)
        dlam_ref[...] += dlam
        dcb_ref[...] += jnp.sum(dxc, axis=0, keepdims=True)
        dcw_ref[...] += jnp.concatenate([jnp.sum(dxc * x3, axis=0, keepdims=True), jnp.sum(dxc * x2, axis=0, keepdims=True),
                                         jnp.sum(dxc * x1, axis=0, keepdims=True), jnp.sum(dxc * x, axis=0, keepdims=True)], axis=0)
        nxt = cx_s[...]
        dx = (w[3:4] * dxc + w[2:3] * _shift_up(dxc, nxt, 1) + w[1:2] * _shift_up(dxc, nxt, 2)
              + w[0:1] * _shift_up(dxc, nxt, 3))
        dx_ref[...] = dx.astype(BF16)
        ca_s[...] = a[0:8]
        cd_s[...] = Dv[0:8]
        cx_s[...] = dxc[0:8]

    rev = lambda i: nT - 1 - i
    const = lambda shape: pl.BlockSpec(shape, lambda i: (0, 0))
    return pl.pallas_call(
        body, name=name,
        out_shape=(jax.ShapeDtypeStruct((T, W), BF16), jax.ShapeDtypeStruct((T, W), BF16),
                   jax.ShapeDtypeStruct((CONV_W, W), F32), jax.ShapeDtypeStruct((1, W), F32),
                   jax.ShapeDtypeStruct((W, 2 * W), F32), jax.ShapeDtypeStruct((1, 2 * W), F32),
                   jax.ShapeDtypeStruct((1, W), F32)),
        grid=(nT,),
        in_specs=[pl.BlockSpec((tT, W), lambda i: (rev(i), 0)),
                  pl.BlockSpec((8, W), lambda i: (jnp.maximum(rev(i) * hb - 1, 0), 0)),
                  pl.BlockSpec((tT, W), lambda i: (rev(i), 1)),
                  pl.BlockSpec((tT, W), lambda i: (rev(i), 0)),
                  pl.BlockSpec((8, W), lambda i: (jnp.maximum(rev(i) * hb - 1, 0), 0)),
                  pl.BlockSpec((tT, W), lambda i: (rev(i), 0)),
                  const((CONV_W, W)), const((1, W)), const((W, 2 * W)), const((1, 2 * W)), const((1, W))],
        out_specs=(pl.BlockSpec((tT, W), lambda i: (rev(i), 0)), pl.BlockSpec((tT, W), lambda i: (rev(i), 0)),
                   const((CONV_W, W)), const((1, W)), const((W, 2 * W)), const((1, 2 * W)), const((1, W))),
        scratch_shapes=[pltpu.VMEM((tT, W), F32), pltpu.VMEM((tT, W), F32), pltpu.VMEM((tT, W), F32),
                        pltpu.VMEM((8, W), F32), pltpu.VMEM((8, W), F32), pltpu.VMEM((8, W), F32)],
        compiler_params=_cparams(("arbitrary",)),
    )(proj, proj, proj, h, h, dy, cw, cb, wax, bax, lam)


def _lru_prep(wa, wx, ba, bx):
    eye = jnp.eye(LRU_BLOCKS, dtype=F32)

    def dense(wb):
        return (wb[:, :, None, :] * eye[:, None, :, None]).reshape(MIX_W, MIX_W)

    wax = jnp.concatenate([dense(wa), dense(wx)], axis=1)
    bax = jnp.concatenate([ba, bx])[None, :]
    return wax, bax


_ATT_QC = 4
_ATT_Q = _ATT_QC * CHUNK
_ATT_KW = (ATT_LEFT + _ATT_QC) * CHUNK
_BVEC_W = _ATT_KW
_N_OFFS = CHUNK - 1 + ATT_BAND


def _attn_bias(rel_bias):
    n_far = ATT_LEFT * CHUNK - MAX_REL + CHUNK
    far = jnp.broadcast_to(rel_bias[:, 2 * MAX_REL:], (ATT_HEADS, n_far))
    near = rel_bias[:, MAX_REL - (CHUNK - 1):2 * MAX_REL][:, ::-1]
    pad = jnp.zeros((ATT_HEADS, _BVEC_W - _N_OFFS), F32)
    return jnp.concatenate([far, near, pad], axis=1)[:, None, :]


def _bias_table(bvec_row):
    return pltpu.roll(jnp.broadcast_to(bvec_row, (_ATT_Q, _BVEC_W)), _BVEC_W - (CHUNK - 1), 1, stride=1, stride_axis=0)


def _bias_table_t(ds):
    r = lax.broadcasted_iota(jnp.int32, (_ATT_Q, _ATT_Q), 0)
    c = lax.broadcasted_iota(jnp.int32, (_ATT_Q, _ATT_Q), 1)
    rev = jnp.dot((r + c == _ATT_Q - 1).astype(F32), ds, preferred_element_type=F32, precision=HI)
    back = pltpu.roll(rev, _BVEC_W - (_ATT_Q - CHUNK), 1, stride=1, stride_axis=0)
    return jnp.sum(back, axis=0, keepdims=True)


_NT = (((1,), (1,)), ((), ()))
_TN = (((0,), (0,)), ((), ()))
_ATT_PAD = ATT_LEFT * CHUNK
_Q_BLK, _K_BLK, _V_BLK = 2 * MIX_W // ATT_HD, 3 * MIX_W // ATT_HD, 4 * MIX_W // ATT_HD


def _in_band():
    first = (lax.broadcasted_iota(jnp.int32, (_ATT_Q, _ATT_KW), 0) // CHUNK) * CHUNK
    k = lax.broadcasted_iota(jnp.int32, (_ATT_Q, _ATT_KW), 1)
    return jnp.logical_and(k >= first, k < first + ATT_BAND)


def _attn_probs(q, kb, bias, in_band, b):
    s = lax.dot_general(q, kb, _NT, preferred_element_type=F32) * (ATT_HD ** -0.5) + bias
    kpos = lax.broadcasted_iota(jnp.int32, s.shape, 1)
    s = jnp.where(jnp.logical_and(in_band, kpos >= _ATT_PAD - b * _ATT_Q), s, MASK_VALUE)
    e = jnp.exp(s - jnp.max(s, axis=-1, keepdims=True))
    return e / jnp.sum(e, axis=-1, keepdims=True)


def _attn_fwd(proj, bias, name, side=None):
    T = proj.shape[0]
    assert T % _ATT_Q == 0
    nB = T // _ATT_Q

    def body(q_ref, k_ref, v_ref, b_ref, o_ref, kp, vp):
        kp[0:_ATT_PAD, :] = jnp.zeros((_ATT_PAD, ATT_HD), BF16)
        vp[0:_ATT_PAD, :] = jnp.zeros((_ATT_PAD, ATT_HD), BF16)
        kp[_ATT_PAD:, :] = k_ref[...].astype(BF16)
        vp[_ATT_PAD:, :] = v_ref[...].astype(BF16)
        bias_v = _bias_table(b_ref[0])
        band = _in_band()

        def step(b, carry):
            r0 = pl.multiple_of(b * _ATT_Q, _ATT_Q)
            q = q_ref[pl.ds(r0, _ATT_Q), :].astype(BF16)
            p = _attn_probs(q, kp[pl.ds(r0, _ATT_KW), :], bias_v, band, b)
            o = jnp.dot(p.astype(BF16), vp[pl.ds(r0, _ATT_KW), :], preferred_element_type=F32)
            o_ref[pl.ds(r0, _ATT_Q), :] = o.astype(BF16)
            return carry

        lax.fori_loop(0, nB, step, 0)

    (out,), got = _call(
        body, name=name, out_shape=[jax.ShapeDtypeStruct((T, MIX_W), BF16)], grid=(ATT_HEADS,),
        in_specs=[pl.BlockSpec((T, ATT_HD), lambda h: (0, _Q_BLK + h)), pl.BlockSpec((T, ATT_HD), lambda h: (0, _K_BLK + h)),
                  pl.BlockSpec((T, ATT_HD), lambda h: (0, _V_BLK + h)), pl.BlockSpec((1, 1, _BVEC_W), lambda h: (h, 0, 0))],
        out_specs=[pl.BlockSpec((T, ATT_HD), lambda h: (0, h))],
        scratch_shapes=[pltpu.VMEM((T + _ATT_PAD, ATT_HD), BF16), pltpu.VMEM((T + _ATT_PAD, ATT_HD), BF16)],
        sem=("arbitrary",), args=(proj, proj, proj, bias), side=side)
    return out if side is None else (out, got)


def _attn_bwd(proj, bias, do, name, side=None):
    T = proj.shape[0]
    assert T % _ATT_Q == 0
    nB = T // _ATT_Q

    def body(q_ref, k_ref, v_ref, b_ref, do_ref, dq_ref, dk_ref, dv_ref, db_ref, kp, vp, dkp, dvp, dbs):
        kp[0:_ATT_PAD, :] = jnp.zeros((_ATT_PAD, ATT_HD), BF16)
        vp[0:_ATT_PAD, :] = jnp.zeros((_ATT_PAD, ATT_HD), BF16)
        kp[_ATT_PAD:, :] = k_ref[...].astype(BF16)
        vp[_ATT_PAD:, :] = v_ref[...].astype(BF16)
        dkp[...] = jnp.zeros_like(dkp)
        dvp[...] = jnp.zeros_like(dvp)
        dbs[...] = jnp.zeros_like(dbs)
        bias_v = _bias_table(b_ref[0])
        band = _in_band()

        def step(b, carry):
            r0 = pl.multiple_of(b * _ATT_Q, _ATT_Q)
            q = q_ref[pl.ds(r0, _ATT_Q), :].astype(BF16)
            kb = kp[pl.ds(r0, _ATT_KW), :]
            vb = vp[pl.ds(r0, _ATT_KW), :]
            dob = do_ref[pl.ds(r0, _ATT_Q), :].astype(BF16)
            p = _attn_probs(q, kb, bias_v, band, b)
            dp = lax.dot_general(dob, vb, _NT, preferred_element_type=F32)
            ds = p * (dp - jnp.sum(p * dp, axis=-1, keepdims=True))
            dbs[...] += ds
            dsb = (ds * (ATT_HD ** -0.5)).astype(BF16)
            dq_ref[pl.ds(r0, _ATT_Q), :] = jnp.dot(dsb, kb, preferred_element_type=F32).astype(BF16)
            dkp[pl.ds(r0, _ATT_KW), :] += lax.dot_general(dsb, q, _TN, preferred_element_type=F32)
            dvp[pl.ds(r0, _ATT_KW), :] += lax.dot_general(p.astype(BF16), dob, _TN, preferred_element_type=F32)
            return carry

        lax.fori_loop(0, nB, step, 0)
        dk_ref[...] = dkp[_ATT_PAD:, :].astype(BF16)
        dv_ref[...] = dvp[_ATT_PAD:, :].astype(BF16)
        db_ref[0] = _bias_table_t(dbs[...])

    hspec = pl.BlockSpec((T, ATT_HD), lambda h: (0, h))
    osd = jax.ShapeDtypeStruct((T, MIX_W), BF16)
    outs, got = _call(
        body, name=name,
        out_shape=[osd, osd, osd, jax.ShapeDtypeStruct((ATT_HEADS, 1, _BVEC_W), F32)], grid=(ATT_HEADS,),
        in_specs=[pl.BlockSpec((T, ATT_HD), lambda h: (0, _Q_BLK + h)), pl.BlockSpec((T, ATT_HD), lambda h: (0, _K_BLK + h)),
                  pl.BlockSpec((T, ATT_HD), lambda h: (0, _V_BLK + h)), pl.BlockSpec((1, 1, _BVEC_W), lambda h: (h, 0, 0)),
                  hspec],
        out_specs=[hspec, hspec, hspec, pl.BlockSpec((1, 1, _BVEC_W), lambda h: (h, 0, 0))],
        scratch_shapes=[pltpu.VMEM((T + _ATT_PAD, ATT_HD), BF16), pltpu.VMEM((T + _ATT_PAD, ATT_HD), BF16),
                        pltpu.VMEM((T + _ATT_PAD, ATT_HD), F32), pltpu.VMEM((T + _ATT_PAD, ATT_HD), F32),
                        pltpu.VMEM((_ATT_Q, _ATT_KW), F32)],
        sem=("arbitrary",), args=(proj, proj, proj, bias, do), side=side)
    return outs if side is None else (outs, got)


def _s5_prep(a_re, a_im, b_re, b_im, c_re, c_im, d, log_step):
    step = jnp.exp(log_step)[:, None]
    mag = jnp.exp(a_re * step)
    ang = a_im * step
    lb_re = mag * jnp.cos(ang)
    lb_im = mag * jnp.sin(ang)
    den = a_re * a_re + a_im * a_im
    nr = lb_re - 1.0
    coef_re = (nr * a_re + lb_im * a_im) / den
    coef_im = (lb_im * a_re - nr * a_im) / den
    bb_re = coef_re[..., None] * b_re - coef_im[..., None] * b_im
    bb_im = coef_re[..., None] * b_im + coef_im[..., None] * b_re
    ks = jnp.arange(S5_L + 1, dtype=F32)[:, None, None]
    pmag = jnp.exp(ks * (a_re * step)[None])
    PR, PI = pmag * jnp.cos(ks * ang[None]), pmag * jnp.sin(ks * ang[None])
    cl_re = c_re[None] * PR[:, :, None, :] - c_im[None] * PI[:, :, None, :]
    cl_im = c_re[None] * PI[:, :, None, :] + c_im[None] * PR[:, :, None, :]
    cla_re = cl_re[:S5_L].transpose(1, 0, 2, 3).reshape(SSM_G, S5_LW, SSM_P)
    cla_im = cl_im[:S5_L].transpose(1, 0, 2, 3).reshape(SSM_G, S5_LW, SSM_P)
    bbt_re, bbt_im = bb_re.transpose(0, 2, 1), bb_im.transpose(0, 2, 1)
    prr, pir = PR[:S5_L][::-1], PI[:S5_L][::-1]
    bret = (prr[:, :, None, :] * bb_re.transpose(0, 2, 1)[None] - pir[:, :, None, :] * bb_im.transpose(0, 2, 1)[None])
    bimt = (prr[:, :, None, :] * bb_im.transpose(0, 2, 1)[None] + pir[:, :, None, :] * bb_re.transpose(0, 2, 1)[None])
    bret = bret.transpose(1, 0, 2, 3).reshape(SSM_G, S5_LW, SSM_P)
    bimt = bimt.transpose(1, 0, 2, 3).reshape(SSM_G, S5_LW, SSM_P)
    cre = cl_re[1:].transpose(1, 3, 0, 2).reshape(SSM_G, SSM_P, S5_LW)
    cim = (-cl_im[1:]).transpose(1, 3, 0, 2).reshape(SSM_G, SSM_P, S5_LW)
    dflat = jnp.broadcast_to(d.reshape(SSM_G, 1, SSM_H), (SSM_G, S5_L, SSM_H)).reshape(SSM_G, 1, S5_LW)
    bt = jnp.concatenate([bret, bimt], axis=2)
    ct = jnp.concatenate([cre, cim], axis=1)
    return cla_re, cla_im, bbt_re, bbt_im, bt, ct, PR[S5_L], PI[S5_L], dflat


_S5_GB = 8


def _bdot(a, b, dims):
    return lax.dot_general(a, b, dims, preferred_element_type=F32, precision=HI)


_B_NN = (((2,), (1,)), ((0,), (0,)))
_B_NT = (((2,), (2,)), ((0,), (0,)))
_B_TN = (((1,), (1,)), ((0,), (0,)))


def _gspec(shape):
    return pl.BlockSpec((_S5_GB,) + shape, lambda g: (g, 0, 0))


def _s5_tgt(cla_re, cla_im, bbt_re, bbt_im, name):
    n_g = cla_re.shape[0]

    def body(cr_ref, ci_ref, br_ref, bi_ref, t_ref):
        kt = _bdot(br_ref[...], cr_ref[...], _B_NT) - _bdot(bi_ref[...], ci_ref[...], _B_NT)
        lane = lax.broadcasted_iota(jnp.int32, (SSM_H, S5_LW), 1)
        for g in range(_S5_GB):
            for lp in range(S5_L):
                rows = kt[g] if lp == 0 else jnp.where(lane >= lp * SSM_H, pltpu.roll(kt[g], lp * SSM_H, 1), 0.0)
                t_ref[g, lp * SSM_H:(lp + 1) * SSM_H, :] = rows

    return pl.pallas_call(
        body, name=name, out_shape=jax.ShapeDtypeStruct((n_g, S5_LW, S5_LW), F32), grid=(n_g // _S5_GB,),
        in_specs=[_gspec((S5_LW, SSM_P)), _gspec((S5_LW, SSM_P)), _gspec((SSM_H, SSM_P)), _gspec((SSM_H, SSM_P))],
        out_specs=_gspec((S5_LW, S5_LW)), compiler_params=_cparams(("parallel",)),
    )(cla_re, cla_im, bbt_re, bbt_im)


def _s5_tgt_t(d_tgt, cla_re, cla_im, bbt_re, bbt_im, name):
    n_g = cla_re.shape[0]

    def body(dt_ref, cr_ref, ci_ref, br_ref, bi_ref, dcr_ref, dci_ref, dbr_ref, dbi_ref):
        lane = lax.broadcasted_iota(jnp.int32, (SSM_H, S5_LW), 1)
        dks = []
        for g in range(_S5_GB):
            dk = dt_ref[g, 0:SSM_H, :]
            for lp in range(1, S5_L):
                rows = dt_ref[g, lp * SSM_H:(lp + 1) * SSM_H, :]
                dk = dk + jnp.where(lane < S5_LW - lp * SSM_H, pltpu.roll(rows, S5_LW - lp * SSM_H, 1), 0.0)
            dks.append(dk)
        dkt = jnp.stack(dks)
        dbr_ref[...] = _bdot(dkt, cr_ref[...], _B_NN)
        dbi_ref[...] = -_bdot(dkt, ci_ref[...], _B_NN)
        dcr_ref[...] = _bdot(dkt, br_ref[...], _B_TN)
        dci_ref[...] = -_bdot(dkt, bi_ref[...], _B_TN)

    sd = jax.ShapeDtypeStruct
    return pl.pallas_call(
        body, name=name,
        out_shape=(sd((n_g, S5_LW, SSM_P), F32), sd((n_g, S5_LW, SSM_P), F32), sd((n_g, SSM_H, SSM_P), F32),
                   sd((n_g, SSM_H, SSM_P), F32)),
        grid=(n_g // _S5_GB,),
        in_specs=[_gspec((S5_LW, S5_LW)), _gspec((S5_LW, SSM_P)), _gspec((S5_LW, SSM_P)), _gspec((SSM_H, SSM_P)),
                  _gspec((SSM_H, SSM_P))],
        out_specs=(_gspec((S5_LW, SSM_P)), _gspec((S5_LW, SSM_P)), _gspec((SSM_H, SSM_P)), _gspec((SSM_H, SSM_P))),
        compiler_params=_cparams(("parallel",)),
    )(d_tgt, cla_re, cla_im, bbt_re, bbt_im)


_S5_SW = 2 * SSM_P


def _sspec(C):
    return pl.BlockSpec((C, _S5_GB, _S5_SW), lambda g: (0, g, 0))


def _group_major(s_ref):
    return jnp.stack([s_ref[:, k, :] for k in range(_S5_GB)])


def _chunk_major(o_ref, v):
    for k in range(_S5_GB):
        o_ref[:, k, :] = v[k]


def _rot_coefs(lr, li, sign):
    return jnp.concatenate([lr, lr], axis=1), jnp.concatenate([-sign * li, sign * li], axis=1)


def _s5_in(u, bt, name):
    C = u.shape[1]

    def body(u_ref, b_ref, s_ref):
        _chunk_major(s_ref, _bdot(u_ref[...], b_ref[...], _B_NN))

    return pl.pallas_call(
        body, name=name, out_shape=jax.ShapeDtypeStruct((C, SSM_G, _S5_SW), F32), grid=(SSM_G // _S5_GB,),
        in_specs=[_gspec((C, S5_LW)), _gspec((S5_LW, _S5_SW))], out_specs=_sspec(C),
        compiler_params=_cparams(("parallel",)),
    )(u, bt)


def _s5_scan(sin, lr, li, name):
    C = sin.shape[0]

    def body(i_ref, lr_ref, li_ref, o_ref):
        a, b = _rot_coefs(lr_ref[...], li_ref[...], 1.0)

        def step(c, s):
            o_ref[c] = s
            return a * s + b * pltpu.roll(s, SSM_P, 1) + i_ref[c]

        lax.fori_loop(0, C, step, jnp.zeros((SSM_G, _S5_SW), F32))

    return pl.pallas_call(body, name=name, out_shape=jax.ShapeDtypeStruct((C, SSM_G, _S5_SW), F32),
                          compiler_params=_cparams())(sin, lr, li)


def _s5_out(u, sp, tgt, ct, dflat, name):
    C = u.shape[1]

    def body(u_ref, s_ref, t_ref, c_ref, d_ref, pre_ref, y_ref):
        uv = u_ref[...]
        pre = _bdot(uv, t_ref[...], _B_NN) + _bdot(_group_major(s_ref), c_ref[...], _B_NN) + d_ref[...] * uv
        pre_ref[...] = pre
        y_ref[...] = jax.nn.gelu(pre)

    return pl.pallas_call(
        body, name=name,
        out_shape=(jax.ShapeDtypeStruct((SSM_G, C, S5_LW), F32), jax.ShapeDtypeStruct((SSM_G, C, S5_LW), F32)),
        grid=(SSM_G // _S5_GB,),
        in_specs=[_gspec((C, S5_LW)), _sspec(C), _gspec((S5_LW, S5_LW)), _gspec((_S5_SW, S5_LW)), _gspec((1, S5_LW))],
        out_specs=(_gspec((C, S5_LW)), _gspec((C, S5_LW))), compiler_params=_cparams(("parallel",)),
    )(u, sp, tgt, ct, dflat)


def _s5_bwd_out(dy, pre, u, sp, ct, name):
    C = u.shape[1]

    def body(dy_ref, pre_ref, u_ref, s_ref, c_ref, dpre_ref, ds_ref, dt_ref, dc_ref, dd_ref):
        _, pull = jax.vjp(jax.nn.gelu, pre_ref[...])
        dpre = pull(dy_ref[...])[0]
        uv = u_ref[...]
        dpre_ref[...] = dpre
        _chunk_major(ds_ref, _bdot(dpre, c_ref[...], _B_NT))
        dt_ref[...] = _bdot(uv, dpre, _B_TN)
        dc_ref[...] = _bdot(_group_major(s_ref), dpre, _B_TN)
        dd_ref[...] = jnp.sum(dpre * uv, axis=1, keepdims=True)

    sd = jax.ShapeDtypeStruct
    return pl.pallas_call(
        body, name=name,
        out_shape=(sd((SSM_G, C, S5_LW), F32), sd((C, SSM_G, _S5_SW), F32), sd((SSM_G, S5_LW, S5_LW), F32),
                   sd((SSM_G, _S5_SW, S5_LW), F32), sd((SSM_G, 1, S5_LW), F32)),
        grid=(SSM_G // _S5_GB,),
        in_specs=[_gspec((C, S5_LW)), _gspec((C, S5_LW)), _gspec((C, S5_LW)), _sspec(C), _gspec((_S5_SW, S5_LW))],
        out_specs=(_gspec((C, S5_LW)), _sspec(C), _gspec((S5_LW, S5_LW)), _gspec((_S5_SW, S5_LW)), _gspec((1, S5_LW))),
        compiler_params=_cparams(("parallel",)),
    )(dy, pre, u, sp, ct)


def _s5_rscan(dsp, sp, lr, li, name):
    C = dsp.shape[0]

    def body(g_ref, s_ref, lr_ref, li_ref, o_ref, dlr_ref, dli_ref):
        a, b = _rot_coefs(lr_ref[...], li_ref[...], -1.0)

        def step(k, carry):
            c = C - 1 - k
            d, acc_same, acc_swap = carry
            o_ref[c] = d
            s = s_ref[c]
            return (g_ref[c] + a * d + b * pltpu.roll(d, SSM_P, 1), acc_same + d * s,
                    acc_swap + d * pltpu.roll(s, SSM_P, 1))

        z = jnp.zeros((SSM_G, _S5_SW), F32)
        _, same, swap = lax.fori_loop(0, C, step, (z, z, z))
        dlr_ref[...] = same[:, :SSM_P] + same[:, SSM_P:]
        dli_ref[...] = swap[:, SSM_P:] - swap[:, :SSM_P]

    sd = jax.ShapeDtypeStruct((C, SSM_G, _S5_SW), F32)
    sp_ = jax.ShapeDtypeStruct((SSM_G, SSM_P), F32)
    return pl.pallas_call(body, name=name, out_shape=(sd, sp_, sp_), compiler_params=_cparams())(dsp, sp, lr, li)


def _s5_bwd_in(dpre, dsin, u, tgt, bt, dflat, name):
    C = u.shape[1]

    def body(dp_ref, ds_ref, u_ref, t_ref, b_ref, d_ref, du_ref, db_ref):
        dp = dp_ref[...]
        ds = _group_major(ds_ref)
        du_ref[...] = _bdot(dp, t_ref[...], _B_NT) + _bdot(ds, b_ref[...], _B_NT) + d_ref[...] * dp
        db_ref[...] = _bdot(u_ref[...], ds, _B_TN)

    sd = jax.ShapeDtypeStruct
    return pl.pallas_call(
        body, name=name,
        out_shape=(sd((SSM_G, C, S5_LW), F32), sd((SSM_G, S5_LW, _S5_SW), F32)),
        grid=(SSM_G // _S5_GB,),
        in_specs=[_gspec((C, S5_LW)), _sspec(C), _gspec((C, S5_LW)), _gspec((S5_LW, S5_LW)), _gspec((S5_LW, _S5_SW)),
                  _gspec((1, S5_LW))],
        out_specs=(_gspec((C, S5_LW)), _gspec((S5_LW, _S5_SW))),
        compiler_params=_cparams(("parallel",)),
    )(dpre, dsin, u, tgt, bt, dflat)


_LANES = 128
_GROUPS_PER_TILE = _LANES // SSM_H


def _to_chunks(v, col0, name):
    T = v.shape[0]
    C = T // S5_L

    def body(x_ref, o_ref):
        for l in range(S5_L):
            xl = x_ref[pl.ds(l, C, stride=S5_L), :]
            for k in range(_GROUPS_PER_TILE):
                o_ref[k, :, l * SSM_H:(l + 1) * SSM_H] = xl[:, k * SSM_H:(k + 1) * SSM_H]

    return pl.pallas_call(
        body, name=name, out_shape=jax.ShapeDtypeStruct((SSM_G, C, S5_LW), F32), grid=(SSM_G // _GROUPS_PER_TILE,),
        in_specs=[pl.BlockSpec((T, _LANES), lambda g: (0, col0 // _LANES + g))],
        out_specs=pl.BlockSpec((_GROUPS_PER_TILE, C, S5_LW), lambda g: (g, 0, 0)),
        compiler_params=_cparams(("parallel",)),
    )(v)


def _from_chunks(v, out_dtype, name):
    C = v.shape[1]
    T = C * S5_L

    def body(f_ref, o_ref, rows):
        for l in range(S5_L):
            rows[pl.ds(l, C, stride=S5_L), :] = jnp.concatenate(
                [f_ref[k, :, l * SSM_H:(l + 1) * SSM_H] for k in range(_GROUPS_PER_TILE)], axis=1)
        o_ref[...] = rows[...].astype(out_dtype)

    return pl.pallas_call(
        body, name=name, out_shape=jax.ShapeDtypeStruct((T, MIX_W), out_dtype), grid=(SSM_G // _GROUPS_PER_TILE,),
        in_specs=[pl.BlockSpec((_GROUPS_PER_TILE, C, S5_LW), lambda g: (g, 0, 0))],
        out_specs=pl.BlockSpec((T, _LANES), lambda g: (0, g)),
        scratch_shapes=[pltpu.VMEM((T, _LANES), F32)],
        compiler_params=_cparams(("parallel",)),
    )(v)


def _merge_fn(bra, brb, pc, pg, g0, g1, g2, b0, b1, b2):
    sg = jax.nn.sigmoid
    return sg(g0 + b0) * bra + sg(g1 + b1) * brb + sg(g2 + b2) * (pc * sg(pg))


_EW_CS = 256
_GATE_OFF = 6 * MIX_W


def _merge_rows(br4, proj):
    return [(b, 0, D_MODEL) for b in br4] + [(proj, _GATE_OFF + k * D_MODEL, D_MODEL) for k in range(3)]


def _merge_fwd(br4, proj, gb3, name):
    def fn(rp, cp):
        b = cp[0]
        return [_merge_fn(*rp, b[0:1], b[1:2], b[2:3])], []

    return _rowwise(fn, _merge_rows(br4, proj), [gb3], [(D_MODEL, BF16)], [], R=proj.shape[0], name=name, cs=_EW_CS)[0]


def _merge_bwd(br4, proj, gb3, dm, name):
    def fn(rp, cp):
        b = cp[0]
        shp = rp[0].shape
        bs = [jnp.broadcast_to(b[k:k + 1], shp) for k in range(3)]
        _, pull = jax.vjp(_merge_fn, *rp[:7], *bs)
        g = pull(rp[7])
        return list(g[:7]), list(g[7:])

    rows = _merge_rows(br4, proj) + [(dm, 0, D_MODEL)]
    outs = _rowwise(fn, rows, [gb3], [(D_MODEL, BF16)] * 7, [D_MODEL] * 3, R=proj.shape[0], name=name, tm=128, cs=_EW_CS)
    return outs[:4], outs[4:7], jnp.concatenate(outs[7:], axis=0)


def _swiglu(g, u):
    return jax.nn.silu(g) * u


def _act_fwd(gu, name):
    def fn(rp, cp):
        return [_swiglu(*rp)], []

    return _rowwise(fn, [(gu[0], 0, FFN_H), (gu[1], 0, FFN_H)], [], [(FFN_H, BF16)], [], R=gu[0].shape[0], name=name, cs=_EW_CS)[0]


def _act_bwd(gu, dact, name):
    def fn(rp, cp):
        _, pull = jax.vjp(_swiglu, rp[0], rp[1])
        return list(pull(rp[2])), []

    return _rowwise(fn, [(gu[0], 0, FFN_H), (gu[1], 0, FFN_H), (dact, 0, FFN_H)], [], [(FFN_H, BF16)] * 2, [],
                    R=dact.shape[0], name=name, tm=128, cs=_EW_CS)


def _adamw_fn(w, g, m, v):
    m = ADAM_B1 * m + (1.0 - ADAM_B1) * g
    v = ADAM_B2 * v + (1.0 - ADAM_B2) * jnp.square(g)
    m_hat = m / (1.0 - ADAM_B1 ** ADAM_STEP)
    v_hat = v / (1.0 - ADAM_B2 ** ADAM_STEP)
    delta = -ADAM_LR * (m_hat / (jnp.sqrt(v_hat) + ADAM_EPS) + ADAM_WD * w)
    return delta, m, v


def _adamw(w, g, m, v, name):
    R, C = w.shape

    def fn(rp, cp):
        return list(_adamw_fn(*rp)), []

    cs = _pick(C, (512, 256, 128))
    tm = _pick(R, (256, 128, 64, 32, 16, 8))
    return _rowwise(fn, [(w, 0, C), (g, 0, C), (m, 0, C), (v, 0, C)], [], [(C, F32)] * 3, [], R=R, name=name,
                    tm=tm, rs=8, cs=cs)


def _my_place():
    return lax.axis_index("x"), lax.axis_index("y"), lax.axis_index("c")


def _other_chips(x, y):
    return [(1 - x, y), (x, 1 - y), (1 - x, 1 - y)]


_ANY = pl.BlockSpec(memory_space=pl.ANY)


def _rcopy(src, dst, ssem, rsem, to):
    return pltpu.make_async_remote_copy(src_ref=src, dst_ref=dst, send_sem=ssem, recv_sem=rsem, device_id=to,
                                        device_id_type=MESH)


def _place_shard(local, axis, jidx, out_dtype, name):
    lead, r, c = local.shape
    shp = [lead, r, c]
    shp[axis] *= N_CHIPS
    tr = _pick(r, (512, 256, 128)) if r % 128 == 0 else r
    nr = r // tr
    omap = (lambda l, i, j: (l, i, j[0])) if axis == 2 else (lambda l, i, j: (l, j[0] * nr + i, 0))

    def body(j_ref, x_ref, o_ref):
        o_ref[...] = x_ref[...].astype(out_dtype)

    return pl.pallas_call(
        body, name=name, out_shape=jax.ShapeDtypeStruct(tuple(shp), out_dtype),
        grid_spec=pltpu.PrefetchScalarGridSpec(
            num_scalar_prefetch=1, grid=(lead, nr),
            in_specs=[pl.BlockSpec((None, tr, c), lambda l, i, j: (l, i, 0))],
            out_specs=pl.BlockSpec((None, tr, c), omap)),
        compiler_params=_cparams(("parallel", "parallel")),
    )(jidx, local)


class _GatherSide:
    def __init__(self, fulls, axes, regions):
        self.inputs = list(fulls)
        self.out_shapes = [jax.ShapeDtypeStruct(f.shape, f.dtype) for f in fulls]
        self.aliases = {t: t for t in range(len(fulls))}
        self.n_sems = 6 * len(fulls)
        self.axes, self.regions = list(axes), list(regions)

    def _block(self, outs, t, chip, half):
        start, size, split = self.regions[t][:3]
        piece, n_pieces = self.regions[t][3] if len(self.regions[t]) > 3 else (0, 1)
        ax = self.axes[t]
        cut = outs[t].shape[ax] // N_CHIPS
        j = 2 * chip[0] + chip[1]
        idx = [pl.ds(start, size), slice(None), slice(None)]
        idx[ax] = pl.ds(j * cut, cut)
        if split == "lead":
            idx[0] = pl.ds(start + half * (size // 2), size // 2)
        else:
            other = 3 - ax
            h = outs[t].shape[other] // (2 * n_pieces)
            idx[other] = pl.ds((half * n_pieces + piece) * h, h)
        return outs[t].at[tuple(idx)]

    def _sends(self, outs, send, recv):
        x, y, c = _my_place()
        cps = []
        for t in range(len(outs)):
            mine = self._block(outs, t, (x, y), c)
            for r, chip in enumerate(_other_chips(x, y)):
                k = 3 * t + r
                cps.append(_rcopy(mine, mine, send.at[k], recv.at[k], (*chip, c)))
        return cps

    def start(self, ins, outs, send, recv):
        for cp in self._sends(outs, send, recv):
            cp.start()

    def finish(self, ins, outs, send, recv):
        x, y, c = _my_place()
        sib = (x, y, 1 - c)
        n = len(outs)
        chips = _other_chips(x, y)
        passed = []
        for t in range(n):
            for r, chip in enumerate(chips):
                k = 3 * t + r
                landed = self._block(outs, t, chip, c)
                _rcopy(landed, landed, send.at[k], recv.at[k], (*chip, c)).wait_recv()
                cp = _rcopy(landed, landed, send.at[3 * n + k], recv.at[3 * n + k], sib)
                cp.start()
                passed.append(cp)
        for t in range(n):
            for r, chip in enumerate(chips):
                k = 3 * n + 3 * t + r
                theirs = self._block(outs, t, chip, 1 - c)
                _rcopy(theirs, theirs, send.at[k], recv.at[k], sib).wait_recv()
        for cp in self._sends(outs, send, recv) + passed:
            cp.wait_send()


def _run_side(side, name):
    s_in = len(side.inputs)

    def body(*refs):
        ins, outs = refs[:s_in], refs[s_in:s_in + len(side.out_shapes)]
        send, recv = refs[s_in + len(side.out_shapes):]
        side.start(ins, outs, send, recv)
        side.finish(ins, outs, send, recv)

    return pl.pallas_call(
        body, name=name, out_shape=tuple(side.out_shapes), in_specs=[_ANY] * s_in,
        out_specs=tuple([_ANY] * len(side.out_shapes)), input_output_aliases=dict(side.aliases),
        scratch_shapes=[pltpu.SemaphoreType.DMA((side.n_sems,)), pltpu.SemaphoreType.DMA((side.n_sems,))],
    )(*side.inputs)


def _half_idx(shape, axis, half):
    size = shape[axis] // 2
    idx = [slice(None), slice(None)]
    idx[axis] = pl.ds(half * size, size)
    return tuple(idx)


class _PairSide:
    def __init__(self, grads, half_axes):
        self.inputs = list(grads)
        self.half_axes = list(half_axes)
        self.out_shapes = []
        for g, ax in zip(grads, half_axes):
            shp = list(g.shape)
            shp[ax] //= 2
            self.out_shapes.append(jax.ShapeDtypeStruct(tuple(shp), g.dtype))
        self.aliases = {}
        self.n_sems = len(grads)

    def _copies(self, srcs, outs, send, recv):
        x, y, c = _my_place()
        return [_rcopy(srcs[t].at[_half_idx(srcs[t].shape, self.half_axes[t], 1 - c)], outs[t], send.at[t], recv.at[t],
                       (x, y, 1 - c)) for t in range(len(srcs))]

    def start(self, srcs, outs, send, recv):
        for cp in self._copies(srcs, outs, send, recv):
            cp.start()

    def finish(self, srcs, outs, send, recv):
        for cp in self._copies(srcs, outs, send, recv):
            cp.wait()


def _pair_sum(g, recv, half_axis, cidx, name):
    K, N = recv.shape
    tm = _pick(K, (256, 128, 64, 32, 16))
    tn = _pick(N, (1024, 1408, 512, 256, 128))
    nbr, nbc = K // tm, N // tn
    if half_axis == 0:
        gmap = lambda i, j, c: (c[0] * nbr + i, j)
    else:
        gmap = lambda i, j, c: (i, c[0] * nbc + j)

    def body(c_ref, g_ref, r_ref, of_ref, ob_ref):
        s = g_ref[...] + r_ref[...]
        of_ref[...] = s
        ob_ref[...] = s.astype(BF16)

    omap = lambda i, j, c: (i, j)
    return pl.pallas_call(
        body, name=name,
        out_shape=(jax.ShapeDtypeStruct((K, N), F32), jax.ShapeDtypeStruct((K, N), BF16)),
        grid_spec=pltpu.PrefetchScalarGridSpec(
            num_scalar_prefetch=1, grid=(nbr, nbc),
            in_specs=[pl.BlockSpec((tm, tn), gmap), pl.BlockSpec((tm, tn), omap)],
            out_specs=(pl.BlockSpec((tm, tn), omap), pl.BlockSpec((tm, tn), omap))),
        compiler_params=_cparams(("parallel", "parallel")),
    )(cidx, g, recv)


def _shard_idx(shape, axis, j):
    size = shape[axis] // N_CHIPS
    idx = [slice(None), slice(None)]
    idx[axis] = pl.ds(j * size, size)
    return tuple(idx)


class _ScatterSide:
    def __init__(self, parts, shard_axes):
        self.inputs = list(parts)
        self.shard_axes = list(shard_axes)
        self.out_shapes = []
        for p, ax in zip(parts, shard_axes):
            shp = list(p.shape)
            shp[ax] //= N_CHIPS
            self.out_shapes.append(jax.ShapeDtypeStruct((3,) + tuple(shp), p.dtype))
        self.aliases = {}
        self.n_sems = 3 * len(parts)

    def _copies(self, srcs, outs, send, recv):
        x, y, c = _my_place()
        cps = []
        for t in range(len(srcs)):
            for r, chip in enumerate(_other_chips(x, y)):
                k = 3 * t + r
                j = 2 * chip[0] + chip[1]
                cps.append(_rcopy(srcs[t].at[_shard_idx(srcs[t].shape, self.shard_axes[t], j)], outs[t].at[r],
                                  send.at[k], recv.at[k], (*chip, c)))
        return cps

    def start(self, srcs, outs, send, recv):
        for cp in self._copies(srcs, outs, send, recv):
            cp.start()

    def finish(self, srcs, outs, send, recv):
        for cp in self._copies(srcs, outs, send, recv):
            cp.wait()


def _shard_sum(pf, recv, acc, layer, shard_axis, jcidx, name):
    _, K, N = recv.shape
    tm = _pick(K, (256, 128, 64, 32, 16))
    tn = _pick(N, (1024, 1408, 512, 256, 128))
    nbr, nbc = K // tm, N // tn
    if shard_axis == 0:
        pmap = lambda i, j, s: (s[0] * nbr + i, j)
        omap = lambda i, j, s: (layer, i, s[1] * nbc + j)
    else:
        pmap = lambda i, j, s: (i, s[0] * nbc + j)
        omap = lambda i, j, s: (layer, s[1] * nbr + i, j)

    def body(j_ref, p_ref, r_ref, a_ref, o_ref):
        o_ref[...] = ((p_ref[...] + r_ref[0].astype(F32)) + r_ref[1].astype(F32)) + r_ref[2].astype(F32)

    return pl.pallas_call(
        body, name=name, out_shape=jax.ShapeDtypeStruct(acc.shape, F32),
        grid_spec=pltpu.PrefetchScalarGridSpec(
            num_scalar_prefetch=1, grid=(nbr, nbc),
            in_specs=[pl.BlockSpec((tm, tn), pmap), pl.BlockSpec((3, tm, tn), lambda i, j, s: (0, i, j)), _ANY],
            out_specs=pl.BlockSpec((None, tm, tn), omap)),
        input_output_aliases={3: 0},
        compiler_params=_cparams(("parallel", "parallel")),
    )(jcidx, pf, recv, acc)


def _pair_join(accs, half_axes, name):
    n = len(accs)

    def body(*refs):
        outs = refs[n:2 * n]
        send_sems, recv_sems = refs[2 * n:]
        x, y, c = _my_place()
        sib = (x, y, 1 - c)

        def half(t, hc):
            return outs[t].at[(slice(None),) + _half_idx(outs[t].shape[1:], half_axes[t], hc)]

        cps = []
        for t in range(n):
            cp = _rcopy(half(t, c), half(t, c), send_sems.at[t], recv_sems.at[t], sib)
            cp.start()
            cps.append(cp)
        for t in range(n):
            _rcopy(half(t, 1 - c), half(t, 1 - c), send_sems.at[t], recv_sems.at[t], sib).wait_recv()
        for cp in cps:
            cp.wait_send()

    return pl.pallas_call(
        body, name=name, out_shape=tuple(jax.ShapeDtypeStruct(a.shape, a.dtype) for a in accs),
        in_specs=[_ANY] * n, out_specs=tuple([_ANY] * n), input_output_aliases={t: t for t in range(n)},
        scratch_shapes=[pltpu.SemaphoreType.DMA((n,)), pltpu.SemaphoreType.DMA((n,))],
    )(*accs)


_N_DEV = 8


def _allreduce_small(flat, name):
    _, R, _ = flat.shape

    def body(in_ref, out_ref, stage, send1, recv1, send2, recv2):
        x, y, c = _my_place()
        me = 4 * x + 2 * y + c
        places = [(px, py, pc) for px in range(2) for py in range(2) for pc in range(2)]
        def peer(r):
            return (x ^ (r >> 2), y ^ ((r >> 1) & 1), c ^ (r & 1))

        def peer_id(r):
            p = peer(r)
            return 4 * p[0] + 2 * p[1] + p[2]

        stage[0] = in_ref[me]
        cps = []
        for r in range(1, _N_DEV):
            cp = _rcopy(in_ref.at[peer_id(r)], stage.at[r], send1.at[r], recv1.at[r], peer(r))
            cp.start()
            cps.append(cp)
        for cp in cps:
            cp.wait()
        tot = jnp.zeros((R, 128), F32)
        for d in range(_N_DEV):
            tot = tot + stage[me ^ d]
        out_ref[me] = tot
        cps = []
        for r in range(1, _N_DEV):
            cp = _rcopy(out_ref.at[me], out_ref.at[me], send2.at[r], recv2.at[r], peer(r))
            cp.start()
            cps.append(cp)
        for r in range(1, _N_DEV):
            _rcopy(out_ref.at[peer_id(r)], out_ref.at[peer_id(r)], send2.at[r], recv2.at[r], peer(r)).wait_recv()
        for cp in cps:
            cp.wait_send()

    vm = pl.BlockSpec(memory_space=pltpu.VMEM)
    return pl.pallas_call(
        body, name=name, out_shape=jax.ShapeDtypeStruct(flat.shape, F32), in_specs=[vm], out_specs=vm,
        scratch_shapes=[pltpu.VMEM(flat.shape, F32)] + [pltpu.SemaphoreType.DMA((_N_DEV,))] * 4,
        compiler_params=_cparams(),
    )(flat)


_BIG = ("w_in", "ssm_w_glu", "w_branch", "w_out", "w_ffn_gate", "w_ffn_up", "w_ffn_down")
_BIG_SHARD_AXIS = {"w_in": 1, "ssm_w_glu": 1, "w_branch": 1, "w_out": 0, "w_ffn_gate": 1, "w_ffn_up": 1, "w_ffn_down": 0}
_SMALL = ("norm_mix_g", "gate_bias", "lru_conv_w", "lru_conv_b", "lru_wa", "lru_ba", "lru_wx", "lru_bx", "lru_lambda",
          "attn_rel_bias", "ssm_a_re", "ssm_a_im", "ssm_b_re", "ssm_b_im", "ssm_c_re", "ssm_c_im", "ssm_d",
          "ssm_log_step", "norm_ffn_g", "norm_final_g")
_SMALL_SHARDED = {"gate_bias": 2, "lru_conv_w": 2}
_WEIGHTS = ("norm_mix_g", "w_in", "gate_bias", "lru_conv_w", "lru_conv_b", "lru_wa", "lru_ba", "lru_wx", "lru_bx",
            "lru_lambda", "attn_rel_bias", "ssm_a_re", "ssm_a_im", "ssm_b_re", "ssm_b_im", "ssm_c_re", "ssm_c_im",
            "ssm_d", "ssm_log_step", "ssm_w_glu", "w_branch", "w_out", "norm_ffn_g", "w_ffn_gate", "w_ffn_up",
            "w_ffn_down", "norm_final_g")


def _carried(comm, phase, l, key, W, fn, *args, **kw):
    side = comm.side(phase, l, key, W)
    if side is None:
        return fn(*args, **kw)
    out, got = fn(*args, side=side, **kw)
    comm.took(phase, l, key, got, W)
    return out


def _layer_fwd(l, x, W, sm, comm):
    T = x.shape[0]
    nm = lambda s: f"{s}"
    h1 = _rms_fwd(x, sm["norm_mix_g"][l][None, :], nm("rms_fwd"))
    proj = _carried(comm, "fwd", l, "mm_in", W, _mm, h1, W["w_in"], M=T, N=IN_W, K=D_MODEL, b_lead=l, name=nm("mm_in"))
    wax, bax = sm["lru_prep"][l]
    cw, cb, lam = sm["lru_conv_w"][l], sm["lru_conv_b"][l][None, :], sm["lru_lambda"][l][None, :]
    y_a, hst = _lru_fwd(proj, cw, cb, wax, bax, lam, nm("lru_fwd"))
    bias = sm["attn_bias"][l]
    y_b = _carried(comm, "fwd", l, "attn_fwd", W, _attn_fwd, proj, bias, nm("attn_fwd"))
    tgt, bt, ct, lr, li, dflat = sm["s5_prep"][l]
    u = _to_chunks(proj, 5 * MIX_W, nm("to_chunks"))
    sp = _s5_scan(_s5_in(u, bt, nm("s5_in")), lr, li, nm("s5_scan"))
    pre, ycf = _s5_out(u, sp, tgt, ct, dflat, nm("s5_out"))
    y_c = _from_chunks(ycf, BF16, nm("from_chunks"))
    brs = []
    for k, yk in enumerate((y_a, y_b, y_c)):
        brs.append(_mm(yk, W["w_branch"], M=T, N=D_MODEL, K=MIX_W, b_lead=3 * l + k, out_dtype=BF16, name=nm("mm_branch")))
    brs.append(_mm(y_c, W["ssm_w_glu"], M=T, N=D_MODEL, K=MIX_W, b_lead=l, out_dtype=BF16, name=nm("mm_branch")))
    br4 = tuple(brs)
    gb3 = sm["gate_bias"][l]
    merged = _merge_fwd(br4, proj, gb3, nm("merge_fwd"))
    x1 = _mm(merged, W["w_out"], M=T, N=D_MODEL, K=D_MODEL, b_lead=l, res=x, name=nm("mm_out"))
    h2 = _rms_fwd(x1, sm["norm_ffn_g"][l][None, :], nm("rms_fwd"))
    gpre = _carried(comm, "fwd", l, "mm_ffn_gate", W, _mm, h2, W["w_ffn_gate"], M=T, N=FFN_H, K=D_MODEL, b_lead=l,
                    out_dtype=BF16, name=nm("mm_ffn_up"))
    upre = _carried(comm, "fwd", l, "mm_ffn_up", W, _mm, h2, W["w_ffn_up"], M=T, N=FFN_H, K=D_MODEL, b_lead=l,
                    out_dtype=BF16, name=nm("mm_ffn_up"))
    gu = (gpre, upre)
    act = _act_fwd(gu, nm("act_fwd"))
    x2 = _carried(comm, "fwd", l, "mm_down", W, _mm, act, W["w_ffn_down"], M=T, N=D_MODEL, K=FFN_H, b_lead=l, res=x1,
                  name=nm("mm_down"))
    saved = dict(x=x, h1=h1, proj=proj, hst=hst, y_a=y_a, y_b=y_b, y_c=y_c, u=u, sp=sp, pre=pre,
                 br4=br4, merged=merged, x1=x1, h2=h2, gu=gu, act=act)
    return x2, saved


def _layer_bwd(l, dx2, sv, W, sm, comm):
    T = dx2.shape[0]
    nm = lambda s: f"{s}"
    big, small = {}, {}
    dxb = dx2.astype(BF16)
    big["w_ffn_down"] = _carried(comm, "bwd", l, "mm_dw_down", W, _mm, sv["act"], dxb, M=FFN_H, N=D_MODEL, K=T, ta=True,
                                 name=nm("mm_dw_down"))
    dact = _carried(comm, "bwd", l, "mm_dact", W, _mm, dxb, W["w_ffn_down"], M=T, N=FFN_H, K=D_MODEL, tb=True, b_lead=l,
                    out_dtype=BF16, name=nm("mm_dact"))
    dg, du = _act_bwd(sv["gu"], dact, nm("act_bwd"))
    big["w_ffn_gate"] = _carried(comm, "bwd", l, "mm_dw_gate", W, _mm, sv["h2"], dg, M=D_MODEL, N=FFN_H, K=T, ta=True,
                                 name=nm("mm_dw_up"))
    big["w_ffn_up"] = _mm(sv["h2"], du, M=D_MODEL, N=FFN_H, K=T, ta=True, name=nm("mm_dw_up"))
    dh2 = _mm(dg, W["w_ffn_gate"], M=T, N=D_MODEL, K=FFN_H, tb=True, b_lead=l, name=nm("mm_dh2"))
    dh2 = _mm(du, W["w_ffn_up"], M=T, N=D_MODEL, K=FFN_H, tb=True, b_lead=l, res=dh2, name=nm("mm_dh2r"))
    dx1, dgn = _rms_bwd(sv["x1"], sm["norm_ffn_g"][l][None, :], dh2, dx2, nm("rms_bwd"))
    small["norm_ffn_g"] = dgn[0]
    dx1b = dx1.astype(BF16)
    big["w_out"] = _mm(sv["merged"], dx1b, M=D_MODEL, N=D_MODEL, K=T, ta=True, name=nm("mm_dw_out"))
    dm = _mm(dx1b, W["w_out"], M=T, N=D_MODEL, K=D_MODEL, tb=True, b_lead=l, out_dtype=BF16, name=nm("mm_dmerged"))
    dbr, dgates, dgb = _merge_bwd(sv["br4"], sv["proj"], sm["gate_bias"][l], dm, nm("merge_bwd"))
    small["gate_bias"] = dgb
    ys = (sv["y_a"], sv["y_b"], sv["y_c"])
    big["w_branch"] = [_mm(ys[k], dbr[k], M=MIX_W, N=D_MODEL, K=T, ta=True, name=nm("mm_dw_branch")) for k in range(3)]
    big["ssm_w_glu"] = _mm(sv["y_c"], dbr[3], M=MIX_W, N=D_MODEL, K=T, ta=True, name=nm("mm_dw_branch"))
    dya = _mm(dbr[0], W["w_branch"], M=T, N=MIX_W, K=D_MODEL, tb=True, b_lead=3 * l, name=nm("mm_dy"))
    dyb = _mm(dbr[1], W["w_branch"], M=T, N=MIX_W, K=D_MODEL, tb=True, b_lead=3 * l + 1, out_dtype=BF16, name=nm("mm_dy"))
    dyc = _mm(dbr[2], W["w_branch"], M=T, N=MIX_W, K=D_MODEL, tb=True, b_lead=3 * l + 2, name=nm("mm_dy"))
    dyc = _mm(dbr[3], W["ssm_w_glu"], M=T, N=MIX_W, K=D_MODEL, tb=True, b_lead=l, res=dyc, name=nm("mm_dyr"))
    tgt, bt, ct, lr, li, dflat = sm["s5_prep"][l]
    dpre, dsp, d_tgt, d_ct, d_dflat = _s5_bwd_out(_to_chunks(dyc, 0, nm("to_chunks")), sv["pre"], sv["u"], sv["sp"], ct,
                                                  nm("s5_bwd_out"))
    dsin, d_lr, d_li = _s5_rscan(dsp, sv["sp"], lr, li, nm("s5_rscan"))
    du_f, d_bt = _s5_bwd_in(dpre, dsin, sv["u"], tgt, bt, dflat, nm("s5_bwd_in"))
    d_u = _from_chunks(du_f, BF16, nm("from_chunks"))
    small["s5_tables"] = (d_tgt, d_bt, d_ct, d_lr, d_li, d_dflat)
    dq, dk, dv, dbias = _carried(comm, "bwd", l, "attn_bwd", W, _attn_bwd, sv["proj"], sm["attn_bias"][l], dyb, nm("attn_bwd"))
    small["attn_bias"] = dbias
    wax, bax = sm["lru_prep"][l]
    cw, cb, lam = sm["lru_conv_w"][l], sm["lru_conv_b"][l][None, :], sm["lru_lambda"][l][None, :]
    d_lx, d_lg, d_cw, d_cb, d_wax, d_bax, d_lam = _lru_bwd(sv["proj"], sv["hst"], dya, cw, cb, wax, bax, lam, nm("lru_bwd"))
    small["lru_conv_w"], small["lru_conv_b"], small["lru_lambda"] = d_cw, d_cb[0], d_lam[0]
    small["lru_tables"] = (d_wax, d_bax)
    dproj = jnp.concatenate([d_lx, d_lg, dq, dk, dv, d_u] + list(dgates), axis=1)
    big["w_in"] = _carried(comm, "bwd", l, "mm_dw_in", W, _mm, sv["h1"], dproj, M=D_MODEL, N=IN_W, K=T, ta=True,
                           name=nm("mm_dw_in"))
    dh1 = _carried(comm, "bwd", l, "mm_dh1", W, _mm, dproj, W["w_in"], M=T, N=D_MODEL, K=IN_W, tb=True, b_lead=l,
                   name=nm("mm_dh1"))
    dx, dgn = _rms_bwd(sv["x"], sm["norm_mix_g"][l][None, :], dh1, dx1, nm("rms_bwd"))
    small["norm_mix_g"] = dgn[0]
    return dx, big, small


_TENSORS = tuple((n, k) for n in _BIG for k in range(N_BRANCH if n == "w_branch" else 1))
_FWD_CARRIERS = {"mm_in": (("w_out", "w_branch", "ssm_w_glu", "w_ffn_gate"), 0), "attn_fwd": (("w_ffn_up",), 0),
                 "mm_ffn_gate": (("w_ffn_down",), 0), "mm_ffn_up": (("w_in",), 1), "mm_down": (("w_in",), 1)}
_FWD_PIECES = {"mm_ffn_up": (0, 2), "mm_down": (1, 2)}
_BWD_PAIR_CARRIERS = {"mm_dw_down": ("w_in", "w_out", "ssm_w_glu"), "mm_dact": ("w_ffn_gate", "w_ffn_up", "w_ffn_down", "w_branch")}
_BWD_CARRIERS = {"mm_dw_gate": ("w_out", "w_branch", "ssm_w_glu"), "attn_bwd": ("w_ffn_down",), "mm_dw_in": ("w_in",),
                 "mm_dh1": ("w_ffn_gate", "w_ffn_up")}


class _StepComm:
    def __init__(self, depth, cidx, jcidx):
        self.depth, self.cidx, self.jcidx = depth, cidx, jcidx
        self.accs = {}
        self.raw = None
        self.paired = {}
        self.pending = None

    def gather_side(self, W, names, l, piece):
        fulls, axes, regions = [], [], []
        for n in names:
            per = N_BRANCH if n == "w_branch" else 1
            fulls.append(W[n])
            axes.append(_BIG_SHARD_AXIS[n] + 1)
            regions.append((per * l, per, "other", piece))
        return _GatherSide(fulls, axes, regions)

    def side(self, phase, l, key, W):
        if phase == "fwd":
            if key not in _FWD_CARRIERS or l + _FWD_CARRIERS[key][1] >= self.depth:
                return None
            names, ahead = _FWD_CARRIERS[key]
            return self.gather_side(W, names, l + ahead, _FWD_PIECES.get(key, (0, 1)))
        if key in _BWD_PAIR_CARRIERS and self.raw is not None:
            nks = [nk for nk in _TENSORS if nk[0] in _BWD_PAIR_CARRIERS[key]]
            return _PairSide([self.raw[1][nk] for nk in nks], [1 - _BIG_SHARD_AXIS[nk[0]] for nk in nks])
        if key in _BWD_CARRIERS and self.pending is not None:
            nks = [nk for nk in _TENSORS if nk[0] in _BWD_CARRIERS[key]]
            return _ScatterSide([self.pending[1][nk][1] for nk in nks], [_BIG_SHARD_AXIS[nk[0]] for nk in nks])
        return None

    def took(self, phase, l, key, got, W):
        if phase == "fwd":
            W.update(zip(_FWD_CARRIERS[key][0], got))
        elif key in _BWD_PAIR_CARRIERS:
            nks = [nk for nk in _TENSORS if nk[0] in _BWD_PAIR_CARRIERS[key]]
            self.paired.update(zip(nks, got))
            if len(self.paired) == len(_TENSORS):
                self.pair_sums()
        else:
            nks = [nk for nk in _TENSORS if nk[0] in _BWD_CARRIERS[key]]
            self.shard_sums(self.pending[0], nks, got)

    def pair_sums(self):
        l, grads = self.raw
        self.pending = (l, {nk: _pair_sum(grads[nk], self.paired[nk], 1 - _BIG_SHARD_AXIS[nk[0]], self.cidx, "pair_sum")
                            for nk in _TENSORS})
        self.raw, self.paired = None, {}

    def shard_sums(self, l, nks, got):
        for nk, r in zip(nks, got):
            pf = self.pending[1][nk][0]
            sa = _BIG_SHARD_AXIS[nk[0]]
            if nk not in self.accs:
                shp = list(pf.shape)
                shp[sa] //= N_CHIPS
                shp[1 - sa] *= 2
                self.accs[nk] = lax.empty((self.depth,) + tuple(shp), F32)
            self.accs[nk] = _shard_sum(pf, r, self.accs[nk], l, sa, self.jcidx, "shard_sum")

    def on_big(self, l, big):
        grads = {}
        for n in _BIG:
            gs = big[n] if isinstance(big[n], list) else [big[n]]
            grads.update({(n, k): g for k, g in enumerate(gs)})
        self.raw = (l, grads)
        if l == 0:
            half_axes = [1 - _BIG_SHARD_AXIS[nk[0]] for nk in _TENSORS]
            self.paired = dict(zip(_TENSORS, _run_side(_PairSide([grads[nk] for nk in _TENSORS], half_axes), "pair_exchange")))
            self.pair_sums()
            side = _ScatterSide([self.pending[1][nk][1] for nk in _TENSORS], [_BIG_SHARD_AXIS[nk[0]] for nk in _TENSORS])
            self.shard_sums(0, _TENSORS, _run_side(side, "chip_scatter"))
            self.pending = None


class _NoComm:
    def __init__(self, on_big):
        self.on_big = on_big

    def side(self, phase, l, key, W):
        return None


def _local_step(xs, tgt, W, sm, comm):
    W = dict(W)
    sm = dict(sm)
    depth = sm["norm_mix_g"].shape[0]
    lru_o, lru_vjp = jax.vjp(jax.vmap(_lru_prep), sm["lru_wa"], sm["lru_wx"], sm["lru_ba"], sm["lru_bx"])
    attn_o, attn_vjp = jax.vjp(jax.vmap(_attn_bias), sm["attn_rel_bias"])
    s5_names = ("ssm_a_re", "ssm_a_im", "ssm_b_re", "ssm_b_im", "ssm_c_re", "ssm_c_im", "ssm_d", "ssm_log_step")
    s5_o, s5_vjp = jax.vjp(jax.vmap(_s5_prep), *[sm[n] for n in s5_names])
    wax_all = lru_o[0].astype(BF16)
    sm["lru_prep"] = [(wax_all[l], lru_o[1][l]) for l in range(depth)]
    sm["attn_bias"] = [attn_o[l] for l in range(depth)]
    kt_in = [t.reshape((depth * SSM_G,) + t.shape[2:]) for t in s5_o[:4]]
    tgt_all = _s5_tgt(*kt_in, "s5_tgt").reshape(depth, SSM_G, S5_LW, S5_LW)
    sm["s5_prep"] = [(tgt_all[l],) + tuple(t[l] for t in s5_o[4:]) for l in range(depth)]

    saved = []
    for l in range(depth):
        xs, sv = _layer_fwd(l, xs, W, sm, comm)
        saved.append(sv)
    loss_part, dx, dgf = _final_loss(xs, sm["norm_final_g"][None, :], tgt, "final_loss")

    direct = ("norm_mix_g", "gate_bias", "lru_conv_w", "lru_conv_b", "lru_lambda", "norm_ffn_g")
    per_layer = [None] * depth
    for l in reversed(range(depth)):
        dx, big, per_layer[l] = _layer_bwd(l, dx, saved[l], W, sm, comm)
        comm.on_big(l, big)
    stacked = lambda key, i: jnp.stack([per_layer[l][key][i] for l in range(depth)])
    small_tree = {n: jnp.stack([per_layer[l][n] for l in range(depth)]) for n in direct}
    d_wa, d_wx, d_ba, d_bx = lru_vjp(tuple(stacked("lru_tables", i) for i in range(2)))
    (d_rel,) = attn_vjp(jnp.stack([per_layer[l]["attn_bias"] for l in range(depth)]))
    d_kt_in = _s5_tgt_t(stacked("s5_tables", 0).reshape(depth * SSM_G, S5_LW, S5_LW), *kt_in, "s5_tgt_t")
    d_kt_in = [t.reshape((depth, SSM_G) + t.shape[1:]) for t in d_kt_in]
    d_s5 = s5_vjp(tuple(d_kt_in) + tuple(stacked("s5_tables", i) for i in range(1, 6)))
    small_tree.update(lru_wa=d_wa, lru_wx=d_wx, lru_ba=d_ba, lru_bx=d_bx, attn_rel_bias=d_rel, norm_final_g=dgf[0])
    small_tree.update(zip(s5_names, d_s5))
    return loss_part, dx, small_tree


def _pack_small(tree, names):
    flat = jnp.concatenate([tree[n].reshape(-1) for n in names])
    per = -(-flat.shape[0] // (_N_DEV * 128 * 8)) * (128 * 8)
    flat = jnp.pad(flat, (0, _N_DEV * per - flat.shape[0]))
    return flat.reshape(_N_DEV, per // 128, 128)


def _unpack_small(flat, like, names):
    flat = flat.reshape(-1)
    out, off = {}, 0
    for n in names:
        size = math.prod(like[n].shape)
        out[n] = flat[off:off + size].reshape(like[n].shape)
        off += size
    return out


def kernel(x, norm_mix_g, w_in, gate_bias, lru_conv_w, lru_conv_b, lru_wa, lru_ba, lru_wx, lru_bx, lru_lambda, attn_rel_bias, ssm_a_re, ssm_a_im, ssm_b_re, ssm_b_im, ssm_c_re, ssm_c_im, ssm_d, ssm_log_step, ssm_w_glu, w_branch, w_out, norm_ffn_g, w_ffn_gate, w_ffn_up, w_ffn_down, norm_final_g, loss_target, m_norm_mix_g, m_w_in, m_gate_bias, m_lru_conv_w, m_lru_conv_b, m_lru_wa, m_lru_ba, m_lru_wx, m_lru_bx, m_lru_lambda, m_attn_rel_bias, m_ssm_a_re, m_ssm_a_im, m_ssm_b_re, m_ssm_b_im, m_ssm_c_re, m_ssm_c_im, m_ssm_d, m_ssm_log_step, m_ssm_w_glu, m_w_branch, m_w_out, m_norm_ffn_g, m_w_ffn_gate, m_w_ffn_up, m_w_ffn_down, m_norm_final_g, v_norm_mix_g, v_w_in, v_gate_bias, v_lru_conv_w, v_lru_conv_b, v_lru_wa, v_lru_ba, v_lru_wx, v_lru_bx, v_lru_lambda, v_attn_rel_bias, v_ssm_a_re, v_ssm_a_im, v_ssm_b_re, v_ssm_b_im, v_ssm_c_re, v_ssm_c_im, v_ssm_d, v_ssm_log_step, v_ssm_w_glu, v_w_branch, v_w_out, v_norm_ffn_g, v_w_ffn_gate, v_w_ffn_up, v_w_ffn_down, v_norm_final_g):
    args = dict(locals())
    w = {n: args[n] for n in _WEIGHTS}
    m = {n: args["m_" + n] for n in _WEIGHTS}
    v = {n: args["v_" + n] for n in _WEIGHTS}
    depth = w_in.shape[0]
    xc, yc, cc = _my_place()
    jchip = 2 * xc + yc
    cidx = jnp.reshape(cc, (1,)).astype(jnp.int32)
    jidx = jnp.reshape(jchip, (1,)).astype(jnp.int32)
    jcidx = jnp.stack([jchip, cc]).astype(jnp.int32)

    blocks = [w[n] for n in _BIG]
    blocks[2] = blocks[2].reshape(depth * N_BRANCH, MIX_W, -1)
    axes = [_BIG_SHARD_AXIS[n] + 1 for n in _BIG] + [_SMALL_SHARDED[n] for n in _SMALL_SHARDED]
    placed = [_place_shard(b, ax, jidx, BF16, "place_shard") for b, ax in zip(blocks, axes)]
    placed += [_place_shard(w[n], _SMALL_SHARDED[n], jidx, F32, "place_shard") for n in _SMALL_SHARDED]
    W = dict(zip(_BIG, placed))
    first = [W["w_in"]] + placed[len(_BIG):]
    regions = [(0, 1, "other")] + [(0, depth, "lead")] * len(_SMALL_SHARDED)
    gathered = _run_side(_GatherSide(first, [axes[0]] + axes[len(_BIG):], regions), "gather_weights")
    W["w_in"] = gathered[0]
    sm_full = dict(zip(_SMALL_SHARDED, gathered[1:]))
    sm = {n: w[n] for n in _SMALL if n not in _SMALL_SHARDED}
    sm.update(sm_full)

    comm = _StepComm(depth, cidx, jcidx)
    loss_part, dx, small_tree = _local_step(x[0], loss_target[0], W, sm, comm)
    loss = lax.psum(loss_part, ("x", "y", "c"))
    grad_x = dx[None]

    joined = _pair_join([comm.accs[nk] for nk in _TENSORS], [1 - _BIG_SHARD_AXIS[nk[0]] for nk in _TENSORS], "pair_join")
    jd = dict(zip(_TENSORS, joined))
    grads = {}
    for n in _BIG:
        if n == "w_branch":
            grads[n] = jnp.stack([jd[(n, k)] for k in range(N_BRANCH)], axis=1)
        else:
            grads[n] = jd[(n, 0)]
    like = {n: (sm_full[n] if n in _SMALL_SHARDED else w[n]) for n in _SMALL}
    red = _unpack_small(_allreduce_small(_pack_small(small_tree, _SMALL), "allreduce_small"), like, _SMALL)
    for n in _SMALL:
        if n in _SMALL_SHARDED:
            size = w[n].shape[2]
            grads[n] = lax.dynamic_slice_in_dim(red[n], (2 * xc + yc) * size, size, axis=2)
        else:
            grads[n] = red[n]

    delta, new_m, new_v = {}, {}, {}
    for n in _BIG:
        shp = w[n].shape
        two = lambda a: a.reshape(-1, shp[-1])
        d_, m_, v_ = _adamw(two(w[n]), two(grads[n]), two(m[n]), two(v[n]), "adamw")
        delta[n], new_m[n], new_v[n] = d_.reshape(shp), m_.reshape(shp), v_.reshape(shp)
    pk = lambda tree: _pack_small(tree, _SMALL).reshape(-1, 128)
    d_, m_, v_ = _adamw(pk(w), pk(grads), pk(m), pk(v), "adamw_small")
    like_local = {n: w[n] for n in _SMALL}
    for tree, flat in ((delta, d_), (new_m, m_), (new_v, v_)):
        tree.update(_unpack_small(flat, like_local, _SMALL))
    return (loss, grad_x, *[grads[n] for n in _WEIGHTS], *[delta[n] for n in _WEIGHTS], *[new_m[n] for n in _WEIGHTS],
            *[new_v[n] for n in _WEIGHTS])
```

```python
import functools
import math

import jax
import jax.numpy as jnp
from jax import lax
from jax.experimental import pallas as pl
from jax.experimental.pallas import tpu as pltpu

F32 = jnp.float32
BF16 = jnp.bfloat16

D_MODEL = 2048
MIX_W = 1024
N_BRANCH = 3
LRU_BLOCKS = 16
LRU_BW = 64
CONV_W = 4
LRU_C = 8.0
CHUNK = 64
ATT_HEADS = 8
ATT_HD = 128
ATT_LEFT = 8
ATT_BAND = (ATT_LEFT + 1) * CHUNK
MAX_REL = 128
N_REL = 2 * MAX_REL + 1
SSM_G = 64
SSM_H = 16
SSM_P = 64
FFN_H = 5632
IN_W = 6 * MIX_W + N_BRANCH * D_MODEL
NORM_EPS = 1e-6
MASK_VALUE = -1e30
ADAM_LR, ADAM_B1, ADAM_B2, ADAM_EPS, ADAM_WD, ADAM_STEP = 0.001, 0.9, 0.999, 1e-08, 0.01, 10

S5_L = 16
S5_LW = S5_L * SSM_H
N_CHIPS = 4
V7X_VMEM_LIMIT = 56 * 1024 * 1024
HI = lax.Precision.HIGHEST
MESH = pl.DeviceIdType.MESH


def _cparams(sem=None):
    return pltpu.CompilerParams(dimension_semantics=sem, vmem_limit_bytes=V7X_VMEM_LIMIT)


def _pick(n, prefs):
    for p in prefs:
        if n % p == 0:
            return p
    return n


_MM_VMEM_BUDGET = 46 * 1024 * 1024


def _mm_tiles(M, N, K, has_res, out_bytes):
    tn = _pick(N, (1024, 1408, 512, 256, 128))
    for tk in (K, 2048, 1408, 1024, 512, 256, 128):
        if K % tk:
            continue
        for tm in (1024, 512, 256, 128, 64, 32, 16, 8):
            if M % tm:
                continue
            need = 2 * 2 * (tm * tk + tk * tn) + 2 * tm * tn * out_bytes
            need += tm * tn * 4 if tk < K else 0
            need += 2 * tm * tn * 4 if has_res else 0
            if need <= _MM_VMEM_BUDGET:
                return tm, tn, tk
    raise ValueError((M, N, K))


def _call(body, *, name, grid, in_specs, out_specs, out_shape, args, scratch_shapes=(), sem=None, side=None):
    in_specs, out_specs, out_shape = list(in_specs), list(out_specs), list(out_shape)
    scratch_shapes = list(scratch_shapes)
    if side is None:
        outs = pl.pallas_call(body, name=name, out_shape=tuple(out_shape), grid=grid, in_specs=in_specs,
                              out_specs=tuple(out_specs), scratch_shapes=scratch_shapes, compiler_params=_cparams(sem))(*args)
        return tuple(outs), ()
    n_in, n_out, n_scr = len(in_specs), len(out_shape), len(scratch_shapes)
    s_in, s_out = len(side.inputs), len(side.out_shapes)

    def wrapped(*refs):
        mi, refs = refs[:n_in], refs[n_in:]
        si, refs = refs[:s_in], refs[s_in:]
        mo, refs = refs[:n_out], refs[n_out:]
        so, refs = refs[:s_out], refs[s_out:]
        scr, (send, recv) = refs[:n_scr], refs[n_scr:]
        first = functools.reduce(jnp.logical_and, [pl.program_id(d) == 0 for d in range(len(grid))])
        last = functools.reduce(jnp.logical_and, [pl.program_id(d) == g - 1 for d, g in enumerate(grid)])

        @pl.when(first)
        def _():
            side.start(si, so, send, recv)

        body(*mi, *mo, *scr)

        @pl.when(last)
        def _():
            side.finish(si, so, send, recv)

    outs = pl.pallas_call(
        wrapped, name=name, out_shape=tuple(out_shape + list(side.out_shapes)), grid=grid,
        in_specs=in_specs + [_ANY] * s_in, out_specs=tuple(out_specs + [_ANY] * s_out),
        scratch_shapes=scratch_shapes + [pltpu.SemaphoreType.DMA((side.n_sems,)), pltpu.SemaphoreType.DMA((side.n_sems,))],
        input_output_aliases={n_in + i: n_out + o for i, o in side.aliases.items()},
        compiler_params=_cparams(("arbitrary",) * len(grid)),
    )(*args, *side.inputs)
    return tuple(outs[:n_out]), tuple(outs[n_out:])


def _mm(a, b, *, M, N, K, name, ta=False, tb=False, a_lead=None, b_lead=None, a_off=(0, 0), b_off=(0, 0),
        out_dtype=F32, res=None, tm=None, tn=None, tk=None, side=None):
    if tm is None and tn is None and tk is None:
        tm, tn, tk = _mm_tiles(M, N, K, res is not None, jnp.dtype(out_dtype).itemsize)
    nk = K // tk

    def spec(blk, lead, off, order):
        r0, c0 = off[0] // blk[0], off[1] // blk[1]
        assert off[0] % blk[0] == 0 and off[1] % blk[1] == 0
        if lead is None:
            return pl.BlockSpec(blk, lambda i, j, k: (r0 + order(i, j, k)[0], c0 + order(i, j, k)[1]))
        return pl.BlockSpec((None,) + blk, lambda i, j, k: (lead, r0 + order(i, j, k)[0], c0 + order(i, j, k)[1]))

    a_spec = spec((tk, tm), a_lead, a_off, lambda i, j, k: (k, i)) if ta else spec((tm, tk), a_lead, a_off, lambda i, j, k: (i, k))
    b_spec = spec((tn, tk), b_lead, b_off, lambda i, j, k: (j, k)) if tb else spec((tk, tn), b_lead, b_off, lambda i, j, k: (k, j))
    dims = (((0 if ta else 1,), (1 if tb else 0,)), ((), ()))
    in_specs = [a_spec, b_spec]
    args = [a, b]
    if res is not None:
        in_specs.append(pl.BlockSpec((tm, tn), lambda i, j, k: (i, j)))
        args.append(res)

    def body(*refs):
        a_ref, b_ref = refs[:2]
        r_ref = refs[2] if res is not None else None
        o_ref = refs[3] if res is not None else refs[2]

        def dot():
            return lax.dot_general(a_ref[...], b_ref[...], dims, preferred_element_type=F32)

        def finish(r):
            if r_ref is not None:
                r = r + r_ref[...]
            o_ref[...] = r.astype(out_dtype)

        if nk == 1:
            finish(dot())
            return
        acc = refs[-1]
        k = pl.program_id(2)

        @pl.when(k == 0)
        def _():
            acc[...] = dot()

        @pl.when(jnp.logical_and(k > 0, k < nk - 1))
        def _():
            acc[...] += dot()

        @pl.when(k == nk - 1)
        def _():
            finish(acc[...] + dot())

    (out,), got = _call(
        body, name=name, out_shape=[jax.ShapeDtypeStruct((M, N), out_dtype)],
        grid=(M // tm, N // tn, nk), in_specs=in_specs, out_specs=[pl.BlockSpec((tm, tn), lambda i, j, k: (i, j))],
        scratch_shapes=[pltpu.VMEM((tm, tn), F32)] if nk > 1 else [],
        sem=("parallel", "parallel", "arbitrary"), args=args, side=side)
    return out if side is None else (out, got)


def _rowwise(fn, rows, consts, out_rows, out_accs, *, R, name, tm=256, rs=16, cs=None, side=None):
    tm = min(tm, R)
    assert R % tm == 0 and tm % rs == 0
    n_r, n_c, n_o, n_a = len(rows), len(consts), len(out_rows), len(out_accs)
    nsteps = R // tm
    widths = [w for _, _, w in rows]
    if cs is not None:
        assert all(w == widths[0] for w in widths) and widths[0] % cs == 0
        col_chunks = [(c0, cs) for c0 in range(0, widths[0], cs)]
    else:
        col_chunks = [None]

    def body(*refs):
        r_refs = refs[:n_r]
        c_refs = refs[n_r:n_r + n_c]
        o_refs = refs[n_r + n_c:n_r + n_c + n_o]
        a_refs = refs[n_r + n_c + n_o:n_r + n_c + n_o + n_a]
        s_refs = refs[n_r + n_c + n_o + n_a:]
        i = pl.program_id(0)

        @pl.when(i == 0)
        def _():
            for s in s_refs:
                s[...] = jnp.zeros_like(s)

        def piece(g, carry):
            r0 = pl.multiple_of(g * rs, rs)
            for cc in col_chunks:
                csl = slice(None) if cc is None else slice(cc[0], cc[0] + cc[1])
                rp = [r[pl.ds(r0, rs), csl].astype(F32) for r in r_refs]
                cp = [c[:, csl] for c in c_refs]
                outs, accs = fn(rp, cp)
                for o_ref, o in zip(o_refs, outs):
                    o_ref[pl.ds(r0, rs), csl] = o.astype(o_ref.dtype)
                for s_ref, av in zip(s_refs, accs):
                    s_ref[:, csl] += av
            return carry

        lax.fori_loop(0, tm // rs, piece, 0)

        @pl.when(i == nsteps - 1)
        def _():
            for a_ref, s_ref in zip(a_refs, s_refs):
                a_ref[...] = jnp.sum(s_ref[...], axis=0, keepdims=True)

    in_specs = [pl.BlockSpec((tm, w), functools.partial(lambda i, cb: (i, cb), cb=off // w)) for _, off, w in rows]
    for _, off, w in rows:
        assert off % w == 0
    in_specs += [pl.BlockSpec(c.shape, lambda i: (0, 0)) for c in consts]
    out_specs = [pl.BlockSpec((tm, w), lambda i: (i, 0)) for w, _ in out_rows]
    out_specs += [pl.BlockSpec((1, w), lambda i: (0, 0)) for w in out_accs]
    out_shape = [jax.ShapeDtypeStruct((R, w), dt) for w, dt in out_rows]
    out_shape += [jax.ShapeDtypeStruct((1, w), F32) for w in out_accs]
    outs, got = _call(
        body, name=name, out_shape=out_shape, grid=(nsteps,), in_specs=in_specs, out_specs=out_specs,
        scratch_shapes=[pltpu.VMEM((rs, w), F32) for w in out_accs], sem=("arbitrary",),
        args=[r for r, _, _ in rows] + list(consts), side=side)
    return outs if side is None else (outs, got)


def _rms(x, g):
    r = lax.rsqrt(jnp.mean(x * x, axis=-1, keepdims=True) + NORM_EPS)
    return x * r * g


def _rms_fwd(x, g, name):
    R = x.shape[0]

    def fn(rp, cp):
        return [_rms(rp[0], cp[0])], []

    return _rowwise(fn, [(x, 0, D_MODEL)], [g], [(D_MODEL, BF16)], [], R=R, name=name)[0]


def _rms_bwd(x, g, dh, dres, name):
    R = x.shape[0]

    def fn(rp, cp):
        xv, dhv, drv = rp
        _, pull = jax.vjp(_rms, xv, jnp.broadcast_to(cp[0], xv.shape))
        dx, dgv = pull(dhv)
        return [drv + dx], [dgv]

    dx, dg = _rowwise(fn, [(x, 0, D_MODEL), (dh, 0, D_MODEL), (dres, 0, D_MODEL)], [g], [(D_MODEL, F32)], [D_MODEL],
                      R=R, name=name, rs=8)
    return dx, dg


def _final_loss(x, g, tgt, name):
    R = x.shape[0]

    def loss_rows(xv, gv, tv):
        e = _rms(xv, gv) - tv
        return 0.5 * jnp.mean(e * e, axis=-1, keepdims=True)

    def fn(rp, cp):
        xv, tv = rp
        lr, pull = jax.vjp(lambda a, b: loss_rows(a, b, tv), xv, jnp.broadcast_to(cp[0], xv.shape))
        dx, dgv = pull(jnp.ones_like(lr))
        return [dx], [dgv, jnp.broadcast_to(lr, (lr.shape[0], 128))]

    dx, dg, lsum = _rowwise(fn, [(x, 0, D_MODEL), (tgt, 0, D_MODEL)], [g], [(D_MODEL, F32)], [D_MODEL, 128],
                            R=R, name=name, rs=8)
    return lsum[0, 0], dx, dg


def _neg_expm1(z):
    u = jnp.exp(z)
    safe = jnp.where(u == 1.0, 0.5, u)
    return -jnp.where(u == 1.0, z, (safe - 1.0) * z / jnp.log(safe))


def _lru_ab(xc, pr, pi, lam):
    r = jax.nn.sigmoid(pr)
    i = jax.nn.sigmoid(pi)
    log_a = -LRU_C * r * jax.nn.softplus(-lam)
    a = jnp.exp(log_a)
    b = jnp.sqrt(_neg_expm1(2.0 * log_a)) * (i * xc)
    return a, b


def _gated(h, gate):
    return h * jax.nn.gelu(gate)


def _row_iota8(w):
    return lax.broadcasted_iota(jnp.int32, (8, w), 0)


def _shift_dn(x, halo, s):
    xs = pltpu.roll(x, s, 0)
    hs = pltpu.roll(halo, s, 0)
    first = jnp.where(_row_iota8(x.shape[1]) < s, hs, xs[0:8])
    return jnp.concatenate([first, xs[8:]], axis=0) if x.shape[0] > 8 else first


def _shift_up(x, nxt, s):
    n = x.shape[0]
    xs = pltpu.roll(x, n - s, 0)
    ns = pltpu.roll(nxt, 8 - s, 0)
    last = jnp.where(_row_iota8(x.shape[1]) >= 8 - s, ns, xs[n - 8:])
    return jnp.concatenate([xs[:n - 8], last], axis=0) if n > 8 else last


def _lru_tiles(T):
    tT = min(256, T)
    return tT, T // tT


def _lru_fwd(proj, cw, cb, wax, bax, lam, name):
    T = proj.shape[0]
    W = MIX_W
    tT, nT = _lru_tiles(T)

    def body(x_ref, xh_ref, gt_ref, cw_ref, cb_ref, wax_ref, bax_ref, lam_ref, y_ref, h_ref, a_s, b_s, hc_s):
        i = pl.program_id(0)

        @pl.when(i == 0)
        def _():
            hc_s[...] = jnp.zeros_like(hc_s)

        x = x_ref[...]
        halo = jnp.where(i > 0, xh_ref[...], 0.0)
        w = cw_ref[...]
        xc = (cb_ref[...] + w[3:4] * x + w[2:3] * _shift_dn(x, halo, 1) + w[1:2] * _shift_dn(x, halo, 2)
              + w[0:1] * _shift_dn(x, halo, 3))
        pre = jnp.dot(xc.astype(BF16), wax_ref[...], preferred_element_type=F32) + bax_ref[...]
        a, b = _lru_ab(xc, pre[:, :W], pre[:, W:], lam_ref[...])
        a_s[...] = a
        b_s[...] = b
        row = _row_iota8(W)

        def grp(gi, hprev):
            r0 = pl.multiple_of(gi * 8, 8)
            A = a_s[pl.ds(r0, 8), :]
            B = b_s[pl.ds(r0, 8), :]
            for s in (1, 2, 4):
                As = pltpu.roll(A, s, 0)
                Bs = pltpu.roll(B, s, 0)
                m = row >= s
                B = jnp.where(m, A * Bs + B, B)
                A = jnp.where(m, A * As, A)
            H = A * hprev + B
            h_ref[pl.ds(r0, 8), :] = H
            return H[7:8, :]

        hc_s[0:1, :] = lax.fori_loop(0, tT // 8, grp, hc_s[0:1, :])
        y_ref[...] = _gated(h_ref[...], gt_ref[...]).astype(BF16)

    hb = tT // 8
    return pl.pallas_call(
        body, name=name,
        out_shape=(jax.ShapeDtypeStruct((T, W), BF16), jax.ShapeDtypeStruct((T, W), F32)),
        grid=(nT,),
        in_specs=[pl.BlockSpec((tT, W), lambda i: (i, 0)),
                  pl.BlockSpec((8, W), lambda i: (jnp.maximum(i * hb - 1, 0), 0)),
                  pl.BlockSpec((tT, W), lambda i: (i, 1)),
                  pl.BlockSpec((CONV_W, W), lambda i: (0, 0)), pl.BlockSpec((1, W), lambda i: (0, 0)),
                  pl.BlockSpec((W, 2 * W), lambda i: (0, 0)), pl.BlockSpec((1, 2 * W), lambda i: (0, 0)),
                  pl.BlockSpec((1, W), lambda i: (0, 0))],
        out_specs=(pl.BlockSpec((tT, W), lambda i: (i, 0)), pl.BlockSpec((tT, W), lambda i: (i, 0))),
        scratch_shapes=[pltpu.VMEM((tT, W), F32), pltpu.VMEM((tT, W), F32), pltpu.VMEM((8, W), F32)],
        compiler_params=_cparams(("arbitrary",)),
    )(proj, proj, proj, cw, cb, wax, bax, lam)


def _lru_bwd(proj, h, dy, cw, cb, wax, bax, lam, name):
    T = proj.shape[0]
    W = MIX_W
    tT, nT = _lru_tiles(T)
    hb = tT // 8

    def body(x_ref, xh_ref, gt_ref, h_ref, hh_ref, dy_ref, cw_ref, cb_ref, wax_ref, bax_ref, lam_ref,
             dx_ref, dgt_ref, dcw_ref, dcb_ref, dwax_ref, dbax_ref, dlam_ref,
             al_s, be_s, d_s, ca_s, cd_s, cx_s):
        i = pl.program_id(0)
        ib = nT - 1 - i

        @pl.when(i == 0)
        def _():
            for r in (ca_s, cd_s, cx_s, dcw_ref, dcb_ref, dwax_ref, dbax_ref, dlam_ref):
                r[...] = jnp.zeros_like(r)

        x = x_ref[...]
        halo = jnp.where(ib > 0, xh_ref[...], 0.0)
        w = cw_ref[...]
        x1, x2, x3 = _shift_dn(x, halo, 1), _shift_dn(x, halo, 2), _shift_dn(x, halo, 3)
        xc = cb_ref[...] + w[3:4] * x + w[2:3] * x1 + w[1:2] * x2 + w[0:1] * x3
        xcb = xc.astype(BF16)
        pre = jnp.dot(xcb, wax_ref[...], preferred_element_type=F32) + bax_ref[...]
        (a, _), pull_ab = jax.vjp(_lru_ab, xc, pre[:, :W], pre[:, W:], lam_ref[...])
        hv = h_ref[...]
        hprev = _shift_dn(hv, jnp.where(ib > 0, hh_ref[...], 0.0), 1)
        _, pull_y = jax.vjp(_gated, hv, gt_ref[...])
        dh_out, dgt = pull_y(dy_ref[...])
        dgt_ref[...] = dgt.astype(BF16)
        al_s[...] = _shift_up(a, ca_s[...], 1)
        be_s[...] = dh_out
        row = _row_iota8(W)
        ng = tT // 8

        def grp(k, dnext):
            r0 = pl.multiple_of((ng - 1 - k) * 8, 8)
            A = al_s[pl.ds(r0, 8), :]
            B = be_s[pl.ds(r0, 8), :]
            for s in (1, 2, 4):
                As = pltpu.roll(A, 8 - s, 0)
                Bs = pltpu.roll(B, 8 - s, 0)
                m = row < 8 - s
                B = jnp.where(m, A * Bs + B, B)
                A = jnp.where(m, A * As, A)
            Dg = A * dnext + B
            d_s[pl.ds(r0, 8), :] = Dg
            return Dg[0:1, :]

        lax.fori_loop(0, ng, grp, cd_s[0:1, :])
        Dv = d_s[...]
        dxc1, dpr, dpi, dlam = pull_ab((Dv * hprev, Dv))
        dpre = jnp.concatenate([dpr, dpi], axis=1)
        dpb = dpre.astype(BF16)
        dxc = dxc1 + lax.dot_general(dpb, wax_ref[...], (((1,), (1,)), ((), ())), preferred_element_type=F32)
        dwax_ref[...] += lax.dot_general(xcb, dpb, (((0,), (0,)), ((), ())), preferred_element_type=F32)
        dbax_ref[...] += jnp.sum(dpre, axis=0, keepdims=True)
        dlam_ref[...] += dlam
        dcb_ref[...] += jnp.sum(dxc, axis=0, keepdims=True)
        dcw_ref[...] += jnp.concatenate([jnp.sum(dxc * x3, axis=0, keepdims=True), jnp.sum(dxc * x2, axis=0, keepdims=True),
                                         jnp.sum(dxc * x1, axis=0, keepdims=True), jnp.sum(dxc * x, axis=0, keepdims=True)], axis=0)
        nxt = cx_s[...]
        dx = (w[3:4] * dxc + w[2:3] * _shift_up(dxc, nxt, 1) + w[1:2] * _shift_up(dxc, nxt, 2)
              + w[0:1] * _shift_up(dxc, nxt, 3))
        dx_ref[...] = dx.astype(BF16)
        ca_s[...] = a[0:8]
        cd_s[...] = Dv[0:8]
        cx_s[...] = dxc[0:8]

    rev = lambda i: nT - 1 - i
    const = lambda shape: pl.BlockSpec(shape, lambda i: (0, 0))
    return pl.pallas_call(
        body, name=name,
        out_shape=(jax.ShapeDtypeStruct((T, W), BF16), jax.ShapeDtypeStruct((T, W), BF16),
                   jax.ShapeDtypeStruct((CONV_W, W), F32), jax.ShapeDtypeStruct((1, W), F32),
                   jax.ShapeDtypeStruct((W, 2 * W), F32), jax.ShapeDtypeStruct((1, 2 * W), F32),
                   jax.ShapeDtypeStruct((1, W), F32)),
        grid=(nT,),
        in_specs=[pl.BlockSpec((tT, W), lambda i: (rev(i), 0)),
                  pl.BlockSpec((8, W), lambda i: (jnp.maximum(rev(i) * hb - 1, 0), 0)),
                  pl.BlockSpec((tT, W), lambda i: (rev(i), 1)),
                  pl.BlockSpec((tT, W), lambda i: (rev(i), 0)),
                  pl.BlockSpec((8, W), lambda i: (jnp.maximum(rev(i) * hb - 1, 0), 0)),
                  pl.BlockSpec((tT, W), lambda i: (rev(i), 0)),
                  const((CONV_W, W)), const((1, W)), const((W, 2 * W)), const((1, 2 * W)), const((1, W))],
        out_specs=(pl.BlockSpec((tT, W), lambda i: (rev(i), 0)), pl.BlockSpec((tT, W), lambda i: (rev(i), 0)),
                   const((CONV_W, W)), const((1, W)), const((W, 2 * W)), const((1, 2 * W)), const((1, W))),
        scratch_shapes=[pltpu.VMEM((tT, W), F32), pltpu.VMEM((tT, W), F32), pltpu.VMEM((tT, W), F32),
                        pltpu.VMEM((8, W), F32), pltpu.VMEM((8, W), F32), pltpu.VMEM((8, W), F32)],
        compiler_params=_cparams(("arbitrary",)),
    )(proj, proj, proj, h, h, dy, cw, cb, wax, bax, lam)


def _lru_prep(wa, wx, ba, bx):
    eye = jnp.eye(LRU_BLOCKS, dtype=F32)

    def dense(wb):
        return (wb[:, :, None, :] * eye[:, None, :, None]).reshape(MIX_W, MIX_W)

    wax = jnp.concatenate([dense(wa), dense(wx)], axis=1)
    bax = jnp.concatenate([ba, bx])[None, :]
    return wax, bax


_ATT_QC = 4
_ATT_Q = _ATT_QC * CHUNK
_ATT_KW = (ATT_LEFT + _ATT_QC) * CHUNK
_BVEC_W = _ATT_KW
_N_OFFS = CHUNK - 1 + ATT_BAND


def _attn_bias(rel_bias):
    n_far = ATT_LEFT * CHUNK - MAX_REL + CHUNK
    far = jnp.broadcast_to(rel_bias[:, 2 * MAX_REL:], (ATT_HEADS, n_far))
    near = rel_bias[:, MAX_REL - (CHUNK - 1):2 * MAX_REL][:, ::-1]
    pad = jnp.zeros((ATT_HEADS, _BVEC_W - _N_OFFS), F32)
    return jnp.concatenate([far, near, pad], axis=1)[:, None, :]


def _bias_table(bvec_row):
    return pltpu.roll(jnp.broadcast_to(bvec_row, (_ATT_Q, _BVEC_W)), _BVEC_W - (CHUNK - 1), 1, stride=1, stride_axis=0)


def _bias_table_t(ds):
    r = lax.broadcasted_iota(jnp.int32, (_ATT_Q, _ATT_Q), 0)
    c = lax.broadcasted_iota(jnp.int32, (_ATT_Q, _ATT_Q), 1)
    rev = jnp.dot((r + c == _ATT_Q - 1).astype(F32), ds, preferred_element_type=F32, precision=HI)
    back = pltpu.roll(rev, _BVEC_W - (_ATT_Q - CHUNK), 1, stride=1, stride_axis=0)
    return jnp.sum(back, axis=0, keepdims=True)


_NT = (((1,), (1,)), ((), ()))
_TN = (((0,), (0,)), ((), ()))
_ATT_PAD = ATT_LEFT * CHUNK
_Q_BLK, _K_BLK, _V_BLK = 2 * MIX_W // ATT_HD, 3 * MIX_W // ATT_HD, 4 * MIX_W // ATT_HD


def _in_band():
    first = (lax.broadcasted_iota(jnp.int32, (_ATT_Q, _ATT_KW), 0) // CHUNK) * CHUNK
    k = lax.broadcasted_iota(jnp.int32, (_ATT_Q, _ATT_KW), 1)
    return jnp.logical_and(k >= first, k < first + ATT_BAND)


def _attn_probs(q, kb, bias, in_band, b):
    s = lax.dot_general(q, kb, _NT, preferred_element_type=F32) * (ATT_HD ** -0.5) + bias
    kpos = lax.broadcasted_iota(jnp.int32, s.shape, 1)
    s = jnp.where(jnp.logical_and(in_band, kpos >= _ATT_PAD - b * _ATT_Q), s, MASK_VALUE)
    e = jnp.exp(s - jnp.max(s, axis=-1, keepdims=True))
    return e / jnp.sum(e, axis=-1, keepdims=True)


def _attn_fwd(proj, bias, name, side=None):
    T = proj.shape[0]
    assert T % _ATT_Q == 0
    nB = T // _ATT_Q

    def body(q_ref, k_ref, v_ref, b_ref, o_ref, kp, vp):
        kp[0:_ATT_PAD, :] = jnp.zeros((_ATT_PAD, ATT_HD), BF16)
        vp[0:_ATT_PAD, :] = jnp.zeros((_ATT_PAD, ATT_HD), BF16)
        kp[_ATT_PAD:, :] = k_ref[...].astype(BF16)
        vp[_ATT_PAD:, :] = v_ref[...].astype(BF16)
        bias_v = _bias_table(b_ref[0])
        band = _in_band()

        def step(b, carry):
            r0 = pl.multiple_of(b * _ATT_Q, _ATT_Q)
            q = q_ref[pl.ds(r0, _ATT_Q), :].astype(BF16)
            p = _attn_probs(q, kp[pl.ds(r0, _ATT_KW), :], bias_v, band, b)
            o = jnp.dot(p.astype(BF16), vp[pl.ds(r0, _ATT_KW), :], preferred_element_type=F32)
            o_ref[pl.ds(r0, _ATT_Q), :] = o.astype(BF16)
            return carry

        lax.fori_loop(0, nB, step, 0)

    (out,), got = _call(
        body, name=name, out_shape=[jax.ShapeDtypeStruct((T, MIX_W), BF16)], grid=(ATT_HEADS,),
        in_specs=[pl.BlockSpec((T, ATT_HD), lambda h: (0, _Q_BLK + h)), pl.BlockSpec((T, ATT_HD), lambda h: (0, _K_BLK + h)),
                  pl.BlockSpec((T, ATT_HD), lambda h: (0, _V_BLK + h)), pl.BlockSpec((1, 1, _BVEC_W), lambda h: (h, 0, 0))],
        out_specs=[pl.BlockSpec((T, ATT_HD), lambda h: (0, h))],
        scratch_shapes=[pltpu.VMEM((T + _ATT_PAD, ATT_HD), BF16), pltpu.VMEM((T + _ATT_PAD, ATT_HD), BF16)],
        sem=("arbitrary",), args=(proj, proj, proj, bias), side=side)
    return out if side is None else (out, got)


def _attn_bwd(proj, bias, do, name, side=None):
    T = proj.shape[0]
    assert T % _ATT_Q == 0
    nB = T // _ATT_Q

    def body(q_ref, k_ref, v_ref, b_ref, do_ref, dq_ref, dk_ref, dv_ref, db_ref, kp, vp, dkp, dvp, dbs):
        kp[0:_ATT_PAD, :] = jnp.zeros((_ATT_PAD, ATT_HD), BF16)
        vp[0:_ATT_PAD, :] = jnp.zeros((_ATT_PAD, ATT_HD), BF16)
        kp[_ATT_PAD:, :] = k_ref[...].astype(BF16)
        vp[_ATT_PAD:, :] = v_ref[...].astype(BF16)
        dkp[...] = jnp.zeros_like(dkp)
        dvp[...] = jnp.zeros_like(dvp)
        dbs[...] = jnp.zeros_like(dbs)
        bias_v = _bias_table(b_ref[0])
        band = _in_band()

        def step(b, carry):
            r0 = pl.multiple_of(b * _ATT_Q, _ATT_Q)
            q = q_ref[pl.ds(r0, _ATT_Q), :].astype(BF16)
            kb = kp[pl.ds(r0, _ATT_KW), :]
            vb = vp[pl.ds(r0, _ATT_KW), :]
            dob = do_ref[pl.ds(r0, _ATT_Q), :].astype(BF16)
            p = _attn_probs(q, kb, bias_v, band, b)
            dp = lax.dot_general(dob, vb, _NT, preferred_element_type=F32)
            ds = p * (dp - jnp.sum(p * dp, axis=-1, keepdims=True))
            dbs[...] += ds
            dsb = (ds * (ATT_HD ** -0.5)).astype(BF16)
            dq_ref[pl.ds(r0, _ATT_Q), :] = jnp.dot(dsb, kb, preferred_element_type=F32).astype(BF16)
            dkp[pl.ds(r0, _ATT_KW), :] += lax.dot_general(dsb, q, _TN, preferred_element_type=F32)
            dvp[pl.ds(r0, _ATT_KW), :] += lax.dot_general(p.astype(BF16), dob, _TN, preferred_element_type=F32)
            return carry

        lax.fori_loop(0, nB, step, 0)
        dk_ref[...] = dkp[_ATT_PAD:, :].astype(BF16)
        dv_ref[...] = dvp[_ATT_PAD:, :].astype(BF16)
        db_ref[0] = _bias_table_t(dbs[...])

    hspec = pl.BlockSpec((T, ATT_HD), lambda h: (0, h))
    osd = jax.ShapeDtypeStruct((T, MIX_W), BF16)
    outs, got = _call(
        body, name=name,
        out_shape=[osd, osd, osd, jax.ShapeDtypeStruct((ATT_HEADS, 1, _BVEC_W), F32)], grid=(ATT_HEADS,),
        in_specs=[pl.BlockSpec((T, ATT_HD), lambda h: (0, _Q_BLK + h)), pl.BlockSpec((T, ATT_HD), lambda h: (0, _K_BLK + h)),
                  pl.BlockSpec((T, ATT_HD), lambda h: (0, _V_BLK + h)), pl.BlockSpec((1, 1, _BVEC_W), lambda h: (h, 0, 0)),
                  hspec],
        out_specs=[hspec, hspec, hspec, pl.BlockSpec((1, 1, _BVEC_W), lambda h: (h, 0, 0))],
        scratch_shapes=[pltpu.VMEM((T + _ATT_PAD, ATT_HD), BF16), pltpu.VMEM((T + _ATT_PAD, ATT_HD), BF16),
                        pltpu.VMEM((T + _ATT_PAD, ATT_HD), F32), pltpu.VMEM((T + _ATT_PAD, ATT_HD), F32),
                        pltpu.VMEM((_ATT_Q, _ATT_KW), F32)],
        sem=("arbitrary",), args=(proj, proj, proj, bias, do), side=side)
    return outs if side is None else (outs, got)


def _s5_prep(a_re, a_im, b_re, b_im, c_re, c_im, d, log_step):
    step = jnp.exp(log_step)[:, None]
    mag = jnp.exp(a_re * step)
    ang = a_im * step
    lb_re = mag * jnp.cos(ang)
    lb_im = mag * jnp.sin(ang)
    den = a_re * a_re + a_im * a_im
    nr = lb_re - 1.0
    coef_re = (nr * a_re + lb_im * a_im) / den
    coef_im = (lb_im * a_re - nr * a_im) / den
    bb_re = coef_re[..., None] * b_re - coef_im[..., None] * b_im
    bb_im = coef_re[..., None] * b_im + coef_im[..., None] * b_re
    ks = jnp.arange(S5_L + 1, dtype=F32)[:, None, None]
    pmag = jnp.exp(ks * (a_re * step)[None])
    PR, PI = pmag * jnp.cos(ks * ang[None]), pmag * jnp.sin(ks * ang[None])
    cl_re = c_re[None] * PR[:, :, None, :] - c_im[None] * PI[:, :, None, :]
    cl_im = c_re[None] * PI[:, :, None, :] + c_im[None] * PR[:, :, None, :]
    cla_re = cl_re[:S5_L].transpose(1, 0, 2, 3).reshape(SSM_G, S5_LW, SSM_P)
    cla_im = cl_im[:S5_L].transpose(1, 0, 2, 3).reshape(SSM_G, S5_LW, SSM_P)
    bbt_re, bbt_im = bb_re.transpose(0, 2, 1), bb_im.transpose(0, 2, 1)
    prr, pir = PR[:S5_L][::-1], PI[:S5_L][::-1]
    bret = (prr[:, :, None, :] * bb_re.transpose(0, 2, 1)[None] - pir[:, :, None, :] * bb_im.transpose(0, 2, 1)[None])
    bimt = (prr[:, :, None, :] * bb_im.transpose(0, 2, 1)[None] + pir[:, :, None, :] * bb_re.transpose(0, 2, 1)[None])
    bret = bret.transpose(1, 0, 2, 3).reshape(SSM_G, S5_LW, SSM_P)
    bimt = bimt.transpose(1, 0, 2, 3).reshape(SSM_G, S5_LW, SSM_P)
    cre = cl_re[1:].transpose(1, 3, 0, 2).reshape(SSM_G, SSM_P, S5_LW)
    cim = (-cl_im[1:]).transpose(1, 3, 0, 2).reshape(SSM_G, SSM_P, S5_LW)
    dflat = jnp.broadcast_to(d.reshape(SSM_G, 1, SSM_H), (SSM_G, S5_L, SSM_H)).reshape(SSM_G, 1, S5_LW)
    bt = jnp.concatenate([bret, bimt], axis=2)
    ct = jnp.concatenate([cre, cim], axis=1)
    return cla_re, cla_im, bbt_re, bbt_im, bt, ct, PR[S5_L], PI[S5_L], dflat


_S5_GB = 8


def _bdot(a, b, dims):
    return lax.dot_general(a, b, dims, preferred_element_type=F32, precision=HI)


_B_NN = (((2,), (1,)), ((0,), (0,)))
_B_NT = (((2,), (2,)), ((0,), (0,)))
_B_TN = (((1,), (1,)), ((0,), (0,)))


def _gspec(shape):
    return pl.BlockSpec((_S5_GB,) + shape, lambda g: (g, 0, 0))


def _s5_tgt(cla_re, cla_im, bbt_re, bbt_im, name):
    n_g = cla_re.shape[0]

    def body(cr_ref, ci_ref, br_ref, bi_ref, t_ref):
        kt = _bdot(br_ref[...], cr_ref[...], _B_NT) - _bdot(bi_ref[...], ci_ref[...], _B_NT)
        lane = lax.broadcasted_iota(jnp.int32, (SSM_H, S5_LW), 1)
        for g in range(_S5_GB):
            for lp in range(S5_L):
                rows = kt[g] if lp == 0 else jnp.where(lane >= lp * SSM_H, pltpu.roll(kt[g], lp * SSM_H, 1), 0.0)
                t_ref[g, lp * SSM_H:(lp + 1) * SSM_H, :] = rows

    return pl.pallas_call(
        body, name=name, out_shape=jax.ShapeDtypeStruct((n_g, S5_LW, S5_LW), F32), grid=(n_g // _S5_GB,),
        in_specs=[_gspec((S5_LW, SSM_P)), _gspec((S5_LW, SSM_P)), _gspec((SSM_H, SSM_P)), _gspec((SSM_H, SSM_P))],
        out_specs=_gspec((S5_LW, S5_LW)), compiler_params=_cparams(("parallel",)),
    )(cla_re, cla_im, bbt_re, bbt_im)


def _s5_tgt_t(d_tgt, cla_re, cla_im, bbt_re, bbt_im, name):
    n_g = cla_re.shape[0]

    def body(dt_ref, cr_ref, ci_ref, br_ref, bi_ref, dcr_ref, dci_ref, dbr_ref, dbi_ref):
        lane = lax.broadcasted_iota(jnp.int32, (SSM_H, S5_LW), 1)
        dks = []
        for g in range(_S5_GB):
            dk = dt_ref[g, 0:SSM_H, :]
            for lp in range(1, S5_L):
                rows = dt_ref[g, lp * SSM_H:(lp + 1) * SSM_H, :]
                dk = dk + jnp.where(lane < S5_LW - lp * SSM_H, pltpu.roll(rows, S5_LW - lp * SSM_H, 1), 0.0)
            dks.append(dk)
        dkt = jnp.stack(dks)
        dbr_ref[...] = _bdot(dkt, cr_ref[...], _B_NN)
        dbi_ref[...] = -_bdot(dkt, ci_ref[...], _B_NN)
        dcr_ref[...] = _bdot(dkt, br_ref[...], _B_TN)
        dci_ref[...] = -_bdot(dkt, bi_ref[...], _B_TN)

    sd = jax.ShapeDtypeStruct
    return pl.pallas_call(
        body, name=name,
        out_shape=(sd((n_g, S5_LW, SSM_P), F32), sd((n_g, S5_LW, SSM_P), F32), sd((n_g, SSM_H, SSM_P), F32),
                   sd((n_g, SSM_H, SSM_P), F32)),
        grid=(n_g // _S5_GB,),
        in_specs=[_gspec((S5_LW, S5_LW)), _gspec((S5_LW, SSM_P)), _gspec((S5_LW, SSM_P)), _gspec((SSM_H, SSM_P)),
                  _gspec((SSM_H, SSM_P))],
        out_specs=(_gspec((S5_LW, SSM_P)), _gspec((S5_LW, SSM_P)), _gspec((SSM_H, SSM_P)), _gspec((SSM_H, SSM_P))),
        compiler_params=_cparams(("parallel",)),
    )(d_tgt, cla_re, cla_im, bbt_re, bbt_im)


_S5_SW = 2 * SSM_P


def _sspec(C):
    return pl.BlockSpec((C, _S5_GB, _S5_SW), lambda g: (0, g, 0))


def _group_major(s_ref):
    return jnp.stack([s_ref[:, k, :] for k in range(_S5_GB)])


def _chunk_major(o_ref, v):
    for k in range(_S5_GB):
        o_ref[:, k, :] = v[k]


def _rot_coefs(lr, li, sign):
    return jnp.concatenate([lr, lr], axis=1), jnp.concatenate([-sign * li, sign * li], axis=1)


def _s5_in(u, bt, name):
    C = u.shape[1]

    def body(u_ref, b_ref, s_ref):
        _chunk_major(s_ref, _bdot(u_ref[...], b_ref[...], _B_NN))

    return pl.pallas_call(
        body, name=name, out_shape=jax.ShapeDtypeStruct((C, SSM_G, _S5_SW), F32), grid=(SSM_G // _S5_GB,),
        in_specs=[_gspec((C, S5_LW)), _gspec((S5_LW, _S5_SW))], out_specs=_sspec(C),
        compiler_params=_cparams(("parallel",)),
    )(u, bt)


def _s5_scan(sin, lr, li, name):
    C = sin.shape[0]

    def body(i_ref, lr_ref, li_ref, o_ref):
        a, b = _rot_coefs(lr_ref[...], li_ref[...], 1.0)

        def step(c, s):
            o_ref[c] = s
            return a * s + b * pltpu.roll(s, SSM_P, 1) + i_ref[c]

        lax.fori_loop(0, C, step, jnp.zeros((SSM_G, _S5_SW), F32))

    return pl.pallas_call(body, name=name, out_shape=jax.ShapeDtypeStruct((C, SSM_G, _S5_SW), F32),
                          compiler_params=_cparams())(sin, lr, li)


def _s5_out(u, sp, tgt, ct, dflat, name):
    C = u.shape[1]

    def body(u_ref, s_ref, t_ref, c_ref, d_ref, pre_ref, y_ref):
        uv = u_ref[...]
        pre = _bdot(uv, t_ref[...], _B_NN) + _bdot(_group_major(s_ref), c_ref[...], _B_NN) + d_ref[...] * uv
        pre_ref[...] = pre
        y_ref[...] = jax.nn.gelu(pre)

    return pl.pallas_call(
        body, name=name,
        out_shape=(jax.ShapeDtypeStruct((SSM_G, C, S5_LW), F32), jax.ShapeDtypeStruct((SSM_G, C, S5_LW), F32)),
        grid=(SSM_G // _S5_GB,),
        in_specs=[_gspec((C, S5_LW)), _sspec(C), _gspec((S5_LW, S5_LW)), _gspec((_S5_SW, S5_LW)), _gspec((1, S5_LW))],
        out_specs=(_gspec((C, S5_LW)), _gspec((C, S5_LW))), compiler_params=_cparams(("parallel",)),
    )(u, sp, tgt, ct, dflat)


def _s5_bwd_out(dy, pre, u, sp, ct, name):
    C = u.shape[1]

    def body(dy_ref, pre_ref, u_ref, s_ref, c_ref, dpre_ref, ds_ref, dt_ref, dc_ref, dd_ref):
        _, pull = jax.vjp(jax.nn.gelu, pre_ref[...])
        dpre = pull(dy_ref[...])[0]
        uv = u_ref[...]
        dpre_ref[...] = dpre
        _chunk_major(ds_ref, _bdot(dpre, c_ref[...], _B_NT))
        dt_ref[...] = _bdot(uv, dpre, _B_TN)
        dc_ref[...] = _bdot(_group_major(s_ref), dpre, _B_TN)
        dd_ref[...] = jnp.sum(dpre * uv, axis=1, keepdims=True)

    sd = jax.ShapeDtypeStruct
    return pl.pallas_call(
        body, name=name,
        out_shape=(sd((SSM_G, C, S5_LW), F32), sd((C, SSM_G, _S5_SW), F32), sd((SSM_G, S5_LW, S5_LW), F32),
                   sd((SSM_G, _S5_SW, S5_LW), F32), sd((SSM_G, 1, S5_LW), F32)),
        grid=(SSM_G // _S5_GB,),
        in_specs=[_gspec((C, S5_LW)), _gspec((C, S5_LW)), _gspec((C, S5_LW)), _sspec(C), _gspec((_S5_SW, S5_LW))],
        out_specs=(_gspec((C, S5_LW)), _sspec(C), _gspec((S5_LW, S5_LW)), _gspec((_S5_SW, S5_LW)), _gspec((1, S5_LW))),
        compiler_params=_cparams(("parallel",)),
    )(dy, pre, u, sp, ct)


def _s5_rscan(dsp, sp, lr, li, name):
    C = dsp.shape[0]

    def body(g_ref, s_ref, lr_ref, li_ref, o_ref, dlr_ref, dli_ref):
        a, b = _rot_coefs(lr_ref[...], li_ref[...], -1.0)

        def step(k, carry):
            c = C - 1 - k
            d, acc_same, acc_swap = carry
            o_ref[c] = d
            s = s_ref[c]
            return (g_ref[c] + a * d + b * pltpu.roll(d, SSM_P, 1), acc_same + d * s,
                    acc_swap + d * pltpu.roll(s, SSM_P, 1))

        z = jnp.zeros((SSM_G, _S5_SW), F32)
        _, same, swap = lax.fori_loop(0, C, step, (z, z, z))
        dlr_ref[...] = same[:, :SSM_P] + same[:, SSM_P:]
        dli_ref[...] = swap[:, SSM_P:] - swap[:, :SSM_P]

    sd = jax.ShapeDtypeStruct((C, SSM_G, _S5_SW), F32)
    sp_ = jax.ShapeDtypeStruct((SSM_G, SSM_P), F32)
    return pl.pallas_call(body, name=name, out_shape=(sd, sp_, sp_), compiler_params=_cparams())(dsp, sp, lr, li)


def _s5_bwd_in(dpre, dsin, u, tgt, bt, dflat, name):
    C = u.shape[1]

    def body(dp_ref, ds_ref, u_ref, t_ref, b_ref, d_ref, du_ref, db_ref):
        dp = dp_ref[...]
        ds = _group_major(ds_ref)
        du_ref[...] = _bdot(dp, t_ref[...], _B_NT) + _bdot(ds, b_ref[...], _B_NT) + d_ref[...] * dp
        db_ref[...] = _bdot(u_ref[...], ds, _B_TN)

    sd = jax.ShapeDtypeStruct
    return pl.pallas_call(
        body, name=name,
        out_shape=(sd((SSM_G, C, S5_LW), F32), sd((SSM_G, S5_LW, _S5_SW), F32)),
        grid=(SSM_G // _S5_GB,),
        in_specs=[_gspec((C, S5_LW)), _sspec(C), _gspec((C, S5_LW)), _gspec((S5_LW, S5_LW)), _gspec((S5_LW, _S5_SW)),
                  _gspec((1, S5_LW))],
        out_specs=(_gspec((C, S5_LW)), _gspec((S5_LW, _S5_SW))),
        compiler_params=_cparams(("parallel",)),
    )(dpre, dsin, u, tgt, bt, dflat)


_LANES = 128
_GROUPS_PER_TILE = _LANES // SSM_H


def _to_chunks(v, col0, name):
    T = v.shape[0]
    C = T // S5_L

    def body(x_ref, o_ref):
        for l in range(S5_L):
            xl = x_ref[pl.ds(l, C, stride=S5_L), :]
            for k in range(_GROUPS_PER_TILE):
                o_ref[k, :, l * SSM_H:(l + 1) * SSM_H] = xl[:, k * SSM_H:(k + 1) * SSM_H]

    return pl.pallas_call(
        body, name=name, out_shape=jax.ShapeDtypeStruct((SSM_G, C, S5_LW), F32), grid=(SSM_G // _GROUPS_PER_TILE,),
        in_specs=[pl.BlockSpec((T, _LANES), lambda g: (0, col0 // _LANES + g))],
        out_specs=pl.BlockSpec((_GROUPS_PER_TILE, C, S5_LW), lambda g: (g, 0, 0)),
        compiler_params=_cparams(("parallel",)),
    )(v)


def _from_chunks(v, out_dtype, name):
    C = v.shape[1]
    T = C * S5_L

    def body(f_ref, o_ref, rows):
        for l in range(S5_L):
            rows[pl.ds(l, C, stride=S5_L), :] = jnp.concatenate(
                [f_ref[k, :, l * SSM_H:(l + 1) * SSM_H] for k in range(_GROUPS_PER_TILE)], axis=1)
        o_ref[...] = rows[...].astype(out_dtype)

    return pl.pallas_call(
        body, name=name, out_shape=jax.ShapeDtypeStruct((T, MIX_W), out_dtype), grid=(SSM_G // _GROUPS_PER_TILE,),
        in_specs=[pl.BlockSpec((_GROUPS_PER_TILE, C, S5_LW), lambda g: (g, 0, 0))],
        out_specs=pl.BlockSpec((T, _LANES), lambda g: (0, g)),
        scratch_shapes=[pltpu.VMEM((T, _LANES), F32)],
        compiler_params=_cparams(("parallel",)),
    )(v)


def _merge_fn(bra, brb, pc, pg, g0, g1, g2, b0, b1, b2):
    sg = jax.nn.sigmoid
    return sg(g0 + b0) * bra + sg(g1 + b1) * brb + sg(g2 + b2) * (pc * sg(pg))


_EW_CS = 256
_GATE_OFF = 6 * MIX_W


def _merge_rows(br4, proj):
    return [(b, 0, D_MODEL) for b in br4] + [(proj, _GATE_OFF + k * D_MODEL, D_MODEL) for k in range(3)]


def _merge_fwd(br4, proj, gb3, name):
    def fn(rp, cp):
        b = cp[0]
        return [_merge_fn(*rp, b[0:1], b[1:2], b[2:3])], []

    return _rowwise(fn, _merge_rows(br4, proj), [gb3], [(D_MODEL, BF16)], [], R=proj.shape[0], name=name, cs=_EW_CS)[0]


def _merge_bwd(br4, proj, gb3, dm, name):
    def fn(rp, cp):
        b = cp[0]
        shp = rp[0].shape
        bs = [jnp.broadcast_to(b[k:k + 1], shp) for k in range(3)]
        _, pull = jax.vjp(_merge_fn, *rp[:7], *bs)
        g = pull(rp[7])
        return list(g[:7]), list(g[7:])

    rows = _merge_rows(br4, proj) + [(dm, 0, D_MODEL)]
    outs = _rowwise(fn, rows, [gb3], [(D_MODEL, BF16)] * 7, [D_MODEL] * 3, R=proj.shape[0], name=name, tm=128, cs=_EW_CS)
    return outs[:4], outs[4:7], jnp.concatenate(outs[7:], axis=0)


def _swiglu(g, u):
    return jax.nn.silu(g) * u


def _act_fwd(gu, name):
    def fn(rp, cp):
        return [_swiglu(*rp)], []

    return _rowwise(fn, [(gu[0], 0, FFN_H), (gu[1], 0, FFN_H)], [], [(FFN_H, BF16)], [], R=gu[0].shape[0], name=name, cs=_EW_CS)[0]


def _act_bwd(gu, dact, name):
    def fn(rp, cp):
        _, pull = jax.vjp(_swiglu, rp[0], rp[1])
        return list(pull(rp[2])), []

    return _rowwise(fn, [(gu[0], 0, FFN_H), (gu[1], 0, FFN_H), (dact, 0, FFN_H)], [], [(FFN_H, BF16)] * 2, [],
                    R=dact.shape[0], name=name, tm=128, cs=_EW_CS)


def _adamw_fn(w, g, m, v):
    m = ADAM_B1 * m + (1.0 - ADAM_B1) * g
    v = ADAM_B2 * v + (1.0 - ADAM_B2) * jnp.square(g)
    m_hat = m / (1.0 - ADAM_B1 ** ADAM_STEP)
    v_hat = v / (1.0 - ADAM_B2 ** ADAM_STEP)
    delta = -ADAM_LR * (m_hat / (jnp.sqrt(v_hat) + ADAM_EPS) + ADAM_WD * w)
    return delta, m, v


def _adamw(w, g, m, v, name, side=None):
    R, C = w.shape

    def fn(rp, cp):
        return list(_adamw_fn(*rp)), []

    cs = _pick(C, (512, 256, 128))
    tm = _pick(R, (256, 128, 64, 32, 16, 8))
    return _rowwise(fn, [(w, 0, C), (g, 0, C), (m, 0, C), (v, 0, C)], [], [(C, F32)] * 3, [], R=R, name=name,
                    tm=tm, rs=8, cs=cs, side=side)


def _my_place():
    return lax.axis_index("x"), lax.axis_index("y"), lax.axis_index("c")


def _other_chips(x, y):
    return [(1 - x, y), (x, 1 - y), (1 - x, 1 - y)]


_ANY = pl.BlockSpec(memory_space=pl.ANY)


def _rcopy(src, dst, ssem, rsem, to):
    return pltpu.make_async_remote_copy(src_ref=src, dst_ref=dst, send_sem=ssem, recv_sem=rsem, device_id=to,
                                        device_id_type=MESH)


def _place_shard(local, axis, jidx, out_dtype, name):
    lead, r, c = local.shape
    shp = [lead, r, c]
    shp[axis] *= N_CHIPS
    tr = _pick(r, (512, 256, 128)) if r % 128 == 0 else r
    nr = r // tr
    omap = (lambda l, i, j: (l, i, j[0])) if axis == 2 else (lambda l, i, j: (l, j[0] * nr + i, 0))

    def body(j_ref, x_ref, o_ref):
        o_ref[...] = x_ref[...].astype(out_dtype)

    return pl.pallas_call(
        body, name=name, out_shape=jax.ShapeDtypeStruct(tuple(shp), out_dtype),
        grid_spec=pltpu.PrefetchScalarGridSpec(
            num_scalar_prefetch=1, grid=(lead, nr),
            in_specs=[pl.BlockSpec((None, tr, c), lambda l, i, j: (l, i, 0))],
            out_specs=pl.BlockSpec((None, tr, c), omap)),
        compiler_params=_cparams(("parallel", "parallel")),
    )(jidx, local)


class _GatherSide:
    def __init__(self, fulls, axes, regions):
        self.inputs = list(fulls)
        self.out_shapes = [jax.ShapeDtypeStruct(f.shape, f.dtype) for f in fulls]
        self.aliases = {t: t for t in range(len(fulls))}
        self.n_sems = 6 * len(fulls)
        self.axes, self.regions = list(axes), list(regions)

    def _block(self, outs, t, chip, half):
        start, size, split = self.regions[t][:3]
        piece, n_pieces = self.regions[t][3] if len(self.regions[t]) > 3 else (0, 1)
        ax = self.axes[t]
        cut = outs[t].shape[ax] // N_CHIPS
        j = 2 * chip[0] + chip[1]
        idx = [pl.ds(start, size), slice(None), slice(None)]
        idx[ax] = pl.ds(j * cut, cut)
        if split == "lead":
            idx[0] = pl.ds(start + half * (size // 2), size // 2)
        else:
            other = 3 - ax
            h = outs[t].shape[other] // (2 * n_pieces)
            idx[other] = pl.ds((half * n_pieces + piece) * h, h)
        return outs[t].at[tuple(idx)]

    def _sends(self, outs, send, recv):
        x, y, c = _my_place()
        cps = []
        for t in range(len(outs)):
            mine = self._block(outs, t, (x, y), c)
            for r, chip in enumerate(_other_chips(x, y)):
                k = 3 * t + r
                cps.append(_rcopy(mine, mine, send.at[k], recv.at[k], (*chip, c)))
        return cps

    def start(self, ins, outs, send, recv):
        for cp in self._sends(outs, send, recv):
            cp.start()

    def finish(self, ins, outs, send, recv):
        x, y, c = _my_place()
        sib = (x, y, 1 - c)
        n = len(outs)
        chips = _other_chips(x, y)
        passed = []
        for t in range(n):
            for r, chip in enumerate(chips):
                k = 3 * t + r
                landed = self._block(outs, t, chip, c)
                _rcopy(landed, landed, send.at[k], recv.at[k], (*chip, c)).wait_recv()
                cp = _rcopy(landed, landed, send.at[3 * n + k], recv.at[3 * n + k], sib)
                cp.start()
                passed.append(cp)
        for t in range(n):
            for r, chip in enumerate(chips):
                k = 3 * n + 3 * t + r
                theirs = self._block(outs, t, chip, 1 - c)
                _rcopy(theirs, theirs, send.at[k], recv.at[k], sib).wait_recv()
        for cp in self._sends(outs, send, recv) + passed:
            cp.wait_send()


def _run_side(side, name):
    s_in = len(side.inputs)

    def body(*refs):
        ins, outs = refs[:s_in], refs[s_in:s_in + len(side.out_shapes)]
        send, recv = refs[s_in + len(side.out_shapes):]
        side.start(ins, outs, send, recv)
        side.finish(ins, outs, send, recv)

    return pl.pallas_call(
        body, name=name, out_shape=tuple(side.out_shapes), in_specs=[_ANY] * s_in,
        out_specs=tuple([_ANY] * len(side.out_shapes)), input_output_aliases=dict(side.aliases),
        scratch_shapes=[pltpu.SemaphoreType.DMA((side.n_sems,)), pltpu.SemaphoreType.DMA((side.n_sems,))],
    )(*side.inputs)


def _half_idx(shape, axis, half):
    size = shape[axis] // 2
    idx = [slice(None), slice(None)]
    idx[axis] = pl.ds(half * size, size)
    return tuple(idx)


class _PairSide:
    def __init__(self, grads, half_axes):
        self.inputs = list(grads)
        self.half_axes = list(half_axes)
        self.out_shapes = []
        for g, ax in zip(grads, half_axes):
            shp = list(g.shape)
            shp[ax] //= 2
            self.out_shapes.append(jax.ShapeDtypeStruct(tuple(shp), g.dtype))
        self.aliases = {}
        self.n_sems = len(grads)

    def _copies(self, srcs, outs, send, recv):
        x, y, c = _my_place()
        return [_rcopy(srcs[t].at[_half_idx(srcs[t].shape, self.half_axes[t], 1 - c)], outs[t], send.at[t], recv.at[t],
                       (x, y, 1 - c)) for t in range(len(srcs))]

    def start(self, srcs, outs, send, recv):
        for cp in self._copies(srcs, outs, send, recv):
            cp.start()

    def finish(self, srcs, outs, send, recv):
        for cp in self._copies(srcs, outs, send, recv):
            cp.wait()


def _pair_sum(g, recv, half_axis, cidx, name):
    K, N = recv.shape
    tm = _pick(K, (256, 128, 64, 32, 16))
    tn = _pick(N, (1024, 1408, 512, 256, 128))
    nbr, nbc = K // tm, N // tn
    if half_axis == 0:
        gmap = lambda i, j, c: (c[0] * nbr + i, j)
    else:
        gmap = lambda i, j, c: (i, c[0] * nbc + j)

    def body(c_ref, g_ref, r_ref, of_ref, ob_ref):
        s = g_ref[...] + r_ref[...]
        of_ref[...] = s
        ob_ref[...] = s.astype(BF16)

    omap = lambda i, j, c: (i, j)
    return pl.pallas_call(
        body, name=name,
        out_shape=(jax.ShapeDtypeStruct((K, N), F32), jax.ShapeDtypeStruct((K, N), BF16)),
        grid_spec=pltpu.PrefetchScalarGridSpec(
            num_scalar_prefetch=1, grid=(nbr, nbc),
            in_specs=[pl.BlockSpec((tm, tn), gmap), pl.BlockSpec((tm, tn), omap)],
            out_specs=(pl.BlockSpec((tm, tn), omap), pl.BlockSpec((tm, tn), omap))),
        compiler_params=_cparams(("parallel", "parallel")),
    )(cidx, g, recv)


def _shard_idx(shape, axis, j):
    size = shape[axis] // N_CHIPS
    idx = [slice(None), slice(None)]
    idx[axis] = pl.ds(j * size, size)
    return tuple(idx)


class _ScatterSide:
    def __init__(self, parts, shard_axes):
        self.inputs = list(parts)
        self.shard_axes = list(shard_axes)
        self.out_shapes = []
        for p, ax in zip(parts, shard_axes):
            shp = list(p.shape)
            shp[ax] //= N_CHIPS
            self.out_shapes.append(jax.ShapeDtypeStruct((3,) + tuple(shp), p.dtype))
        self.aliases = {}
        self.n_sems = 3 * len(parts)

    def _copies(self, srcs, outs, send, recv):
        x, y, c = _my_place()
        cps = []
        for t in range(len(srcs)):
            for r, chip in enumerate(_other_chips(x, y)):
                k = 3 * t + r
                j = 2 * chip[0] + chip[1]
                cps.append(_rcopy(srcs[t].at[_shard_idx(srcs[t].shape, self.shard_axes[t], j)], outs[t].at[r],
                                  send.at[k], recv.at[k], (*chip, c)))
        return cps

    def start(self, srcs, outs, send, recv):
        for cp in self._copies(srcs, outs, send, recv):
            cp.start()

    def finish(self, srcs, outs, send, recv):
        for cp in self._copies(srcs, outs, send, recv):
            cp.wait()


def _shard_sum(pf, recv, acc, layer, shard_axis, jcidx, name):
    _, K, N = recv.shape
    tm = _pick(K, (256, 128, 64, 32, 16))
    tn = _pick(N, (1024, 1408, 512, 256, 128))
    nbr, nbc = K // tm, N // tn
    if shard_axis == 0:
        pmap = lambda i, j, s: (s[0] * nbr + i, j)
        omap = lambda i, j, s: (layer, i, s[1] * nbc + j)
    else:
        pmap = lambda i, j, s: (i, s[0] * nbc + j)
        omap = lambda i, j, s: (layer, s[1] * nbr + i, j)

    def body(j_ref, p_ref, r_ref, a_ref, o_ref):
        o_ref[...] = ((p_ref[...] + r_ref[0].astype(F32)) + r_ref[1].astype(F32)) + r_ref[2].astype(F32)

    return pl.pallas_call(
        body, name=name, out_shape=jax.ShapeDtypeStruct(acc.shape, F32),
        grid_spec=pltpu.PrefetchScalarGridSpec(
            num_scalar_prefetch=1, grid=(nbr, nbc),
            in_specs=[pl.BlockSpec((tm, tn), pmap), pl.BlockSpec((3, tm, tn), lambda i, j, s: (0, i, j)), _ANY],
            out_specs=pl.BlockSpec((None, tm, tn), omap)),
        input_output_aliases={3: 0},
        compiler_params=_cparams(("parallel", "parallel")),
    )(jcidx, pf, recv, acc)


def _pair_join(accs, half_axes, name):
    n = len(accs)

    def body(*refs):
        outs = refs[n:2 * n]
        send_sems, recv_sems = refs[2 * n:]
        x, y, c = _my_place()
        sib = (x, y, 1 - c)

        def half(t, hc):
            return outs[t].at[(slice(None),) + _half_idx(outs[t].shape[1:], half_axes[t], hc)]

        cps = []
        for t in range(n):
            cp = _rcopy(half(t, c), half(t, c), send_sems.at[t], recv_sems.at[t], sib)
            cp.start()
            cps.append(cp)
        for t in range(n):
            _rcopy(half(t, 1 - c), half(t, 1 - c), send_sems.at[t], recv_sems.at[t], sib).wait_recv()
        for cp in cps:
            cp.wait_send()

    return pl.pallas_call(
        body, name=name, out_shape=tuple(jax.ShapeDtypeStruct(a.shape, a.dtype) for a in accs),
        in_specs=[_ANY] * n, out_specs=tuple([_ANY] * n), input_output_aliases={t: t for t in range(n)},
        scratch_shapes=[pltpu.SemaphoreType.DMA((n,)), pltpu.SemaphoreType.DMA((n,))],
    )(*accs)


_N_DEV = 8


def _allreduce_small(flat, name):
    _, R, _ = flat.shape

    def body(in_ref, out_ref, stage, send1, recv1, send2, recv2):
        x, y, c = _my_place()
        me = 4 * x + 2 * y + c
        places = [(px, py, pc) for px in range(2) for py in range(2) for pc in range(2)]
        def peer(r):
            return (x ^ (r >> 2), y ^ ((r >> 1) & 1), c ^ (r & 1))

        def peer_id(r):
            p = peer(r)
            return 4 * p[0] + 2 * p[1] + p[2]

        stage[0] = in_ref[me]
        cps = []
        for r in range(1, _N_DEV):
            cp = _rcopy(in_ref.at[peer_id(r)], stage.at[r], send1.at[r], recv1.at[r], peer(r))
            cp.start()
            cps.append(cp)
        for cp in cps:
            cp.wait()
        tot = jnp.zeros((R, 128), F32)
        for d in range(_N_DEV):
            tot = tot + stage[me ^ d]
        out_ref[me] = tot
        cps = []
        for r in range(1, _N_DEV):
            cp = _rcopy(out_ref.at[me], out_ref.at[me], send2.at[r], recv2.at[r], peer(r))
            cp.start()
            cps.append(cp)
        for r in range(1, _N_DEV):
            _rcopy(out_ref.at[peer_id(r)], out_ref.at[peer_id(r)], send2.at[r], recv2.at[r], peer(r)).wait_recv()
        for cp in cps:
            cp.wait_send()

    vm = pl.BlockSpec(memory_space=pltpu.VMEM)
    return pl.pallas_call(
        body, name=name, out_shape=jax.ShapeDtypeStruct(flat.shape, F32), in_specs=[vm], out_specs=vm,
        scratch_shapes=[pltpu.VMEM(flat.shape, F32)] + [pltpu.SemaphoreType.DMA((_N_DEV,))] * 4,
        compiler_params=_cparams(),
    )(flat)


_BIG = ("w_in", "ssm_w_glu", "w_branch", "w_out", "w_ffn_gate", "w_ffn_up", "w_ffn_down")
_BIG_SHARD_AXIS = {"w_in": 1, "ssm_w_glu": 1, "w_branch": 1, "w_out": 0, "w_ffn_gate": 1, "w_ffn_up": 1, "w_ffn_down": 0}
_SMALL = ("norm_mix_g", "gate_bias", "lru_conv_w", "lru_conv_b", "lru_wa", "lru_ba", "lru_wx", "lru_bx", "lru_lambda",
          "attn_rel_bias", "ssm_a_re", "ssm_a_im", "ssm_b_re", "ssm_b_im", "ssm_c_re", "ssm_c_im", "ssm_d",
          "ssm_log_step", "norm_ffn_g", "norm_final_g")
_SMALL_SHARDED = {"gate_bias": 2, "lru_conv_w": 2}
_WEIGHTS = ("norm_mix_g", "w_in", "gate_bias", "lru_conv_w", "lru_conv_b", "lru_wa", "lru_ba", "lru_wx", "lru_bx",
            "lru_lambda", "attn_rel_bias", "ssm_a_re", "ssm_a_im", "ssm_b_re", "ssm_b_im", "ssm_c_re", "ssm_c_im",
            "ssm_d", "ssm_log_step", "ssm_w_glu", "w_branch", "w_out", "norm_ffn_g", "w_ffn_gate", "w_ffn_up",
            "w_ffn_down", "norm_final_g")


def _carried(comm, phase, l, key, W, fn, *args, **kw):
    side = comm.side(phase, l, key, W)
    if side is None:
        return fn(*args, **kw)
    out, got = fn(*args, side=side, **kw)
    comm.took(phase, l, key, got, W)
    return out


def _layer_fwd(l, x, W, sm, comm):
    T = x.shape[0]
    nm = lambda s: f"{s}"
    h1 = _rms_fwd(x, sm["norm_mix_g"][l][None, :], nm("rms_fwd"))
    proj = _carried(comm, "fwd", l, "mm_in", W, _mm, h1, W["w_in"], M=T, N=IN_W, K=D_MODEL, b_lead=l, name=nm("mm_in"))
    wax, bax = sm["lru_prep"][l]
    cw, cb, lam = sm["lru_conv_w"][l], sm["lru_conv_b"][l][None, :], sm["lru_lambda"][l][None, :]
    y_a, hst = _lru_fwd(proj, cw, cb, wax, bax, lam, nm("lru_fwd"))
    bias = sm["attn_bias"][l]
    y_b = _carried(comm, "fwd", l, "attn_fwd", W, _attn_fwd, proj, bias, nm("attn_fwd"))
    tgt, bt, ct, lr, li, dflat = sm["s5_prep"][l]
    u = _to_chunks(proj, 5 * MIX_W, nm("to_chunks"))
    sp = _s5_scan(_s5_in(u, bt, nm("s5_in")), lr, li, nm("s5_scan"))
    pre, ycf = _s5_out(u, sp, tgt, ct, dflat, nm("s5_out"))
    y_c = _from_chunks(ycf, BF16, nm("from_chunks"))
    brs = []
    for k, yk in enumerate((y_a, y_b, y_c)):
        brs.append(_mm(yk, W["w_branch"], M=T, N=D_MODEL, K=MIX_W, b_lead=3 * l + k, out_dtype=BF16, name=nm("mm_branch")))
    brs.append(_mm(y_c, W["ssm_w_glu"], M=T, N=D_MODEL, K=MIX_W, b_lead=l, out_dtype=BF16, name=nm("mm_branch")))
    br4 = tuple(brs)
    gb3 = sm["gate_bias"][l]
    merged = _merge_fwd(br4, proj, gb3, nm("merge_fwd"))
    x1 = _mm(merged, W["w_out"], M=T, N=D_MODEL, K=D_MODEL, b_lead=l, res=x, name=nm("mm_out"))
    h2 = _rms_fwd(x1, sm["norm_ffn_g"][l][None, :], nm("rms_fwd"))
    gpre = _carried(comm, "fwd", l, "mm_ffn_gate", W, _mm, h2, W["w_ffn_gate"], M=T, N=FFN_H, K=D_MODEL, b_lead=l,
                    out_dtype=BF16, name=nm("mm_ffn_up"))
    upre = _carried(comm, "fwd", l, "mm_ffn_up", W, _mm, h2, W["w_ffn_up"], M=T, N=FFN_H, K=D_MODEL, b_lead=l,
                    out_dtype=BF16, name=nm("mm_ffn_up"))
    gu = (gpre, upre)
    act = _act_fwd(gu, nm("act_fwd"))
    x2 = _carried(comm, "fwd", l, "mm_down", W, _mm, act, W["w_ffn_down"], M=T, N=D_MODEL, K=FFN_H, b_lead=l, res=x1,
                  name=nm("mm_down"))
    saved = dict(x=x, h1=h1, proj=proj, hst=hst, y_a=y_a, y_b=y_b, y_c=y_c, u=u, sp=sp, pre=pre,
                 br4=br4, merged=merged, x1=x1, h2=h2, gu=gu, act=act)
    return x2, saved


def _layer_bwd(l, dx2, sv, W, sm, comm):
    T = dx2.shape[0]
    nm = lambda s: f"{s}"
    big, small = {}, {}
    dxb = dx2.astype(BF16)
    big["w_ffn_down"] = _carried(comm, "bwd", l, "mm_dw_down", W, _mm, sv["act"], dxb, M=FFN_H, N=D_MODEL, K=T, ta=True,
                                 name=nm("mm_dw_down"))
    dact = _carried(comm, "bwd", l, "mm_dact", W, _mm, dxb, W["w_ffn_down"], M=T, N=FFN_H, K=D_MODEL, tb=True, b_lead=l,
                    out_dtype=BF16, name=nm("mm_dact"))
    dg, du = _act_bwd(sv["gu"], dact, nm("act_bwd"))
    big["w_ffn_gate"] = _carried(comm, "bwd", l, "mm_dw_gate", W, _mm, sv["h2"], dg, M=D_MODEL, N=FFN_H, K=T, ta=True,
                                 name=nm("mm_dw_up"))
    big["w_ffn_up"] = _mm(sv["h2"], du, M=D_MODEL, N=FFN_H, K=T, ta=True, name=nm("mm_dw_up"))
    dh2 = _mm(dg, W["w_ffn_gate"], M=T, N=D_MODEL, K=FFN_H, tb=True, b_lead=l, name=nm("mm_dh2"))
    dh2 = _mm(du, W["w_ffn_up"], M=T, N=D_MODEL, K=FFN_H, tb=True, b_lead=l, res=dh2, name=nm("mm_dh2r"))
    dx1, dgn = _rms_bwd(sv["x1"], sm["norm_ffn_g"][l][None, :], dh2, dx2, nm("rms_bwd"))
    small["norm_ffn_g"] = dgn[0]
    dx1b = dx1.astype(BF16)
    big["w_out"] = _mm(sv["merged"], dx1b, M=D_MODEL, N=D_MODEL, K=T, ta=True, name=nm("mm_dw_out"))
    dm = _mm(dx1b, W["w_out"], M=T, N=D_MODEL, K=D_MODEL, tb=True, b_lead=l, out_dtype=BF16, name=nm("mm_dmerged"))
    dbr, dgates, dgb = _merge_bwd(sv["br4"], sv["proj"], sm["gate_bias"][l], dm, nm("merge_bwd"))
    small["gate_bias"] = dgb
    ys = (sv["y_a"], sv["y_b"], sv["y_c"])
    big["w_branch"] = [_mm(ys[k], dbr[k], M=MIX_W, N=D_MODEL, K=T, ta=True, name=nm("mm_dw_branch")) for k in range(3)]
    big["ssm_w_glu"] = _mm(sv["y_c"], dbr[3], M=MIX_W, N=D_MODEL, K=T, ta=True, name=nm("mm_dw_branch"))
    dya = _mm(dbr[0], W["w_branch"], M=T, N=MIX_W, K=D_MODEL, tb=True, b_lead=3 * l, name=nm("mm_dy"))
    dyb = _mm(dbr[1], W["w_branch"], M=T, N=MIX_W, K=D_MODEL, tb=True, b_lead=3 * l + 1, out_dtype=BF16, name=nm("mm_dy"))
    dyc = _mm(dbr[2], W["w_branch"], M=T, N=MIX_W, K=D_MODEL, tb=True, b_lead=3 * l + 2, name=nm("mm_dy"))
    dyc = _mm(dbr[3], W["ssm_w_glu"], M=T, N=MIX_W, K=D_MODEL, tb=True, b_lead=l, res=dyc, name=nm("mm_dyr"))
    tgt, bt, ct, lr, li, dflat = sm["s5_prep"][l]
    dpre, dsp, d_tgt, d_ct, d_dflat = _s5_bwd_out(_to_chunks(dyc, 0, nm("to_chunks")), sv["pre"], sv["u"], sv["sp"], ct,
                                                  nm("s5_bwd_out"))
    dsin, d_lr, d_li = _s5_rscan(dsp, sv["sp"], lr, li, nm("s5_rscan"))
    du_f, d_bt = _s5_bwd_in(dpre, dsin, sv["u"], tgt, bt, dflat, nm("s5_bwd_in"))
    d_u = _from_chunks(du_f, BF16, nm("from_chunks"))
    small["s5_tables"] = (d_tgt, d_bt, d_ct, d_lr, d_li, d_dflat)
    dq, dk, dv, dbias = _carried(comm, "bwd", l, "attn_bwd", W, _attn_bwd, sv["proj"], sm["attn_bias"][l], dyb, nm("attn_bwd"))
    small["attn_bias"] = dbias
    wax, bax = sm["lru_prep"][l]
    cw, cb, lam = sm["lru_conv_w"][l], sm["lru_conv_b"][l][None, :], sm["lru_lambda"][l][None, :]
    d_lx, d_lg, d_cw, d_cb, d_wax, d_bax, d_lam = _lru_bwd(sv["proj"], sv["hst"], dya, cw, cb, wax, bax, lam, nm("lru_bwd"))
    small["lru_conv_w"], small["lru_conv_b"], small["lru_lambda"] = d_cw, d_cb[0], d_lam[0]
    small["lru_tables"] = (d_wax, d_bax)
    dproj = jnp.concatenate([d_lx, d_lg, dq, dk, dv, d_u] + list(dgates), axis=1)
    big["w_in"] = _carried(comm, "bwd", l, "mm_dw_in", W, _mm, sv["h1"], dproj, M=D_MODEL, N=IN_W, K=T, ta=True,
                           name=nm("mm_dw_in"))
    dh1 = _carried(comm, "bwd", l, "mm_dh1", W, _mm, dproj, W["w_in"], M=T, N=D_MODEL, K=IN_W, tb=True, b_lead=l,
                   name=nm("mm_dh1"))
    dx, dgn = _rms_bwd(sv["x"], sm["norm_mix_g"][l][None, :], dh1, dx1, nm("rms_bwd"))
    small["norm_mix_g"] = dgn[0]
    return dx, big, small


_TENSORS = tuple((n, k) for n in _BIG for k in range(N_BRANCH if n == "w_branch" else 1))
_BY_SIZE = ("w_in", "w_ffn_gate", "w_ffn_up", "w_ffn_down", "w_branch", "w_out", "ssm_w_glu")
_FWD_CARRIERS = {"mm_in": (("w_out", "w_branch", "ssm_w_glu", "w_ffn_gate"), 0), "attn_fwd": (("w_ffn_up",), 0),
                 "mm_ffn_gate": (("w_ffn_down",), 0), "mm_ffn_up": (("w_in",), 1), "mm_down": (("w_in",), 1)}
_FWD_PIECES = {"mm_ffn_up": (0, 2), "mm_down": (1, 2)}
_BWD_PAIR_CARRIERS = {"mm_dw_down": ("w_in", "w_out", "ssm_w_glu"), "mm_dact": ("w_ffn_gate", "w_ffn_up", "w_ffn_down", "w_branch")}
_BWD_CARRIERS = {"mm_dw_gate": ("w_out", "w_branch", "ssm_w_glu"), "attn_bwd": ("w_ffn_down",), "mm_dw_in": ("w_in",),
                 "mm_dh1": ("w_ffn_gate", "w_ffn_up")}


class _StepComm:
    def __init__(self, depth, cidx, jcidx):
        self.depth, self.cidx, self.jcidx = depth, cidx, jcidx
        self.accs = {}
        self.raw = None
        self.paired = {}
        self.pending = None

    def gather_side(self, W, names, l, piece):
        fulls, axes, regions = [], [], []
        for n in names:
            per = N_BRANCH if n == "w_branch" else 1
            fulls.append(W[n])
            axes.append(_BIG_SHARD_AXIS[n] + 1)
            regions.append((per * l, per, "other", piece))
        return _GatherSide(fulls, axes, regions)

    def side(self, phase, l, key, W):
        if phase == "fwd":
            if key not in _FWD_CARRIERS or l + _FWD_CARRIERS[key][1] >= self.depth:
                return None
            names, ahead = _FWD_CARRIERS[key]
            return self.gather_side(W, names, l + ahead, _FWD_PIECES.get(key, (0, 1)))
        if key in _BWD_PAIR_CARRIERS and self.raw is not None:
            nks = [nk for nk in _TENSORS if nk[0] in _BWD_PAIR_CARRIERS[key]]
            return _PairSide([self.raw[1][nk] for nk in nks], [1 - _BIG_SHARD_AXIS[nk[0]] for nk in nks])
        if key in _BWD_CARRIERS and self.pending is not None:
            nks = [nk for nk in _TENSORS if nk[0] in _BWD_CARRIERS[key]]
            return _ScatterSide([self.pending[1][nk][1] for nk in nks], [_BIG_SHARD_AXIS[nk[0]] for nk in nks])
        return None

    def took(self, phase, l, key, got, W):
        if phase == "fwd":
            W.update(zip(_FWD_CARRIERS[key][0], got))
        elif key in _BWD_PAIR_CARRIERS:
            nks = [nk for nk in _TENSORS if nk[0] in _BWD_PAIR_CARRIERS[key]]
            self.paired.update(zip(nks, got))
            if len(self.paired) == len(_TENSORS):
                self.pair_sums()
        else:
            nks = [nk for nk in _TENSORS if nk[0] in _BWD_CARRIERS[key]]
            self.shard_sums(self.pending[0], nks, got)

    def pair_sums(self):
        l, grads = self.raw
        self.pending = (l, {nk: _pair_sum(grads[nk], self.paired[nk], 1 - _BIG_SHARD_AXIS[nk[0]], self.cidx, "pair_sum")
                            for nk in _TENSORS})
        self.raw, self.paired = None, {}

    def shard_sums(self, l, nks, got):
        for nk, r in zip(nks, got):
            pf = self.pending[1][nk][0]
            sa = _BIG_SHARD_AXIS[nk[0]]
            if nk not in self.accs:
                shp = list(pf.shape)
                shp[sa] //= N_CHIPS
                shp[1 - sa] *= 2
                self.accs[nk] = lax.empty((self.depth,) + tuple(shp), F32)
            self.accs[nk] = _shard_sum(pf, r, self.accs[nk], l, sa, self.jcidx, "shard_sum")

    def on_big(self, l, big):
        grads = {}
        for n in _BIG:
            gs = big[n] if isinstance(big[n], list) else [big[n]]
            grads.update({(n, k): g for k, g in enumerate(gs)})
        self.raw = (l, grads)
        if l == 0:
            half_axes = [1 - _BIG_SHARD_AXIS[nk[0]] for nk in _TENSORS]
            self.paired = dict(zip(_TENSORS, _run_side(_PairSide([grads[nk] for nk in _TENSORS], half_axes), "pair_exchange")))
            self.pair_sums()

    def last_scatter(self, name):
        nks = [nk for nk in _TENSORS if nk[0] == name]
        return _ScatterSide([self.pending[1][nk][1] for nk in nks], [_BIG_SHARD_AXIS[name]] * len(nks)), nks


class _NoComm:
    def __init__(self, on_big):
        self.on_big = on_big

    def side(self, phase, l, key, W):
        return None


def _local_step(xs, tgt, W, sm, comm):
    W = dict(W)
    sm = dict(sm)
    depth = sm["norm_mix_g"].shape[0]
    lru_o, lru_vjp = jax.vjp(jax.vmap(_lru_prep), sm["lru_wa"], sm["lru_wx"], sm["lru_ba"], sm["lru_bx"])
    attn_o, attn_vjp = jax.vjp(jax.vmap(_attn_bias), sm["attn_rel_bias"])
    s5_names = ("ssm_a_re", "ssm_a_im", "ssm_b_re", "ssm_b_im", "ssm_c_re", "ssm_c_im", "ssm_d", "ssm_log_step")
    s5_o, s5_vjp = jax.vjp(jax.vmap(_s5_prep), *[sm[n] for n in s5_names])
    wax_all = lru_o[0].astype(BF16)
    sm["lru_prep"] = [(wax_all[l], lru_o[1][l]) for l in range(depth)]
    sm["attn_bias"] = [attn_o[l] for l in range(depth)]
    kt_in = [t.reshape((depth * SSM_G,) + t.shape[2:]) for t in s5_o[:4]]
    tgt_all = _s5_tgt(*kt_in, "s5_tgt").reshape(depth, SSM_G, S5_LW, S5_LW)
    sm["s5_prep"] = [(tgt_all[l],) + tuple(t[l] for t in s5_o[4:]) for l in range(depth)]

    saved = []
    for l in range(depth):
        xs, sv = _layer_fwd(l, xs, W, sm, comm)
        saved.append(sv)
    loss_part, dx, dgf = _final_loss(xs, sm["norm_final_g"][None, :], tgt, "final_loss")

    direct = ("norm_mix_g", "gate_bias", "lru_conv_w", "lru_conv_b", "lru_lambda", "norm_ffn_g")
    per_layer = [None] * depth
    for l in reversed(range(depth)):
        dx, big, per_layer[l] = _layer_bwd(l, dx, saved[l], W, sm, comm)
        comm.on_big(l, big)
    stacked = lambda key, i: jnp.stack([per_layer[l][key][i] for l in range(depth)])
    small_tree = {n: jnp.stack([per_layer[l][n] for l in range(depth)]) for n in direct}
    d_wa, d_wx, d_ba, d_bx = lru_vjp(tuple(stacked("lru_tables", i) for i in range(2)))
    (d_rel,) = attn_vjp(jnp.stack([per_layer[l]["attn_bias"] for l in range(depth)]))
    d_kt_in = _s5_tgt_t(stacked("s5_tables", 0).reshape(depth * SSM_G, S5_LW, S5_LW), *kt_in, "s5_tgt_t")
    d_kt_in = [t.reshape((depth, SSM_G) + t.shape[1:]) for t in d_kt_in]
    d_s5 = s5_vjp(tuple(d_kt_in) + tuple(stacked("s5_tables", i) for i in range(1, 6)))
    small_tree.update(lru_wa=d_wa, lru_wx=d_wx, lru_ba=d_ba, lru_bx=d_bx, attn_rel_bias=d_rel, norm_final_g=dgf[0])
    small_tree.update(zip(s5_names, d_s5))
    return loss_part, dx, small_tree


def _pack_small(tree, names):
    flat = jnp.concatenate([tree[n].reshape(-1) for n in names])
    per = -(-flat.shape[0] // (_N_DEV * 128 * 8)) * (128 * 8)
    flat = jnp.pad(flat, (0, _N_DEV * per - flat.shape[0]))
    return flat.reshape(_N_DEV, per // 128, 128)


def _unpack_small(flat, like, names):
    flat = flat.reshape(-1)
    out, off = {}, 0
    for n in names:
        size = math.prod(like[n].shape)
        out[n] = flat[off:off + size].reshape(like[n].shape)
        off += size
    return out


def kernel(x, norm_mix_g, w_in, gate_bias, lru_conv_w, lru_conv_b, lru_wa, lru_ba, lru_wx, lru_bx, lru_lambda, attn_rel_bias, ssm_a_re, ssm_a_im, ssm_b_re, ssm_b_im, ssm_c_re, ssm_c_im, ssm_d, ssm_log_step, ssm_w_glu, w_branch, w_out, norm_ffn_g, w_ffn_gate, w_ffn_up, w_ffn_down, norm_final_g, loss_target, m_norm_mix_g, m_w_in, m_gate_bias, m_lru_conv_w, m_lru_conv_b, m_lru_wa, m_lru_ba, m_lru_wx, m_lru_bx, m_lru_lambda, m_attn_rel_bias, m_ssm_a_re, m_ssm_a_im, m_ssm_b_re, m_ssm_b_im, m_ssm_c_re, m_ssm_c_im, m_ssm_d, m_ssm_log_step, m_ssm_w_glu, m_w_branch, m_w_out, m_norm_ffn_g, m_w_ffn_gate, m_w_ffn_up, m_w_ffn_down, m_norm_final_g, v_norm_mix_g, v_w_in, v_gate_bias, v_lru_conv_w, v_lru_conv_b, v_lru_wa, v_lru_ba, v_lru_wx, v_lru_bx, v_lru_lambda, v_attn_rel_bias, v_ssm_a_re, v_ssm_a_im, v_ssm_b_re, v_ssm_b_im, v_ssm_c_re, v_ssm_c_im, v_ssm_d, v_ssm_log_step, v_ssm_w_glu, v_w_branch, v_w_out, v_norm_ffn_g, v_w_ffn_gate, v_w_ffn_up, v_w_ffn_down, v_norm_final_g):
    args = dict(locals())
    w = {n: args[n] for n in _WEIGHTS}
    m = {n: args["m_" + n] for n in _WEIGHTS}
    v = {n: args["v_" + n] for n in _WEIGHTS}
    depth = w_in.shape[0]
    xc, yc, cc = _my_place()
    jchip = 2 * xc + yc
    cidx = jnp.reshape(cc, (1,)).astype(jnp.int32)
    jidx = jnp.reshape(jchip, (1,)).astype(jnp.int32)
    jcidx = jnp.stack([jchip, cc]).astype(jnp.int32)

    blocks = [w[n] for n in _BIG]
    blocks[2] = blocks[2].reshape(depth * N_BRANCH, MIX_W, -1)
    axes = [_BIG_SHARD_AXIS[n] + 1 for n in _BIG] + [_SMALL_SHARDED[n] for n in _SMALL_SHARDED]
    placed = [_place_shard(b, ax, jidx, BF16, "place_shard") for b, ax in zip(blocks, axes)]
    placed += [_place_shard(w[n], _SMALL_SHARDED[n], jidx, F32, "place_shard") for n in _SMALL_SHARDED]
    W = dict(zip(_BIG, placed))
    first = [W["w_in"]] + placed[len(_BIG):]
    regions = [(0, 1, "other")] + [(0, depth, "lead")] * len(_SMALL_SHARDED)
    gathered = _run_side(_GatherSide(first, [axes[0]] + axes[len(_BIG):], regions), "gather_weights")
    W["w_in"] = gathered[0]
    sm_full = dict(zip(_SMALL_SHARDED, gathered[1:]))
    sm = {n: w[n] for n in _SMALL if n not in _SMALL_SHARDED}
    sm.update(sm_full)

    comm = _StepComm(depth, cidx, jcidx)
    loss_part, dx, small_tree = _local_step(x[0], loss_target[0], W, sm, comm)
    loss = lax.psum(loss_part, ("x", "y", "c"))
    grad_x = dx[None]

    grads, delta, new_m, new_v = {}, {}, {}, {}
    side, nks = comm.last_scatter(_BY_SIZE[0])
    comm.shard_sums(0, nks, _run_side(side, "chip_scatter"))
    for i, n in enumerate(_BY_SIZE):
        joined = _pair_join([comm.accs[nk] for nk in nks], [1 - _BIG_SHARD_AXIS[n]] * len(nks), "pair_join")
        grads[n] = jnp.stack(joined, axis=1) if n == "w_branch" else joined[0]
        shp = w[n].shape
        two = lambda a: a.reshape(-1, shp[-1])
        if i + 1 < len(_BY_SIZE):
            side, nks = comm.last_scatter(_BY_SIZE[i + 1])
            (d_, m_, v_), got = _adamw(two(w[n]), two(grads[n]), two(m[n]), two(v[n]), "adamw", side=side)
            comm.shard_sums(0, nks, got)
        else:
            d_, m_, v_ = _adamw(two(w[n]), two(grads[n]), two(m[n]), two(v[n]), "adamw")
        delta[n], new_m[n], new_v[n] = d_.reshape(shp), m_.reshape(shp), v_.reshape(shp)
    like = {n: (sm_full[n] if n in _SMALL_SHARDED else w[n]) for n in _SMALL}
    red = _unpack_small(_allreduce_small(_pack_small(small_tree, _SMALL), "allreduce_small"), like, _SMALL)
    for n in _SMALL:
        if n in _SMALL_SHARDED:
            size = w[n].shape[2]
            grads[n] = lax.dynamic_slice_in_dim(red[n], (2 * xc + yc) * size, size, axis=2)
        else:
            grads[n] = red[n]

    pk = lambda tree: _pack_small(tree, _SMALL).reshape(-1, 128)
    d_, m_, v_ = _adamw(pk(w), pk(grads), pk(m), pk(v), "adamw_small")
    like_local = {n: w[n] for n in _SMALL}
    for tree, flat in ((delta, d_), (new_m, m_), (new_v, v_)):
        tree.update(_unpack_small(flat, like_local, _SMALL))
    return (loss, grad_x, *[grads[n] for n in _WEIGHTS], *[delta[n] for n in _WEIGHTS], *[new_m[n] for n in _WEIGHTS],
            *[new_v[n] for n in _WEIGHTS])
```

```python
import functools
import math

import jax
import jax.numpy as jnp
from jax import lax
from jax.experimental import pallas as pl
from jax.experimental.pallas import tpu as pltpu

F32 = jnp.float32
BF16 = jnp.bfloat16

D_MODEL = 2048
MIX_W = 1024
N_BRANCH = 3
LRU_BLOCKS = 16
LRU_BW = 64
CONV_W = 4
LRU_C = 8.0
CHUNK = 64
ATT_HEADS = 8
ATT_HD = 128
ATT_LEFT = 8
ATT_BAND = (ATT_LEFT + 1) * CHUNK
MAX_REL = 128
N_REL = 2 * MAX_REL + 1
SSM_G = 64
SSM_H = 16
SSM_P = 64
FFN_H = 5632
IN_W = 6 * MIX_W + N_BRANCH * D_MODEL
NORM_EPS = 1e-6
MASK_VALUE = -1e30
ADAM_LR, ADAM_B1, ADAM_B2, ADAM_EPS, ADAM_WD, ADAM_STEP = 0.001, 0.9, 0.999, 1e-08, 0.01, 10

S5_L = 16
S5_LW = S5_L * SSM_H
N_CHIPS = 4
V7X_VMEM_LIMIT = 56 * 1024 * 1024
HI = lax.Precision.HIGHEST
MESH = pl.DeviceIdType.MESH


def _cparams(sem=None):
    return pltpu.CompilerParams(dimension_semantics=sem, vmem_limit_bytes=V7X_VMEM_LIMIT)


def _pick(n, prefs):
    for p in prefs:
        if n % p == 0:
            return p
    return n


_MM_VMEM_BUDGET = 46 * 1024 * 1024


def _mm_tiles(M, N, K, has_res, out_bytes):
    tn = _pick(N, (1024, 1408, 512, 256, 128))
    for tk in (K, 2048, 1408, 1024, 512, 256, 128):
        if K % tk:
            continue
        for tm in (1024, 512, 256, 128, 64, 32, 16, 8):
            if M % tm:
                continue
            need = 2 * 2 * (tm * tk + tk * tn) + 2 * tm * tn * out_bytes
            need += tm * tn * 4 if tk < K else 0
            need += 2 * tm * tn * 4 if has_res else 0
            if need <= _MM_VMEM_BUDGET:
                return tm, tn, tk
    raise ValueError((M, N, K))


def _call(body, *, name, grid, in_specs, out_specs, out_shape, args, scratch_shapes=(), sem=None, side=None):
    in_specs, out_specs, out_shape = list(in_specs), list(out_specs), list(out_shape)
    scratch_shapes = list(scratch_shapes)
    if side is None:
        outs = pl.pallas_call(body, name=name, out_shape=tuple(out_shape), grid=grid, in_specs=in_specs,
                              out_specs=tuple(out_specs), scratch_shapes=scratch_shapes, compiler_params=_cparams(sem))(*args)
        return tuple(outs), ()
    n_in, n_out, n_scr = len(in_specs), len(out_shape), len(scratch_shapes)
    s_in, s_out = len(side.inputs), len(side.out_shapes)

    def wrapped(*refs):
        mi, refs = refs[:n_in], refs[n_in:]
        si, refs = refs[:s_in], refs[s_in:]
        mo, refs = refs[:n_out], refs[n_out:]
        so, refs = refs[:s_out], refs[s_out:]
        scr, (send, recv) = refs[:n_scr], refs[n_scr:]
        first = functools.reduce(jnp.logical_and, [pl.program_id(d) == 0 for d in range(len(grid))])
        last = functools.reduce(jnp.logical_and, [pl.program_id(d) == g - 1 for d, g in enumerate(grid)])

        @pl.when(first)
        def _():
            side.start(si, so, send, recv)

        body(*mi, *mo, *scr)

        @pl.when(last)
        def _():
            side.finish(si, so, send, recv)

    outs = pl.pallas_call(
        wrapped, name=name, out_shape=tuple(out_shape + list(side.out_shapes)), grid=grid,
        in_specs=in_specs + [_ANY] * s_in, out_specs=tuple(out_specs + [_ANY] * s_out),
        scratch_shapes=scratch_shapes + [pltpu.SemaphoreType.DMA((side.n_sems,)), pltpu.SemaphoreType.DMA((side.n_sems,))],
        input_output_aliases={n_in + i: n_out + o for i, o in side.aliases.items()},
        compiler_params=_cparams(("arbitrary",) * len(grid)),
    )(*args, *side.inputs)
    return tuple(outs[:n_out]), tuple(outs[n_out:])


def _mm(a, b, *, M, N, K, name, ta=False, tb=False, a_lead=None, b_lead=None, a_off=(0, 0), b_off=(0, 0),
        out_dtype=F32, res=None, tm=None, tn=None, tk=None, side=None):
    if tm is None and tn is None and tk is None:
        tm, tn, tk = _mm_tiles(M, N, K, res is not None, jnp.dtype(out_dtype).itemsize)
    nk = K // tk

    def spec(blk, lead, off, order):
        r0, c0 = off[0] // blk[0], off[1] // blk[1]
        assert off[0] % blk[0] == 0 and off[1] % blk[1] == 0
        if lead is None:
            return pl.BlockSpec(blk, lambda i, j, k: (r0 + order(i, j, k)[0], c0 + order(i, j, k)[1]))
        return pl.BlockSpec((None,) + blk, lambda i, j, k: (lead, r0 + order(i, j, k)[0], c0 + order(i, j, k)[1]))

    a_spec = spec((tk, tm), a_lead, a_off, lambda i, j, k: (k, i)) if ta else spec((tm, tk), a_lead, a_off, lambda i, j, k: (i, k))
    b_spec = spec((tn, tk), b_lead, b_off, lambda i, j, k: (j, k)) if tb else spec((tk, tn), b_lead, b_off, lambda i, j, k: (k, j))
    dims = (((0 if ta else 1,), (1 if tb else 0,)), ((), ()))
    in_specs = [a_spec, b_spec]
    args = [a, b]
    if res is not None:
        in_specs.append(pl.BlockSpec((tm, tn), lambda i, j, k: (i, j)))
        args.append(res)

    def body(*refs):
        a_ref, b_ref = refs[:2]
        r_ref = refs[2] if res is not None else None
        o_ref = refs[3] if res is not None else refs[2]

        def dot():
            return lax.dot_general(a_ref[...], b_ref[...], dims, preferred_element_type=F32)

        def finish(r):
            if r_ref is not None:
                r = r + r_ref[...]
            o_ref[...] = r.astype(out_dtype)

        if nk == 1:
            finish(dot())
            return
        acc = refs[-1]
        k = pl.program_id(2)

        @pl.when(k == 0)
        def _():
            acc[...] = dot()

        @pl.when(jnp.logical_and(k > 0, k < nk - 1))
        def _():
            acc[...] += dot()

        @pl.when(k == nk - 1)
        def _():
            finish(acc[...] + dot())

    (out,), got = _call(
        body, name=name, out_shape=[jax.ShapeDtypeStruct((M, N), out_dtype)],
        grid=(M // tm, N // tn, nk), in_specs=in_specs, out_specs=[pl.BlockSpec((tm, tn), lambda i, j, k: (i, j))],
        scratch_shapes=[pltpu.VMEM((tm, tn), F32)] if nk > 1 else [],
        sem=("parallel", "parallel", "arbitrary"), args=args, side=side)
    return out if side is None else (out, got)


def _rowwise(fn, rows, consts, out_rows, out_accs, *, R, name, tm=256, rs=16, cs=None):
    tm = min(tm, R)
    assert R % tm == 0 and tm % rs == 0
    n_r, n_c, n_o, n_a = len(rows), len(consts), len(out_rows), len(out_accs)
    nsteps = R // tm
    widths = [w for _, _, w in rows]
    if cs is not None:
        assert all(w == widths[0] for w in widths) and widths[0] % cs == 0
        col_chunks = [(c0, cs) for c0 in range(0, widths[0], cs)]
    else:
        col_chunks = [None]

    def body(*refs):
        r_refs = refs[:n_r]
        c_refs = refs[n_r:n_r + n_c]
        o_refs = refs[n_r + n_c:n_r + n_c + n_o]
        a_refs = refs[n_r + n_c + n_o:n_r + n_c + n_o + n_a]
        s_refs = refs[n_r + n_c + n_o + n_a:]
        i = pl.program_id(0)

        @pl.when(i == 0)
        def _():
            for s in s_refs:
                s[...] = jnp.zeros_like(s)

        def piece(g, carry):
            r0 = pl.multiple_of(g * rs, rs)
            for cc in col_chunks:
                csl = slice(None) if cc is None else slice(cc[0], cc[0] + cc[1])
                rp = [r[pl.ds(r0, rs), csl].astype(F32) for r in r_refs]
                cp = [c[:, csl] for c in c_refs]
                outs, accs = fn(rp, cp)
                for o_ref, o in zip(o_refs, outs):
                    o_ref[pl.ds(r0, rs), csl] = o.astype(o_ref.dtype)
                for s_ref, av in zip(s_refs, accs):
                    s_ref[:, csl] += av
            return carry

        lax.fori_loop(0, tm // rs, piece, 0)

        @pl.when(i == nsteps - 1)
        def _():
            for a_ref, s_ref in zip(a_refs, s_refs):
                a_ref[...] = jnp.sum(s_ref[...], axis=0, keepdims=True)

    in_specs = [pl.BlockSpec((tm, w), functools.partial(lambda i, cb: (i, cb), cb=off // w)) for _, off, w in rows]
    for _, off, w in rows:
        assert off % w == 0
    in_specs += [pl.BlockSpec(c.shape, lambda i: (0, 0)) for c in consts]
    out_specs = [pl.BlockSpec((tm, w), lambda i: (i, 0)) for w, _ in out_rows]
    out_specs += [pl.BlockSpec((1, w), lambda i: (0, 0)) for w in out_accs]
    out_shape = [jax.ShapeDtypeStruct((R, w), dt) for w, dt in out_rows]
    out_shape += [jax.ShapeDtypeStruct((1, w), F32) for w in out_accs]
    return pl.pallas_call(
        body, name=name, out_shape=tuple(out_shape), grid=(nsteps,), in_specs=in_specs, out_specs=tuple(out_specs),
        scratch_shapes=[pltpu.VMEM((rs, w), F32) for w in out_accs],
        compiler_params=_cparams(("arbitrary",)),
    )(*[r for r, _, _ in rows], *consts)


def _rms(x, g):
    r = lax.rsqrt(jnp.mean(x * x, axis=-1, keepdims=True) + NORM_EPS)
    return x * r * g


def _rms_fwd(x, g, name):
    R = x.shape[0]

    def fn(rp, cp):
        return [_rms(rp[0], cp[0])], []

    return _rowwise(fn, [(x, 0, D_MODEL)], [g], [(D_MODEL, BF16)], [], R=R, name=name)[0]


def _rms_bwd(x, g, dh, dres, name):
    R = x.shape[0]

    def fn(rp, cp):
        xv, dhv, drv = rp
        _, pull = jax.vjp(_rms, xv, jnp.broadcast_to(cp[0], xv.shape))
        dx, dgv = pull(dhv)
        return [drv + dx], [dgv]

    dx, dg = _rowwise(fn, [(x, 0, D_MODEL), (dh, 0, D_MODEL), (dres, 0, D_MODEL)], [g], [(D_MODEL, F32)], [D_MODEL],
                      R=R, name=name, rs=8)
    return dx, dg


def _final_loss(x, g, tgt, name):
    R = x.shape[0]

    def loss_rows(xv, gv, tv):
        e = _rms(xv, gv) - tv
        return 0.5 * jnp.mean(e * e, axis=-1, keepdims=True)

    def fn(rp, cp):
        xv, tv = rp
        lr, pull = jax.vjp(lambda a, b: loss_rows(a, b, tv), xv, jnp.broadcast_to(cp[0], xv.shape))
        dx, dgv = pull(jnp.ones_like(lr))
        return [dx], [dgv, jnp.broadcast_to(lr, (lr.shape[0], 128))]

    dx, dg, lsum = _rowwise(fn, [(x, 0, D_MODEL), (tgt, 0, D_MODEL)], [g], [(D_MODEL, F32)], [D_MODEL, 128],
                            R=R, name=name, rs=8)
    return lsum[0, 0], dx, dg


def _neg_expm1(z):
    u = jnp.exp(z)
    safe = jnp.where(u == 1.0, 0.5, u)
    return -jnp.where(u == 1.0, z, (safe - 1.0) * z / jnp.log(safe))


def _lru_ab(xc, pr, pi, lam):
    r = jax.nn.sigmoid(pr)
    i = jax.nn.sigmoid(pi)
    log_a = -LRU_C * r * jax.nn.softplus(-lam)
    a = jnp.exp(log_a)
    b = jnp.sqrt(_neg_expm1(2.0 * log_a)) * (i * xc)
    return a, b


def _gated(h, gate):
    return h * jax.nn.gelu(gate)


def _row_iota8(w):
    return lax.broadcasted_iota(jnp.int32, (8, w), 0)


def _shift_dn(x, halo, s):
    xs = pltpu.roll(x, s, 0)
    hs = pltpu.roll(halo, s, 0)
    first = jnp.where(_row_iota8(x.shape[1]) < s, hs, xs[0:8])
    return jnp.concatenate([first, xs[8:]], axis=0) if x.shape[0] > 8 else first


def _shift_up(x, nxt, s):
    n = x.shape[0]
    xs = pltpu.roll(x, n - s, 0)
    ns = pltpu.roll(nxt, 8 - s, 0)
    last = jnp.where(_row_iota8(x.shape[1]) >= 8 - s, ns, xs[n - 8:])
    return jnp.concatenate([xs[:n - 8], last], axis=0) if n > 8 else last


def _lru_tiles(T):
    tT = min(256, T)
    return tT, T // tT


def _lru_fwd(proj, cw, cb, wax, bax, lam, name):
    T = proj.shape[0]
    W = MIX_W
    tT, nT = _lru_tiles(T)

    def body(x_ref, xh_ref, gt_ref, cw_ref, cb_ref, wax_ref, bax_ref, lam_ref, y_ref, h_ref, a_s, b_s, hc_s):
        i = pl.program_id(0)

        @pl.when(i == 0)
        def _():
            hc_s[...] = jnp.zeros_like(hc_s)

        x = x_ref[...]
        halo = jnp.where(i > 0, xh_ref[...], 0.0)
        w = cw_ref[...]
        xc = (cb_ref[...] + w[3:4] * x + w[2:3] * _shift_dn(x, halo, 1) + w[1:2] * _shift_dn(x, halo, 2)
              + w[0:1] * _shift_dn(x, halo, 3))
        pre = jnp.dot(xc.astype(BF16), wax_ref[...], preferred_element_type=F32) + bax_ref[...]
        a, b = _lru_ab(xc, pre[:, :W], pre[:, W:], lam_ref[...])
        a_s[...] = a
        b_s[...] = b
        row = _row_iota8(W)

        def grp(gi, hprev):
            r0 = pl.multiple_of(gi * 8, 8)
            A = a_s[pl.ds(r0, 8), :]
            B = b_s[pl.ds(r0, 8), :]
            for s in (1, 2, 4):
                As = pltpu.roll(A, s, 0)
                Bs = pltpu.roll(B, s, 0)
                m = row >= s
                B = jnp.where(m, A * Bs + B, B)
                A = jnp.where(m, A * As, A)
            H = A * hprev + B
            h_ref[pl.ds(r0, 8), :] = H
            return H[7:8, :]

        hc_s[0:1, :] = lax.fori_loop(0, tT // 8, grp, hc_s[0:1, :])
        y_ref[...] = _gated(h_ref[...], gt_ref[...]).astype(BF16)

    hb = tT // 8
    return pl.pallas_call(
        body, name=name,
        out_shape=(jax.ShapeDtypeStruct((T, W), BF16), jax.ShapeDtypeStruct((T, W), F32)),
        grid=(nT,),
        in_specs=[pl.BlockSpec((tT, W), lambda i: (i, 0)),
                  pl.BlockSpec((8, W), lambda i: (jnp.maximum(i * hb - 1, 0), 0)),
                  pl.BlockSpec((tT, W), lambda i: (i, 1)),
                  pl.BlockSpec((CONV_W, W), lambda i: (0, 0)), pl.BlockSpec((1, W), lambda i: (0, 0)),
                  pl.BlockSpec((W, 2 * W), lambda i: (0, 0)), pl.BlockSpec((1, 2 * W), lambda i: (0, 0)),
                  pl.BlockSpec((1, W), lambda i: (0, 0))],
        out_specs=(pl.BlockSpec((tT, W), lambda i: (i, 0)), pl.BlockSpec((tT, W), lambda i: (i, 0))),
        scratch_shapes=[pltpu.VMEM((tT, W), F32), pltpu.VMEM((tT, W), F32), pltpu.VMEM((8, W), F32)],
        compiler_params=_cparams(("arbitrary",)),
    )(proj, proj, proj, cw, cb, wax, bax, lam)


def _lru_bwd(proj, h, dy, cw, cb, wax, bax, lam, name):
    T = proj.shape[0]
    W = MIX_W
    tT, nT = _lru_tiles(T)
    hb = tT // 8

    def body(x_ref, xh_ref, gt_ref, h_ref, hh_ref, dy_ref, cw_ref, cb_ref, wax_ref, bax_ref, lam_ref,
             dx_ref, dgt_ref, dcw_ref, dcb_ref, dwax_ref, dbax_ref, dlam_ref,
             al_s, be_s, d_s, ca_s, cd_s, cx_s):
        i = pl.program_id(0)
        ib = nT - 1 - i

        @pl.when(i == 0)
        def _():
            for r in (ca_s, cd_s, cx_s, dcw_ref, dcb_ref, dwax_ref, dbax_ref, dlam_ref):
                r[...] = jnp.zeros_like(r)

        x = x_ref[...]
        halo = jnp.where(ib > 0, xh_ref[...], 0.0)
        w = cw_ref[...]
        x1, x2, x3 = _shift_dn(x, halo, 1), _shift_dn(x, halo, 2), _shift_dn(x, halo, 3)
        xc = cb_ref[...] + w[3:4] * x + w[2:3] * x1 + w[1:2] * x2 + w[0:1] * x3
        xcb = xc.astype(BF16)
        pre = jnp.dot(xcb, wax_ref[...], preferred_element_type=F32) + bax_ref[...]
        (a, _), pull_ab = jax.vjp(_lru_ab, xc, pre[:, :W], pre[:, W:], lam_ref[...])
        hv = h_ref[...]
        hprev = _shift_dn(hv, jnp.where(ib > 0, hh_ref[...], 0.0), 1)
        _, pull_y = jax.vjp(_gated, hv, gt_ref[...])
        dh_out, dgt = pull_y(dy_ref[...])
        dgt_ref[...] = dgt.astype(BF16)
        al_s[...] = _shift_up(a, ca_s[...], 1)
        be_s[...] = dh_out
        row = _row_iota8(W)
        ng = tT // 8

        def grp(k, dnext):
            r0 = pl.multiple_of((ng - 1 - k) * 8, 8)
            A = al_s[pl.ds(r0, 8), :]
            B = be_s[pl.ds(r0, 8), :]
            for s in (1, 2, 4):
                As = pltpu.roll(A, 8 - s, 0)
                Bs = pltpu.roll(B, 8 - s, 0)
                m = row < 8 - s
                B = jnp.where(m, A * Bs + B, B)
                A = jnp.where(m, A * As, A)
            Dg = A * dnext + B
            d_s[pl.ds(r0, 8), :] = Dg
            return Dg[0:1, :]

        lax.fori_loop(0, ng, grp, cd_s[0:1, :])
        Dv = d_s[...]
        dxc1, dpr, dpi, dlam = pull_ab((Dv * hprev, Dv))
        dpre = jnp.concatenate([dpr, dpi], axis=1)
        dpb = dpre.astype(BF16)
        dxc = dxc1 + lax.dot_general(dpb, wax_ref[...], (((1,), (1,)), ((), ())), preferred_element_type=F32)
        dwax_ref[...] += lax.dot_general(xcb, dpb, (((0,), (0,)), ((), ())), preferred_element_type=F32)
        dbax_ref[...] += jnp.sum(dpre, axis=0, keepdims=True)
        dlam_ref[...] += dlam
        dcb_ref[...] += jnp.sum(dxc, axis=0, keepdims=True)
        dcw_ref[...] += jnp.concatenate([jnp.sum(dxc * x3, axis=0, keepdims=True), jnp.sum(dxc * x2, axis=0, keepdims=True),
                                         jnp.sum(dxc * x1, axis=0, keepdims=True), jnp.sum(dxc * x, axis=0, keepdims=True)], axis=0)
        nxt = cx_s[...]
        dx = (w[3:4] * dxc + w[2:3] * _shift_up(dxc, nxt, 1) + w[1:2] * _shift_up(dxc, nxt, 2)
              + w[0:1] * _shift_up(dxc, nxt, 3))
        dx_ref[...] = dx.astype(BF16)
        ca_s[...] = a[0:8]
        cd_s[...] = Dv[0:8]
        cx_s[...] = dxc[0:8]

    rev = lambda i: nT - 1 - i
    const = lambda shape: pl.BlockSpec(shape, lambda i: (0, 0))
    return pl.pallas_call(
        body, name=name,
        out_shape=(jax.ShapeDtypeStruct((T, W), BF16), jax.ShapeDtypeStruct((T, W), BF16),
                   jax.ShapeDtypeStruct((CONV_W, W), F32), jax.ShapeDtypeStruct((1, W), F32),
                   jax.ShapeDtypeStruct((W, 2 * W), F32), jax.ShapeDtypeStruct((1, 2 * W), F32),
                   jax.ShapeDtypeStruct((1, W), F32)),
        grid=(nT,),
        in_specs=[pl.BlockSpec((tT, W), lambda i: (rev(i), 0)),
                  pl.BlockSpec((8, W), lambda i: (jnp.maximum(rev(i) * hb - 1, 0), 0)),
                  pl.BlockSpec((tT, W), lambda i: (rev(i), 1)),
                  pl.BlockSpec((tT, W), lambda i: (rev(i), 0)),
                  pl.BlockSpec((8, W), lambda i: (jnp.maximum(rev(i) * hb - 1, 0), 0)),
                  pl.BlockSpec((tT, W), lambda i: (rev(i), 0)),
                  const((CONV_W, W)), const((1, W)), const((W, 2 * W)), const((1, 2 * W)), const((1, W))],
        out_specs=(pl.BlockSpec((tT, W), lambda i: (rev(i), 0)), pl.BlockSpec((tT, W), lambda i: (rev(i), 0)),
                   const((CONV_W, W)), const((1, W)), const((W, 2 * W)), const((1, 2 * W)), const((1, W))),
        scratch_shapes=[pltpu.VMEM((tT, W), F32), pltpu.VMEM((tT, W), F32), pltpu.VMEM((tT, W), F32),
                        pltpu.VMEM((8, W), F32), pltpu.VMEM((8, W), F32), pltpu.VMEM((8, W), F32)],
        compiler_params=_cparams(("arbitrary",)),
    )(proj, proj, proj, h, h, dy, cw, cb, wax, bax, lam)


def _lru_prep(wa, wx, ba, bx):
    eye = jnp.eye(LRU_BLOCKS, dtype=F32)

    def dense(wb):
        return (wb[:, :, None, :] * eye[:, None, :, None]).reshape(MIX_W, MIX_W)

    wax = jnp.concatenate([dense(wa), dense(wx)], axis=1)
    bax = jnp.concatenate([ba, bx])[None, :]
    return wax, bax


_ATT_QC = 4
_ATT_Q = _ATT_QC * CHUNK
_ATT_KW = (ATT_LEFT + _ATT_QC) * CHUNK
_BVEC_W = _ATT_KW
_N_OFFS = CHUNK - 1 + ATT_BAND


def _attn_bias(rel_bias):
    n_far = ATT_LEFT * CHUNK - MAX_REL + CHUNK
    far = jnp.broadcast_to(rel_bias[:, 2 * MAX_REL:], (ATT_HEADS, n_far))
    near = rel_bias[:, MAX_REL - (CHUNK - 1):2 * MAX_REL][:, ::-1]
    pad = jnp.zeros((ATT_HEADS, _BVEC_W - _N_OFFS), F32)
    return jnp.concatenate([far, near, pad], axis=1)[:, None, :]


def _bias_table(bvec_row):
    return pltpu.roll(jnp.broadcast_to(bvec_row, (_ATT_Q, _BVEC_W)), _BVEC_W - (CHUNK - 1), 1, stride=1, stride_axis=0)


def _bias_table_t(ds):
    r = lax.broadcasted_iota(jnp.int32, (_ATT_Q, _ATT_Q), 0)
    c = lax.broadcasted_iota(jnp.int32, (_ATT_Q, _ATT_Q), 1)
    rev = jnp.dot((r + c == _ATT_Q - 1).astype(F32), ds, preferred_element_type=F32, precision=HI)
    back = pltpu.roll(rev, _BVEC_W - (_ATT_Q - CHUNK), 1, stride=1, stride_axis=0)
    return jnp.sum(back, axis=0, keepdims=True)


_NT = (((1,), (1,)), ((), ()))
_TN = (((0,), (0,)), ((), ()))
_ATT_PAD = ATT_LEFT * CHUNK
_Q_BLK, _K_BLK, _V_BLK = 2 * MIX_W // ATT_HD, 3 * MIX_W // ATT_HD, 4 * MIX_W // ATT_HD


def _in_band():
    first = (lax.broadcasted_iota(jnp.int32, (_ATT_Q, _ATT_KW), 0) // CHUNK) * CHUNK
    k = lax.broadcasted_iota(jnp.int32, (_ATT_Q, _ATT_KW), 1)
    return jnp.logical_and(k >= first, k < first + ATT_BAND)


def _attn_probs(q, kb, bias, in_band, b):
    s = lax.dot_general(q, kb, _NT, preferred_element_type=F32) * (ATT_HD ** -0.5) + bias
    kpos = lax.broadcasted_iota(jnp.int32, s.shape, 1)
    s = jnp.where(jnp.logical_and(in_band, kpos >= _ATT_PAD - b * _ATT_Q), s, MASK_VALUE)
    e = jnp.exp(s - jnp.max(s, axis=-1, keepdims=True))
    return e / jnp.sum(e, axis=-1, keepdims=True)


def _attn_fwd(proj, bias, name, side=None):
    T = proj.shape[0]
    assert T % _ATT_Q == 0
    nB = T // _ATT_Q

    def body(q_ref, k_ref, v_ref, b_ref, o_ref, kp, vp):
        kp[0:_ATT_PAD, :] = jnp.zeros((_ATT_PAD, ATT_HD), BF16)
        vp[0:_ATT_PAD, :] = jnp.zeros((_ATT_PAD, ATT_HD), BF16)
        kp[_ATT_PAD:, :] = k_ref[...].astype(BF16)
        vp[_ATT_PAD:, :] = v_ref[...].astype(BF16)
        bias_v = _bias_table(b_ref[0])
        band = _in_band()

        def step(b, carry):
            r0 = pl.multiple_of(b * _ATT_Q, _ATT_Q)
            q = q_ref[pl.ds(r0, _ATT_Q), :].astype(BF16)
            p = _attn_probs(q, kp[pl.ds(r0, _ATT_KW), :], bias_v, band, b)
            o = jnp.dot(p.astype(BF16), vp[pl.ds(r0, _ATT_KW), :], preferred_element_type=F32)
            o_ref[pl.ds(r0, _ATT_Q), :] = o.astype(BF16)
            return carry

        lax.fori_loop(0, nB, step, 0)

    (out,), got = _call(
        body, name=name, out_shape=[jax.ShapeDtypeStruct((T, MIX_W), BF16)], grid=(ATT_HEADS,),
        in_specs=[pl.BlockSpec((T, ATT_HD), lambda h: (0, _Q_BLK + h)), pl.BlockSpec((T, ATT_HD), lambda h: (0, _K_BLK + h)),
                  pl.BlockSpec((T, ATT_HD), lambda h: (0, _V_BLK + h)), pl.BlockSpec((1, 1, _BVEC_W), lambda h: (h, 0, 0))],
        out_specs=[pl.BlockSpec((T, ATT_HD), lambda h: (0, h))],
        scratch_shapes=[pltpu.VMEM((T + _ATT_PAD, ATT_HD), BF16), pltpu.VMEM((T + _ATT_PAD, ATT_HD), BF16)],
        sem=("arbitrary",), args=(proj, proj, proj, bias), side=side)
    return out if side is None else (out, got)


def _attn_bwd(proj, bias, do, name, side=None):
    T = proj.shape[0]
    assert T % _ATT_Q == 0
    nB = T // _ATT_Q

    def body(q_ref, k_ref, v_ref, b_ref, do_ref, dq_ref, dk_ref, dv_ref, db_ref, kp, vp, dkp, dvp, dbs):
        kp[0:_ATT_PAD, :] = jnp.zeros((_ATT_PAD, ATT_HD), BF16)
        vp[0:_ATT_PAD, :] = jnp.zeros((_ATT_PAD, ATT_HD), BF16)
        kp[_ATT_PAD:, :] = k_ref[...].astype(BF16)
        vp[_ATT_PAD:, :] = v_ref[...].astype(BF16)
        dkp[...] = jnp.zeros_like(dkp)
        dvp[...] = jnp.zeros_like(dvp)
        dbs[...] = jnp.zeros_like(dbs)
        bias_v = _bias_table(b_ref[0])
        band = _in_band()

        def step(b, carry):
            r0 = pl.multiple_of(b * _ATT_Q, _ATT_Q)
            q = q_ref[pl.ds(r0, _ATT_Q), :].astype(BF16)
            kb = kp[pl.ds(r0, _ATT_KW), :]
            vb = vp[pl.ds(r0, _ATT_KW), :]
            dob = do_ref[pl.ds(r0, _ATT_Q), :].astype(BF16)
            p = _attn_probs(q, kb, bias_v, band, b)
            dp = lax.dot_general(dob, vb, _NT, preferred_element_type=F32)
            ds = p * (dp - jnp.sum(p * dp, axis=-1, keepdims=True))
            dbs[...] += ds
            dsb = (ds * (ATT_HD ** -0.5)).astype(BF16)
            dq_ref[pl.ds(r0, _ATT_Q), :] = jnp.dot(dsb, kb, preferred_element_type=F32).astype(BF16)
            dkp[pl.ds(r0, _ATT_KW), :] += lax.dot_general(dsb, q, _TN, preferred_element_type=F32)
            dvp[pl.ds(r0, _ATT_KW), :] += lax.dot_general(p.astype(BF16), dob, _TN, preferred_element_type=F32)
            return carry

        lax.fori_loop(0, nB, step, 0)
        dk_ref[...] = dkp[_ATT_PAD:, :].astype(BF16)
        dv_ref[...] = dvp[_ATT_PAD:, :].astype(BF16)
        db_ref[0] = _bias_table_t(dbs[...])

    hspec = pl.BlockSpec((T, ATT_HD), lambda h: (0, h))
    osd = jax.ShapeDtypeStruct((T, MIX_W), BF16)
    outs, got = _call(
        body, name=name,
        out_shape=[osd, osd, osd, jax.ShapeDtypeStruct((ATT_HEADS, 1, _BVEC_W), F32)], grid=(ATT_HEADS,),
        in_specs=[pl.BlockSpec((T, ATT_HD), lambda h: (0, _Q_BLK + h)), pl.BlockSpec((T, ATT_HD), lambda h: (0, _K_BLK + h)),
                  pl.BlockSpec((T, ATT_HD), lambda h: (0, _V_BLK + h)), pl.BlockSpec((1, 1, _BVEC_W), lambda h: (h, 0, 0)),
                  hspec],
        out_specs=[hspec, hspec, hspec, pl.BlockSpec((1, 1, _BVEC_W), lambda h: (h, 0, 0))],
        scratch_shapes=[pltpu.VMEM((T + _ATT_PAD, ATT_HD), BF16), pltpu.VMEM((T + _ATT_PAD, ATT_HD), BF16),
                        pltpu.VMEM((T + _ATT_PAD, ATT_HD), F32), pltpu.VMEM((T + _ATT_PAD, ATT_HD), F32),
                        pltpu.VMEM((_ATT_Q, _ATT_KW), F32)],
        sem=("arbitrary",), args=(proj, proj, proj, bias, do), side=side)
    return outs if side is None else (outs, got)


def _s5_prep(a_re, a_im, b_re, b_im, c_re, c_im, d, log_step):
    step = jnp.exp(log_step)[:, None]
    mag = jnp.exp(a_re * step)
    ang = a_im * step
    lb_re = mag * jnp.cos(ang)
    lb_im = mag * jnp.sin(ang)
    den = a_re * a_re + a_im * a_im
    nr = lb_re - 1.0
    coef_re = (nr * a_re + lb_im * a_im) / den
    coef_im = (lb_im * a_re - nr * a_im) / den
    bb_re = coef_re[..., None] * b_re - coef_im[..., None] * b_im
    bb_im = coef_re[..., None] * b_im + coef_im[..., None] * b_re
    ks = jnp.arange(S5_L + 1, dtype=F32)[:, None, None]
    pmag = jnp.exp(ks * (a_re * step)[None])
    PR, PI = pmag * jnp.cos(ks * ang[None]), pmag * jnp.sin(ks * ang[None])
    cl_re = c_re[None] * PR[:, :, None, :] - c_im[None] * PI[:, :, None, :]
    cl_im = c_re[None] * PI[:, :, None, :] + c_im[None] * PR[:, :, None, :]
    cla_re = cl_re[:S5_L].transpose(1, 0, 2, 3).reshape(SSM_G, S5_LW, SSM_P)
    cla_im = cl_im[:S5_L].transpose(1, 0, 2, 3).reshape(SSM_G, S5_LW, SSM_P)
    bbt_re, bbt_im = bb_re.transpose(0, 2, 1), bb_im.transpose(0, 2, 1)
    prr, pir = PR[:S5_L][::-1], PI[:S5_L][::-1]
    bret = (prr[:, :, None, :] * bb_re.transpose(0, 2, 1)[None] - pir[:, :, None, :] * bb_im.transpose(0, 2, 1)[None])
    bimt = (prr[:, :, None, :] * bb_im.transpose(0, 2, 1)[None] + pir[:, :, None, :] * bb_re.transpose(0, 2, 1)[None])
    bret = bret.transpose(1, 0, 2, 3).reshape(SSM_G, S5_LW, SSM_P)
    bimt = bimt.transpose(1, 0, 2, 3).reshape(SSM_G, S5_LW, SSM_P)
    cre = cl_re[1:].transpose(1, 3, 0, 2).reshape(SSM_G, SSM_P, S5_LW)
    cim = (-cl_im[1:]).transpose(1, 3, 0, 2).reshape(SSM_G, SSM_P, S5_LW)
    dflat = jnp.broadcast_to(d.reshape(SSM_G, 1, SSM_H), (SSM_G, S5_L, SSM_H)).reshape(SSM_G, 1, S5_LW)
    bt = jnp.concatenate([bret, bimt], axis=2)
    ct = jnp.concatenate([cre, cim], axis=1)
    return cla_re, cla_im, bbt_re, bbt_im, bt, ct, PR[S5_L], PI[S5_L], dflat


_S5_GB = 8


def _bdot(a, b, dims):
    return lax.dot_general(a, b, dims, preferred_element_type=F32, precision=HI)


_B_NN = (((2,), (1,)), ((0,), (0,)))
_B_NT = (((2,), (2,)), ((0,), (0,)))
_B_TN = (((1,), (1,)), ((0,), (0,)))


def _gspec(shape):
    return pl.BlockSpec((_S5_GB,) + shape, lambda g: (g, 0, 0))


def _s5_tgt(cla_re, cla_im, bbt_re, bbt_im, name):
    n_g = cla_re.shape[0]

    def body(cr_ref, ci_ref, br_ref, bi_ref, t_ref):
        kt = _bdot(br_ref[...], cr_ref[...], _B_NT) - _bdot(bi_ref[...], ci_ref[...], _B_NT)
        lane = lax.broadcasted_iota(jnp.int32, (SSM_H, S5_LW), 1)
        for g in range(_S5_GB):
            for lp in range(S5_L):
                rows = kt[g] if lp == 0 else jnp.where(lane >= lp * SSM_H, pltpu.roll(kt[g], lp * SSM_H, 1), 0.0)
                t_ref[g, lp * SSM_H:(lp + 1) * SSM_H, :] = rows

    return pl.pallas_call(
        body, name=name, out_shape=jax.ShapeDtypeStruct((n_g, S5_LW, S5_LW), F32), grid=(n_g // _S5_GB,),
        in_specs=[_gspec((S5_LW, SSM_P)), _gspec((S5_LW, SSM_P)), _gspec((SSM_H, SSM_P)), _gspec((SSM_H, SSM_P))],
        out_specs=_gspec((S5_LW, S5_LW)), compiler_params=_cparams(("parallel",)),
    )(cla_re, cla_im, bbt_re, bbt_im)


def _s5_tgt_t(d_tgt, cla_re, cla_im, bbt_re, bbt_im, name):
    n_g = cla_re.shape[0]

    def body(dt_ref, cr_ref, ci_ref, br_ref, bi_ref, dcr_ref, dci_ref, dbr_ref, dbi_ref):
        lane = lax.broadcasted_iota(jnp.int32, (SSM_H, S5_LW), 1)
        dks = []
        for g in range(_S5_GB):
            dk = dt_ref[g, 0:SSM_H, :]
            for lp in range(1, S5_L):
                rows = dt_ref[g, lp * SSM_H:(lp + 1) * SSM_H, :]
                dk = dk + jnp.where(lane < S5_LW - lp * SSM_H, pltpu.roll(rows, S5_LW - lp * SSM_H, 1), 0.0)
            dks.append(dk)
        dkt = jnp.stack(dks)
        dbr_ref[...] = _bdot(dkt, cr_ref[...], _B_NN)
        dbi_ref[...] = -_bdot(dkt, ci_ref[...], _B_NN)
        dcr_ref[...] = _bdot(dkt, br_ref[...], _B_TN)
        dci_ref[...] = -_bdot(dkt, bi_ref[...], _B_TN)

    sd = jax.ShapeDtypeStruct
    return pl.pallas_call(
        body, name=name,
        out_shape=(sd((n_g, S5_LW, SSM_P), F32), sd((n_g, S5_LW, SSM_P), F32), sd((n_g, SSM_H, SSM_P), F32),
                   sd((n_g, SSM_H, SSM_P), F32)),
        grid=(n_g // _S5_GB,),
        in_specs=[_gspec((S5_LW, S5_LW)), _gspec((S5_LW, SSM_P)), _gspec((S5_LW, SSM_P)), _gspec((SSM_H, SSM_P)),
                  _gspec((SSM_H, SSM_P))],
        out_specs=(_gspec((S5_LW, SSM_P)), _gspec((S5_LW, SSM_P)), _gspec((SSM_H, SSM_P)), _gspec((SSM_H, SSM_P))),
        compiler_params=_cparams(("parallel",)),
    )(d_tgt, cla_re, cla_im, bbt_re, bbt_im)


_S5_SW = 2 * SSM_P


def _sspec(C):
    return pl.BlockSpec((C, _S5_GB, _S5_SW), lambda g: (0, g, 0))


def _group_major(s_ref):
    return jnp.stack([s_ref[:, k, :] for k in range(_S5_GB)])


def _chunk_major(o_ref, v):
    for k in range(_S5_GB):
        o_ref[:, k, :] = v[k]


def _rot_coefs(lr, li, sign):
    return jnp.concatenate([lr, lr], axis=1), jnp.concatenate([-sign * li, sign * li], axis=1)


def _s5_in(u, bt, name):
    C = u.shape[1]

    def body(u_ref, b_ref, s_ref):
        _chunk_major(s_ref, _bdot(u_ref[...], b_ref[...], _B_NN))

    return pl.pallas_call(
        body, name=name, out_shape=jax.ShapeDtypeStruct((C, SSM_G, _S5_SW), F32), grid=(SSM_G // _S5_GB,),
        in_specs=[_gspec((C, S5_LW)), _gspec((S5_LW, _S5_SW))], out_specs=_sspec(C),
        compiler_params=_cparams(("parallel",)),
    )(u, bt)


def _s5_scan(sin, lr, li, name):
    C = sin.shape[0]

    def body(i_ref, lr_ref, li_ref, o_ref):
        a, b = _rot_coefs(lr_ref[...], li_ref[...], 1.0)

        def step(c, s):
            o_ref[c] = s
            return a * s + b * pltpu.roll(s, SSM_P, 1) + i_ref[c]

        lax.fori_loop(0, C, step, jnp.zeros((SSM_G, _S5_SW), F32))

    return pl.pallas_call(body, name=name, out_shape=jax.ShapeDtypeStruct((C, SSM_G, _S5_SW), F32),
                          compiler_params=_cparams())(sin, lr, li)


def _s5_out(u, sp, tgt, ct, dflat, name):
    C = u.shape[1]

    def body(u_ref, s_ref, t_ref, c_ref, d_ref, pre_ref, y_ref):
        uv = u_ref[...]
        pre = _bdot(uv, t_ref[...], _B_NN) + _bdot(_group_major(s_ref), c_ref[...], _B_NN) + d_ref[...] * uv
        pre_ref[...] = pre
        y_ref[...] = jax.nn.gelu(pre)

    return pl.pallas_call(
        body, name=name,
        out_shape=(jax.ShapeDtypeStruct((SSM_G, C, S5_LW), F32), jax.ShapeDtypeStruct((SSM_G, C, S5_LW), F32)),
        grid=(SSM_G // _S5_GB,),
        in_specs=[_gspec((C, S5_LW)), _sspec(C), _gspec((S5_LW, S5_LW)), _gspec((_S5_SW, S5_LW)), _gspec((1, S5_LW))],
        out_specs=(_gspec((C, S5_LW)), _gspec((C, S5_LW))), compiler_params=_cparams(("parallel",)),
    )(u, sp, tgt, ct, dflat)


def _s5_bwd_out(dy, pre, u, sp, ct, name):
    C = u.shape[1]

    def body(dy_ref, pre_ref, u_ref, s_ref, c_ref, dpre_ref, ds_ref, dt_ref, dc_ref, dd_ref):
        _, pull = jax.vjp(jax.nn.gelu, pre_ref[...])
        dpre = pull(dy_ref[...])[0]
        uv = u_ref[...]
        dpre_ref[...] = dpre
        _chunk_major(ds_ref, _bdot(dpre, c_ref[...], _B_NT))
        dt_ref[...] = _bdot(uv, dpre, _B_TN)
        dc_ref[...] = _bdot(_group_major(s_ref), dpre, _B_TN)
        dd_ref[...] = jnp.sum(dpre * uv, axis=1, keepdims=True)

    sd = jax.ShapeDtypeStruct
    return pl.pallas_call(
        body, name=name,
        out_shape=(sd((SSM_G, C, S5_LW), F32), sd((C, SSM_G, _S5_SW), F32), sd((SSM_G, S5_LW, S5_LW), F32),
                   sd((SSM_G, _S5_SW, S5_LW), F32), sd((SSM_G, 1, S5_LW), F32)),
        grid=(SSM_G // _S5_GB,),
        in_specs=[_gspec((C, S5_LW)), _gspec((C, S5_LW)), _gspec((C, S5_LW)), _sspec(C), _gspec((_S5_SW, S5_LW))],
        out_specs=(_gspec((C, S5_LW)), _sspec(C), _gspec((S5_LW, S5_LW)), _gspec((_S5_SW, S5_LW)), _gspec((1, S5_LW))),
        compiler_params=_cparams(("parallel",)),
    )(dy, pre, u, sp, ct)


def _s5_rscan(dsp, sp, lr, li, name):
    C = dsp.shape[0]

    def body(g_ref, s_ref, lr_ref, li_ref, o_ref, dlr_ref, dli_ref):
        a, b = _rot_coefs(lr_ref[...], li_ref[...], -1.0)

        def step(k, carry):
            c = C - 1 - k
            d, acc_same, acc_swap = carry
            o_ref[c] = d
            s = s_ref[c]
            return (g_ref[c] + a * d + b * pltpu.roll(d, SSM_P, 1), acc_same + d * s,
                    acc_swap + d * pltpu.roll(s, SSM_P, 1))

        z = jnp.zeros((SSM_G, _S5_SW), F32)
        _, same, swap = lax.fori_loop(0, C, step, (z, z, z))
        dlr_ref[...] = same[:, :SSM_P] + same[:, SSM_P:]
        dli_ref[...] = swap[:, SSM_P:] - swap[:, :SSM_P]

    sd = jax.ShapeDtypeStruct((C, SSM_G, _S5_SW), F32)
    sp_ = jax.ShapeDtypeStruct((SSM_G, SSM_P), F32)
    return pl.pallas_call(body, name=name, out_shape=(sd, sp_, sp_), compiler_params=_cparams())(dsp, sp, lr, li)


def _s5_bwd_in(dpre, dsin, u, tgt, bt, dflat, name):
    C = u.shape[1]

    def body(dp_ref, ds_ref, u_ref, t_ref, b_ref, d_ref, du_ref, db_ref):
        dp = dp_ref[...]
        ds = _group_major(ds_ref)
        du_ref[...] = _bdot(dp, t_ref[...], _B_NT) + _bdot(ds, b_ref[...], _B_NT) + d_ref[...] * dp
        db_ref[...] = _bdot(u_ref[...], ds, _B_TN)

    sd = jax.ShapeDtypeStruct
    return pl.pallas_call(
        body, name=name,
        out_shape=(sd((SSM_G, C, S5_LW), F32), sd((SSM_G, S5_LW, _S5_SW), F32)),
        grid=(SSM_G // _S5_GB,),
        in_specs=[_gspec((C, S5_LW)), _sspec(C), _gspec((C, S5_LW)), _gspec((S5_LW, S5_LW)), _gspec((S5_LW, _S5_SW)),
                  _gspec((1, S5_LW))],
        out_specs=(_gspec((C, S5_LW)), _gspec((S5_LW, _S5_SW))),
        compiler_params=_cparams(("parallel",)),
    )(dpre, dsin, u, tgt, bt, dflat)


_LANES = 128
_GROUPS_PER_TILE = _LANES // SSM_H


def _to_chunks(v, col0, name):
    T = v.shape[0]
    C = T // S5_L

    def body(x_ref, o_ref):
        for l in range(S5_L):
            xl = x_ref[pl.ds(l, C, stride=S5_L), :]
            for k in range(_GROUPS_PER_TILE):
                o_ref[k, :, l * SSM_H:(l + 1) * SSM_H] = xl[:, k * SSM_H:(k + 1) * SSM_H]

    return pl.pallas_call(
        body, name=name, out_shape=jax.ShapeDtypeStruct((SSM_G, C, S5_LW), F32), grid=(SSM_G // _GROUPS_PER_TILE,),
        in_specs=[pl.BlockSpec((T, _LANES), lambda g: (0, col0 // _LANES + g))],
        out_specs=pl.BlockSpec((_GROUPS_PER_TILE, C, S5_LW), lambda g: (g, 0, 0)),
        compiler_params=_cparams(("parallel",)),
    )(v)


def _from_chunks(v, out_dtype, name):
    C = v.shape[1]
    T = C * S5_L

    def body(f_ref, o_ref, rows):
        for l in range(S5_L):
            rows[pl.ds(l, C, stride=S5_L), :] = jnp.concatenate(
                [f_ref[k, :, l * SSM_H:(l + 1) * SSM_H] for k in range(_GROUPS_PER_TILE)], axis=1)
        o_ref[...] = rows[...].astype(out_dtype)

    return pl.pallas_call(
        body, name=name, out_shape=jax.ShapeDtypeStruct((T, MIX_W), out_dtype), grid=(SSM_G // _GROUPS_PER_TILE,),
        in_specs=[pl.BlockSpec((_GROUPS_PER_TILE, C, S5_LW), lambda g: (g, 0, 0))],
        out_specs=pl.BlockSpec((T, _LANES), lambda g: (0, g)),
        scratch_shapes=[pltpu.VMEM((T, _LANES), F32)],
        compiler_params=_cparams(("parallel",)),
    )(v)


def _merge_fn(bra, brb, pc, pg, g0, g1, g2, b0, b1, b2):
    sg = jax.nn.sigmoid
    return sg(g0 + b0) * bra + sg(g1 + b1) * brb + sg(g2 + b2) * (pc * sg(pg))


_EW_CS = 256
_GATE_OFF = 6 * MIX_W


def _merge_rows(br4, proj):
    return [(b, 0, D_MODEL) for b in br4] + [(proj, _GATE_OFF + k * D_MODEL, D_MODEL) for k in range(3)]


def _merge_fwd(br4, proj, gb3, name):
    def fn(rp, cp):
        b = cp[0]
        return [_merge_fn(*rp, b[0:1], b[1:2], b[2:3])], []

    return _rowwise(fn, _merge_rows(br4, proj), [gb3], [(D_MODEL, BF16)], [], R=proj.shape[0], name=name, cs=_EW_CS)[0]


def _merge_bwd(br4, proj, gb3, dm, name):
    def fn(rp, cp):
        b = cp[0]
        shp = rp[0].shape
        bs = [jnp.broadcast_to(b[k:k + 1], shp) for k in range(3)]
        _, pull = jax.vjp(_merge_fn, *rp[:7], *bs)
        g = pull(rp[7])
        return list(g[:7]), list(g[7:])

    rows = _merge_rows(br4, proj) + [(dm, 0, D_MODEL)]
    outs = _rowwise(fn, rows, [gb3], [(D_MODEL, BF16)] * 7, [D_MODEL] * 3, R=proj.shape[0], name=name, tm=128, cs=_EW_CS)
    return outs[:4], outs[4:7], jnp.concatenate(outs[7:], axis=0)


def _swiglu(g, u):
    return jax.nn.silu(g) * u


def _act_fwd(gu, name):
    def fn(rp, cp):
        return [_swiglu(*rp)], []

    return _rowwise(fn, [(gu[0], 0, FFN_H), (gu[1], 0, FFN_H)], [], [(FFN_H, BF16)], [], R=gu[0].shape[0], name=name, cs=_EW_CS)[0]


def _act_bwd(gu, dact, name):
    def fn(rp, cp):
        _, pull = jax.vjp(_swiglu, rp[0], rp[1])
        return list(pull(rp[2])), []

    return _rowwise(fn, [(gu[0], 0, FFN_H), (gu[1], 0, FFN_H), (dact, 0, FFN_H)], [], [(FFN_H, BF16)] * 2, [],
                    R=dact.shape[0], name=name, tm=128, cs=_EW_CS)


def _adamw_fn(w, g, m, v):
    m = ADAM_B1 * m + (1.0 - ADAM_B1) * g
    v = ADAM_B2 * v + (1.0 - ADAM_B2) * jnp.square(g)
    m_hat = m / (1.0 - ADAM_B1 ** ADAM_STEP)
    v_hat = v / (1.0 - ADAM_B2 ** ADAM_STEP)
    delta = -ADAM_LR * (m_hat / (jnp.sqrt(v_hat) + ADAM_EPS) + ADAM_WD * w)
    return delta, m, v


def _adamw(w, g, m, v, name):
    R, C = w.shape

    def fn(rp, cp):
        return list(_adamw_fn(*rp)), []

    cs = _pick(C, (512, 256, 128))
    tm = _pick(R, (256, 128, 64, 32, 16, 8))
    return _rowwise(fn, [(w, 0, C), (g, 0, C), (m, 0, C), (v, 0, C)], [], [(C, F32)] * 3, [], R=R, name=name,
                    tm=tm, rs=8, cs=cs)


def _my_place():
    return lax.axis_index("x"), lax.axis_index("y"), lax.axis_index("c")


def _other_chips(x, y):
    return [(1 - x, y), (x, 1 - y), (1 - x, 1 - y)]


_ANY = pl.BlockSpec(memory_space=pl.ANY)


def _rcopy(src, dst, ssem, rsem, to):
    return pltpu.make_async_remote_copy(src_ref=src, dst_ref=dst, send_sem=ssem, recv_sem=rsem, device_id=to,
                                        device_id_type=MESH)


def _place_shard(local, axis, jidx, out_dtype, name):
    lead, r, c = local.shape
    shp = [lead, r, c]
    shp[axis] *= N_CHIPS
    tr = _pick(r, (512, 256, 128)) if r % 128 == 0 else r
    nr = r // tr
    omap = (lambda l, i, j: (l, i, j[0])) if axis == 2 else (lambda l, i, j: (l, j[0] * nr + i, 0))

    def body(j_ref, x_ref, o_ref):
        o_ref[...] = x_ref[...].astype(out_dtype)

    return pl.pallas_call(
        body, name=name, out_shape=jax.ShapeDtypeStruct(tuple(shp), out_dtype),
        grid_spec=pltpu.PrefetchScalarGridSpec(
            num_scalar_prefetch=1, grid=(lead, nr),
            in_specs=[pl.BlockSpec((None, tr, c), lambda l, i, j: (l, i, 0))],
            out_specs=pl.BlockSpec((None, tr, c), omap)),
        compiler_params=_cparams(("parallel", "parallel")),
    )(jidx, local)


class _GatherSide:
    def __init__(self, fulls, axes, regions):
        self.inputs = list(fulls)
        self.out_shapes = [jax.ShapeDtypeStruct(f.shape, f.dtype) for f in fulls]
        self.aliases = {t: t for t in range(len(fulls))}
        self.n_sems = 6 * len(fulls)
        self.axes, self.regions = list(axes), list(regions)

    def _block(self, outs, t, chip, half):
        start, size, split = self.regions[t][:3]
        piece, n_pieces = self.regions[t][3] if len(self.regions[t]) > 3 else (0, 1)
        ax = self.axes[t]
        cut = outs[t].shape[ax] // N_CHIPS
        j = 2 * chip[0] + chip[1]
        idx = [pl.ds(start, size), slice(None), slice(None)]
        idx[ax] = pl.ds(j * cut, cut)
        if split == "lead":
            idx[0] = pl.ds(start + half * (size // 2), size // 2)
        else:
            other = 3 - ax
            h = outs[t].shape[other] // (2 * n_pieces)
            idx[other] = pl.ds((half * n_pieces + piece) * h, h)
        return outs[t].at[tuple(idx)]

    def _sends(self, outs, send, recv):
        x, y, c = _my_place()
        cps = []
        for t in range(len(outs)):
            mine = self._block(outs, t, (x, y), c)
            for r, chip in enumerate(_other_chips(x, y)):
                k = 3 * t + r
                cps.append(_rcopy(mine, mine, send.at[k], recv.at[k], (*chip, c)))
        return cps

    def start(self, ins, outs, send, recv):
        for cp in self._sends(outs, send, recv):
            cp.start()

    def finish(self, ins, outs, send, recv):
        x, y, c = _my_place()
        sib = (x, y, 1 - c)
        n = len(outs)
        chips = _other_chips(x, y)
        passed = []
        for t in range(n):
            for r, chip in enumerate(chips):
                k = 3 * t + r
                landed = self._block(outs, t, chip, c)
                _rcopy(landed, landed, send.at[k], recv.at[k], (*chip, c)).wait_recv()
                cp = _rcopy(landed, landed, send.at[3 * n + k], recv.at[3 * n + k], sib)
                cp.start()
                passed.append(cp)
        for t in range(n):
            for r, chip in enumerate(chips):
                k = 3 * n + 3 * t + r
                theirs = self._block(outs, t, chip, 1 - c)
                _rcopy(theirs, theirs, send.at[k], recv.at[k], sib).wait_recv()
        for cp in self._sends(outs, send, recv) + passed:
            cp.wait_send()


def _run_side(side, name):
    s_in = len(side.inputs)

    def body(*refs):
        ins, outs = refs[:s_in], refs[s_in:s_in + len(side.out_shapes)]
        send, recv = refs[s_in + len(side.out_shapes):]
        side.start(ins, outs, send, recv)
        side.finish(ins, outs, send, recv)

    return pl.pallas_call(
        body, name=name, out_shape=tuple(side.out_shapes), in_specs=[_ANY] * s_in,
        out_specs=tuple([_ANY] * len(side.out_shapes)), input_output_aliases=dict(side.aliases),
        scratch_shapes=[pltpu.SemaphoreType.DMA((side.n_sems,)), pltpu.SemaphoreType.DMA((side.n_sems,))],
    )(*side.inputs)


def _half_idx(shape, axis, half):
    size = shape[axis] // 2
    idx = [slice(None), slice(None)]
    idx[axis] = pl.ds(half * size, size)
    return tuple(idx)


class _PairSide:
    def __init__(self, grads, half_axes):
        self.inputs = list(grads)
        self.half_axes = list(half_axes)
        self.out_shapes = []
        for g, ax in zip(grads, half_axes):
            shp = list(g.shape)
            shp[ax] //= 2
            self.out_shapes.append(jax.ShapeDtypeStruct(tuple(shp), g.dtype))
        self.aliases = {}
        self.n_sems = len(grads)

    def _copies(self, srcs, outs, send, recv):
        x, y, c = _my_place()
        return [_rcopy(srcs[t].at[_half_idx(srcs[t].shape, self.half_axes[t], 1 - c)], outs[t], send.at[t], recv.at[t],
                       (x, y, 1 - c)) for t in range(len(srcs))]

    def start(self, srcs, outs, send, recv):
        for cp in self._copies(srcs, outs, send, recv):
            cp.start()

    def finish(self, srcs, outs, send, recv):
        for cp in self._copies(srcs, outs, send, recv):
            cp.wait()


def _pair_sum(g, recv, half_axis, cidx, name):
    K, N = recv.shape
    tm = _pick(K, (256, 128, 64, 32, 16))
    tn = _pick(N, (1024, 1408, 512, 256, 128))
    nbr, nbc = K // tm, N // tn
    if half_axis == 0:
        gmap = lambda i, j, c: (c[0] * nbr + i, j)
    else:
        gmap = lambda i, j, c: (i, c[0] * nbc + j)

    def body(c_ref, g_ref, r_ref, of_ref, ob_ref):
        s = g_ref[...] + r_ref[...]
        of_ref[...] = s
        ob_ref[...] = s.astype(BF16)

    omap = lambda i, j, c: (i, j)
    return pl.pallas_call(
        body, name=name,
        out_shape=(jax.ShapeDtypeStruct((K, N), F32), jax.ShapeDtypeStruct((K, N), BF16)),
        grid_spec=pltpu.PrefetchScalarGridSpec(
            num_scalar_prefetch=1, grid=(nbr, nbc),
            in_specs=[pl.BlockSpec((tm, tn), gmap), pl.BlockSpec((tm, tn), omap)],
            out_specs=(pl.BlockSpec((tm, tn), omap), pl.BlockSpec((tm, tn), omap))),
        compiler_params=_cparams(("parallel", "parallel")),
    )(cidx, g, recv)


def _shard_idx(shape, axis, j):
    size = shape[axis] // N_CHIPS
    idx = [slice(None), slice(None)]
    idx[axis] = pl.ds(j * size, size)
    return tuple(idx)


class _ScatterSide:
    def __init__(self, parts, shard_axes):
        self.inputs = list(parts)
        self.shard_axes = list(shard_axes)
        self.out_shapes = []
        for p, ax in zip(parts, shard_axes):
            shp = list(p.shape)
            shp[ax] //= N_CHIPS
            self.out_shapes.append(jax.ShapeDtypeStruct((3,) + tuple(shp), p.dtype))
        self.aliases = {}
        self.n_sems = 3 * len(parts)

    def _copies(self, srcs, outs, send, recv):
        x, y, c = _my_place()
        cps = []
        for t in range(len(srcs)):
            for r, chip in enumerate(_other_chips(x, y)):
                k = 3 * t + r
                j = 2 * chip[0] + chip[1]
                cps.append(_rcopy(srcs[t].at[_shard_idx(srcs[t].shape, self.shard_axes[t], j)], outs[t].at[r],
                                  send.at[k], recv.at[k], (*chip, c)))
        return cps

    def start(self, srcs, outs, send, recv):
        for cp in self._copies(srcs, outs, send, recv):
            cp.start()

    def finish(self, srcs, outs, send, recv):
        for cp in self._copies(srcs, outs, send, recv):
            cp.wait()


def _shard_sum(pf, recv, acc, layer, shard_axis, jcidx, name):
    _, K, N = recv.shape
    tm = _pick(K, (256, 128, 64, 32, 16))
    tn = _pick(N, (1024, 1408, 512, 256, 128))
    nbr, nbc = K // tm, N // tn
    if shard_axis == 0:
        pmap = lambda i, j, s: (s[0] * nbr + i, j)
        omap = lambda i, j, s: (layer, i, s[1] * nbc + j)
    else:
        pmap = lambda i, j, s: (i, s[0] * nbc + j)
        omap = lambda i, j, s: (layer, s[1] * nbr + i, j)

    def body(j_ref, p_ref, r_ref, a_ref, o_ref):
        o_ref[...] = ((p_ref[...] + r_ref[0].astype(F32)) + r_ref[1].astype(F32)) + r_ref[2].astype(F32)

    return pl.pallas_call(
        body, name=name, out_shape=jax.ShapeDtypeStruct(acc.shape, F32),
        grid_spec=pltpu.PrefetchScalarGridSpec(
            num_scalar_prefetch=1, grid=(nbr, nbc),
            in_specs=[pl.BlockSpec((tm, tn), pmap), pl.BlockSpec((3, tm, tn), lambda i, j, s: (0, i, j)), _ANY],
            out_specs=pl.BlockSpec((None, tm, tn), omap)),
        input_output_aliases={3: 0},
        compiler_params=_cparams(("parallel", "parallel")),
    )(jcidx, pf, recv, acc)


def _pair_join(accs, half_axes, name):
    n = len(accs)

    def body(*refs):
        outs = refs[n:2 * n]
        send_sems, recv_sems = refs[2 * n:]
        x, y, c = _my_place()
        sib = (x, y, 1 - c)

        def half(t, hc):
            return outs[t].at[(slice(None),) + _half_idx(outs[t].shape[1:], half_axes[t], hc)]

        cps = []
        for t in range(n):
            cp = _rcopy(half(t, c), half(t, c), send_sems.at[t], recv_sems.at[t], sib)
            cp.start()
            cps.append(cp)
        for t in range(n):
            _rcopy(half(t, 1 - c), half(t, 1 - c), send_sems.at[t], recv_sems.at[t], sib).wait_recv()
        for cp in cps:
            cp.wait_send()

    return pl.pallas_call(
        body, name=name, out_shape=tuple(jax.ShapeDtypeStruct(a.shape, a.dtype) for a in accs),
        in_specs=[_ANY] * n, out_specs=tuple([_ANY] * n), input_output_aliases={t: t for t in range(n)},
        scratch_shapes=[pltpu.SemaphoreType.DMA((n,)), pltpu.SemaphoreType.DMA((n,))],
    )(*accs)


_N_DEV = 8


def _allreduce_small(flat, name):
    _, R, _ = flat.shape

    def body(in_ref, out_ref, stage, send1, recv1, send2, recv2):
        x, y, c = _my_place()
        me = 4 * x + 2 * y + c
        places = [(px, py, pc) for px in range(2) for py in range(2) for pc in range(2)]
        def peer(r):
            return (x ^ (r >> 2), y ^ ((r >> 1) & 1), c ^ (r & 1))

        def peer_id(r):
            p = peer(r)
            return 4 * p[0] + 2 * p[1] + p[2]

        stage[0] = in_ref[me]
        cps = []
        for r in range(1, _N_DEV):
            cp = _rcopy(in_ref.at[peer_id(r)], stage.at[r], send1.at[r], recv1.at[r], peer(r))
            cp.start()
            cps.append(cp)
        for cp in cps:
            cp.wait()
        tot = jnp.zeros((R, 128), F32)
        for d in range(_N_DEV):
            tot = tot + stage[me ^ d]
        out_ref[me] = tot
        cps = []
        for r in range(1, _N_DEV):
            cp = _rcopy(out_ref.at[me], out_ref.at[me], send2.at[r], recv2.at[r], peer(r))
            cp.start()
            cps.append(cp)
        for r in range(1, _N_DEV):
            _rcopy(out_ref.at[peer_id(r)], out_ref.at[peer_id(r)], send2.at[r], recv2.at[r], peer(r)).wait_recv()
        for cp in cps:
            cp.wait_send()

    vm = pl.BlockSpec(memory_space=pltpu.VMEM)
    return pl.pallas_call(
        body, name=name, out_shape=jax.ShapeDtypeStruct(flat.shape, F32), in_specs=[vm], out_specs=vm,
        scratch_shapes=[pltpu.VMEM(flat.shape, F32)] + [pltpu.SemaphoreType.DMA((_N_DEV,))] * 4,
        compiler_params=_cparams(),
    )(flat)


_BIG = ("w_in", "ssm_w_glu", "w_branch", "w_out", "w_ffn_gate", "w_ffn_up", "w_ffn_down")
_BIG_SHARD_AXIS = {"w_in": 1, "ssm_w_glu": 1, "w_branch": 1, "w_out": 0, "w_ffn_gate": 1, "w_ffn_up": 1, "w_ffn_down": 0}
_SMALL = ("norm_mix_g", "gate_bias", "lru_conv_w", "lru_conv_b", "lru_wa", "lru_ba", "lru_wx", "lru_bx", "lru_lambda",
          "attn_rel_bias", "ssm_a_re", "ssm_a_im", "ssm_b_re", "ssm_b_im", "ssm_c_re", "ssm_c_im", "ssm_d",
          "ssm_log_step", "norm_ffn_g", "norm_final_g")
_SMALL_SHARDED = {"gate_bias": 2, "lru_conv_w": 2}
_WEIGHTS = ("norm_mix_g", "w_in", "gate_bias", "lru_conv_w", "lru_conv_b", "lru_wa", "lru_ba", "lru_wx", "lru_bx",
            "lru_lambda", "attn_rel_bias", "ssm_a_re", "ssm_a_im", "ssm_b_re", "ssm_b_im", "ssm_c_re", "ssm_c_im",
            "ssm_d", "ssm_log_step", "ssm_w_glu", "w_branch", "w_out", "norm_ffn_g", "w_ffn_gate", "w_ffn_up",
            "w_ffn_down", "norm_final_g")


def _carried(comm, phase, l, key, W, fn, *args, **kw):
    side = comm.side(phase, l, key, W)
    if side is None:
        return fn(*args, **kw)
    out, got = fn(*args, side=side, **kw)
    comm.took(phase, l, key, got, W)
    return out


def _layer_fwd(l, x, W, sm, comm):
    T = x.shape[0]
    nm = lambda s: f"{s}"
    h1 = _rms_fwd(x, sm["norm_mix_g"][l][None, :], nm("rms_fwd"))
    proj = _carried(comm, "fwd", l, "mm_in", W, _mm, h1, W["w_in"], M=T, N=IN_W, K=D_MODEL, b_lead=l, name=nm("mm_in"))
    wax, bax = sm["lru_prep"][l]
    cw, cb, lam = sm["lru_conv_w"][l], sm["lru_conv_b"][l][None, :], sm["lru_lambda"][l][None, :]
    y_a, hst = _lru_fwd(proj, cw, cb, wax, bax, lam, nm("lru_fwd"))
    bias = sm["attn_bias"][l]
    y_b = _carried(comm, "fwd", l, "attn_fwd", W, _attn_fwd, proj, bias, nm("attn_fwd"))
    tgt, bt, ct, lr, li, dflat = sm["s5_prep"][l]
    u = _to_chunks(proj, 5 * MIX_W, nm("to_chunks"))
    sp = _s5_scan(_s5_in(u, bt, nm("s5_in")), lr, li, nm("s5_scan"))
    pre, ycf = _s5_out(u, sp, tgt, ct, dflat, nm("s5_out"))
    y_c = _from_chunks(ycf, BF16, nm("from_chunks"))
    brs = []
    for k, yk in enumerate((y_a, y_b, y_c)):
        brs.append(_carried(comm, "fwd", l, f"mm_branch{k}", W, _mm, yk, W["w_branch"], M=T, N=D_MODEL, K=MIX_W,
                            b_lead=3 * l + k, out_dtype=BF16, name=nm("mm_branch")))
    brs.append(_carried(comm, "fwd", l, "mm_branch3", W, _mm, y_c, W["ssm_w_glu"], M=T, N=D_MODEL, K=MIX_W, b_lead=l,
                        out_dtype=BF16, name=nm("mm_branch")))
    br4 = tuple(brs)
    gb3 = sm["gate_bias"][l]
    merged = _merge_fwd(br4, proj, gb3, nm("merge_fwd"))
    x1 = _carried(comm, "fwd", l, "mm_out", W, _mm, merged, W["w_out"], M=T, N=D_MODEL, K=D_MODEL, b_lead=l, res=x,
                  name=nm("mm_out"))
    h2 = _rms_fwd(x1, sm["norm_ffn_g"][l][None, :], nm("rms_fwd"))
    gpre = _carried(comm, "fwd", l, "mm_ffn_gate", W, _mm, h2, W["w_ffn_gate"], M=T, N=FFN_H, K=D_MODEL, b_lead=l,
                    out_dtype=BF16, name=nm("mm_ffn_up"))
    upre = _carried(comm, "fwd", l, "mm_ffn_up", W, _mm, h2, W["w_ffn_up"], M=T, N=FFN_H, K=D_MODEL, b_lead=l,
                    out_dtype=BF16, name=nm("mm_ffn_up"))
    gu = (gpre, upre)
    act = _act_fwd(gu, nm("act_fwd"))
    x2 = _carried(comm, "fwd", l, "mm_down", W, _mm, act, W["w_ffn_down"], M=T, N=D_MODEL, K=FFN_H, b_lead=l, res=x1,
                  name=nm("mm_down"))
    saved = dict(x=x, h1=h1, proj=proj, hst=hst, y_a=y_a, y_b=y_b, y_c=y_c, u=u, sp=sp, pre=pre,
                 br4=br4, merged=merged, x1=x1, h2=h2, gu=gu, act=act)
    return x2, saved


def _layer_bwd(l, dx2, sv, W, sm, comm):
    T = dx2.shape[0]
    nm = lambda s: f"{s}"
    big, small = {}, {}
    dxb = dx2.astype(BF16)
    big["w_ffn_down"] = _carried(comm, "bwd", l, "mm_dw_down", W, _mm, sv["act"], dxb, M=FFN_H, N=D_MODEL, K=T, ta=True,
                                 name=nm("mm_dw_down"))
    dact = _carried(comm, "bwd", l, "mm_dact", W, _mm, dxb, W["w_ffn_down"], M=T, N=FFN_H, K=D_MODEL, tb=True, b_lead=l,
                    out_dtype=BF16, name=nm("mm_dact"))
    dg, du = _act_bwd(sv["gu"], dact, nm("act_bwd"))
    big["w_ffn_gate"] = _carried(comm, "bwd", l, "mm_dw_gate", W, _mm, sv["h2"], dg, M=D_MODEL, N=FFN_H, K=T, ta=True,
                                 name=nm("mm_dw_up"))
    big["w_ffn_up"] = _mm(sv["h2"], du, M=D_MODEL, N=FFN_H, K=T, ta=True, name=nm("mm_dw_up"))
    dh2 = _mm(dg, W["w_ffn_gate"], M=T, N=D_MODEL, K=FFN_H, tb=True, b_lead=l, name=nm("mm_dh2"))
    dh2 = _mm(du, W["w_ffn_up"], M=T, N=D_MODEL, K=FFN_H, tb=True, b_lead=l, res=dh2, name=nm("mm_dh2r"))
    dx1, dgn = _rms_bwd(sv["x1"], sm["norm_ffn_g"][l][None, :], dh2, dx2, nm("rms_bwd"))
    small["norm_ffn_g"] = dgn[0]
    dx1b = dx1.astype(BF16)
    big["w_out"] = _mm(sv["merged"], dx1b, M=D_MODEL, N=D_MODEL, K=T, ta=True, name=nm("mm_dw_out"))
    dm = _mm(dx1b, W["w_out"], M=T, N=D_MODEL, K=D_MODEL, tb=True, b_lead=l, out_dtype=BF16, name=nm("mm_dmerged"))
    dbr, dgates, dgb = _merge_bwd(sv["br4"], sv["proj"], sm["gate_bias"][l], dm, nm("merge_bwd"))
    small["gate_bias"] = dgb
    ys = (sv["y_a"], sv["y_b"], sv["y_c"])
    big["w_branch"] = [_mm(ys[k], dbr[k], M=MIX_W, N=D_MODEL, K=T, ta=True, name=nm("mm_dw_branch")) for k in range(3)]
    big["ssm_w_glu"] = _mm(sv["y_c"], dbr[3], M=MIX_W, N=D_MODEL, K=T, ta=True, name=nm("mm_dw_branch"))
    dya = _mm(dbr[0], W["w_branch"], M=T, N=MIX_W, K=D_MODEL, tb=True, b_lead=3 * l, name=nm("mm_dy"))
    dyb = _mm(dbr[1], W["w_branch"], M=T, N=MIX_W, K=D_MODEL, tb=True, b_lead=3 * l + 1, out_dtype=BF16, name=nm("mm_dy"))
    dyc = _mm(dbr[2], W["w_branch"], M=T, N=MIX_W, K=D_MODEL, tb=True, b_lead=3 * l + 2, name=nm("mm_dy"))
    dyc = _mm(dbr[3], W["ssm_w_glu"], M=T, N=MIX_W, K=D_MODEL, tb=True, b_lead=l, res=dyc, name=nm("mm_dyr"))
    tgt, bt, ct, lr, li, dflat = sm["s5_prep"][l]
    dpre, dsp, d_tgt, d_ct, d_dflat = _s5_bwd_out(_to_chunks(dyc, 0, nm("to_chunks")), sv["pre"], sv["u"], sv["sp"], ct,
                                                  nm("s5_bwd_out"))
    dsin, d_lr, d_li = _s5_rscan(dsp, sv["sp"], lr, li, nm("s5_rscan"))
    du_f, d_bt = _s5_bwd_in(dpre, dsin, sv["u"], tgt, bt, dflat, nm("s5_bwd_in"))
    d_u = _from_chunks(du_f, BF16, nm("from_chunks"))
    small["s5_tables"] = (d_tgt, d_bt, d_ct, d_lr, d_li, d_dflat)
    dq, dk, dv, dbias = _carried(comm, "bwd", l, "attn_bwd", W, _attn_bwd, sv["proj"], sm["attn_bias"][l], dyb, nm("attn_bwd"))
    small["attn_bias"] = dbias
    wax, bax = sm["lru_prep"][l]
    cw, cb, lam = sm["lru_conv_w"][l], sm["lru_conv_b"][l][None, :], sm["lru_lambda"][l][None, :]
    d_lx, d_lg, d_cw, d_cb, d_wax, d_bax, d_lam = _lru_bwd(sv["proj"], sv["hst"], dya, cw, cb, wax, bax, lam, nm("lru_bwd"))
    small["lru_conv_w"], small["lru_conv_b"], small["lru_lambda"] = d_cw, d_cb[0], d_lam[0]
    small["lru_tables"] = (d_wax, d_bax)
    dproj = jnp.concatenate([d_lx, d_lg, dq, dk, dv, d_u] + list(dgates), axis=1)
    big["w_in"] = _carried(comm, "bwd", l, "mm_dw_in", W, _mm, sv["h1"], dproj, M=D_MODEL, N=IN_W, K=T, ta=True,
                           name=nm("mm_dw_in"))
    dh1 = _carried(comm, "bwd", l, "mm_dh1", W, _mm, dproj, W["w_in"], M=T, N=D_MODEL, K=IN_W, tb=True, b_lead=l,
                   name=nm("mm_dh1"))
    dx, dgn = _rms_bwd(sv["x"], sm["norm_mix_g"][l][None, :], dh1, dx1, nm("rms_bwd"))
    small["norm_mix_g"] = dgn[0]
    return dx, big, small


_TENSORS = tuple((n, k) for n in _BIG for k in range(N_BRANCH if n == "w_branch" else 1))
_FWD_CARRIERS = {"mm_in": (("w_out", "w_branch", "ssm_w_glu", "w_ffn_gate"), 0), "attn_fwd": (("w_ffn_up",), 0),
                 "mm_branch0": (("w_ffn_down",), 0), "mm_branch1": (("w_ffn_down",), 0), "mm_branch2": (("w_ffn_down",), 0),
                 "mm_branch3": (("w_ffn_down",), 0), "mm_out": (("w_in",), 1), "mm_ffn_gate": (("w_in",), 1),
                 "mm_ffn_up": (("w_in",), 1), "mm_down": (("w_in",), 1)}
_FWD_PIECES = {"mm_branch0": (0, 4), "mm_branch1": (1, 4), "mm_branch2": (2, 4), "mm_branch3": (3, 4),
               "mm_out": (0, 4), "mm_ffn_gate": (1, 4), "mm_ffn_up": (2, 4), "mm_down": (3, 4)}
_BWD_PAIR_CARRIERS = {"mm_dw_down": ("w_in", "w_out", "ssm_w_glu"), "mm_dact": ("w_ffn_gate", "w_ffn_up", "w_ffn_down", "w_branch")}
_BWD_CARRIERS = {"mm_dw_gate": ("w_out", "w_branch", "ssm_w_glu"), "attn_bwd": ("w_ffn_down",), "mm_dw_in": ("w_in",),
                 "mm_dh1": ("w_ffn_gate", "w_ffn_up")}


class _StepComm:
    def __init__(self, depth, cidx, jcidx):
        self.depth, self.cidx, self.jcidx = depth, cidx, jcidx
        self.accs = {}
        self.raw = None
        self.paired = {}
        self.pending = None

    def gather_side(self, W, names, l, piece):
        fulls, axes, regions = [], [], []
        for n in names:
            per = N_BRANCH if n == "w_branch" else 1
            fulls.append(W[n])
            axes.append(_BIG_SHARD_AXIS[n] + 1)
            regions.append((per * l, per, "other", piece))
        return _GatherSide(fulls, axes, regions)

    def side(self, phase, l, key, W):
        if phase == "fwd":
            if key not in _FWD_CARRIERS or l + _FWD_CARRIERS[key][1] >= self.depth:
                return None
            names, ahead = _FWD_CARRIERS[key]
            return self.gather_side(W, names, l + ahead, _FWD_PIECES.get(key, (0, 1)))
        if key in _BWD_PAIR_CARRIERS and self.raw is not None:
            nks = [nk for nk in _TENSORS if nk[0] in _BWD_PAIR_CARRIERS[key]]
            return _PairSide([self.raw[1][nk] for nk in nks], [1 - _BIG_SHARD_AXIS[nk[0]] for nk in nks])
        if key in _BWD_CARRIERS and self.pending is not None:
            nks = [nk for nk in _TENSORS if nk[0] in _BWD_CARRIERS[key]]
            return _ScatterSide([self.pending[1][nk][1] for nk in nks], [_BIG_SHARD_AXIS[nk[0]] for nk in nks])
        return None

    def took(self, phase, l, key, got, W):
        if phase == "fwd":
            W.update(zip(_FWD_CARRIERS[key][0], got))
        elif key in _BWD_PAIR_CARRIERS:
            nks = [nk for nk in _TENSORS if nk[0] in _BWD_PAIR_CARRIERS[key]]
            self.paired.update(zip(nks, got))
            if len(self.paired) == len(_TENSORS):
                self.pair_sums()
        else:
            nks = [nk for nk in _TENSORS if nk[0] in _BWD_CARRIERS[key]]
            self.shard_sums(self.pending[0], nks, got)

    def pair_sums(self):
        l, grads = self.raw
        self.pending = (l, {nk: _pair_sum(grads[nk], self.paired[nk], 1 - _BIG_SHARD_AXIS[nk[0]], self.cidx, "pair_sum")
                            for nk in _TENSORS})
        self.raw, self.paired = None, {}

    def shard_sums(self, l, nks, got):
        for nk, r in zip(nks, got):
            pf = self.pending[1][nk][0]
            sa = _BIG_SHARD_AXIS[nk[0]]
            if nk not in self.accs:
                shp = list(pf.shape)
                shp[sa] //= N_CHIPS
                shp[1 - sa] *= 2
                self.accs[nk] = lax.empty((self.depth,) + tuple(shp), F32)
            self.accs[nk] = _shard_sum(pf, r, self.accs[nk], l, sa, self.jcidx, "shard_sum")

    def on_big(self, l, big):
        grads = {}
        for n in _BIG:
            gs = big[n] if isinstance(big[n], list) else [big[n]]
            grads.update({(n, k): g for k, g in enumerate(gs)})
        self.raw = (l, grads)
        if l == 0:
            half_axes = [1 - _BIG_SHARD_AXIS[nk[0]] for nk in _TENSORS]
            self.paired = dict(zip(_TENSORS, _run_side(_PairSide([grads[nk] for nk in _TENSORS], half_axes), "pair_exchange")))
            self.pair_sums()
            side = _ScatterSide([self.pending[1][nk][1] for nk in _TENSORS], [_BIG_SHARD_AXIS[nk[0]] for nk in _TENSORS])
            self.shard_sums(0, _TENSORS, _run_side(side, "chip_scatter"))
            self.pending = None


class _NoComm:
    def __init__(self, on_big):
        self.on_big = on_big

    def side(self, phase, l, key, W):
        return None


def _local_step(xs, tgt, W, sm, comm):
    W = dict(W)
    sm = dict(sm)
    depth = sm["norm_mix_g"].shape[0]
    lru_o, lru_vjp = jax.vjp(jax.vmap(_lru_prep), sm["lru_wa"], sm["lru_wx"], sm["lru_ba"], sm["lru_bx"])
    attn_o, attn_vjp = jax.vjp(jax.vmap(_attn_bias), sm["attn_rel_bias"])
    s5_names = ("ssm_a_re", "ssm_a_im", "ssm_b_re", "ssm_b_im", "ssm_c_re", "ssm_c_im", "ssm_d", "ssm_log_step")
    s5_o, s5_vjp = jax.vjp(jax.vmap(_s5_prep), *[sm[n] for n in s5_names])
    wax_all = lru_o[0].astype(BF16)
    sm["lru_prep"] = [(wax_all[l], lru_o[1][l]) for l in range(depth)]
    sm["attn_bias"] = [attn_o[l] for l in range(depth)]
    kt_in = [t.reshape((depth * SSM_G,) + t.shape[2:]) for t in s5_o[:4]]
    tgt_all = _s5_tgt(*kt_in, "s5_tgt").reshape(depth, SSM_G, S5_LW, S5_LW)
    sm["s5_prep"] = [(tgt_all[l],) + tuple(t[l] for t in s5_o[4:]) for l in range(depth)]

    saved = []
    for l in range(depth):
        xs, sv = _layer_fwd(l, xs, W, sm, comm)
        saved.append(sv)
    loss_part, dx, dgf = _final_loss(xs, sm["norm_final_g"][None, :], tgt, "final_loss")

    direct = ("norm_mix_g", "gate_bias", "lru_conv_w", "lru_conv_b", "lru_lambda", "norm_ffn_g")
    per_layer = [None] * depth
    for l in reversed(range(depth)):
        dx, big, per_layer[l] = _layer_bwd(l, dx, saved[l], W, sm, comm)
        comm.on_big(l, big)
    stacked = lambda key, i: jnp.stack([per_layer[l][key][i] for l in range(depth)])
    small_tree = {n: jnp.stack([per_layer[l][n] for l in range(depth)]) for n in direct}
    d_wa, d_wx, d_ba, d_bx = lru_vjp(tuple(stacked("lru_tables", i) for i in range(2)))
    (d_rel,) = attn_vjp(jnp.stack([per_layer[l]["attn_bias"] for l in range(depth)]))
    d_kt_in = _s5_tgt_t(stacked("s5_tables", 0).reshape(depth * SSM_G, S5_LW, S5_LW), *kt_in, "s5_tgt_t")
    d_kt_in = [t.reshape((depth, SSM_G) + t.shape[1:]) for t in d_kt_in]
    d_s5 = s5_vjp(tuple(d_kt_in) + tuple(stacked("s5_tables", i) for i in range(1, 6)))
    small_tree.update(lru_wa=d_wa, lru_wx=d_wx, lru_ba=d_ba, lru_bx=d_bx, attn_rel_bias=d_rel, norm_final_g=dgf[0])
    small_tree.update(zip(s5_names, d_s5))
    return loss_part, dx, small_tree


def _pack_small(tree, names):
    flat = jnp.concatenate([tree[n].reshape(-1) for n in names])
    per = -(-flat.shape[0] // (_N_DEV * 128 * 8)) * (128 * 8)
    flat = jnp.pad(flat, (0, _N_DEV * per - flat.shape[0]))
    return flat.reshape(_N_DEV, per // 128, 128)


def _unpack_small(flat, like, names):
    flat = flat.reshape(-1)
    out, off = {}, 0
    for n in names:
        size = math.prod(like[n].shape)
        out[n] = flat[off:off + size].reshape(like[n].shape)
        off += size
    return out


def kernel(x, norm_mix_g, w_in, gate_bias, lru_conv_w, lru_conv_b, lru_wa, lru_ba, lru_wx, lru_bx, lru_lambda, attn_rel_bias, ssm_a_re, ssm_a_im, ssm_b_re, ssm_b_im, ssm_c_re, ssm_c_im, ssm_d, ssm_log_step, ssm_w_glu, w_branch, w_out, norm_ffn_g, w_ffn_gate, w_ffn_up, w_ffn_down, norm_final_g, loss_target, m_norm_mix_g, m_w_in, m_gate_bias, m_lru_conv_w, m_lru_conv_b, m_lru_wa, m_lru_ba, m_lru_wx, m_lru_bx, m_lru_lambda, m_attn_rel_bias, m_ssm_a_re, m_ssm_a_im, m_ssm_b_re, m_ssm_b_im, m_ssm_c_re, m_ssm_c_im, m_ssm_d, m_ssm_log_step, m_ssm_w_glu, m_w_branch, m_w_out, m_norm_ffn_g, m_w_ffn_gate, m_w_ffn_up, m_w_ffn_down, m_norm_final_g, v_norm_mix_g, v_w_in, v_gate_bias, v_lru_conv_w, v_lru_conv_b, v_lru_wa, v_lru_ba, v_lru_wx, v_lru_bx, v_lru_lambda, v_attn_rel_bias, v_ssm_a_re, v_ssm_a_im, v_ssm_b_re, v_ssm_b_im, v_ssm_c_re, v_ssm_c_im, v_ssm_d, v_ssm_log_step, v_ssm_w_glu, v_w_branch, v_w_out, v_norm_ffn_g, v_w_ffn_gate, v_w_ffn_up, v_w_ffn_down, v_norm_final_g):
    args = dict(locals())
    w = {n: args[n] for n in _WEIGHTS}
    m = {n: args["m_" + n] for n in _WEIGHTS}
    v = {n: args["v_" + n] for n in _WEIGHTS}
    depth = w_in.shape[0]
    xc, yc, cc = _my_place()
    jchip = 2 * xc + yc
    cidx = jnp.reshape(cc, (1,)).astype(jnp.int32)
    jidx = jnp.reshape(jchip, (1,)).astype(jnp.int32)
    jcidx = jnp.stack([jchip, cc]).astype(jnp.int32)

    blocks = [w[n] for n in _BIG]
    blocks[2] = blocks[2].reshape(depth * N_BRANCH, MIX_W, -1)
    axes = [_BIG_SHARD_AXIS[n] + 1 for n in _BIG] + [_SMALL_SHARDED[n] for n in _SMALL_SHARDED]
    placed = [_place_shard(b, ax, jidx, BF16, "place_shard") for b, ax in zip(blocks, axes)]
    placed += [_place_shard(w[n], _SMALL_SHARDED[n], jidx, F32, "place_shard") for n in _SMALL_SHARDED]
    W = dict(zip(_BIG, placed))
    first = [W["w_in"]] + placed[len(_BIG):]
    regions = [(0, 1, "other")] + [(0, depth, "lead")] * len(_SMALL_SHARDED)
    gathered = _run_side(_GatherSide(first, [axes[0]] + axes[len(_BIG):], regions), "gather_weights")
    W["w_in"] = gathered[0]
    sm_full = dict(zip(_SMALL_SHARDED, gathered[1:]))
    sm = {n: w[n] for n in _SMALL if n not in _SMALL_SHARDED}
    sm.update(sm_full)

    comm = _StepComm(depth, cidx, jcidx)
    loss_part, dx, small_tree = _local_step(x[0], loss_target[0], W, sm, comm)
    loss = lax.psum(loss_part, ("x", "y", "c"))
    grad_x = dx[None]

    joined = _pair_join([comm.accs[nk] for nk in _TENSORS], [1 - _BIG_SHARD_AXIS[nk[0]] for nk in _TENSORS], "pair_join")
    jd = dict(zip(_TENSORS, joined))
    grads = {}
    for n in _BIG:
        if n == "w_branch":
            grads[n] = jnp.stack([jd[(n, k)] for k in range(N_BRANCH)], axis=1)
        else:
            grads[n] = jd[(n, 0)]
    like = {n: (sm_full[n] if n in _SMALL_SHARDED else w[n]) for n in _SMALL}
    red = _unpack_small(_allreduce_small(_pack_small(small_tree, _SMALL), "allreduce_small"), like, _SMALL)
    for n in _SMALL:
        if n in _SMALL_SHARDED:
            size = w[n].shape[2]
            grads[n] = lax.dynamic_slice_in_dim(red[n], (2 * xc + yc) * size, size, axis=2)
        else:
            grads[n] = red[n]

    delta, new_m, new_v = {}, {}, {}
    for n in _BIG:
        shp = w[n].shape
        two = lambda a: a.reshape(-1, shp[-1])
        d_, m_, v_ = _adamw(two(w[n]), two(grads[n]), two(m[n]), two(v[n]), "adamw")
        delta[n], new_m[n], new_v[n] = d_.reshape(shp), m_.reshape(shp), v_.reshape(shp)
    pk = lambda tree: _pack_small(tree, _SMALL).reshape(-1, 128)
    d_, m_, v_ = _adamw(pk(w), pk(grads), pk(m), pk(v), "adamw_small")
    like_local = {n: w[n] for n in _SMALL}
    for tree, flat in ((delta, d_), (new_m, m_), (new_v, v_)):
        tree.update(_unpack_small(flat, like_local, _SMALL))
    return (loss, grad_x, *[grads[n] for n in _WEIGHTS], *[delta[n] for n in _WEIGHTS], *[new_m[n] for n in _WEIGHTS],
            *[new_v[n] for n in _WEIGHTS])
```

```python
import functools
import math

import jax
import jax.numpy as jnp
from jax import lax
from jax.experimental import pallas as pl
from jax.experimental.pallas import tpu as pltpu

F32 = jnp.float32
BF16 = jnp.bfloat16

D_MODEL = 2048
MIX_W = 1024
N_BRANCH = 3
LRU_BLOCKS = 16
LRU_BW = 64
CONV_W = 4
LRU_C = 8.0
CHUNK = 64
ATT_HEADS = 8
ATT_HD = 128
ATT_LEFT = 8
ATT_BAND = (ATT_LEFT + 1) * CHUNK
MAX_REL = 128
N_REL = 2 * MAX_REL + 1
SSM_G = 64
SSM_H = 16
SSM_P = 64
FFN_H = 5632
IN_W = 6 * MIX_W + N_BRANCH * D_MODEL
NORM_EPS = 1e-6
MASK_VALUE = -1e30
ADAM_LR, ADAM_B1, ADAM_B2, ADAM_EPS, ADAM_WD, ADAM_STEP = 0.001, 0.9, 0.999, 1e-08, 0.01, 10

S5_L = 16
S5_LW = S5_L * SSM_H
N_CHIPS = 4
V7X_VMEM_LIMIT = 56 * 1024 * 1024
HI = lax.Precision.HIGHEST
MESH = pl.DeviceIdType.MESH


def _cparams(sem=None):
    return pltpu.CompilerParams(dimension_semantics=sem, vmem_limit_bytes=V7X_VMEM_LIMIT)


def _pick(n, prefs):
    for p in prefs:
        if n % p == 0:
            return p
    return n


_MM_VMEM_BUDGET = 46 * 1024 * 1024


def _mm_tiles(M, N, K, has_res, out_bytes):
    tn = _pick(N, (1024, 1408, 512, 256, 128))
    for tk in (K, 2048, 1408, 1024, 512, 256, 128):
        if K % tk:
            continue
        for tm in (1024, 512, 256, 128, 64, 32, 16, 8):
            if M % tm:
                continue
            need = 2 * 2 * (tm * tk + tk * tn) + 2 * tm * tn * out_bytes
            need += tm * tn * 4 if tk < K else 0
            need += 2 * tm * tn * 4 if has_res else 0
            if need <= _MM_VMEM_BUDGET:
                return tm, tn, tk
    raise ValueError((M, N, K))


def _call(body, *, name, grid, in_specs, out_specs, out_shape, args, scratch_shapes=(), sem=None, side=None):
    in_specs, out_specs, out_shape = list(in_specs), list(out_specs), list(out_shape)
    scratch_shapes = list(scratch_shapes)
    if side is None:
        outs = pl.pallas_call(body, name=name, out_shape=tuple(out_shape), grid=grid, in_specs=in_specs,
                              out_specs=tuple(out_specs), scratch_shapes=scratch_shapes, compiler_params=_cparams(sem))(*args)
        return tuple(outs), ()
    n_in, n_out, n_scr = len(in_specs), len(out_shape), len(scratch_shapes)
    s_in, s_out = len(side.inputs), len(side.out_shapes)

    def wrapped(*refs):
        mi, refs = refs[:n_in], refs[n_in:]
        si, refs = refs[:s_in], refs[s_in:]
        mo, refs = refs[:n_out], refs[n_out:]
        so, refs = refs[:s_out], refs[s_out:]
        scr, (send, recv) = refs[:n_scr], refs[n_scr:]
        first = functools.reduce(jnp.logical_and, [pl.program_id(d) == 0 for d in range(len(grid))])
        last = functools.reduce(jnp.logical_and, [pl.program_id(d) == g - 1 for d, g in enumerate(grid)])

        @pl.when(first)
        def _():
            side.start(si, so, send, recv)

        body(*mi, *mo, *scr)

        @pl.when(last)
        def _():
            side.finish(si, so, send, recv)

    outs = pl.pallas_call(
        wrapped, name=name, out_shape=tuple(out_shape + list(side.out_shapes)), grid=grid,
        in_specs=in_specs + [_ANY] * s_in, out_specs=tuple(out_specs + [_ANY] * s_out),
        scratch_shapes=scratch_shapes + [pltpu.SemaphoreType.DMA((side.n_sems,)), pltpu.SemaphoreType.DMA((side.n_sems,))],
        input_output_aliases={n_in + i: n_out + o for i, o in side.aliases.items()},
        compiler_params=_cparams(("arbitrary",) * len(grid)),
    )(*args, *side.inputs)
    return tuple(outs[:n_out]), tuple(outs[n_out:])


def _mm(a, b, *, M, N, K, name, ta=False, tb=False, a_lead=None, b_lead=None, a_off=(0, 0), b_off=(0, 0),
        out_dtype=F32, res=None, tm=None, tn=None, tk=None, side=None):
    if tm is None and tn is None and tk is None:
        tm, tn, tk = _mm_tiles(M, N, K, res is not None, jnp.dtype(out_dtype).itemsize)
    nk = K // tk

    def spec(blk, lead, off, order):
        r0, c0 = off[0] // blk[0], off[1] // blk[1]
        assert off[0] % blk[0] == 0 and off[1] % blk[1] == 0
        if lead is None:
            return pl.BlockSpec(blk, lambda i, j, k: (r0 + order(i, j, k)[0], c0 + order(i, j, k)[1]))
        return pl.BlockSpec((None,) + blk, lambda i, j, k: (lead, r0 + order(i, j, k)[0], c0 + order(i, j, k)[1]))

    a_spec = spec((tk, tm), a_lead, a_off, lambda i, j, k: (k, i)) if ta else spec((tm, tk), a_lead, a_off, lambda i, j, k: (i, k))
    b_spec = spec((tn, tk), b_lead, b_off, lambda i, j, k: (j, k)) if tb else spec((tk, tn), b_lead, b_off, lambda i, j, k: (k, j))
    dims = (((0 if ta else 1,), (1 if tb else 0,)), ((), ()))
    in_specs = [a_spec, b_spec]
    args = [a, b]
    if res is not None:
        in_specs.append(pl.BlockSpec((tm, tn), lambda i, j, k: (i, j)))
        args.append(res)

    def body(*refs):
        a_ref, b_ref = refs[:2]
        r_ref = refs[2] if res is not None else None
        o_ref = refs[3] if res is not None else refs[2]

        def dot():
            return lax.dot_general(a_ref[...], b_ref[...], dims, preferred_element_type=F32)

        def finish(r):
            if r_ref is not None:
                r = r + r_ref[...]
            o_ref[...] = r.astype(out_dtype)

        if nk == 1:
            finish(dot())
            return
        acc = refs[-1]
        k = pl.program_id(2)

        @pl.when(k == 0)
        def _():
            acc[...] = dot()

        @pl.when(jnp.logical_and(k > 0, k < nk - 1))
        def _():
            acc[...] += dot()

        @pl.when(k == nk - 1)
        def _():
            finish(acc[...] + dot())

    (out,), got = _call(
        body, name=name, out_shape=[jax.ShapeDtypeStruct((M, N), out_dtype)],
        grid=(M // tm, N // tn, nk), in_specs=in_specs, out_specs=[pl.BlockSpec((tm, tn), lambda i, j, k: (i, j))],
        scratch_shapes=[pltpu.VMEM((tm, tn), F32)] if nk > 1 else [],
        sem=("parallel", "parallel", "arbitrary"), args=args, side=side)
    return out if side is None else (out, got)


def _rowwise(fn, rows, consts, out_rows, out_accs, *, R, name, tm=256, rs=16, cs=None):
    tm = min(tm, R)
    assert R % tm == 0 and tm % rs == 0
    n_r, n_c, n_o, n_a = len(rows), len(consts), len(out_rows), len(out_accs)
    nsteps = R // tm
    widths = [w for _, _, w in rows]
    if cs is not None:
        assert all(w == widths[0] for w in widths) and widths[0] % cs == 0
        col_chunks = [(c0, cs) for c0 in range(0, widths[0], cs)]
    else:
        col_chunks = [None]

    def body(*refs):
        r_refs = refs[:n_r]
        c_refs = refs[n_r:n_r + n_c]
        o_refs = refs[n_r + n_c:n_r + n_c + n_o]
        a_refs = refs[n_r + n_c + n_o:n_r + n_c + n_o + n_a]
        s_refs = refs[n_r + n_c + n_o + n_a:]
        i = pl.program_id(0)

        @pl.when(i == 0)
        def _():
            for s in s_refs:
                s[...] = jnp.zeros_like(s)

        def piece(g, carry):
            r0 = pl.multiple_of(g * rs, rs)
            for cc in col_chunks:
                csl = slice(None) if cc is None else slice(cc[0], cc[0] + cc[1])
                rp = [r[pl.ds(r0, rs), csl].astype(F32) for r in r_refs]
                cp = [c[:, csl] for c in c_refs]
                outs, accs = fn(rp, cp)
                for o_ref, o in zip(o_refs, outs):
                    o_ref[pl.ds(r0, rs), csl] = o.astype(o_ref.dtype)
                for s_ref, av in zip(s_refs, accs):
                    s_ref[:, csl] += av
            return carry

        lax.fori_loop(0, tm // rs, piece, 0)

        @pl.when(i == nsteps - 1)
        def _():
            for a_ref, s_ref in zip(a_refs, s_refs):
                a_ref[...] = jnp.sum(s_ref[...], axis=0, keepdims=True)

    in_specs = [pl.BlockSpec((tm, w), functools.partial(lambda i, cb: (i, cb), cb=off // w)) for _, off, w in rows]
    for _, off, w in rows:
        assert off % w == 0
    in_specs += [pl.BlockSpec(c.shape, lambda i: (0, 0)) for c in consts]
    out_specs = [pl.BlockSpec((tm, w), lambda i: (i, 0)) for w, _ in out_rows]
    out_specs += [pl.BlockSpec((1, w), lambda i: (0, 0)) for w in out_accs]
    out_shape = [jax.ShapeDtypeStruct((R, w), dt) for w, dt in out_rows]
    out_shape += [jax.ShapeDtypeStruct((1, w), F32) for w in out_accs]
    return pl.pallas_call(
        body, name=name, out_shape=tuple(out_shape), grid=(nsteps,), in_specs=in_specs, out_specs=tuple(out_specs),
        scratch_shapes=[pltpu.VMEM((rs, w), F32) for w in out_accs],
        compiler_params=_cparams(("arbitrary",)),
    )(*[r for r, _, _ in rows], *consts)


def _rms(x, g):
    r = lax.rsqrt(jnp.mean(x * x, axis=-1, keepdims=True) + NORM_EPS)
    return x * r * g


def _rms_fwd(x, g, name):
    R = x.shape[0]

    def fn(rp, cp):
        return [_rms(rp[0], cp[0])], []

    return _rowwise(fn, [(x, 0, D_MODEL)], [g], [(D_MODEL, BF16)], [], R=R, name=name)[0]


def _rms_bwd(x, g, dh, dres, name):
    R = x.shape[0]

    def fn(rp, cp):
        xv, dhv, drv = rp
        _, pull = jax.vjp(_rms, xv, jnp.broadcast_to(cp[0], xv.shape))
        dx, dgv = pull(dhv)
        return [drv + dx], [dgv]

    dx, dg = _rowwise(fn, [(x, 0, D_MODEL), (dh, 0, D_MODEL), (dres, 0, D_MODEL)], [g], [(D_MODEL, F32)], [D_MODEL],
                      R=R, name=name, rs=16)
    return dx, dg


def _final_loss(x, g, tgt, name):
    R = x.shape[0]

    def loss_rows(xv, gv, tv):
        e = _rms(xv, gv) - tv
        return 0.5 * jnp.mean(e * e, axis=-1, keepdims=True)

    def fn(rp, cp):
        xv, tv = rp
        lr, pull = jax.vjp(lambda a, b: loss_rows(a, b, tv), xv, jnp.broadcast_to(cp[0], xv.shape))
        dx, dgv = pull(jnp.ones_like(lr))
        return [dx], [dgv, jnp.broadcast_to(lr, (lr.shape[0], 128))]

    dx, dg, lsum = _rowwise(fn, [(x, 0, D_MODEL), (tgt, 0, D_MODEL)], [g], [(D_MODEL, F32)], [D_MODEL, 128],
                            R=R, name=name, rs=8)
    return lsum[0, 0], dx, dg


def _neg_expm1(z):
    u = jnp.exp(z)
    safe = jnp.where(u == 1.0, 0.5, u)
    return -jnp.where(u == 1.0, z, (safe - 1.0) * z / jnp.log(safe))


def _lru_ab(xc, pr, pi, lam):
    r = jax.nn.sigmoid(pr)
    i = jax.nn.sigmoid(pi)
    log_a = -LRU_C * r * jax.nn.softplus(-lam)
    a = jnp.exp(log_a)
    b = jnp.sqrt(_neg_expm1(2.0 * log_a)) * (i * xc)
    return a, b


def _gated(h, gate):
    return h * jax.nn.gelu(gate)


def _row_iota8(w):
    return lax.broadcasted_iota(jnp.int32, (8, w), 0)


def _shift_dn(x, halo, s):
    xs = pltpu.roll(x, s, 0)
    hs = pltpu.roll(halo, s, 0)
    first = jnp.where(_row_iota8(x.shape[1]) < s, hs, xs[0:8])
    return jnp.concatenate([first, xs[8:]], axis=0) if x.shape[0] > 8 else first


def _shift_up(x, nxt, s):
    n = x.shape[0]
    xs = pltpu.roll(x, n - s, 0)
    ns = pltpu.roll(nxt, 8 - s, 0)
    last = jnp.where(_row_iota8(x.shape[1]) >= 8 - s, ns, xs[n - 8:])
    return jnp.concatenate([xs[:n - 8], last], axis=0) if n > 8 else last


def _lru_tiles(T):
    tT = min(256, T)
    return tT, T // tT


def _lru_fwd(proj, cw, cb, wax, bax, lam, name):
    T = proj.shape[0]
    W = MIX_W
    tT, nT = _lru_tiles(T)

    def body(x_ref, xh_ref, gt_ref, cw_ref, cb_ref, wax_ref, bax_ref, lam_ref, y_ref, h_ref, a_s, b_s, hc_s):
        i = pl.program_id(0)

        @pl.when(i == 0)
        def _():
            hc_s[...] = jnp.zeros_like(hc_s)

        x = x_ref[...]
        halo = jnp.where(i > 0, xh_ref[...], 0.0)
        w = cw_ref[...]
        xc = (cb_ref[...] + w[3:4] * x + w[2:3] * _shift_dn(x, halo, 1) + w[1:2] * _shift_dn(x, halo, 2)
              + w[0:1] * _shift_dn(x, halo, 3))
        pre = jnp.dot(xc.astype(BF16), wax_ref[...], preferred_element_type=F32) + bax_ref[...]
        a, b = _lru_ab(xc, pre[:, :W], pre[:, W:], lam_ref[...])
        a_s[...] = a
        b_s[...] = b
        row = _row_iota8(W)

        def grp(gi, hprev):
            r0 = pl.multiple_of(gi * 8, 8)
            A = a_s[pl.ds(r0, 8), :]
            B = b_s[pl.ds(r0, 8), :]
            for s in (1, 2, 4):
                As = pltpu.roll(A, s, 0)
                Bs = pltpu.roll(B, s, 0)
                m = row >= s
                B = jnp.where(m, A * Bs + B, B)
                A = jnp.where(m, A * As, A)
            H = A * hprev + B
            h_ref[pl.ds(r0, 8), :] = H
            return H[7:8, :]

        hc_s[0:1, :] = lax.fori_loop(0, tT // 8, grp, hc_s[0:1, :])
        y_ref[...] = _gated(h_ref[...], gt_ref[...]).astype(BF16)

    hb = tT // 8
    return pl.pallas_call(
        body, name=name,
        out_shape=(jax.ShapeDtypeStruct((T, W), BF16), jax.ShapeDtypeStruct((T, W), F32)),
        grid=(nT,),
        in_specs=[pl.BlockSpec((tT, W), lambda i: (i, 0)),
                  pl.BlockSpec((8, W), lambda i: (jnp.maximum(i * hb - 1, 0), 0)),
                  pl.BlockSpec((tT, W), lambda i: (i, 1)),
                  pl.BlockSpec((CONV_W, W), lambda i: (0, 0)), pl.BlockSpec((1, W), lambda i: (0, 0)),
                  pl.BlockSpec((W, 2 * W), lambda i: (0, 0)), pl.BlockSpec((1, 2 * W), lambda i: (0, 0)),
                  pl.BlockSpec((1, W), lambda i: (0, 0))],
        out_specs=(pl.BlockSpec((tT, W), lambda i: (i, 0)), pl.BlockSpec((tT, W), lambda i: (i, 0))),
        scratch_shapes=[pltpu.VMEM((tT, W), F32), pltpu.VMEM((tT, W), F32), pltpu.VMEM((8, W), F32)],
        compiler_params=_cparams(("arbitrary",)),
    )(proj, proj, proj, cw, cb, wax, bax, lam)


def _lru_bwd(proj, h, dy, cw, cb, wax, bax, lam, name):
    T = proj.shape[0]
    W = MIX_W
    tT, nT = _lru_tiles(T)
    hb = tT // 8

    def body(x_ref, xh_ref, gt_ref, h_ref, hh_ref, dy_ref, cw_ref, cb_ref, wax_ref, bax_ref, lam_ref,
             dx_ref, dgt_ref, dcw_ref, dcb_ref, dwax_ref, dbax_ref, dlam_ref,
             al_s, be_s, d_s, ca_s, cd_s, cx_s):
        i = pl.program_id(0)
        ib = nT - 1 - i

        @pl.when(i == 0)
        def _():
            for r in (ca_s, cd_s, cx_s, dcw_ref, dcb_ref, dwax_ref, dbax_ref, dlam_ref):
                r[...] = jnp.zeros_like(r)

        x = x_ref[...]
        halo = jnp.where(ib > 0, xh_ref[...], 0.0)
        w = cw_ref[...]
        x1, x2, x3 = _shift_dn(x, halo, 1), _shift_dn(x, halo, 2), _shift_dn(x, halo, 3)
        xc = cb_ref[...] + w[3:4] * x + w[2:3] * x1 + w[1:2] * x2 + w[0:1] * x3
        xcb = xc.astype(BF16)
        pre = jnp.dot(xcb, wax_ref[...], preferred_element_type=F32) + bax_ref[...]
        (a, _), pull_ab = jax.vjp(_lru_ab, xc, pre[:, :W], pre[:, W:], lam_ref[...])
        hv = h_ref[...]
        hprev = _shift_dn(hv, jnp.where(ib > 0, hh_ref[...], 0.0), 1)
        _, pull_y = jax.vjp(_gated, hv, gt_ref[...])
        dh_out, dgt = pull_y(dy_ref[...])
        dgt_ref[...] = dgt.astype(BF16)
        al_s[...] = _shift_up(a, ca_s[...], 1)
        be_s[...] = dh_out
        row = _row_iota8(W)
        ng = tT // 8

        def grp(k, dnext):
            r0 = pl.multiple_of((ng - 1 - k) * 8, 8)
            A = al_s[pl.ds(r0, 8), :]
            B = be_s[pl.ds(r0, 8), :]
            for s in (1, 2, 4):
                As = pltpu.roll(A, 8 - s, 0)
                Bs = pltpu.roll(B, 8 - s, 0)
                m = row < 8 - s
                B = jnp.where(m, A * Bs + B, B)
                A = jnp.where(m, A * As, A)
            Dg = A * dnext + B
            d_s[pl.ds(r0, 8), :] = Dg
            return Dg[0:1, :]

        lax.fori_loop(0, ng, grp, cd_s[0:1, :])
        Dv = d_s[...]
        dxc1, dpr, dpi, dlam = pull_ab((Dv * hprev, Dv))
        dpre = jnp.concatenate([dpr, dpi], axis=1)
        dpb = dpre.astype(BF16)
        dxc = dxc1 + lax.dot_general(dpb, wax_ref[...], (((1,), (1,)), ((), ())), preferred_element_type=F32)
        dwax_ref[...] += lax.dot_general(xcb, dpb, (((0,), (0,)), ((), ())), preferred_element_type=F32)
        dbax_ref[...] += jnp.sum(dpre, axis=0, keepdims=True)
        dlam_ref[...] += dlam
        dcb_ref[...] += jnp.sum(dxc, axis=0, keepdims=True)
        dcw_ref[...] += jnp.concatenate([jnp.sum(dxc * x3, axis=0, keepdims=True), jnp.sum(dxc * x2, axis=0, keepdims=True),
                                         jnp.sum(dxc * x1, axis=0, keepdims=True), jnp.sum(dxc * x, axis=0, keepdims=True)], axis=0)
        nxt = cx_s[...]
        dx = (w[3:4] * dxc + w[2:3] * _shift_up(dxc, nxt, 1) + w[1:2] * _shift_up(dxc, nxt, 2)
              + w[0:1] * _shift_up(dxc, nxt, 3))
        dx_ref[...] = dx.astype(BF16)
        ca_s[...] = a[0:8]
        cd_s[...] = Dv[0:8]
        cx_s[...] = dxc[0:8]

    rev = lambda i: nT - 1 - i
    const = lambda shape: pl.BlockSpec(shape, lambda i: (0, 0))
    return pl.pallas_call(
        body, name=name,
        out_shape=(jax.ShapeDtypeStruct((T, W), BF16), jax.ShapeDtypeStruct((T, W), BF16),
                   jax.ShapeDtypeStruct((CONV_W, W), F32), jax.ShapeDtypeStruct((1, W), F32),
                   jax.ShapeDtypeStruct((W, 2 * W), F32), jax.ShapeDtypeStruct((1, 2 * W), F32),
                   jax.ShapeDtypeStruct((1, W), F32)),
        grid=(nT,),
        in_specs=[pl.BlockSpec((tT, W), lambda i: (rev(i), 0)),
                  pl.BlockSpec((8, W), lambda i: (jnp.maximum(rev(i) * hb - 1, 0), 0)),
                  pl.BlockSpec((tT, W), lambda i: (rev(i), 1)),
                  pl.BlockSpec((tT, W), lambda i: (rev(i), 0)),
                  pl.BlockSpec((8, W), lambda i: (jnp.maximum(rev(i) * hb - 1, 0), 0)),
                  pl.BlockSpec((tT, W), lambda i: (rev(i), 0)),
                  const((CONV_W, W)), const((1, W)), const((W, 2 * W)), const((1, 2 * W)), const((1, W))],
        out_specs=(pl.BlockSpec((tT, W), lambda i: (rev(i), 0)), pl.BlockSpec((tT, W), lambda i: (rev(i), 0)),
                   const((CONV_W, W)), const((1, W)), const((W, 2 * W)), const((1, 2 * W)), const((1, W))),
        scratch_shapes=[pltpu.VMEM((tT, W), F32), pltpu.VMEM((tT, W), F32), pltpu.VMEM((tT, W), F32),
                        pltpu.VMEM((8, W), F32), pltpu.VMEM((8, W), F32), pltpu.VMEM((8, W), F32)],
        compiler_params=_cparams(("arbitrary",)),
    )(proj, proj, proj, h, h, dy, cw, cb, wax, bax, lam)


def _lru_prep(wa, wx, ba, bx):
    eye = jnp.eye(LRU_BLOCKS, dtype=F32)

    def dense(wb):
        return (wb[:, :, None, :] * eye[:, None, :, None]).reshape(MIX_W, MIX_W)

    wax = jnp.concatenate([dense(wa), dense(wx)], axis=1)
    bax = jnp.concatenate([ba, bx])[None, :]
    return wax, bax


_ATT_QC = 4
_ATT_Q = _ATT_QC * CHUNK
_ATT_KW = (ATT_LEFT + _ATT_QC) * CHUNK
_BVEC_W = _ATT_KW
_N_OFFS = CHUNK - 1 + ATT_BAND


def _attn_bias(rel_bias):
    n_far = ATT_LEFT * CHUNK - MAX_REL + CHUNK
    far = jnp.broadcast_to(rel_bias[:, 2 * MAX_REL:], (ATT_HEADS, n_far))
    near = rel_bias[:, MAX_REL - (CHUNK - 1):2 * MAX_REL][:, ::-1]
    pad = jnp.zeros((ATT_HEADS, _BVEC_W - _N_OFFS), F32)
    return jnp.concatenate([far, near, pad], axis=1)[:, None, :]


def _bias_table(bvec_row):
    return pltpu.roll(jnp.broadcast_to(bvec_row, (_ATT_Q, _BVEC_W)), _BVEC_W - (CHUNK - 1), 1, stride=1, stride_axis=0)


def _bias_table_t(ds):
    r = lax.broadcasted_iota(jnp.int32, (_ATT_Q, _ATT_Q), 0)
    c = lax.broadcasted_iota(jnp.int32, (_ATT_Q, _ATT_Q), 1)
    rev = jnp.dot((r + c == _ATT_Q - 1).astype(F32), ds, preferred_element_type=F32, precision=HI)
    back = pltpu.roll(rev, _BVEC_W - (_ATT_Q - CHUNK), 1, stride=1, stride_axis=0)
    return jnp.sum(back, axis=0, keepdims=True)


_NT = (((1,), (1,)), ((), ()))
_TN = (((0,), (0,)), ((), ()))
_ATT_PAD = ATT_LEFT * CHUNK
_Q_BLK, _K_BLK, _V_BLK = 2 * MIX_W // ATT_HD, 3 * MIX_W // ATT_HD, 4 * MIX_W // ATT_HD


def _in_band():
    first = (lax.broadcasted_iota(jnp.int32, (_ATT_Q, _ATT_KW), 0) // CHUNK) * CHUNK
    k = lax.broadcasted_iota(jnp.int32, (_ATT_Q, _ATT_KW), 1)
    return jnp.logical_and(k >= first, k < first + ATT_BAND)


def _attn_probs(q, kb, bias, in_band, b):
    s = lax.dot_general(q, kb, _NT, preferred_element_type=F32) * (ATT_HD ** -0.5) + bias
    kpos = lax.broadcasted_iota(jnp.int32, s.shape, 1)
    s = jnp.where(jnp.logical_and(in_band, kpos >= _ATT_PAD - b * _ATT_Q), s, MASK_VALUE)
    e = jnp.exp(s - jnp.max(s, axis=-1, keepdims=True))
    return e / jnp.sum(e, axis=-1, keepdims=True)


def _attn_fwd(proj, bias, name, side=None):
    T = proj.shape[0]
    assert T % _ATT_Q == 0
    nB = T // _ATT_Q

    def body(q_ref, k_ref, v_ref, b_ref, o_ref, kp, vp):
        kp[0:_ATT_PAD, :] = jnp.zeros((_ATT_PAD, ATT_HD), BF16)
        vp[0:_ATT_PAD, :] = jnp.zeros((_ATT_PAD, ATT_HD), BF16)
        kp[_ATT_PAD:, :] = k_ref[...].astype(BF16)
        vp[_ATT_PAD:, :] = v_ref[...].astype(BF16)
        bias_v = _bias_table(b_ref[0])
        band = _in_band()

        def step(b, carry):
            r0 = pl.multiple_of(b * _ATT_Q, _ATT_Q)
            q = q_ref[pl.ds(r0, _ATT_Q), :].astype(BF16)
            p = _attn_probs(q, kp[pl.ds(r0, _ATT_KW), :], bias_v, band, b)
            o = jnp.dot(p.astype(BF16), vp[pl.ds(r0, _ATT_KW), :], preferred_element_type=F32)
            o_ref[pl.ds(r0, _ATT_Q), :] = o.astype(BF16)
            return carry

        lax.fori_loop(0, nB, step, 0)

    (out,), got = _call(
        body, name=name, out_shape=[jax.ShapeDtypeStruct((T, MIX_W), BF16)], grid=(ATT_HEADS,),
        in_specs=[pl.BlockSpec((T, ATT_HD), lambda h: (0, _Q_BLK + h)), pl.BlockSpec((T, ATT_HD), lambda h: (0, _K_BLK + h)),
                  pl.BlockSpec((T, ATT_HD), lambda h: (0, _V_BLK + h)), pl.BlockSpec((1, 1, _BVEC_W), lambda h: (h, 0, 0))],
        out_specs=[pl.BlockSpec((T, ATT_HD), lambda h: (0, h))],
        scratch_shapes=[pltpu.VMEM((T + _ATT_PAD, ATT_HD), BF16), pltpu.VMEM((T + _ATT_PAD, ATT_HD), BF16)],
        sem=("arbitrary",), args=(proj, proj, proj, bias), side=side)
    return out if side is None else (out, got)


def _attn_bwd(proj, bias, do, name, side=None):
    T = proj.shape[0]
    assert T % _ATT_Q == 0
    nB = T // _ATT_Q

    def body(q_ref, k_ref, v_ref, b_ref, do_ref, dq_ref, dk_ref, dv_ref, db_ref, kp, vp, dkp, dvp, dbs):
        kp[0:_ATT_PAD, :] = jnp.zeros((_ATT_PAD, ATT_HD), BF16)
        vp[0:_ATT_PAD, :] = jnp.zeros((_ATT_PAD, ATT_HD), BF16)
        kp[_ATT_PAD:, :] = k_ref[...].astype(BF16)
        vp[_ATT_PAD:, :] = v_ref[...].astype(BF16)
        dkp[...] = jnp.zeros_like(dkp)
        dvp[...] = jnp.zeros_like(dvp)
        dbs[...] = jnp.zeros_like(dbs)
        bias_v = _bias_table(b_ref[0])
        band = _in_band()

        def step(b, carry):
            r0 = pl.multiple_of(b * _ATT_Q, _ATT_Q)
            q = q_ref[pl.ds(r0, _ATT_Q), :].astype(BF16)
            kb = kp[pl.ds(r0, _ATT_KW), :]
            vb = vp[pl.ds(r0, _ATT_KW), :]
            dob = do_ref[pl.ds(r0, _ATT_Q), :].astype(BF16)
            p = _attn_probs(q, kb, bias_v, band, b)
            dp = lax.dot_general(dob, vb, _NT, preferred_element_type=F32)
            ds = p * (dp - jnp.sum(p * dp, axis=-1, keepdims=True))
            dbs[...] += ds
            dsb = (ds * (ATT_HD ** -0.5)).astype(BF16)
            dq_ref[pl.ds(r0, _ATT_Q), :] = jnp.dot(dsb, kb, preferred_element_type=F32).astype(BF16)
            dkp[pl.ds(r0, _ATT_KW), :] += lax.dot_general(dsb, q, _TN, preferred_element_type=F32)
            dvp[pl.ds(r0, _ATT_KW), :] += lax.dot_general(p.astype(BF16), dob, _TN, preferred_element_type=F32)
            return carry

        lax.fori_loop(0, nB, step, 0)
        dk_ref[...] = dkp[_ATT_PAD:, :].astype(BF16)
        dv_ref[...] = dvp[_ATT_PAD:, :].astype(BF16)
        db_ref[0] = _bias_table_t(dbs[...])

    hspec = pl.BlockSpec((T, ATT_HD), lambda h: (0, h))
    osd = jax.ShapeDtypeStruct((T, MIX_W), BF16)
    outs, got = _call(
        body, name=name,
        out_shape=[osd, osd, osd, jax.ShapeDtypeStruct((ATT_HEADS, 1, _BVEC_W), F32)], grid=(ATT_HEADS,),
        in_specs=[pl.BlockSpec((T, ATT_HD), lambda h: (0, _Q_BLK + h)), pl.BlockSpec((T, ATT_HD), lambda h: (0, _K_BLK + h)),
                  pl.BlockSpec((T, ATT_HD), lambda h: (0, _V_BLK + h)), pl.BlockSpec((1, 1, _BVEC_W), lambda h: (h, 0, 0)),
                  hspec],
        out_specs=[hspec, hspec, hspec, pl.BlockSpec((1, 1, _BVEC_W), lambda h: (h, 0, 0))],
        scratch_shapes=[pltpu.VMEM((T + _ATT_PAD, ATT_HD), BF16), pltpu.VMEM((T + _ATT_PAD, ATT_HD), BF16),
                        pltpu.VMEM((T + _ATT_PAD, ATT_HD), F32), pltpu.VMEM((T + _ATT_PAD, ATT_HD), F32),
                        pltpu.VMEM((_ATT_Q, _ATT_KW), F32)],
        sem=("arbitrary",), args=(proj, proj, proj, bias, do), side=side)
    return outs if side is None else (outs, got)


def _s5_prep(a_re, a_im, b_re, b_im, c_re, c_im, d, log_step):
    step = jnp.exp(log_step)[:, None]
    mag = jnp.exp(a_re * step)
    ang = a_im * step
    lb_re = mag * jnp.cos(ang)
    lb_im = mag * jnp.sin(ang)
    den = a_re * a_re + a_im * a_im
    nr = lb_re - 1.0
    coef_re = (nr * a_re + lb_im * a_im) / den
    coef_im = (lb_im * a_re - nr * a_im) / den
    bb_re = coef_re[..., None] * b_re - coef_im[..., None] * b_im
    bb_im = coef_re[..., None] * b_im + coef_im[..., None] * b_re
    ks = jnp.arange(S5_L + 1, dtype=F32)[:, None, None]
    pmag = jnp.exp(ks * (a_re * step)[None])
    PR, PI = pmag * jnp.cos(ks * ang[None]), pmag * jnp.sin(ks * ang[None])
    cl_re = c_re[None] * PR[:, :, None, :] - c_im[None] * PI[:, :, None, :]
    cl_im = c_re[None] * PI[:, :, None, :] + c_im[None] * PR[:, :, None, :]
    cla_re = cl_re[:S5_L].transpose(1, 0, 2, 3).reshape(SSM_G, S5_LW, SSM_P)
    cla_im = cl_im[:S5_L].transpose(1, 0, 2, 3).reshape(SSM_G, S5_LW, SSM_P)
    bbt_re, bbt_im = bb_re.transpose(0, 2, 1), bb_im.transpose(0, 2, 1)
    prr, pir = PR[:S5_L][::-1], PI[:S5_L][::-1]
    bret = (prr[:, :, None, :] * bb_re.transpose(0, 2, 1)[None] - pir[:, :, None, :] * bb_im.transpose(0, 2, 1)[None])
    bimt = (prr[:, :, None, :] * bb_im.transpose(0, 2, 1)[None] + pir[:, :, None, :] * bb_re.transpose(0, 2, 1)[None])
    bret = bret.transpose(1, 0, 2, 3).reshape(SSM_G, S5_LW, SSM_P)
    bimt = bimt.transpose(1, 0, 2, 3).reshape(SSM_G, S5_LW, SSM_P)
    cre = cl_re[1:].transpose(1, 3, 0, 2).reshape(SSM_G, SSM_P, S5_LW)
    cim = (-cl_im[1:]).transpose(1, 3, 0, 2).reshape(SSM_G, SSM_P, S5_LW)
    dflat = jnp.broadcast_to(d.reshape(SSM_G, 1, SSM_H), (SSM_G, S5_L, SSM_H)).reshape(SSM_G, 1, S5_LW)
    bt = jnp.concatenate([bret, bimt], axis=2)
    ct = jnp.concatenate([cre, cim], axis=1)
    return cla_re, cla_im, bbt_re, bbt_im, bt, ct, PR[S5_L], PI[S5_L], dflat


_S5_GB = 8


def _bdot(a, b, dims):
    return lax.dot_general(a, b, dims, preferred_element_type=F32, precision=HI)


_B_NN = (((2,), (1,)), ((0,), (0,)))
_B_NT = (((2,), (2,)), ((0,), (0,)))
_B_TN = (((1,), (1,)), ((0,), (0,)))


def _gspec(shape):
    return pl.BlockSpec((_S5_GB,) + shape, lambda g: (g, 0, 0))


def _s5_tgt(cla_re, cla_im, bbt_re, bbt_im, name):
    n_g = cla_re.shape[0]

    def body(cr_ref, ci_ref, br_ref, bi_ref, t_ref):
        kt = _bdot(br_ref[...], cr_ref[...], _B_NT) - _bdot(bi_ref[...], ci_ref[...], _B_NT)
        lane = lax.broadcasted_iota(jnp.int32, (SSM_H, S5_LW), 1)
        for g in range(_S5_GB):
            for lp in range(S5_L):
                rows = kt[g] if lp == 0 else jnp.where(lane >= lp * SSM_H, pltpu.roll(kt[g], lp * SSM_H, 1), 0.0)
                t_ref[g, lp * SSM_H:(lp + 1) * SSM_H, :] = rows

    return pl.pallas_call(
        body, name=name, out_shape=jax.ShapeDtypeStruct((n_g, S5_LW, S5_LW), F32), grid=(n_g // _S5_GB,),
        in_specs=[_gspec((S5_LW, SSM_P)), _gspec((S5_LW, SSM_P)), _gspec((SSM_H, SSM_P)), _gspec((SSM_H, SSM_P))],
        out_specs=_gspec((S5_LW, S5_LW)), compiler_params=_cparams(("parallel",)),
    )(cla_re, cla_im, bbt_re, bbt_im)


def _s5_tgt_t(d_tgt, cla_re, cla_im, bbt_re, bbt_im, name):
    n_g = cla_re.shape[0]

    def body(dt_ref, cr_ref, ci_ref, br_ref, bi_ref, dcr_ref, dci_ref, dbr_ref, dbi_ref):
        lane = lax.broadcasted_iota(jnp.int32, (SSM_H, S5_LW), 1)
        dks = []
        for g in range(_S5_GB):
            dk = dt_ref[g, 0:SSM_H, :]
            for lp in range(1, S5_L):
                rows = dt_ref[g, lp * SSM_H:(lp + 1) * SSM_H, :]
                dk = dk + jnp.where(lane < S5_LW - lp * SSM_H, pltpu.roll(rows, S5_LW - lp * SSM_H, 1), 0.0)
            dks.append(dk)
        dkt = jnp.stack(dks)
        dbr_ref[...] = _bdot(dkt, cr_ref[...], _B_NN)
        dbi_ref[...] = -_bdot(dkt, ci_ref[...], _B_NN)
        dcr_ref[...] = _bdot(dkt, br_ref[...], _B_TN)
        dci_ref[...] = -_bdot(dkt, bi_ref[...], _B_TN)

    sd = jax.ShapeDtypeStruct
    return pl.pallas_call(
        body, name=name,
        out_shape=(sd((n_g, S5_LW, SSM_P), F32), sd((n_g, S5_LW, SSM_P), F32), sd((n_g, SSM_H, SSM_P), F32),
                   sd((n_g, SSM_H, SSM_P), F32)),
        grid=(n_g // _S5_GB,),
        in_specs=[_gspec((S5_LW, S5_LW)), _gspec((S5_LW, SSM_P)), _gspec((S5_LW, SSM_P)), _gspec((SSM_H, SSM_P)),
                  _gspec((SSM_H, SSM_P))],
        out_specs=(_gspec((S5_LW, SSM_P)), _gspec((S5_LW, SSM_P)), _gspec((SSM_H, SSM_P)), _gspec((SSM_H, SSM_P))),
        compiler_params=_cparams(("parallel",)),
    )(d_tgt, cla_re, cla_im, bbt_re, bbt_im)


_S5_SW = 2 * SSM_P


def _sspec(C):
    return pl.BlockSpec((C, _S5_GB, _S5_SW), lambda g: (0, g, 0))


def _group_major(s_ref):
    return jnp.stack([s_ref[:, k, :] for k in range(_S5_GB)])


def _chunk_major(o_ref, v):
    for k in range(_S5_GB):
        o_ref[:, k, :] = v[k]


def _rot_coefs(lr, li, sign):
    return jnp.concatenate([lr, lr], axis=1), jnp.concatenate([-sign * li, sign * li], axis=1)


def _s5_in(u, bt, name):
    C = u.shape[1]

    def body(u_ref, b_ref, s_ref):
        _chunk_major(s_ref, _bdot(u_ref[...], b_ref[...], _B_NN))

    return pl.pallas_call(
        body, name=name, out_shape=jax.ShapeDtypeStruct((C, SSM_G, _S5_SW), F32), grid=(SSM_G // _S5_GB,),
        in_specs=[_gspec((C, S5_LW)), _gspec((S5_LW, _S5_SW))], out_specs=_sspec(C),
        compiler_params=_cparams(("parallel",)),
    )(u, bt)


def _s5_scan(sin, lr, li, name):
    C = sin.shape[0]

    def body(i_ref, lr_ref, li_ref, o_ref):
        a, b = _rot_coefs(lr_ref[...], li_ref[...], 1.0)

        def step(c, s):
            o_ref[c] = s
            return a * s + b * pltpu.roll(s, SSM_P, 1) + i_ref[c]

        lax.fori_loop(0, C, step, jnp.zeros((SSM_G, _S5_SW), F32))

    return pl.pallas_call(body, name=name, out_shape=jax.ShapeDtypeStruct((C, SSM_G, _S5_SW), F32),
                          compiler_params=_cparams())(sin, lr, li)


def _s5_out(u, sp, tgt, ct, dflat, name):
    C = u.shape[1]

    def body(u_ref, s_ref, t_ref, c_ref, d_ref, pre_ref, y_ref):
        uv = u_ref[...]
        pre = _bdot(uv, t_ref[...], _B_NN) + _bdot(_group_major(s_ref), c_ref[...], _B_NN) + d_ref[...] * uv
        pre_ref[...] = pre
        y_ref[...] = jax.nn.gelu(pre)

    return pl.pallas_call(
        body, name=name,
        out_shape=(jax.ShapeDtypeStruct((SSM_G, C, S5_LW), F32), jax.ShapeDtypeStruct((SSM_G, C, S5_LW), F32)),
        grid=(SSM_G // _S5_GB,),
        in_specs=[_gspec((C, S5_LW)), _sspec(C), _gspec((S5_LW, S5_LW)), _gspec((_S5_SW, S5_LW)), _gspec((1, S5_LW))],
        out_specs=(_gspec((C, S5_LW)), _gspec((C, S5_LW))), compiler_params=_cparams(("parallel",)),
    )(u, sp, tgt, ct, dflat)


def _s5_bwd_out(dy, pre, u, sp, ct, name):
    C = u.shape[1]

    def body(dy_ref, pre_ref, u_ref, s_ref, c_ref, dpre_ref, ds_ref, dt_ref, dc_ref, dd_ref):
        _, pull = jax.vjp(jax.nn.gelu, pre_ref[...])
        dpre = pull(dy_ref[...])[0]
        uv = u_ref[...]
        dpre_ref[...] = dpre
        _chunk_major(ds_ref, _bdot(dpre, c_ref[...], _B_NT))
        dt_ref[...] = _bdot(uv, dpre, _B_TN)
        dc_ref[...] = _bdot(_group_major(s_ref), dpre, _B_TN)
        dd_ref[...] = jnp.sum(dpre * uv, axis=1, keepdims=True)

    sd = jax.ShapeDtypeStruct
    return pl.pallas_call(
        body, name=name,
        out_shape=(sd((SSM_G, C, S5_LW), F32), sd((C, SSM_G, _S5_SW), F32), sd((SSM_G, S5_LW, S5_LW), F32),
                   sd((SSM_G, _S5_SW, S5_LW), F32), sd((SSM_G, 1, S5_LW), F32)),
        grid=(SSM_G // _S5_GB,),
        in_specs=[_gspec((C, S5_LW)), _gspec((C, S5_LW)), _gspec((C, S5_LW)), _sspec(C), _gspec((_S5_SW, S5_LW))],
        out_specs=(_gspec((C, S5_LW)), _sspec(C), _gspec((S5_LW, S5_LW)), _gspec((_S5_SW, S5_LW)), _gspec((1, S5_LW))),
        compiler_params=_cparams(("parallel",)),
    )(dy, pre, u, sp, ct)


def _s5_rscan(dsp, sp, lr, li, name):
    C = dsp.shape[0]

    def body(g_ref, s_ref, lr_ref, li_ref, o_ref, dlr_ref, dli_ref):
        a, b = _rot_coefs(lr_ref[...], li_ref[...], -1.0)

        def step(k, carry):
            c = C - 1 - k
            d, acc_same, acc_swap = carry
            o_ref[c] = d
            s = s_ref[c]
            return (g_ref[c] + a * d + b * pltpu.roll(d, SSM_P, 1), acc_same + d * s,
                    acc_swap + d * pltpu.roll(s, SSM_P, 1))

        z = jnp.zeros((SSM_G, _S5_SW), F32)
        _, same, swap = lax.fori_loop(0, C, step, (z, z, z))
        dlr_ref[...] = same[:, :SSM_P] + same[:, SSM_P:]
        dli_ref[...] = swap[:, SSM_P:] - swap[:, :SSM_P]

    sd = jax.ShapeDtypeStruct((C, SSM_G, _S5_SW), F32)
    sp_ = jax.ShapeDtypeStruct((SSM_G, SSM_P), F32)
    return pl.pallas_call(body, name=name, out_shape=(sd, sp_, sp_), compiler_params=_cparams())(dsp, sp, lr, li)


def _s5_bwd_in(dpre, dsin, u, tgt, bt, dflat, name):
    C = u.shape[1]

    def body(dp_ref, ds_ref, u_ref, t_ref, b_ref, d_ref, du_ref, db_ref):
        dp = dp_ref[...]
        ds = _group_major(ds_ref)
        du_ref[...] = _bdot(dp, t_ref[...], _B_NT) + _bdot(ds, b_ref[...], _B_NT) + d_ref[...] * dp
        db_ref[...] = _bdot(u_ref[...], ds, _B_TN)

    sd = jax.ShapeDtypeStruct
    return pl.pallas_call(
        body, name=name,
        out_shape=(sd((SSM_G, C, S5_LW), F32), sd((SSM_G, S5_LW, _S5_SW), F32)),
        grid=(SSM_G // _S5_GB,),
        in_specs=[_gspec((C, S5_LW)), _sspec(C), _gspec((C, S5_LW)), _gspec((S5_LW, S5_LW)), _gspec((S5_LW, _S5_SW)),
                  _gspec((1, S5_LW))],
        out_specs=(_gspec((C, S5_LW)), _gspec((S5_LW, _S5_SW))),
        compiler_params=_cparams(("parallel",)),
    )(dpre, dsin, u, tgt, bt, dflat)


_LANES = 128
_GROUPS_PER_TILE = _LANES // SSM_H


def _to_chunks(v, col0, name):
    T = v.shape[0]
    C = T // S5_L

    def body(x_ref, o_ref):
        for l in range(S5_L):
            xl = x_ref[pl.ds(l, C, stride=S5_L), :]
            for k in range(_GROUPS_PER_TILE):
                o_ref[k, :, l * SSM_H:(l + 1) * SSM_H] = xl[:, k * SSM_H:(k + 1) * SSM_H]

    return pl.pallas_call(
        body, name=name, out_shape=jax.ShapeDtypeStruct((SSM_G, C, S5_LW), F32), grid=(SSM_G // _GROUPS_PER_TILE,),
        in_specs=[pl.BlockSpec((T, _LANES), lambda g: (0, col0 // _LANES + g))],
        out_specs=pl.BlockSpec((_GROUPS_PER_TILE, C, S5_LW), lambda g: (g, 0, 0)),
        compiler_params=_cparams(("parallel",)),
    )(v)


def _from_chunks(v, out_dtype, name):
    C = v.shape[1]
    T = C * S5_L

    def body(f_ref, o_ref, rows):
        for l in range(S5_L):
            rows[pl.ds(l, C, stride=S5_L), :] = jnp.concatenate(
                [f_ref[k, :, l * SSM_H:(l + 1) * SSM_H] for k in range(_GROUPS_PER_TILE)], axis=1)
        o_ref[...] = rows[...].astype(out_dtype)

    return pl.pallas_call(
        body, name=name, out_shape=jax.ShapeDtypeStruct((T, MIX_W), out_dtype), grid=(SSM_G // _GROUPS_PER_TILE,),
        in_specs=[pl.BlockSpec((_GROUPS_PER_TILE, C, S5_LW), lambda g: (g, 0, 0))],
        out_specs=pl.BlockSpec((T, _LANES), lambda g: (0, g)),
        scratch_shapes=[pltpu.VMEM((T, _LANES), F32)],
        compiler_params=_cparams(("parallel",)),
    )(v)


def _merge_fn(bra, brb, pc, pg, g0, g1, g2, b0, b1, b2):
    sg = jax.nn.sigmoid
    return sg(g0 + b0) * bra + sg(g1 + b1) * brb + sg(g2 + b2) * (pc * sg(pg))


_EW_CS = 256
_GATE_OFF = 6 * MIX_W


def _merge_rows(br4, proj):
    return [(b, 0, D_MODEL) for b in br4] + [(proj, _GATE_OFF + k * D_MODEL, D_MODEL) for k in range(3)]


def _merge_fwd(br4, proj, gb3, name):
    def fn(rp, cp):
        b = cp[0]
        return [_merge_fn(*rp, b[0:1], b[1:2], b[2:3])], []

    return _rowwise(fn, _merge_rows(br4, proj), [gb3], [(D_MODEL, BF16)], [], R=proj.shape[0], name=name, cs=_EW_CS)[0]


def _merge_bwd(br4, proj, gb3, dm, name):
    def fn(rp, cp):
        b = cp[0]
        shp = rp[0].shape
        bs = [jnp.broadcast_to(b[k:k + 1], shp) for k in range(3)]
        _, pull = jax.vjp(_merge_fn, *rp[:7], *bs)
        g = pull(rp[7])
        return list(g[:7]), list(g[7:])

    rows = _merge_rows(br4, proj) + [(dm, 0, D_MODEL)]
    outs = _rowwise(fn, rows, [gb3], [(D_MODEL, BF16)] * 7, [D_MODEL] * 3, R=proj.shape[0], name=name, tm=128, cs=_EW_CS)
    return outs[:4], outs[4:7], jnp.concatenate(outs[7:], axis=0)


def _swiglu(g, u):
    return jax.nn.silu(g) * u


def _act_fwd(gu, name):
    def fn(rp, cp):
        return [_swiglu(*rp)], []

    return _rowwise(fn, [(gu[0], 0, FFN_H), (gu[1], 0, FFN_H)], [], [(FFN_H, BF16)], [], R=gu[0].shape[0], name=name, cs=_EW_CS)[0]


def _act_bwd(gu, dact, name):
    def fn(rp, cp):
        _, pull = jax.vjp(_swiglu, rp[0], rp[1])
        return list(pull(rp[2])), []

    return _rowwise(fn, [(gu[0], 0, FFN_H), (gu[1], 0, FFN_H), (dact, 0, FFN_H)], [], [(FFN_H, BF16)] * 2, [],
                    R=dact.shape[0], name=name, tm=128, cs=_EW_CS)


def _adamw_fn(w, g, m, v):
    m = ADAM_B1 * m + (1.0 - ADAM_B1) * g
    v = ADAM_B2 * v + (1.0 - ADAM_B2) * jnp.square(g)
    m_hat = m / (1.0 - ADAM_B1 ** ADAM_STEP)
    v_hat = v / (1.0 - ADAM_B2 ** ADAM_STEP)
    delta = -ADAM_LR * (m_hat / (jnp.sqrt(v_hat) + ADAM_EPS) + ADAM_WD * w)
    return delta, m, v


def _adamw(w, g, m, v, name):
    R, C = w.shape

    def fn(rp, cp):
        return list(_adamw_fn(*rp)), []

    cs = _pick(C, (512, 256, 128))
    tm = _pick(R, (256, 128, 64, 32, 16, 8))
    return _rowwise(fn, [(w, 0, C), (g, 0, C), (m, 0, C), (v, 0, C)], [], [(C, F32)] * 3, [], R=R, name=name,
                    tm=tm, rs=8, cs=cs)


def _my_place():
    return lax.axis_index("x"), lax.axis_index("y"), lax.axis_index("c")


def _other_chips(x, y):
    return [(1 - x, y), (x, 1 - y), (1 - x, 1 - y)]


_ANY = pl.BlockSpec(memory_space=pl.ANY)


def _rcopy(src, dst, ssem, rsem, to):
    return pltpu.make_async_remote_copy(src_ref=src, dst_ref=dst, send_sem=ssem, recv_sem=rsem, device_id=to,
                                        device_id_type=MESH)


def _place_shard(local, axis, jidx, out_dtype, name):
    lead, r, c = local.shape
    shp = [lead, r, c]
    shp[axis] *= N_CHIPS
    tr = _pick(r, (512, 256, 128)) if r % 128 == 0 else r
    nr = r // tr
    omap = (lambda l, i, j: (l, i, j[0])) if axis == 2 else (lambda l, i, j: (l, j[0] * nr + i, 0))

    def body(j_ref, x_ref, o_ref):
        o_ref[...] = x_ref[...].astype(out_dtype)

    return pl.pallas_call(
        body, name=name, out_shape=jax.ShapeDtypeStruct(tuple(shp), out_dtype),
        grid_spec=pltpu.PrefetchScalarGridSpec(
            num_scalar_prefetch=1, grid=(lead, nr),
            in_specs=[pl.BlockSpec((None, tr, c), lambda l, i, j: (l, i, 0))],
            out_specs=pl.BlockSpec((None, tr, c), omap)),
        compiler_params=_cparams(("parallel", "parallel")),
    )(jidx, local)


class _GatherSide:
    def __init__(self, fulls, axes, regions):
        self.inputs = list(fulls)
        self.out_shapes = [jax.ShapeDtypeStruct(f.shape, f.dtype) for f in fulls]
        self.aliases = {t: t for t in range(len(fulls))}
        self.n_sems = 6 * len(fulls)
        self.axes, self.regions = list(axes), list(regions)

    def _block(self, outs, t, chip, half):
        start, size, split = self.regions[t][:3]
        piece, n_pieces = self.regions[t][3] if len(self.regions[t]) > 3 else (0, 1)
        ax = self.axes[t]
        cut = outs[t].shape[ax] // N_CHIPS
        j = 2 * chip[0] + chip[1]
        idx = [pl.ds(start, size), slice(None), slice(None)]
        idx[ax] = pl.ds(j * cut, cut)
        if split == "lead":
            idx[0] = pl.ds(start + half * (size // 2), size // 2)
        else:
            other = 3 - ax
            h = outs[t].shape[other] // (2 * n_pieces)
            idx[other] = pl.ds((half * n_pieces + piece) * h, h)
        return outs[t].at[tuple(idx)]

    def _sends(self, outs, send, recv):
        x, y, c = _my_place()
        cps = []
        for t in range(len(outs)):
            mine = self._block(outs, t, (x, y), c)
            for r, chip in enumerate(_other_chips(x, y)):
                k = 3 * t + r
                cps.append(_rcopy(mine, mine, send.at[k], recv.at[k], (*chip, c)))
        return cps

    def start(self, ins, outs, send, recv):
        for cp in self._sends(outs, send, recv):
            cp.start()

    def finish(self, ins, outs, send, recv):
        x, y, c = _my_place()
        sib = (x, y, 1 - c)
        n = len(outs)
        chips = _other_chips(x, y)
        passed = []
        for t in range(n):
            for r, chip in enumerate(chips):
                k = 3 * t + r
                landed = self._block(outs, t, chip, c)
                _rcopy(landed, landed, send.at[k], recv.at[k], (*chip, c)).wait_recv()
                cp = _rcopy(landed, landed, send.at[3 * n + k], recv.at[3 * n + k], sib)
                cp.start()
                passed.append(cp)
        for t in range(n):
            for r, chip in enumerate(chips):
                k = 3 * n + 3 * t + r
                theirs = self._block(outs, t, chip, 1 - c)
                _rcopy(theirs, theirs, send.at[k], recv.at[k], sib).wait_recv()
        for cp in self._sends(outs, send, recv) + passed:
            cp.wait_send()


def _run_side(side, name):
    s_in = len(side.inputs)

    def body(*refs):
        ins, outs = refs[:s_in], refs[s_in:s_in + len(side.out_shapes)]
        send, recv = refs[s_in + len(side.out_shapes):]
        side.start(ins, outs, send, recv)
        side.finish(ins, outs, send, recv)

    return pl.pallas_call(
        body, name=name, out_shape=tuple(side.out_shapes), in_specs=[_ANY] * s_in,
        out_specs=tuple([_ANY] * len(side.out_shapes)), input_output_aliases=dict(side.aliases),
        scratch_shapes=[pltpu.SemaphoreType.DMA((side.n_sems,)), pltpu.SemaphoreType.DMA((side.n_sems,))],
    )(*side.inputs)


def _half_idx(shape, axis, half):
    size = shape[axis] // 2
    idx = [slice(None), slice(None)]
    idx[axis] = pl.ds(half * size, size)
    return tuple(idx)


class _PairSide:
    def __init__(self, grads, half_axes):
        self.inputs = list(grads)
        self.half_axes = list(half_axes)
        self.out_shapes = []
        for g, ax in zip(grads, half_axes):
            shp = list(g.shape)
            shp[ax] //= 2
            self.out_shapes.append(jax.ShapeDtypeStruct(tuple(shp), g.dtype))
        self.aliases = {}
        self.n_sems = len(grads)

    def _copies(self, srcs, outs, send, recv):
        x, y, c = _my_place()
        return [_rcopy(srcs[t].at[_half_idx(srcs[t].shape, self.half_axes[t], 1 - c)], outs[t], send.at[t], recv.at[t],
                       (x, y, 1 - c)) for t in range(len(srcs))]

    def start(self, srcs, outs, send, recv):
        for cp in self._copies(srcs, outs, send, recv):
            cp.start()

    def finish(self, srcs, outs, send, recv):
        for cp in self._copies(srcs, outs, send, recv):
            cp.wait()


def _pair_sum(g, recv, half_axis, cidx, name):
    K, N = recv.shape
    tm = _pick(K, (256, 128, 64, 32, 16))
    tn = _pick(N, (1024, 1408, 512, 256, 128))
    nbr, nbc = K // tm, N // tn
    if half_axis == 0:
        gmap = lambda i, j, c: (c[0] * nbr + i, j)
    else:
        gmap = lambda i, j, c: (i, c[0] * nbc + j)

    def body(c_ref, g_ref, r_ref, of_ref, ob_ref):
        s = g_ref[...] + r_ref[...]
        of_ref[...] = s
        ob_ref[...] = s.astype(BF16)

    omap = lambda i, j, c: (i, j)
    return pl.pallas_call(
        body, name=name,
        out_shape=(jax.ShapeDtypeStruct((K, N), F32), jax.ShapeDtypeStruct((K, N), BF16)),
        grid_spec=pltpu.PrefetchScalarGridSpec(
            num_scalar_prefetch=1, grid=(nbr, nbc),
            in_specs=[pl.BlockSpec((tm, tn), gmap), pl.BlockSpec((tm, tn), omap)],
            out_specs=(pl.BlockSpec((tm, tn), omap), pl.BlockSpec((tm, tn), omap))),
        compiler_params=_cparams(("parallel", "parallel")),
    )(cidx, g, recv)


def _shard_idx(shape, axis, j):
    size = shape[axis] // N_CHIPS
    idx = [slice(None), slice(None)]
    idx[axis] = pl.ds(j * size, size)
    return tuple(idx)


class _ScatterSide:
    def __init__(self, parts, shard_axes):
        self.inputs = list(parts)
        self.shard_axes = list(shard_axes)
        self.out_shapes = []
        for p, ax in zip(parts, shard_axes):
            shp = list(p.shape)
            shp[ax] //= N_CHIPS
            self.out_shapes.append(jax.ShapeDtypeStruct((3,) + tuple(shp), p.dtype))
        self.aliases = {}
        self.n_sems = 3 * len(parts)

    def _copies(self, srcs, outs, send, recv):
        x, y, c = _my_place()
        cps = []
        for t in range(len(srcs)):
            for r, chip in enumerate(_other_chips(x, y)):
                k = 3 * t + r
                j = 2 * chip[0] + chip[1]
                cps.append(_rcopy(srcs[t].at[_shard_idx(srcs[t].shape, self.shard_axes[t], j)], outs[t].at[r],
                                  send.at[k], recv.at[k], (*chip, c)))
        return cps

    def start(self, srcs, outs, send, recv):
        for cp in self._copies(srcs, outs, send, recv):
            cp.start()

    def finish(self, srcs, outs, send, recv):
        for cp in self._copies(srcs, outs, send, recv):
            cp.wait()


def _shard_sum(pf, recv, acc, layer, shard_axis, jcidx, name):
    _, K, N = recv.shape
    tm = _pick(K, (256, 128, 64, 32, 16))
    tn = _pick(N, (1024, 1408, 512, 256, 128))
    nbr, nbc = K // tm, N // tn
    if shard_axis == 0:
        pmap = lambda i, j, s: (s[0] * nbr + i, j)
        omap = lambda i, j, s: (layer, i, s[1] * nbc + j)
    else:
        pmap = lambda i, j, s: (i, s[0] * nbc + j)
        omap = lambda i, j, s: (layer, s[1] * nbr + i, j)

    def body(j_ref, p_ref, r_ref, a_ref, o_ref):
        o_ref[...] = ((p_ref[...] + r_ref[0].astype(F32)) + r_ref[1].astype(F32)) + r_ref[2].astype(F32)

    return pl.pallas_call(
        body, name=name, out_shape=jax.ShapeDtypeStruct(acc.shape, F32),
        grid_spec=pltpu.PrefetchScalarGridSpec(
            num_scalar_prefetch=1, grid=(nbr, nbc),
            in_specs=[pl.BlockSpec((tm, tn), pmap), pl.BlockSpec((3, tm, tn), lambda i, j, s: (0, i, j)), _ANY],
            out_specs=pl.BlockSpec((None, tm, tn), omap)),
        input_output_aliases={3: 0},
        compiler_params=_cparams(("parallel", "parallel")),
    )(jcidx, pf, recv, acc)


def _pair_join(accs, half_axes, name):
    n = len(accs)

    def body(*refs):
        outs = refs[n:2 * n]
        send_sems, recv_sems = refs[2 * n:]
        x, y, c = _my_place()
        sib = (x, y, 1 - c)

        def half(t, hc):
            return outs[t].at[(slice(None),) + _half_idx(outs[t].shape[1:], half_axes[t], hc)]

        cps = []
        for t in range(n):
            cp = _rcopy(half(t, c), half(t, c), send_sems.at[t], recv_sems.at[t], sib)
            cp.start()
            cps.append(cp)
        for t in range(n):
            _rcopy(half(t, 1 - c), half(t, 1 - c), send_sems.at[t], recv_sems.at[t], sib).wait_recv()
        for cp in cps:
            cp.wait_send()

    return pl.pallas_call(
        body, name=name, out_shape=tuple(jax.ShapeDtypeStruct(a.shape, a.dtype) for a in accs),
        in_specs=[_ANY] * n, out_specs=tuple([_ANY] * n), input_output_aliases={t: t for t in range(n)},
        scratch_shapes=[pltpu.SemaphoreType.DMA((n,)), pltpu.SemaphoreType.DMA((n,))],
    )(*accs)


_N_DEV = 8


def _allreduce_small(flat, name):
    _, R, _ = flat.shape

    def body(in_ref, out_ref, stage, send1, recv1, send2, recv2):
        x, y, c = _my_place()
        me = 4 * x + 2 * y + c
        places = [(px, py, pc) for px in range(2) for py in range(2) for pc in range(2)]
        def peer(r):
            return (x ^ (r >> 2), y ^ ((r >> 1) & 1), c ^ (r & 1))

        def peer_id(r):
            p = peer(r)
            return 4 * p[0] + 2 * p[1] + p[2]

        stage[0] = in_ref[me]
        cps = []
        for r in range(1, _N_DEV):
            cp = _rcopy(in_ref.at[peer_id(r)], stage.at[r], send1.at[r], recv1.at[r], peer(r))
            cp.start()
            cps.append(cp)
        for cp in cps:
            cp.wait()
        tot = jnp.zeros((R, 128), F32)
        for d in range(_N_DEV):
            tot = tot + stage[me ^ d]
        out_ref[me] = tot
        cps = []
        for r in range(1, _N_DEV):
            cp = _rcopy(out_ref.at[me], out_ref.at[me], send2.at[r], recv2.at[r], peer(r))
            cp.start()
            cps.append(cp)
        for r in range(1, _N_DEV):
            _rcopy(out_ref.at[peer_id(r)], out_ref.at[peer_id(r)], send2.at[r], recv2.at[r], peer(r)).wait_recv()
        for cp in cps:
            cp.wait_send()

    vm = pl.BlockSpec(memory_space=pltpu.VMEM)
    return pl.pallas_call(
        body, name=name, out_shape=jax.ShapeDtypeStruct(flat.shape, F32), in_specs=[vm], out_specs=vm,
        scratch_shapes=[pltpu.VMEM(flat.shape, F32)] + [pltpu.SemaphoreType.DMA((_N_DEV,))] * 4,
        compiler_params=_cparams(),
    )(flat)


_BIG = ("w_in", "ssm_w_glu", "w_branch", "w_out", "w_ffn_gate", "w_ffn_up", "w_ffn_down")
_BIG_SHARD_AXIS = {"w_in": 1, "ssm_w_glu": 1, "w_branch": 1, "w_out": 0, "w_ffn_gate": 1, "w_ffn_up": 1, "w_ffn_down": 0}
_SMALL = ("norm_mix_g", "gate_bias", "lru_conv_w", "lru_conv_b", "lru_wa", "lru_ba", "lru_wx", "lru_bx", "lru_lambda",
          "attn_rel_bias", "ssm_a_re", "ssm_a_im", "ssm_b_re", "ssm_b_im", "ssm_c_re", "ssm_c_im", "ssm_d",
          "ssm_log_step", "norm_ffn_g", "norm_final_g")
_SMALL_SHARDED = {"gate_bias": 2, "lru_conv_w": 2}
_WEIGHTS = ("norm_mix_g", "w_in", "gate_bias", "lru_conv_w", "lru_conv_b", "lru_wa", "lru_ba", "lru_wx", "lru_bx",
            "lru_lambda", "attn_rel_bias", "ssm_a_re", "ssm_a_im", "ssm_b_re", "ssm_b_im", "ssm_c_re", "ssm_c_im",
            "ssm_d", "ssm_log_step", "ssm_w_glu", "w_branch", "w_out", "norm_ffn_g", "w_ffn_gate", "w_ffn_up",
            "w_ffn_down", "norm_final_g")


def _carried(comm, phase, l, key, W, fn, *args, **kw):
    side = comm.side(phase, l, key, W)
    if side is None:
        return fn(*args, **kw)
    out, got = fn(*args, side=side, **kw)
    comm.took(phase, l, key, got, W)
    return out


def _layer_fwd(l, x, W, sm, comm):
    T = x.shape[0]
    nm = lambda s: f"{s}"
    h1 = _rms_fwd(x, sm["norm_mix_g"][l][None, :], nm("rms_fwd"))
    proj = _carried(comm, "fwd", l, "mm_in", W, _mm, h1, W["w_in"], M=T, N=IN_W, K=D_MODEL, b_lead=l, name=nm("mm_in"))
    wax, bax = sm["lru_prep"][l]
    cw, cb, lam = sm["lru_conv_w"][l], sm["lru_conv_b"][l][None, :], sm["lru_lambda"][l][None, :]
    y_a, hst = _lru_fwd(proj, cw, cb, wax, bax, lam, nm("lru_fwd"))
    bias = sm["attn_bias"][l]
    y_b = _carried(comm, "fwd", l, "attn_fwd", W, _attn_fwd, proj, bias, nm("attn_fwd"))
    tgt, bt, ct, lr, li, dflat = sm["s5_prep"][l]
    u = _to_chunks(proj, 5 * MIX_W, nm("to_chunks"))
    sp = _s5_scan(_s5_in(u, bt, nm("s5_in")), lr, li, nm("s5_scan"))
    pre, ycf = _s5_out(u, sp, tgt, ct, dflat, nm("s5_out"))
    y_c = _from_chunks(ycf, BF16, nm("from_chunks"))
    brs = []
    for k, yk in enumerate((y_a, y_b, y_c)):
        brs.append(_mm(yk, W["w_branch"], M=T, N=D_MODEL, K=MIX_W, b_lead=3 * l + k, out_dtype=BF16, name=nm("mm_branch")))
    brs.append(_mm(y_c, W["ssm_w_glu"], M=T, N=D_MODEL, K=MIX_W, b_lead=l, out_dtype=BF16, name=nm("mm_branch")))
    br4 = tuple(brs)
    gb3 = sm["gate_bias"][l]
    merged = _merge_fwd(br4, proj, gb3, nm("merge_fwd"))
    x1 = _mm(merged, W["w_out"], M=T, N=D_MODEL, K=D_MODEL, b_lead=l, res=x, name=nm("mm_out"))
    h2 = _rms_fwd(x1, sm["norm_ffn_g"][l][None, :], nm("rms_fwd"))
    gpre = _carried(comm, "fwd", l, "mm_ffn_gate", W, _mm, h2, W["w_ffn_gate"], M=T, N=FFN_H, K=D_MODEL, b_lead=l,
                    out_dtype=BF16, name=nm("mm_ffn_up"))
    upre = _carried(comm, "fwd", l, "mm_ffn_up", W, _mm, h2, W["w_ffn_up"], M=T, N=FFN_H, K=D_MODEL, b_lead=l,
                    out_dtype=BF16, name=nm("mm_ffn_up"))
    gu = (gpre, upre)
    act = _act_fwd(gu, nm("act_fwd"))
    x2 = _carried(comm, "fwd", l, "mm_down", W, _mm, act, W["w_ffn_down"], M=T, N=D_MODEL, K=FFN_H, b_lead=l, res=x1,
                  name=nm("mm_down"))
    saved = dict(x=x, h1=h1, proj=proj, hst=hst, y_a=y_a, y_b=y_b, y_c=y_c, u=u, sp=sp, pre=pre,
                 br4=br4, merged=merged, x1=x1, h2=h2, gu=gu, act=act)
    return x2, saved


def _layer_bwd(l, dx2, sv, W, sm, comm):
    T = dx2.shape[0]
    nm = lambda s: f"{s}"
    big, small = {}, {}
    dxb = dx2.astype(BF16)
    big["w_ffn_down"] = _carried(comm, "bwd", l, "mm_dw_down", W, _mm, sv["act"], dxb, M=FFN_H, N=D_MODEL, K=T, ta=True,
                                 name=nm("mm_dw_down"))
    dact = _carried(comm, "bwd", l, "mm_dact", W, _mm, dxb, W["w_ffn_down"], M=T, N=FFN_H, K=D_MODEL, tb=True, b_lead=l,
                    out_dtype=BF16, name=nm("mm_dact"))
    dg, du = _act_bwd(sv["gu"], dact, nm("act_bwd"))
    big["w_ffn_gate"] = _carried(comm, "bwd", l, "mm_dw_gate", W, _mm, sv["h2"], dg, M=D_MODEL, N=FFN_H, K=T, ta=True,
                                 name=nm("mm_dw_up"))
    big["w_ffn_up"] = _mm(sv["h2"], du, M=D_MODEL, N=FFN_H, K=T, ta=True, name=nm("mm_dw_up"))
    dh2 = _mm(dg, W["w_ffn_gate"], M=T, N=D_MODEL, K=FFN_H, tb=True, b_lead=l, name=nm("mm_dh2"))
    dh2 = _mm(du, W["w_ffn_up"], M=T, N=D_MODEL, K=FFN_H, tb=True, b_lead=l, res=dh2, out_dtype=BF16, name=nm("mm_dh2r"))
    dx1, dgn = _rms_bwd(sv["x1"], sm["norm_ffn_g"][l][None, :], dh2, dx2, nm("rms_bwd"))
    small["norm_ffn_g"] = dgn[0]
    dx1b = dx1.astype(BF16)
    big["w_out"] = _mm(sv["merged"], dx1b, M=D_MODEL, N=D_MODEL, K=T, ta=True, name=nm("mm_dw_out"))
    dm = _mm(dx1b, W["w_out"], M=T, N=D_MODEL, K=D_MODEL, tb=True, b_lead=l, out_dtype=BF16, name=nm("mm_dmerged"))
    dbr, dgates, dgb = _merge_bwd(sv["br4"], sv["proj"], sm["gate_bias"][l], dm, nm("merge_bwd"))
    small["gate_bias"] = dgb
    ys = (sv["y_a"], sv["y_b"], sv["y_c"])
    big["w_branch"] = [_mm(ys[k], dbr[k], M=MIX_W, N=D_MODEL, K=T, ta=True, name=nm("mm_dw_branch")) for k in range(3)]
    big["ssm_w_glu"] = _mm(sv["y_c"], dbr[3], M=MIX_W, N=D_MODEL, K=T, ta=True, name=nm("mm_dw_branch"))
    dya = _mm(dbr[0], W["w_branch"], M=T, N=MIX_W, K=D_MODEL, tb=True, b_lead=3 * l, name=nm("mm_dy"))
    dyb = _mm(dbr[1], W["w_branch"], M=T, N=MIX_W, K=D_MODEL, tb=True, b_lead=3 * l + 1, out_dtype=BF16, name=nm("mm_dy"))
    dyc = _mm(dbr[2], W["w_branch"], M=T, N=MIX_W, K=D_MODEL, tb=True, b_lead=3 * l + 2, name=nm("mm_dy"))
    dyc = _mm(dbr[3], W["ssm_w_glu"], M=T, N=MIX_W, K=D_MODEL, tb=True, b_lead=l, res=dyc, name=nm("mm_dyr"))
    tgt, bt, ct, lr, li, dflat = sm["s5_prep"][l]
    dpre, dsp, d_tgt, d_ct, d_dflat = _s5_bwd_out(_to_chunks(dyc, 0, nm("to_chunks")), sv["pre"], sv["u"], sv["sp"], ct,
                                                  nm("s5_bwd_out"))
    dsin, d_lr, d_li = _s5_rscan(dsp, sv["sp"], lr, li, nm("s5_rscan"))
    du_f, d_bt = _s5_bwd_in(dpre, dsin, sv["u"], tgt, bt, dflat, nm("s5_bwd_in"))
    d_u = _from_chunks(du_f, BF16, nm("from_chunks"))
    small["s5_tables"] = (d_tgt, d_bt, d_ct, d_lr, d_li, d_dflat)
    dq, dk, dv, dbias = _carried(comm, "bwd", l, "attn_bwd", W, _attn_bwd, sv["proj"], sm["attn_bias"][l], dyb, nm("attn_bwd"))
    small["attn_bias"] = dbias
    wax, bax = sm["lru_prep"][l]
    cw, cb, lam = sm["lru_conv_w"][l], sm["lru_conv_b"][l][None, :], sm["lru_lambda"][l][None, :]
    d_lx, d_lg, d_cw, d_cb, d_wax, d_bax, d_lam = _lru_bwd(sv["proj"], sv["hst"], dya, cw, cb, wax, bax, lam, nm("lru_bwd"))
    small["lru_conv_w"], small["lru_conv_b"], small["lru_lambda"] = d_cw, d_cb[0], d_lam[0]
    small["lru_tables"] = (d_wax, d_bax)
    dproj = jnp.concatenate([d_lx, d_lg, dq, dk, dv, d_u] + list(dgates), axis=1)
    big["w_in"] = _carried(comm, "bwd", l, "mm_dw_in", W, _mm, sv["h1"], dproj, M=D_MODEL, N=IN_W, K=T, ta=True,
                           name=nm("mm_dw_in"))
    dh1 = _carried(comm, "bwd", l, "mm_dh1", W, _mm, dproj, W["w_in"], M=T, N=D_MODEL, K=IN_W, tb=True, b_lead=l,
                   out_dtype=BF16, name=nm("mm_dh1"))
    dx, dgn = _rms_bwd(sv["x"], sm["norm_mix_g"][l][None, :], dh1, dx1, nm("rms_bwd"))
    small["norm_mix_g"] = dgn[0]
    return dx, big, small


_TENSORS = tuple((n, k) for n in _BIG for k in range(N_BRANCH if n == "w_branch" else 1))
_FWD_CARRIERS = {"mm_in": (("w_out", "w_branch", "ssm_w_glu", "w_ffn_gate"), 0), "attn_fwd": (("w_ffn_up",), 0),
                 "mm_ffn_gate": (("w_ffn_down",), 0), "mm_ffn_up": (("w_in",), 1), "mm_down": (("w_in",), 1)}
_FWD_PIECES = {"mm_ffn_up": (0, 2), "mm_down": (1, 2)}
_BWD_PAIR_CARRIERS = {"mm_dw_down": ("w_in", "w_out", "ssm_w_glu"), "mm_dact": ("w_ffn_gate", "w_ffn_up", "w_ffn_down", "w_branch")}
_BWD_CARRIERS = {"mm_dw_gate": ("w_out", "w_branch", "ssm_w_glu"), "attn_bwd": ("w_ffn_down",), "mm_dw_in": ("w_in",),
                 "mm_dh1": ("w_ffn_gate", "w_ffn_up")}


class _StepComm:
    def __init__(self, depth, cidx, jcidx):
        self.depth, self.cidx, self.jcidx = depth, cidx, jcidx
        self.accs = {}
        self.raw = None
        self.paired = {}
        self.pending = None

    def gather_side(self, W, names, l, piece):
        fulls, axes, regions = [], [], []
        for n in names:
            per = N_BRANCH if n == "w_branch" else 1
            fulls.append(W[n])
            axes.append(_BIG_SHARD_AXIS[n] + 1)
            regions.append((per * l, per, "other", piece))
        return _GatherSide(fulls, axes, regions)

    def side(self, phase, l, key, W):
        if phase == "fwd":
            if key not in _FWD_CARRIERS or l + _FWD_CARRIERS[key][1] >= self.depth:
                return None
            names, ahead = _FWD_CARRIERS[key]
            return self.gather_side(W, names, l + ahead, _FWD_PIECES.get(key, (0, 1)))
        if key in _BWD_PAIR_CARRIERS and self.raw is not None:
            nks = [nk for nk in _TENSORS if nk[0] in _BWD_PAIR_CARRIERS[key]]
            return _PairSide([self.raw[1][nk] for nk in nks], [1 - _BIG_SHARD_AXIS[nk[0]] for nk in nks])
        if key in _BWD_CARRIERS and self.pending is not None:
            nks = [nk for nk in _TENSORS if nk[0] in _BWD_CARRIERS[key]]
            return _ScatterSide([self.pending[1][nk][1] for nk in nks], [_BIG_SHARD_AXIS[nk[0]] for nk in nks])
        return None

    def took(self, phase, l, key, got, W):
        if phase == "fwd":
            W.update(zip(_FWD_CARRIERS[key][0], got))
        elif key in _BWD_PAIR_CARRIERS:
            nks = [nk for nk in _TENSORS if nk[0] in _BWD_PAIR_CARRIERS[key]]
            self.paired.update(zip(nks, got))
            if len(self.paired) == len(_TENSORS):
                self.pair_sums()
        else:
            nks = [nk for nk in _TENSORS if nk[0] in _BWD_CARRIERS[key]]
            self.shard_sums(self.pending[0], nks, got)

    def pair_sums(self):
        l, grads = self.raw
        self.pending = (l, {nk: _pair_sum(grads[nk], self.paired[nk], 1 - _BIG_SHARD_AXIS[nk[0]], self.cidx, "pair_sum")
                            for nk in _TENSORS})
        self.raw, self.paired = None, {}

    def shard_sums(self, l, nks, got):
        for nk, r in zip(nks, got):
            pf = self.pending[1][nk][0]
            sa = _BIG_SHARD_AXIS[nk[0]]
            if nk not in self.accs:
                shp = list(pf.shape)
                shp[sa] //= N_CHIPS
                shp[1 - sa] *= 2
                self.accs[nk] = lax.empty((self.depth,) + tuple(shp), F32)
            self.accs[nk] = _shard_sum(pf, r, self.accs[nk], l, sa, self.jcidx, "shard_sum")

    def on_big(self, l, big):
        grads = {}
        for n in _BIG:
            gs = big[n] if isinstance(big[n], list) else [big[n]]
            grads.update({(n, k): g for k, g in enumerate(gs)})
        self.raw = (l, grads)
        if l == 0:
            half_axes = [1 - _BIG_SHARD_AXIS[nk[0]] for nk in _TENSORS]
            self.paired = dict(zip(_TENSORS, _run_side(_PairSide([grads[nk] for nk in _TENSORS], half_axes), "pair_exchange")))
            self.pair_sums()
            side = _ScatterSide([self.pending[1][nk][1] for nk in _TENSORS], [_BIG_SHARD_AXIS[nk[0]] for nk in _TENSORS])
            self.shard_sums(0, _TENSORS, _run_side(side, "chip_scatter"))
            self.pending = None


class _NoComm:
    def __init__(self, on_big):
        self.on_big = on_big

    def side(self, phase, l, key, W):
        return None


def _local_step(xs, tgt, W, sm, comm):
    W = dict(W)
    sm = dict(sm)
    depth = sm["norm_mix_g"].shape[0]
    lru_o, lru_vjp = jax.vjp(jax.vmap(_lru_prep), sm["lru_wa"], sm["lru_wx"], sm["lru_ba"], sm["lru_bx"])
    attn_o, attn_vjp = jax.vjp(jax.vmap(_attn_bias), sm["attn_rel_bias"])
    s5_names = ("ssm_a_re", "ssm_a_im", "ssm_b_re", "ssm_b_im", "ssm_c_re", "ssm_c_im", "ssm_d", "ssm_log_step")
    s5_o, s5_vjp = jax.vjp(jax.vmap(_s5_prep), *[sm[n] for n in s5_names])
    wax_all = lru_o[0].astype(BF16)
    sm["lru_prep"] = [(wax_all[l], lru_o[1][l]) for l in range(depth)]
    sm["attn_bias"] = [attn_o[l] for l in range(depth)]
    kt_in = [t.reshape((depth * SSM_G,) + t.shape[2:]) for t in s5_o[:4]]
    tgt_all = _s5_tgt(*kt_in, "s5_tgt").reshape(depth, SSM_G, S5_LW, S5_LW)
    sm["s5_prep"] = [(tgt_all[l],) + tuple(t[l] for t in s5_o[4:]) for l in range(depth)]

    saved = []
    for l in range(depth):
        xs, sv = _layer_fwd(l, xs, W, sm, comm)
        saved.append(sv)
    loss_part, dx, dgf = _final_loss(xs, sm["norm_final_g"][None, :], tgt, "final_loss")

    direct = ("norm_mix_g", "gate_bias", "lru_conv_w", "lru_conv_b", "lru_lambda", "norm_ffn_g")
    per_layer = [None] * depth
    for l in reversed(range(depth)):
        dx, big, per_layer[l] = _layer_bwd(l, dx, saved[l], W, sm, comm)
        comm.on_big(l, big)
    stacked = lambda key, i: jnp.stack([per_layer[l][key][i] for l in range(depth)])
    small_tree = {n: jnp.stack([per_layer[l][n] for l in range(depth)]) for n in direct}
    d_wa, d_wx, d_ba, d_bx = lru_vjp(tuple(stacked("lru_tables", i) for i in range(2)))
    (d_rel,) = attn_vjp(jnp.stack([per_layer[l]["attn_bias"] for l in range(depth)]))
    d_kt_in = _s5_tgt_t(stacked("s5_tables", 0).reshape(depth * SSM_G, S5_LW, S5_LW), *kt_in, "s5_tgt_t")
    d_kt_in = [t.reshape((depth, SSM_G) + t.shape[1:]) for t in d_kt_in]
    d_s5 = s5_vjp(tuple(d_kt_in) + tuple(stacked("s5_tables", i) for i in range(1, 6)))
    small_tree.update(lru_wa=d_wa, lru_wx=d_wx, lru_ba=d_ba, lru_bx=d_bx, attn_rel_bias=d_rel, norm_final_g=dgf[0])
    small_tree.update(zip(s5_names, d_s5))
    return loss_part, dx, small_tree


def _pack_small(tree, names):
    flat = jnp.concatenate([tree[n].reshape(-1) for n in names])
    per = -(-flat.shape[0] // (_N_DEV * 128 * 8)) * (128 * 8)
    flat = jnp.pad(flat, (0, _N_DEV * per - flat.shape[0]))
    return flat.reshape(_N_DEV, per // 128, 128)


def _unpack_small(flat, like, names):
    flat = flat.reshape(-1)
    out, off = {}, 0
    for n in names:
        size = math.prod(like[n].shape)
        out[n] = flat[off:off + size].reshape(like[n].shape)
        off += size
    return out


def kernel(x, norm_mix_g, w_in, gate_bias, lru_conv_w, lru_conv_b, lru_wa, lru_ba, lru_wx, lru_bx, lru_lambda, attn_rel_bias, ssm_a_re, ssm_a_im, ssm_b_re, ssm_b_im, ssm_c_re, ssm_c_im, ssm_d, ssm_log_step, ssm_w_glu, w_branch, w_out, norm_ffn_g, w_ffn_gate, w_ffn_up, w_ffn_down, norm_final_g, loss_target, m_norm_mix_g, m_w_in, m_gate_bias, m_lru_conv_w, m_lru_conv_b, m_lru_wa, m_lru_ba, m_lru_wx, m_lru_bx, m_lru_lambda, m_attn_rel_bias, m_ssm_a_re, m_ssm_a_im, m_ssm_b_re, m_ssm_b_im, m_ssm_c_re, m_ssm_c_im, m_ssm_d, m_ssm_log_step, m_ssm_w_glu, m_w_branch, m_w_out, m_norm_ffn_g, m_w_ffn_gate, m_w_ffn_up, m_w_ffn_down, m_norm_final_g, v_norm_mix_g, v_w_in, v_gate_bias, v_lru_conv_w, v_lru_conv_b, v_lru_wa, v_lru_ba, v_lru_wx, v_lru_bx, v_lru_lambda, v_attn_rel_bias, v_ssm_a_re, v_ssm_a_im, v_ssm_b_re, v_ssm_b_im, v_ssm_c_re, v_ssm_c_im, v_ssm_d, v_ssm_log_step, v_ssm_w_glu, v_w_branch, v_w_out, v_norm_ffn_g, v_w_ffn_gate, v_w_ffn_up, v_w_ffn_down, v_norm_final_g):
    args = dict(locals())
    w = {n: args[n] for n in _WEIGHTS}
    m = {n: args["m_" + n] for n in _WEIGHTS}
    v = {n: args["v_" + n] for n in _WEIGHTS}
    depth = w_in.shape[0]
    xc, yc, cc = _my_place()
    jchip = 2 * xc + yc
    cidx = jnp.reshape(cc, (1,)).astype(jnp.int32)
    jidx = jnp.reshape(jchip, (1,)).astype(jnp.int32)
    jcidx = jnp.stack([jchip, cc]).astype(jnp.int32)

    blocks = [w[n] for n in _BIG]
    blocks[2] = blocks[2].reshape(depth * N_BRANCH, MIX_W, -1)
    axes = [_BIG_SHARD_AXIS[n] + 1 for n in _BIG] + [_SMALL_SHARDED[n] for n in _SMALL_SHARDED]
    placed = [_place_shard(b, ax, jidx, BF16, "place_shard") for b, ax in zip(blocks, axes)]
    placed += [_place_shard(w[n], _SMALL_SHARDED[n], jidx, F32, "place_shard") for n in _SMALL_SHARDED]
    W = dict(zip(_BIG, placed))
    first = [W["w_in"]] + placed[len(_BIG):]
    regions = [(0, 1, "other")] + [(0, depth, "lead")] * len(_SMALL_SHARDED)
    gathered = _run_side(_GatherSide(first, [axes[0]] + axes[len(_BIG):], regions), "gather_weights")
    W["w_in"] = gathered[0]
    sm_full = dict(zip(_SMALL_SHARDED, gathered[1:]))
    sm = {n: w[n] for n in _SMALL if n not in _SMALL_SHARDED}
    sm.update(sm_full)

    comm = _StepComm(depth, cidx, jcidx)
    loss_part, dx, small_tree = _local_step(x[0], loss_target[0], W, sm, comm)
    loss = lax.psum(loss_part, ("x", "y", "c"))
    grad_x = dx[None]

    joined = _pair_join([comm.accs[nk] for nk in _TENSORS], [1 - _BIG_SHARD_AXIS[nk[0]] for nk in _TENSORS], "pair_join")
    jd = dict(zip(_TENSORS, joined))
    grads = {}
    for n in _BIG:
        if n == "w_branch":
            grads[n] = jnp.stack([jd[(n, k)] for k in range(N_BRANCH)], axis=1)
        else:
            grads[n] = jd[(n, 0)]
    like = {n: (sm_full[n] if n in _SMALL_SHARDED else w[n]) for n in _SMALL}
    red = _unpack_small(_allreduce_small(_pack_small(small_tree, _SMALL), "allreduce_small"), like, _SMALL)
    for n in _SMALL:
        if n in _SMALL_SHARDED:
            size = w[n].shape[2]
            grads[n] = lax.dynamic_slice_in_dim(red[n], (2 * xc + yc) * size, size, axis=2)
        else:
            grads[n] = red[n]

    delta, new_m, new_v = {}, {}, {}
    for n in _BIG:
        shp = w[n].shape
        two = lambda a: a.reshape(-1, shp[-1])
        d_, m_, v_ = _adamw(two(w[n]), two(grads[n]), two(m[n]), two(v[n]), "adamw")
        delta[n], new_m[n], new_v[n] = d_.reshape(shp), m_.reshape(shp), v_.reshape(shp)
    pk = lambda tree: _pack_small(tree, _SMALL).reshape(-1, 128)
    d_, m_, v_ = _adamw(pk(w), pk(grads), pk(m), pk(v), "adamw_small")
    like_local = {n: w[n] for n in _SMALL}
    for tree, flat in ((delta, d_), (new_m, m_), (new_v, v_)):
        tree.update(_unpack_small(flat, like_local, _SMALL))
    return (loss, grad_x, *[grads[n] for n in _WEIGHTS], *[delta[n] for n in _WEIGHTS], *[new_m[n] for n in _WEIGHTS],
            *[new_v[n] for n in _WEIGHTS])
```
